```python
import math
import jax, jax.numpy as jnp
from jax import lax
import numpy as np

D_MODEL = 1024
BATCH = 8
SEQ = 4096
DEPTH = 2

SSM_HEADS = 16
SSM_HEAD_DIM = 64
SSM_D_INNER = SSM_HEADS * SSM_HEAD_DIM
SSM_GROUPS = 2
SSM_STATE = 128
SSM_CONV_DIM = SSM_D_INNER + 2 * SSM_GROUPS * SSM_STATE
GDN_HEADS = 8
GDN_HEAD_K = 128
GDN_HEAD_V = 128
GDN_K_DIM = GDN_HEADS * GDN_HEAD_K
GDN_V_DIM = GDN_HEADS * GDN_HEAD_V
GDN_QKV_DIM = 2 * GDN_K_DIM + GDN_V_DIM
CONV_K = 4
CHUNK = 64
FFN_HIDDEN = -(-8 * D_MODEL // (3 * 256)) * 256
EPS = 1e-6
IN_SPLIT_SIZES = (SSM_D_INNER, SSM_CONV_DIM, SSM_HEADS,
                  GDN_QKV_DIM, GDN_V_DIM, GDN_HEADS, GDN_HEADS,
                  D_MODEL, D_MODEL)
IN_DIM = sum(IN_SPLIT_SIZES)
IN_SPLIT_IDX = tuple(int(i) for i in np.cumsum(IN_SPLIT_SIZES)[:-1])

kernel_name = "hybrid_ssd_gdn_gated_merge_block"


def rmsnorm(x, w):
    xf = x.astype(jnp.float32)
    y = xf * lax.rsqrt(jnp.mean(xf * xf, axis=-1, keepdims=True) + EPS)
    return (y * w.astype(jnp.float32)).astype(x.dtype)


def causal_dwconv(u, w):
    c = u.shape[-1]
    return lax.conv_general_dilated(
        u, w[:, None, :].astype(u.dtype), window_strides=(1,),
        padding=[(CONV_K - 1, 0)], dimension_numbers=("NWC", "WIO", "NWC"),
        feature_group_count=c)


def to_chunks(t):
    b, s = t.shape[:2]
    return jnp.moveaxis(t.reshape(b, s // CHUNK, CHUNK, *t.shape[2:]), 1, 0)


def from_chunks(t):
    t = jnp.moveaxis(t, 0, 1)
    return t.reshape(t.shape[0], t.shape[1] * t.shape[2], *t.shape[3:])


def ssd_scan(xdt, a, bm, cm):
    bsz, s = xdt.shape[:2]
    r = SSM_HEADS // SSM_GROUPS
    xdt = xdt.reshape(bsz, s, SSM_GROUPS, r, SSM_HEAD_DIM)
    a = a.reshape(bsz, s, SSM_GROUPS, r)
    causal = jnp.tril(jnp.ones((CHUNK, CHUNK), dtype=bool))[None, :, :, None, None]

    def body(state, inp):
        xc, ac, bc, cc = inp
        acum = jnp.cumsum(ac, axis=1)
        seg = acum[:, :, None] - acum[:, None, :]
        lmat = jnp.exp(jnp.where(causal, seg, -jnp.inf))
        cb = jnp.einsum("bign,bjgn->bijg", cc, bc)
        y_diag = jnp.einsum("bijg,bijgr,bjgrp->bigrp", cb, lmat, xc)
        y_off = jnp.einsum("bign,bgrpn->bigrp", cc, state) * jnp.exp(acum)[..., None]
        decay_end = jnp.exp(acum[:, -1:] - acum)
        new_state = state * jnp.exp(acum[:, -1])[..., None, None] + jnp.einsum(
            "bjgn,bjgr,bjgrp->bgrpn", bc, decay_end, xc)
        return new_state, y_diag + y_off

    state0 = jnp.zeros((bsz, SSM_GROUPS, r, SSM_HEAD_DIM, SSM_STATE), jnp.float32)
    _, y = lax.scan(body, state0, (to_chunks(xdt), to_chunks(a), to_chunks(bm), to_chunks(cm)))
    return from_chunks(y).reshape(bsz, s, SSM_HEADS, SSM_HEAD_DIM)


def mamba2_mixer(z, xbc, dt_raw, conv_w, conv_b, dt_bias, a_log, d_skip, norm_w):
    dtype = z.dtype
    bsz, s = z.shape[:2]
    xbc = jax.nn.silu(causal_dwconv(xbc, conv_w) + conv_b.astype(xbc.dtype))
    xs, bm, cm = jnp.split(xbc.astype(jnp.float32),
                           [SSM_D_INNER, SSM_D_INNER + SSM_GROUPS * SSM_STATE], axis=-1)
    xs = xs.reshape(bsz, s, SSM_HEADS, SSM_HEAD_DIM)
    bm = bm.reshape(bsz, s, SSM_GROUPS, SSM_STATE)
    cm = cm.reshape(bsz, s, SSM_GROUPS, SSM_STATE)
    dt = jax.nn.softplus(dt_raw.astype(jnp.float32) + dt_bias.astype(jnp.float32))
    a = -jnp.exp(a_log.astype(jnp.float32))
    y = ssd_scan(xs * dt[..., None], dt * a, bm, cm)
    y = y + d_skip.astype(jnp.float32)[:, None] * xs
    y = y.reshape(bsz, s, SSM_D_INNER) * jax.nn.silu(z.astype(jnp.float32))
    y = y.reshape(bsz, s, SSM_GROUPS, SSM_D_INNER // SSM_GROUPS)
    y = y * lax.rsqrt(jnp.mean(y * y, axis=-1, keepdims=True) + EPS)
    y = y.reshape(bsz, s, SSM_D_INNER) * norm_w.astype(jnp.float32)
    return y.astype(dtype)


def gdn_scan(q, k, v, g, beta):
    bsz = q.shape[0]
    tril = jnp.tril(jnp.ones((CHUNK, CHUNK), dtype=bool))
    strict = jnp.tril(jnp.ones((CHUNK, CHUNK), dtype=bool), -1)
    eye = jnp.eye(CHUNK, dtype=jnp.float32)

    def body(state, inp):
        qc, kc, vc, gcv, bc = inp
        gc = jnp.cumsum(gcv, axis=1)
        gch = jnp.swapaxes(gc, 1, 2)
        decay = jnp.exp(jnp.where(tril, gch[..., :, None] - gch[..., None, :], -jnp.inf))
        kk = jnp.einsum("bihd,bjhd->bhij", kc, kc)
        amat = jnp.where(strict, kk * decay * jnp.swapaxes(bc, 1, 2)[..., :, None], 0.0)
        rhs = jnp.concatenate([vc * bc[..., None], kc * (bc * jnp.exp(gc))[..., None]], axis=-1)
        rhs = jnp.swapaxes(rhs, 1, 2)
        t = lax.linalg.triangular_solve(eye + amat, rhs, left_side=True, lower=True,
                                        unit_diagonal=True)
        u, w = t[..., :GDN_HEAD_V], t[..., GDN_HEAD_V:]
        v_new = u - jnp.einsum("bhqk,bhkv->bhqv", w, state)
        qk = jnp.einsum("bihd,bjhd->bhij", qc, kc) * decay
        o = (jnp.einsum("bihk,bhkv->bhiv", qc * jnp.exp(gc)[..., None], state)
             + jnp.einsum("bhij,bhjv->bhiv", qk, v_new))
        new_state = state * jnp.exp(gch[..., -1])[..., None, None] + jnp.einsum(
            "bjhk,bhjv->bhkv", kc * jnp.exp(gc[:, -1:] - gc)[..., None], v_new)
        return new_state, jnp.swapaxes(o, 1, 2)

    state0 = jnp.zeros((bsz, GDN_HEADS, GDN_HEAD_K, GDN_HEAD_V), jnp.float32)
    _, o = lax.scan(body, state0, (to_chunks(q), to_chunks(k), to_chunks(v),
                                   to_chunks(g), to_chunks(beta)))
    return from_chunks(o)


def gdn_mixer(qkv, z, a_raw, b_raw, conv_w, a_log, dt_bias, norm_w):
    dtype = z.dtype
    bsz, s = z.shape[:2]
    qkv = jax.nn.silu(causal_dwconv(qkv, conv_w)).astype(jnp.float32)
    q, k, v = jnp.split(qkv, [GDN_K_DIM, 2 * GDN_K_DIM], axis=-1)
    q = q.reshape(bsz, s, GDN_HEADS, GDN_HEAD_K)
    k = k.reshape(bsz, s, GDN_HEADS, GDN_HEAD_K)
    v = v.reshape(bsz, s, GDN_HEADS, GDN_HEAD_V)
    q = q * lax.rsqrt(jnp.sum(q * q, axis=-1, keepdims=True) + EPS) * (GDN_HEAD_K ** -0.5)
    k = k * lax.rsqrt(jnp.sum(k * k, axis=-1, keepdims=True) + EPS)
    beta = jax.nn.sigmoid(b_raw.astype(jnp.float32))
    g = -jnp.exp(a_log.astype(jnp.float32)) * jax.nn.softplus(
        a_raw.astype(jnp.float32) + dt_bias.astype(jnp.float32))
    o = gdn_scan(q, k, v, g, beta)
    o = o * lax.rsqrt(jnp.mean(o * o, axis=-1, keepdims=True) + EPS) * norm_w.astype(jnp.float32)
    o = o * jax.nn.silu(z.astype(jnp.float32).reshape(bsz, s, GDN_HEADS, GDN_HEAD_V))
    return o.reshape(bsz, s, GDN_V_DIM).astype(dtype)


def _fwd_setup_inputs(seed: int = 0) -> dict:
    key = jax.random.key(seed)
    ks = jax.random.split(key, 24)
    f32 = jnp.float32

    def nrm(k, shape, scale):
        return jax.random.normal(k, shape, f32) * scale

    def gain(k, shape):
        return 1.0 + 0.02 * jax.random.normal(k, shape, f32)

    def dt_bias_init(k, shape):
        dt = jnp.exp(jax.random.uniform(k, shape, f32, math.log(1e-3), math.log(1e-1)))
        return dt + jnp.log(-jnp.expm1(-dt))

    return {
        "x": jax.random.normal(ks[0], (BATCH, SEQ, D_MODEL), f32),
        "norm_mix_w": gain(ks[1], (DEPTH, D_MODEL)),
        "w_in": nrm(ks[2], (DEPTH, D_MODEL, IN_DIM), D_MODEL ** -0.5),
        "ssm_conv_w": nrm(ks[3], (DEPTH, CONV_K, SSM_CONV_DIM), CONV_K ** -0.5),
        "ssm_conv_b": nrm(ks[4], (DEPTH, SSM_CONV_DIM), 0.02),
        "ssm_dt_bias": dt_bias_init(ks[5], (DEPTH, SSM_HEADS)),
        "ssm_a_log": jnp.log(jax.random.uniform(ks[6], (DEPTH, SSM_HEADS), f32, 1.0, 16.0)),
        "ssm_d": gain(ks[7], (DEPTH, SSM_HEADS)),
        "ssm_norm_w": gain(ks[8], (DEPTH, SSM_D_INNER)),
        "gdn_conv_w": nrm(ks[9], (DEPTH, CONV_K, GDN_QKV_DIM), CONV_K ** -0.5),
        "gdn_a_log": jnp.log(jax.random.uniform(ks[10], (DEPTH, GDN_HEADS), f32, 1.0, 16.0)),
        "gdn_dt_bias": dt_bias_init(ks[11], (DEPTH, GDN_HEADS)),
        "gdn_norm_w": gain(ks[12], (DEPTH, GDN_HEAD_V)),
        "w_proj_ssm": nrm(ks[13], (DEPTH, SSM_D_INNER, D_MODEL), SSM_D_INNER ** -0.5),
        "w_proj_gdn": nrm(ks[14], (DEPTH, GDN_V_DIM, D_MODEL), GDN_V_DIM ** -0.5),
        "w_out": nrm(ks[15], (DEPTH, D_MODEL, D_MODEL), D_MODEL ** -0.5),
        "norm_ffn_w": gain(ks[16], (DEPTH, D_MODEL)),
        "w_ffn_in": nrm(ks[17], (DEPTH, D_MODEL, 2 * FFN_HIDDEN), D_MODEL ** -0.5),
        "w_ffn_down": nrm(ks[18], (DEPTH, FFN_HIDDEN, D_MODEL), FFN_HIDDEN ** -0.5),
        "final_norm_w": gain(ks[19], (D_MODEL,)),
    }


def _fwd_reference(x, norm_mix_w, w_in, ssm_conv_w, ssm_conv_b, ssm_dt_bias, ssm_a_log, ssm_d,
              ssm_norm_w, gdn_conv_w, gdn_a_log, gdn_dt_bias, gdn_norm_w, w_proj_ssm,
              w_proj_gdn, w_out, norm_ffn_w, w_ffn_in, w_ffn_down, final_norm_w):
    for l in range(DEPTH):
        h = rmsnorm(x, norm_mix_w[l])
        proj = h @ w_in[l]
        (ssm_z, ssm_xbc, ssm_dt, gdn_qkv, gdn_z, gdn_a, gdn_b,
         gate_ssm, gate_gdn) = jnp.split(proj, IN_SPLIT_IDX, axis=-1)
        y_ssm = mamba2_mixer(ssm_z, ssm_xbc, ssm_dt, ssm_conv_w[l], ssm_conv_b[l],
                             ssm_dt_bias[l], ssm_a_log[l], ssm_d[l], ssm_norm_w[l])
        y_gdn = gdn_mixer(gdn_qkv, gdn_z, gdn_a, gdn_b, gdn_conv_w[l], gdn_a_log[l],
                          gdn_dt_bias[l], gdn_norm_w[l])
        merged = (jax.nn.sigmoid(gate_ssm) * (y_ssm @ w_proj_ssm[l])
                  + jax.nn.sigmoid(gate_gdn) * (y_gdn @ w_proj_gdn[l]))
        x = x + merged @ w_out[l]
        h = rmsnorm(x, norm_ffn_w[l])
        gate, up = jnp.split(h @ w_ffn_in[l], [FFN_HIDDEN], axis=-1)
        x = x + (jax.nn.silu(gate) * up) @ w_ffn_down[l]
    return rmsnorm(x, final_norm_w)


import jax as _jax
import jax.numpy as _jnp

TWIN_FORMAT = 'train_step'
FWD_PARAMS = ['x', 'norm_mix_w', 'w_in', 'ssm_conv_w', 'ssm_conv_b', 'ssm_dt_bias', 'ssm_a_log', 'ssm_d', 'ssm_norm_w', 'gdn_conv_w', 'gdn_a_log', 'gdn_dt_bias', 'gdn_norm_w', 'w_proj_ssm', 'w_proj_gdn', 'w_out', 'norm_ffn_w', 'w_ffn_in', 'w_ffn_down', 'final_norm_w']
TWIN_WEIGHTS = ['norm_mix_w', 'w_in', 'ssm_conv_w', 'ssm_conv_b', 'ssm_dt_bias', 'ssm_a_log', 'ssm_d', 'ssm_norm_w', 'gdn_conv_w', 'gdn_a_log', 'gdn_dt_bias', 'gdn_norm_w', 'w_proj_ssm', 'w_proj_gdn', 'w_out', 'norm_ffn_w', 'w_ffn_in', 'w_ffn_down', 'final_norm_w']
TWIN_DIFF_INPUT = 'x'
TWIN_INPUTS = ['x', 'norm_mix_w', 'w_in', 'ssm_conv_w', 'ssm_conv_b', 'ssm_dt_bias', 'ssm_a_log', 'ssm_d', 'ssm_norm_w', 'gdn_conv_w', 'gdn_a_log', 'gdn_dt_bias', 'gdn_norm_w', 'w_proj_ssm', 'w_proj_gdn', 'w_out', 'norm_ffn_w', 'w_ffn_in', 'w_ffn_down', 'final_norm_w', 'loss_target', 'm_norm_mix_w', 'm_w_in', 'm_ssm_conv_w', 'm_ssm_conv_b', 'm_ssm_dt_bias', 'm_ssm_a_log', 'm_ssm_d', 'm_ssm_norm_w', 'm_gdn_conv_w', 'm_gdn_a_log', 'm_gdn_dt_bias', 'm_gdn_norm_w', 'm_w_proj_ssm', 'm_w_proj_gdn', 'm_w_out', 'm_norm_ffn_w', 'm_w_ffn_in', 'm_w_ffn_down', 'm_final_norm_w', 'v_norm_mix_w', 'v_w_in', 'v_ssm_conv_w', 'v_ssm_conv_b', 'v_ssm_dt_bias', 'v_ssm_a_log', 'v_ssm_d', 'v_ssm_norm_w', 'v_gdn_conv_w', 'v_gdn_a_log', 'v_gdn_dt_bias', 'v_gdn_norm_w', 'v_w_proj_ssm', 'v_w_proj_gdn', 'v_w_out', 'v_norm_ffn_w', 'v_w_ffn_in', 'v_w_ffn_down', 'v_final_norm_w']
TWIN_OUTPUTS = ['loss', 'grad_x', 'grad_norm_mix_w', 'grad_w_in', 'grad_ssm_conv_w', 'grad_ssm_conv_b', 'grad_ssm_dt_bias', 'grad_ssm_a_log', 'grad_ssm_d', 'grad_ssm_norm_w', 'grad_gdn_conv_w', 'grad_gdn_a_log', 'grad_gdn_dt_bias', 'grad_gdn_norm_w', 'grad_w_proj_ssm', 'grad_w_proj_gdn', 'grad_w_out', 'grad_norm_ffn_w', 'grad_w_ffn_in', 'grad_w_ffn_down', 'grad_final_norm_w', 'delta_norm_mix_w', 'delta_w_in', 'delta_ssm_conv_w', 'delta_ssm_conv_b', 'delta_ssm_dt_bias', 'delta_ssm_a_log', 'delta_ssm_d', 'delta_ssm_norm_w', 'delta_gdn_conv_w', 'delta_gdn_a_log', 'delta_gdn_dt_bias', 'delta_gdn_norm_w', 'delta_w_proj_ssm', 'delta_w_proj_gdn', 'delta_w_out', 'delta_norm_ffn_w', 'delta_w_ffn_in', 'delta_w_ffn_down', 'delta_final_norm_w', 'new_m_norm_mix_w', 'new_m_w_in', 'new_m_ssm_conv_w', 'new_m_ssm_conv_b', 'new_m_ssm_dt_bias', 'new_m_ssm_a_log', 'new_m_ssm_d', 'new_m_ssm_norm_w', 'new_m_gdn_conv_w', 'new_m_gdn_a_log', 'new_m_gdn_dt_bias', 'new_m_gdn_norm_w', 'new_m_w_proj_ssm', 'new_m_w_proj_gdn', 'new_m_w_out', 'new_m_norm_ffn_w', 'new_m_w_ffn_in', 'new_m_w_ffn_down', 'new_m_final_norm_w', 'new_v_norm_mix_w', 'new_v_w_in', 'new_v_ssm_conv_w', 'new_v_ssm_conv_b', 'new_v_ssm_dt_bias', 'new_v_ssm_a_log', 'new_v_ssm_d', 'new_v_ssm_norm_w', 'new_v_gdn_conv_w', 'new_v_gdn_a_log', 'new_v_gdn_dt_bias', 'new_v_gdn_norm_w', 'new_v_w_proj_ssm', 'new_v_w_proj_gdn', 'new_v_w_out', 'new_v_norm_ffn_w', 'new_v_w_ffn_in', 'new_v_w_ffn_down', 'new_v_final_norm_w']
TWIN_LEAF_KINDS = {'loss': 'loss', 'grad_x': 'grad_x', 'grad_norm_mix_w': 'grad_w', 'grad_w_in': 'grad_w', 'grad_ssm_conv_w': 'grad_w', 'grad_ssm_conv_b': 'grad_w', 'grad_ssm_dt_bias': 'grad_w', 'grad_ssm_a_log': 'grad_w', 'grad_ssm_d': 'grad_w', 'grad_ssm_norm_w': 'grad_w', 'grad_gdn_conv_w': 'grad_w', 'grad_gdn_a_log': 'grad_w', 'grad_gdn_dt_bias': 'grad_w', 'grad_gdn_norm_w': 'grad_w', 'grad_w_proj_ssm': 'grad_w', 'grad_w_proj_gdn': 'grad_w', 'grad_w_out': 'grad_w', 'grad_norm_ffn_w': 'grad_w', 'grad_w_ffn_in': 'grad_w', 'grad_w_ffn_down': 'grad_w', 'grad_final_norm_w': 'grad_w', 'delta_norm_mix_w': 'delta_w', 'delta_w_in': 'delta_w', 'delta_ssm_conv_w': 'delta_w', 'delta_ssm_conv_b': 'delta_w', 'delta_ssm_dt_bias': 'delta_w', 'delta_ssm_a_log': 'delta_w', 'delta_ssm_d': 'delta_w', 'delta_ssm_norm_w': 'delta_w', 'delta_gdn_conv_w': 'delta_w', 'delta_gdn_a_log': 'delta_w', 'delta_gdn_dt_bias': 'delta_w', 'delta_gdn_norm_w': 'delta_w', 'delta_w_proj_ssm': 'delta_w', 'delta_w_proj_gdn': 'delta_w', 'delta_w_out': 'delta_w', 'delta_norm_ffn_w': 'delta_w', 'delta_w_ffn_in': 'delta_w', 'delta_w_ffn_down': 'delta_w', 'delta_final_norm_w': 'delta_w', 'new_m_norm_mix_w': 'new_m', 'new_m_w_in': 'new_m', 'new_m_ssm_conv_w': 'new_m', 'new_m_ssm_conv_b': 'new_m', 'new_m_ssm_dt_bias': 'new_m', 'new_m_ssm_a_log': 'new_m', 'new_m_ssm_d': 'new_m', 'new_m_ssm_norm_w': 'new_m', 'new_m_gdn_conv_w': 'new_m', 'new_m_gdn_a_log': 'new_m', 'new_m_gdn_dt_bias': 'new_m', 'new_m_gdn_norm_w': 'new_m', 'new_m_w_proj_ssm': 'new_m', 'new_m_w_proj_gdn': 'new_m', 'new_m_w_out': 'new_m', 'new_m_norm_ffn_w': 'new_m', 'new_m_w_ffn_in': 'new_m', 'new_m_w_ffn_down': 'new_m', 'new_m_final_norm_w': 'new_m', 'new_v_norm_mix_w': 'new_v', 'new_v_w_in': 'new_v', 'new_v_ssm_conv_w': 'new_v', 'new_v_ssm_conv_b': 'new_v', 'new_v_ssm_dt_bias': 'new_v', 'new_v_ssm_a_log': 'new_v', 'new_v_ssm_d': 'new_v', 'new_v_ssm_norm_w': 'new_v', 'new_v_gdn_conv_w': 'new_v', 'new_v_gdn_a_log': 'new_v', 'new_v_gdn_dt_bias': 'new_v', 'new_v_gdn_norm_w': 'new_v', 'new_v_w_proj_ssm': 'new_v', 'new_v_w_proj_gdn': 'new_v', 'new_v_w_out': 'new_v', 'new_v_norm_ffn_w': 'new_v', 'new_v_w_ffn_in': 'new_v', 'new_v_w_ffn_down': 'new_v', 'new_v_final_norm_w': 'new_v'}


def _forward(args):
    return _fwd_reference(*[args[k] for k in FWD_PARAMS])


def _output_shape():
    out = _jax.eval_shape(lambda: _forward(_fwd_setup_inputs(0)))
    return out.shape, out.dtype

N_MICROBATCH = 1
ADAM_LR = 0.001
ADAM_B1 = 0.9
ADAM_B2 = 0.999
ADAM_EPS = 1e-08
ADAM_WD = 0.01
ADAM_STEP = 10
PER_EXAMPLE_BATCH_AXIS = {'x': 0, 'loss_target': 0}
SHARED_INPUTS = []
_WEIGHT_DTYPES = {'norm_mix_w': _jnp.float32, 'w_in': _jnp.float32, 'ssm_conv_w': _jnp.float32, 'ssm_conv_b': _jnp.float32, 'ssm_dt_bias': _jnp.float32, 'ssm_a_log': _jnp.float32, 'ssm_d': _jnp.float32, 'ssm_norm_w': _jnp.float32, 'gdn_conv_w': _jnp.float32, 'gdn_a_log': _jnp.float32, 'gdn_dt_bias': _jnp.float32, 'gdn_norm_w': _jnp.float32, 'w_proj_ssm': _jnp.float32, 'w_proj_gdn': _jnp.float32, 'w_out': _jnp.float32, 'norm_ffn_w': _jnp.float32, 'w_ffn_in': _jnp.float32, 'w_ffn_down': _jnp.float32, 'final_norm_w': _jnp.float32}
MOMENT_SCALE = {'norm_mix_w': 1.811837e-01, 'w_in': 5.854510e-02, 'ssm_conv_w': 8.179253e-02, 'ssm_conv_b': 1.152932e-01, 'ssm_dt_bias': 2.835493e-01, 'ssm_a_log': 2.267538e-01, 'ssm_d': 5.709785e-01, 'ssm_norm_w': 1.022634e-01, 'gdn_conv_w': 4.101596e-02, 'gdn_a_log': 1.994998e-01, 'gdn_dt_bias': 1.888970e-01, 'gdn_norm_w': 1.604546e-01, 'w_proj_ssm': 9.202777e-02, 'w_proj_gdn': 5.343789e-02, 'w_out': 1.064450e-01, 'norm_ffn_w': 1.238527e-01, 'w_ffn_in': 5.123566e-02, 'w_ffn_down': 8.339371e-02, 'final_norm_w': 3.200532e+01}


def _to_microbatches(a, axis):
    t = _jnp.moveaxis(a, axis, 0)
    t = t.reshape((N_MICROBATCH, t.shape[0] // N_MICROBATCH) + t.shape[1:])
    return _jnp.moveaxis(t, 1, axis + 1)


def setup_inputs(seed: int = 0) -> dict:
    inp = _fwd_setup_inputs(seed)
    key = _jax.random.fold_in(_jax.random.key(seed), 7919)
    shape, _ = _output_shape()
    out = dict(inp)
    out["loss_target"] = _jax.random.normal(_jax.random.fold_in(key, 0), shape, _jnp.float32)
    for i, name in enumerate(TWIN_WEIGHTS):
        w = inp[name].astype(_jnp.float32)
        if MOMENT_SCALE is None:
            s = _jnp.sqrt(_jnp.mean(_jnp.square(w)) + 1e-30)
        else:
            s = MOMENT_SCALE[name]
        km, kv = _jax.random.split(_jax.random.fold_in(key, i + 1))
        out[name] = w
        out["m_" + name] = s * _jax.random.normal(km, w.shape, _jnp.float32)
        out["v_" + name] = (s * s) * _jax.random.uniform(kv, w.shape, _jnp.float32, 0.5, 1.5)
    if N_MICROBATCH > 1:
        for name, axis in PER_EXAMPLE_BATCH_AXIS.items():
            out[name] = _to_microbatches(out[name], axis)
    return {'x': out['x'], 'norm_mix_w': out['norm_mix_w'], 'w_in': out['w_in'], 'ssm_conv_w': out['ssm_conv_w'], 'ssm_conv_b': out['ssm_conv_b'], 'ssm_dt_bias': out['ssm_dt_bias'], 'ssm_a_log': out['ssm_a_log'], 'ssm_d': out['ssm_d'], 'ssm_norm_w': out['ssm_norm_w'], 'gdn_conv_w': out['gdn_conv_w'], 'gdn_a_log': out['gdn_a_log'], 'gdn_dt_bias': out['gdn_dt_bias'], 'gdn_norm_w': out['gdn_norm_w'], 'w_proj_ssm': out['w_proj_ssm'], 'w_proj_gdn': out['w_proj_gdn'], 'w_out': out['w_out'], 'norm_ffn_w': out['norm_ffn_w'], 'w_ffn_in': out['w_ffn_in'], 'w_ffn_down': out['w_ffn_down'], 'final_norm_w': out['final_norm_w'], 'loss_target': out['loss_target'], 'm_norm_mix_w': out['m_norm_mix_w'], 'm_w_in': out['m_w_in'], 'm_ssm_conv_w': out['m_ssm_conv_w'], 'm_ssm_conv_b': out['m_ssm_conv_b'], 'm_ssm_dt_bias': out['m_ssm_dt_bias'], 'm_ssm_a_log': out['m_ssm_a_log'], 'm_ssm_d': out['m_ssm_d'], 'm_ssm_norm_w': out['m_ssm_norm_w'], 'm_gdn_conv_w': out['m_gdn_conv_w'], 'm_gdn_a_log': out['m_gdn_a_log'], 'm_gdn_dt_bias': out['m_gdn_dt_bias'], 'm_gdn_norm_w': out['m_gdn_norm_w'], 'm_w_proj_ssm': out['m_w_proj_ssm'], 'm_w_proj_gdn': out['m_w_proj_gdn'], 'm_w_out': out['m_w_out'], 'm_norm_ffn_w': out['m_norm_ffn_w'], 'm_w_ffn_in': out['m_w_ffn_in'], 'm_w_ffn_down': out['m_w_ffn_down'], 'm_final_norm_w': out['m_final_norm_w'], 'v_norm_mix_w': out['v_norm_mix_w'], 'v_w_in': out['v_w_in'], 'v_ssm_conv_w': out['v_ssm_conv_w'], 'v_ssm_conv_b': out['v_ssm_conv_b'], 'v_ssm_dt_bias': out['v_ssm_dt_bias'], 'v_ssm_a_log': out['v_ssm_a_log'], 'v_ssm_d': out['v_ssm_d'], 'v_ssm_norm_w': out['v_ssm_norm_w'], 'v_gdn_conv_w': out['v_gdn_conv_w'], 'v_gdn_a_log': out['v_gdn_a_log'], 'v_gdn_dt_bias': out['v_gdn_dt_bias'], 'v_gdn_norm_w': out['v_gdn_norm_w'], 'v_w_proj_ssm': out['v_w_proj_ssm'], 'v_w_proj_gdn': out['v_w_proj_gdn'], 'v_w_out': out['v_w_out'], 'v_norm_ffn_w': out['v_norm_ffn_w'], 'v_w_ffn_in': out['v_w_ffn_in'], 'v_w_ffn_down': out['v_w_ffn_down'], 'v_final_norm_w': out['v_final_norm_w']}


def _loss(weights, diff, rest, loss_target):
    with _jax.named_scope("forward"):
        args = {**rest, TWIN_DIFF_INPUT: diff, **{k: w.astype(_WEIGHT_DTYPES[k]) for k, w in weights.items()}}
        y = _forward(args)
    with _jax.named_scope("loss_head"):
        err = _jnp.square(y.astype(_jnp.float32) - loss_target)
        return 0.5 * _jnp.sum(_jnp.mean(err, axis=-1)) if err.ndim else 0.5 * err


def _adamw(w, g, m, v):
    m = ADAM_B1 * m + (1.0 - ADAM_B1) * g
    v = ADAM_B2 * v + (1.0 - ADAM_B2) * _jnp.square(g)
    m_hat = m / (1.0 - ADAM_B1 ** ADAM_STEP)
    v_hat = v / (1.0 - ADAM_B2 ** ADAM_STEP)
    delta = -ADAM_LR * (m_hat / (_jnp.sqrt(v_hat) + ADAM_EPS) + ADAM_WD * w)
    return delta, m, v


def reference(x, norm_mix_w, w_in, ssm_conv_w, ssm_conv_b, ssm_dt_bias, ssm_a_log, ssm_d, ssm_norm_w, gdn_conv_w, gdn_a_log, gdn_dt_bias, gdn_norm_w, w_proj_ssm, w_proj_gdn, w_out, norm_ffn_w, w_ffn_in, w_ffn_down, final_norm_w, loss_target, m_norm_mix_w, m_w_in, m_ssm_conv_w, m_ssm_conv_b, m_ssm_dt_bias, m_ssm_a_log, m_ssm_d, m_ssm_norm_w, m_gdn_conv_w, m_gdn_a_log, m_gdn_dt_bias, m_gdn_norm_w, m_w_proj_ssm, m_w_proj_gdn, m_w_out, m_norm_ffn_w, m_w_ffn_in, m_w_ffn_down, m_final_norm_w, v_norm_mix_w, v_w_in, v_ssm_conv_w, v_ssm_conv_b, v_ssm_dt_bias, v_ssm_a_log, v_ssm_d, v_ssm_norm_w, v_gdn_conv_w, v_gdn_a_log, v_gdn_dt_bias, v_gdn_norm_w, v_w_proj_ssm, v_w_proj_gdn, v_w_out, v_norm_ffn_w, v_w_ffn_in, v_w_ffn_down, v_final_norm_w):
    given = dict(x=x, norm_mix_w=norm_mix_w, w_in=w_in, ssm_conv_w=ssm_conv_w, ssm_conv_b=ssm_conv_b, ssm_dt_bias=ssm_dt_bias, ssm_a_log=ssm_a_log, ssm_d=ssm_d, ssm_norm_w=ssm_norm_w, gdn_conv_w=gdn_conv_w, gdn_a_log=gdn_a_log, gdn_dt_bias=gdn_dt_bias, gdn_norm_w=gdn_norm_w, w_proj_ssm=w_proj_ssm, w_proj_gdn=w_proj_gdn, w_out=w_out, norm_ffn_w=norm_ffn_w, w_ffn_in=w_ffn_in, w_ffn_down=w_ffn_down, final_norm_w=final_norm_w, loss_target=loss_target, m_norm_mix_w=m_norm_mix_w, m_w_in=m_w_in, m_ssm_conv_w=m_ssm_conv_w, m_ssm_conv_b=m_ssm_conv_b, m_ssm_dt_bias=m_ssm_dt_bias, m_ssm_a_log=m_ssm_a_log, m_ssm_d=m_ssm_d, m_ssm_norm_w=m_ssm_norm_w, m_gdn_conv_w=m_gdn_conv_w, m_gdn_a_log=m_gdn_a_log, m_gdn_dt_bias=m_gdn_dt_bias, m_gdn_norm_w=m_gdn_norm_w, m_w_proj_ssm=m_w_proj_ssm, m_w_proj_gdn=m_w_proj_gdn, m_w_out=m_w_out, m_norm_ffn_w=m_norm_ffn_w, m_w_ffn_in=m_w_ffn_in, m_w_ffn_down=m_w_ffn_down, m_final_norm_w=m_final_norm_w, v_norm_mix_w=v_norm_mix_w, v_w_in=v_w_in, v_ssm_conv_w=v_ssm_conv_w, v_ssm_conv_b=v_ssm_conv_b, v_ssm_dt_bias=v_ssm_dt_bias, v_ssm_a_log=v_ssm_a_log, v_ssm_d=v_ssm_d, v_ssm_norm_w=v_ssm_norm_w, v_gdn_conv_w=v_gdn_conv_w, v_gdn_a_log=v_gdn_a_log, v_gdn_dt_bias=v_gdn_dt_bias, v_gdn_norm_w=v_gdn_norm_w, v_w_proj_ssm=v_w_proj_ssm, v_w_proj_gdn=v_w_proj_gdn, v_w_out=v_w_out, v_norm_ffn_w=v_norm_ffn_w, v_w_ffn_in=v_w_ffn_in, v_w_ffn_down=v_w_ffn_down, v_final_norm_w=v_final_norm_w)
    weights = {n: given[n] for n in TWIN_WEIGHTS}
    shared = {n: given[n] for n in SHARED_INPUTS}
    per_example = {n: given[n] for n in ['x']}
    grad_fn = _jax.value_and_grad(_loss, argnums=(0, 1))

    def one_microbatch(ex, loss_target):
        ex = dict(ex)
        diff = ex.pop(TWIN_DIFF_INPUT)
        return grad_fn(weights, diff, {**shared, **ex}, loss_target)

    if N_MICROBATCH == 1:
        loss, (grad_w, grad_x) = one_microbatch(per_example, given["loss_target"])
    else:
        def body(carry, xs):
            loss_sum, grad_sum = carry
            l_k, (gw_k, gx_k) = one_microbatch(xs[0], xs[1])
            with _jax.named_scope("update"):
                return (loss_sum + l_k, _jax.tree.map(_jnp.add, grad_sum, gw_k)), gx_k

        init = (_jnp.zeros((), _jnp.float32), _jax.tree.map(_jnp.zeros_like, weights))
        (loss, grad_w), grad_x = _jax.lax.scan(body, init, (per_example, given["loss_target"]))
    with _jax.named_scope("update"):
        delta_w, new_m, new_v = {}, {}, {}
        for n in TWIN_WEIGHTS:
            delta_w[n], new_m[n], new_v[n] = _adamw(weights[n], grad_w[n], given["m_" + n], given["v_" + n])
    return (loss, grad_x, *[grad_w[n] for n in TWIN_WEIGHTS], *[delta_w[n] for n in TWIN_WEIGHTS],
            *[new_m[n] for n in TWIN_WEIGHTS], *[new_v[n] for n in TWIN_WEIGHTS])
```

```python
import functools

import jax
import jax.numpy as jnp
from jax import lax
from jax.experimental import pallas as pl
from jax.experimental.pallas import tpu as pltpu

F32 = jnp.float32
BF16 = jnp.bfloat16
HI = lax.Precision.HIGHEST
SDS = jax.ShapeDtypeStruct

D_MODEL = 1024
DEPTH = 2
SSM_HEADS = 16
SSM_P = 64
SSM_N = 128
SSM_GROUPS = 2
SSM_CONV = 1536
GDN_HEADS = 8
GDN_DK = 128
GDN_QKV = 3072
CONV_K = 4
CHUNK = 64
FFN = 2816
IN_DIM = 8736
EPS = 1e-6
N_DEV = 8

Z_OFF = 0
GZ_OFF = 1024
G1_OFF = 2048
G2_OFF = 3072
QKV_OFF = 4096
XBC_OFF = 7168
SM_OFF = 8704
PROJ_W = 8960
LANE_A = 16
LANE_B = 24
O_Z, O_XBC, O_DT, O_QKV, O_GZ, O_A, O_B, O_G1, O_G2 = 0, 1024, 2560, 2576, 5648, 6672, 6680, 6688, 7712

ADAM_LR = 0.001
ADAM_B1 = 0.9
ADAM_B2 = 0.999
ADAM_EPS = 1e-08
ADAM_WD = 0.01
ADAM_STEP = 10

V7X_VMEM_LIMIT = 48 * 1024 * 1024

NN = ((1,), (0,))
NT = ((1,), (1,))
TN = ((0,), (0,))


def _bdot(a, b, dims):
    return lax.dot_general(a.astype(BF16), b.astype(BF16), (dims, ((), ())), preferred_element_type=F32)


def _hdot(a, b, dims=NN):
    return lax.dot_general(a, b, (dims, ((), ())), precision=HI, preferred_element_type=F32)


def _sigmoid(x):
    return 1.0 / (1.0 + jnp.exp(-x))


def _softplus(x):
    return jnp.maximum(x, 0.0) + jnp.log(1.0 + jnp.exp(-jnp.abs(x)))


def _params(dims):
    return pltpu.CompilerParams(dimension_semantics=dims, vmem_limit_bytes=V7X_VMEM_LIMIT)


def _rowsum(x):
    return jnp.sum(x, axis=-1, keepdims=True)


def _colsum(x):
    return jnp.sum(x, axis=0, keepdims=True)


def _matmul(name, mode, pairs, m, n, kdim, tm, tn, tk, out_dtypes, epi=None, extras=()):
    tm, tn, tk = min(tm, m), min(tn, n), min(tk, kdim)
    nk = kdim // tk
    assert m % tm == 0 and n % tn == 0 and kdim % tk == 0, (name, m, n, kdim, tm, tn, tk)
    in_specs, args = [], []
    for a, a_off, b, b_off in pairs:
        if mode == "nn":
            in_specs.append(pl.BlockSpec((tm, tk), lambda i, j, k, o=a_off: (i, k + o)))
            in_specs.append(pl.BlockSpec((tk, tn), lambda i, j, k, o=b_off: (k, j + o)))
            dims = NN
        elif mode == "nt":
            in_specs.append(pl.BlockSpec((tm, tk), lambda i, j, k, o=a_off: (i, k + o)))
            in_specs.append(pl.BlockSpec((tn, tk), lambda i, j, k, o=b_off: (j, k + o)))
            dims = NT
        else:
            in_specs.append(pl.BlockSpec((tk, tm), lambda i, j, k, o=a_off: (k, i + o)))
            in_specs.append(pl.BlockSpec((tk, tn), lambda i, j, k, o=b_off: (k, j + o)))
            dims = TN
        args += [a, b]
    for e, e_off in extras:
        in_specs.append(pl.BlockSpec((tm, tn), lambda i, j, k, o=e_off: (i, j + o)))
        args.append(e)
    npair, nex, nout = len(pairs), len(extras), len(out_dtypes)

    def body(*refs):
        prefs = refs[: 2 * npair]
        erefs = refs[2 * npair: 2 * npair + nex]
        orefs = refs[2 * npair + nex: 2 * npair + nex + nout]
        acc = refs[-1]
        k = pl.program_id(2)

        @pl.when(k == 0)
        def _():
            acc[...] = jnp.zeros_like(acc)

        s = _bdot(prefs[0][...], prefs[1][...], dims)
        for p in range(1, npair):
            s = s + _bdot(prefs[2 * p][...], prefs[2 * p + 1][...], dims)
        acc[...] += s

        @pl.when(k == nk - 1)
        def _():
            res = acc[...]
            outs = (res,) if epi is None else epi(res, *[e[...] for e in erefs])
            for o, r in zip(orefs, outs):
                o[...] = r.astype(o.dtype)

    out_shape = tuple(SDS((m, n), dt) for dt in out_dtypes)
    out_specs = tuple(pl.BlockSpec((tm, tn), lambda i, j, k: (i, j)) for _ in out_dtypes)
    res = pl.pallas_call(
        body, grid=(m // tm, n // tn, nk), in_specs=in_specs, out_specs=out_specs, out_shape=out_shape,
        scratch_shapes=[pltpu.VMEM((tm, tn), F32)], name=name,
        compiler_params=_params(("parallel", "parallel", "arbitrary")),
    )(*args)
    return res if nout > 1 else res[0]


def _rmsnorm_fwd(name, x, w):
    t, d = x.shape
    tm = min(512, t)

    def body(x_ref, w_ref, h_ref):
        xv = x_ref[...]
        r = lax.rsqrt(jnp.mean(xv * xv, axis=-1, keepdims=True) + EPS)
        h_ref[...] = (xv * r * w_ref[...]).astype(BF16)

    return pl.pallas_call(
        body, grid=(t // tm,),
        in_specs=[pl.BlockSpec((tm, d), lambda i: (i, 0)), pl.BlockSpec((1, d), lambda i: (0, 0))],
        out_specs=pl.BlockSpec((tm, d), lambda i: (i, 0)), out_shape=SDS((t, d), BF16), name=name,
        compiler_params=_params(("parallel",)),
    )(x, w.reshape(1, d))


def _rmsnorm_bwd(name, x, w, dh, dres):
    t, d = x.shape
    tm = min(512, t)

    def body(x_ref, w_ref, dh_ref, dres_ref, dx_ref, dw_ref):
        xv = x_ref[...]
        r = lax.rsqrt(jnp.mean(xv * xv, axis=-1, keepdims=True) + EPS)
        xh = xv * r
        dhv = dh_ref[...].astype(F32)
        dxh = dhv * w_ref[...]
        dx_ref[...] = r * (dxh - xh * jnp.mean(dxh * xh, axis=-1, keepdims=True)) + dres_ref[...]

        @pl.when(pl.program_id(0) == 0)
        def _():
            dw_ref[...] = jnp.zeros_like(dw_ref)

        dw_ref[...] += _colsum(dhv * xh)

    row = pl.BlockSpec((tm, d), lambda i: (i, 0))
    vec = pl.BlockSpec((1, d), lambda i: (0, 0))
    return pl.pallas_call(
        body, grid=(t // tm,), in_specs=[row, vec, row, row], out_specs=(row, vec),
        out_shape=(SDS((t, d), F32), SDS((1, d), F32)), name=name, compiler_params=_params(("arbitrary",)),
    )(x, w.reshape(1, d), dh, dres)


def _loss_head(name, x, w, tgt):
    t, d = x.shape
    tm = min(512, t)

    def body(x_ref, w_ref, t_ref, loss_ref, dx_ref, dw_ref):
        xv = x_ref[...]
        wv = w_ref[...]
        r = lax.rsqrt(jnp.mean(xv * xv, axis=-1, keepdims=True) + EPS)
        xh = xv * r
        e = xh * wv - t_ref[...]
        dy = e * (1.0 / d)
        dxh = dy * wv
        dx_ref[...] = r * (dxh - xh * jnp.mean(dxh * xh, axis=-1, keepdims=True))

        @pl.when(pl.program_id(0) == 0)
        def _():
            dw_ref[...] = jnp.zeros_like(dw_ref)
            loss_ref[...] = jnp.zeros_like(loss_ref)

        dw_ref[...] += _colsum(dy * xh)
        loss_ref[...] += 0.5 * jnp.sum(jnp.mean(e * e, axis=-1, keepdims=True), axis=0, keepdims=True)

    row = pl.BlockSpec((tm, d), lambda i: (i, 0))
    vec = pl.BlockSpec((1, d), lambda i: (0, 0))
    return pl.pallas_call(
        body, grid=(t // tm,), in_specs=[row, vec, row],
        out_specs=(pl.BlockSpec((1, 1), lambda i: (0, 0)), row, vec),
        out_shape=(SDS((1, 1), F32), SDS((t, d), F32), SDS((1, d), F32)), name=name,
        compiler_params=_params(("arbitrary",)),
    )(x, w.reshape(1, d), tgt)


def _shift_down(u, s, row):
    return jnp.where(row >= s, pltpu.roll(u, shift=s, axis=0), 0.0)


def _conv_fwd(name, src, col0, w, b):
    t = src.shape[0]
    c = w.shape[1]
    tc = 256
    assert c % tc == 0 and col0 % tc == 0

    def body(u_ref, w_ref, b_ref, o_ref):
        u = u_ref[...]
        wv = w_ref[...]
        row = lax.broadcasted_iota(jnp.int32, u.shape, 0)
        pre = b_ref[...] + wv[3:4, :] * u
        for s in range(1, CONV_K):
            pre = pre + wv[3 - s: 4 - s, :] * _shift_down(u, s, row)
        o_ref[...] = pre * _sigmoid(pre)

    return pl.pallas_call(
        body, grid=(c // tc,),
        in_specs=[pl.BlockSpec((t, tc), lambda j: (0, j + col0 // tc)), pl.BlockSpec((CONV_K, tc), lambda j: (0, j)),
                  pl.BlockSpec((1, tc), lambda j: (0, j))],
        out_specs=pl.BlockSpec((t, tc), lambda j: (0, j)), out_shape=SDS((t, c), F32), name=name,
        compiler_params=_params(("parallel",)),
    )(src, w, b)


def _conv_bwd(name, src, col0, w, b, dact):
    t = src.shape[0]
    c = w.shape[1]
    tc = 128

    def body(u_ref, w_ref, b_ref, da_ref, du_ref, dw_ref, db_ref):
        u = u_ref[...]
        wv = w_ref[...]
        row = lax.broadcasted_iota(jnp.int32, u.shape, 0)
        shifted = [u] + [_shift_down(u, s, row) for s in range(1, CONV_K)]
        pre = b_ref[...] + wv[3:4, :] * u
        for s in range(1, CONV_K):
            pre = pre + wv[3 - s: 4 - s, :] * shifted[s]
        sg = _sigmoid(pre)
        dpre = da_ref[...] * (sg * (1.0 + pre * (1.0 - sg)))
        du = wv[3:4, :] * dpre
        for s in range(1, CONV_K):
            du = du + wv[3 - s: 4 - s, :] * jnp.where(row < t - s, pltpu.roll(dpre, shift=t - s, axis=0), 0.0)
        du_ref[...] = du.astype(BF16)
        for s in range(CONV_K):
            dw_ref[3 - s: 4 - s, :] = _colsum(dpre * shifted[s])
        db_ref[...] = _colsum(dpre)

    return pl.pallas_call(
        body, grid=(c // tc,),
        in_specs=[pl.BlockSpec((t, tc), lambda j: (0, j + col0 // tc)), pl.BlockSpec((CONV_K, tc), lambda j: (0, j)),
                  pl.BlockSpec((1, tc), lambda j: (0, j)), pl.BlockSpec((t, tc), lambda j: (0, j))],
        out_specs=(pl.BlockSpec((t, tc), lambda j: (0, j)), pl.BlockSpec((CONV_K, tc), lambda j: (0, j)),
                   pl.BlockSpec((1, tc), lambda j: (0, j))),
        out_shape=(SDS((t, c), BF16), SDS((CONV_K, c), F32), SDS((1, c), F32)), name=name,
        compiler_params=_params(("parallel",)),
    )(src, w, b, dact)


def _tri(q):
    ii = lax.broadcasted_iota(jnp.int32, (q, q), 0)
    jj = lax.broadcasted_iota(jnp.int32, (q, q), 1)
    return ii, jj


def _ssd_common(act, sm, dtb, arow, rmat):
    q = CHUNK
    ii, jj = _tri(q)
    lane = lax.broadcasted_iota(jnp.int32, (q, 128), 1)
    m16 = lane < SSM_HEADS
    dt = jnp.where(m16, _softplus(sm + dtb), 0.0)
    a = dt * arow
    tril = (ii >= jj).astype(F32)
    triu = (ii <= jj).astype(F32)
    acum = _hdot(tril, a)
    acum_r = _hdot(a.T, triu)
    dtx = _hdot(dt, rmat)
    acx = _hdot(acum, rmat)
    ex = jnp.exp(acx)
    dex = jnp.exp(acx[q - 1: q, :] - acx)
    el = jnp.exp(acum_r[:, q - 1: q])
    elmat = _hdot(rmat, jnp.broadcast_to(el, (128, 128)), TN)
    xs = act[:, :1024]
    return dict(ii=ii, jj=jj, m16=m16, dt=dt, a=a, triu=triu, acum=acum, acum_r=acum_r, dtx=dtx, ex=ex, dex=dex,
                elmat=elmat, xs=xs, x=xs * dtx)


def _ssd_lmat(cm, h):
    return jnp.where(cm["ii"] >= cm["jj"], jnp.exp(cm["acum"][:, h: h + 1] - cm["acum_r"][h: h + 1, :]), 0.0)


def _ssd_fwd(name, act, proj, dtb, arow, dxrow, nw, rmat):
    t = act.shape[0]
    q = CHUNK
    nc = t // q
    hg = SSM_HEADS // SSM_GROUPS
    gw = hg * SSM_P

    def body(act_ref, z_ref, sm_ref, dtb_ref, arow_ref, dx_ref, nw_ref, r_ref, y_ref, ys_ref, st_ref, s_scr, yd_scr):
        @pl.when(pl.program_id(0) == 0)
        def _():
            s_scr[...] = jnp.zeros_like(s_scr)

        s_all = s_scr[...]
        st_ref[0] = s_all
        actv = act_ref[...]
        cm = _ssd_common(actv, sm_ref[...], dtb_ref[...], arow_ref[...], r_ref[...])
        x = cm["x"]
        xd = x * cm["dex"]
        yoffs, snew = [], []
        for g in range(SSM_GROUPS):
            bg = actv[:, 1024 + g * SSM_N: 1024 + (g + 1) * SSM_N]
            cg = actv[:, 1280 + g * SSM_N: 1280 + (g + 1) * SSM_N]
            sg = s_all[g * gw: (g + 1) * gw, :]
            cb = _bdot(cg, bg, NT)
            yoffs.append(_bdot(cg, sg, NT))
            snew.append(_bdot(xd[:, g * gw: (g + 1) * gw], bg, TN))
            for r in range(hg):
                h = g * hg + r
                mm = cb * _ssd_lmat(cm, h)
                yd_scr[:, h * SSM_P: (h + 1) * SSM_P] = _bdot(mm, x[:, h * SSM_P: (h + 1) * SSM_P], NN)
        s_scr[...] = s_all * cm["elmat"] + jnp.concatenate(snew, axis=0)
        ysc = yd_scr[...] + jnp.concatenate(yoffs, axis=1) * cm["ex"]
        ys_ref[...] = ysc
        zv = z_ref[...]
        yg = (ysc + dx_ref[...] * cm["xs"]) * (zv * _sigmoid(zv))
        nwv = nw_ref[...]
        for g in range(SSM_GROUPS):
            sl = yg[:, g * gw: (g + 1) * gw]
            rr = lax.rsqrt(jnp.mean(sl * sl, axis=-1, keepdims=True) + EPS)
            y_ref[:, g * gw: (g + 1) * gw] = (sl * rr * nwv[:, g * gw: (g + 1) * gw]).astype(BF16)

    vec128 = pl.BlockSpec((1, 128), lambda c: (0, 0))
    vec1k = pl.BlockSpec((1, 1024), lambda c: (0, 0))
    return pl.pallas_call(
        body, grid=(nc,),
        in_specs=[pl.BlockSpec((q, SSM_CONV), lambda c: (c, 0)), pl.BlockSpec((q, 1024), lambda c: (c, Z_OFF // 1024)),
                  pl.BlockSpec((q, 128), lambda c: (c, SM_OFF // 128)), vec128, vec128, vec1k, vec1k,
                  pl.BlockSpec((128, 1024), lambda c: (0, 0))],
        out_specs=(pl.BlockSpec((q, 1024), lambda c: (c, 0)), pl.BlockSpec((q, 1024), lambda c: (c, 0)),
                   pl.BlockSpec((1, 1024, 128), lambda c: (c, 0, 0))),
        out_shape=(SDS((t, 1024), BF16), SDS((t, 1024), F32), SDS((nc, 1024, 128), F32)),
        scratch_shapes=[pltpu.VMEM((1024, 128), F32), pltpu.VMEM((q, 1024), F32)], name=name,
        compiler_params=_params(("arbitrary",)),
    )(act, proj, proj, dtb, arow, dxrow, nw, rmat)


def _ssd_bwd(name, act, proj, dtb, arow, dxrow, nw, rmat, ysc, states, dy):
    t = act.shape[0]
    q = CHUNK
    nc = t // q
    hg = SSM_HEADS // SSM_GROUPS
    gw = hg * SSM_P

    def body(act_ref, z_ref, sm_ref, dtb_ref, arow_ref, dx_ref, nw_ref, r_ref, ys_ref, st_ref, dy_ref,
             dact_ref, dz_ref, dsm_ref, dnw_ref, dd_ref, dal_ref, ddtb_ref, ds_scr, dxd_scr):
        @pl.when(pl.program_id(0) == 0)
        def _():
            ds_scr[...] = jnp.zeros_like(ds_scr)
            dnw_ref[...] = jnp.zeros_like(dnw_ref)
            dd_ref[...] = jnp.zeros_like(dd_ref)
            dal_ref[...] = jnp.zeros_like(dal_ref)
            ddtb_ref[...] = jnp.zeros_like(ddtb_ref)

        actv = act_ref[...]
        smv = sm_ref[...]
        rmat_v = r_ref[...]
        cm = _ssd_common(actv, smv, dtb_ref[...], arow_ref[...], rmat_v)
        ii, jj = cm["ii"], cm["jj"]
        x, xs = cm["x"], cm["xs"]
        s_all = st_ref[0]
        dsn = ds_scr[...]
        ysv = ys_ref[...]
        dxr = dx_ref[...]
        y = ysv + dxr * xs
        zv = z_ref[...]
        sz = _sigmoid(zv)
        silz = zv * sz
        yg = y * silz
        dout = dy_ref[...]
        nwv = nw_ref[...]
        dyn = dout * nwv
        yn_parts, dyg_parts = [], []
        for g in range(SSM_GROUPS):
            sl = yg[:, g * gw: (g + 1) * gw]
            rr = lax.rsqrt(jnp.mean(sl * sl, axis=-1, keepdims=True) + EPS)
            yn = sl * rr
            dn = dyn[:, g * gw: (g + 1) * gw]
            yn_parts.append(yn)
            dyg_parts.append(rr * (dn - yn * jnp.mean(dn * yn, axis=-1, keepdims=True)))
        dnw_ref[...] += _colsum(dout * jnp.concatenate(yn_parts, axis=1))
        dyg = jnp.concatenate(dyg_parts, axis=1)
        dyv = dyg * silz
        dz_ref[...] = (dyg * y * (sz * (1.0 + zv * (1.0 - sz)))).astype(BF16)
        dd_ref[...] += _colsum(_hdot(dyv * xs, rmat_v, NT))
        dxs = dyv * dxr
        dcs = dyv * cm["ex"]
        xd = x * cm["dex"]
        dxst_parts, ds_parts, db_parts, dc_parts, yoff_parts, wcol_rows = [], [], [], [], [], []
        lane128 = lax.broadcasted_iota(jnp.int32, (q, 128), 1)
        wrow = jnp.zeros((q, 128), F32)
        for g in range(SSM_GROUPS):
            bg = actv[:, 1024 + g * SSM_N: 1024 + (g + 1) * SSM_N]
            cg = actv[:, 1280 + g * SSM_N: 1280 + (g + 1) * SSM_N]
            sg = s_all[g * gw: (g + 1) * gw, :]
            dsng = dsn[g * gw: (g + 1) * gw, :]
            dcsg = dcs[:, g * gw: (g + 1) * gw]
            dcg = _bdot(dcsg, sg, NN)
            yoff_parts.append(_bdot(cg, sg, NT))
            ds_parts.append(_bdot(dcsg, cg, TN))
            dxst_parts.append(_bdot(bg, dsng, NT))
            dbg = _bdot(xd[:, g * gw: (g + 1) * gw], dsng, NN)
            cb = _bdot(cg, bg, NT)
            dcb = jnp.zeros((q, q), F32)
            for r in range(hg):
                h = g * hg + r
                lm = _ssd_lmat(cm, h)
                mm = cb * lm
                dyh = dyv[:, h * SSM_P: (h + 1) * SSM_P]
                dm = jnp.where(ii >= jj, _bdot(dyh, x[:, h * SSM_P: (h + 1) * SSM_P], NT), 0.0)
                dxd_scr[:, h * SSM_P: (h + 1) * SSM_P] = _bdot(mm, dyh, TN)
                dcb = dcb + dm * lm
                wm = dm * mm
                wrow = wrow + jnp.where(lane128 == h, _rowsum(wm), 0.0)
                wcol_rows.append(_colsum(wm))
            dc_parts.append(dcg + _bdot(dcb, bg, NN))
            db_parts.append(dbg + _bdot(dcb, cg, TN))
        dxst = jnp.concatenate(dxst_parts, axis=1) * cm["dex"]
        dx = dxd_scr[...] + dxst
        ds_scr[...] = jnp.concatenate(ds_parts, axis=0) + dsn * cm["elmat"]
        wcol = jnp.concatenate(wcol_rows + [jnp.zeros((128 - SSM_HEADS, q), F32)], axis=0).T
        yoff = jnp.concatenate(yoff_parts, axis=1) * cm["ex"]
        dac = wrow - wcol + _hdot(dyv * yoff - x * dxst, rmat_v, NT)
        term1 = jnp.sum(_hdot(rmat_v, dsn * s_all * cm["elmat"]).T, axis=0, keepdims=True)
        term2 = _colsum(_hdot(x * dxst, rmat_v, NT))
        rowq = lax.broadcasted_iota(jnp.int32, (q, 128), 0)
        dac = dac + jnp.where(rowq == q - 1, term1 + term2, 0.0)
        da = _hdot(cm["triu"], dac)
        arow_v = arow_ref[...]
        ddt = da * arow_v + _hdot(dx * xs, rmat_v, NT)
        dxs = dxs + dx * cm["dtx"]
        dal_ref[...] += _colsum(da * cm["a"])
        ddtraw = jnp.where(cm["m16"], ddt * _sigmoid(smv + dtb_ref[...]), 0.0)
        ddtb_ref[...] += _colsum(ddtraw)
        dsm_ref[...] = ddtraw.astype(BF16)
        dact_ref[:, :1024] = dxs
        for g in range(SSM_GROUPS):
            dact_ref[:, 1024 + g * SSM_N: 1024 + (g + 1) * SSM_N] = db_parts[g]
            dact_ref[:, 1280 + g * SSM_N: 1280 + (g + 1) * SSM_N] = dc_parts[g]

    rev = lambda c: nc - 1 - c
    vec128 = pl.BlockSpec((1, 128), lambda c: (0, 0))
    vec1k = pl.BlockSpec((1, 1024), lambda c: (0, 0))
    return pl.pallas_call(
        body, grid=(nc,),
        in_specs=[pl.BlockSpec((q, SSM_CONV), lambda c: (rev(c), 0)),
                  pl.BlockSpec((q, 1024), lambda c: (rev(c), Z_OFF // 1024)),
                  pl.BlockSpec((q, 128), lambda c: (rev(c), SM_OFF // 128)), vec128, vec128, vec1k, vec1k,
                  pl.BlockSpec((128, 1024), lambda c: (0, 0)),
                  pl.BlockSpec((q, 1024), lambda c: (rev(c), 0)), pl.BlockSpec((1, 1024, 128), lambda c: (rev(c), 0, 0)),
                  pl.BlockSpec((q, 1024), lambda c: (rev(c), 0))],
        out_specs=(pl.BlockSpec((q, SSM_CONV), lambda c: (rev(c), 0)), pl.BlockSpec((q, 1024), lambda c: (rev(c), 0)),
                   pl.BlockSpec((q, 128), lambda c: (rev(c), 0)), vec1k, vec128, vec128, vec128),
        out_shape=(SDS((t, SSM_CONV), F32), SDS((t, 1024), BF16), SDS((t, 128), BF16), SDS((1, 1024), F32),
                   SDS((1, 128), F32), SDS((1, 128), F32), SDS((1, 128), F32)),
        scratch_shapes=[pltpu.VMEM((1024, 128), F32), pltpu.VMEM((q, 1024), F32)], name=name,
        compiler_params=_params(("arbitrary",)),
    )(act, proj, proj, dtb, arow, dxrow, nw, rmat, ysc, states, dy)


def _tri_inverse(amat, ii, jj):
    tm = jnp.where(ii == jj, 1.0, 0.0) - amat
    p = amat
    for _ in range(5):
        p = _hdot(p, p)
        tm = tm + _hdot(tm, p)
    return tm


def _gdn_common(sm, gb, garow):
    q = CHUNK
    ii, jj = _tri(q)
    lane = lax.broadcasted_iota(jnp.int32, (q, 128), 1)
    ma = (lane >= LANE_A) & (lane < LANE_A + GDN_HEADS)
    spre = sm + gb
    g = jnp.where(ma, garow * _softplus(spre), 0.0)
    beta = _sigmoid(sm)
    tril = (ii >= jj).astype(F32)
    triu = (ii <= jj).astype(F32)
    gc = _hdot(tril, g)
    gc_r = _hdot(g.T, triu)
    return dict(ii=ii, jj=jj, lane=lane, ma=ma, spre=spre, g=g, beta=beta, triu=triu, gc=gc, gc_r=gc_r)


def _gdn_head(cm, actv, s, h):
    q = CHUNK
    ii, jj = cm["ii"], cm["jj"]
    la, lb = LANE_A + h, LANE_B + h
    qr = actv[:, h * 128: (h + 1) * 128]
    kr = actv[:, 1024 + h * 128: 1024 + (h + 1) * 128]
    v = actv[:, 2048 + h * 128: 2048 + (h + 1) * 128]
    rq = lax.rsqrt(_rowsum(qr * qr) + EPS)
    rk = lax.rsqrt(_rowsum(kr * kr) + EPS)
    scale = GDN_DK ** -0.5
    qn = qr * rq * scale
    kn = kr * rk
    gcc = cm["gc"][:, la: la + 1]
    gcr = cm["gc_r"][la: la + 1, :]
    bcol = cm["beta"][:, lb: lb + 1]
    dm = jnp.where(ii >= jj, jnp.exp(gcc - gcr), 0.0)
    kk = _bdot(kn, kn, NT)
    ak = jnp.where(ii > jj, kk * dm, 0.0)
    tm = _tri_inverse(ak * bcol, ii, jj)
    eg = jnp.exp(gcc)
    gl = gcc[q - 1: q, :]
    rm = jnp.concatenate([v * bcol, kn * (bcol * eg)], axis=1)
    tt = _hdot(tm, rm)
    u, w = tt[:, :128], tt[:, 128:]
    vnew = u - _bdot(w, s, NN)
    qkm = jnp.where(ii >= jj, _bdot(qn, kn, NT) * dm, 0.0)
    return dict(qr=qr, kr=kr, v=v, rq=rq, rk=rk, scale=scale, qn=qn, kn=kn, gcc=gcc, bcol=bcol, dm=dm, ak=ak, tm=tm,
                eg=eg, gl=gl, egl=jnp.exp(gl), ed=jnp.exp(gl - gcc), rm=rm, tt=tt, w=w, vnew=vnew, qkm=qkm, qg=qn * eg)


def _gdn_fwd(name, act, proj, gb, garow, gnw):
    t = act.shape[0]
    q = CHUNK
    nc = t // q

    def body(act_ref, gz_ref, sm_ref, gb_ref, ga_ref, nw_ref, y_ref, o_ref, st_ref, s_scr):
        @pl.when(pl.program_id(0) == 0)
        def _():
            s_scr[...] = jnp.zeros_like(s_scr)

        st_ref[0] = s_scr[...]
        actv = act_ref[...]
        cm = _gdn_common(sm_ref[...], gb_ref[...], ga_ref[...])
        nwv = nw_ref[...]
        for h in range(GDN_HEADS):
            s = s_scr[h * 128: (h + 1) * 128, :]
            hd = _gdn_head(cm, actv, s, h)
            o = _bdot(hd["qg"], s, NN) + _bdot(hd["qkm"], hd["vnew"], NN)
            s_scr[h * 128: (h + 1) * 128, :] = s * hd["egl"] + _bdot(hd["kn"] * hd["ed"], hd["vnew"], TN)
            o_ref[:, h * 128: (h + 1) * 128] = o
            rr = lax.rsqrt(jnp.mean(o * o, axis=-1, keepdims=True) + EPS)
            gz = gz_ref[:, h * 128: (h + 1) * 128]
            y_ref[:, h * 128: (h + 1) * 128] = (o * rr * nwv * (gz * _sigmoid(gz))).astype(BF16)

    vec128 = pl.BlockSpec((1, 128), lambda c: (0, 0))
    return pl.pallas_call(
        body, grid=(nc,),
        in_specs=[pl.BlockSpec((q, GDN_QKV), lambda c: (c, 0)), pl.BlockSpec((q, 1024), lambda c: (c, GZ_OFF // 1024)),
                  pl.BlockSpec((q, 128), lambda c: (c, SM_OFF // 128)), vec128, vec128, vec128],
        out_specs=(pl.BlockSpec((q, 1024), lambda c: (c, 0)), pl.BlockSpec((q, 1024), lambda c: (c, 0)),
                   pl.BlockSpec((1, 1024, 128), lambda c: (c, 0, 0))),
        out_shape=(SDS((t, 1024), BF16), SDS((t, 1024), F32), SDS((nc, 1024, 128), F32)),
        scratch_shapes=[pltpu.VMEM((1024, 128), F32)], name=name, compiler_params=_params(("arbitrary",)),
    )(act, proj, proj, gb, garow, gnw)


def _gdn_bwd(name, act, proj, gb, garow, gnw, oraw, states, dy):
    t = act.shape[0]
    q = CHUNK
    nc = t // q

    def body(act_ref, gz_ref, sm_ref, gb_ref, ga_ref, nw_ref, o_ref, st_ref, dy_ref,
             dact_ref, dgz_ref, dsm_ref, dnw_ref, dal_ref, dgb_ref, ds_scr):
        @pl.when(pl.program_id(0) == 0)
        def _():
            ds_scr[...] = jnp.zeros_like(ds_scr)
            dnw_ref[...] = jnp.zeros_like(dnw_ref)
            dal_ref[...] = jnp.zeros_like(dal_ref)
            dgb_ref[...] = jnp.zeros_like(dgb_ref)

        actv = act_ref[...]
        smv = sm_ref[...]
        garow_v = ga_ref[...]
        cm = _gdn_common(smv, gb_ref[...], garow_v)
        ii, jj, lane = cm["ii"], cm["jj"], cm["lane"]
        nwv = nw_ref[...]
        rowq = lax.broadcasted_iota(jnp.int32, (q, 1), 0)
        dgc_all = jnp.zeros((q, 128), F32)
        dbeta_all = jnp.zeros((q, 128), F32)
        dnw_acc = jnp.zeros((1, 128), F32)
        for h in range(GDN_HEADS):
            hs = slice(h * 128, (h + 1) * 128)
            s = st_ref[0, hs, :]
            hd = _gdn_head(cm, actv, s, h)
            qn, kn, v, eg, ed, egl, bcol = hd["qn"], hd["kn"], hd["v"], hd["eg"], hd["ed"], hd["egl"], hd["bcol"]
            vnew, qkm, qg, w, tt, rm, dm = hd["vnew"], hd["qkm"], hd["qg"], hd["w"], hd["tt"], hd["rm"], hd["dm"]
            o = o_ref[:, hs]
            rr = lax.rsqrt(jnp.mean(o * o, axis=-1, keepdims=True) + EPS)
            on = o * rr
            gz = gz_ref[:, hs]
            sz = _sigmoid(gz)
            silz = gz * sz
            dyh = dy_ref[:, hs]
            dnw_acc = dnw_acc + _colsum(dyh * on * silz)
            dgz_ref[:, hs] = (dyh * on * nwv * (sz * (1.0 + gz * (1.0 - sz)))).astype(BF16)
            don = dyh * nwv * silz
            do = rr * (don - on * jnp.mean(don * on, axis=-1, keepdims=True))
            dsn = ds_scr[hs, :]
            kd = kn * ed
            dkd = _bdot(vnew, dsn, NT)
            dvnew = _bdot(kd, dsn, NN)
            ds = dsn * egl
            dk = dkd * ed
            ded = _rowsum(dkd * kd)
            dgc = -ded
            dgl = jnp.sum(_rowsum(dsn * s), axis=0, keepdims=True) * egl + _colsum(ded)
            dqg = _bdot(do, s, NT)
            ds = ds + _bdot(qg, do, TN)
            dqk = jnp.where(ii >= jj, _bdot(do, vnew, NT), 0.0)
            dvnew_b = _bdot(qkm, do, TN)
            dvnew = dvnew + dvnew_b
            dq = dqg * eg
            dgc = dgc + _rowsum(dqg * qg)
            pq = dqk * dm
            dq = dq + _bdot(pq, kn, NN)
            dk = dk + _bdot(pq, qn, TN)
            w1 = dqk * qkm
            dgc = dgc + _rowsum(w1) - _rowsum(w1.T)
            dw = -_bdot(dvnew, s, NT)
            ds = ds - _bdot(w, dvnew, TN)
            dtt = jnp.concatenate([dvnew, dw], axis=1)
            dr = _hdot(hd["tm"], dtt, TN)
            da = jnp.where(ii > jj, -_hdot(dr, tt, NT), 0.0)
            dvb, dkbg = dr[:, :128], dr[:, 128:]
            dv = dvb * bcol
            dbeta = _rowsum(dvb * v)
            sk = _rowsum(dkbg * kn)
            dk = dk + dkbg * (bcol * eg)
            dbeta = dbeta + sk * eg + _rowsum(da * hd["ak"])
            dgc = dgc + sk * bcol * eg
            pk = da * dm * bcol
            dk = dk + _bdot(pk, kn, NN) + _bdot(pk, kn, TN)
            w2 = da * (hd["ak"] * bcol)
            dgc = dgc + _rowsum(w2) - _rowsum(w2.T)
            dgc = dgc + jnp.where(rowq == q - 1, dgl, 0.0)
            qhat = hd["qr"] * hd["rq"]
            dqhat = dq * hd["scale"]
            dact_ref[:, hs] = hd["rq"] * (dqhat - qhat * _rowsum(dqhat * qhat))
            dact_ref[:, 1024 + h * 128: 1024 + (h + 1) * 128] = hd["rk"] * (dk - kn * _rowsum(dk * kn))
            dact_ref[:, 2048 + h * 128: 2048 + (h + 1) * 128] = dv
            dgc_all = dgc_all + jnp.where(lane == LANE_A + h, dgc, 0.0)
            dbeta_all = dbeta_all + jnp.where(lane == LANE_B + h, dbeta, 0.0)
            ds_scr[hs, :] = ds
        dnw_ref[...] += dnw_acc
        dg = _hdot(cm["triu"], dgc_all)
        da_raw = jnp.where(cm["ma"], dg * garow_v * _sigmoid(cm["spre"]), 0.0)
        dal_ref[...] += _colsum(dg * cm["g"])
        dgb_ref[...] += _colsum(da_raw)
        beta = cm["beta"]
        dsm_ref[...] = (da_raw + dbeta_all * beta * (1.0 - beta)).astype(BF16)

    rev = lambda c: nc - 1 - c
    vec128 = pl.BlockSpec((1, 128), lambda c: (0, 0))
    return pl.pallas_call(
        body, grid=(nc,),
        in_specs=[pl.BlockSpec((q, GDN_QKV), lambda c: (rev(c), 0)),
                  pl.BlockSpec((q, 1024), lambda c: (rev(c), GZ_OFF // 1024)),
                  pl.BlockSpec((q, 128), lambda c: (rev(c), SM_OFF // 128)), vec128, vec128, vec128,
                  pl.BlockSpec((q, 1024), lambda c: (rev(c), 0)), pl.BlockSpec((1, 1024, 128), lambda c: (rev(c), 0, 0)),
                  pl.BlockSpec((q, 1024), lambda c: (rev(c), 0))],
        out_specs=(pl.BlockSpec((q, GDN_QKV), lambda c: (rev(c), 0)), pl.BlockSpec((q, 1024), lambda c: (rev(c), 0)),
                   pl.BlockSpec((q, 128), lambda c: (rev(c), 0)), vec128, vec128, vec128),
        out_shape=(SDS((t, GDN_QKV), F32), SDS((t, 1024), BF16), SDS((t, 128), BF16), SDS((1, 128), F32),
                   SDS((1, 128), F32), SDS((1, 128), F32)),
        scratch_shapes=[pltpu.VMEM((1024, 128), F32)], name=name, compiler_params=_params(("arbitrary",)),
    )(act, proj, proj, gb, garow, gnw, oraw, states, dy)


def _row_tile(r):
    for cand in (512, 256, 128, 64, 32, 16, 8):
        if r % cand == 0:
            return cand
    return r


def _sum_arrays(name, arrs, out_dtype):
    r, c = arrs[0].shape
    tr = _row_tile(r)
    n = len(arrs)

    def body(*refs):
        acc = refs[0][...].astype(F32)
        for k in range(1, n):
            acc = acc + refs[k][...].astype(F32)
        refs[n][...] = acc.astype(out_dtype)

    spec = pl.BlockSpec((tr, c), lambda i: (i, 0))
    return pl.pallas_call(body, grid=(r // tr,), in_specs=[spec] * n, out_specs=spec, out_shape=SDS((r, c), out_dtype),
                          name=name, compiler_params=_params(("parallel",)))(*arrs)


def _adamw(name, w, g, m, v):
    shape = w.shape
    c = shape[-1]
    w2, g2, m2, v2 = (a.reshape(-1, c) for a in (w, g, m, v))
    r = w2.shape[0]
    tr = min(_row_tile(r), 256)

    def body(w_ref, g_ref, m_ref, v_ref, d_ref, nm_ref, nv_ref):
        gv = g_ref[...]
        mn = ADAM_B1 * m_ref[...] + (1.0 - ADAM_B1) * gv
        vn = ADAM_B2 * v_ref[...] + (1.0 - ADAM_B2) * (gv * gv)
        m_hat = mn / (1.0 - ADAM_B1 ** ADAM_STEP)
        v_hat = vn / (1.0 - ADAM_B2 ** ADAM_STEP)
        d_ref[...] = -ADAM_LR * (m_hat / (jnp.sqrt(v_hat) + ADAM_EPS) + ADAM_WD * w_ref[...])
        nm_ref[...] = mn
        nv_ref[...] = vn

    spec = pl.BlockSpec((tr, c), lambda i: (i, 0))
    outs = pl.pallas_call(body, grid=(r // tr,), in_specs=[spec] * 4, out_specs=(spec,) * 3,
                          out_shape=(SDS((r, c), F32),) * 3, name=name, compiler_params=_params(("parallel",)))(w2, g2, m2, v2)
    return tuple(o.reshape(shape) for o in outs)


ANY = pl.BlockSpec(memory_space=pl.ANY)
MESH = pl.DeviceIdType.MESH


def _allgather(name, xs):
    r, c = xs.shape

    def body(x_ref, out_ref, send_sems, recv_sems, local_sem):
        x, y, cc = lax.axis_index("x"), lax.axis_index("y"), lax.axis_index("c")
        me, sibling = (x, y, cc), (x, y, 1 - cc)
        chips = [(1 - x, y), (x, 1 - y), (1 - x, 1 - y)]

        def rows(px, py, pc):
            return out_ref.at[4 * px + 2 * py + pc]

        def copy(k, block, to, src=None):
            return pltpu.make_async_remote_copy(
                src_ref=rows(*block) if src is None else src, dst_ref=rows(*block),
                send_sem=send_sems.at[k], recv_sem=recv_sems.at[k], device_id=to, device_id_type=MESH)

        mine = pltpu.make_async_copy(x_ref, rows(*me), local_sem)
        mine.start()
        first = [copy(0, me, sibling, src=x_ref)]
        first += [copy(1 + j, me, (*chip, cc), src=x_ref) for j, chip in enumerate(chips)]
        for cp in first:
            cp.start()
        passed = [copy(4 + j, (*chip, cc), sibling) for j, chip in enumerate(chips)]
        for j, chip in enumerate(chips):
            copy(1 + j, (*chip, cc), me).wait_recv()
            passed[j].start()
        copy(0, sibling, me).wait_recv()
        for j, chip in enumerate(chips):
            copy(4 + j, (*chip, 1 - cc), me).wait_recv()
        for cp in first + passed:
            cp.wait_send()
        mine.wait()

    return pl.pallas_call(
        body, out_shape=SDS((N_DEV, r, c), xs.dtype), in_specs=[ANY], out_specs=ANY,
        scratch_shapes=[pltpu.SemaphoreType.DMA((7,)), pltpu.SemaphoreType.DMA((7,)), pltpu.SemaphoreType.DMA],
        name=name,
    )(xs)


def _sibling_swap(name, xs):
    def body(x_ref, out_ref, send_sem, recv_sem):
        x, y, cc = lax.axis_index("x"), lax.axis_index("y"), lax.axis_index("c")
        cp = pltpu.make_async_remote_copy(src_ref=x_ref, dst_ref=out_ref, send_sem=send_sem, recv_sem=recv_sem,
                                          device_id=(x, y, 1 - cc), device_id_type=MESH)
        cp.start()
        cp.wait()

    return pl.pallas_call(
        body, out_shape=SDS(xs.shape, xs.dtype), in_specs=[ANY], out_specs=ANY,
        scratch_shapes=[pltpu.SemaphoreType.DMA, pltpu.SemaphoreType.DMA], name=name,
    )(xs)


def _chip_exchange(name, xs):
    def body(x_ref, out_ref, send_sems, recv_sems, local_sem):
        x, y, cc = lax.axis_index("x"), lax.axis_index("y"), lax.axis_index("c")
        my_chip = 2 * x + y
        chips = [(1 - x, y), (x, 1 - y), (1 - x, 1 - y)]
        mine = pltpu.make_async_copy(x_ref.at[my_chip], out_ref.at[my_chip], local_sem)
        mine.start()
        copies = []
        for j, (px, py) in enumerate(chips):
            copies.append(pltpu.make_async_remote_copy(
                src_ref=x_ref.at[2 * px + py], dst_ref=out_ref.at[my_chip], send_sem=send_sems.at[j],
                recv_sem=recv_sems.at[j], device_id=(px, py, cc), device_id_type=MESH))
        for cp in copies:
            cp.start()
        for j, (px, py) in enumerate(chips):
            pltpu.make_async_remote_copy(
                src_ref=x_ref.at[2 * px + py], dst_ref=out_ref.at[2 * px + py], send_sem=send_sems.at[j],
                recv_sem=recv_sems.at[j], device_id=(px, py, cc), device_id_type=MESH).wait_recv()
        for cp in copies:
            cp.wait_send()
        mine.wait()

    return pl.pallas_call(
        body, out_shape=SDS(xs.shape, xs.dtype), in_specs=[ANY], out_specs=ANY,
        scratch_shapes=[pltpu.SemaphoreType.DMA((3,)), pltpu.SemaphoreType.DMA((3,)), pltpu.SemaphoreType.DMA],
        name=name,
    )(xs)


BIG = (("w_in", 2, (DEPTH, 1024, 1092)), ("w_ffn_in", 2, (DEPTH, 1024, 704)), ("w_proj_ssm", 1, (DEPTH, 128, 1024)),
       ("w_proj_gdn", 1, (DEPTH, 128, 1024)), ("w_out", 1, (DEPTH, 128, 1024)), ("w_ffn_down", 1, (DEPTH, 352, 1024)))
CONVS = (("ssm_conv_w", 2, (DEPTH, CONV_K, 192)), ("gdn_conv_w", 2, (DEPTH, CONV_K, 384)))


def _rows_of(shape):
    n = 1
    for s in shape:
        n *= s
    return -(-n // 1024)


def _pack(shards, specs, pad_to):
    parts = []
    for a, (_, _, shape) in zip(shards, specs):
        flat = a.reshape(-1)
        rows = _rows_of(shape)
        flat = jnp.pad(flat, (0, rows * 1024 - flat.shape[0]))
        parts.append(flat.reshape(rows, 1024))
    buf = jnp.concatenate(parts, axis=0)
    return jnp.pad(buf, ((0, pad_to - buf.shape[0]), (0, 0)))


def _unpack(buf, specs):
    out, r0 = [], 0
    for _, _, shape in specs:
        rows = _rows_of(shape)
        n = 1
        for s in shape:
            n *= s
        out.append(buf[r0: r0 + rows].reshape(-1)[:n].reshape(shape))
        r0 += rows
    return out


def _packed_rows(specs, mult):
    r = sum(_rows_of(s[2]) for s in specs)
    return -(-r // mult) * mult


def _permute_in_cols(w):
    sm = jnp.concatenate([w[..., O_DT: O_DT + 16], w[..., O_A: O_A + 8], w[..., O_B: O_B + 8]], axis=-1)
    pad_sm = jnp.zeros(w.shape[:-1] + (128 - 32,), w.dtype)
    pad_end = jnp.zeros(w.shape[:-1] + (PROJ_W - SM_OFF - 128,), w.dtype)
    return jnp.concatenate([w[..., O_Z: O_Z + 1024], w[..., O_GZ: O_GZ + 1024], w[..., O_G1: O_G1 + 1024],
                            w[..., O_G2: O_G2 + 1024], w[..., O_QKV: O_QKV + 3072], w[..., O_XBC: O_XBC + 1536],
                            sm, pad_sm, pad_end], axis=-1)


def _unpermute_in_cols(w):
    return jnp.concatenate([w[..., Z_OFF: Z_OFF + 1024], w[..., XBC_OFF: XBC_OFF + 1536], w[..., SM_OFF: SM_OFF + 16],
                            w[..., QKV_OFF: QKV_OFF + 3072], w[..., GZ_OFF: GZ_OFF + 1024],
                            w[..., SM_OFF + LANE_A: SM_OFF + LANE_A + 8], w[..., SM_OFF + LANE_B: SM_OFF + LANE_B + 8],
                            w[..., G1_OFF: G1_OFF + 1024], w[..., G2_OFF: G2_OFF + 1024]], axis=-1)


def _pad128(v, lane0):
    return jnp.zeros((1, 128), F32).at[0, lane0: lane0 + v.shape[0]].set(v)


def _layer_consts(p):
    return dict(
        dtb=_pad128(p["ssm_dt_bias"], 0), arow=_pad128(-jnp.exp(p["ssm_a_log"]), 0),
        dxrow=jnp.repeat(p["ssm_d"], SSM_P).reshape(1, 1024), snw=p["ssm_norm_w"].reshape(1, 1024),
        gb=_pad128(p["gdn_dt_bias"], LANE_A), garow=_pad128(-jnp.exp(p["gdn_a_log"]), LANE_A),
        gnw=p["gdn_norm_w"].reshape(1, 128), zb=jnp.zeros((1, GDN_QKV), F32), scb=p["ssm_conv_b"].reshape(1, SSM_CONV))


def _expand_matrix():
    row = lax.broadcasted_iota(jnp.int32, (128, 1024), 0)
    col = lax.broadcasted_iota(jnp.int32, (128, 1024), 1)
    return (col // SSM_P == row).astype(F32)


def _silu_mul_epi(acc, up):
    g = acc
    return g, g * _sigmoid(g) * up.astype(F32)


def _merge_epi(acc, p1, g1, g2):
    return acc, _sigmoid(g1) * p1.astype(F32) + _sigmoid(g2) * acc


def _add_epi(acc, res):
    return (acc + res,)


def _ffn_bwd_epi(acc, gate, up):
    g = gate.astype(F32)
    sg = _sigmoid(g)
    return acc * up.astype(F32) * (sg * (1.0 + g * (1.0 - sg))), acc * (g * sg)


def _merge_bwd_epi(acc, g1, g2, p1, p2):
    s1, s2 = _sigmoid(g1), _sigmoid(g2)
    return acc * s1, acc * s2, acc * p1.astype(F32) * (s1 * (1.0 - s1)), acc * p2.astype(F32) * (s2 * (1.0 - s2))


def _layer_fwd(l, x, p, rmat):
    t = x.shape[0]
    n = f"l{l}_"
    k = _layer_consts(p)
    h = _rmsnorm_fwd(n + "norm_mix", x, p["norm_mix_w"])
    proj = _matmul(n + "in_proj", "nn", [(h, 0, p["w_in"], 0)], t, PROJ_W, 1024, 512, 1280, 1024, (F32,))
    act_g = _conv_fwd(n + "conv_gdn", proj, QKV_OFF, p["gdn_conv_w"], k["zb"])
    act_s = _conv_fwd(n + "conv_ssm", proj, XBC_OFF, p["ssm_conv_w"], k["scb"])
    y_ssm, ysc, st_s = _ssd_fwd(n + "ssd_fwd", act_s, proj, k["dtb"], k["arow"], k["dxrow"], k["snw"], rmat)
    y_gdn, oraw, st_g = _gdn_fwd(n + "gdn_fwd", act_g, proj, k["gb"], k["garow"], k["gnw"])
    p1 = _matmul(n + "proj_ssm", "nn", [(y_ssm, 0, p["w_proj_ssm"], 0)], t, 1024, 1024, 512, 512, 1024, (BF16,))
    p2, merged = _matmul(n + "proj_gdn_merge", "nn", [(y_gdn, 0, p["w_proj_gdn"], 0)], t, 1024, 1024, 512, 512, 1024,
                         (BF16, BF16), epi=_merge_epi, extras=[(p1, 0), (proj, G1_OFF // 512), (proj, G2_OFF // 512)])
    x1 = _matmul(n + "out_proj", "nn", [(merged, 0, p["w_out"], 0)], t, 1024, 1024, 512, 512, 1024, (F32,),
                 epi=_add_epi, extras=[(x, 0)])
    h2 = _rmsnorm_fwd(n + "norm_ffn", x1, p["norm_ffn_w"])
    up = _matmul(n + "ffn_up", "nn", [(h2, 0, p["w_ffn_in"], FFN // 256)], t, FFN, 1024, 512, 256, 1024, (BF16,))
    gate, act = _matmul(n + "ffn_gate", "nn", [(h2, 0, p["w_ffn_in"], 0)], t, FFN, 1024, 512, 256, 1024, (BF16, BF16),
                        epi=_silu_mul_epi, extras=[(up, 0)])
    x2 = _matmul(n + "ffn_down", "nn", [(act, 0, p["w_ffn_down"], 0)], t, 1024, FFN, 512, 512, FFN, (F32,),
                 epi=_add_epi, extras=[(x1, 0)])
    saved = dict(x=x, h=h, proj=proj, act_g=act_g, act_s=act_s, y_ssm=y_ssm, ysc=ysc, st_s=st_s, y_gdn=y_gdn, oraw=oraw,
                 st_g=st_g, p1=p1, p2=p2, merged=merged, x1=x1, h2=h2, up=up, gate=gate, act=act, k=k)
    return x2, saved


def _layer_bwd(l, dx2, s, p, rmat):
    t = dx2.shape[0]
    n = f"l{l}_"
    k = s["k"]
    tk_tok = 1024
    g = {}
    dgate, dup = _matmul(n + "d_ffn_act", "nt", [(dx2, 0, p["w_ffn_down"], 0)], t, FFN, 1024, 512, 256, 1024, (BF16, BF16),
                         epi=_ffn_bwd_epi, extras=[(s["gate"], 0), (s["up"], 0)])
    g["w_ffn_down"] = _matmul(n + "dw_ffn_down", "tn", [(s["act"], 0, dx2, 0)], FFN, 1024, t, 256, 1024, tk_tok, (F32,))
    dh2 = _matmul(n + "d_ffn_in", "nt", [(dgate, 0, p["w_ffn_in"], 0), (dup, 0, p["w_ffn_in"], 1)], t, 1024, FFN,
                  256, 512, FFN, (F32,))
    dwg = _matmul(n + "dw_ffn_gate", "tn", [(s["h2"], 0, dgate, 0)], 1024, FFN, t, 512, 256, tk_tok, (F32,))
    dwu = _matmul(n + "dw_ffn_up", "tn", [(s["h2"], 0, dup, 0)], 1024, FFN, t, 512, 256, tk_tok, (F32,))
    g["w_ffn_in"] = jnp.concatenate([dwg, dwu], axis=1)
    dx1, g["norm_ffn_w"] = _rmsnorm_bwd(n + "d_norm_ffn", s["x1"], p["norm_ffn_w"], dh2, dx2)
    dp1, dp2, dg1, dg2 = _matmul(
        n + "d_out_proj", "nt", [(dx1, 0, p["w_out"], 0)], t, 1024, 1024, 512, 512, 1024, (BF16,) * 4, epi=_merge_bwd_epi,
        extras=[(s["proj"], G1_OFF // 512), (s["proj"], G2_OFF // 512), (s["p1"], 0), (s["p2"], 0)])
    g["w_out"] = _matmul(n + "dw_out", "tn", [(s["merged"], 0, dx1, 0)], 1024, 1024, t, 512, 512, tk_tok, (F32,))
    dy_ssm = _matmul(n + "d_proj_ssm", "nt", [(dp1, 0, p["w_proj_ssm"], 0)], t, 1024, 1024, 512, 512, 1024, (F32,))
    g["w_proj_ssm"] = _matmul(n + "dw_proj_ssm", "tn", [(s["y_ssm"], 0, dp1, 0)], 1024, 1024, t, 512, 512, tk_tok, (F32,))
    dy_gdn = _matmul(n + "d_proj_gdn", "nt", [(dp2, 0, p["w_proj_gdn"], 0)], t, 1024, 1024, 512, 512, 1024, (F32,))
    g["w_proj_gdn"] = _matmul(n + "dw_proj_gdn", "tn", [(s["y_gdn"], 0, dp2, 0)], 1024, 1024, t, 512, 512, tk_tok, (F32,))
    dact_s, dz, dsm_s, dsnw, dd, dal, ddtb = _ssd_bwd(n + "ssd_bwd", s["act_s"], s["proj"], k["dtb"], k["arow"], k["dxrow"],
                                                        k["snw"], rmat, s["ysc"], s["st_s"], dy_ssm)
    dact_g, dgz, dsm_g, dgnw, dgal, dgb = _gdn_bwd(n + "gdn_bwd", s["act_g"], s["proj"], k["gb"], k["garow"], k["gnw"],
                                                     s["oraw"], s["st_g"], dy_gdn)
    du_s, g["ssm_conv_w"], dcb = _conv_bwd(n + "d_conv_ssm", s["proj"], XBC_OFF, p["ssm_conv_w"], k["scb"], dact_s)
    du_g, g["gdn_conv_w"], _ = _conv_bwd(n + "d_conv_gdn", s["proj"], QKV_OFF, p["gdn_conv_w"], k["zb"], dact_g)
    g["ssm_conv_b"] = dcb.reshape(-1)
    g["ssm_norm_w"] = dsnw.reshape(-1)
    g["ssm_d"] = dd[0, :SSM_HEADS]
    g["ssm_a_log"] = dal[0, :SSM_HEADS]
    g["ssm_dt_bias"] = ddtb[0, :SSM_HEADS]
    g["gdn_norm_w"] = dgnw.reshape(-1)
    g["gdn_a_log"] = dgal[0, LANE_A: LANE_A + GDN_HEADS]
    g["gdn_dt_bias"] = dgb[0, LANE_A: LANE_A + GDN_HEADS]
    dproj = jnp.concatenate([dz, dgz, dg1, dg2, du_g, du_s, dsm_s + dsm_g, jnp.zeros((t, PROJ_W - SM_OFF - 128), BF16)],
                            axis=1)
    dh = _matmul(n + "d_in_proj", "nt", [(dproj, 0, p["w_in"], 0)], t, 1024, PROJ_W, 512, 512, 1280, (F32,))
    g["w_in"] = _matmul(n + "dw_in", "tn", [(s["h"], 0, dproj, 0)], 1024, PROJ_W, t, 512, 1280, tk_tok, (F32,))
    dx, g["norm_mix_w"] = _rmsnorm_bwd(n + "d_norm_mix", s["x"], p["norm_mix_w"], dh, dx1)
    g["norm_mix_w"] = g["norm_mix_w"].reshape(-1)
    g["norm_ffn_w"] = g["norm_ffn_w"].reshape(-1)
    return dx, g


def _local_step(x, tgt, layers, final_norm_w):
    rmat = _expand_matrix()
    saved = []
    for l in range(DEPTH):
        x, s = _layer_fwd(l, x, layers[l], rmat)
        saved.append(s)
    loss, dx, dfw = _loss_head("loss_head", x, final_norm_w, tgt)
    grads = [None] * DEPTH
    for l in reversed(range(DEPTH)):
        dx, grads[l] = _layer_bwd(l, dx, saved[l], layers[l], rmat)
    return loss[0, 0], dx, grads, dfw.reshape(-1)


SMALL = ("norm_mix_w", "ssm_conv_b", "ssm_dt_bias", "ssm_a_log", "ssm_d", "ssm_norm_w", "gdn_a_log", "gdn_dt_bias",
         "gdn_norm_w", "norm_ffn_w")
WEIGHTS = ("norm_mix_w", "w_in", "ssm_conv_w", "ssm_conv_b", "ssm_dt_bias", "ssm_a_log", "ssm_d", "ssm_norm_w", "gdn_conv_w",
           "gdn_a_log", "gdn_dt_bias", "gdn_norm_w", "w_proj_ssm", "w_proj_gdn", "w_out", "norm_ffn_w", "w_ffn_in",
           "w_ffn_down", "final_norm_w")


def _gather_full(w):
    rows = _packed_rows(BIG, 128)
    buf = _pack([w[nm].astype(BF16) for nm, _, _ in BIG], BIG, rows)
    full = _allgather("gather_weights", buf)
    crow = _packed_rows(CONVS, 8)
    cfull = _allgather("gather_conv_weights", _pack([w[nm] for nm, _, _ in CONVS], CONVS, crow))
    per_dev = [_unpack(full[d], BIG) for d in range(N_DEV)]
    per_dev_c = [_unpack(cfull[d], CONVS) for d in range(N_DEV)]
    out = {}
    for i, (nm, axis, _) in enumerate(BIG):
        out[nm] = jnp.concatenate([per_dev[d][i] for d in range(N_DEV)], axis=axis)
    for i, (nm, axis, _) in enumerate(CONVS):
        out[nm] = jnp.concatenate([per_dev_c[d][i] for d in range(N_DEV)], axis=axis)
    out["w_in"] = _permute_in_cols(out["w_in"])
    return out


def _reduce_scatter(grads):
    specs = BIG + CONVS
    rows = _packed_rows(specs, 128)
    blocks = []
    for d in range(N_DEV):
        shards = []
        for nm, axis, shape in specs:
            size = shape[axis]
            shards.append(lax.slice_in_dim(grads[nm], d * size, (d + 1) * size, axis=axis))
        blocks.append(_pack(shards, specs, rows))
    gp = jnp.stack(blocks).reshape(4, 2, rows, 1024)
    cc = lax.axis_index("c")
    keep = jnp.where(cc == 0, gp[:, 0], gp[:, 1]).reshape(4 * rows, 1024)
    give = jnp.where(cc == 0, gp[:, 1], gp[:, 0]).reshape(4 * rows, 1024)
    got = _sibling_swap("grads_to_sibling", give)
    chip_sum = _sum_arrays("grads_chip_sum", [keep, got], BF16)
    ex = _chip_exchange("grads_between_chips", chip_sum.reshape(4, rows, 1024))
    total = _sum_arrays("grads_total", [ex[qi] for qi in range(4)], F32)
    return dict(zip([s[0] for s in specs], _unpack(total, specs)))


def _allreduce_small(vecs):
    flat = jnp.concatenate(vecs)
    n = flat.shape[0]
    rows = -(-n // 128)
    rows = -(-rows // 8) * 8
    buf = jnp.pad(flat, (0, rows * 128 - n)).reshape(rows, 128)
    allv = _allgather("gather_small_grads", buf)
    tot = _sum_arrays("small_grads_total", [allv[d] for d in range(N_DEV)], F32).reshape(-1)
    out, o = [], 0
    for v in vecs:
        out.append(tot[o: o + v.shape[0]])
        o += v.shape[0]
    return out


def kernel(x, norm_mix_w, w_in, ssm_conv_w, ssm_conv_b, ssm_dt_bias, ssm_a_log, ssm_d, ssm_norm_w, gdn_conv_w, gdn_a_log, gdn_dt_bias, gdn_norm_w, w_proj_ssm, w_proj_gdn, w_out, norm_ffn_w, w_ffn_in, w_ffn_down, final_norm_w, loss_target, m_norm_mix_w, m_w_in, m_ssm_conv_w, m_ssm_conv_b, m_ssm_dt_bias, m_ssm_a_log, m_ssm_d, m_ssm_norm_w, m_gdn_conv_w, m_gdn_a_log, m_gdn_dt_bias, m_gdn_norm_w, m_w_proj_ssm, m_w_proj_gdn, m_w_out, m_norm_ffn_w, m_w_ffn_in, m_w_ffn_down, m_final_norm_w, v_norm_mix_w, v_w_in, v_ssm_conv_w, v_ssm_conv_b, v_ssm_dt_bias, v_ssm_a_log, v_ssm_d, v_ssm_norm_w, v_gdn_conv_w, v_gdn_a_log, v_gdn_dt_bias, v_gdn_norm_w, v_w_proj_ssm, v_w_proj_gdn, v_w_out, v_norm_ffn_w, v_w_ffn_in, v_w_ffn_down, v_final_norm_w):
    w = dict(norm_mix_w=norm_mix_w, w_in=w_in, ssm_conv_w=ssm_conv_w, ssm_conv_b=ssm_conv_b, ssm_dt_bias=ssm_dt_bias,
             ssm_a_log=ssm_a_log, ssm_d=ssm_d, ssm_norm_w=ssm_norm_w, gdn_conv_w=gdn_conv_w, gdn_a_log=gdn_a_log,
             gdn_dt_bias=gdn_dt_bias, gdn_norm_w=gdn_norm_w, w_proj_ssm=w_proj_ssm, w_proj_gdn=w_proj_gdn, w_out=w_out,
             norm_ffn_w=norm_ffn_w, w_ffn_in=w_ffn_in, w_ffn_down=w_ffn_down, final_norm_w=final_norm_w)
    m = dict(norm_mix_w=m_norm_mix_w, w_in=m_w_in, ssm_conv_w=m_ssm_conv_w, ssm_conv_b=m_ssm_conv_b, ssm_dt_bias=m_ssm_dt_bias,
             ssm_a_log=m_ssm_a_log, ssm_d=m_ssm_d, ssm_norm_w=m_ssm_norm_w, gdn_conv_w=m_gdn_conv_w, gdn_a_log=m_gdn_a_log,
             gdn_dt_bias=m_gdn_dt_bias, gdn_norm_w=m_gdn_norm_w, w_proj_ssm=m_w_proj_ssm, w_proj_gdn=m_w_proj_gdn,
             w_out=m_w_out, norm_ffn_w=m_norm_ffn_w, w_ffn_in=m_w_ffn_in, w_ffn_down=m_w_ffn_down,
             final_norm_w=m_final_norm_w)
    v = dict(norm_mix_w=v_norm_mix_w, w_in=v_w_in, ssm_conv_w=v_ssm_conv_w, ssm_conv_b=v_ssm_conv_b, ssm_dt_bias=v_ssm_dt_bias,
             ssm_a_log=v_ssm_a_log, ssm_d=v_ssm_d, ssm_norm_w=v_ssm_norm_w, gdn_conv_w=v_gdn_conv_w, gdn_a_log=v_gdn_a_log,
             gdn_dt_bias=v_gdn_dt_bias, gdn_norm_w=v_gdn_norm_w, w_proj_ssm=v_w_proj_ssm, w_proj_gdn=v_w_proj_gdn,
             w_out=v_w_out, norm_ffn_w=v_norm_ffn_w, w_ffn_in=v_w_ffn_in, w_ffn_down=v_w_ffn_down,
             final_norm_w=v_final_norm_w)

    full = _gather_full(w)
    layers = []
    for l in range(DEPTH):
        lp = {nm: full[nm][l] for nm in full}
        lp.update({nm: w[nm][l] for nm in SMALL})
        layers.append(lp)
    loss_part, dx, lgrads, dfw = _local_step(x[0], loss_target[0], layers, final_norm_w)
    loss = lax.psum(loss_part, ("x", "y", "c"))

    big = {nm: jnp.stack([lgrads[l][nm] for l in range(DEPTH)]) for nm, _, _ in BIG + CONVS}
    big["w_in"] = _unpermute_in_cols(big["w_in"])
    grad = _reduce_scatter(big)
    small_vecs = [lgrads[l][nm].reshape(-1) for l in range(DEPTH) for nm in SMALL] + [dfw]
    small_sum = _allreduce_small(small_vecs)
    for i, nm in enumerate(SMALL):
        grad[nm] = jnp.stack([small_sum[l * len(SMALL) + i].reshape(w[nm].shape[1:]) for l in range(DEPTH)])
    grad["final_norm_w"] = small_sum[-1]

    deltas, new_m, new_v = {}, {}, {}
    for nm in WEIGHTS:
        deltas[nm], new_m[nm], new_v[nm] = _adamw("adamw_" + nm, w[nm], grad[nm], m[nm], v[nm])
    return (loss, dx[None], *[grad[nm] for nm in WEIGHTS], *[deltas[nm] for nm in WEIGHTS],
            *[new_m[nm] for nm in WEIGHTS], *[new_v[nm] for nm in WEIGHTS])
```

```python
import functools

import jax
import jax.numpy as jnp
from jax import lax
from jax.experimental import pallas as pl
from jax.experimental.pallas import tpu as pltpu

F32 = jnp.float32
BF16 = jnp.bfloat16
HI = lax.Precision.HIGHEST
SDS = jax.ShapeDtypeStruct

D_MODEL = 1024
DEPTH = 2
SSM_HEADS = 16
SSM_P = 64
SSM_N = 128
SSM_GROUPS = 2
SSM_CONV = 1536
GDN_HEADS = 8
GDN_DK = 128
GDN_QKV = 3072
CONV_K = 4
CHUNK = 64
FFN = 2816
IN_DIM = 8736
EPS = 1e-6
N_DEV = 8

Z_OFF = 0
GZ_OFF = 1024
G1_OFF = 2048
G2_OFF = 3072
QKV_OFF = 4096
XBC_OFF = 7168
SM_OFF = 8704
PROJ_W = 8960
LANE_A = 16
LANE_B = 24
O_Z, O_XBC, O_DT, O_QKV, O_GZ, O_A, O_B, O_G1, O_G2 = 0, 1024, 2560, 2576, 5648, 6672, 6680, 6688, 7712

ADAM_LR = 0.001
ADAM_B1 = 0.9
ADAM_B2 = 0.999
ADAM_EPS = 1e-08
ADAM_WD = 0.01
ADAM_STEP = 10

V7X_VMEM_LIMIT = 48 * 1024 * 1024

NN = ((1,), (0,))
NT = ((1,), (1,))
TN = ((0,), (0,))


def _bdot(a, b, dims):
    return lax.dot_general(a.astype(BF16), b.astype(BF16), (dims, ((), ())), preferred_element_type=F32)


def _hdot(a, b, dims=NN):
    return lax.dot_general(a, b, (dims, ((), ())), precision=HI, preferred_element_type=F32)


def _sigmoid(x):
    return 1.0 / (1.0 + jnp.exp(-x))


def _softplus(x):
    return jnp.maximum(x, 0.0) + jnp.log(1.0 + jnp.exp(-jnp.abs(x)))


def _params(dims):
    return pltpu.CompilerParams(dimension_semantics=dims, vmem_limit_bytes=V7X_VMEM_LIMIT)


def _rowsum(x):
    return jnp.sum(x, axis=-1, keepdims=True)


def _colsum(x):
    return jnp.sum(x, axis=0, keepdims=True)


def _matmul(name, mode, pairs, m, n, kdim, tm, tn, tk, out_dtypes, epi=None, extras=()):
    tm, tn, tk = min(tm, m), min(tn, n), min(tk, kdim)
    nk = kdim // tk
    assert m % tm == 0 and n % tn == 0 and kdim % tk == 0, (name, m, n, kdim, tm, tn, tk)
    in_specs, args = [], []
    for a, a_off, b, b_off in pairs:
        if mode == "nn":
            in_specs.append(pl.BlockSpec((tm, tk), lambda i, j, k, o=a_off: (i, k + o)))
            in_specs.append(pl.BlockSpec((tk, tn), lambda i, j, k, o=b_off: (k, j + o)))
            dims = NN
        elif mode == "nt":
            in_specs.append(pl.BlockSpec((tm, tk), lambda i, j, k, o=a_off: (i, k + o)))
            in_specs.append(pl.BlockSpec((tn, tk), lambda i, j, k, o=b_off: (j, k + o)))
            dims = NT
        else:
            in_specs.append(pl.BlockSpec((tk, tm), lambda i, j, k, o=a_off: (k, i + o)))
            in_specs.append(pl.BlockSpec((tk, tn), lambda i, j, k, o=b_off: (k, j + o)))
            dims = TN
        args += [a, b]
    for e, e_off in extras:
        in_specs.append(pl.BlockSpec((tm, tn), lambda i, j, k, o=e_off: (i, j + o)))
        args.append(e)
    npair, nex, nout = len(pairs), len(extras), len(out_dtypes)

    def body(*refs):
        prefs = refs[: 2 * npair]
        erefs = refs[2 * npair: 2 * npair + nex]
        orefs = refs[2 * npair + nex: 2 * npair + nex + nout]

        def finish(res):
            outs = (res,) if epi is None else epi(res, *[e[...] for e in erefs])
            for o, r in zip(orefs, outs):
                o[...] = r.astype(o.dtype)

        s = _bdot(prefs[0][...], prefs[1][...], dims)
        for p in range(1, npair):
            s = s + _bdot(prefs[2 * p][...], prefs[2 * p + 1][...], dims)
        if nk == 1:
            finish(s)
            return
        acc = refs[-1]
        k = pl.program_id(2)

        @pl.when(k == 0)
        def _():
            acc[...] = s

        @pl.when(k > 0)
        def _():
            acc[...] += s

        @pl.when(k == nk - 1)
        def _():
            finish(acc[...])

    out_shape = tuple(SDS((m, n), dt) for dt in out_dtypes)
    out_specs = tuple(pl.BlockSpec((tm, tn), lambda i, j, k: (i, j)) for _ in out_dtypes)
    res = pl.pallas_call(
        body, grid=(m // tm, n // tn, nk), in_specs=in_specs, out_specs=out_specs, out_shape=out_shape,
        scratch_shapes=[pltpu.VMEM((tm, tn), F32)] if nk > 1 else [], name=name,
        compiler_params=_params(("parallel", "parallel", "arbitrary")),
    )(*args)
    return res if nout > 1 else res[0]


def _rmsnorm_fwd(name, x, w):
    t, d = x.shape
    tm = min(512, t)

    def body(x_ref, w_ref, h_ref):
        xv = x_ref[...]
        r = lax.rsqrt(jnp.mean(xv * xv, axis=-1, keepdims=True) + EPS)
        h_ref[...] = (xv * r * w_ref[...]).astype(BF16)

    return pl.pallas_call(
        body, grid=(t // tm,),
        in_specs=[pl.BlockSpec((tm, d), lambda i: (i, 0)), pl.BlockSpec((1, d), lambda i: (0, 0))],
        out_specs=pl.BlockSpec((tm, d), lambda i: (i, 0)), out_shape=SDS((t, d), BF16), name=name,
        compiler_params=_params(("parallel",)),
    )(x, w.reshape(1, d))


def _rmsnorm_bwd(name, x, w, dh, dres):
    t, d = x.shape
    tm = min(512, t)

    def body(x_ref, w_ref, dh_ref, dres_ref, dx_ref, dxb_ref, dw_ref):
        xv = x_ref[...]
        r = lax.rsqrt(jnp.mean(xv * xv, axis=-1, keepdims=True) + EPS)
        xh = xv * r
        dhv = dh_ref[...].astype(F32)
        dxh = dhv * w_ref[...]
        dx = r * (dxh - xh * jnp.mean(dxh * xh, axis=-1, keepdims=True)) + dres_ref[...]
        dx_ref[...] = dx
        dxb_ref[...] = dx.astype(BF16)

        @pl.when(pl.program_id(0) == 0)
        def _():
            dw_ref[...] = jnp.zeros_like(dw_ref)

        dw_ref[...] += _colsum(dhv * xh)

    row = pl.BlockSpec((tm, d), lambda i: (i, 0))
    vec = pl.BlockSpec((1, d), lambda i: (0, 0))
    return pl.pallas_call(
        body, grid=(t // tm,), in_specs=[row, vec, row, row], out_specs=(row, row, vec),
        out_shape=(SDS((t, d), F32), SDS((t, d), BF16), SDS((1, d), F32)), name=name,
        compiler_params=_params(("arbitrary",)),
    )(x, w.reshape(1, d), dh, dres)


def _loss_head(name, x, w, tgt):
    t, d = x.shape
    tm = min(512, t)

    def body(x_ref, w_ref, t_ref, loss_ref, dx_ref, dxb_ref, dw_ref):
        xv = x_ref[...]
        wv = w_ref[...]
        r = lax.rsqrt(jnp.mean(xv * xv, axis=-1, keepdims=True) + EPS)
        xh = xv * r
        e = xh * wv - t_ref[...]
        dy = e * (1.0 / d)
        dxh = dy * wv
        dx = r * (dxh - xh * jnp.mean(dxh * xh, axis=-1, keepdims=True))
        dx_ref[...] = dx
        dxb_ref[...] = dx.astype(BF16)

        @pl.when(pl.program_id(0) == 0)
        def _():
            dw_ref[...] = jnp.zeros_like(dw_ref)
            loss_ref[...] = jnp.zeros_like(loss_ref)

        dw_ref[...] += _colsum(dy * xh)
        loss_ref[...] += 0.5 * jnp.sum(jnp.mean(e * e, axis=-1, keepdims=True), axis=0, keepdims=True)

    row = pl.BlockSpec((tm, d), lambda i: (i, 0))
    vec = pl.BlockSpec((1, d), lambda i: (0, 0))
    return pl.pallas_call(
        body, grid=(t // tm,), in_specs=[row, vec, row],
        out_specs=(pl.BlockSpec((1, 1), lambda i: (0, 0)), row, row, vec),
        out_shape=(SDS((1, 1), F32), SDS((t, d), F32), SDS((t, d), BF16), SDS((1, d), F32)), name=name,
        compiler_params=_params(("arbitrary",)),
    )(x, w.reshape(1, d), tgt)


def _shift_down(u, s, row):
    return jnp.where(row >= s, pltpu.roll(u, shift=s, axis=0), 0.0)


def _conv_fwd(name, src, col0, w, b):
    t = src.shape[0]
    c = w.shape[1]
    tc = 256
    assert c % tc == 0 and col0 % tc == 0

    def body(u_ref, w_ref, b_ref, o_ref):
        u = u_ref[...]
        wv = w_ref[...]
        row = lax.broadcasted_iota(jnp.int32, u.shape, 0)
        pre = b_ref[...] + wv[3:4, :] * u
        for s in range(1, CONV_K):
            pre = pre + wv[3 - s: 4 - s, :] * _shift_down(u, s, row)
        o_ref[...] = pre * _sigmoid(pre)

    return pl.pallas_call(
        body, grid=(c // tc,),
        in_specs=[pl.BlockSpec((t, tc), lambda j: (0, j + col0 // tc)), pl.BlockSpec((CONV_K, tc), lambda j: (0, j)),
                  pl.BlockSpec((1, tc), lambda j: (0, j))],
        out_specs=pl.BlockSpec((t, tc), lambda j: (0, j)), out_shape=SDS((t, c), F32), name=name,
        compiler_params=_params(("parallel",)),
    )(src, w, b)


def _conv_bwd(name, src, col0, w, b, dact):
    t = src.shape[0]
    c = w.shape[1]
    tc = 128

    def body(u_ref, w_ref, b_ref, da_ref, du_ref, dw_ref, db_ref):
        u = u_ref[...]
        wv = w_ref[...]
        row = lax.broadcasted_iota(jnp.int32, u.shape, 0)
        shifted = [u] + [_shift_down(u, s, row) for s in range(1, CONV_K)]
        pre = b_ref[...] + wv[3:4, :] * u
        for s in range(1, CONV_K):
            pre = pre + wv[3 - s: 4 - s, :] * shifted[s]
        sg = _sigmoid(pre)
        dpre = da_ref[...] * (sg * (1.0 + pre * (1.0 - sg)))
        du = wv[3:4, :] * dpre
        for s in range(1, CONV_K):
            du = du + wv[3 - s: 4 - s, :] * jnp.where(row < t - s, pltpu.roll(dpre, shift=t - s, axis=0), 0.0)
        du_ref[...] = du.astype(BF16)
        for s in range(CONV_K):
            dw_ref[3 - s: 4 - s, :] = _colsum(dpre * shifted[s])
        db_ref[...] = _colsum(dpre)

    return pl.pallas_call(
        body, grid=(c // tc,),
        in_specs=[pl.BlockSpec((t, tc), lambda j: (0, j + col0 // tc)), pl.BlockSpec((CONV_K, tc), lambda j: (0, j)),
                  pl.BlockSpec((1, tc), lambda j: (0, j)), pl.BlockSpec((t, tc), lambda j: (0, j))],
        out_specs=(pl.BlockSpec((t, tc), lambda j: (0, j)), pl.BlockSpec((CONV_K, tc), lambda j: (0, j)),
                   pl.BlockSpec((1, tc), lambda j: (0, j))),
        out_shape=(SDS((t, c), BF16), SDS((CONV_K, c), F32), SDS((1, c), F32)), name=name,
        compiler_params=_params(("parallel",)),
    )(src, w, b, dact)


def _tri(q):
    ii = lax.broadcasted_iota(jnp.int32, (q, q), 0)
    jj = lax.broadcasted_iota(jnp.int32, (q, q), 1)
    return ii, jj


def _dot01(x, r01, dims, terms=3):
    out, rem = None, x
    for i in range(terms):
        hi = rem.astype(BF16)
        d = lax.dot_general(hi, r01, (dims, ((), ())), preferred_element_type=F32)
        out = d if out is None else out + d
        if i + 1 < terms:
            rem = rem - hi.astype(F32)
    return out


def _ssd_common(act, sm, dtb, arow, rmat):
    q = CHUNK
    ii, jj = _tri(q)
    lane = lax.broadcasted_iota(jnp.int32, (q, 128), 1)
    m16 = lane < SSM_HEADS
    dt = jnp.where(m16, _softplus(sm + dtb), 0.0)
    a = dt * arow
    tril = (ii >= jj).astype(F32)
    triu = (ii <= jj).astype(F32)
    acum = _hdot(tril, a)
    acum_r = _hdot(a.T, triu)
    dtx = _dot01(dt, rmat, NN)
    acx = _dot01(acum, rmat, NN)
    ex = jnp.exp(acx)
    alx = acx[q - 1: q, :]
    dex = jnp.exp(alx - acx)
    xs = act[:, :1024]
    return dict(ii=ii, jj=jj, m16=m16, dt=dt, a=a, triu=triu, acum=acum, acum_r=acum_r, dtx=dtx, ex=ex, dex=dex,
                elx=jnp.exp(alx), xs=xs, x=xs * dtx)


def _ssd_lmat(cm, h):
    return jnp.where(cm["ii"] >= cm["jj"], jnp.exp(cm["acum"][:, h: h + 1] - cm["acum_r"][h: h + 1, :]), 0.0)


def _ssd_fwd(name, act, proj, dtb, arow, dxrow, nw, rmat):
    t = act.shape[0]
    q = CHUNK
    nc = t // q
    hg = SSM_HEADS // SSM_GROUPS
    gw = hg * SSM_P

    def body(act_ref, z_ref, sm_ref, dtb_ref, arow_ref, dx_ref, nw_ref, r_ref, y_ref, ys_ref, st_ref, s_scr, yd_scr):
        @pl.when(pl.program_id(0) == 0)
        def _():
            s_scr[...] = jnp.zeros_like(s_scr)

        s_all = s_scr[...]
        st_ref[0] = s_all
        actv = act_ref[...]
        cm = _ssd_common(actv, sm_ref[...], dtb_ref[...], arow_ref[...], r_ref[...])
        x = cm["x"]
        xd = x * cm["dex"]
        yoffs, snew = [], []
        for g in range(SSM_GROUPS):
            bg = actv[:, 1024 + g * SSM_N: 1024 + (g + 1) * SSM_N]
            cg = actv[:, 1280 + g * SSM_N: 1280 + (g + 1) * SSM_N]
            sg = s_all[:, g * gw: (g + 1) * gw]
            cb = _bdot(cg, bg, NT)
            yoffs.append(_bdot(cg, sg, NN))
            snew.append(_bdot(bg, xd[:, g * gw: (g + 1) * gw], TN))
            for r in range(hg):
                h = g * hg + r
                mm = cb * _ssd_lmat(cm, h)
                yd_scr[:, h * SSM_P: (h + 1) * SSM_P] = _bdot(mm, x[:, h * SSM_P: (h + 1) * SSM_P], NN)
        s_scr[...] = s_all * cm["elx"] + jnp.concatenate(snew, axis=1)
        ysc = yd_scr[...] + jnp.concatenate(yoffs, axis=1) * cm["ex"]
        ys_ref[...] = ysc
        zv = z_ref[...]
        yg = (ysc + dx_ref[...] * cm["xs"]) * (zv * _sigmoid(zv))
        nwv = nw_ref[...]
        for g in range(SSM_GROUPS):
            sl = yg[:, g * gw: (g + 1) * gw]
            rr = lax.rsqrt(jnp.mean(sl * sl, axis=-1, keepdims=True) + EPS)
            y_ref[:, g * gw: (g + 1) * gw] = (sl * rr * nwv[:, g * gw: (g + 1) * gw]).astype(BF16)

    vec128 = pl.BlockSpec((1, 128), lambda c: (0, 0))
    vec1k = pl.BlockSpec((1, 1024), lambda c: (0, 0))
    return pl.pallas_call(
        body, grid=(nc,),
        in_specs=[pl.BlockSpec((q, SSM_CONV), lambda c: (c, 0)), pl.BlockSpec((q, 1024), lambda c: (c, Z_OFF // 1024)),
                  pl.BlockSpec((q, 128), lambda c: (c, SM_OFF // 128)), vec128, vec128, vec1k, vec1k,
                  pl.BlockSpec((128, 1024), lambda c: (0, 0))],
        out_specs=(pl.BlockSpec((q, 1024), lambda c: (c, 0)), pl.BlockSpec((q, 1024), lambda c: (c, 0)),
                   pl.BlockSpec((1, 128, 1024), lambda c: (c, 0, 0))),
        out_shape=(SDS((t, 1024), BF16), SDS((t, 1024), F32), SDS((nc, 128, 1024), F32)),
        scratch_shapes=[pltpu.VMEM((128, 1024), F32), pltpu.VMEM((q, 1024), F32)], name=name,
        compiler_params=_params(("arbitrary",)),
    )(act, proj, proj, dtb, arow, dxrow, nw, rmat)


def _ssd_bwd(name, act, proj, dtb, arow, dxrow, nw, rmat, ysc, states, dy):
    t = act.shape[0]
    q = CHUNK
    nc = t // q
    hg = SSM_HEADS // SSM_GROUPS
    gw = hg * SSM_P

    def body(act_ref, z_ref, sm_ref, dtb_ref, arow_ref, dx_ref, nw_ref, r_ref, ys_ref, st_ref, dy_ref,
             dact_ref, dz_ref, dsm_ref, dnw_ref, dd_ref, dal_ref, ddtb_ref, ds_scr, dxd_scr):
        @pl.when(pl.program_id(0) == 0)
        def _():
            ds_scr[...] = jnp.zeros_like(ds_scr)
            dnw_ref[...] = jnp.zeros_like(dnw_ref)
            dd_ref[...] = jnp.zeros_like(dd_ref)
            dal_ref[...] = jnp.zeros_like(dal_ref)
            ddtb_ref[...] = jnp.zeros_like(ddtb_ref)

        actv = act_ref[...]
        smv = sm_ref[...]
        rmat_v = r_ref[...]
        cm = _ssd_common(actv, smv, dtb_ref[...], arow_ref[...], rmat_v)
        ii, jj = cm["ii"], cm["jj"]
        x, xs = cm["x"], cm["xs"]
        s_all = st_ref[0]
        dsn = ds_scr[...]
        ysv = ys_ref[...]
        dxr = dx_ref[...]
        y = ysv + dxr * xs
        zv = z_ref[...]
        sz = _sigmoid(zv)
        silz = zv * sz
        yg = y * silz
        dout = dy_ref[...]
        nwv = nw_ref[...]
        dyn = dout * nwv
        yn_parts, dyg_parts = [], []
        for g in range(SSM_GROUPS):
            sl = yg[:, g * gw: (g + 1) * gw]
            rr = lax.rsqrt(jnp.mean(sl * sl, axis=-1, keepdims=True) + EPS)
            yn = sl * rr
            dn = dyn[:, g * gw: (g + 1) * gw]
            yn_parts.append(yn)
            dyg_parts.append(rr * (dn - yn * jnp.mean(dn * yn, axis=-1, keepdims=True)))
        dnw_ref[...] += _colsum(dout * jnp.concatenate(yn_parts, axis=1))
        dyg = jnp.concatenate(dyg_parts, axis=1)
        dyv = dyg * silz
        dz_ref[...] = (dyg * y * (sz * (1.0 + zv * (1.0 - sz)))).astype(BF16)
        dd_ref[...] += _dot01(_colsum(dyv * xs), rmat_v, NT)
        dxs = dyv * dxr
        dcs = dyv * cm["ex"]
        xd = x * cm["dex"]
        dxst_parts, ds_parts, db_parts, dc_parts, yoff_parts, wcol_rows = [], [], [], [], [], []
        lane128 = lax.broadcasted_iota(jnp.int32, (q, 128), 1)
        wrow = jnp.zeros((q, 128), F32)
        for g in range(SSM_GROUPS):
            bg = actv[:, 1024 + g * SSM_N: 1024 + (g + 1) * SSM_N]
            cg = actv[:, 1280 + g * SSM_N: 1280 + (g + 1) * SSM_N]
            sg = s_all[:, g * gw: (g + 1) * gw]
            dsng = dsn[:, g * gw: (g + 1) * gw]
            dcsg = dcs[:, g * gw: (g + 1) * gw]
            dcg = _bdot(dcsg, sg, NT)
            yoff_parts.append(_bdot(cg, sg, NN))
            ds_parts.append(_bdot(cg, dcsg, TN))
            dxst_parts.append(_bdot(bg, dsng, NN))
            dbg = _bdot(xd[:, g * gw: (g + 1) * gw], dsng, NT)
            cb = _bdot(cg, bg, NT)
            dcb = jnp.zeros((q, q), F32)
            for r in range(hg):
                h = g * hg + r
                lm = _ssd_lmat(cm, h)
                mm = cb * lm
                dyh = dyv[:, h * SSM_P: (h + 1) * SSM_P]
                dm = jnp.where(ii >= jj, _bdot(dyh, x[:, h * SSM_P: (h + 1) * SSM_P], NT), 0.0)
                dxd_scr[:, h * SSM_P: (h + 1) * SSM_P] = _bdot(mm, dyh, TN)
                dcb = dcb + dm * lm
                wm = dm * mm
                wrow = wrow + jnp.where(lane128 == h, _rowsum(wm), 0.0)
                wcol_rows.append(_colsum(wm))
            dc_parts.append(dcg + _bdot(dcb, bg, NN))
            db_parts.append(dbg + _bdot(dcb, cg, TN))
        dxst = jnp.concatenate(dxst_parts, axis=1) * cm["dex"]
        dx = dxd_scr[...] + dxst
        ds_scr[...] = jnp.concatenate(ds_parts, axis=1) + dsn * cm["elx"]
        wcol = jnp.concatenate(wcol_rows + [jnp.zeros((128 - SSM_HEADS, q), F32)], axis=0).T
        yoff = jnp.concatenate(yoff_parts, axis=1) * cm["ex"]
        xdxst = x * dxst
        dac = wrow - wcol + _dot01(dyv * yoff - xdxst, rmat_v, NT)
        last = _dot01(_colsum(dsn * s_all) * cm["elx"] + _colsum(xdxst), rmat_v, NT)
        rowq = lax.broadcasted_iota(jnp.int32, (q, 128), 0)
        dac = dac + jnp.where(rowq == q - 1, last, 0.0)
        da = _hdot(cm["triu"], dac)
        arow_v = arow_ref[...]
        ddt = da * arow_v + _dot01(dx * xs, rmat_v, NT)
        dxs = dxs + dx * cm["dtx"]
        dal_ref[...] += _colsum(da * cm["a"])
        ddtraw = jnp.where(cm["m16"], ddt * _sigmoid(smv + dtb_ref[...]), 0.0)
        ddtb_ref[...] += _colsum(ddtraw)
        dsm_ref[...] = ddtraw.astype(BF16)
        dact_ref[:, :1024] = dxs
        for g in range(SSM_GROUPS):
            dact_ref[:, 1024 + g * SSM_N: 1024 + (g + 1) * SSM_N] = db_parts[g]
            dact_ref[:, 1280 + g * SSM_N: 1280 + (g + 1) * SSM_N] = dc_parts[g]

    rev = lambda c: nc - 1 - c
    vec128 = pl.BlockSpec((1, 128), lambda c: (0, 0))
    vec1k = pl.BlockSpec((1, 1024), lambda c: (0, 0))
    return pl.pallas_call(
        body, grid=(nc,),
        in_specs=[pl.BlockSpec((q, SSM_CONV), lambda c: (rev(c), 0)),
                  pl.BlockSpec((q, 1024), lambda c: (rev(c), Z_OFF // 1024)),
                  pl.BlockSpec((q, 128), lambda c: (rev(c), SM_OFF // 128)), vec128, vec128, vec1k, vec1k,
                  pl.BlockSpec((128, 1024), lambda c: (0, 0)),
                  pl.BlockSpec((q, 1024), lambda c: (rev(c), 0)), pl.BlockSpec((1, 128, 1024), lambda c: (rev(c), 0, 0)),
                  pl.BlockSpec((q, 1024), lambda c: (rev(c), 0))],
        out_specs=(pl.BlockSpec((q, SSM_CONV), lambda c: (rev(c), 0)), pl.BlockSpec((q, 1024), lambda c: (rev(c), 0)),
                   pl.BlockSpec((q, 128), lambda c: (rev(c), 0)), vec1k, vec128, vec128, vec128),
        out_shape=(SDS((t, SSM_CONV), F32), SDS((t, 1024), BF16), SDS((t, 128), BF16), SDS((1, 1024), F32),
                   SDS((1, 128), F32), SDS((1, 128), F32), SDS((1, 128), F32)),
        scratch_shapes=[pltpu.VMEM((128, 1024), F32), pltpu.VMEM((q, 1024), F32)], name=name,
        compiler_params=_params(("arbitrary",)),
    )(act, proj, proj, dtb, arow, dxrow, nw, rmat, ysc, states, dy)


def _split(a):
    hi = a.astype(BF16)
    return hi, (a - hi.astype(F32)).astype(BF16)


def _dot3(a, b, dims=NN):
    (ah, al), (bh, bl) = a, b

    def d(x, y):
        return lax.dot_general(x, y, (dims, ((), ())), preferred_element_type=F32)

    return d(ah, bh) + (d(ah, bl) + d(al, bh))


def _tri_inverses(amats, ii, jj):
    eye = jnp.where(ii == jj, 1.0, 0.0)
    tms = [eye - a for a in amats]
    sp = [_split(a) for a in amats]
    for _ in range(5):
        sp = [_split(_dot3(s, s)) for s in sp]
        tms = [t + _dot3(_split(t), s) for t, s in zip(tms, sp)]
    return tms


def _gdn_common(sm, gb, garow):
    q = CHUNK
    ii, jj = _tri(q)
    lane = lax.broadcasted_iota(jnp.int32, (q, 128), 1)
    ma = (lane >= LANE_A) & (lane < LANE_A + GDN_HEADS)
    spre = sm + gb
    g = jnp.where(ma, garow * _softplus(spre), 0.0)
    beta = _sigmoid(sm)
    tril = (ii >= jj).astype(F32)
    triu = (ii <= jj).astype(F32)
    gc = _hdot(tril, g)
    gc_r = _hdot(g.T, triu)
    return dict(ii=ii, jj=jj, lane=lane, ma=ma, spre=spre, g=g, beta=beta, triu=triu, gc=gc, gc_r=gc_r)


def _each(f, *lists):
    return [f(*xs) for xs in zip(*lists)]


GDN_SCALE = GDN_DK ** -0.5


def _gdn_heads(cm, actv, states):
    q = CHUNK
    ii, jj = cm["ii"], cm["jj"]
    heads = range(GDN_HEADS)
    qr = [actv[:, h * 128: (h + 1) * 128] for h in heads]
    kr = [actv[:, 1024 + h * 128: 1024 + (h + 1) * 128] for h in heads]
    v = [actv[:, 2048 + h * 128: 2048 + (h + 1) * 128] for h in heads]
    rq = _each(lambda x: lax.rsqrt(_rowsum(x * x) + EPS), qr)
    rk = _each(lambda x: lax.rsqrt(_rowsum(x * x) + EPS), kr)
    qn = _each(lambda x, r: x * r * GDN_SCALE, qr, rq)
    kn = _each(lambda x, r: x * r, kr, rk)
    gcc = [cm["gc"][:, LANE_A + h: LANE_A + h + 1] for h in heads]
    gcr = [cm["gc_r"][LANE_A + h: LANE_A + h + 1, :] for h in heads]
    bcol = [cm["beta"][:, LANE_B + h: LANE_B + h + 1] for h in heads]
    dm = _each(lambda c, r: jnp.where(ii >= jj, jnp.exp(c - r), 0.0), gcc, gcr)
    ak = _each(lambda k, d: jnp.where(ii > jj, _bdot(k, k, NT) * d, 0.0), kn, dm)
    tm = _tri_inverses(_each(lambda a, b: a * b, ak, bcol), ii, jj)
    eg = _each(jnp.exp, gcc)
    gl = [c[q - 1: q, :] for c in gcc]
    rm = _each(lambda vv, k, b, e: jnp.concatenate([vv * b, k * (b * e)], axis=1), v, kn, bcol, eg)
    tt = _each(lambda t, r: _dot3(_split(t), _split(r)), tm, rm)
    w = [t[:, 128:] for t in tt]
    vnew = _each(lambda t, ww, s: t[:, :128] - _bdot(ww, s, NN), tt, w, states)
    qkm = _each(lambda a, k, d: jnp.where(ii >= jj, _bdot(a, k, NT) * d, 0.0), qn, kn, dm)
    return dict(qr=qr, v=v, rq=rq, rk=rk, qn=qn, kn=kn, gcc=gcc, bcol=bcol, dm=dm, ak=ak, tm=tm, eg=eg, gl=gl,
                egl=_each(jnp.exp, gl), ed=_each(lambda g, c: jnp.exp(g - c), gl, gcc), tt=tt, w=w, vnew=vnew, qkm=qkm,
                qg=_each(lambda a, e: a * e, qn, eg))


def _gdn_fwd(name, act, proj, gb, garow, gnw):
    t = act.shape[0]
    q = CHUNK
    nc = t // q

    def body(act_ref, gz_ref, sm_ref, gb_ref, ga_ref, nw_ref, y_ref, o_ref, st_ref, s_scr):
        @pl.when(pl.program_id(0) == 0)
        def _():
            s_scr[...] = jnp.zeros_like(s_scr)

        st_ref[0] = s_scr[...]
        actv = act_ref[...]
        cm = _gdn_common(sm_ref[...], gb_ref[...], ga_ref[...])
        nwv = nw_ref[...]
        gzv = gz_ref[...]
        states = [s_scr[h * 128: (h + 1) * 128, :] for h in range(GDN_HEADS)]
        hd = _gdn_heads(cm, actv, states)
        outs = _each(lambda a, s, m, vn: _bdot(a, s, NN) + _bdot(m, vn, NN), hd["qg"], states, hd["qkm"], hd["vnew"])
        snew = _each(lambda s, e, k, d, vn: s * e + _bdot(k * d, vn, TN), states, hd["egl"], hd["kn"], hd["ed"], hd["vnew"])
        for h in range(GDN_HEADS):
            o = outs[h]
            s_scr[h * 128: (h + 1) * 128, :] = snew[h]
            o_ref[:, h * 128: (h + 1) * 128] = o
            rr = lax.rsqrt(jnp.mean(o * o, axis=-1, keepdims=True) + EPS)
            gz = gzv[:, h * 128: (h + 1) * 128]
            y_ref[:, h * 128: (h + 1) * 128] = (o * rr * nwv * (gz * _sigmoid(gz))).astype(BF16)

    vec128 = pl.BlockSpec((1, 128), lambda c: (0, 0))
    return pl.pallas_call(
        body, grid=(nc,),
        in_specs=[pl.BlockSpec((q, GDN_QKV), lambda c: (c, 0)), pl.BlockSpec((q, 1024), lambda c: (c, GZ_OFF // 1024)),
                  pl.BlockSpec((q, 128), lambda c: (c, SM_OFF // 128)), vec128, vec128, vec128],
        out_specs=(pl.BlockSpec((q, 1024), lambda c: (c, 0)), pl.BlockSpec((q, 1024), lambda c: (c, 0)),
                   pl.BlockSpec((1, 1024, 128), lambda c: (c, 0, 0))),
        out_shape=(SDS((t, 1024), BF16), SDS((t, 1024), F32), SDS((nc, 1024, 128), F32)),
        scratch_shapes=[pltpu.VMEM((1024, 128), F32)], name=name, compiler_params=_params(("arbitrary",)),
    )(act, proj, proj, gb, garow, gnw)


def _gdn_bwd(name, act, proj, gb, garow, gnw, oraw, states, dy):
    t = act.shape[0]
    q = CHUNK
    nc = t // q

    def body(act_ref, gz_ref, sm_ref, gb_ref, ga_ref, nw_ref, o_ref, st_ref, dy_ref,
             dact_ref, dgz_ref, dsm_ref, dnw_ref, dal_ref, dgb_ref, ds_scr):
        @pl.when(pl.program_id(0) == 0)
        def _():
            ds_scr[...] = jnp.zeros_like(ds_scr)
            dnw_ref[...] = jnp.zeros_like(dnw_ref)
            dal_ref[...] = jnp.zeros_like(dal_ref)
            dgb_ref[...] = jnp.zeros_like(dgb_ref)

        actv = act_ref[...]
        smv = sm_ref[...]
        garow_v = ga_ref[...]
        cm = _gdn_common(smv, gb_ref[...], garow_v)
        ii, jj, lane = cm["ii"], cm["jj"], cm["lane"]
        nwv = nw_ref[...]
        rowq = lax.broadcasted_iota(jnp.int32, (q, 1), 0)
        dgc_all = jnp.zeros((q, 128), F32)
        dbeta_all = jnp.zeros((q, 128), F32)
        dnw_acc = jnp.zeros((1, 128), F32)
        heads = range(GDN_HEADS)
        sts = [st_ref[0, h * 128: (h + 1) * 128, :] for h in heads]
        dsn = [ds_scr[h * 128: (h + 1) * 128, :] for h in heads]
        ov, gzv, dyv = o_ref[...], gz_ref[...], dy_ref[...]
        hd = _gdn_heads(cm, actv, sts)
        qn, kn, v, eg, ed, egl, bcol = hd["qn"], hd["kn"], hd["v"], hd["eg"], hd["ed"], hd["egl"], hd["bcol"]
        vnew, qkm, qg, w, tt, dm, ak = hd["vnew"], hd["qkm"], hd["qg"], hd["w"], hd["tt"], hd["dm"], hd["ak"]
        do = []
        for h in heads:
            hs = slice(h * 128, (h + 1) * 128)
            o = ov[:, hs]
            rr = lax.rsqrt(jnp.mean(o * o, axis=-1, keepdims=True) + EPS)
            on = o * rr
            gz = gzv[:, hs]
            sz = _sigmoid(gz)
            silz = gz * sz
            dyh = dyv[:, hs]
            dnw_acc = dnw_acc + _colsum(dyh * on * silz)
            dgz_ref[:, hs] = (dyh * on * nwv * (sz * (1.0 + gz * (1.0 - sz)))).astype(BF16)
            don = dyh * nwv * silz
            do.append(rr * (don - on * jnp.mean(don * on, axis=-1, keepdims=True)))
        kd = _each(lambda k, e: k * e, kn, ed)
        dkd = _each(lambda vn, d: _bdot(vn, d, NT), vnew, dsn)
        dvnew_a = _each(lambda k, d: _bdot(k, d, NN), kd, dsn)
        ded = _each(lambda a, b: _rowsum(a * b), dkd, kd)
        dgl = _each(lambda d, s, e, de: jnp.sum(_rowsum(d * s), axis=0, keepdims=True) * e + _colsum(de), dsn, sts, egl, ded)
        dqg = _each(lambda d, s: _bdot(d, s, NT), do, sts)
        ds1 = _each(lambda a, d: _bdot(a, d, TN), qg, do)
        dqk = _each(lambda d, vn: jnp.where(ii >= jj, _bdot(d, vn, NT), 0.0), do, vnew)
        dvnew = _each(lambda a, m, d: a + _bdot(m, d, TN), dvnew_a, qkm, do)
        pq = _each(lambda a, b: a * b, dqk, dm)
        dq = _each(lambda a, e, p, k: a * e + _bdot(p, k, NN), dqg, eg, pq, kn)
        dk1 = _each(lambda a, e, p, x: a * e + _bdot(p, x, TN), dkd, ed, pq, qn)
        w1 = _each(lambda a, b: a * b, dqk, qkm)
        dw = _each(lambda d, s: -_bdot(d, s, NT), dvnew, sts)
        ds2 = _each(lambda a, d: _bdot(a, d, TN), w, dvnew)
        dr = _each(lambda t, a, b: _dot3(_split(t), _split(jnp.concatenate([a, b], axis=1)), TN), hd["tm"], dvnew, dw)
        da = _each(lambda r, t: jnp.where(ii > jj, -_dot3(_split(r), _split(t), NT), 0.0), dr, tt)
        sk = _each(lambda r, k: _rowsum(r[:, 128:] * k), dr, kn)
        pk = _each(lambda a, d, b: a * d * b, da, dm, bcol)
        dk = _each(lambda a, r, b, e, p, k: a + r[:, 128:] * (b * e) + _bdot(p, k, NN) + _bdot(p, k, TN),
                   dk1, dr, bcol, eg, pk, kn)
        w2 = _each(lambda a, k, b: a * (k * b), da, ak, bcol)
        for h in heads:
            hs = slice(h * 128, (h + 1) * 128)
            dgc = (-ded[h] + _rowsum(dqg[h] * qg[h]) + _rowsum(w1[h]) - _rowsum(w1[h].T) + sk[h] * bcol[h] * eg[h]
                   + _rowsum(w2[h]) - _rowsum(w2[h].T) + jnp.where(rowq == q - 1, dgl[h], 0.0))
            dbeta = _rowsum(dr[h][:, :128] * v[h]) + sk[h] * eg[h] + _rowsum(da[h] * ak[h])
            qhat = hd["qr"][h] * hd["rq"][h]
            dqhat = dq[h] * GDN_SCALE
            dact_ref[:, hs] = hd["rq"][h] * (dqhat - qhat * _rowsum(dqhat * qhat))
            dact_ref[:, 1024 + h * 128: 1024 + (h + 1) * 128] = hd["rk"][h] * (dk[h] - kn[h] * _rowsum(dk[h] * kn[h]))
            dact_ref[:, 2048 + h * 128: 2048 + (h + 1) * 128] = dr[h][:, :128] * bcol[h]
            dgc_all = dgc_all + jnp.where(lane == LANE_A + h, dgc, 0.0)
            dbeta_all = dbeta_all + jnp.where(lane == LANE_B + h, dbeta, 0.0)
            ds_scr[hs, :] = dsn[h] * egl[h] + ds1[h] - ds2[h]
        dnw_ref[...] += dnw_acc
        dg = _hdot(cm["triu"], dgc_all)
        da_raw = jnp.where(cm["ma"], dg * garow_v * _sigmoid(cm["spre"]), 0.0)
        dal_ref[...] += _colsum(dg * cm["g"])
        dgb_ref[...] += _colsum(da_raw)
        beta = cm["beta"]
        dsm_ref[...] = (da_raw + dbeta_all * beta * (1.0 - beta)).astype(BF16)

    rev = lambda c: nc - 1 - c
    vec128 = pl.BlockSpec((1, 128), lambda c: (0, 0))
    return pl.pallas_call(
        body, grid=(nc,),
        in_specs=[pl.BlockSpec((q, GDN_QKV), lambda c: (rev(c), 0)),
                  pl.BlockSpec((q, 1024), lambda c: (rev(c), GZ_OFF // 1024)),
                  pl.BlockSpec((q, 128), lambda c: (rev(c), SM_OFF // 128)), vec128, vec128, vec128,
                  pl.BlockSpec((q, 1024), lambda c: (rev(c), 0)), pl.BlockSpec((1, 1024, 128), lambda c: (rev(c), 0, 0)),
                  pl.BlockSpec((q, 1024), lambda c: (rev(c), 0))],
        out_specs=(pl.BlockSpec((q, GDN_QKV), lambda c: (rev(c), 0)), pl.BlockSpec((q, 1024), lambda c: (rev(c), 0)),
                   pl.BlockSpec((q, 128), lambda c: (rev(c), 0)), vec128, vec128, vec128),
        out_shape=(SDS((t, GDN_QKV), F32), SDS((t, 1024), BF16), SDS((t, 128), BF16), SDS((1, 128), F32),
                   SDS((1, 128), F32), SDS((1, 128), F32)),
        scratch_shapes=[pltpu.VMEM((1024, 128), F32)], name=name, compiler_params=_params(("arbitrary",)),
    )(act, proj, proj, gb, garow, gnw, oraw, states, dy)


def _row_tile(r):
    for cand in (512, 256, 128, 64, 32, 16, 8):
        if r % cand == 0:
            return cand
    return r


def _sum_terms(name, terms, out_dtype):
    shape = terms[0][0].shape[1:]
    c = shape[-1]
    r = 1
    for s in shape[:-1]:
        r *= s
    tr = min(_row_tile(r), 256)
    n = len(terms)

    def body(*refs):
        acc = refs[0][...].astype(F32)
        for k in range(1, n):
            acc = acc + refs[k][...].astype(F32)
        refs[n][...] = acc.astype(out_dtype)

    in_specs = [pl.BlockSpec((None, tr, c), lambda i, q=lead: (q, i, 0)) for _, lead in terms]
    args = [a.reshape(a.shape[0], r, c) for a, _ in terms]
    out = pl.pallas_call(body, grid=(r // tr,), in_specs=in_specs, out_specs=pl.BlockSpec((tr, c), lambda i: (i, 0)),
                         out_shape=SDS((r, c), out_dtype), name=name, compiler_params=_params(("parallel",)))(*args)
    return out.reshape(shape)


def _adamw(name, w, g, m, v):
    shape = w.shape
    c = shape[-1]
    w2, g2, m2, v2 = (a.reshape(-1, c) for a in (w, g, m, v))
    r = w2.shape[0]
    tr = min(_row_tile(r), 256)

    def body(w_ref, g_ref, m_ref, v_ref, d_ref, nm_ref, nv_ref):
        gv = g_ref[...]
        mn = ADAM_B1 * m_ref[...] + (1.0 - ADAM_B1) * gv
        vn = ADAM_B2 * v_ref[...] + (1.0 - ADAM_B2) * (gv * gv)
        m_hat = mn / (1.0 - ADAM_B1 ** ADAM_STEP)
        v_hat = vn / (1.0 - ADAM_B2 ** ADAM_STEP)
        d_ref[...] = -ADAM_LR * (m_hat / (jnp.sqrt(v_hat) + ADAM_EPS) + ADAM_WD * w_ref[...])
        nm_ref[...] = mn
        nv_ref[...] = vn

    spec = pl.BlockSpec((tr, c), lambda i: (i, 0))
    outs = pl.pallas_call(body, grid=(r // tr,), in_specs=[spec] * 4, out_specs=(spec,) * 3,
                          out_shape=(SDS((r, c), F32),) * 3, name=name, compiler_params=_params(("parallel",)))(w2, g2, m2, v2)
    return tuple(o.reshape(shape) for o in outs)


ANY = pl.BlockSpec(memory_space=pl.ANY)
MESH = pl.DeviceIdType.MESH


def _allgather(name, xs):
    n = len(xs)

    def body(*refs):
        x_refs, out_refs = refs[:n], refs[n: 2 * n]
        send_sems, recv_sems, local_sems = refs[2 * n:]
        x, y, cc = lax.axis_index("x"), lax.axis_index("y"), lax.axis_index("c")
        me, sibling = (x, y, cc), (x, y, 1 - cc)
        chips = [(1 - x, y), (x, 1 - y), (1 - x, 1 - y)]

        def rows(a, px, py, pc):
            return out_refs[a].at[4 * px + 2 * py + pc]

        def copy(a, k, block, to, src=None):
            return pltpu.make_async_remote_copy(
                src_ref=rows(a, *block) if src is None else src, dst_ref=rows(a, *block),
                send_sem=send_sems.at[7 * a + k], recv_sem=recv_sems.at[7 * a + k], device_id=to, device_id_type=MESH)

        mine = [pltpu.make_async_copy(x_refs[a], rows(a, *me), local_sems.at[a]) for a in range(n)]
        for cp in mine:
            cp.start()
        first = []
        for a in range(n):
            first.append(copy(a, 0, me, sibling, src=x_refs[a]))
            first += [copy(a, 1 + j, me, (*chip, cc), src=x_refs[a]) for j, chip in enumerate(chips)]
        for cp in first:
            cp.start()
        passed = []
        for j, chip in enumerate(chips):
            for a in range(n):
                copy(a, 1 + j, (*chip, cc), me).wait_recv()
                fwd = copy(a, 4 + j, (*chip, cc), sibling)
                fwd.start()
                passed.append(fwd)
        for a in range(n):
            copy(a, 0, sibling, me).wait_recv()
        for j, chip in enumerate(chips):
            for a in range(n):
                copy(a, 4 + j, (*chip, 1 - cc), me).wait_recv()
        for cp in first + passed:
            cp.wait_send()
        for cp in mine:
            cp.wait()

    return pl.pallas_call(
        body, out_shape=tuple(SDS((N_DEV,) + a.shape, a.dtype) for a in xs), in_specs=[ANY] * n, out_specs=(ANY,) * n,
        scratch_shapes=[pltpu.SemaphoreType.DMA((7 * n,)), pltpu.SemaphoreType.DMA((7 * n,)),
                        pltpu.SemaphoreType.DMA((n,))],
        name=name,
    )(*xs)


def _sibling_split(name, xs):
    n = len(xs)

    def body(*refs):
        x_refs, keep_refs, got_refs = refs[:n], refs[n: 2 * n], refs[2 * n: 3 * n]
        send_sems, recv_sems, local_sems = refs[3 * n:]
        x, y, cc = lax.axis_index("x"), lax.axis_index("y"), lax.axis_index("c")
        remote, local = [], []
        for a in range(n):
            for q in range(4):
                k = 4 * a + q
                remote.append(pltpu.make_async_remote_copy(
                    src_ref=x_refs[a].at[2 * q + 1 - cc], dst_ref=got_refs[a].at[q], send_sem=send_sems.at[k],
                    recv_sem=recv_sems.at[k], device_id=(x, y, 1 - cc), device_id_type=MESH))
                local.append(pltpu.make_async_copy(x_refs[a].at[2 * q + cc], keep_refs[a].at[q], local_sems.at[k]))
        for cp in remote + local:
            cp.start()
        for cp in remote + local:
            cp.wait()

    half = tuple(SDS((4,) + a.shape[1:], a.dtype) for a in xs)
    outs = pl.pallas_call(
        body, out_shape=half + half, in_specs=[ANY] * n, out_specs=(ANY,) * (2 * n),
        scratch_shapes=[pltpu.SemaphoreType.DMA((4 * n,)), pltpu.SemaphoreType.DMA((4 * n,)),
                        pltpu.SemaphoreType.DMA((4 * n,))],
        name=name,
    )(*xs)
    return outs[:n], outs[n:]


def _chip_exchange(name, xs):
    n = len(xs)

    def body(*refs):
        x_refs, out_refs = refs[:n], refs[n: 2 * n]
        send_sems, recv_sems, local_sems = refs[2 * n:]
        x, y, cc = lax.axis_index("x"), lax.axis_index("y"), lax.axis_index("c")
        my_chip = 2 * x + y
        chips = [(1 - x, y), (x, 1 - y), (1 - x, 1 - y)]
        mine = [pltpu.make_async_copy(x_refs[a].at[my_chip], out_refs[a].at[my_chip], local_sems.at[a]) for a in range(n)]
        for cp in mine:
            cp.start()

        def copy(a, j, dst_slot):
            px, py = chips[j]
            return pltpu.make_async_remote_copy(
                src_ref=x_refs[a].at[2 * px + py], dst_ref=out_refs[a].at[dst_slot], send_sem=send_sems.at[3 * a + j],
                recv_sem=recv_sems.at[3 * a + j], device_id=(px, py, cc), device_id_type=MESH)

        sends = [copy(a, j, my_chip) for a in range(n) for j in range(3)]
        for cp in sends:
            cp.start()
        for a in range(n):
            for j, (px, py) in enumerate(chips):
                copy(a, j, 2 * px + py).wait_recv()
        for cp in sends:
            cp.wait_send()
        for cp in mine:
            cp.wait()

    return pl.pallas_call(
        body, out_shape=tuple(SDS(a.shape, a.dtype) for a in xs), in_specs=[ANY] * n, out_specs=(ANY,) * n,
        scratch_shapes=[pltpu.SemaphoreType.DMA((3 * n,)), pltpu.SemaphoreType.DMA((3 * n,)),
                        pltpu.SemaphoreType.DMA((n,))],
        name=name,
    )(*xs)


BIG = (("w_in", 1), ("w_ffn_in", 1), ("w_proj_ssm", 0), ("w_proj_gdn", 0), ("w_out", 0), ("w_ffn_down", 0))
CONVS = (("ssm_conv_w", 1), ("gdn_conv_w", 1))


def _to_dest_major(full, axis):
    a, b = full.shape
    if axis == 0:
        return full.reshape(N_DEV, a // N_DEV, b)
    return full.reshape(a, N_DEV, b // N_DEV).transpose(1, 0, 2)


def _from_gathered(g, axis):
    if axis == 0:
        return g.reshape(-1, g.shape[2])
    return g.transpose(1, 0, 2).reshape(g.shape[1], -1)


def _permute_in_cols(w):
    sm = jnp.concatenate([w[..., O_DT: O_DT + 16], w[..., O_A: O_A + 8], w[..., O_B: O_B + 8]], axis=-1)
    pad_sm = jnp.zeros(w.shape[:-1] + (128 - 32,), w.dtype)
    pad_end = jnp.zeros(w.shape[:-1] + (PROJ_W - SM_OFF - 128,), w.dtype)
    return jnp.concatenate([w[..., O_Z: O_Z + 1024], w[..., O_GZ: O_GZ + 1024], w[..., O_G1: O_G1 + 1024],
                            w[..., O_G2: O_G2 + 1024], w[..., O_QKV: O_QKV + 3072], w[..., O_XBC: O_XBC + 1536],
                            sm, pad_sm, pad_end], axis=-1)


def _unpermute_in_cols(w):
    return jnp.concatenate([w[..., Z_OFF: Z_OFF + 1024], w[..., XBC_OFF: XBC_OFF + 1536], w[..., SM_OFF: SM_OFF + 16],
                            w[..., QKV_OFF: QKV_OFF + 3072], w[..., GZ_OFF: GZ_OFF + 1024],
                            w[..., SM_OFF + LANE_A: SM_OFF + LANE_A + 8], w[..., SM_OFF + LANE_B: SM_OFF + LANE_B + 8],
                            w[..., G1_OFF: G1_OFF + 1024], w[..., G2_OFF: G2_OFF + 1024]], axis=-1)


def _pad128(v, lane0):
    return jnp.zeros((1, 128), F32).at[0, lane0: lane0 + v.shape[0]].set(v)


def _layer_consts(p):
    return dict(
        dtb=_pad128(p["ssm_dt_bias"], 0), arow=_pad128(-jnp.exp(p["ssm_a_log"]), 0),
        dxrow=jnp.repeat(p["ssm_d"], SSM_P).reshape(1, 1024), snw=p["ssm_norm_w"].reshape(1, 1024),
        gb=_pad128(p["gdn_dt_bias"], LANE_A), garow=_pad128(-jnp.exp(p["gdn_a_log"]), LANE_A),
        gnw=p["gdn_norm_w"].reshape(1, 128), zb=jnp.zeros((1, GDN_QKV), F32), scb=p["ssm_conv_b"].reshape(1, SSM_CONV))


def _expand_matrix():
    row = lax.broadcasted_iota(jnp.int32, (128, 1024), 0)
    col = lax.broadcasted_iota(jnp.int32, (128, 1024), 1)
    return (col // SSM_P == row).astype(BF16)


def _silu_mul_epi(acc, up):
    g = acc
    return g, g * _sigmoid(g) * up.astype(F32)


def _merge_epi(acc, p1, g1, g2):
    return acc, _sigmoid(g1) * p1.astype(F32) + _sigmoid(g2) * acc


def _add_epi(acc, res):
    return (acc + res,)


def _ffn_bwd_epi(acc, gate, up):
    g = gate.astype(F32)
    sg = _sigmoid(g)
    return acc * up.astype(F32) * (sg * (1.0 + g * (1.0 - sg))), acc * (g * sg)


def _merge_bwd_epi(acc, g1, g2, p1, p2):
    s1, s2 = _sigmoid(g1), _sigmoid(g2)
    return acc * s1, acc * s2, acc * p1.astype(F32) * (s1 * (1.0 - s1)), acc * p2.astype(F32) * (s2 * (1.0 - s2))


def _layer_fwd(l, x, p, rmat):
    t = x.shape[0]
    n = f"l{l}_"
    k = _layer_consts(p)
    h = _rmsnorm_fwd(n + "norm_mix", x, p["norm_mix_w"])
    proj = _matmul(n + "in_proj", "nn", [(h, 0, p["w_in"], 0)], t, PROJ_W, 1024, 1024, 1280, 1024, (F32,))
    act_g = _conv_fwd(n + "conv_gdn", proj, QKV_OFF, p["gdn_conv_w"], k["zb"])
    act_s = _conv_fwd(n + "conv_ssm", proj, XBC_OFF, p["ssm_conv_w"], k["scb"])
    y_ssm, ysc, st_s = _ssd_fwd(n + "ssd_fwd", act_s, proj, k["dtb"], k["arow"], k["dxrow"], k["snw"], rmat)
    y_gdn, oraw, st_g = _gdn_fwd(n + "gdn_fwd", act_g, proj, k["gb"], k["garow"], k["gnw"])
    p1 = _matmul(n + "proj_ssm", "nn", [(y_ssm, 0, p["w_proj_ssm"], 0)], t, 1024, 1024, 1024, 1024, 1024, (BF16,))
    p2, merged = _matmul(n + "proj_gdn_merge", "nn", [(y_gdn, 0, p["w_proj_gdn"], 0)], t, 1024, 1024, 512, 1024, 1024,
                         (BF16, BF16), epi=_merge_epi, extras=[(p1, 0), (proj, G1_OFF // 1024), (proj, G2_OFF // 1024)])
    x1 = _matmul(n + "out_proj", "nn", [(merged, 0, p["w_out"], 0)], t, 1024, 1024, 1024, 1024, 1024, (F32,),
                 epi=_add_epi, extras=[(x, 0)])
    h2 = _rmsnorm_fwd(n + "norm_ffn", x1, p["norm_ffn_w"])
    up = _matmul(n + "ffn_up", "nn", [(h2, 0, p["w_ffn_in"], 2)], t, FFN, 1024, 512, FFN // 2, 1024, (BF16,))
    gate, act = _matmul(n + "ffn_gate", "nn", [(h2, 0, p["w_ffn_in"], 0)], t, FFN, 1024, 512, FFN // 2, 1024, (BF16, BF16),
                        epi=_silu_mul_epi, extras=[(up, 0)])
    x2 = _matmul(n + "ffn_down", "nn", [(act, 0, p["w_ffn_down"], 0)], t, 1024, FFN, 512, 1024, FFN, (F32,),
                 epi=_add_epi, extras=[(x1, 0)])
    saved = dict(x=x, h=h, proj=proj, act_g=act_g, act_s=act_s, y_ssm=y_ssm, ysc=ysc, st_s=st_s, y_gdn=y_gdn, oraw=oraw,
                 st_g=st_g, p1=p1, p2=p2, merged=merged, x1=x1, h2=h2, up=up, gate=gate, act=act, k=k)
    return x2, saved


def _layer_bwd(l, dx2, dx2b, s, p, rmat):
    t = dx2.shape[0]
    n = f"l{l}_"
    k = s["k"]
    tk_tok = 1024
    hf = FFN // 2
    g = {}
    dgate, dup = _matmul(n + "d_ffn_act", "nt", [(dx2b, 0, p["w_ffn_down"], 0)], t, FFN, 1024, 512, hf, 1024, (BF16, BF16),
                         epi=_ffn_bwd_epi, extras=[(s["gate"], 0), (s["up"], 0)])
    g["w_ffn_down"] = _matmul(n + "dw_ffn_down", "tn", [(s["act"], 0, dx2b, 0)], FFN, 1024, t, hf, 1024, tk_tok, (F32,))
    dh2 = _matmul(n + "d_ffn_in", "nt", [(dgate, 0, p["w_ffn_in"], 0), (dup, 0, p["w_ffn_in"], 2)], t, 1024, FFN,
                  512, 1024, hf, (F32,))
    dwg = _matmul(n + "dw_ffn_gate", "tn", [(s["h2"], 0, dgate, 0)], 1024, FFN, t, 1024, hf, tk_tok, (F32,))
    dwu = _matmul(n + "dw_ffn_up", "tn", [(s["h2"], 0, dup, 0)], 1024, FFN, t, 1024, hf, tk_tok, (F32,))
    g["w_ffn_in"] = jnp.concatenate([dwg, dwu], axis=1)
    dx1, dx1b, g["norm_ffn_w"] = _rmsnorm_bwd(n + "d_norm_ffn", s["x1"], p["norm_ffn_w"], dh2, dx2)
    dp1, dp2, dg1, dg2 = _matmul(
        n + "d_out_proj", "nt", [(dx1b, 0, p["w_out"], 0)], t, 1024, 1024, 512, 1024, 1024, (BF16,) * 4, epi=_merge_bwd_epi,
        extras=[(s["proj"], G1_OFF // 1024), (s["proj"], G2_OFF // 1024), (s["p1"], 0), (s["p2"], 0)])
    g["w_out"] = _matmul(n + "dw_out", "tn", [(s["merged"], 0, dx1b, 0)], 1024, 1024, t, 1024, 1024, tk_tok, (F32,))
    dy_ssm = _matmul(n + "d_proj_ssm", "nt", [(dp1, 0, p["w_proj_ssm"], 0)], t, 1024, 1024, 1024, 1024, 1024, (F32,))
    g["w_proj_ssm"] = _matmul(n + "dw_proj_ssm", "tn", [(s["y_ssm"], 0, dp1, 0)], 1024, 1024, t, 1024, 1024, tk_tok, (F32,))
    dy_gdn = _matmul(n + "d_proj_gdn", "nt", [(dp2, 0, p["w_proj_gdn"], 0)], t, 1024, 1024, 1024, 1024, 1024, (F32,))
    g["w_proj_gdn"] = _matmul(n + "dw_proj_gdn", "tn", [(s["y_gdn"], 0, dp2, 0)], 1024, 1024, t, 1024, 1024, tk_tok, (F32,))
    dact_s, dz, dsm_s, dsnw, dd, dal, ddtb = _ssd_bwd(n + "ssd_bwd", s["act_s"], s["proj"], k["dtb"], k["arow"], k["dxrow"],
                                                        k["snw"], rmat, s["ysc"], s["st_s"], dy_ssm)
    dact_g, dgz, dsm_g, dgnw, dgal, dgb = _gdn_bwd(n + "gdn_bwd", s["act_g"], s["proj"], k["gb"], k["garow"], k["gnw"],
                                                     s["oraw"], s["st_g"], dy_gdn)
    du_s, g["ssm_conv_w"], dcb = _conv_bwd(n + "d_conv_ssm", s["proj"], XBC_OFF, p["ssm_conv_w"], k["scb"], dact_s)
    du_g, g["gdn_conv_w"], _ = _conv_bwd(n + "d_conv_gdn", s["proj"], QKV_OFF, p["gdn_conv_w"], k["zb"], dact_g)
    g["ssm_conv_b"] = dcb.reshape(-1)
    g["ssm_norm_w"] = dsnw.reshape(-1)
    g["ssm_d"] = dd[0, :SSM_HEADS]
    g["ssm_a_log"] = dal[0, :SSM_HEADS]
    g["ssm_dt_bias"] = ddtb[0, :SSM_HEADS]
    g["gdn_norm_w"] = dgnw.reshape(-1)
    g["gdn_a_log"] = dgal[0, LANE_A: LANE_A + GDN_HEADS]
    g["gdn_dt_bias"] = dgb[0, LANE_A: LANE_A + GDN_HEADS]
    dproj = jnp.concatenate([dz, dgz, dg1, dg2, du_g, du_s, dsm_s + dsm_g, jnp.zeros((t, PROJ_W - SM_OFF - 128), BF16)],
                            axis=1)
    dh = _matmul(n + "d_in_proj", "nt", [(dproj, 0, p["w_in"], 0)], t, 1024, PROJ_W, 1024, 1024, 1280, (F32,))
    g["w_in"] = _matmul(n + "dw_in", "tn", [(s["h"], 0, dproj, 0)], 1024, PROJ_W, t, 1024, 1280, tk_tok, (F32,))
    dx, dxb, g["norm_mix_w"] = _rmsnorm_bwd(n + "d_norm_mix", s["x"], p["norm_mix_w"], dh, dx1)
    g["norm_mix_w"] = g["norm_mix_w"].reshape(-1)
    g["norm_ffn_w"] = g["norm_ffn_w"].reshape(-1)
    return dx, dxb, g


def _local_step(x, tgt, layers, final_norm_w):
    rmat = _expand_matrix()
    saved = []
    for l in range(DEPTH):
        x, s = _layer_fwd(l, x, layers[l], rmat)
        saved.append(s)
    loss, dx, dxb, dfw = _loss_head("loss_head", x, final_norm_w, tgt)
    grads = [None] * DEPTH
    for l in reversed(range(DEPTH)):
        dx, dxb, grads[l] = _layer_bwd(l, dx, dxb, saved[l], layers[l], rmat)
    return loss[0, 0], dx, grads, dfw.reshape(-1)


SMALL = ("norm_mix_w", "ssm_conv_b", "ssm_dt_bias", "ssm_a_log", "ssm_d", "ssm_norm_w", "gdn_a_log", "gdn_dt_bias",
         "gdn_norm_w", "norm_ffn_w")
WEIGHTS = ("norm_mix_w", "w_in", "ssm_conv_w", "ssm_conv_b", "ssm_dt_bias", "ssm_a_log", "ssm_d", "ssm_norm_w", "gdn_conv_w",
           "gdn_a_log", "gdn_dt_bias", "gdn_norm_w", "w_proj_ssm", "w_proj_gdn", "w_out", "norm_ffn_w", "w_ffn_in",
           "w_ffn_down", "final_norm_w")


def _gather_layer(l, w):
    specs = BIG + CONVS
    shards = [w[nm][l].astype(BF16) for nm, _ in BIG] + [w[nm][l] for nm, _ in CONVS]
    gathered = _allgather(f"l{l}_gather_weights", shards)
    out = {nm: _from_gathered(g, axis) for (nm, axis), g in zip(specs, gathered)}
    out["w_in"] = _permute_in_cols(out["w_in"])
    return out


def _reduce_scatter_layer(l, grads):
    specs = BIG + CONVS
    blocks = [_to_dest_major(grads[nm], axis) for nm, axis in specs]
    keep, got = _sibling_split(f"l{l}_grads_to_sibling", blocks)
    chip_sums = [_sum_terms(f"l{l}_chip_sum_{nm}", [(k[None], 0), (g[None], 0)], BF16)
                 for (nm, _), k, g in zip(specs, keep, got)]
    exchanged = _chip_exchange(f"l{l}_grads_between_chips", chip_sums)
    return {nm: _sum_terms(f"l{l}_total_{nm}", [(e, q) for q in range(4)], F32) for (nm, _), e in zip(specs, exchanged)}


def _allreduce_small(vecs):
    flat = jnp.concatenate(vecs)
    n = flat.shape[0]
    rows = -(-n // 128)
    rows = -(-rows // 8) * 8
    buf = jnp.pad(flat, (0, rows * 128 - n)).reshape(rows, 128)
    (allv,) = _allgather("gather_small_grads", [buf])
    tot = _sum_terms("small_grads_total", [(allv, d) for d in range(N_DEV)], F32).reshape(-1)
    out, o = [], 0
    for v in vecs:
        out.append(tot[o: o + v.shape[0]])
        o += v.shape[0]
    return out


def kernel(x, norm_mix_w, w_in, ssm_conv_w, ssm_conv_b, ssm_dt_bias, ssm_a_log, ssm_d, ssm_norm_w, gdn_conv_w, gdn_a_log, gdn_dt_bias, gdn_norm_w, w_proj_ssm, w_proj_gdn, w_out, norm_ffn_w, w_ffn_in, w_ffn_down, final_norm_w, loss_target, m_norm_mix_w, m_w_in, m_ssm_conv_w, m_ssm_conv_b, m_ssm_dt_bias, m_ssm_a_log, m_ssm_d, m_ssm_norm_w, m_gdn_conv_w, m_gdn_a_log, m_gdn_dt_bias, m_gdn_norm_w, m_w_proj_ssm, m_w_proj_gdn, m_w_out, m_norm_ffn_w, m_w_ffn_in, m_w_ffn_down, m_final_norm_w, v_norm_mix_w, v_w_in, v_ssm_conv_w, v_ssm_conv_b, v_ssm_dt_bias, v_ssm_a_log, v_ssm_d, v_ssm_norm_w, v_gdn_conv_w, v_gdn_a_log, v_gdn_dt_bias, v_gdn_norm_w, v_w_proj_ssm, v_w_proj_gdn, v_w_out, v_norm_ffn_w, v_w_ffn_in, v_w_ffn_down, v_final_norm_w):
    w = dict(norm_mix_w=norm_mix_w, w_in=w_in, ssm_conv_w=ssm_conv_w, ssm_conv_b=ssm_conv_b, ssm_dt_bias=ssm_dt_bias,
             ssm_a_log=ssm_a_log, ssm_d=ssm_d, ssm_norm_w=ssm_norm_w, gdn_conv_w=gdn_conv_w, gdn_a_log=gdn_a_log,
             gdn_dt_bias=gdn_dt_bias, gdn_norm_w=gdn_norm_w, w_proj_ssm=w_proj_ssm, w_proj_gdn=w_proj_gdn, w_out=w_out,
             norm_ffn_w=norm_ffn_w, w_ffn_in=w_ffn_in, w_ffn_down=w_ffn_down, final_norm_w=final_norm_w)
    m = dict(norm_mix_w=m_norm_mix_w, w_in=m_w_in, ssm_conv_w=m_ssm_conv_w, ssm_conv_b=m_ssm_conv_b, ssm_dt_bias=m_ssm_dt_bias,
             ssm_a_log=m_ssm_a_log, ssm_d=m_ssm_d, ssm_norm_w=m_ssm_norm_w, gdn_conv_w=m_gdn_conv_w, gdn_a_log=m_gdn_a_log,
             gdn_dt_bias=m_gdn_dt_bias, gdn_norm_w=m_gdn_norm_w, w_proj_ssm=m_w_proj_ssm, w_proj_gdn=m_w_proj_gdn,
             w_out=m_w_out, norm_ffn_w=m_norm_ffn_w, w_ffn_in=m_w_ffn_in, w_ffn_down=m_w_ffn_down,
             final_norm_w=m_final_norm_w)
    v = dict(norm_mix_w=v_norm_mix_w, w_in=v_w_in, ssm_conv_w=v_ssm_conv_w, ssm_conv_b=v_ssm_conv_b, ssm_dt_bias=v_ssm_dt_bias,
             ssm_a_log=v_ssm_a_log, ssm_d=v_ssm_d, ssm_norm_w=v_ssm_norm_w, gdn_conv_w=v_gdn_conv_w, gdn_a_log=v_gdn_a_log,
             gdn_dt_bias=v_gdn_dt_bias, gdn_norm_w=v_gdn_norm_w, w_proj_ssm=v_w_proj_ssm, w_proj_gdn=v_w_proj_gdn,
             w_out=v_w_out, norm_ffn_w=v_norm_ffn_w, w_ffn_in=v_w_ffn_in, w_ffn_down=v_w_ffn_down,
             final_norm_w=v_final_norm_w)

    layers = []
    for l in range(DEPTH):
        lp = _gather_layer(l, w)
        lp.update({nm: w[nm][l] for nm in SMALL})
        layers.append(lp)
    loss_part, dx, lgrads, dfw = _local_step(x[0], loss_target[0], layers, final_norm_w)
    loss = lax.psum(loss_part, ("x", "y", "c"))

    shard_grads = []
    for l in range(DEPTH):
        lg = {nm: lgrads[l][nm] for nm, _ in BIG + CONVS}
        lg["w_in"] = _unpermute_in_cols(lg["w_in"])
        shard_grads.append(_reduce_scatter_layer(l, lg))
    grad = {nm: jnp.stack([shard_grads[l][nm] for l in range(DEPTH)]) for nm, _ in BIG + CONVS}
    small_vecs = [lgrads[l][nm].reshape(-1) for l in range(DEPTH) for nm in SMALL] + [dfw]
    small_sum = _allreduce_small(small_vecs)
    for i, nm in enumerate(SMALL):
        grad[nm] = jnp.stack([small_sum[l * len(SMALL) + i].reshape(w[nm].shape[1:]) for l in range(DEPTH)])
    grad["final_norm_w"] = small_sum[-1]

    deltas, new_m, new_v = {}, {}, {}
    for nm in WEIGHTS:
        deltas[nm], new_m[nm], new_v[nm] = _adamw("adamw_" + nm, w[nm], grad[nm], m[nm], v[nm])
    return (loss, dx[None], *[grad[nm] for nm in WEIGHTS], *[deltas[nm] for nm in WEIGHTS],
            *[new_m[nm] for nm in WEIGHTS], *[new_v[nm] for nm in WEIGHTS])
```

```python
import functools

import jax
import jax.numpy as jnp
from jax import lax
from jax.experimental import pallas as pl
from jax.experimental.pallas import tpu as pltpu
from jax.experimental.pallas import tpu_sc as plsc

F32 = jnp.float32
BF16 = jnp.bfloat16
HI = lax.Precision.HIGHEST
SDS = jax.ShapeDtypeStruct

D_MODEL = 1024
DEPTH = 2
SSM_HEADS = 16
SSM_P = 64
SSM_N = 128
SSM_GROUPS = 2
SSM_CONV = 1536
GDN_HEADS = 8
GDN_DK = 128
GDN_QKV = 3072
CONV_K = 4
CHUNK = 64
FFN = 2816
IN_DIM = 8736
EPS = 1e-6
N_DEV = 8

Z_OFF = 0
GZ_OFF = 1024
G1_OFF = 2048
G2_OFF = 3072
QKV_OFF = 4096
XBC_OFF = 7168
SM_OFF = 8704
PROJ_W = 8960
LANE_A = 16
LANE_B = 24
O_Z, O_XBC, O_DT, O_QKV, O_GZ, O_A, O_B, O_G1, O_G2 = 0, 1024, 2560, 2576, 5648, 6672, 6680, 6688, 7712

ADAM_LR = 0.001
ADAM_B1 = 0.9
ADAM_B2 = 0.999
ADAM_EPS = 1e-08
ADAM_WD = 0.01
ADAM_STEP = 10

V7X_VMEM_LIMIT = 48 * 1024 * 1024

NN = ((1,), (0,))
NT = ((1,), (1,))
TN = ((0,), (0,))


def _bdot(a, b, dims):
    return lax.dot_general(a.astype(BF16), b.astype(BF16), (dims, ((), ())), preferred_element_type=F32)


def _hdot(a, b, dims=NN):
    return lax.dot_general(a, b, (dims, ((), ())), precision=HI, preferred_element_type=F32)


def _sigmoid(x):
    return 1.0 / (1.0 + jnp.exp(-x))


def _softplus(x):
    return jnp.maximum(x, 0.0) + jnp.log(1.0 + jnp.exp(-jnp.abs(x)))


def _params(dims):
    return pltpu.CompilerParams(dimension_semantics=dims, vmem_limit_bytes=V7X_VMEM_LIMIT)


def _rowsum(x):
    return jnp.sum(x, axis=-1, keepdims=True)


def _colsum(x):
    return jnp.sum(x, axis=0, keepdims=True)


def _matmul(name, mode, pairs, m, n, kdim, tm, tn, tk, out_dtypes, epi=None, extras=()):
    tm, tn, tk = min(tm, m), min(tn, n), min(tk, kdim)
    nk = kdim // tk
    assert m % tm == 0 and n % tn == 0 and kdim % tk == 0, (name, m, n, kdim, tm, tn, tk)
    in_specs, args = [], []
    for a, a_off, b, b_off in pairs:
        if mode == "nn":
            in_specs.append(pl.BlockSpec((tm, tk), lambda i, j, k, o=a_off: (i, k + o)))
            in_specs.append(pl.BlockSpec((tk, tn), lambda i, j, k, o=b_off: (k, j + o)))
            dims = NN
        elif mode == "nt":
            in_specs.append(pl.BlockSpec((tm, tk), lambda i, j, k, o=a_off: (i, k + o)))
            in_specs.append(pl.BlockSpec((tn, tk), lambda i, j, k, o=b_off: (j, k + o)))
            dims = NT
        else:
            in_specs.append(pl.BlockSpec((tk, tm), lambda i, j, k, o=a_off: (k, i + o)))
            in_specs.append(pl.BlockSpec((tk, tn), lambda i, j, k, o=b_off: (k, j + o)))
            dims = TN
        args += [a, b]
    for e, e_off in extras:
        in_specs.append(pl.BlockSpec((tm, tn), lambda i, j, k, o=e_off: (i, j + o)))
        args.append(e)
    npair, nex, nout = len(pairs), len(extras), len(out_dtypes)

    def body(*refs):
        prefs = refs[: 2 * npair]
        erefs = refs[2 * npair: 2 * npair + nex]
        orefs = refs[2 * npair + nex: 2 * npair + nex + nout]

        def finish(res):
            outs = (res,) if epi is None else epi(res, *[e[...] for e in erefs])
            for o, r in zip(orefs, outs):
                o[...] = r.astype(o.dtype)

        s = _bdot(prefs[0][...], prefs[1][...], dims)
        for p in range(1, npair):
            s = s + _bdot(prefs[2 * p][...], prefs[2 * p + 1][...], dims)
        if nk == 1:
            finish(s)
            return
        acc = refs[-1]
        k = pl.program_id(2)

        @pl.when(k == 0)
        def _():
            acc[...] = s

        @pl.when(k > 0)
        def _():
            acc[...] += s

        @pl.when(k == nk - 1)
        def _():
            finish(acc[...])

    out_shape = tuple(SDS((m, n), dt) for dt in out_dtypes)
    out_specs = tuple(pl.BlockSpec((tm, tn), lambda i, j, k: (i, j)) for _ in out_dtypes)
    res = pl.pallas_call(
        body, grid=(m // tm, n // tn, nk), in_specs=in_specs, out_specs=out_specs, out_shape=out_shape,
        scratch_shapes=[pltpu.VMEM((tm, tn), F32)] if nk > 1 else [], name=name,
        compiler_params=_params(("parallel", "parallel", "arbitrary")),
    )(*args)
    return res if nout > 1 else res[0]


def _rmsnorm_fwd(name, x, w):
    t, d = x.shape
    tm = min(512, t)

    def body(x_ref, w_ref, h_ref):
        xv = x_ref[...]
        r = lax.rsqrt(jnp.mean(xv * xv, axis=-1, keepdims=True) + EPS)
        h_ref[...] = (xv * r * w_ref[...]).astype(BF16)

    return pl.pallas_call(
        body, grid=(t // tm,),
        in_specs=[pl.BlockSpec((tm, d), lambda i: (i, 0)), pl.BlockSpec((1, d), lambda i: (0, 0))],
        out_specs=pl.BlockSpec((tm, d), lambda i: (i, 0)), out_shape=SDS((t, d), BF16), name=name,
        compiler_params=_params(("parallel",)),
    )(x, w.reshape(1, d))


def _rmsnorm_bwd(name, x, w, dh, dres):
    t, d = x.shape
    tm = min(512, t)

    def body(x_ref, w_ref, dh_ref, dres_ref, dx_ref, dxb_ref, dw_ref):
        xv = x_ref[...]
        r = lax.rsqrt(jnp.mean(xv * xv, axis=-1, keepdims=True) + EPS)
        xh = xv * r
        dhv = dh_ref[...].astype(F32)
        dxh = dhv * w_ref[...]
        dx = r * (dxh - xh * jnp.mean(dxh * xh, axis=-1, keepdims=True)) + dres_ref[...]
        dx_ref[...] = dx
        dxb_ref[...] = dx.astype(BF16)

        @pl.when(pl.program_id(0) == 0)
        def _():
            dw_ref[...] = jnp.zeros_like(dw_ref)

        dw_ref[...] += _colsum(dhv * xh)

    row = pl.BlockSpec((tm, d), lambda i: (i, 0))
    vec = pl.BlockSpec((1, d), lambda i: (0, 0))
    return pl.pallas_call(
        body, grid=(t // tm,), in_specs=[row, vec, row, row], out_specs=(row, row, vec),
        out_shape=(SDS((t, d), F32), SDS((t, d), BF16), SDS((1, d), F32)), name=name,
        compiler_params=_params(("arbitrary",)),
    )(x, w.reshape(1, d), dh, dres)


def _loss_head(name, x, w, tgt):
    t, d = x.shape
    tm = min(512, t)

    def body(x_ref, w_ref, t_ref, loss_ref, dx_ref, dxb_ref, dw_ref):
        xv = x_ref[...]
        wv = w_ref[...]
        r = lax.rsqrt(jnp.mean(xv * xv, axis=-1, keepdims=True) + EPS)
        xh = xv * r
        e = xh * wv - t_ref[...]
        dy = e * (1.0 / d)
        dxh = dy * wv
        dx = r * (dxh - xh * jnp.mean(dxh * xh, axis=-1, keepdims=True))
        dx_ref[...] = dx
        dxb_ref[...] = dx.astype(BF16)

        @pl.when(pl.program_id(0) == 0)
        def _():
            dw_ref[...] = jnp.zeros_like(dw_ref)
            loss_ref[...] = jnp.zeros_like(loss_ref)

        dw_ref[...] += _colsum(dy * xh)
        loss_ref[...] += 0.5 * jnp.sum(jnp.mean(e * e, axis=-1, keepdims=True), axis=0, keepdims=True)

    row = pl.BlockSpec((tm, d), lambda i: (i, 0))
    vec = pl.BlockSpec((1, d), lambda i: (0, 0))
    return pl.pallas_call(
        body, grid=(t // tm,), in_specs=[row, vec, row],
        out_specs=(pl.BlockSpec((1, 1), lambda i: (0, 0)), row, row, vec),
        out_shape=(SDS((1, 1), F32), SDS((t, d), F32), SDS((t, d), BF16), SDS((1, d), F32)), name=name,
        compiler_params=_params(("arbitrary",)),
    )(x, w.reshape(1, d), tgt)


def _shift_down(u, s, row):
    return jnp.where(row >= s, pltpu.roll(u, shift=s, axis=0), 0.0)


def _conv_fwd(name, src, col0, w, b):
    t = src.shape[0]
    c = w.shape[1]
    tc = 256
    assert c % tc == 0 and col0 % tc == 0

    def body(u_ref, w_ref, b_ref, o_ref):
        u = u_ref[...]
        wv = w_ref[...]
        row = lax.broadcasted_iota(jnp.int32, u.shape, 0)
        pre = b_ref[...] + wv[3:4, :] * u
        for s in range(1, CONV_K):
            pre = pre + wv[3 - s: 4 - s, :] * _shift_down(u, s, row)
        o_ref[...] = pre * _sigmoid(pre)

    return pl.pallas_call(
        body, grid=(c // tc,),
        in_specs=[pl.BlockSpec((t, tc), lambda j: (0, j + col0 // tc)), pl.BlockSpec((CONV_K, tc), lambda j: (0, j)),
                  pl.BlockSpec((1, tc), lambda j: (0, j))],
        out_specs=pl.BlockSpec((t, tc), lambda j: (0, j)), out_shape=SDS((t, c), F32), name=name,
        compiler_params=_params(("parallel",)),
    )(src, w, b)


def _conv_bwd(name, src, col0, w, b, dact):
    t = src.shape[0]
    c = w.shape[1]
    tc = 128

    def body(u_ref, w_ref, b_ref, da_ref, du_ref, dw_ref, db_ref):
        u = u_ref[...]
        wv = w_ref[...]
        row = lax.broadcasted_iota(jnp.int32, u.shape, 0)
        shifted = [u] + [_shift_down(u, s, row) for s in range(1, CONV_K)]
        pre = b_ref[...] + wv[3:4, :] * u
        for s in range(1, CONV_K):
            pre = pre + wv[3 - s: 4 - s, :] * shifted[s]
        sg = _sigmoid(pre)
        dpre = da_ref[...] * (sg * (1.0 + pre * (1.0 - sg)))
        du = wv[3:4, :] * dpre
        for s in range(1, CONV_K):
            du = du + wv[3 - s: 4 - s, :] * jnp.where(row < t - s, pltpu.roll(dpre, shift=t - s, axis=0), 0.0)
        du_ref[...] = du.astype(BF16)
        for s in range(CONV_K):
            dw_ref[3 - s: 4 - s, :] = _colsum(dpre * shifted[s])
        db_ref[...] = _colsum(dpre)

    return pl.pallas_call(
        body, grid=(c // tc,),
        in_specs=[pl.BlockSpec((t, tc), lambda j: (0, j + col0 // tc)), pl.BlockSpec((CONV_K, tc), lambda j: (0, j)),
                  pl.BlockSpec((1, tc), lambda j: (0, j)), pl.BlockSpec((t, tc), lambda j: (0, j))],
        out_specs=(pl.BlockSpec((t, tc), lambda j: (0, j)), pl.BlockSpec((CONV_K, tc), lambda j: (0, j)),
                   pl.BlockSpec((1, tc), lambda j: (0, j))),
        out_shape=(SDS((t, c), BF16), SDS((CONV_K, c), F32), SDS((1, c), F32)), name=name,
        compiler_params=_params(("parallel",)),
    )(src, w, b, dact)


def _tri(q):
    ii = lax.broadcasted_iota(jnp.int32, (q, q), 0)
    jj = lax.broadcasted_iota(jnp.int32, (q, q), 1)
    return ii, jj


def _dot01(x, r01, dims, terms=3):
    out, rem = None, x
    for i in range(terms):
        hi = rem.astype(BF16)
        d = lax.dot_general(hi, r01, (dims, ((), ())), preferred_element_type=F32)
        out = d if out is None else out + d
        if i + 1 < terms:
            rem = rem - hi.astype(F32)
    return out


def _ssd_common(act, sm, dtb, arow, rmat):
    q = CHUNK
    ii, jj = _tri(q)
    lane = lax.broadcasted_iota(jnp.int32, (q, 128), 1)
    m16 = lane < SSM_HEADS
    dt = jnp.where(m16, _softplus(sm + dtb), 0.0)
    a = dt * arow
    tril = (ii >= jj).astype(F32)
    triu = (ii <= jj).astype(F32)
    acum = _hdot(tril, a)
    acum_r = _hdot(a.T, triu)
    dtx = _dot01(dt, rmat, NN)
    acx = _dot01(acum, rmat, NN)
    ex = jnp.exp(acx)
    alx = acx[q - 1: q, :]
    dex = jnp.exp(alx - acx)
    xs = act[:, :1024]
    return dict(ii=ii, jj=jj, m16=m16, dt=dt, a=a, triu=triu, acum=acum, acum_r=acum_r, dtx=dtx, ex=ex, dex=dex,
                elx=jnp.exp(alx), xs=xs, x=xs * dtx)


def _ssd_lmat(cm, h):
    return jnp.where(cm["ii"] >= cm["jj"], jnp.exp(cm["acum"][:, h: h + 1] - cm["acum_r"][h: h + 1, :]), 0.0)


def _ssd_fwd(name, act, proj, dtb, arow, dxrow, nw, rmat):
    t = act.shape[0]
    q = CHUNK
    nc = t // q
    hg = SSM_HEADS // SSM_GROUPS
    gw = hg * SSM_P

    def body(act_ref, z_ref, sm_ref, dtb_ref, arow_ref, dx_ref, nw_ref, r_ref, y_ref, ys_ref, st_ref, s_scr, yd_scr):
        @pl.when(pl.program_id(0) == 0)
        def _():
            s_scr[...] = jnp.zeros_like(s_scr)

        s_all = s_scr[...]
        st_ref[0] = s_all
        actv = act_ref[...]
        cm = _ssd_common(actv, sm_ref[...], dtb_ref[...], arow_ref[...], r_ref[...])
        x = cm["x"]
        xd = x * cm["dex"]
        yoffs, snew = [], []
        for g in range(SSM_GROUPS):
            bg = actv[:, 1024 + g * SSM_N: 1024 + (g + 1) * SSM_N]
            cg = actv[:, 1280 + g * SSM_N: 1280 + (g + 1) * SSM_N]
            sg = s_all[:, g * gw: (g + 1) * gw]
            cb = _bdot(cg, bg, NT)
            yoffs.append(_bdot(cg, sg, NN))
            snew.append(_bdot(bg, xd[:, g * gw: (g + 1) * gw], TN))
            for r in range(hg):
                h = g * hg + r
                mm = cb * _ssd_lmat(cm, h)
                yd_scr[:, h * SSM_P: (h + 1) * SSM_P] = _bdot(mm, x[:, h * SSM_P: (h + 1) * SSM_P], NN)
        s_scr[...] = s_all * cm["elx"] + jnp.concatenate(snew, axis=1)
        ysc = yd_scr[...] + jnp.concatenate(yoffs, axis=1) * cm["ex"]
        ys_ref[...] = ysc
        zv = z_ref[...]
        yg = (ysc + dx_ref[...] * cm["xs"]) * (zv * _sigmoid(zv))
        nwv = nw_ref[...]
        for g in range(SSM_GROUPS):
            sl = yg[:, g * gw: (g + 1) * gw]
            rr = lax.rsqrt(jnp.mean(sl * sl, axis=-1, keepdims=True) + EPS)
            y_ref[:, g * gw: (g + 1) * gw] = (sl * rr * nwv[:, g * gw: (g + 1) * gw]).astype(BF16)

    vec128 = pl.BlockSpec((1, 128), lambda c: (0, 0))
    vec1k = pl.BlockSpec((1, 1024), lambda c: (0, 0))
    return pl.pallas_call(
        body, grid=(nc,),
        in_specs=[pl.BlockSpec((q, SSM_CONV), lambda c: (c, 0)), pl.BlockSpec((q, 1024), lambda c: (c, Z_OFF // 1024)),
                  pl.BlockSpec((q, 128), lambda c: (c, SM_OFF // 128)), vec128, vec128, vec1k, vec1k,
                  pl.BlockSpec((128, 1024), lambda c: (0, 0))],
        out_specs=(pl.BlockSpec((q, 1024), lambda c: (c, 0)), pl.BlockSpec((q, 1024), lambda c: (c, 0)),
                   pl.BlockSpec((1, 128, 1024), lambda c: (c, 0, 0))),
        out_shape=(SDS((t, 1024), BF16), SDS((t, 1024), F32), SDS((nc, 128, 1024), F32)),
        scratch_shapes=[pltpu.VMEM((128, 1024), F32), pltpu.VMEM((q, 1024), F32)], name=name,
        compiler_params=_params(("arbitrary",)),
    )(act, proj, proj, dtb, arow, dxrow, nw, rmat)


def _ssd_bwd(name, act, proj, dtb, arow, dxrow, nw, rmat, ysc, states, dy):
    t = act.shape[0]
    q = CHUNK
    nc = t // q
    hg = SSM_HEADS // SSM_GROUPS
    gw = hg * SSM_P

    def body(act_ref, z_ref, sm_ref, dtb_ref, arow_ref, dx_ref, nw_ref, r_ref, ys_ref, st_ref, dy_ref,
             dact_ref, dz_ref, dsm_ref, dnw_ref, dd_ref, dal_ref, ddtb_ref, ds_scr, dxd_scr):
        @pl.when(pl.program_id(0) == 0)
        def _():
            ds_scr[...] = jnp.zeros_like(ds_scr)
            dnw_ref[...] = jnp.zeros_like(dnw_ref)
            dd_ref[...] = jnp.zeros_like(dd_ref)
            dal_ref[...] = jnp.zeros_like(dal_ref)
            ddtb_ref[...] = jnp.zeros_like(ddtb_ref)

        actv = act_ref[...]
        smv = sm_ref[...]
        rmat_v = r_ref[...]
        cm = _ssd_common(actv, smv, dtb_ref[...], arow_ref[...], rmat_v)
        ii, jj = cm["ii"], cm["jj"]
        x, xs = cm["x"], cm["xs"]
        s_all = st_ref[0]
        dsn = ds_scr[...]
        ysv = ys_ref[...]
        dxr = dx_ref[...]
        y = ysv + dxr * xs
        zv = z_ref[...]
        sz = _sigmoid(zv)
        silz = zv * sz
        yg = y * silz
        dout = dy_ref[...]
        nwv = nw_ref[...]
        dyn = dout * nwv
        yn_parts, dyg_parts = [], []
        for g in range(SSM_GROUPS):
            sl = yg[:, g * gw: (g + 1) * gw]
            rr = lax.rsqrt(jnp.mean(sl * sl, axis=-1, keepdims=True) + EPS)
            yn = sl * rr
            dn = dyn[:, g * gw: (g + 1) * gw]
            yn_parts.append(yn)
            dyg_parts.append(rr * (dn - yn * jnp.mean(dn * yn, axis=-1, keepdims=True)))
        dnw_ref[...] += _colsum(dout * jnp.concatenate(yn_parts, axis=1))
        dyg = jnp.concatenate(dyg_parts, axis=1)
        dyv = dyg * silz
        dz_ref[...] = (dyg * y * (sz * (1.0 + zv * (1.0 - sz)))).astype(BF16)
        dd_ref[...] += _dot01(_colsum(dyv * xs), rmat_v, NT)
        dxs = dyv * dxr
        dcs = dyv * cm["ex"]
        xd = x * cm["dex"]
        dxst_parts, ds_parts, db_parts, dc_parts, yoff_parts, wcol_rows = [], [], [], [], [], []
        lane128 = lax.broadcasted_iota(jnp.int32, (q, 128), 1)
        wrow = jnp.zeros((q, 128), F32)
        for g in range(SSM_GROUPS):
            bg = actv[:, 1024 + g * SSM_N: 1024 + (g + 1) * SSM_N]
            cg = actv[:, 1280 + g * SSM_N: 1280 + (g + 1) * SSM_N]
            sg = s_all[:, g * gw: (g + 1) * gw]
            dsng = dsn[:, g * gw: (g + 1) * gw]
            dcsg = dcs[:, g * gw: (g + 1) * gw]
            dcg = _bdot(dcsg, sg, NT)
            yoff_parts.append(_bdot(cg, sg, NN))
            ds_parts.append(_bdot(cg, dcsg, TN))
            dxst_parts.append(_bdot(bg, dsng, NN))
            dbg = _bdot(xd[:, g * gw: (g + 1) * gw], dsng, NT)
            cb = _bdot(cg, bg, NT)
            dcb = jnp.zeros((q, q), F32)
            for r in range(hg):
                h = g * hg + r
                lm = _ssd_lmat(cm, h)
                mm = cb * lm
                dyh = dyv[:, h * SSM_P: (h + 1) * SSM_P]
                dm = jnp.where(ii >= jj, _bdot(dyh, x[:, h * SSM_P: (h + 1) * SSM_P], NT), 0.0)
                dxd_scr[:, h * SSM_P: (h + 1) * SSM_P] = _bdot(mm, dyh, TN)
                dcb = dcb + dm * lm
                wm = dm * mm
                wrow = wrow + jnp.where(lane128 == h, _rowsum(wm), 0.0)
                wcol_rows.append(_colsum(wm))
            dc_parts.append(dcg + _bdot(dcb, bg, NN))
            db_parts.append(dbg + _bdot(dcb, cg, TN))
        dxst = jnp.concatenate(dxst_parts, axis=1) * cm["dex"]
        dx = dxd_scr[...] + dxst
        ds_scr[...] = jnp.concatenate(ds_parts, axis=1) + dsn * cm["elx"]
        wcol = jnp.concatenate(wcol_rows + [jnp.zeros((128 - SSM_HEADS, q), F32)], axis=0).T
        yoff = jnp.concatenate(yoff_parts, axis=1) * cm["ex"]
        xdxst = x * dxst
        dac = wrow - wcol + _dot01(dyv * yoff - xdxst, rmat_v, NT)
        last = _dot01(_colsum(dsn * s_all) * cm["elx"] + _colsum(xdxst), rmat_v, NT)
        rowq = lax.broadcasted_iota(jnp.int32, (q, 128), 0)
        dac = dac + jnp.where(rowq == q - 1, last, 0.0)
        da = _hdot(cm["triu"], dac)
        arow_v = arow_ref[...]
        ddt = da * arow_v + _dot01(dx * xs, rmat_v, NT)
        dxs = dxs + dx * cm["dtx"]
        dal_ref[...] += _colsum(da * cm["a"])
        ddtraw = jnp.where(cm["m16"], ddt * _sigmoid(smv + dtb_ref[...]), 0.0)
        ddtb_ref[...] += _colsum(ddtraw)
        dsm_ref[...] = ddtraw.astype(BF16)
        dact_ref[:, :1024] = dxs
        for g in range(SSM_GROUPS):
            dact_ref[:, 1024 + g * SSM_N: 1024 + (g + 1) * SSM_N] = db_parts[g]
            dact_ref[:, 1280 + g * SSM_N: 1280 + (g + 1) * SSM_N] = dc_parts[g]

    rev = lambda c: nc - 1 - c
    vec128 = pl.BlockSpec((1, 128), lambda c: (0, 0))
    vec1k = pl.BlockSpec((1, 1024), lambda c: (0, 0))
    return pl.pallas_call(
        body, grid=(nc,),
        in_specs=[pl.BlockSpec((q, SSM_CONV), lambda c: (rev(c), 0)),
                  pl.BlockSpec((q, 1024), lambda c: (rev(c), Z_OFF // 1024)),
                  pl.BlockSpec((q, 128), lambda c: (rev(c), SM_OFF // 128)), vec128, vec128, vec1k, vec1k,
                  pl.BlockSpec((128, 1024), lambda c: (0, 0)),
                  pl.BlockSpec((q, 1024), lambda c: (rev(c), 0)), pl.BlockSpec((1, 128, 1024), lambda c: (rev(c), 0, 0)),
                  pl.BlockSpec((q, 1024), lambda c: (rev(c), 0))],
        out_specs=(pl.BlockSpec((q, SSM_CONV), lambda c: (rev(c), 0)), pl.BlockSpec((q, 1024), lambda c: (rev(c), 0)),
                   pl.BlockSpec((q, 128), lambda c: (rev(c), 0)), vec1k, vec128, vec128, vec128),
        out_shape=(SDS((t, SSM_CONV), F32), SDS((t, 1024), BF16), SDS((t, 128), BF16), SDS((1, 1024), F32),
                   SDS((1, 128), F32), SDS((1, 128), F32), SDS((1, 128), F32)),
        scratch_shapes=[pltpu.VMEM((128, 1024), F32), pltpu.VMEM((q, 1024), F32)], name=name,
        compiler_params=_params(("arbitrary",)),
    )(act, proj, proj, dtb, arow, dxrow, nw, rmat, ysc, states, dy)


def _split(a):
    hi = a.astype(BF16)
    return hi, (a - hi.astype(F32)).astype(BF16)


def _dot3(a, b, dims=NN):
    (ah, al), (bh, bl) = a, b

    def d(x, y):
        return lax.dot_general(x, y, (dims, ((), ())), preferred_element_type=F32)

    return d(ah, bh) + (d(ah, bl) + d(al, bh))


def _tri_inverses(amats, ii, jj):
    eye = jnp.where(ii == jj, 1.0, 0.0)
    tms = [eye - a for a in amats]
    sp = [_split(a) for a in amats]
    for _ in range(5):
        sp = [_split(_dot3(s, s)) for s in sp]
        tms = [t + _dot3(_split(t), s) for t, s in zip(tms, sp)]
    return tms


def _gdn_common(sm, gb, garow):
    q = CHUNK
    ii, jj = _tri(q)
    lane = lax.broadcasted_iota(jnp.int32, (q, 128), 1)
    ma = (lane >= LANE_A) & (lane < LANE_A + GDN_HEADS)
    spre = sm + gb
    g = jnp.where(ma, garow * _softplus(spre), 0.0)
    beta = _sigmoid(sm)
    tril = (ii >= jj).astype(F32)
    triu = (ii <= jj).astype(F32)
    gc = _hdot(tril, g)
    gc_r = _hdot(g.T, triu)
    return dict(ii=ii, jj=jj, lane=lane, ma=ma, spre=spre, g=g, beta=beta, triu=triu, gc=gc, gc_r=gc_r)


def _each(f, *lists):
    return [f(*xs) for xs in zip(*lists)]


GDN_SCALE = GDN_DK ** -0.5


def _gdn_heads(cm, actv, states):
    q = CHUNK
    ii, jj = cm["ii"], cm["jj"]
    heads = range(GDN_HEADS)
    qr = [actv[:, h * 128: (h + 1) * 128] for h in heads]
    kr = [actv[:, 1024 + h * 128: 1024 + (h + 1) * 128] for h in heads]
    v = [actv[:, 2048 + h * 128: 2048 + (h + 1) * 128] for h in heads]
    rq = _each(lambda x: lax.rsqrt(_rowsum(x * x) + EPS), qr)
    rk = _each(lambda x: lax.rsqrt(_rowsum(x * x) + EPS), kr)
    qn = _each(lambda x, r: x * r * GDN_SCALE, qr, rq)
    kn = _each(lambda x, r: x * r, kr, rk)
    gcc = [cm["gc"][:, LANE_A + h: LANE_A + h + 1] for h in heads]
    gcr = [cm["gc_r"][LANE_A + h: LANE_A + h + 1, :] for h in heads]
    bcol = [cm["beta"][:, LANE_B + h: LANE_B + h + 1] for h in heads]
    dm = _each(lambda c, r: jnp.where(ii >= jj, jnp.exp(c - r), 0.0), gcc, gcr)
    ak = _each(lambda k, d: jnp.where(ii > jj, _bdot(k, k, NT) * d, 0.0), kn, dm)
    tm = _tri_inverses(_each(lambda a, b: a * b, ak, bcol), ii, jj)
    eg = _each(jnp.exp, gcc)
    gl = [c[q - 1: q, :] for c in gcc]
    rm = _each(lambda vv, k, b, e: jnp.concatenate([vv * b, k * (b * e)], axis=1), v, kn, bcol, eg)
    tt = _each(lambda t, r: _dot3(_split(t), _split(r)), tm, rm)
    w = [t[:, 128:] for t in tt]
    vnew = _each(lambda t, ww, s: t[:, :128] - _bdot(ww, s, NN), tt, w, states)
    qkm = _each(lambda a, k, d: jnp.where(ii >= jj, _bdot(a, k, NT) * d, 0.0), qn, kn, dm)
    return dict(qr=qr, v=v, rq=rq, rk=rk, qn=qn, kn=kn, gcc=gcc, bcol=bcol, dm=dm, ak=ak, tm=tm, eg=eg, gl=gl,
                egl=_each(jnp.exp, gl), ed=_each(lambda g, c: jnp.exp(g - c), gl, gcc), tt=tt, w=w, vnew=vnew, qkm=qkm,
                qg=_each(lambda a, e: a * e, qn, eg))


def _gdn_fwd(name, act, proj, gb, garow, gnw):
    t = act.shape[0]
    q = CHUNK
    nc = t // q

    def body(act_ref, gz_ref, sm_ref, gb_ref, ga_ref, nw_ref, y_ref, o_ref, st_ref, s_scr):
        @pl.when(pl.program_id(0) == 0)
        def _():
            s_scr[...] = jnp.zeros_like(s_scr)

        st_ref[0] = s_scr[...]
        actv = act_ref[...]
        cm = _gdn_common(sm_ref[...], gb_ref[...], ga_ref[...])
        nwv = nw_ref[...]
        gzv = gz_ref[...]
        states = [s_scr[h * 128: (h + 1) * 128, :] for h in range(GDN_HEADS)]
        hd = _gdn_heads(cm, actv, states)
        outs = _each(lambda a, s, m, vn: _bdot(a, s, NN) + _bdot(m, vn, NN), hd["qg"], states, hd["qkm"], hd["vnew"])
        snew = _each(lambda s, e, k, d, vn: s * e + _bdot(k * d, vn, TN), states, hd["egl"], hd["kn"], hd["ed"], hd["vnew"])
        for h in range(GDN_HEADS):
            o = outs[h]
            s_scr[h * 128: (h + 1) * 128, :] = snew[h]
            o_ref[:, h * 128: (h + 1) * 128] = o
            rr = lax.rsqrt(jnp.mean(o * o, axis=-1, keepdims=True) + EPS)
            gz = gzv[:, h * 128: (h + 1) * 128]
            y_ref[:, h * 128: (h + 1) * 128] = (o * rr * nwv * (gz * _sigmoid(gz))).astype(BF16)

    vec128 = pl.BlockSpec((1, 128), lambda c: (0, 0))
    return pl.pallas_call(
        body, grid=(nc,),
        in_specs=[pl.BlockSpec((q, GDN_QKV), lambda c: (c, 0)), pl.BlockSpec((q, 1024), lambda c: (c, GZ_OFF // 1024)),
                  pl.BlockSpec((q, 128), lambda c: (c, SM_OFF // 128)), vec128, vec128, vec128],
        out_specs=(pl.BlockSpec((q, 1024), lambda c: (c, 0)), pl.BlockSpec((q, 1024), lambda c: (c, 0)),
                   pl.BlockSpec((1, 1024, 128), lambda c: (c, 0, 0))),
        out_shape=(SDS((t, 1024), BF16), SDS((t, 1024), F32), SDS((nc, 1024, 128), F32)),
        scratch_shapes=[pltpu.VMEM((1024, 128), F32)], name=name, compiler_params=_params(("arbitrary",)),
    )(act, proj, proj, gb, garow, gnw)


def _gdn_bwd(name, act, proj, gb, garow, gnw, oraw, states, dy):
    t = act.shape[0]
    q = CHUNK
    nc = t // q

    def body(act_ref, gz_ref, sm_ref, gb_ref, ga_ref, nw_ref, o_ref, st_ref, dy_ref,
             dact_ref, dgz_ref, dsm_ref, dnw_ref, dal_ref, dgb_ref, ds_scr):
        @pl.when(pl.program_id(0) == 0)
        def _():
            ds_scr[...] = jnp.zeros_like(ds_scr)
            dnw_ref[...] = jnp.zeros_like(dnw_ref)
            dal_ref[...] = jnp.zeros_like(dal_ref)
            dgb_ref[...] = jnp.zeros_like(dgb_ref)

        actv = act_ref[...]
        smv = sm_ref[...]
        garow_v = ga_ref[...]
        cm = _gdn_common(smv, gb_ref[...], garow_v)
        ii, jj, lane = cm["ii"], cm["jj"], cm["lane"]
        nwv = nw_ref[...]
        rowq = lax.broadcasted_iota(jnp.int32, (q, 1), 0)
        dgc_all = jnp.zeros((q, 128), F32)
        dbeta_all = jnp.zeros((q, 128), F32)
        dnw_acc = jnp.zeros((1, 128), F32)
        heads = range(GDN_HEADS)
        sts = [st_ref[0, h * 128: (h + 1) * 128, :] for h in heads]
        dsn = [ds_scr[h * 128: (h + 1) * 128, :] for h in heads]
        ov, gzv, dyv = o_ref[...], gz_ref[...], dy_ref[...]
        hd = _gdn_heads(cm, actv, sts)
        qn, kn, v, eg, ed, egl, bcol = hd["qn"], hd["kn"], hd["v"], hd["eg"], hd["ed"], hd["egl"], hd["bcol"]
        vnew, qkm, qg, w, tt, dm, ak = hd["vnew"], hd["qkm"], hd["qg"], hd["w"], hd["tt"], hd["dm"], hd["ak"]
        do = []
        for h in heads:
            hs = slice(h * 128, (h + 1) * 128)
            o = ov[:, hs]
            rr = lax.rsqrt(jnp.mean(o * o, axis=-1, keepdims=True) + EPS)
            on = o * rr
            gz = gzv[:, hs]
            sz = _sigmoid(gz)
            silz = gz * sz
            dyh = dyv[:, hs]
            dnw_acc = dnw_acc + _colsum(dyh * on * silz)
            dgz_ref[:, hs] = (dyh * on * nwv * (sz * (1.0 + gz * (1.0 - sz)))).astype(BF16)
            don = dyh * nwv * silz
            do.append(rr * (don - on * jnp.mean(don * on, axis=-1, keepdims=True)))
        kd = _each(lambda k, e: k * e, kn, ed)
        dkd = _each(lambda vn, d: _bdot(vn, d, NT), vnew, dsn)
        dvnew_a = _each(lambda k, d: _bdot(k, d, NN), kd, dsn)
        ded = _each(lambda a, b: _rowsum(a * b), dkd, kd)
        dgl = _each(lambda d, s, e, de: jnp.sum(_rowsum(d * s), axis=0, keepdims=True) * e + _colsum(de), dsn, sts, egl, ded)
        dqg = _each(lambda d, s: _bdot(d, s, NT), do, sts)
        ds1 = _each(lambda a, d: _bdot(a, d, TN), qg, do)
        dqk = _each(lambda d, vn: jnp.where(ii >= jj, _bdot(d, vn, NT), 0.0), do, vnew)
        dvnew = _each(lambda a, m, d: a + _bdot(m, d, TN), dvnew_a, qkm, do)
        pq = _each(lambda a, b: a * b, dqk, dm)
        dq = _each(lambda a, e, p, k: a * e + _bdot(p, k, NN), dqg, eg, pq, kn)
        dk1 = _each(lambda a, e, p, x: a * e + _bdot(p, x, TN), dkd, ed, pq, qn)
        w1 = _each(lambda a, b: a * b, dqk, qkm)
        dw = _each(lambda d, s: -_bdot(d, s, NT), dvnew, sts)
        ds2 = _each(lambda a, d: _bdot(a, d, TN), w, dvnew)
        dr = _each(lambda t, a, b: _dot3(_split(t), _split(jnp.concatenate([a, b], axis=1)), TN), hd["tm"], dvnew, dw)
        da = _each(lambda r, t: jnp.where(ii > jj, -_dot3(_split(r), _split(t), NT), 0.0), dr, tt)
        sk = _each(lambda r, k: _rowsum(r[:, 128:] * k), dr, kn)
        pk = _each(lambda a, d, b: a * d * b, da, dm, bcol)
        dk = _each(lambda a, r, b, e, p, k: a + r[:, 128:] * (b * e) + _bdot(p, k, NN) + _bdot(p, k, TN),
                   dk1, dr, bcol, eg, pk, kn)
        w2 = _each(lambda a, k, b: a * (k * b), da, ak, bcol)
        for h in heads:
            hs = slice(h * 128, (h + 1) * 128)
            dgc = (-ded[h] + _rowsum(dqg[h] * qg[h]) + _rowsum(w1[h]) - _rowsum(w1[h].T) + sk[h] * bcol[h] * eg[h]
                   + _rowsum(w2[h]) - _rowsum(w2[h].T) + jnp.where(rowq == q - 1, dgl[h], 0.0))
            dbeta = _rowsum(dr[h][:, :128] * v[h]) + sk[h] * eg[h] + _rowsum(da[h] * ak[h])
            qhat = hd["qr"][h] * hd["rq"][h]
            dqhat = dq[h] * GDN_SCALE
            dact_ref[:, hs] = hd["rq"][h] * (dqhat - qhat * _rowsum(dqhat * qhat))
            dact_ref[:, 1024 + h * 128: 1024 + (h + 1) * 128] = hd["rk"][h] * (dk[h] - kn[h] * _rowsum(dk[h] * kn[h]))
            dact_ref[:, 2048 + h * 128: 2048 + (h + 1) * 128] = dr[h][:, :128] * bcol[h]
            dgc_all = dgc_all + jnp.where(lane == LANE_A + h, dgc, 0.0)
            dbeta_all = dbeta_all + jnp.where(lane == LANE_B + h, dbeta, 0.0)
            ds_scr[hs, :] = dsn[h] * egl[h] + ds1[h] - ds2[h]
        dnw_ref[...] += dnw_acc
        dg = _hdot(cm["triu"], dgc_all)
        da_raw = jnp.where(cm["ma"], dg * garow_v * _sigmoid(cm["spre"]), 0.0)
        dal_ref[...] += _colsum(dg * cm["g"])
        dgb_ref[...] += _colsum(da_raw)
        beta = cm["beta"]
        dsm_ref[...] = (da_raw + dbeta_all * beta * (1.0 - beta)).astype(BF16)

    rev = lambda c: nc - 1 - c
    vec128 = pl.BlockSpec((1, 128), lambda c: (0, 0))
    return pl.pallas_call(
        body, grid=(nc,),
        in_specs=[pl.BlockSpec((q, GDN_QKV), lambda c: (rev(c), 0)),
                  pl.BlockSpec((q, 1024), lambda c: (rev(c), GZ_OFF // 1024)),
                  pl.BlockSpec((q, 128), lambda c: (rev(c), SM_OFF // 128)), vec128, vec128, vec128,
                  pl.BlockSpec((q, 1024), lambda c: (rev(c), 0)), pl.BlockSpec((1, 1024, 128), lambda c: (rev(c), 0, 0)),
                  pl.BlockSpec((q, 1024), lambda c: (rev(c), 0))],
        out_specs=(pl.BlockSpec((q, GDN_QKV), lambda c: (rev(c), 0)), pl.BlockSpec((q, 1024), lambda c: (rev(c), 0)),
                   pl.BlockSpec((q, 128), lambda c: (rev(c), 0)), vec128, vec128, vec128),
        out_shape=(SDS((t, GDN_QKV), F32), SDS((t, 1024), BF16), SDS((t, 128), BF16), SDS((1, 128), F32),
                   SDS((1, 128), F32), SDS((1, 128), F32)),
        scratch_shapes=[pltpu.VMEM((1024, 128), F32)], name=name, compiler_params=_params(("arbitrary",)),
    )(act, proj, proj, gb, garow, gnw, oraw, states, dy)


def _row_tile(r):
    for cand in (512, 256, 128, 64, 32, 16, 8):
        if r % cand == 0:
            return cand
    return r


def _sum_terms(name, terms, out_dtype):
    shape = terms[0][0].shape[1:]
    c = shape[-1]
    r = 1
    for s in shape[:-1]:
        r *= s
    tr = min(_row_tile(r), 256)
    n = len(terms)

    def body(*refs):
        acc = refs[0][...].astype(F32)
        for k in range(1, n):
            acc = acc + refs[k][...].astype(F32)
        refs[n][...] = acc.astype(out_dtype)

    in_specs = [pl.BlockSpec((None, tr, c), lambda i, q=lead: (q, i, 0)) for _, lead in terms]
    args = [a.reshape(a.shape[0], r, c) for a, _ in terms]
    out = pl.pallas_call(body, grid=(r // tr,), in_specs=in_specs, out_specs=pl.BlockSpec((tr, c), lambda i: (i, 0)),
                         out_shape=SDS((r, c), out_dtype), name=name, compiler_params=_params(("parallel",)))(*args)
    return out.reshape(shape)


def _adamw(name, w, g, m, v):
    shape = w.shape
    c = shape[-1]
    w2, g2, m2, v2 = (a.reshape(-1, c) for a in (w, g, m, v))
    r = w2.shape[0]
    tr = min(_row_tile(r), 256)

    def body(w_ref, g_ref, m_ref, v_ref, d_ref, nm_ref, nv_ref):
        gv = g_ref[...]
        mn = ADAM_B1 * m_ref[...] + (1.0 - ADAM_B1) * gv
        vn = ADAM_B2 * v_ref[...] + (1.0 - ADAM_B2) * (gv * gv)
        m_hat = mn / (1.0 - ADAM_B1 ** ADAM_STEP)
        v_hat = vn / (1.0 - ADAM_B2 ** ADAM_STEP)
        d_ref[...] = -ADAM_LR * (m_hat / (jnp.sqrt(v_hat) + ADAM_EPS) + ADAM_WD * w_ref[...])
        nm_ref[...] = mn
        nv_ref[...] = vn

    spec = pl.BlockSpec((tr, c), lambda i: (i, 0))
    outs = pl.pallas_call(body, grid=(r // tr,), in_specs=[spec] * 4, out_specs=(spec,) * 3,
                          out_shape=(SDS((r, c), F32),) * 3, name=name, compiler_params=_params(("parallel",)))(w2, g2, m2, v2)
    return tuple(o.reshape(shape) for o in outs)


ANY = pl.BlockSpec(memory_space=pl.ANY)
MESH = pl.DeviceIdType.MESH


def _allgather(name, xs):
    n = len(xs)

    def body(*refs):
        x_refs, out_refs = refs[:n], refs[n: 2 * n]
        send_sems, recv_sems, local_sems = refs[2 * n:]
        x, y, cc = lax.axis_index("x"), lax.axis_index("y"), lax.axis_index("c")
        me, sibling = (x, y, cc), (x, y, 1 - cc)
        chips = [(1 - x, y), (x, 1 - y), (1 - x, 1 - y)]

        def rows(a, px, py, pc):
            return out_refs[a].at[4 * px + 2 * py + pc]

        def copy(a, k, block, to, src=None):
            return pltpu.make_async_remote_copy(
                src_ref=rows(a, *block) if src is None else src, dst_ref=rows(a, *block),
                send_sem=send_sems.at[7 * a + k], recv_sem=recv_sems.at[7 * a + k], device_id=to, device_id_type=MESH)

        mine = [pltpu.make_async_copy(x_refs[a], rows(a, *me), local_sems.at[a]) for a in range(n)]
        for cp in mine:
            cp.start()
        first = []
        for a in range(n):
            first.append(copy(a, 0, me, sibling, src=x_refs[a]))
            first += [copy(a, 1 + j, me, (*chip, cc), src=x_refs[a]) for j, chip in enumerate(chips)]
        for cp in first:
            cp.start()
        passed = []
        for j, chip in enumerate(chips):
            for a in range(n):
                copy(a, 1 + j, (*chip, cc), me).wait_recv()
                fwd = copy(a, 4 + j, (*chip, cc), sibling)
                fwd.start()
                passed.append(fwd)
        for a in range(n):
            copy(a, 0, sibling, me).wait_recv()
        for j, chip in enumerate(chips):
            for a in range(n):
                copy(a, 4 + j, (*chip, 1 - cc), me).wait_recv()
        for cp in first + passed:
            cp.wait_send()
        for cp in mine:
            cp.wait()

    return pl.pallas_call(
        body, out_shape=tuple(SDS((N_DEV,) + a.shape, a.dtype) for a in xs), in_specs=[ANY] * n, out_specs=(ANY,) * n,
        scratch_shapes=[pltpu.SemaphoreType.DMA((7 * n,)), pltpu.SemaphoreType.DMA((7 * n,)),
                        pltpu.SemaphoreType.DMA((n,))],
        name=name,
    )(*xs)


def _allgather_seq(name, xs, collective_id):
    n = len(xs)
    x_refs = [jax.new_ref(a, memory_space=pltpu.MemorySpace.HBM) for a in xs]
    out_refs = [jax.empty_ref(SDS((N_DEV,) + a.shape, a.dtype), memory_space=pltpu.MemorySpace.HBM) for a in xs]

    @pl.kernel(mesh=plsc.ScalarSubcoreMesh(axis_name="seq", num_cores=1), name=name,
               scratch_types=(pltpu.SemaphoreType.DMA((7 * n,)), pltpu.SemaphoreType.DMA((7 * n,)),
                              pltpu.SemaphoreType.DMA((n,))),
               compiler_params=pltpu.CompilerParams(collective_id=collective_id))
    def launch(send_sems, recv_sems, local_sems):
        x, y, cc = lax.axis_index("x"), lax.axis_index("y"), lax.axis_index("c")
        me, sibling = (x, y, cc), (x, y, 1 - cc)
        chips = [(1 - x, y), (x, 1 - y), (1 - x, 1 - y)]
        barrier = pltpu.get_barrier_semaphore()
        for peer in [sibling] + [(*chip, cc) for chip in chips]:
            pl.semaphore_signal(barrier, inc=1, device_id=peer, device_id_type=MESH)
        pl.semaphore_wait(barrier, 4)

        def rows(a, px, py, pc):
            return out_refs[a].at[4 * px + 2 * py + pc]

        def copy(a, k, block, to, src=None):
            return pltpu.make_async_remote_copy(
                src_ref=rows(a, *block) if src is None else src, dst_ref=rows(a, *block),
                send_sem=send_sems.at[7 * a + k], recv_sem=recv_sems.at[7 * a + k], device_id=to, device_id_type=MESH)

        mine = [pltpu.make_async_copy(x_refs[a], rows(a, *me), local_sems.at[a]) for a in range(n)]
        for cp in mine:
            cp.start()
        first = []
        for a in range(n):
            first.append(copy(a, 0, me, sibling, src=x_refs[a]))
            first += [copy(a, 1 + j, me, (*chip, cc), src=x_refs[a]) for j, chip in enumerate(chips)]
        for cp in first:
            cp.start()
        passed = []
        for j, chip in enumerate(chips):
            for a in range(n):
                copy(a, 1 + j, (*chip, cc), me).wait_recv()
                fwd = copy(a, 4 + j, (*chip, cc), sibling)
                fwd.start()
                passed.append(fwd)
        for a in range(n):
            copy(a, 0, sibling, me).wait_recv()
        for j, chip in enumerate(chips):
            for a in range(n):
                copy(a, 4 + j, (*chip, 1 - cc), me).wait_recv()
        for cp in first + passed:
            cp.wait_send()
        for cp in mine:
            cp.wait()

    launch()
    return [r[...] for r in out_refs]


def _sibling_send(name, xs):
    n = len(xs)

    def body(*refs):
        x_refs, got_refs = refs[:n], refs[n: 2 * n]
        send_sems, recv_sems = refs[2 * n:]
        x, y, cc = lax.axis_index("x"), lax.axis_index("y"), lax.axis_index("c")
        copies = []
        for a in range(n):
            for q in range(4):
                k = 4 * a + q
                copies.append(pltpu.make_async_remote_copy(
                    src_ref=x_refs[a].at[2 * q + 1 - cc], dst_ref=got_refs[a].at[q], send_sem=send_sems.at[k],
                    recv_sem=recv_sems.at[k], device_id=(x, y, 1 - cc), device_id_type=MESH))
        for cp in copies:
            cp.start()
        for cp in copies:
            cp.wait()

    return pl.pallas_call(
        body, out_shape=tuple(SDS((4,) + a.shape[1:], a.dtype) for a in xs), in_specs=[ANY] * n, out_specs=(ANY,) * n,
        scratch_shapes=[pltpu.SemaphoreType.DMA((4 * n,)), pltpu.SemaphoreType.DMA((4 * n,))],
        name=name,
    )(*xs)


def _chip_exchange(name, xs):
    n = len(xs)

    def body(*refs):
        x_refs, out_refs = refs[:n], refs[n: 2 * n]
        send_sems, recv_sems, local_sems = refs[2 * n:]
        x, y, cc = lax.axis_index("x"), lax.axis_index("y"), lax.axis_index("c")
        my_chip = 2 * x + y
        chips = [(1 - x, y), (x, 1 - y), (1 - x, 1 - y)]
        mine = [pltpu.make_async_copy(x_refs[a].at[my_chip], out_refs[a].at[my_chip], local_sems.at[a]) for a in range(n)]
        for cp in mine:
            cp.start()

        def copy(a, j, dst_slot):
            px, py = chips[j]
            return pltpu.make_async_remote_copy(
                src_ref=x_refs[a].at[2 * px + py], dst_ref=out_refs[a].at[dst_slot], send_sem=send_sems.at[3 * a + j],
                recv_sem=recv_sems.at[3 * a + j], device_id=(px, py, cc), device_id_type=MESH)

        sends = [copy(a, j, my_chip) for a in range(n) for j in range(3)]
        for cp in sends:
            cp.start()
        for a in range(n):
            for j, (px, py) in enumerate(chips):
                copy(a, j, 2 * px + py).wait_recv()
        for cp in sends:
            cp.wait_send()
        for cp in mine:
            cp.wait()

    return pl.pallas_call(
        body, out_shape=tuple(SDS(a.shape, a.dtype) for a in xs), in_specs=[ANY] * n, out_specs=(ANY,) * n,
        scratch_shapes=[pltpu.SemaphoreType.DMA((3 * n,)), pltpu.SemaphoreType.DMA((3 * n,)),
                        pltpu.SemaphoreType.DMA((n,))],
        name=name,
    )(*xs)


BIG = (("w_in", 1), ("w_ffn_in", 1), ("w_proj_ssm", 0), ("w_proj_gdn", 0), ("w_out", 0), ("w_ffn_down", 0))
CONVS = (("ssm_conv_w", 1), ("gdn_conv_w", 1))


def _to_dest_major(full, axis):
    a, b = full.shape
    if axis == 0:
        return full.reshape(N_DEV, a // N_DEV, b)
    return full.reshape(a, N_DEV, b // N_DEV).transpose(1, 0, 2)


def _from_gathered(g, axis):
    if axis == 0:
        return g.reshape(-1, g.shape[2])
    return g.transpose(1, 0, 2).reshape(g.shape[1], -1)


def _permute_in_cols(w):
    sm = jnp.concatenate([w[..., O_DT: O_DT + 16], w[..., O_A: O_A + 8], w[..., O_B: O_B + 8]], axis=-1)
    pad_sm = jnp.zeros(w.shape[:-1] + (128 - 32,), w.dtype)
    pad_end = jnp.zeros(w.shape[:-1] + (PROJ_W - SM_OFF - 128,), w.dtype)
    return jnp.concatenate([w[..., O_Z: O_Z + 1024], w[..., O_GZ: O_GZ + 1024], w[..., O_G1: O_G1 + 1024],
                            w[..., O_G2: O_G2 + 1024], w[..., O_QKV: O_QKV + 3072], w[..., O_XBC: O_XBC + 1536],
                            sm, pad_sm, pad_end], axis=-1)


def _unpermute_in_cols(w):
    return jnp.concatenate([w[..., Z_OFF: Z_OFF + 1024], w[..., XBC_OFF: XBC_OFF + 1536], w[..., SM_OFF: SM_OFF + 16],
                            w[..., QKV_OFF: QKV_OFF + 3072], w[..., GZ_OFF: GZ_OFF + 1024],
                            w[..., SM_OFF + LANE_A: SM_OFF + LANE_A + 8], w[..., SM_OFF + LANE_B: SM_OFF + LANE_B + 8],
                            w[..., G1_OFF: G1_OFF + 1024], w[..., G2_OFF: G2_OFF + 1024]], axis=-1)


def _pad128(v, lane0):
    return jnp.zeros((1, 128), F32).at[0, lane0: lane0 + v.shape[0]].set(v)


def _layer_consts(p):
    return dict(
        dtb=_pad128(p["ssm_dt_bias"], 0), arow=_pad128(-jnp.exp(p["ssm_a_log"]), 0),
        dxrow=jnp.repeat(p["ssm_d"], SSM_P).reshape(1, 1024), snw=p["ssm_norm_w"].reshape(1, 1024),
        gb=_pad128(p["gdn_dt_bias"], LANE_A), garow=_pad128(-jnp.exp(p["gdn_a_log"]), LANE_A),
        gnw=p["gdn_norm_w"].reshape(1, 128), zb=jnp.zeros((1, GDN_QKV), F32), scb=p["ssm_conv_b"].reshape(1, SSM_CONV))


def _expand_matrix():
    row = lax.broadcasted_iota(jnp.int32, (128, 1024), 0)
    col = lax.broadcasted_iota(jnp.int32, (128, 1024), 1)
    return (col // SSM_P == row).astype(BF16)


def _silu_mul_epi(acc, up):
    g = acc
    return g, g * _sigmoid(g) * up.astype(F32)


def _merge_epi(acc, p1, g1, g2):
    return acc, _sigmoid(g1) * p1.astype(F32) + _sigmoid(g2) * acc


def _add_epi(acc, res):
    return (acc + res,)


def _ffn_bwd_epi(acc, gate, up):
    g = gate.astype(F32)
    sg = _sigmoid(g)
    return acc * up.astype(F32) * (sg * (1.0 + g * (1.0 - sg))), acc * (g * sg)


def _merge_bwd_epi(acc, g1, g2, p1, p2):
    s1, s2 = _sigmoid(g1), _sigmoid(g2)
    return acc * s1, acc * s2, acc * p1.astype(F32) * (s1 * (1.0 - s1)), acc * p2.astype(F32) * (s2 * (1.0 - s2))


def _layer_fwd(l, x, p, rmat):
    t = x.shape[0]
    n = f"l{l}_"
    k = _layer_consts(p)
    h = _rmsnorm_fwd(n + "norm_mix", x, p["norm_mix_w"])
    proj = _matmul(n + "in_proj", "nn", [(h, 0, p["w_in"], 0)], t, PROJ_W, 1024, 1024, 1280, 1024, (F32,))
    act_g = _conv_fwd(n + "conv_gdn", proj, QKV_OFF, p["gdn_conv_w"], k["zb"])
    act_s = _conv_fwd(n + "conv_ssm", proj, XBC_OFF, p["ssm_conv_w"], k["scb"])
    y_ssm, ysc, st_s = _ssd_fwd(n + "ssd_fwd", act_s, proj, k["dtb"], k["arow"], k["dxrow"], k["snw"], rmat)
    y_gdn, oraw, st_g = _gdn_fwd(n + "gdn_fwd", act_g, proj, k["gb"], k["garow"], k["gnw"])
    p1 = _matmul(n + "proj_ssm", "nn", [(y_ssm, 0, p["w_proj_ssm"], 0)], t, 1024, 1024, 1024, 1024, 1024, (BF16,))
    p2, merged = _matmul(n + "proj_gdn_merge", "nn", [(y_gdn, 0, p["w_proj_gdn"], 0)], t, 1024, 1024, 512, 1024, 1024,
                         (BF16, BF16), epi=_merge_epi, extras=[(p1, 0), (proj, G1_OFF // 1024), (proj, G2_OFF // 1024)])
    x1 = _matmul(n + "out_proj", "nn", [(merged, 0, p["w_out"], 0)], t, 1024, 1024, 1024, 1024, 1024, (F32,),
                 epi=_add_epi, extras=[(x, 0)])
    h2 = _rmsnorm_fwd(n + "norm_ffn", x1, p["norm_ffn_w"])
    up = _matmul(n + "ffn_up", "nn", [(h2, 0, p["w_ffn_in"], 2)], t, FFN, 1024, 512, FFN // 2, 1024, (BF16,))
    gate, act = _matmul(n + "ffn_gate", "nn", [(h2, 0, p["w_ffn_in"], 0)], t, FFN, 1024, 512, FFN // 2, 1024, (BF16, BF16),
                        epi=_silu_mul_epi, extras=[(up, 0)])
    x2 = _matmul(n + "ffn_down", "nn", [(act, 0, p["w_ffn_down"], 0)], t, 1024, FFN, 512, 1024, FFN, (F32,),
                 epi=_add_epi, extras=[(x1, 0)])
    saved = dict(x=x, h=h, proj=proj, act_g=act_g, act_s=act_s, y_ssm=y_ssm, ysc=ysc, st_s=st_s, y_gdn=y_gdn, oraw=oraw,
                 st_g=st_g, p1=p1, p2=p2, merged=merged, x1=x1, h2=h2, up=up, gate=gate, act=act, k=k)
    return x2, saved


def _layer_bwd(l, dx2, dx2b, s, p, rmat):
    t = dx2.shape[0]
    n = f"l{l}_"
    k = s["k"]
    tk_tok = 1024
    hf = FFN // 2
    g = {}
    dgate, dup = _matmul(n + "d_ffn_act", "nt", [(dx2b, 0, p["w_ffn_down"], 0)], t, FFN, 1024, 512, hf, 1024, (BF16, BF16),
                         epi=_ffn_bwd_epi, extras=[(s["gate"], 0), (s["up"], 0)])
    g["w_ffn_down"] = _matmul(n + "dw_ffn_down", "tn", [(s["act"], 0, dx2b, 0)], FFN, 1024, t, hf, 1024, tk_tok, (F32,))
    dh2 = _matmul(n + "d_ffn_in", "nt", [(dgate, 0, p["w_ffn_in"], 0), (dup, 0, p["w_ffn_in"], 2)], t, 1024, FFN,
                  512, 1024, hf, (F32,))
    dwg = _matmul(n + "dw_ffn_gate", "tn", [(s["h2"], 0, dgate, 0)], 1024, FFN, t, 1024, hf, tk_tok, (F32,))
    dwu = _matmul(n + "dw_ffn_up", "tn", [(s["h2"], 0, dup, 0)], 1024, FFN, t, 1024, hf, tk_tok, (F32,))
    g["w_ffn_in"] = jnp.concatenate([dwg, dwu], axis=1)
    dx1, dx1b, g["norm_ffn_w"] = _rmsnorm_bwd(n + "d_norm_ffn", s["x1"], p["norm_ffn_w"], dh2, dx2)
    dp1, dp2, dg1, dg2 = _matmul(
        n + "d_out_proj", "nt", [(dx1b, 0, p["w_out"], 0)], t, 1024, 1024, 512, 1024, 1024, (BF16,) * 4, epi=_merge_bwd_epi,
        extras=[(s["proj"], G1_OFF // 1024), (s["proj"], G2_OFF // 1024), (s["p1"], 0), (s["p2"], 0)])
    g["w_out"] = _matmul(n + "dw_out", "tn", [(s["merged"], 0, dx1b, 0)], 1024, 1024, t, 1024, 1024, tk_tok, (F32,))
    dy_ssm = _matmul(n + "d_proj_ssm", "nt", [(dp1, 0, p["w_proj_ssm"], 0)], t, 1024, 1024, 1024, 1024, 1024, (F32,))
    g["w_proj_ssm"] = _matmul(n + "dw_proj_ssm", "tn", [(s["y_ssm"], 0, dp1, 0)], 1024, 1024, t, 1024, 1024, tk_tok, (F32,))
    dy_gdn = _matmul(n + "d_proj_gdn", "nt", [(dp2, 0, p["w_proj_gdn"], 0)], t, 1024, 1024, 1024, 1024, 1024, (F32,))
    g["w_proj_gdn"] = _matmul(n + "dw_proj_gdn", "tn", [(s["y_gdn"], 0, dp2, 0)], 1024, 1024, t, 1024, 1024, tk_tok, (F32,))
    dact_s, dz, dsm_s, dsnw, dd, dal, ddtb = _ssd_bwd(n + "ssd_bwd", s["act_s"], s["proj"], k["dtb"], k["arow"], k["dxrow"],
                                                        k["snw"], rmat, s["ysc"], s["st_s"], dy_ssm)
    dact_g, dgz, dsm_g, dgnw, dgal, dgb = _gdn_bwd(n + "gdn_bwd", s["act_g"], s["proj"], k["gb"], k["garow"], k["gnw"],
                                                     s["oraw"], s["st_g"], dy_gdn)
    du_s, g["ssm_conv_w"], dcb = _conv_bwd(n + "d_conv_ssm", s["proj"], XBC_OFF, p["ssm_conv_w"], k["scb"], dact_s)
    du_g, g["gdn_conv_w"], _ = _conv_bwd(n + "d_conv_gdn", s["proj"], QKV_OFF, p["gdn_conv_w"], k["zb"], dact_g)
    g["ssm_conv_b"] = dcb.reshape(-1)
    g["ssm_norm_w"] = dsnw.reshape(-1)
    g["ssm_d"] = dd[0, :SSM_HEADS]
    g["ssm_a_log"] = dal[0, :SSM_HEADS]
    g["ssm_dt_bias"] = ddtb[0, :SSM_HEADS]
    g["gdn_norm_w"] = dgnw.reshape(-1)
    g["gdn_a_log"] = dgal[0, LANE_A: LANE_A + GDN_HEADS]
    g["gdn_dt_bias"] = dgb[0, LANE_A: LANE_A + GDN_HEADS]
    dproj = jnp.concatenate([dz, dgz, dg1, dg2, du_g, du_s, dsm_s + dsm_g, jnp.zeros((t, PROJ_W - SM_OFF - 128), BF16)],
                            axis=1)
    dh = _matmul(n + "d_in_proj", "nt", [(dproj, 0, p["w_in"], 0)], t, 1024, PROJ_W, 1024, 1024, 1280, (F32,))
    g["w_in"] = _matmul(n + "dw_in", "tn", [(s["h"], 0, dproj, 0)], 1024, PROJ_W, t, 1024, 1280, tk_tok, (F32,))
    dx, dxb, g["norm_mix_w"] = _rmsnorm_bwd(n + "d_norm_mix", s["x"], p["norm_mix_w"], dh, dx1)
    g["norm_mix_w"] = g["norm_mix_w"].reshape(-1)
    g["norm_ffn_w"] = g["norm_ffn_w"].reshape(-1)
    return dx, dxb, g


def _local_step(x, tgt, layers, final_norm_w):
    rmat = _expand_matrix()
    saved = []
    for l in range(DEPTH):
        x, s = _layer_fwd(l, x, layers[l], rmat)
        saved.append(s)
    loss, dx, dxb, dfw = _loss_head("loss_head", x, final_norm_w, tgt)
    grads = [None] * DEPTH
    for l in reversed(range(DEPTH)):
        dx, dxb, grads[l] = _layer_bwd(l, dx, dxb, saved[l], layers[l], rmat)
    return loss[0, 0], dx, grads, dfw.reshape(-1)


SMALL = ("norm_mix_w", "ssm_conv_b", "ssm_dt_bias", "ssm_a_log", "ssm_d", "ssm_norm_w", "gdn_a_log", "gdn_dt_bias",
         "gdn_norm_w", "norm_ffn_w")
WEIGHTS = ("norm_mix_w", "w_in", "ssm_conv_w", "ssm_conv_b", "ssm_dt_bias", "ssm_a_log", "ssm_d", "ssm_norm_w", "gdn_conv_w",
           "gdn_a_log", "gdn_dt_bias", "gdn_norm_w", "w_proj_ssm", "w_proj_gdn", "w_out", "norm_ffn_w", "w_ffn_in",
           "w_ffn_down", "final_norm_w")


def _gather_layer(l, w):
    specs = BIG + CONVS
    shards = [w[nm][l].astype(BF16) for nm, _ in BIG] + [w[nm][l] for nm, _ in CONVS]
    gathered = _allgather_seq(f"l{l}_gather_weights", shards, collective_id=l)
    out = {nm: _from_gathered(g, axis) for (nm, axis), g in zip(specs, gathered)}
    out["w_in"] = _permute_in_cols(out["w_in"])
    return out


def _reduce_scatter_layer(l, grads):
    specs = BIG + CONVS
    blocks = [_to_dest_major(grads[nm], axis) for nm, axis in specs]
    cc = lax.axis_index("c")
    keep = [lax.dynamic_index_in_dim(b.reshape((4, 2) + b.shape[1:]), cc, axis=1, keepdims=False) for b in blocks]
    got = _sibling_send(f"l{l}_grads_to_sibling", blocks)
    chip_sums = [_sum_terms(f"l{l}_chip_sum_{nm}", [(k[None], 0), (g[None], 0)], BF16)
                 for (nm, _), k, g in zip(specs, keep, got)]
    exchanged = _chip_exchange(f"l{l}_grads_between_chips", chip_sums)
    return {nm: _sum_terms(f"l{l}_total_{nm}", [(e, q) for q in range(4)], F32) for (nm, _), e in zip(specs, exchanged)}


def _allreduce_small(vecs):
    flat = jnp.concatenate(vecs)
    n = flat.shape[0]
    rows = -(-n // 128)
    rows = -(-rows // 8) * 8
    buf = jnp.pad(flat, (0, rows * 128 - n)).reshape(rows, 128)
    (allv,) = _allgather("gather_small_grads", [buf])
    tot = _sum_terms("small_grads_total", [(allv, d) for d in range(N_DEV)], F32).reshape(-1)
    out, o = [], 0
    for v in vecs:
        out.append(tot[o: o + v.shape[0]])
        o += v.shape[0]
    return out


def kernel(x, norm_mix_w, w_in, ssm_conv_w, ssm_conv_b, ssm_dt_bias, ssm_a_log, ssm_d, ssm_norm_w, gdn_conv_w, gdn_a_log, gdn_dt_bias, gdn_norm_w, w_proj_ssm, w_proj_gdn, w_out, norm_ffn_w, w_ffn_in, w_ffn_down, final_norm_w, loss_target, m_norm_mix_w, m_w_in, m_ssm_conv_w, m_ssm_conv_b, m_ssm_dt_bias, m_ssm_a_log, m_ssm_d, m_ssm_norm_w, m_gdn_conv_w, m_gdn_a_log, m_gdn_dt_bias, m_gdn_norm_w, m_w_proj_ssm, m_w_proj_gdn, m_w_out, m_norm_ffn_w, m_w_ffn_in, m_w_ffn_down, m_final_norm_w, v_norm_mix_w, v_w_in, v_ssm_conv_w, v_ssm_conv_b, v_ssm_dt_bias, v_ssm_a_log, v_ssm_d, v_ssm_norm_w, v_gdn_conv_w, v_gdn_a_log, v_gdn_dt_bias, v_gdn_norm_w, v_w_proj_ssm, v_w_proj_gdn, v_w_out, v_norm_ffn_w, v_w_ffn_in, v_w_ffn_down, v_final_norm_w):
    w = dict(norm_mix_w=norm_mix_w, w_in=w_in, ssm_conv_w=ssm_conv_w, ssm_conv_b=ssm_conv_b, ssm_dt_bias=ssm_dt_bias,
             ssm_a_log=ssm_a_log, ssm_d=ssm_d, ssm_norm_w=ssm_norm_w, gdn_conv_w=gdn_conv_w, gdn_a_log=gdn_a_log,
             gdn_dt_bias=gdn_dt_bias, gdn_norm_w=gdn_norm_w, w_proj_ssm=w_proj_ssm, w_proj_gdn=w_proj_gdn, w_out=w_out,
             norm_ffn_w=norm_ffn_w, w_ffn_in=w_ffn_in, w_ffn_down=w_ffn_down, final_norm_w=final_norm_w)
    m = dict(norm_mix_w=m_norm_mix_w, w_in=m_w_in, ssm_conv_w=m_ssm_conv_w, ssm_conv_b=m_ssm_conv_b, ssm_dt_bias=m_ssm_dt_bias,
             ssm_a_log=m_ssm_a_log, ssm_d=m_ssm_d, ssm_norm_w=m_ssm_norm_w, gdn_conv_w=m_gdn_conv_w, gdn_a_log=m_gdn_a_log,
             gdn_dt_bias=m_gdn_dt_bias, gdn_norm_w=m_gdn_norm_w, w_proj_ssm=m_w_proj_ssm, w_proj_gdn=m_w_proj_gdn,
             w_out=m_w_out, norm_ffn_w=m_norm_ffn_w, w_ffn_in=m_w_ffn_in, w_ffn_down=m_w_ffn_down,
             final_norm_w=m_final_norm_w)
    v = dict(norm_mix_w=v_norm_mix_w, w_in=v_w_in, ssm_conv_w=v_ssm_conv_w, ssm_conv_b=v_ssm_conv_b, ssm_dt_bias=v_ssm_dt_bias,
             ssm_a_log=v_ssm_a_log, ssm_d=v_ssm_d, ssm_norm_w=v_ssm_norm_w, gdn_conv_w=v_gdn_conv_w, gdn_a_log=v_gdn_a_log,
             gdn_dt_bias=v_gdn_dt_bias, gdn_norm_w=v_gdn_norm_w, w_proj_ssm=v_w_proj_ssm, w_proj_gdn=v_w_proj_gdn,
             w_out=v_w_out, norm_ffn_w=v_norm_ffn_w, w_ffn_in=v_w_ffn_in, w_ffn_down=v_w_ffn_down,
             final_norm_w=v_final_norm_w)

    layers = []
    for l in range(DEPTH):
        lp = _gather_layer(l, w)
        lp.update({nm: w[nm][l] for nm in SMALL})
        layers.append(lp)
    loss_part, dx, lgrads, dfw = _local_step(x[0], loss_target[0], layers, final_norm_w)
    loss = lax.psum(loss_part, ("x", "y", "c"))

    shard_grads = []
    for l in range(DEPTH):
        lg = {nm: lgrads[l][nm] for nm, _ in BIG + CONVS}
        lg["w_in"] = _unpermute_in_cols(lg["w_in"])
        shard_grads.append(_reduce_scatter_layer(l, lg))
    grad = {nm: jnp.stack([shard_grads[l][nm] for l in range(DEPTH)]) for nm, _ in BIG + CONVS}
    small_vecs = [lgrads[l][nm].reshape(-1) for l in range(DEPTH) for nm in SMALL] + [dfw]
    small_sum = _allreduce_small(small_vecs)
    for i, nm in enumerate(SMALL):
        grad[nm] = jnp.stack([small_sum[l * len(SMALL) + i].reshape(w[nm].shape[1:]) for l in range(DEPTH)])
    grad["final_norm_w"] = small_sum[-1]

    deltas, new_m, new_v = {}, {}, {}
    for nm in WEIGHTS:
        deltas[nm], new_m[nm], new_v[nm] = _adamw("adamw_" + nm, w[nm], grad[nm], m[nm], v[nm])
    return (loss, dx[None], *[grad[nm] for nm in WEIGHTS], *[deltas[nm] for nm in WEIGHTS],
            *[new_m[nm] for nm in WEIGHTS], *[new_v[nm] for nm in WEIGHTS])
```

```python
import functools

import jax
import jax.numpy as jnp
from jax import lax
from jax.experimental import pallas as pl
from jax.experimental.pallas import tpu as pltpu
from jax.experimental.pallas import tpu_sc as plsc

F32 = jnp.float32
BF16 = jnp.bfloat16
HI = lax.Precision.HIGHEST
SDS = jax.ShapeDtypeStruct

D_MODEL = 1024
DEPTH = 2
SSM_HEADS = 16
SSM_P = 64
SSM_N = 128
SSM_GROUPS = 2
SSM_CONV = 1536
GDN_HEADS = 8
GDN_DK = 128
GDN_QKV = 3072
CONV_K = 4
CHUNK = 64
FFN = 2816
IN_DIM = 8736
EPS = 1e-6
N_DEV = 8

Z_OFF = 0
GZ_OFF = 1024
G1_OFF = 2048
G2_OFF = 3072
QKV_OFF = 4096
XBC_OFF = 7168
SM_OFF = 8704
PROJ_W = 8960
LANE_A = 16
LANE_B = 24
O_Z, O_XBC, O_DT, O_QKV, O_GZ, O_A, O_B, O_G1, O_G2 = 0, 1024, 2560, 2576, 5648, 6672, 6680, 6688, 7712

ADAM_LR = 0.001
ADAM_B1 = 0.9
ADAM_B2 = 0.999
ADAM_EPS = 1e-08
ADAM_WD = 0.01
ADAM_STEP = 10

V7X_VMEM_LIMIT = 48 * 1024 * 1024

NN = ((1,), (0,))
NT = ((1,), (1,))
TN = ((0,), (0,))


def _bdot(a, b, dims):
    return lax.dot_general(a.astype(BF16), b.astype(BF16), (dims, ((), ())), preferred_element_type=F32)


def _hdot(a, b, dims=NN):
    return lax.dot_general(a, b, (dims, ((), ())), precision=HI, preferred_element_type=F32)


def _sigmoid(x):
    return 1.0 / (1.0 + jnp.exp(-x))


def _softplus(x):
    return jnp.maximum(x, 0.0) + jnp.log(1.0 + jnp.exp(-jnp.abs(x)))


def _params(dims):
    return pltpu.CompilerParams(dimension_semantics=dims, vmem_limit_bytes=V7X_VMEM_LIMIT)


def _rowsum(x):
    return jnp.sum(x, axis=-1, keepdims=True)


def _colsum(x):
    return jnp.sum(x, axis=0, keepdims=True)


def _matmul(name, mode, pairs, m, n, kdim, tm, tn, tk, out_dtypes, epi=None, extras=()):
    tm, tn, tk = min(tm, m), min(tn, n), min(tk, kdim)
    nk = kdim // tk
    assert m % tm == 0 and n % tn == 0 and kdim % tk == 0, (name, m, n, kdim, tm, tn, tk)
    in_specs, args = [], []
    for a, a_off, b, b_off in pairs:
        if mode == "nn":
            in_specs.append(pl.BlockSpec((tm, tk), lambda i, j, k, o=a_off: (i, k + o)))
            in_specs.append(pl.BlockSpec((tk, tn), lambda i, j, k, o=b_off: (k, j + o)))
            dims = NN
        elif mode == "nt":
            in_specs.append(pl.BlockSpec((tm, tk), lambda i, j, k, o=a_off: (i, k + o)))
            in_specs.append(pl.BlockSpec((tn, tk), lambda i, j, k, o=b_off: (j, k + o)))
            dims = NT
        else:
            in_specs.append(pl.BlockSpec((tk, tm), lambda i, j, k, o=a_off: (k, i + o)))
            in_specs.append(pl.BlockSpec((tk, tn), lambda i, j, k, o=b_off: (k, j + o)))
            dims = TN
        args += [a, b]
    for e, e_off in extras:
        in_specs.append(pl.BlockSpec((tm, tn), lambda i, j, k, o=e_off: (i, j + o)))
        args.append(e)
    npair, nex, nout = len(pairs), len(extras), len(out_dtypes)

    def body(*refs):
        prefs = refs[: 2 * npair]
        erefs = refs[2 * npair: 2 * npair + nex]
        orefs = refs[2 * npair + nex: 2 * npair + nex + nout]

        def finish(res):
            outs = (res,) if epi is None else epi(res, *[e[...] for e in erefs])
            for o, r in zip(orefs, outs):
                o[...] = r.astype(o.dtype)

        s = _bdot(prefs[0][...], prefs[1][...], dims)
        for p in range(1, npair):
            s = s + _bdot(prefs[2 * p][...], prefs[2 * p + 1][...], dims)
        if nk == 1:
            finish(s)
            return
        acc = refs[-1]
        k = pl.program_id(2)

        @pl.when(k == 0)
        def _():
            acc[...] = s

        @pl.when(k > 0)
        def _():
            acc[...] += s

        @pl.when(k == nk - 1)
        def _():
            finish(acc[...])

    out_shape = tuple(SDS((m, n), dt) for dt in out_dtypes)
    out_specs = tuple(pl.BlockSpec((tm, tn), lambda i, j, k: (i, j)) for _ in out_dtypes)
    res = pl.pallas_call(
        body, grid=(m // tm, n // tn, nk), in_specs=in_specs, out_specs=out_specs, out_shape=out_shape,
        scratch_shapes=[pltpu.VMEM((tm, tn), F32)] if nk > 1 else [], name=name,
        compiler_params=_params(("parallel", "parallel", "arbitrary")),
    )(*args)
    return res if nout > 1 else res[0]


def _rmsnorm_fwd(name, x, w):
    t, d = x.shape
    tm = min(512, t)

    def body(x_ref, w_ref, h_ref):
        xv = x_ref[...]
        r = lax.rsqrt(jnp.mean(xv * xv, axis=-1, keepdims=True) + EPS)
        h_ref[...] = (xv * r * w_ref[...]).astype(BF16)

    return pl.pallas_call(
        body, grid=(t // tm,),
        in_specs=[pl.BlockSpec((tm, d), lambda i: (i, 0)), pl.BlockSpec((1, d), lambda i: (0, 0))],
        out_specs=pl.BlockSpec((tm, d), lambda i: (i, 0)), out_shape=SDS((t, d), BF16), name=name,
        compiler_params=_params(("parallel",)),
    )(x, w.reshape(1, d))


def _rmsnorm_bwd(name, x, w, dh, dres):
    t, d = x.shape
    tm = min(512, t)

    def body(x_ref, w_ref, dh_ref, dres_ref, dx_ref, dxb_ref, dw_ref):
        xv = x_ref[...]
        r = lax.rsqrt(jnp.mean(xv * xv, axis=-1, keepdims=True) + EPS)
        xh = xv * r
        dhv = dh_ref[...].astype(F32)
        dxh = dhv * w_ref[...]
        dx = r * (dxh - xh * jnp.mean(dxh * xh, axis=-1, keepdims=True)) + dres_ref[...]
        dx_ref[...] = dx
        dxb_ref[...] = dx.astype(BF16)

        @pl.when(pl.program_id(0) == 0)
        def _():
            dw_ref[...] = jnp.zeros_like(dw_ref)

        dw_ref[...] += _colsum(dhv * xh)

    row = pl.BlockSpec((tm, d), lambda i: (i, 0))
    vec = pl.BlockSpec((1, d), lambda i: (0, 0))
    return pl.pallas_call(
        body, grid=(t // tm,), in_specs=[row, vec, row, row], out_specs=(row, row, vec),
        out_shape=(SDS((t, d), F32), SDS((t, d), BF16), SDS((1, d), F32)), name=name,
        compiler_params=_params(("arbitrary",)),
    )(x, w.reshape(1, d), dh, dres)


def _loss_head(name, x, w, tgt):
    t, d = x.shape
    tm = min(512, t)

    def body(x_ref, w_ref, t_ref, loss_ref, dx_ref, dxb_ref, dw_ref):
        xv = x_ref[...]
        wv = w_ref[...]
        r = lax.rsqrt(jnp.mean(xv * xv, axis=-1, keepdims=True) + EPS)
        xh = xv * r
        e = xh * wv - t_ref[...]
        dy = e * (1.0 / d)
        dxh = dy * wv
        dx = r * (dxh - xh * jnp.mean(dxh * xh, axis=-1, keepdims=True))
        dx_ref[...] = dx
        dxb_ref[...] = dx.astype(BF16)

        @pl.when(pl.program_id(0) == 0)
        def _():
            dw_ref[...] = jnp.zeros_like(dw_ref)
            loss_ref[...] = jnp.zeros_like(loss_ref)

        dw_ref[...] += _colsum(dy * xh)
        loss_ref[...] += 0.5 * jnp.sum(jnp.mean(e * e, axis=-1, keepdims=True), axis=0, keepdims=True)

    row = pl.BlockSpec((tm, d), lambda i: (i, 0))
    vec = pl.BlockSpec((1, d), lambda i: (0, 0))
    return pl.pallas_call(
        body, grid=(t // tm,), in_specs=[row, vec, row],
        out_specs=(pl.BlockSpec((1, 1), lambda i: (0, 0)), row, row, vec),
        out_shape=(SDS((1, 1), F32), SDS((t, d), F32), SDS((t, d), BF16), SDS((1, d), F32)), name=name,
        compiler_params=_params(("arbitrary",)),
    )(x, w.reshape(1, d), tgt)


def _shift_down(u, s, row):
    return jnp.where(row >= s, pltpu.roll(u, shift=s, axis=0), 0.0)


def _conv_fwd(name, src, col0, w, b):
    t = src.shape[0]
    c = w.shape[1]
    tc = 256
    assert c % tc == 0 and col0 % tc == 0

    def body(u_ref, w_ref, b_ref, o_ref):
        u = u_ref[...]
        wv = w_ref[...]
        row = lax.broadcasted_iota(jnp.int32, u.shape, 0)
        pre = b_ref[...] + wv[3:4, :] * u
        for s in range(1, CONV_K):
            pre = pre + wv[3 - s: 4 - s, :] * _shift_down(u, s, row)
        o_ref[...] = pre * _sigmoid(pre)

    return pl.pallas_call(
        body, grid=(c // tc,),
        in_specs=[pl.BlockSpec((t, tc), lambda j: (0, j + col0 // tc)), pl.BlockSpec((CONV_K, tc), lambda j: (0, j)),
                  pl.BlockSpec((1, tc), lambda j: (0, j))],
        out_specs=pl.BlockSpec((t, tc), lambda j: (0, j)), out_shape=SDS((t, c), F32), name=name,
        compiler_params=_params(("parallel",)),
    )(src, w, b)


def _conv_bwd(name, src, col0, w, b, dact):
    t = src.shape[0]
    c = w.shape[1]
    tc = 128

    def body(u_ref, w_ref, b_ref, da_ref, du_ref, dw_ref, db_ref):
        u = u_ref[...]
        wv = w_ref[...]
        row = lax.broadcasted_iota(jnp.int32, u.shape, 0)
        shifted = [u] + [_shift_down(u, s, row) for s in range(1, CONV_K)]
        pre = b_ref[...] + wv[3:4, :] * u
        for s in range(1, CONV_K):
            pre = pre + wv[3 - s: 4 - s, :] * shifted[s]
        sg = _sigmoid(pre)
        dpre = da_ref[...] * (sg * (1.0 + pre * (1.0 - sg)))
        du = wv[3:4, :] * dpre
        for s in range(1, CONV_K):
            du = du + wv[3 - s: 4 - s, :] * jnp.where(row < t - s, pltpu.roll(dpre, shift=t - s, axis=0), 0.0)
        du_ref[...] = du.astype(BF16)
        for s in range(CONV_K):
            dw_ref[3 - s: 4 - s, :] = _colsum(dpre * shifted[s])
        db_ref[...] = _colsum(dpre)

    return pl.pallas_call(
        body, grid=(c // tc,),
        in_specs=[pl.BlockSpec((t, tc), lambda j: (0, j + col0 // tc)), pl.BlockSpec((CONV_K, tc), lambda j: (0, j)),
                  pl.BlockSpec((1, tc), lambda j: (0, j)), pl.BlockSpec((t, tc), lambda j: (0, j))],
        out_specs=(pl.BlockSpec((t, tc), lambda j: (0, j)), pl.BlockSpec((CONV_K, tc), lambda j: (0, j)),
                   pl.BlockSpec((1, tc), lambda j: (0, j))),
        out_shape=(SDS((t, c), BF16), SDS((CONV_K, c), F32), SDS((1, c), F32)), name=name,
        compiler_params=_params(("parallel",)),
    )(src, w, b, dact)


def _tri(q):
    ii = lax.broadcasted_iota(jnp.int32, (q, q), 0)
    jj = lax.broadcasted_iota(jnp.int32, (q, q), 1)
    return ii, jj


def _dot01(x, r01, dims, terms=3):
    out, rem = None, x
    for i in range(terms):
        hi = rem.astype(BF16)
        d = lax.dot_general(hi, r01, (dims, ((), ())), preferred_element_type=F32)
        out = d if out is None else out + d
        if i + 1 < terms:
            rem = rem - hi.astype(F32)
    return out


def _ssd_common(act, sm, dtb, arow, rmat):
    q = CHUNK
    ii, jj = _tri(q)
    lane = lax.broadcasted_iota(jnp.int32, (q, 128), 1)
    m16 = lane < SSM_HEADS
    dt = jnp.where(m16, _softplus(sm + dtb), 0.0)
    a = dt * arow
    tril = (ii >= jj).astype(F32)
    triu = (ii <= jj).astype(F32)
    acum = _hdot(tril, a)
    acum_r = _hdot(a.T, triu)
    dtx = _dot01(dt, rmat, NN)
    acx = _dot01(acum, rmat, NN)
    ex = jnp.exp(acx)
    alx = acx[q - 1: q, :]
    dex = jnp.exp(alx - acx)
    xs = act[:, :1024]
    return dict(ii=ii, jj=jj, m16=m16, dt=dt, a=a, triu=triu, acum=acum, acum_r=acum_r, dtx=dtx, ex=ex, dex=dex,
                elx=jnp.exp(alx), xs=xs, x=xs * dtx)


def _ssd_lmat(cm, h):
    return jnp.where(cm["ii"] >= cm["jj"], jnp.exp(cm["acum"][:, h: h + 1] - cm["acum_r"][h: h + 1, :]), 0.0)


def _ssd_fwd(name, act, proj, dtb, arow, dxrow, nw, rmat):
    t = act.shape[0]
    q = CHUNK
    nc = t // q
    hg = SSM_HEADS // SSM_GROUPS
    gw = hg * SSM_P

    def body(act_ref, z_ref, sm_ref, dtb_ref, arow_ref, dx_ref, nw_ref, r_ref, y_ref, ys_ref, st_ref, s_scr, yd_scr):
        @pl.when(pl.program_id(0) == 0)
        def _():
            s_scr[...] = jnp.zeros_like(s_scr)

        s_all = s_scr[...]
        st_ref[0] = s_all
        actv = act_ref[...]
        cm = _ssd_common(actv, sm_ref[...], dtb_ref[...], arow_ref[...], r_ref[...])
        x = cm["x"]
        xd = x * cm["dex"]
        yoffs, snew = [], []
        for g in range(SSM_GROUPS):
            bg = actv[:, 1024 + g * SSM_N: 1024 + (g + 1) * SSM_N]
            cg = actv[:, 1280 + g * SSM_N: 1280 + (g + 1) * SSM_N]
            sg = s_all[:, g * gw: (g + 1) * gw]
            cb = _bdot(cg, bg, NT)
            yoffs.append(_bdot(cg, sg, NN))
            snew.append(_bdot(bg, xd[:, g * gw: (g + 1) * gw], TN))
            for r in range(hg):
                h = g * hg + r
                mm = cb * _ssd_lmat(cm, h)
                yd_scr[:, h * SSM_P: (h + 1) * SSM_P] = _bdot(mm, x[:, h * SSM_P: (h + 1) * SSM_P], NN)
        s_scr[...] = s_all * cm["elx"] + jnp.concatenate(snew, axis=1)
        ysc = yd_scr[...] + jnp.concatenate(yoffs, axis=1) * cm["ex"]
        ys_ref[...] = ysc
        zv = z_ref[...]
        yg = (ysc + dx_ref[...] * cm["xs"]) * (zv * _sigmoid(zv))
        nwv = nw_ref[...]
        for g in range(SSM_GROUPS):
            sl = yg[:, g * gw: (g + 1) * gw]
            rr = lax.rsqrt(jnp.mean(sl * sl, axis=-1, keepdims=True) + EPS)
            y_ref[:, g * gw: (g + 1) * gw] = (sl * rr * nwv[:, g * gw: (g + 1) * gw]).astype(BF16)

    vec128 = pl.BlockSpec((1, 128), lambda c: (0, 0))
    vec1k = pl.BlockSpec((1, 1024), lambda c: (0, 0))
    return pl.pallas_call(
        body, grid=(nc,),
        in_specs=[pl.BlockSpec((q, SSM_CONV), lambda c: (c, 0)), pl.BlockSpec((q, 1024), lambda c: (c, Z_OFF // 1024)),
                  pl.BlockSpec((q, 128), lambda c: (c, SM_OFF // 128)), vec128, vec128, vec1k, vec1k,
                  pl.BlockSpec((128, 1024), lambda c: (0, 0))],
        out_specs=(pl.BlockSpec((q, 1024), lambda c: (c, 0)), pl.BlockSpec((q, 1024), lambda c: (c, 0)),
                   pl.BlockSpec((1, 128, 1024), lambda c: (c, 0, 0))),
        out_shape=(SDS((t, 1024), BF16), SDS((t, 1024), F32), SDS((nc, 128, 1024), F32)),
        scratch_shapes=[pltpu.VMEM((128, 1024), F32), pltpu.VMEM((q, 1024), F32)], name=name,
        compiler_params=_params(("arbitrary",)),
    )(act, proj, proj, dtb, arow, dxrow, nw, rmat)


def _ssd_bwd(name, act, proj, dtb, arow, dxrow, nw, rmat, ysc, states, dy):
    t = act.shape[0]
    q = CHUNK
    nc = t // q
    hg = SSM_HEADS // SSM_GROUPS
    gw = hg * SSM_P

    def body(act_ref, z_ref, sm_ref, dtb_ref, arow_ref, dx_ref, nw_ref, r_ref, ys_ref, st_ref, dy_ref,
             dact_ref, dz_ref, dsm_ref, dnw_ref, dd_ref, dal_ref, ddtb_ref, ds_scr, dxd_scr):
        @pl.when(pl.program_id(0) == 0)
        def _():
            ds_scr[...] = jnp.zeros_like(ds_scr)
            dnw_ref[...] = jnp.zeros_like(dnw_ref)
            dd_ref[...] = jnp.zeros_like(dd_ref)
            dal_ref[...] = jnp.zeros_like(dal_ref)
            ddtb_ref[...] = jnp.zeros_like(ddtb_ref)

        actv = act_ref[...]
        smv = sm_ref[...]
        rmat_v = r_ref[...]
        cm = _ssd_common(actv, smv, dtb_ref[...], arow_ref[...], rmat_v)
        ii, jj = cm["ii"], cm["jj"]
        x, xs = cm["x"], cm["xs"]
        s_all = st_ref[0]
        dsn = ds_scr[...]
        ysv = ys_ref[...]
        dxr = dx_ref[...]
        y = ysv + dxr * xs
        zv = z_ref[...]
        sz = _sigmoid(zv)
        silz = zv * sz
        yg = y * silz
        dout = dy_ref[...]
        nwv = nw_ref[...]
        dyn = dout * nwv
        yn_parts, dyg_parts = [], []
        for g in range(SSM_GROUPS):
            sl = yg[:, g * gw: (g + 1) * gw]
            rr = lax.rsqrt(jnp.mean(sl * sl, axis=-1, keepdims=True) + EPS)
            yn = sl * rr
            dn = dyn[:, g * gw: (g + 1) * gw]
            yn_parts.append(yn)
            dyg_parts.append(rr * (dn - yn * jnp.mean(dn * yn, axis=-1, keepdims=True)))
        dnw_ref[...] += _colsum(dout * jnp.concatenate(yn_parts, axis=1))
        dyg = jnp.concatenate(dyg_parts, axis=1)
        dyv = dyg * silz
        dz_ref[...] = (dyg * y * (sz * (1.0 + zv * (1.0 - sz)))).astype(BF16)
        dd_ref[...] += _dot01(_colsum(dyv * xs), rmat_v, NT)
        dxs = dyv * dxr
        dcs = dyv * cm["ex"]
        xd = x * cm["dex"]
        dxst_parts, ds_parts, db_parts, dc_parts, yoff_parts, wcol_rows = [], [], [], [], [], []
        lane128 = lax.broadcasted_iota(jnp.int32, (q, 128), 1)
        wrow = jnp.zeros((q, 128), F32)
        for g in range(SSM_GROUPS):
            bg = actv[:, 1024 + g * SSM_N: 1024 + (g + 1) * SSM_N]
            cg = actv[:, 1280 + g * SSM_N: 1280 + (g + 1) * SSM_N]
            sg = s_all[:, g * gw: (g + 1) * gw]
            dsng = dsn[:, g * gw: (g + 1) * gw]
            dcsg = dcs[:, g * gw: (g + 1) * gw]
            dcg = _bdot(dcsg, sg, NT)
            yoff_parts.append(_bdot(cg, sg, NN))
            ds_parts.append(_bdot(cg, dcsg, TN))
            dxst_parts.append(_bdot(bg, dsng, NN))
            dbg = _bdot(xd[:, g * gw: (g + 1) * gw], dsng, NT)
            cb = _bdot(cg, bg, NT)
            dcb = jnp.zeros((q, q), F32)
            for r in range(hg):
                h = g * hg + r
                lm = _ssd_lmat(cm, h)
                mm = cb * lm
                dyh = dyv[:, h * SSM_P: (h + 1) * SSM_P]
                dm = jnp.where(ii >= jj, _bdot(dyh, x[:, h * SSM_P: (h + 1) * SSM_P], NT), 0.0)
                dxd_scr[:, h * SSM_P: (h + 1) * SSM_P] = _bdot(mm, dyh, TN)
                dcb = dcb + dm * lm
                wm = dm * mm
                wrow = wrow + jnp.where(lane128 == h, _rowsum(wm), 0.0)
                wcol_rows.append(_colsum(wm))
            dc_parts.append(dcg + _bdot(dcb, bg, NN))
            db_parts.append(dbg + _bdot(dcb, cg, TN))
        dxst = jnp.concatenate(dxst_parts, axis=1) * cm["dex"]
        dx = dxd_scr[...] + dxst
        ds_scr[...] = jnp.concatenate(ds_parts, axis=1) + dsn * cm["elx"]
        wcol = jnp.concatenate(wcol_rows + [jnp.zeros((128 - SSM_HEADS, q), F32)], axis=0).T
        yoff = jnp.concatenate(yoff_parts, axis=1) * cm["ex"]
        xdxst = x * dxst
        dac = wrow - wcol + _dot01(dyv * yoff - xdxst, rmat_v, NT)
        last = _dot01(_colsum(dsn * s_all) * cm["elx"] + _colsum(xdxst), rmat_v, NT)
        rowq = lax.broadcasted_iota(jnp.int32, (q, 128), 0)
        dac = dac + jnp.where(rowq == q - 1, last, 0.0)
        da = _hdot(cm["triu"], dac)
        arow_v = arow_ref[...]
        ddt = da * arow_v + _dot01(dx * xs, rmat_v, NT)
        dxs = dxs + dx * cm["dtx"]
        dal_ref[...] += _colsum(da * cm["a"])
        ddtraw = jnp.where(cm["m16"], ddt * _sigmoid(smv + dtb_ref[...]), 0.0)
        ddtb_ref[...] += _colsum(ddtraw)
        dsm_ref[...] = ddtraw.astype(BF16)
        dact_ref[:, :1024] = dxs
        for g in range(SSM_GROUPS):
            dact_ref[:, 1024 + g * SSM_N: 1024 + (g + 1) * SSM_N] = db_parts[g]
            dact_ref[:, 1280 + g * SSM_N: 1280 + (g + 1) * SSM_N] = dc_parts[g]

    rev = lambda c: nc - 1 - c
    vec128 = pl.BlockSpec((1, 128), lambda c: (0, 0))
    vec1k = pl.BlockSpec((1, 1024), lambda c: (0, 0))
    return pl.pallas_call(
        body, grid=(nc,),
        in_specs=[pl.BlockSpec((q, SSM_CONV), lambda c: (rev(c), 0)),
                  pl.BlockSpec((q, 1024), lambda c: (rev(c), Z_OFF // 1024)),
                  pl.BlockSpec((q, 128), lambda c: (rev(c), SM_OFF // 128)), vec128, vec128, vec1k, vec1k,
                  pl.BlockSpec((128, 1024), lambda c: (0, 0)),
                  pl.BlockSpec((q, 1024), lambda c: (rev(c), 0)), pl.BlockSpec((1, 128, 1024), lambda c: (rev(c), 0, 0)),
                  pl.BlockSpec((q, 1024), lambda c: (rev(c), 0))],
        out_specs=(pl.BlockSpec((q, SSM_CONV), lambda c: (rev(c), 0)), pl.BlockSpec((q, 1024), lambda c: (rev(c), 0)),
                   pl.BlockSpec((q, 128), lambda c: (rev(c), 0)), vec1k, vec128, vec128, vec128),
        out_shape=(SDS((t, SSM_CONV), F32), SDS((t, 1024), BF16), SDS((t, 128), BF16), SDS((1, 1024), F32),
                   SDS((1, 128), F32), SDS((1, 128), F32), SDS((1, 128), F32)),
        scratch_shapes=[pltpu.VMEM((128, 1024), F32), pltpu.VMEM((q, 1024), F32)], name=name,
        compiler_params=_params(("arbitrary",)),
    )(act, proj, proj, dtb, arow, dxrow, nw, rmat, ysc, states, dy)


def _split(a):
    hi = a.astype(BF16)
    return hi, (a - hi.astype(F32)).astype(BF16)


def _dot3(a, b, dims=NN):
    (ah, al), (bh, bl) = a, b

    def d(x, y):
        return lax.dot_general(x, y, (dims, ((), ())), preferred_element_type=F32)

    return d(ah, bh) + (d(ah, bl) + d(al, bh))


def _tri_inverses(amats, ii, jj):
    eye = jnp.where(ii == jj, 1.0, 0.0)
    tms = [eye - a for a in amats]
    sp = [_split(a) for a in amats]
    for _ in range(5):
        sp = [_split(_dot3(s, s)) for s in sp]
        tms = [t + _dot3(_split(t), s) for t, s in zip(tms, sp)]
    return tms


def _gdn_common(sm, gb, garow):
    q = CHUNK
    ii, jj = _tri(q)
    lane = lax.broadcasted_iota(jnp.int32, (q, 128), 1)
    ma = (lane >= LANE_A) & (lane < LANE_A + GDN_HEADS)
    spre = sm + gb
    g = jnp.where(ma, garow * _softplus(spre), 0.0)
    beta = _sigmoid(sm)
    tril = (ii >= jj).astype(F32)
    triu = (ii <= jj).astype(F32)
    gc = _hdot(tril, g)
    gc_r = _hdot(g.T, triu)
    return dict(ii=ii, jj=jj, lane=lane, ma=ma, spre=spre, g=g, beta=beta, triu=triu, gc=gc, gc_r=gc_r)


def _each(f, *lists):
    return [f(*xs) for xs in zip(*lists)]


GDN_SCALE = GDN_DK ** -0.5


def _gdn_heads(cm, actv, states):
    q = CHUNK
    ii, jj = cm["ii"], cm["jj"]
    heads = range(GDN_HEADS)
    qr = [actv[:, h * 128: (h + 1) * 128] for h in heads]
    kr = [actv[:, 1024 + h * 128: 1024 + (h + 1) * 128] for h in heads]
    v = [actv[:, 2048 + h * 128: 2048 + (h + 1) * 128] for h in heads]
    rq = _each(lambda x: lax.rsqrt(_rowsum(x * x) + EPS), qr)
    rk = _each(lambda x: lax.rsqrt(_rowsum(x * x) + EPS), kr)
    qn = _each(lambda x, r: x * r * GDN_SCALE, qr, rq)
    kn = _each(lambda x, r: x * r, kr, rk)
    gcc = [cm["gc"][:, LANE_A + h: LANE_A + h + 1] for h in heads]
    gcr = [cm["gc_r"][LANE_A + h: LANE_A + h + 1, :] for h in heads]
    bcol = [cm["beta"][:, LANE_B + h: LANE_B + h + 1] for h in heads]
    dm = _each(lambda c, r: jnp.where(ii >= jj, jnp.exp(c - r), 0.0), gcc, gcr)
    ak = _each(lambda k, d: jnp.where(ii > jj, _bdot(k, k, NT) * d, 0.0), kn, dm)
    tm = _tri_inverses(_each(lambda a, b: a * b, ak, bcol), ii, jj)
    eg = _each(jnp.exp, gcc)
    gl = [c[q - 1: q, :] for c in gcc]
    rm = _each(lambda vv, k, b, e: jnp.concatenate([vv * b, k * (b * e)], axis=1), v, kn, bcol, eg)
    tt = _each(lambda t, r: _dot3(_split(t), _split(r)), tm, rm)
    w = [t[:, 128:] for t in tt]
    vnew = _each(lambda t, ww, s: t[:, :128] - _bdot(ww, s, NN), tt, w, states)
    qkm = _each(lambda a, k, d: jnp.where(ii >= jj, _bdot(a, k, NT) * d, 0.0), qn, kn, dm)
    return dict(qr=qr, v=v, rq=rq, rk=rk, qn=qn, kn=kn, gcc=gcc, bcol=bcol, dm=dm, ak=ak, tm=tm, eg=eg, gl=gl,
                egl=_each(jnp.exp, gl), ed=_each(lambda g, c: jnp.exp(g - c), gl, gcc), tt=tt, w=w, vnew=vnew, qkm=qkm,
                qg=_each(lambda a, e: a * e, qn, eg))


def _gdn_fwd(name, act, proj, gb, garow, gnw):
    t = act.shape[0]
    q = CHUNK
    nc = t // q

    def body(act_ref, gz_ref, sm_ref, gb_ref, ga_ref, nw_ref, y_ref, o_ref, st_ref, s_scr):
        @pl.when(pl.program_id(0) == 0)
        def _():
            s_scr[...] = jnp.zeros_like(s_scr)

        st_ref[0] = s_scr[...]
        actv = act_ref[...]
        cm = _gdn_common(sm_ref[...], gb_ref[...], ga_ref[...])
        nwv = nw_ref[...]
        gzv = gz_ref[...]
        states = [s_scr[h * 128: (h + 1) * 128, :] for h in range(GDN_HEADS)]
        hd = _gdn_heads(cm, actv, states)
        outs = _each(lambda a, s, m, vn: _bdot(a, s, NN) + _bdot(m, vn, NN), hd["qg"], states, hd["qkm"], hd["vnew"])
        snew = _each(lambda s, e, k, d, vn: s * e + _bdot(k * d, vn, TN), states, hd["egl"], hd["kn"], hd["ed"], hd["vnew"])
        for h in range(GDN_HEADS):
            o = outs[h]
            s_scr[h * 128: (h + 1) * 128, :] = snew[h]
            o_ref[:, h * 128: (h + 1) * 128] = o
            rr = lax.rsqrt(jnp.mean(o * o, axis=-1, keepdims=True) + EPS)
            gz = gzv[:, h * 128: (h + 1) * 128]
            y_ref[:, h * 128: (h + 1) * 128] = (o * rr * nwv * (gz * _sigmoid(gz))).astype(BF16)

    vec128 = pl.BlockSpec((1, 128), lambda c: (0, 0))
    return pl.pallas_call(
        body, grid=(nc,),
        in_specs=[pl.BlockSpec((q, GDN_QKV), lambda c: (c, 0)), pl.BlockSpec((q, 1024), lambda c: (c, GZ_OFF // 1024)),
                  pl.BlockSpec((q, 128), lambda c: (c, SM_OFF // 128)), vec128, vec128, vec128],
        out_specs=(pl.BlockSpec((q, 1024), lambda c: (c, 0)), pl.BlockSpec((q, 1024), lambda c: (c, 0)),
                   pl.BlockSpec((1, 1024, 128), lambda c: (c, 0, 0))),
        out_shape=(SDS((t, 1024), BF16), SDS((t, 1024), F32), SDS((nc, 1024, 128), F32)),
        scratch_shapes=[pltpu.VMEM((1024, 128), F32)], name=name, compiler_params=_params(("arbitrary",)),
    )(act, proj, proj, gb, garow, gnw)


def _gdn_bwd(name, act, proj, gb, garow, gnw, oraw, states, dy):
    t = act.shape[0]
    q = CHUNK
    nc = t // q

    def body(act_ref, gz_ref, sm_ref, gb_ref, ga_ref, nw_ref, o_ref, st_ref, dy_ref,
             dact_ref, dgz_ref, dsm_ref, dnw_ref, dal_ref, dgb_ref, ds_scr):
        @pl.when(pl.program_id(0) == 0)
        def _():
            ds_scr[...] = jnp.zeros_like(ds_scr)
            dnw_ref[...] = jnp.zeros_like(dnw_ref)
            dal_ref[...] = jnp.zeros_like(dal_ref)
            dgb_ref[...] = jnp.zeros_like(dgb_ref)

        actv = act_ref[...]
        smv = sm_ref[...]
        garow_v = ga_ref[...]
        cm = _gdn_common(smv, gb_ref[...], garow_v)
        ii, jj, lane = cm["ii"], cm["jj"], cm["lane"]
        nwv = nw_ref[...]
        rowq = lax.broadcasted_iota(jnp.int32, (q, 1), 0)
        dgc_all = jnp.zeros((q, 128), F32)
        dbeta_all = jnp.zeros((q, 128), F32)
        dnw_acc = jnp.zeros((1, 128), F32)
        heads = range(GDN_HEADS)
        sts = [st_ref[0, h * 128: (h + 1) * 128, :] for h in heads]
        dsn = [ds_scr[h * 128: (h + 1) * 128, :] for h in heads]
        ov, gzv, dyv = o_ref[...], gz_ref[...], dy_ref[...]
        hd = _gdn_heads(cm, actv, sts)
        qn, kn, v, eg, ed, egl, bcol = hd["qn"], hd["kn"], hd["v"], hd["eg"], hd["ed"], hd["egl"], hd["bcol"]
        vnew, qkm, qg, w, tt, dm, ak = hd["vnew"], hd["qkm"], hd["qg"], hd["w"], hd["tt"], hd["dm"], hd["ak"]
        do = []
        for h in heads:
            hs = slice(h * 128, (h + 1) * 128)
            o = ov[:, hs]
            rr = lax.rsqrt(jnp.mean(o * o, axis=-1, keepdims=True) + EPS)
            on = o * rr
            gz = gzv[:, hs]
            sz = _sigmoid(gz)
            silz = gz * sz
            dyh = dyv[:, hs]
            dnw_acc = dnw_acc + _colsum(dyh * on * silz)
            dgz_ref[:, hs] = (dyh * on * nwv * (sz * (1.0 + gz * (1.0 - sz)))).astype(BF16)
            don = dyh * nwv * silz
            do.append(rr * (don - on * jnp.mean(don * on, axis=-1, keepdims=True)))
        kd = _each(lambda k, e: k * e, kn, ed)
        dkd = _each(lambda vn, d: _bdot(vn, d, NT), vnew, dsn)
        dvnew_a = _each(lambda k, d: _bdot(k, d, NN), kd, dsn)
        ded = _each(lambda a, b: _rowsum(a * b), dkd, kd)
        dgl = _each(lambda d, s, e, de: jnp.sum(_rowsum(d * s), axis=0, keepdims=True) * e + _colsum(de), dsn, sts, egl, ded)
        dqg = _each(lambda d, s: _bdot(d, s, NT), do, sts)
        ds1 = _each(lambda a, d: _bdot(a, d, TN), qg, do)
        dqk = _each(lambda d, vn: jnp.where(ii >= jj, _bdot(d, vn, NT), 0.0), do, vnew)
        dvnew = _each(lambda a, m, d: a + _bdot(m, d, TN), dvnew_a, qkm, do)
        pq = _each(lambda a, b: a * b, dqk, dm)
        dq = _each(lambda a, e, p, k: a * e + _bdot(p, k, NN), dqg, eg, pq, kn)
        dk1 = _each(lambda a, e, p, x: a * e + _bdot(p, x, TN), dkd, ed, pq, qn)
        w1 = _each(lambda a, b: a * b, dqk, qkm)
        dw = _each(lambda d, s: -_bdot(d, s, NT), dvnew, sts)
        ds2 = _each(lambda a, d: _bdot(a, d, TN), w, dvnew)
        dr = _each(lambda t, a, b: _dot3(_split(t), _split(jnp.concatenate([a, b], axis=1)), TN), hd["tm"], dvnew, dw)
        da = _each(lambda r, t: jnp.where(ii > jj, -_dot3(_split(r), _split(t), NT), 0.0), dr, tt)
        sk = _each(lambda r, k: _rowsum(r[:, 128:] * k), dr, kn)
        pk = _each(lambda a, d, b: a * d * b, da, dm, bcol)
        dk = _each(lambda a, r, b, e, p, k: a + r[:, 128:] * (b * e) + _bdot(p, k, NN) + _bdot(p, k, TN),
                   dk1, dr, bcol, eg, pk, kn)
        w2 = _each(lambda a, k, b: a * (k * b), da, ak, bcol)
        for h in heads:
            hs = slice(h * 128, (h + 1) * 128)
            dgc = (-ded[h] + _rowsum(dqg[h] * qg[h]) + _rowsum(w1[h]) - _rowsum(w1[h].T) + sk[h] * bcol[h] * eg[h]
                   + _rowsum(w2[h]) - _rowsum(w2[h].T) + jnp.where(rowq == q - 1, dgl[h], 0.0))
            dbeta = _rowsum(dr[h][:, :128] * v[h]) + sk[h] * eg[h] + _rowsum(da[h] * ak[h])
            qhat = hd["qr"][h] * hd["rq"][h]
            dqhat = dq[h] * GDN_SCALE
            dact_ref[:, hs] = hd["rq"][h] * (dqhat - qhat * _rowsum(dqhat * qhat))
            dact_ref[:, 1024 + h * 128: 1024 + (h + 1) * 128] = hd["rk"][h] * (dk[h] - kn[h] * _rowsum(dk[h] * kn[h]))
            dact_ref[:, 2048 + h * 128: 2048 + (h + 1) * 128] = dr[h][:, :128] * bcol[h]
            dgc_all = dgc_all + jnp.where(lane == LANE_A + h, dgc, 0.0)
            dbeta_all = dbeta_all + jnp.where(lane == LANE_B + h, dbeta, 0.0)
            ds_scr[hs, :] = dsn[h] * egl[h] + ds1[h] - ds2[h]
        dnw_ref[...] += dnw_acc
        dg = _hdot(cm["triu"], dgc_all)
        da_raw = jnp.where(cm["ma"], dg * garow_v * _sigmoid(cm["spre"]), 0.0)
        dal_ref[...] += _colsum(dg * cm["g"])
        dgb_ref[...] += _colsum(da_raw)
        beta = cm["beta"]
        dsm_ref[...] = (da_raw + dbeta_all * beta * (1.0 - beta)).astype(BF16)

    rev = lambda c: nc - 1 - c
    vec128 = pl.BlockSpec((1, 128), lambda c: (0, 0))
    return pl.pallas_call(
        body, grid=(nc,),
        in_specs=[pl.BlockSpec((q, GDN_QKV), lambda c: (rev(c), 0)),
                  pl.BlockSpec((q, 1024), lambda c: (rev(c), GZ_OFF // 1024)),
                  pl.BlockSpec((q, 128), lambda c: (rev(c), SM_OFF // 128)), vec128, vec128, vec128,
                  pl.BlockSpec((q, 1024), lambda c: (rev(c), 0)), pl.BlockSpec((1, 1024, 128), lambda c: (rev(c), 0, 0)),
                  pl.BlockSpec((q, 1024), lambda c: (rev(c), 0))],
        out_specs=(pl.BlockSpec((q, GDN_QKV), lambda c: (rev(c), 0)), pl.BlockSpec((q, 1024), lambda c: (rev(c), 0)),
                   pl.BlockSpec((q, 128), lambda c: (rev(c), 0)), vec128, vec128, vec128),
        out_shape=(SDS((t, GDN_QKV), F32), SDS((t, 1024), BF16), SDS((t, 128), BF16), SDS((1, 128), F32),
                   SDS((1, 128), F32), SDS((1, 128), F32)),
        scratch_shapes=[pltpu.VMEM((1024, 128), F32)], name=name, compiler_params=_params(("arbitrary",)),
    )(act, proj, proj, gb, garow, gnw, oraw, states, dy)


def _row_tile(r):
    for cand in (512, 256, 128, 64, 32, 16, 8):
        if r % cand == 0:
            return cand
    return r


def _sum_terms(name, terms, out_dtype):
    shape = terms[0][0].shape[1:]
    c = shape[-1]
    r = 1
    for s in shape[:-1]:
        r *= s
    tr = min(_row_tile(r), 256)
    n = len(terms)

    def body(*refs):
        acc = refs[0][...].astype(F32)
        for k in range(1, n):
            acc = acc + refs[k][...].astype(F32)
        refs[n][...] = acc.astype(out_dtype)

    in_specs = [pl.BlockSpec((None, tr, c), lambda i, q=lead: (q, i, 0)) for _, lead in terms]
    args = [a.reshape(a.shape[0], r, c) for a, _ in terms]
    out = pl.pallas_call(body, grid=(r // tr,), in_specs=in_specs, out_specs=pl.BlockSpec((tr, c), lambda i: (i, 0)),
                         out_shape=SDS((r, c), out_dtype), name=name, compiler_params=_params(("parallel",)))(*args)
    return out.reshape(shape)


def _adamw(name, w, g, m, v):
    shape = w.shape
    c = shape[-1]
    w2, g2, m2, v2 = (a.reshape(-1, c) for a in (w, g, m, v))
    r = w2.shape[0]
    tr = min(_row_tile(r), 256)

    def body(w_ref, g_ref, m_ref, v_ref, d_ref, nm_ref, nv_ref):
        gv = g_ref[...]
        mn = ADAM_B1 * m_ref[...] + (1.0 - ADAM_B1) * gv
        vn = ADAM_B2 * v_ref[...] + (1.0 - ADAM_B2) * (gv * gv)
        m_hat = mn / (1.0 - ADAM_B1 ** ADAM_STEP)
        v_hat = vn / (1.0 - ADAM_B2 ** ADAM_STEP)
        d_ref[...] = -ADAM_LR * (m_hat / (jnp.sqrt(v_hat) + ADAM_EPS) + ADAM_WD * w_ref[...])
        nm_ref[...] = mn
        nv_ref[...] = vn

    spec = pl.BlockSpec((tr, c), lambda i: (i, 0))
    outs = pl.pallas_call(body, grid=(r // tr,), in_specs=[spec] * 4, out_specs=(spec,) * 3,
                          out_shape=(SDS((r, c), F32),) * 3, name=name, compiler_params=_params(("parallel",)))(w2, g2, m2, v2)
    return tuple(o.reshape(shape) for o in outs)


ANY = pl.BlockSpec(memory_space=pl.ANY)
MESH = pl.DeviceIdType.MESH


def _allgather(name, xs):
    n = len(xs)

    def body(*refs):
        x_refs, out_refs = refs[:n], refs[n: 2 * n]
        send_sems, recv_sems, local_sems = refs[2 * n:]
        x, y, cc = lax.axis_index("x"), lax.axis_index("y"), lax.axis_index("c")
        me, sibling = (x, y, cc), (x, y, 1 - cc)
        chips = [(1 - x, y), (x, 1 - y), (1 - x, 1 - y)]

        def rows(a, px, py, pc):
            return out_refs[a].at[4 * px + 2 * py + pc]

        def copy(a, k, block, to, src=None):
            return pltpu.make_async_remote_copy(
                src_ref=rows(a, *block) if src is None else src, dst_ref=rows(a, *block),
                send_sem=send_sems.at[7 * a + k], recv_sem=recv_sems.at[7 * a + k], device_id=to, device_id_type=MESH)

        mine = [pltpu.make_async_copy(x_refs[a], rows(a, *me), local_sems.at[a]) for a in range(n)]
        for cp in mine:
            cp.start()
        first = []
        for a in range(n):
            first.append(copy(a, 0, me, sibling, src=x_refs[a]))
            first += [copy(a, 1 + j, me, (*chip, cc), src=x_refs[a]) for j, chip in enumerate(chips)]
        for cp in first:
            cp.start()
        passed = []
        for j, chip in enumerate(chips):
            for a in range(n):
                copy(a, 1 + j, (*chip, cc), me).wait_recv()
                fwd = copy(a, 4 + j, (*chip, cc), sibling)
                fwd.start()
                passed.append(fwd)
        for a in range(n):
            copy(a, 0, sibling, me).wait_recv()
        for j, chip in enumerate(chips):
            for a in range(n):
                copy(a, 4 + j, (*chip, 1 - cc), me).wait_recv()
        for cp in first + passed:
            cp.wait_send()
        for cp in mine:
            cp.wait()

    return pl.pallas_call(
        body, out_shape=tuple(SDS((N_DEV,) + a.shape, a.dtype) for a in xs), in_specs=[ANY] * n, out_specs=(ANY,) * n,
        scratch_shapes=[pltpu.SemaphoreType.DMA((7 * n,)), pltpu.SemaphoreType.DMA((7 * n,)),
                        pltpu.SemaphoreType.DMA((n,))],
        name=name,
    )(*xs)


def _allgather_seq(name, xs, collective_id):
    n = len(xs)
    x_refs = [jax.new_ref(a, memory_space=pltpu.MemorySpace.HBM) for a in xs]
    out_refs = [jax.empty_ref(SDS((N_DEV,) + a.shape, a.dtype), memory_space=pltpu.MemorySpace.HBM) for a in xs]

    @pl.kernel(mesh=plsc.ScalarSubcoreMesh(axis_name="seq", num_cores=1), name=name,
               scratch_types=(pltpu.SemaphoreType.DMA((7 * n,)), pltpu.SemaphoreType.DMA((7 * n,)),
                              pltpu.SemaphoreType.DMA((n,))),
               compiler_params=pltpu.CompilerParams(collective_id=collective_id))
    def launch(send_sems, recv_sems, local_sems):
        x, y, cc = lax.axis_index("x"), lax.axis_index("y"), lax.axis_index("c")
        me, sibling = (x, y, cc), (x, y, 1 - cc)
        chips = [(1 - x, y), (x, 1 - y), (1 - x, 1 - y)]
        barrier = pltpu.get_barrier_semaphore()
        for peer in [sibling] + [(*chip, cc) for chip in chips]:
            pl.semaphore_signal(barrier, inc=1, device_id=peer, device_id_type=MESH)
        pl.semaphore_wait(barrier, 4)

        def rows(a, px, py, pc):
            return out_refs[a].at[4 * px + 2 * py + pc]

        def copy(a, k, block, to, src=None):
            return pltpu.make_async_remote_copy(
                src_ref=rows(a, *block) if src is None else src, dst_ref=rows(a, *block),
                send_sem=send_sems.at[7 * a + k], recv_sem=recv_sems.at[7 * a + k], device_id=to, device_id_type=MESH)

        mine = [pltpu.make_async_copy(x_refs[a], rows(a, *me), local_sems.at[a]) for a in range(n)]
        for cp in mine:
            cp.start()
        first = []
        for a in range(n):
            first.append(copy(a, 0, me, sibling, src=x_refs[a]))
            first += [copy(a, 1 + j, me, (*chip, cc), src=x_refs[a]) for j, chip in enumerate(chips)]
        for cp in first:
            cp.start()
        passed = []
        for j, chip in enumerate(chips):
            for a in range(n):
                copy(a, 1 + j, (*chip, cc), me).wait_recv()
                fwd = copy(a, 4 + j, (*chip, cc), sibling)
                fwd.start()
                passed.append(fwd)
        for a in range(n):
            copy(a, 0, sibling, me).wait_recv()
        for j, chip in enumerate(chips):
            for a in range(n):
                copy(a, 4 + j, (*chip, 1 - cc), me).wait_recv()
        for cp in first + passed:
            cp.wait_send()
        for cp in mine:
            cp.wait()

    launch()
    return [r[...] for r in out_refs]


HBM = pl.BlockSpec(memory_space=pltpu.HBM)
SEM = pl.BlockSpec(memory_space=pltpu.SEMAPHORE)
EFFECT = pltpu.SideEffectType.DATAFLOW_SIDE_EFFECTING


def _sibling_plan(srcs, lands, send_sems, recv_sems):
    x, y, cc = lax.axis_index("x"), lax.axis_index("y"), lax.axis_index("c")
    return [pltpu.make_async_remote_copy(
        src_ref=srcs[a].at[2 * q + 1 - cc], dst_ref=lands[a].at[q], send_sem=send_sems.at[4 * a + q],
        recv_sem=recv_sems.at[4 * a + q], device_id=(x, y, 1 - cc), device_id_type=MESH)
        for a in range(len(srcs)) for q in range(4)]


def _chips_plan(srcs, lands, send_sems, recv_sems):
    x, y, cc = lax.axis_index("x"), lax.axis_index("y"), lax.axis_index("c")
    chips = [(1 - x, y), (x, 1 - y), (1 - x, 1 - y)]
    return [pltpu.make_async_remote_copy(
        src_ref=srcs[a].at[2 * px + py], dst_ref=lands[a].at[j], send_sem=send_sems.at[3 * a + j],
        recv_sem=recv_sems.at[3 * a + j], device_id=(px, py, cc), device_id_type=MESH)
        for a in range(len(srcs)) for j, (px, py) in enumerate(chips)]


def _copies_start(name, plan, per_array, srcs, land_lead):
    n = len(srcs)
    k = per_array * n

    def body(*refs):
        src_refs, land_refs = refs[:n], refs[n: 2 * n]
        send_sems, recv_sems = refs[2 * n], refs[2 * n + 1]
        token = refs[-1]
        for cp in plan(src_refs, land_refs, send_sems, recv_sems):
            cp.start()
        token[...] = jnp.zeros_like(token)

    lands = [lax.empty((land_lead,) + a.shape[1:], a.dtype) for a in srcs]
    outs = pl.pallas_call(
        body, name=name,
        out_shape=(pltpu.SemaphoreType.DMA((k,)), pltpu.SemaphoreType.DMA((k,)),
                   *[pltpu.HBM(a.shape, a.dtype) for a in srcs], *[pltpu.HBM(a.shape, a.dtype) for a in lands],
                   SDS((8, 128), F32)),
        in_specs=[HBM] * (2 * n), out_specs=(SEM, SEM, *[HBM] * (2 * n), pl.BlockSpec(memory_space=pltpu.VMEM)),
        input_output_aliases={i: 2 + i for i in range(2 * n)},
        compiler_params=pltpu.CompilerParams(has_side_effects=EFFECT),
    )(*[pltpu.with_memory_space_constraint(a, pltpu.HBM) for a in srcs],
      *[pltpu.with_memory_space_constraint(a, pltpu.HBM) for a in lands])
    return outs[0], outs[1], list(outs[2: 2 + n]), list(outs[2 + n: 2 + 2 * n]), outs[-1]


def _copies_wait(name, plan, started, after):
    send_sems, recv_sems, srcs, lands, _ = started
    n = len(srcs)

    def body(*refs):
        src_refs, land_refs = refs[:n], refs[n: 2 * n]
        for cp in plan(src_refs, land_refs, refs[2 * n], refs[2 * n + 1]):
            cp.wait_send()
            cp.wait_recv()

    outs = pl.pallas_call(
        body, name=name,
        out_shape=tuple(pltpu.HBM(a.shape, a.dtype) for a in srcs + lands),
        in_specs=[HBM] * (2 * n) + [SEM, SEM, ANY], out_specs=(HBM,) * (2 * n),
        input_output_aliases={i: i for i in range(2 * n)},
        compiler_params=pltpu.CompilerParams(has_side_effects=EFFECT),
    )(*srcs, *lands, send_sems, recv_sems, after)
    return list(outs[n:])


BIG = (("w_in", 1), ("w_ffn_in", 1), ("w_proj_ssm", 0), ("w_proj_gdn", 0), ("w_out", 0), ("w_ffn_down", 0))
CONVS = (("ssm_conv_w", 1), ("gdn_conv_w", 1))


def _to_dest_major(full, axis):
    a, b = full.shape
    if axis == 0:
        return full.reshape(N_DEV, a // N_DEV, b)
    return full.reshape(a, N_DEV, b // N_DEV).transpose(1, 0, 2)


def _from_gathered(g, axis):
    if axis == 0:
        return g.reshape(-1, g.shape[2])
    return g.transpose(1, 0, 2).reshape(g.shape[1], -1)


def _permute_in_cols(w):
    sm = jnp.concatenate([w[..., O_DT: O_DT + 16], w[..., O_A: O_A + 8], w[..., O_B: O_B + 8]], axis=-1)
    pad_sm = jnp.zeros(w.shape[:-1] + (128 - 32,), w.dtype)
    pad_end = jnp.zeros(w.shape[:-1] + (PROJ_W - SM_OFF - 128,), w.dtype)
    return jnp.concatenate([w[..., O_Z: O_Z + 1024], w[..., O_GZ: O_GZ + 1024], w[..., O_G1: O_G1 + 1024],
                            w[..., O_G2: O_G2 + 1024], w[..., O_QKV: O_QKV + 3072], w[..., O_XBC: O_XBC + 1536],
                            sm, pad_sm, pad_end], axis=-1)


def _unpermute_in_cols(w):
    return jnp.concatenate([w[..., Z_OFF: Z_OFF + 1024], w[..., XBC_OFF: XBC_OFF + 1536], w[..., SM_OFF: SM_OFF + 16],
                            w[..., QKV_OFF: QKV_OFF + 3072], w[..., GZ_OFF: GZ_OFF + 1024],
                            w[..., SM_OFF + LANE_A: SM_OFF + LANE_A + 8], w[..., SM_OFF + LANE_B: SM_OFF + LANE_B + 8],
                            w[..., G1_OFF: G1_OFF + 1024], w[..., G2_OFF: G2_OFF + 1024]], axis=-1)


def _pad128(v, lane0):
    return jnp.zeros((1, 128), F32).at[0, lane0: lane0 + v.shape[0]].set(v)


def _layer_consts(p):
    return dict(
        dtb=_pad128(p["ssm_dt_bias"], 0), arow=_pad128(-jnp.exp(p["ssm_a_log"]), 0),
        dxrow=jnp.repeat(p["ssm_d"], SSM_P).reshape(1, 1024), snw=p["ssm_norm_w"].reshape(1, 1024),
        gb=_pad128(p["gdn_dt_bias"], LANE_A), garow=_pad128(-jnp.exp(p["gdn_a_log"]), LANE_A),
        gnw=p["gdn_norm_w"].reshape(1, 128), zb=jnp.zeros((1, GDN_QKV), F32), scb=p["ssm_conv_b"].reshape(1, SSM_CONV))


def _expand_matrix():
    row = lax.broadcasted_iota(jnp.int32, (128, 1024), 0)
    col = lax.broadcasted_iota(jnp.int32, (128, 1024), 1)
    return (col // SSM_P == row).astype(BF16)


def _silu_mul_epi(acc, up):
    g = acc
    return g, g * _sigmoid(g) * up.astype(F32)


def _merge_epi(acc, p1, g1, g2):
    return acc, _sigmoid(g1) * p1.astype(F32) + _sigmoid(g2) * acc


def _add_epi(acc, res):
    return (acc + res,)


def _ffn_bwd_epi(acc, gate, up):
    g = gate.astype(F32)
    sg = _sigmoid(g)
    return acc * up.astype(F32) * (sg * (1.0 + g * (1.0 - sg))), acc * (g * sg)


def _merge_bwd_epi(acc, g1, g2, p1, p2):
    s1, s2 = _sigmoid(g1), _sigmoid(g2)
    return acc * s1, acc * s2, acc * p1.astype(F32) * (s1 * (1.0 - s1)), acc * p2.astype(F32) * (s2 * (1.0 - s2))


def _layer_fwd(l, x, p, rmat):
    t = x.shape[0]
    n = f"l{l}_"
    k = _layer_consts(p)
    h = _rmsnorm_fwd(n + "norm_mix", x, p["norm_mix_w"])
    proj = _matmul(n + "in_proj", "nn", [(h, 0, p["w_in"], 0)], t, PROJ_W, 1024, 1024, 1280, 1024, (F32,))
    act_g = _conv_fwd(n + "conv_gdn", proj, QKV_OFF, p["gdn_conv_w"], k["zb"])
    act_s = _conv_fwd(n + "conv_ssm", proj, XBC_OFF, p["ssm_conv_w"], k["scb"])
    y_ssm, ysc, st_s = _ssd_fwd(n + "ssd_fwd", act_s, proj, k["dtb"], k["arow"], k["dxrow"], k["snw"], rmat)
    y_gdn, oraw, st_g = _gdn_fwd(n + "gdn_fwd", act_g, proj, k["gb"], k["garow"], k["gnw"])
    if "late" in p:
        y_gdn, late = p["late"](y_gdn)
        p = {**p, **late}
    p1 = _matmul(n + "proj_ssm", "nn", [(y_ssm, 0, p["w_proj_ssm"], 0)], t, 1024, 1024, 1024, 1024, 1024, (BF16,))
    p2, merged = _matmul(n + "proj_gdn_merge", "nn", [(y_gdn, 0, p["w_proj_gdn"], 0)], t, 1024, 1024, 512, 1024, 1024,
                         (BF16, BF16), epi=_merge_epi, extras=[(p1, 0), (proj, G1_OFF // 1024), (proj, G2_OFF // 1024)])
    x1 = _matmul(n + "out_proj", "nn", [(merged, 0, p["w_out"], 0)], t, 1024, 1024, 1024, 1024, 1024, (F32,),
                 epi=_add_epi, extras=[(x, 0)])
    h2 = _rmsnorm_fwd(n + "norm_ffn", x1, p["norm_ffn_w"])
    up = _matmul(n + "ffn_up", "nn", [(h2, 0, p["w_ffn_in"], 2)], t, FFN, 1024, 512, FFN // 2, 1024, (BF16,))
    gate, act = _matmul(n + "ffn_gate", "nn", [(h2, 0, p["w_ffn_in"], 0)], t, FFN, 1024, 512, FFN // 2, 1024, (BF16, BF16),
                        epi=_silu_mul_epi, extras=[(up, 0)])
    x2 = _matmul(n + "ffn_down", "nn", [(act, 0, p["w_ffn_down"], 0)], t, 1024, FFN, 512, 1024, FFN, (F32,),
                 epi=_add_epi, extras=[(x1, 0)])
    saved = dict(x=x, h=h, proj=proj, act_g=act_g, act_s=act_s, y_ssm=y_ssm, ysc=ysc, st_s=st_s, y_gdn=y_gdn, oraw=oraw,
                 st_g=st_g, p1=p1, p2=p2, merged=merged, x1=x1, h2=h2, up=up, gate=gate, act=act, k=k, p=p)
    return x2, saved


def _layer_bwd(l, dx2, dx2b, s, p, rmat, mid=None):
    t = dx2.shape[0]
    n = f"l{l}_"
    k = s["k"]
    tk_tok = 1024
    hf = FFN // 2
    g = {}
    dgate, dup = _matmul(n + "d_ffn_act", "nt", [(dx2b, 0, p["w_ffn_down"], 0)], t, FFN, 1024, 512, hf, 1024, (BF16, BF16),
                         epi=_ffn_bwd_epi, extras=[(s["gate"], 0), (s["up"], 0)])
    g["w_ffn_down"] = _matmul(n + "dw_ffn_down", "tn", [(s["act"], 0, dx2b, 0)], FFN, 1024, t, hf, 1024, tk_tok, (F32,))
    dh2 = _matmul(n + "d_ffn_in", "nt", [(dgate, 0, p["w_ffn_in"], 0), (dup, 0, p["w_ffn_in"], 2)], t, 1024, FFN,
                  512, 1024, hf, (F32,))
    dwg = _matmul(n + "dw_ffn_gate", "tn", [(s["h2"], 0, dgate, 0)], 1024, FFN, t, 1024, hf, tk_tok, (F32,))
    dwu = _matmul(n + "dw_ffn_up", "tn", [(s["h2"], 0, dup, 0)], 1024, FFN, t, 1024, hf, tk_tok, (F32,))
    g["w_ffn_in"] = jnp.concatenate([dwg, dwu], axis=1)
    dx1, dx1b, g["norm_ffn_w"] = _rmsnorm_bwd(n + "d_norm_ffn", s["x1"], p["norm_ffn_w"], dh2, dx2)
    if mid is not None:
        dx1, dx1b = mid(dx1, dx1b)
    dp1, dp2, dg1, dg2 = _matmul(
        n + "d_out_proj", "nt", [(dx1b, 0, p["w_out"], 0)], t, 1024, 1024, 512, 1024, 1024, (BF16,) * 4, epi=_merge_bwd_epi,
        extras=[(s["proj"], G1_OFF // 1024), (s["proj"], G2_OFF // 1024), (s["p1"], 0), (s["p2"], 0)])
    g["w_out"] = _matmul(n + "dw_out", "tn", [(s["merged"], 0, dx1b, 0)], 1024, 1024, t, 1024, 1024, tk_tok, (F32,))
    dy_ssm = _matmul(n + "d_proj_ssm", "nt", [(dp1, 0, p["w_proj_ssm"], 0)], t, 1024, 1024, 1024, 1024, 1024, (F32,))
    g["w_proj_ssm"] = _matmul(n + "dw_proj_ssm", "tn", [(s["y_ssm"], 0, dp1, 0)], 1024, 1024, t, 1024, 1024, tk_tok, (F32,))
    dy_gdn = _matmul(n + "d_proj_gdn", "nt", [(dp2, 0, p["w_proj_gdn"], 0)], t, 1024, 1024, 1024, 1024, 1024, (F32,))
    g["w_proj_gdn"] = _matmul(n + "dw_proj_gdn", "tn", [(s["y_gdn"], 0, dp2, 0)], 1024, 1024, t, 1024, 1024, tk_tok, (F32,))
    dact_s, dz, dsm_s, dsnw, dd, dal, ddtb = _ssd_bwd(n + "ssd_bwd", s["act_s"], s["proj"], k["dtb"], k["arow"], k["dxrow"],
                                                        k["snw"], rmat, s["ysc"], s["st_s"], dy_ssm)
    dact_g, dgz, dsm_g, dgnw, dgal, dgb = _gdn_bwd(n + "gdn_bwd", s["act_g"], s["proj"], k["gb"], k["garow"], k["gnw"],
                                                     s["oraw"], s["st_g"], dy_gdn)
    du_s, g["ssm_conv_w"], dcb = _conv_bwd(n + "d_conv_ssm", s["proj"], XBC_OFF, p["ssm_conv_w"], k["scb"], dact_s)
    du_g, g["gdn_conv_w"], _ = _conv_bwd(n + "d_conv_gdn", s["proj"], QKV_OFF, p["gdn_conv_w"], k["zb"], dact_g)
    g["ssm_conv_b"] = dcb.reshape(-1)
    g["ssm_norm_w"] = dsnw.reshape(-1)
    g["ssm_d"] = dd[0, :SSM_HEADS]
    g["ssm_a_log"] = dal[0, :SSM_HEADS]
    g["ssm_dt_bias"] = ddtb[0, :SSM_HEADS]
    g["gdn_norm_w"] = dgnw.reshape(-1)
    g["gdn_a_log"] = dgal[0, LANE_A: LANE_A + GDN_HEADS]
    g["gdn_dt_bias"] = dgb[0, LANE_A: LANE_A + GDN_HEADS]
    dproj = jnp.concatenate([dz, dgz, dg1, dg2, du_g, du_s, dsm_s + dsm_g, jnp.zeros((t, PROJ_W - SM_OFF - 128), BF16)],
                            axis=1)
    dh = _matmul(n + "d_in_proj", "nt", [(dproj, 0, p["w_in"], 0)], t, 1024, PROJ_W, 1024, 1024, 1280, (F32,))
    g["w_in"] = _matmul(n + "dw_in", "tn", [(s["h"], 0, dproj, 0)], 1024, PROJ_W, t, 1024, 1280, tk_tok, (F32,))
    dx, dxb, g["norm_mix_w"] = _rmsnorm_bwd(n + "d_norm_mix", s["x"], p["norm_mix_w"], dh, dx1)
    g["norm_mix_w"] = g["norm_mix_w"].reshape(-1)
    g["norm_ffn_w"] = g["norm_ffn_w"].reshape(-1)
    return dx, dxb, g


def _local_step(x, tgt, layers, final_norm_w, reducer=None):
    rmat = _expand_matrix()
    saved, params = [], []
    for l in range(DEPTH):
        x, p = layers[l](x)
        x, s = _layer_fwd(l, x, p, rmat)
        saved.append(s)
        params.append(s["p"])
    loss, dx, dxb, dfw = _loss_head("loss_head", x, final_norm_w, tgt)
    grads, shards = [None] * DEPTH, [None] * DEPTH
    pending = None
    for l in reversed(range(DEPTH)):
        dx, dxb, grads[l] = _layer_bwd(l, dx, dxb, saved[l], params[l], rmat, mid=pending.mid if pending else None)
        if reducer is None:
            continue
        if pending:
            shards[l + 1] = pending.end(dxb)
        pending = reducer(l, grads[l])
        dx, dxb = pending.start(dx, dxb)
    if pending:
        dx, dxb = pending.mid(dx, dxb)
        shards[0] = pending.end(dxb)
    return loss[0, 0], dx, grads, dfw.reshape(-1), shards


SMALL = ("norm_mix_w", "ssm_conv_b", "ssm_dt_bias", "ssm_a_log", "ssm_d", "ssm_norm_w", "gdn_a_log", "gdn_dt_bias",
         "gdn_norm_w", "norm_ffn_w")
WEIGHTS = ("norm_mix_w", "w_in", "ssm_conv_w", "ssm_conv_b", "ssm_dt_bias", "ssm_a_log", "ssm_d", "ssm_norm_w", "gdn_conv_w",
           "gdn_a_log", "gdn_dt_bias", "gdn_norm_w", "w_proj_ssm", "w_proj_gdn", "w_out", "norm_ffn_w", "w_ffn_in",
           "w_ffn_down", "final_norm_w")


FIRST_USED = ("w_in", "ssm_conv_w", "gdn_conv_w")


def _gather_layer(l, w):
    conv_names = [nm for nm, _ in CONVS]
    groups = ([s for s in BIG + CONVS if s[0] in FIRST_USED], [s for s in BIG + CONVS if s[0] not in FIRST_USED])
    gathered = []
    for i, (specs, tag) in enumerate(zip(groups, ("first", "rest"))):
        shards = [w[nm][l] if nm in conv_names else w[nm][l].astype(BF16) for nm, _ in specs]
        gathered.append(_allgather_seq(f"l{l}_gather_{tag}", shards, collective_id=2 * l + i))
    small = {nm: w[nm][l] for nm in SMALL}

    def use(i, act):
        act, blocks = lax.optimization_barrier((act, gathered[i]))
        return act, {nm: _from_gathered(g, axis) for (nm, axis), g in zip(groups[i], blocks)}

    def full_weights(x):
        x, out = use(0, x)
        out["w_in"] = _permute_in_cols(out["w_in"])
        out.update(small)
        out["late"] = lambda y: use(1, y)
        return x, out

    return full_weights


def _tie(a, token):
    return a + token[0, 0].astype(a.dtype)


class _GradReduceScatter:
    def __init__(self, l, grads):
        self.l = l
        self.specs = BIG + CONVS
        grads = dict(grads)
        grads["w_in"] = _unpermute_in_cols(grads["w_in"])
        self.blocks = [_to_dest_major(grads[nm], axis) for nm, axis in self.specs]

    def start(self, dx, dxb):
        cc = lax.axis_index("c")
        self.keep = [lax.dynamic_index_in_dim(b.reshape((4, 2) + b.shape[1:]), cc, axis=1, keepdims=False)
                     for b in self.blocks]
        self.to_sibling = _copies_start(f"l{self.l}_grads_to_sibling_start", _sibling_plan, 4, self.blocks, 4)
        return dx, _tie(dxb, self.to_sibling[4])

    def mid(self, dx, dxb):
        got = _copies_wait(f"l{self.l}_grads_to_sibling_wait", _sibling_plan, self.to_sibling, dxb)
        chip_sums = [_sum_terms(f"l{self.l}_chip_sum_{nm}", [(k[None], 0), (g[None], 0)], BF16)
                     for (nm, _), k, g in zip(self.specs, self.keep, got)]
        self.to_chips = _copies_start(f"l{self.l}_grads_between_chips_start", _chips_plan, 3, chip_sums, 3)
        return dx, _tie(dxb, self.to_chips[4])

    def end(self, after):
        landed = _copies_wait(f"l{self.l}_grads_between_chips_wait", _chips_plan, self.to_chips, after)
        my_chip = 2 * lax.axis_index("x") + lax.axis_index("y")
        own = [lax.dynamic_index_in_dim(s, my_chip, axis=0, keepdims=True) for s in self.to_chips[2]]
        return {nm: _sum_terms(f"l{self.l}_total_{nm}", [(o, 0), (e, 0), (e, 1), (e, 2)], F32)
                for (nm, _), o, e in zip(self.specs, own, landed)}


def _allreduce_small(vecs):
    flat = jnp.concatenate(vecs)
    n = flat.shape[0]
    rows = -(-n // 128)
    rows = -(-rows // 8) * 8
    buf = jnp.pad(flat, (0, rows * 128 - n)).reshape(rows, 128)
    (allv,) = _allgather("gather_small_grads", [buf])
    tot = _sum_terms("small_grads_total", [(allv, d) for d in range(N_DEV)], F32).reshape(-1)
    out, o = [], 0
    for v in vecs:
        out.append(tot[o: o + v.shape[0]])
        o += v.shape[0]
    return out


def kernel(x, norm_mix_w, w_in, ssm_conv_w, ssm_conv_b, ssm_dt_bias, ssm_a_log, ssm_d, ssm_norm_w, gdn_conv_w, gdn_a_log, gdn_dt_bias, gdn_norm_w, w_proj_ssm, w_proj_gdn, w_out, norm_ffn_w, w_ffn_in, w_ffn_down, final_norm_w, loss_target, m_norm_mix_w, m_w_in, m_ssm_conv_w, m_ssm_conv_b, m_ssm_dt_bias, m_ssm_a_log, m_ssm_d, m_ssm_norm_w, m_gdn_conv_w, m_gdn_a_log, m_gdn_dt_bias, m_gdn_norm_w, m_w_proj_ssm, m_w_proj_gdn, m_w_out, m_norm_ffn_w, m_w_ffn_in, m_w_ffn_down, m_final_norm_w, v_norm_mix_w, v_w_in, v_ssm_conv_w, v_ssm_conv_b, v_ssm_dt_bias, v_ssm_a_log, v_ssm_d, v_ssm_norm_w, v_gdn_conv_w, v_gdn_a_log, v_gdn_dt_bias, v_gdn_norm_w, v_w_proj_ssm, v_w_proj_gdn, v_w_out, v_norm_ffn_w, v_w_ffn_in, v_w_ffn_down, v_final_norm_w):
    w = dict(norm_mix_w=norm_mix_w, w_in=w_in, ssm_conv_w=ssm_conv_w, ssm_conv_b=ssm_conv_b, ssm_dt_bias=ssm_dt_bias,
             ssm_a_log=ssm_a_log, ssm_d=ssm_d, ssm_norm_w=ssm_norm_w, gdn_conv_w=gdn_conv_w, gdn_a_log=gdn_a_log,
             gdn_dt_bias=gdn_dt_bias, gdn_norm_w=gdn_norm_w, w_proj_ssm=w_proj_ssm, w_proj_gdn=w_proj_gdn, w_out=w_out,
             norm_ffn_w=norm_ffn_w, w_ffn_in=w_ffn_in, w_ffn_down=w_ffn_down, final_norm_w=final_norm_w)
    m = dict(norm_mix_w=m_norm_mix_w, w_in=m_w_in, ssm_conv_w=m_ssm_conv_w, ssm_conv_b=m_ssm_conv_b, ssm_dt_bias=m_ssm_dt_bias,
             ssm_a_log=m_ssm_a_log, ssm_d=m_ssm_d, ssm_norm_w=m_ssm_norm_w, gdn_conv_w=m_gdn_conv_w, gdn_a_log=m_gdn_a_log,
             gdn_dt_bias=m_gdn_dt_bias, gdn_norm_w=m_gdn_norm_w, w_proj_ssm=m_w_proj_ssm, w_proj_gdn=m_w_proj_gdn,
             w_out=m_w_out, norm_ffn_w=m_norm_ffn_w, w_ffn_in=m_w_ffn_in, w_ffn_down=m_w_ffn_down,
             final_norm_w=m_final_norm_w)
    v = dict(norm_mix_w=v_norm_mix_w, w_in=v_w_in, ssm_conv_w=v_ssm_conv_w, ssm_conv_b=v_ssm_conv_b, ssm_dt_bias=v_ssm_dt_bias,
             ssm_a_log=v_ssm_a_log, ssm_d=v_ssm_d, ssm_norm_w=v_ssm_norm_w, gdn_conv_w=v_gdn_conv_w, gdn_a_log=v_gdn_a_log,
             gdn_dt_bias=v_gdn_dt_bias, gdn_norm_w=v_gdn_norm_w, w_proj_ssm=v_w_proj_ssm, w_proj_gdn=v_w_proj_gdn,
             w_out=v_w_out, norm_ffn_w=v_norm_ffn_w, w_ffn_in=v_w_ffn_in, w_ffn_down=v_w_ffn_down,
             final_norm_w=v_final_norm_w)

    layers = [_gather_layer(l, w) for l in range(DEPTH)]
    loss_part, dx, lgrads, dfw, shard_grads = _local_step(x[0], loss_target[0], layers, final_norm_w,
                                                         reducer=_GradReduceScatter)
    loss = lax.psum(loss_part, ("x", "y", "c"))
    grad ={nm: jnp.stack([shard_grads[l][nm] for l in range(DEPTH)]) for nm, _ in BIG + CONVS}
    small_vecs = [lgrads[l][nm].reshape(-1) for l in range(DEPTH) for nm in SMALL] + [dfw]
    small_sum = _allreduce_small(small_vecs)
    for i, nm in enumerate(SMALL):
        grad[nm] = jnp.stack([small_sum[l * len(SMALL) + i].reshape(w[nm].shape[1:]) for l in range(DEPTH)])
    grad["final_norm_w"] = small_sum[-1]

    deltas, new_m, new_v = {}, {}, {}
    for nm in WEIGHTS:
        deltas[nm], new_m[nm], new_v[nm] = _adamw("adamw_" + nm, w[nm], grad[nm], m[nm], v[nm])
    return (loss, dx[None], *[grad[nm] for nm in WEIGHTS], *[deltas[nm] for nm in WEIGHTS],
            *[new_m[nm] for nm in WEIGHTS], *[new_v[nm] for nm in WEIGHTS])
```

```python
import functools

import jax
import jax.numpy as jnp
from jax import lax
from jax.experimental import pallas as pl
from jax.experimental.pallas import tpu as pltpu
from jax.experimental.pallas import tpu_sc as plsc

F32 = jnp.float32
BF16 = jnp.bfloat16
HI = lax.Precision.HIGHEST
SDS = jax.ShapeDtypeStruct

D_MODEL = 1024
DEPTH = 2
SSM_HEADS = 16
SSM_P = 64
SSM_N = 128
SSM_GROUPS = 2
SSM_CONV = 1536
GDN_HEADS = 8
GDN_DK = 128
GDN_QKV = 3072
CONV_K = 4
CHUNK = 64
FFN = 2816
IN_DIM = 8736
EPS = 1e-6
N_DEV = 8

Z_OFF = 0
GZ_OFF = 1024
G1_OFF = 2048
G2_OFF = 3072
QKV_OFF = 4096
XBC_OFF = 7168
SM_OFF = 8704
PROJ_W = 8960
LANE_A = 16
LANE_B = 24
O_Z, O_XBC, O_DT, O_QKV, O_GZ, O_A, O_B, O_G1, O_G2 = 0, 1024, 2560, 2576, 5648, 6672, 6680, 6688, 7712

ADAM_LR = 0.001
ADAM_B1 = 0.9
ADAM_B2 = 0.999
ADAM_EPS = 1e-08
ADAM_WD = 0.01
ADAM_STEP = 10

V7X_VMEM_LIMIT = 48 * 1024 * 1024

NN = ((1,), (0,))
NT = ((1,), (1,))
TN = ((0,), (0,))


def _bdot(a, b, dims):
    return lax.dot_general(a.astype(BF16), b.astype(BF16), (dims, ((), ())), preferred_element_type=F32)


def _hdot(a, b, dims=NN):
    return lax.dot_general(a, b, (dims, ((), ())), precision=HI, preferred_element_type=F32)


def _sigmoid(x):
    return 1.0 / (1.0 + jnp.exp(-x))


def _softplus(x):
    return jnp.maximum(x, 0.0) + jnp.log(1.0 + jnp.exp(-jnp.abs(x)))


def _params(dims):
    return pltpu.CompilerParams(dimension_semantics=dims, vmem_limit_bytes=V7X_VMEM_LIMIT)


def _rowsum(x):
    return jnp.sum(x, axis=-1, keepdims=True)


def _colsum(x):
    return jnp.sum(x, axis=0, keepdims=True)


def _matmul(name, mode, pairs, m, n, kdim, tm, tn, tk, out_dtypes, epi=None, extras=()):
    tm, tn, tk = min(tm, m), min(tn, n), min(tk, kdim)
    nk = kdim // tk
    assert m % tm == 0 and n % tn == 0 and kdim % tk == 0, (name, m, n, kdim, tm, tn, tk)
    in_specs, args = [], []
    for a, a_off, b, b_off in pairs:
        if mode == "nn":
            in_specs.append(pl.BlockSpec((tm, tk), lambda i, j, k, o=a_off: (i, k + o)))
            in_specs.append(pl.BlockSpec((tk, tn), lambda i, j, k, o=b_off: (k, j + o)))
            dims = NN
        elif mode == "nt":
            in_specs.append(pl.BlockSpec((tm, tk), lambda i, j, k, o=a_off: (i, k + o)))
            in_specs.append(pl.BlockSpec((tn, tk), lambda i, j, k, o=b_off: (j, k + o)))
            dims = NT
        else:
            in_specs.append(pl.BlockSpec((tk, tm), lambda i, j, k, o=a_off: (k, i + o)))
            in_specs.append(pl.BlockSpec((tk, tn), lambda i, j, k, o=b_off: (k, j + o)))
            dims = TN
        args += [a, b]
    for e, e_off in extras:
        in_specs.append(pl.BlockSpec((tm, tn), lambda i, j, k, o=e_off: (i, j + o)))
        args.append(e)
    npair, nex, nout = len(pairs), len(extras), len(out_dtypes)

    def body(*refs):
        prefs = refs[: 2 * npair]
        erefs = refs[2 * npair: 2 * npair + nex]
        orefs = refs[2 * npair + nex: 2 * npair + nex + nout]

        def finish(res):
            outs = (res,) if epi is None else epi(res, *[e[...] for e in erefs])
            for o, r in zip(orefs, outs):
                o[...] = r.astype(o.dtype)

        s = _bdot(prefs[0][...], prefs[1][...], dims)
        for p in range(1, npair):
            s = s + _bdot(prefs[2 * p][...], prefs[2 * p + 1][...], dims)
        if nk == 1:
            finish(s)
            return
        acc = refs[-1]
        k = pl.program_id(2)

        @pl.when(k == 0)
        def _():
            acc[...] = s

        @pl.when(k > 0)
        def _():
            acc[...] += s

        @pl.when(k == nk - 1)
        def _():
            finish(acc[...])

    out_shape = tuple(SDS((m, n), dt) for dt in out_dtypes)
    out_specs = tuple(pl.BlockSpec((tm, tn), lambda i, j, k: (i, j)) for _ in out_dtypes)
    res = pl.pallas_call(
        body, grid=(m // tm, n // tn, nk), in_specs=in_specs, out_specs=out_specs, out_shape=out_shape,
        scratch_shapes=[pltpu.VMEM((tm, tn), F32)] if nk > 1 else [], name=name,
        compiler_params=_params(("parallel", "parallel", "arbitrary")),
    )(*args)
    return res if nout > 1 else res[0]


def _rmsnorm_fwd(name, x, w):
    t, d = x.shape
    tm = min(512, t)

    def body(x_ref, w_ref, h_ref):
        xv = x_ref[...]
        r = lax.rsqrt(jnp.mean(xv * xv, axis=-1, keepdims=True) + EPS)
        h_ref[...] = (xv * r * w_ref[...]).astype(BF16)

    return pl.pallas_call(
        body, grid=(t // tm,),
        in_specs=[pl.BlockSpec((tm, d), lambda i: (i, 0)), pl.BlockSpec((1, d), lambda i: (0, 0))],
        out_specs=pl.BlockSpec((tm, d), lambda i: (i, 0)), out_shape=SDS((t, d), BF16), name=name,
        compiler_params=_params(("parallel",)),
    )(x, w.reshape(1, d))


def _rmsnorm_bwd(name, x, w, dh, dres):
    t, d = x.shape
    tm = min(512, t)

    def body(x_ref, w_ref, dh_ref, dres_ref, dx_ref, dxb_ref, dw_ref):
        xv = x_ref[...]
        r = lax.rsqrt(jnp.mean(xv * xv, axis=-1, keepdims=True) + EPS)
        xh = xv * r
        dhv = dh_ref[...].astype(F32)
        dxh = dhv * w_ref[...]
        dx = r * (dxh - xh * jnp.mean(dxh * xh, axis=-1, keepdims=True)) + dres_ref[...]
        dx_ref[...] = dx
        dxb_ref[...] = dx.astype(BF16)

        @pl.when(pl.program_id(0) == 0)
        def _():
            dw_ref[...] = jnp.zeros_like(dw_ref)

        dw_ref[...] += _colsum(dhv * xh)

    row = pl.BlockSpec((tm, d), lambda i: (i, 0))
    vec = pl.BlockSpec((1, d), lambda i: (0, 0))
    return pl.pallas_call(
        body, grid=(t // tm,), in_specs=[row, vec, row, row], out_specs=(row, row, vec),
        out_shape=(SDS((t, d), F32), SDS((t, d), BF16), SDS((1, d), F32)), name=name,
        compiler_params=_params(("arbitrary",)),
    )(x, w.reshape(1, d), dh, dres)


def _loss_head(name, x, w, tgt):
    t, d = x.shape
    tm = min(512, t)

    def body(x_ref, w_ref, t_ref, loss_ref, dx_ref, dxb_ref, dw_ref):
        xv = x_ref[...]
        wv = w_ref[...]
        r = lax.rsqrt(jnp.mean(xv * xv, axis=-1, keepdims=True) + EPS)
        xh = xv * r
        e = xh * wv - t_ref[...]
        dy = e * (1.0 / d)
        dxh = dy * wv
        dx = r * (dxh - xh * jnp.mean(dxh * xh, axis=-1, keepdims=True))
        dx_ref[...] = dx
        dxb_ref[...] = dx.astype(BF16)

        @pl.when(pl.program_id(0) == 0)
        def _():
            dw_ref[...] = jnp.zeros_like(dw_ref)
            loss_ref[...] = jnp.zeros_like(loss_ref)

        dw_ref[...] += _colsum(dy * xh)
        loss_ref[...] += 0.5 * jnp.sum(jnp.mean(e * e, axis=-1, keepdims=True), axis=0, keepdims=True)

    row = pl.BlockSpec((tm, d), lambda i: (i, 0))
    vec = pl.BlockSpec((1, d), lambda i: (0, 0))
    return pl.pallas_call(
        body, grid=(t // tm,), in_specs=[row, vec, row],
        out_specs=(pl.BlockSpec((1, 1), lambda i: (0, 0)), row, row, vec),
        out_shape=(SDS((1, 1), F32), SDS((t, d), F32), SDS((t, d), BF16), SDS((1, d), F32)), name=name,
        compiler_params=_params(("arbitrary",)),
    )(x, w.reshape(1, d), tgt)


def _shift_down(u, s, row):
    return jnp.where(row >= s, pltpu.roll(u, shift=s, axis=0), 0.0)


def _conv_fwd(name, src, col0, w, b):
    t = src.shape[0]
    c = w.shape[1]
    tc = 256
    assert c % tc == 0 and col0 % tc == 0

    def body(u_ref, w_ref, b_ref, o_ref):
        u = u_ref[...]
        wv = w_ref[...]
        row = lax.broadcasted_iota(jnp.int32, u.shape, 0)
        pre = b_ref[...] + wv[3:4, :] * u
        for s in range(1, CONV_K):
            pre = pre + wv[3 - s: 4 - s, :] * _shift_down(u, s, row)
        o_ref[...] = pre * _sigmoid(pre)

    return pl.pallas_call(
        body, grid=(c // tc,),
        in_specs=[pl.BlockSpec((t, tc), lambda j: (0, j + col0 // tc)), pl.BlockSpec((CONV_K, tc), lambda j: (0, j)),
                  pl.BlockSpec((1, tc), lambda j: (0, j))],
        out_specs=pl.BlockSpec((t, tc), lambda j: (0, j)), out_shape=SDS((t, c), F32), name=name,
        compiler_params=_params(("parallel",)),
    )(src, w, b)


def _conv_bwd(name, src, col0, w, b, dact):
    t = src.shape[0]
    c = w.shape[1]
    tc = 128

    def body(u_ref, w_ref, b_ref, da_ref, du_ref, dw_ref, db_ref):
        u = u_ref[...]
        wv = w_ref[...]
        row = lax.broadcasted_iota(jnp.int32, u.shape, 0)
        shifted = [u] + [_shift_down(u, s, row) for s in range(1, CONV_K)]
        pre = b_ref[...] + wv[3:4, :] * u
        for s in range(1, CONV_K):
            pre = pre + wv[3 - s: 4 - s, :] * shifted[s]
        sg = _sigmoid(pre)
        dpre = da_ref[...] * (sg * (1.0 + pre * (1.0 - sg)))
        du = wv[3:4, :] * dpre
        for s in range(1, CONV_K):
            du = du + wv[3 - s: 4 - s, :] * jnp.where(row < t - s, pltpu.roll(dpre, shift=t - s, axis=0), 0.0)
        du_ref[...] = du.astype(BF16)
        for s in range(CONV_K):
            dw_ref[3 - s: 4 - s, :] = _colsum(dpre * shifted[s])
        db_ref[...] = _colsum(dpre)

    return pl.pallas_call(
        body, grid=(c // tc,),
        in_specs=[pl.BlockSpec((t, tc), lambda j: (0, j + col0 // tc)), pl.BlockSpec((CONV_K, tc), lambda j: (0, j)),
                  pl.BlockSpec((1, tc), lambda j: (0, j)), pl.BlockSpec((t, tc), lambda j: (0, j))],
        out_specs=(pl.BlockSpec((t, tc), lambda j: (0, j)), pl.BlockSpec((CONV_K, tc), lambda j: (0, j)),
                   pl.BlockSpec((1, tc), lambda j: (0, j))),
        out_shape=(SDS((t, c), BF16), SDS((CONV_K, c), F32), SDS((1, c), F32)), name=name,
        compiler_params=_params(("parallel",)),
    )(src, w, b, dact)


def _tri(q):
    ii = lax.broadcasted_iota(jnp.int32, (q, q), 0)
    jj = lax.broadcasted_iota(jnp.int32, (q, q), 1)
    return ii, jj


def _dot01(x, r01, dims, terms=3):
    out, rem = None, x
    for i in range(terms):
        hi = rem.astype(BF16)
        d = lax.dot_general(hi, r01, (dims, ((), ())), preferred_element_type=F32)
        out = d if out is None else out + d
        if i + 1 < terms:
            rem = rem - hi.astype(F32)
    return out


def _ssd_common(act, sm, dtb, arow, rmat):
    q = CHUNK
    ii, jj = _tri(q)
    lane = lax.broadcasted_iota(jnp.int32, (q, 128), 1)
    m16 = lane < SSM_HEADS
    dt = jnp.where(m16, _softplus(sm + dtb), 0.0)
    a = dt * arow
    tril = (ii >= jj).astype(F32)
    triu = (ii <= jj).astype(F32)
    acum = _hdot(tril, a)
    acum_r = _hdot(a.T, triu)
    dtx = _dot01(dt, rmat, NN)
    acx = _dot01(acum, rmat, NN)
    ex = jnp.exp(acx)
    alx = acx[q - 1: q, :]
    dex = jnp.exp(alx - acx)
    xs = act[:, :1024]
    return dict(ii=ii, jj=jj, m16=m16, dt=dt, a=a, triu=triu, acum=acum, acum_r=acum_r, dtx=dtx, ex=ex, dex=dex,
                elx=jnp.exp(alx), xs=xs, x=xs * dtx)


def _ssd_lmat(cm, h):
    return jnp.where(cm["ii"] >= cm["jj"], jnp.exp(cm["acum"][:, h: h + 1] - cm["acum_r"][h: h + 1, :]), 0.0)


def _ssd_fwd(name, act, proj, dtb, arow, dxrow, nw, rmat):
    t = act.shape[0]
    q = CHUNK
    nc = t // q
    hg = SSM_HEADS // SSM_GROUPS
    gw = hg * SSM_P

    def body(act_ref, z_ref, sm_ref, dtb_ref, arow_ref, dx_ref, nw_ref, r_ref, y_ref, ys_ref, st_ref, s_scr, yd_scr):
        @pl.when(pl.program_id(0) == 0)
        def _():
            s_scr[...] = jnp.zeros_like(s_scr)

        s_all = s_scr[...]
        st_ref[0] = s_all
        actv = act_ref[...]
        cm = _ssd_common(actv, sm_ref[...], dtb_ref[...], arow_ref[...], r_ref[...])
        x = cm["x"]
        xd = x * cm["dex"]
        yoffs, snew = [], []
        for g in range(SSM_GROUPS):
            bg = actv[:, 1024 + g * SSM_N: 1024 + (g + 1) * SSM_N]
            cg = actv[:, 1280 + g * SSM_N: 1280 + (g + 1) * SSM_N]
            sg = s_all[:, g * gw: (g + 1) * gw]
            cb = _bdot(cg, bg, NT)
            yoffs.append(_bdot(cg, sg, NN))
            snew.append(_bdot(bg, xd[:, g * gw: (g + 1) * gw], TN))
            for r in range(hg):
                h = g * hg + r
                mm = cb * _ssd_lmat(cm, h)
                yd_scr[:, h * SSM_P: (h + 1) * SSM_P] = _bdot(mm, x[:, h * SSM_P: (h + 1) * SSM_P], NN)
        s_scr[...] = s_all * cm["elx"] + jnp.concatenate(snew, axis=1)
        ysc = yd_scr[...] + jnp.concatenate(yoffs, axis=1) * cm["ex"]
        ys_ref[...] = ysc
        zv = z_ref[...]
        yg = (ysc + dx_ref[...] * cm["xs"]) * (zv * _sigmoid(zv))
        nwv = nw_ref[...]
        for g in range(SSM_GROUPS):
            sl = yg[:, g * gw: (g + 1) * gw]
            rr = lax.rsqrt(jnp.mean(sl * sl, axis=-1, keepdims=True) + EPS)
            y_ref[:, g * gw: (g + 1) * gw] = (sl * rr * nwv[:, g * gw: (g + 1) * gw]).astype(BF16)

    vec128 = pl.BlockSpec((1, 128), lambda c: (0, 0))
    vec1k = pl.BlockSpec((1, 1024), lambda c: (0, 0))
    return pl.pallas_call(
        body, grid=(nc,),
        in_specs=[pl.BlockSpec((q, SSM_CONV), lambda c: (c, 0)), pl.BlockSpec((q, 1024), lambda c: (c, Z_OFF // 1024)),
                  pl.BlockSpec((q, 128), lambda c: (c, SM_OFF // 128)), vec128, vec128, vec1k, vec1k,
                  pl.BlockSpec((128, 1024), lambda c: (0, 0))],
        out_specs=(pl.BlockSpec((q, 1024), lambda c: (c, 0)), pl.BlockSpec((q, 1024), lambda c: (c, 0)),
                   pl.BlockSpec((1, 128, 1024), lambda c: (c, 0, 0))),
        out_shape=(SDS((t, 1024), BF16), SDS((t, 1024), F32), SDS((nc, 128, 1024), F32)),
        scratch_shapes=[pltpu.VMEM((128, 1024), F32), pltpu.VMEM((q, 1024), F32)], name=name,
        compiler_params=_params(("arbitrary",)),
    )(act, proj, proj, dtb, arow, dxrow, nw, rmat)


def _ssd_bwd(name, act, proj, dtb, arow, dxrow, nw, rmat, ysc, states, dy):
    t = act.shape[0]
    q = CHUNK
    nc = t // q
    hg = SSM_HEADS // SSM_GROUPS
    gw = hg * SSM_P

    def body(act_ref, z_ref, sm_ref, dtb_ref, arow_ref, dx_ref, nw_ref, r_ref, ys_ref, st_ref, dy_ref,
             dact_ref, dz_ref, dsm_ref, dnw_ref, dd_ref, dal_ref, ddtb_ref, ds_scr, dxd_scr):
        @pl.when(pl.program_id(0) == 0)
        def _():
            ds_scr[...] = jnp.zeros_like(ds_scr)
            dnw_ref[...] = jnp.zeros_like(dnw_ref)
            dd_ref[...] = jnp.zeros_like(dd_ref)
            dal_ref[...] = jnp.zeros_like(dal_ref)
            ddtb_ref[...] = jnp.zeros_like(ddtb_ref)

        actv = act_ref[...]
        smv = sm_ref[...]
        rmat_v = r_ref[...]
        cm = _ssd_common(actv, smv, dtb_ref[...], arow_ref[...], rmat_v)
        ii, jj = cm["ii"], cm["jj"]
        x, xs = cm["x"], cm["xs"]
        s_all = st_ref[0]
        dsn = ds_scr[...]
        ysv = ys_ref[...]
        dxr = dx_ref[...]
        y = ysv + dxr * xs
        zv = z_ref[...]
        sz = _sigmoid(zv)
        silz = zv * sz
        yg = y * silz
        dout = dy_ref[...]
        nwv = nw_ref[...]
        dyn = dout * nwv
        yn_parts, dyg_parts = [], []
        for g in range(SSM_GROUPS):
            sl = yg[:, g * gw: (g + 1) * gw]
            rr = lax.rsqrt(jnp.mean(sl * sl, axis=-1, keepdims=True) + EPS)
            yn = sl * rr
            dn = dyn[:, g * gw: (g + 1) * gw]
            yn_parts.append(yn)
            dyg_parts.append(rr * (dn - yn * jnp.mean(dn * yn, axis=-1, keepdims=True)))
        dnw_ref[...] += _colsum(dout * jnp.concatenate(yn_parts, axis=1))
        dyg = jnp.concatenate(dyg_parts, axis=1)
        dyv = dyg * silz
        dz_ref[...] = (dyg * y * (sz * (1.0 + zv * (1.0 - sz)))).astype(BF16)
        dd_ref[...] += _dot01(_colsum(dyv * xs), rmat_v, NT)
        dxs = dyv * dxr
        dcs = dyv * cm["ex"]
        xd = x * cm["dex"]
        dxst_parts, ds_parts, db_parts, dc_parts, yoff_parts, wcol_rows = [], [], [], [], [], []
        lane128 = lax.broadcasted_iota(jnp.int32, (q, 128), 1)
        wrow = jnp.zeros((q, 128), F32)
        for g in range(SSM_GROUPS):
            bg = actv[:, 1024 + g * SSM_N: 1024 + (g + 1) * SSM_N]
            cg = actv[:, 1280 + g * SSM_N: 1280 + (g + 1) * SSM_N]
            sg = s_all[:, g * gw: (g + 1) * gw]
            dsng = dsn[:, g * gw: (g + 1) * gw]
            dcsg = dcs[:, g * gw: (g + 1) * gw]
            dcg = _bdot(dcsg, sg, NT)
            yoff_parts.append(_bdot(cg, sg, NN))
            ds_parts.append(_bdot(cg, dcsg, TN))
            dxst_parts.append(_bdot(bg, dsng, NN))
            dbg = _bdot(xd[:, g * gw: (g + 1) * gw], dsng, NT)
            cb = _bdot(cg, bg, NT)
            dcb = jnp.zeros((q, q), F32)
            for r in range(hg):
                h = g * hg + r
                lm = _ssd_lmat(cm, h)
                mm = cb * lm
                dyh = dyv[:, h * SSM_P: (h + 1) * SSM_P]
                dm = jnp.where(ii >= jj, _bdot(dyh, x[:, h * SSM_P: (h + 1) * SSM_P], NT), 0.0)
                dxd_scr[:, h * SSM_P: (h + 1) * SSM_P] = _bdot(mm, dyh, TN)
                dcb = dcb + dm * lm
                wm = dm * mm
                wrow = wrow + jnp.where(lane128 == h, _rowsum(wm), 0.0)
                wcol_rows.append(_colsum(wm))
            dc_parts.append(dcg + _bdot(dcb, bg, NN))
            db_parts.append(dbg + _bdot(dcb, cg, TN))
        dxst = jnp.concatenate(dxst_parts, axis=1) * cm["dex"]
        dx = dxd_scr[...] + dxst
        ds_scr[...] = jnp.concatenate(ds_parts, axis=1) + dsn * cm["elx"]
        wcol = jnp.concatenate(wcol_rows + [jnp.zeros((128 - SSM_HEADS, q), F32)], axis=0).T
        yoff = jnp.concatenate(yoff_parts, axis=1) * cm["ex"]
        xdxst = x * dxst
        dac = wrow - wcol + _dot01(dyv * yoff - xdxst, rmat_v, NT)
        last = _dot01(_colsum(dsn * s_all) * cm["elx"] + _colsum(xdxst), rmat_v, NT)
        rowq = lax.broadcasted_iota(jnp.int32, (q, 128), 0)
        dac = dac + jnp.where(rowq == q - 1, last, 0.0)
        da = _hdot(cm["triu"], dac)
        arow_v = arow_ref[...]
        ddt = da * arow_v + _dot01(dx * xs, rmat_v, NT)
        dxs = dxs + dx * cm["dtx"]
        dal_ref[...] += _colsum(da * cm["a"])
        ddtraw = jnp.where(cm["m16"], ddt * _sigmoid(smv + dtb_ref[...]), 0.0)
        ddtb_ref[...] += _colsum(ddtraw)
        dsm_ref[...] = ddtraw.astype(BF16)
        dact_ref[:, :1024] = dxs
        for g in range(SSM_GROUPS):
            dact_ref[:, 1024 + g * SSM_N: 1024 + (g + 1) * SSM_N] = db_parts[g]
            dact_ref[:, 1280 + g * SSM_N: 1280 + (g + 1) * SSM_N] = dc_parts[g]

    rev = lambda c: nc - 1 - c
    vec128 = pl.BlockSpec((1, 128), lambda c: (0, 0))
    vec1k = pl.BlockSpec((1, 1024), lambda c: (0, 0))
    return pl.pallas_call(
        body, grid=(nc,),
        in_specs=[pl.BlockSpec((q, SSM_CONV), lambda c: (rev(c), 0)),
                  pl.BlockSpec((q, 1024), lambda c: (rev(c), Z_OFF // 1024)),
                  pl.BlockSpec((q, 128), lambda c: (rev(c), SM_OFF // 128)), vec128, vec128, vec1k, vec1k,
                  pl.BlockSpec((128, 1024), lambda c: (0, 0)),
                  pl.BlockSpec((q, 1024), lambda c: (rev(c), 0)), pl.BlockSpec((1, 128, 1024), lambda c: (rev(c), 0, 0)),
                  pl.BlockSpec((q, 1024), lambda c: (rev(c), 0))],
        out_specs=(pl.BlockSpec((q, SSM_CONV), lambda c: (rev(c), 0)), pl.BlockSpec((q, 1024), lambda c: (rev(c), 0)),
                   pl.BlockSpec((q, 128), lambda c: (rev(c), 0)), vec1k, vec128, vec128, vec128),
        out_shape=(SDS((t, SSM_CONV), F32), SDS((t, 1024), BF16), SDS((t, 128), BF16), SDS((1, 1024), F32),
                   SDS((1, 128), F32), SDS((1, 128), F32), SDS((1, 128), F32)),
        scratch_shapes=[pltpu.VMEM((128, 1024), F32), pltpu.VMEM((q, 1024), F32)], name=name,
        compiler_params=_params(("arbitrary",)),
    )(act, proj, proj, dtb, arow, dxrow, nw, rmat, ysc, states, dy)


def _split(a):
    hi = a.astype(BF16)
    return hi, (a - hi.astype(F32)).astype(BF16)


def _dot3(a, b, dims=NN):
    (ah, al), (bh, bl) = a, b

    def d(x, y):
        return lax.dot_general(x, y, (dims, ((), ())), preferred_element_type=F32)

    return d(ah, bh) + (d(ah, bl) + d(al, bh))


def _tri_inverses(amats, ii, jj):
    eye = jnp.where(ii == jj, 1.0, 0.0)
    tms = [eye - a for a in amats]
    sp = [_split(a) for a in amats]
    for _ in range(5):
        sp = [_split(_dot3(s, s)) for s in sp]
        tms = [t + _dot3(_split(t), s) for t, s in zip(tms, sp)]
    return tms


def _gdn_common(sm, gb, garow):
    q = CHUNK
    ii, jj = _tri(q)
    lane = lax.broadcasted_iota(jnp.int32, (q, 128), 1)
    ma = (lane >= LANE_A) & (lane < LANE_A + GDN_HEADS)
    spre = sm + gb
    g = jnp.where(ma, garow * _softplus(spre), 0.0)
    beta = _sigmoid(sm)
    tril = (ii >= jj).astype(F32)
    triu = (ii <= jj).astype(F32)
    gc = _hdot(tril, g)
    gc_r = _hdot(g.T, triu)
    return dict(ii=ii, jj=jj, lane=lane, ma=ma, spre=spre, g=g, beta=beta, triu=triu, gc=gc, gc_r=gc_r)


def _each(f, *lists):
    return [f(*xs) for xs in zip(*lists)]


GDN_SCALE = GDN_DK ** -0.5


def _gdn_heads(cm, actv, states):
    q = CHUNK
    ii, jj = cm["ii"], cm["jj"]
    heads = range(GDN_HEADS)
    qr = [actv[:, h * 128: (h + 1) * 128] for h in heads]
    kr = [actv[:, 1024 + h * 128: 1024 + (h + 1) * 128] for h in heads]
    v = [actv[:, 2048 + h * 128: 2048 + (h + 1) * 128] for h in heads]
    rq = _each(lambda x: lax.rsqrt(_rowsum(x * x) + EPS), qr)
    rk = _each(lambda x: lax.rsqrt(_rowsum(x * x) + EPS), kr)
    qn = _each(lambda x, r: x * r * GDN_SCALE, qr, rq)
    kn = _each(lambda x, r: x * r, kr, rk)
    gcc = [cm["gc"][:, LANE_A + h: LANE_A + h + 1] for h in heads]
    gcr = [cm["gc_r"][LANE_A + h: LANE_A + h + 1, :] for h in heads]
    bcol = [cm["beta"][:, LANE_B + h: LANE_B + h + 1] for h in heads]
    dm = _each(lambda c, r: jnp.where(ii >= jj, jnp.exp(c - r), 0.0), gcc, gcr)
    kq = _each(lambda k, a: _bdot(jnp.concatenate([k, a], axis=0), k, NT), kn, qn)
    ak = _each(lambda x, d: jnp.where(ii > jj, x[:q] * d, 0.0), kq, dm)
    qkm = _each(lambda x, d: jnp.where(ii >= jj, x[q:] * d, 0.0), kq, dm)
    tm = _tri_inverses(_each(lambda a, b: a * b, ak, bcol), ii, jj)
    eg = _each(jnp.exp, gcc)
    gl = [c[q - 1: q, :] for c in gcc]
    rm = _each(lambda vv, k, b, e: jnp.concatenate([vv * b, k * (b * e)], axis=1), v, kn, bcol, eg)
    tt = _each(lambda t, r: _dot3(_split(t), _split(r)), tm, rm)
    w = [t[:, 128:] for t in tt]
    qg = _each(lambda a, e: a * e, qn, eg)
    ws = _each(lambda ww, a, s: _bdot(jnp.concatenate([ww, a], axis=0), s, NN), w, qg, states)
    vnew = _each(lambda t, x: t[:, :128] - x[:q], tt, ws)
    return dict(qr=qr, v=v, rq=rq, rk=rk, qn=qn, kn=kn, gcc=gcc, bcol=bcol, dm=dm, ak=ak, tm=tm, eg=eg, gl=gl,
                egl=_each(jnp.exp, gl), ed=_each(lambda g, c: jnp.exp(g - c), gl, gcc), tt=tt, w=w, vnew=vnew, qkm=qkm,
                qg=qg, qgs=[x[q:] for x in ws])


def _gdn_fwd(name, act, proj, gb, garow, gnw):
    t = act.shape[0]
    q = CHUNK
    nc = t // q

    def body(act_ref, gz_ref, sm_ref, gb_ref, ga_ref, nw_ref, y_ref, o_ref, st_ref, s_scr):
        @pl.when(pl.program_id(0) == 0)
        def _():
            s_scr[...] = jnp.zeros_like(s_scr)

        st_ref[0] = s_scr[...]
        actv = act_ref[...]
        cm = _gdn_common(sm_ref[...], gb_ref[...], ga_ref[...])
        nwv = nw_ref[...]
        gzv = gz_ref[...]
        states = [s_scr[h * 128: (h + 1) * 128, :] for h in range(GDN_HEADS)]
        hd = _gdn_heads(cm, actv, states)
        outs = _each(lambda qs, m, vn: qs + _bdot(m, vn, NN), hd["qgs"], hd["qkm"], hd["vnew"])
        snew = _each(lambda s, e, k, d, vn: s * e + _bdot(k * d, vn, TN), states, hd["egl"], hd["kn"], hd["ed"], hd["vnew"])
        for h in range(GDN_HEADS):
            o = outs[h]
            s_scr[h * 128: (h + 1) * 128, :] = snew[h]
            o_ref[:, h * 128: (h + 1) * 128] = o
            rr = lax.rsqrt(jnp.mean(o * o, axis=-1, keepdims=True) + EPS)
            gz = gzv[:, h * 128: (h + 1) * 128]
            y_ref[:, h * 128: (h + 1) * 128] = (o * rr * nwv * (gz * _sigmoid(gz))).astype(BF16)

    vec128 = pl.BlockSpec((1, 128), lambda c: (0, 0))
    return pl.pallas_call(
        body, grid=(nc,),
        in_specs=[pl.BlockSpec((q, GDN_QKV), lambda c: (c, 0)), pl.BlockSpec((q, 1024), lambda c: (c, GZ_OFF // 1024)),
                  pl.BlockSpec((q, 128), lambda c: (c, SM_OFF // 128)), vec128, vec128, vec128],
        out_specs=(pl.BlockSpec((q, 1024), lambda c: (c, 0)), pl.BlockSpec((q, 1024), lambda c: (c, 0)),
                   pl.BlockSpec((1, 1024, 128), lambda c: (c, 0, 0))),
        out_shape=(SDS((t, 1024), BF16), SDS((t, 1024), F32), SDS((nc, 1024, 128), F32)),
        scratch_shapes=[pltpu.VMEM((1024, 128), F32)], name=name, compiler_params=_params(("arbitrary",)),
    )(act, proj, proj, gb, garow, gnw)


def _gdn_bwd(name, act, proj, gb, garow, gnw, oraw, states, dy):
    t = act.shape[0]
    q = CHUNK
    nc = t // q

    def body(act_ref, gz_ref, sm_ref, gb_ref, ga_ref, nw_ref, o_ref, st_ref, dy_ref,
             dact_ref, dgz_ref, dsm_ref, dnw_ref, dal_ref, dgb_ref, ds_scr):
        @pl.when(pl.program_id(0) == 0)
        def _():
            ds_scr[...] = jnp.zeros_like(ds_scr)
            dnw_ref[...] = jnp.zeros_like(dnw_ref)
            dal_ref[...] = jnp.zeros_like(dal_ref)
            dgb_ref[...] = jnp.zeros_like(dgb_ref)

        actv = act_ref[...]
        smv = sm_ref[...]
        garow_v = ga_ref[...]
        cm = _gdn_common(smv, gb_ref[...], garow_v)
        ii, jj, lane = cm["ii"], cm["jj"], cm["lane"]
        nwv = nw_ref[...]
        rowq = lax.broadcasted_iota(jnp.int32, (q, 1), 0)
        dgc_all = jnp.zeros((q, 128), F32)
        dbeta_all = jnp.zeros((q, 128), F32)
        dnw_acc = jnp.zeros((1, 128), F32)
        heads = range(GDN_HEADS)
        sts = [st_ref[0, h * 128: (h + 1) * 128, :] for h in heads]
        dsn = [ds_scr[h * 128: (h + 1) * 128, :] for h in heads]
        ov, gzv, dyv = o_ref[...], gz_ref[...], dy_ref[...]
        hd = _gdn_heads(cm, actv, sts)
        qn, kn, v, eg, ed, egl, bcol = hd["qn"], hd["kn"], hd["v"], hd["eg"], hd["ed"], hd["egl"], hd["bcol"]
        vnew, qkm, qg, w, tt, dm, ak = hd["vnew"], hd["qkm"], hd["qg"], hd["w"], hd["tt"], hd["dm"], hd["ak"]
        do = []
        for h in heads:
            hs = slice(h * 128, (h + 1) * 128)
            o = ov[:, hs]
            rr = lax.rsqrt(jnp.mean(o * o, axis=-1, keepdims=True) + EPS)
            on = o * rr
            gz = gzv[:, hs]
            sz = _sigmoid(gz)
            silz = gz * sz
            dyh = dyv[:, hs]
            dnw_acc = dnw_acc + _colsum(dyh * on * silz)
            dgz_ref[:, hs] = (dyh * on * nwv * (sz * (1.0 + gz * (1.0 - sz)))).astype(BF16)
            don = dyh * nwv * silz
            do.append(rr * (don - on * jnp.mean(don * on, axis=-1, keepdims=True)))
        kd = _each(lambda k, e: k * e, kn, ed)
        dkd = _each(lambda vn, d: _bdot(vn, d, NT), vnew, dsn)
        dvnew_a = _each(lambda k, d: _bdot(k, d, NN), kd, dsn)
        ded = _each(lambda a, b: _rowsum(a * b), dkd, kd)
        dgl = _each(lambda d, s, e, de: jnp.sum(_rowsum(d * s), axis=0, keepdims=True) * e + _colsum(de), dsn, sts, egl, ded)
        dqk = _each(lambda d, vn: jnp.where(ii >= jj, _bdot(d, vn, NT), 0.0), do, vnew)
        dvnew = _each(lambda a, m, d: a + _bdot(m, d, TN), dvnew_a, qkm, do)
        pq = _each(lambda a, b: a * b, dqk, dm)
        w1 = _each(lambda a, b: a * b, dqk, qkm)
        dod = _each(lambda a, b: jnp.concatenate([a, b], axis=0), do, dvnew)
        dos = _each(lambda x, s: _bdot(x, s, NT), dod, sts)
        dqg = [x[:q] for x in dos]
        dw = [-x[q:] for x in dos]
        ds12 = _each(lambda a, ww, x: _bdot(jnp.concatenate([a, -ww], axis=0), x, TN), qg, w, dod)
        dr = _each(lambda t, a, b: _dot3(_split(t), _split(jnp.concatenate([a, b], axis=1)), TN), hd["tm"], dvnew, dw)
        da = _each(lambda r, t: jnp.where(ii > jj, -_dot3(_split(r), _split(t), NT), 0.0), dr, tt)
        sk = _each(lambda r, k: _rowsum(r[:, 128:] * k), dr, kn)
        pk = _each(lambda a, d, b: a * d * b, da, dm, bcol)
        pkn = _each(lambda p, pp, k: _bdot(jnp.concatenate([p, pp + pp.T], axis=0), k, NN), pq, pk, kn)
        dq = _each(lambda a, e, x: a * e + x[:q], dqg, eg, pkn)
        dk = _each(lambda a, e, p, x, r, b, eg_, y: a * e + _bdot(p, x, TN) + r[:, 128:] * (b * eg_) + y[q:],
                   dkd, ed, pq, qn, dr, bcol, eg, pkn)
        w2 = _each(lambda a, k, b: a * (k * b), da, ak, bcol)
        for h in heads:
            hs = slice(h * 128, (h + 1) * 128)
            dgc = (-ded[h] + _rowsum(dqg[h] * qg[h]) + _rowsum(w1[h]) - _rowsum(w1[h].T) + sk[h] * bcol[h] * eg[h]
                   + _rowsum(w2[h]) - _rowsum(w2[h].T) + jnp.where(rowq == q - 1, dgl[h], 0.0))
            dbeta = _rowsum(dr[h][:, :128] * v[h]) + sk[h] * eg[h] + _rowsum(da[h] * ak[h])
            qhat = hd["qr"][h] * hd["rq"][h]
            dqhat = dq[h] * GDN_SCALE
            dact_ref[:, hs] = hd["rq"][h] * (dqhat - qhat * _rowsum(dqhat * qhat))
            dact_ref[:, 1024 + h * 128: 1024 + (h + 1) * 128] = hd["rk"][h] * (dk[h] - kn[h] * _rowsum(dk[h] * kn[h]))
            dact_ref[:, 2048 + h * 128: 2048 + (h + 1) * 128] = dr[h][:, :128] * bcol[h]
            dgc_all = dgc_all + jnp.where(lane == LANE_A + h, dgc, 0.0)
            dbeta_all = dbeta_all + jnp.where(lane == LANE_B + h, dbeta, 0.0)
            ds_scr[hs, :] = dsn[h] * egl[h] + ds12[h]
        dnw_ref[...] += dnw_acc
        dg = _hdot(cm["triu"], dgc_all)
        da_raw = jnp.where(cm["ma"], dg * garow_v * _sigmoid(cm["spre"]), 0.0)
        dal_ref[...] += _colsum(dg * cm["g"])
        dgb_ref[...] += _colsum(da_raw)
        beta = cm["beta"]
        dsm_ref[...] = (da_raw + dbeta_all * beta * (1.0 - beta)).astype(BF16)

    rev = lambda c: nc - 1 - c
    vec128 = pl.BlockSpec((1, 128), lambda c: (0, 0))
    return pl.pallas_call(
        body, grid=(nc,),
        in_specs=[pl.BlockSpec((q, GDN_QKV), lambda c: (rev(c), 0)),
                  pl.BlockSpec((q, 1024), lambda c: (rev(c), GZ_OFF // 1024)),
                  pl.BlockSpec((q, 128), lambda c: (rev(c), SM_OFF // 128)), vec128, vec128, vec128,
                  pl.BlockSpec((q, 1024), lambda c: (rev(c), 0)), pl.BlockSpec((1, 1024, 128), lambda c: (rev(c), 0, 0)),
                  pl.BlockSpec((q, 1024), lambda c: (rev(c), 0))],
        out_specs=(pl.BlockSpec((q, GDN_QKV), lambda c: (rev(c), 0)), pl.BlockSpec((q, 1024), lambda c: (rev(c), 0)),
                   pl.BlockSpec((q, 128), lambda c: (rev(c), 0)), vec128, vec128, vec128),
        out_shape=(SDS((t, GDN_QKV), F32), SDS((t, 1024), BF16), SDS((t, 128), BF16), SDS((1, 128), F32),
                   SDS((1, 128), F32), SDS((1, 128), F32)),
        scratch_shapes=[pltpu.VMEM((1024, 128), F32)], name=name, compiler_params=_params(("arbitrary",)),
    )(act, proj, proj, gb, garow, gnw, oraw, states, dy)


def _row_tile(r):
    for cand in (512, 256, 128, 64, 32, 16, 8):
        if r % cand == 0:
            return cand
    return r


def _sum_terms(name, terms, out_dtype):
    shape = terms[0][0].shape[1:]
    c = shape[-1]
    r = 1
    for s in shape[:-1]:
        r *= s
    tr = min(_row_tile(r), 256)
    n = len(terms)

    def body(*refs):
        acc = refs[0][...].astype(F32)
        for k in range(1, n):
            acc = acc + refs[k][...].astype(F32)
        refs[n][...] = acc.astype(out_dtype)

    in_specs = [pl.BlockSpec((None, tr, c), lambda i, q=lead: (q, i, 0)) for _, lead in terms]
    args = [a.reshape(a.shape[0], r, c) for a, _ in terms]
    out = pl.pallas_call(body, grid=(r // tr,), in_specs=in_specs, out_specs=pl.BlockSpec((tr, c), lambda i: (i, 0)),
                         out_shape=SDS((r, c), out_dtype), name=name, compiler_params=_params(("parallel",)))(*args)
    return out.reshape(shape)


def _adamw(name, w, g, m, v):
    shape = w.shape
    c = shape[-1]
    per_layer = isinstance(g, (list, tuple))
    nl = len(g) if per_layer else 1
    gs = [a.reshape(-1, c) for a in g] if per_layer else [g.reshape(-1, c)]
    r = gs[0].shape[0]
    w3, m3, v3 = (a.reshape(nl, r, c) for a in (w, m, v))
    tr = min(_row_tile(r), 256)

    def body(*refs):
        w_ref, m_ref, v_ref = refs[:3]
        g_refs = refs[3: 3 + nl]
        go_ref, d_ref, nm_ref, nv_ref = refs[3 + nl:]
        layer = pl.program_id(0)
        gv = g_refs[0][...]
        for k in range(1, nl):
            gv = jnp.where(layer == k, g_refs[k][...], gv)
        mn = ADAM_B1 * m_ref[...] + (1.0 - ADAM_B1) * gv
        vn = ADAM_B2 * v_ref[...] + (1.0 - ADAM_B2) * (gv * gv)
        m_hat = mn / (1.0 - ADAM_B1 ** ADAM_STEP)
        v_hat = vn / (1.0 - ADAM_B2 ** ADAM_STEP)
        go_ref[...] = gv
        d_ref[...] = -ADAM_LR * (m_hat / (jnp.sqrt(v_hat) + ADAM_EPS) + ADAM_WD * w_ref[...])
        nm_ref[...] = mn
        nv_ref[...] = vn

    spec3 = pl.BlockSpec((None, tr, c), lambda l, i: (l, i, 0))
    gspec = pl.BlockSpec((tr, c), lambda l, i: (i, 0))
    outs = pl.pallas_call(body, grid=(nl, r // tr), in_specs=[spec3] * 3 + [gspec] * nl, out_specs=(spec3,) * 4,
                          out_shape=(SDS((nl, r, c), F32),) * 4, name=name,
                          compiler_params=_params(("parallel", "parallel")))(w3, m3, v3, *gs)
    return tuple(o.reshape(shape) for o in outs)


ANY = pl.BlockSpec(memory_space=pl.ANY)
MESH = pl.DeviceIdType.MESH


def _allgather(name, xs):
    n = len(xs)

    def body(*refs):
        x_refs, out_refs = refs[:n], refs[n: 2 * n]
        send_sems, recv_sems, local_sems = refs[2 * n:]
        x, y, cc = lax.axis_index("x"), lax.axis_index("y"), lax.axis_index("c")
        me, sibling = (x, y, cc), (x, y, 1 - cc)
        chips = [(1 - x, y), (x, 1 - y), (1 - x, 1 - y)]

        def rows(a, px, py, pc):
            return out_refs[a].at[4 * px + 2 * py + pc]

        def copy(a, k, block, to, src=None):
            return pltpu.make_async_remote_copy(
                src_ref=rows(a, *block) if src is None else src, dst_ref=rows(a, *block),
                send_sem=send_sems.at[7 * a + k], recv_sem=recv_sems.at[7 * a + k], device_id=to, device_id_type=MESH)

        mine = [pltpu.make_async_copy(x_refs[a], rows(a, *me), local_sems.at[a]) for a in range(n)]
        for cp in mine:
            cp.start()
        first = []
        for a in range(n):
            first.append(copy(a, 0, me, sibling, src=x_refs[a]))
            first += [copy(a, 1 + j, me, (*chip, cc), src=x_refs[a]) for j, chip in enumerate(chips)]
        for cp in first:
            cp.start()
        passed = []
        for j, chip in enumerate(chips):
            for a in range(n):
                copy(a, 1 + j, (*chip, cc), me).wait_recv()
                fwd = copy(a, 4 + j, (*chip, cc), sibling)
                fwd.start()
                passed.append(fwd)
        for a in range(n):
            copy(a, 0, sibling, me).wait_recv()
        for j, chip in enumerate(chips):
            for a in range(n):
                copy(a, 4 + j, (*chip, 1 - cc), me).wait_recv()
        for cp in first + passed:
            cp.wait_send()
        for cp in mine:
            cp.wait()

    return pl.pallas_call(
        body, out_shape=tuple(SDS((N_DEV,) + a.shape, a.dtype) for a in xs), in_specs=[ANY] * n, out_specs=(ANY,) * n,
        scratch_shapes=[pltpu.SemaphoreType.DMA((7 * n,)), pltpu.SemaphoreType.DMA((7 * n,)),
                        pltpu.SemaphoreType.DMA((n,))],
        name=name,
    )(*xs)


def _allgather_seq(name, xs, collective_id):
    n = len(xs)
    x_refs = [jax.new_ref(a, memory_space=pltpu.MemorySpace.HBM) for a in xs]
    out_refs = [jax.empty_ref(SDS((N_DEV,) + a.shape, a.dtype), memory_space=pltpu.MemorySpace.HBM) for a in xs]

    @pl.kernel(mesh=plsc.ScalarSubcoreMesh(axis_name="seq", num_cores=1), name=name,
               scratch_types=(pltpu.SemaphoreType.DMA((7 * n,)), pltpu.SemaphoreType.DMA((7 * n,)),
                              pltpu.SemaphoreType.DMA((n,))),
               compiler_params=pltpu.CompilerParams(collective_id=collective_id))
    def launch(send_sems, recv_sems, local_sems):
        x, y, cc = lax.axis_index("x"), lax.axis_index("y"), lax.axis_index("c")
        me, sibling = (x, y, cc), (x, y, 1 - cc)
        chips = [(1 - x, y), (x, 1 - y), (1 - x, 1 - y)]
        barrier = pltpu.get_barrier_semaphore()
        for peer in [sibling] + [(*chip, cc) for chip in chips]:
            pl.semaphore_signal(barrier, inc=1, device_id=peer, device_id_type=MESH)
        pl.semaphore_wait(barrier, 4)

        def rows(a, px, py, pc):
            return out_refs[a].at[4 * px + 2 * py + pc]

        def copy(a, k, block, to, src=None):
            return pltpu.make_async_remote_copy(
                src_ref=rows(a, *block) if src is None else src, dst_ref=rows(a, *block),
                send_sem=send_sems.at[7 * a + k], recv_sem=recv_sems.at[7 * a + k], device_id=to, device_id_type=MESH)

        mine = [pltpu.make_async_copy(x_refs[a], rows(a, *me), local_sems.at[a]) for a in range(n)]
        for cp in mine:
            cp.start()
        first = []
        for a in range(n):
            first.append(copy(a, 0, me, sibling, src=x_refs[a]))
            first += [copy(a, 1 + j, me, (*chip, cc), src=x_refs[a]) for j, chip in enumerate(chips)]
        for cp in first:
            cp.start()
        passed = []
        for j, chip in enumerate(chips):
            for a in range(n):
                copy(a, 1 + j, (*chip, cc), me).wait_recv()
                fwd = copy(a, 4 + j, (*chip, cc), sibling)
                fwd.start()
                passed.append(fwd)
        for a in range(n):
            copy(a, 0, sibling, me).wait_recv()
        for j, chip in enumerate(chips):
            for a in range(n):
                copy(a, 4 + j, (*chip, 1 - cc), me).wait_recv()
        for cp in first + passed:
            cp.wait_send()
        for cp in mine:
            cp.wait()

    launch()
    return [r[...] for r in out_refs]


HBM = pl.BlockSpec(memory_space=pltpu.HBM)
SEM = pl.BlockSpec(memory_space=pltpu.SEMAPHORE)
EFFECT = pltpu.SideEffectType.DATAFLOW_SIDE_EFFECTING


def _sibling_plan(srcs, lands, send_sems, recv_sems):
    x, y, cc = lax.axis_index("x"), lax.axis_index("y"), lax.axis_index("c")
    return [pltpu.make_async_remote_copy(
        src_ref=srcs[a].at[2 * q + 1 - cc], dst_ref=lands[a].at[q], send_sem=send_sems.at[4 * a + q],
        recv_sem=recv_sems.at[4 * a + q], device_id=(x, y, 1 - cc), device_id_type=MESH)
        for a in range(len(srcs)) for q in range(4)]


def _chips_plan(srcs, lands, send_sems, recv_sems):
    x, y, cc = lax.axis_index("x"), lax.axis_index("y"), lax.axis_index("c")
    chips = [(1 - x, y), (x, 1 - y), (1 - x, 1 - y)]
    return [pltpu.make_async_remote_copy(
        src_ref=srcs[a].at[2 * px + py], dst_ref=lands[a].at[j], send_sem=send_sems.at[3 * a + j],
        recv_sem=recv_sems.at[3 * a + j], device_id=(px, py, cc), device_id_type=MESH)
        for a in range(len(srcs)) for j, (px, py) in enumerate(chips)]


def _copies_start(name, plan, per_array, srcs, land_lead):
    n = len(srcs)
    k = per_array * n

    def body(*refs):
        src_refs, land_refs = refs[:n], refs[n: 2 * n]
        send_sems, recv_sems = refs[2 * n], refs[2 * n + 1]
        token = refs[-1]
        for cp in plan(src_refs, land_refs, send_sems, recv_sems):
            cp.start()
        token[...] = jnp.zeros_like(token)

    lands = [lax.empty((land_lead,) + a.shape[1:], a.dtype) for a in srcs]
    outs = pl.pallas_call(
        body, name=name,
        out_shape=(pltpu.SemaphoreType.DMA((k,)), pltpu.SemaphoreType.DMA((k,)),
                   *[pltpu.HBM(a.shape, a.dtype) for a in srcs], *[pltpu.HBM(a.shape, a.dtype) for a in lands],
                   SDS((8, 128), F32)),
        in_specs=[HBM] * (2 * n), out_specs=(SEM, SEM, *[HBM] * (2 * n), pl.BlockSpec(memory_space=pltpu.VMEM)),
        input_output_aliases={i: 2 + i for i in range(2 * n)},
        compiler_params=pltpu.CompilerParams(has_side_effects=EFFECT),
    )(*[pltpu.with_memory_space_constraint(a, pltpu.HBM) for a in srcs],
      *[pltpu.with_memory_space_constraint(a, pltpu.HBM) for a in lands])
    return outs[0], outs[1], list(outs[2: 2 + n]), list(outs[2 + n: 2 + 2 * n]), outs[-1]


def _copies_wait(name, plan, started, after):
    send_sems, recv_sems, srcs, lands, _ = started
    n = len(srcs)
    after = tuple(after)

    def body(*refs):
        src_refs, land_refs = refs[:n], refs[n: 2 * n]
        for cp in plan(src_refs, land_refs, refs[2 * n], refs[2 * n + 1]):
            cp.wait_send()
            cp.wait_recv()

    outs = pl.pallas_call(
        body, name=name,
        out_shape=tuple(pltpu.HBM(a.shape, a.dtype) for a in srcs + lands),
        in_specs=[HBM] * (2 * n) + [SEM, SEM] + [ANY] * len(after), out_specs=(HBM,) * (2 * n),
        input_output_aliases={i: i for i in range(2 * n)},
        compiler_params=pltpu.CompilerParams(has_side_effects=EFFECT),
    )(*srcs, *lands, send_sems, recv_sems, *after)
    return list(outs[n:])


BIG = (("w_in", 1), ("w_ffn_in", 1), ("w_proj_ssm", 0), ("w_proj_gdn", 0), ("w_out", 0), ("w_ffn_down", 0))
CONVS = (("ssm_conv_w", 1), ("gdn_conv_w", 1))


def _to_dest_major(full, axis):
    a, b = full.shape
    if axis == 0:
        return full.reshape(N_DEV, a // N_DEV, b)
    return full.reshape(a, N_DEV, b // N_DEV).transpose(1, 0, 2)


def _from_gathered(g, axis):
    if axis == 0:
        return g.reshape(-1, g.shape[2])
    return g.transpose(1, 0, 2).reshape(g.shape[1], -1)


def _permute_in_cols(w):
    sm = jnp.concatenate([w[..., O_DT: O_DT + 16], w[..., O_A: O_A + 8], w[..., O_B: O_B + 8]], axis=-1)
    pad_sm = jnp.zeros(w.shape[:-1] + (128 - 32,), w.dtype)
    pad_end = jnp.zeros(w.shape[:-1] + (PROJ_W - SM_OFF - 128,), w.dtype)
    return jnp.concatenate([w[..., O_Z: O_Z + 1024], w[..., O_GZ: O_GZ + 1024], w[..., O_G1: O_G1 + 1024],
                            w[..., O_G2: O_G2 + 1024], w[..., O_QKV: O_QKV + 3072], w[..., O_XBC: O_XBC + 1536],
                            sm, pad_sm, pad_end], axis=-1)


def _unpermute_in_cols(w):
    return jnp.concatenate([w[..., Z_OFF: Z_OFF + 1024], w[..., XBC_OFF: XBC_OFF + 1536], w[..., SM_OFF: SM_OFF + 16],
                            w[..., QKV_OFF: QKV_OFF + 3072], w[..., GZ_OFF: GZ_OFF + 1024],
                            w[..., SM_OFF + LANE_A: SM_OFF + LANE_A + 8], w[..., SM_OFF + LANE_B: SM_OFF + LANE_B + 8],
                            w[..., G1_OFF: G1_OFF + 1024], w[..., G2_OFF: G2_OFF + 1024]], axis=-1)


def _pad128(v, lane0):
    return jnp.zeros((1, 128), F32).at[0, lane0: lane0 + v.shape[0]].set(v)


def _layer_consts(p):
    return dict(
        dtb=_pad128(p["ssm_dt_bias"], 0), arow=_pad128(-jnp.exp(p["ssm_a_log"]), 0),
        dxrow=jnp.repeat(p["ssm_d"], SSM_P).reshape(1, 1024), snw=p["ssm_norm_w"].reshape(1, 1024),
        gb=_pad128(p["gdn_dt_bias"], LANE_A), garow=_pad128(-jnp.exp(p["gdn_a_log"]), LANE_A),
        gnw=p["gdn_norm_w"].reshape(1, 128), zb=jnp.zeros((1, GDN_QKV), F32), scb=p["ssm_conv_b"].reshape(1, SSM_CONV))


def _expand_matrix():
    row = lax.broadcasted_iota(jnp.int32, (128, 1024), 0)
    col = lax.broadcasted_iota(jnp.int32, (128, 1024), 1)
    return (col // SSM_P == row).astype(BF16)


def _silu_mul_epi(acc, up):
    g = acc
    return g, g * _sigmoid(g) * up.astype(F32)


def _merge_epi(acc, p1, g1, g2):
    return acc, _sigmoid(g1) * p1.astype(F32) + _sigmoid(g2) * acc


def _add_epi(acc, res):
    return (acc + res,)


def _ffn_bwd_epi(acc, gate, up):
    g = gate.astype(F32)
    sg = _sigmoid(g)
    return acc * up.astype(F32) * (sg * (1.0 + g * (1.0 - sg))), acc * (g * sg)


def _merge_bwd_epi(acc, g1, g2, p1, p2):
    s1, s2 = _sigmoid(g1), _sigmoid(g2)
    return acc * s1, acc * s2, acc * p1.astype(F32) * (s1 * (1.0 - s1)), acc * p2.astype(F32) * (s2 * (1.0 - s2))


def _layer_fwd(l, x, p, rmat):
    t = x.shape[0]
    n = f"l{l}_"
    k = _layer_consts(p)
    h = _rmsnorm_fwd(n + "norm_mix", x, p["norm_mix_w"])
    proj = _matmul(n + "in_proj", "nn", [(h, 0, p["w_in"], 0)], t, PROJ_W, 1024, 1024, 1280, 1024, (F32,))
    act_g = _conv_fwd(n + "conv_gdn", proj, QKV_OFF, p["gdn_conv_w"], k["zb"])
    act_s = _conv_fwd(n + "conv_ssm", proj, XBC_OFF, p["ssm_conv_w"], k["scb"])
    y_ssm, ysc, st_s = _ssd_fwd(n + "ssd_fwd", act_s, proj, k["dtb"], k["arow"], k["dxrow"], k["snw"], rmat)
    y_gdn, oraw, st_g = _gdn_fwd(n + "gdn_fwd", act_g, proj, k["gb"], k["garow"], k["gnw"])
    if "late" in p:
        y_gdn, late = p["late"](y_gdn)
        p = {**p, **late}
    p1 = _matmul(n + "proj_ssm", "nn", [(y_ssm, 0, p["w_proj_ssm"], 0)], t, 1024, 1024, 1024, 1024, 1024, (BF16,))
    p2, merged = _matmul(n + "proj_gdn_merge", "nn", [(y_gdn, 0, p["w_proj_gdn"], 0)], t, 1024, 1024, 512, 1024, 1024,
                         (BF16, BF16), epi=_merge_epi, extras=[(p1, 0), (proj, G1_OFF // 1024), (proj, G2_OFF // 1024)])
    x1 = _matmul(n + "out_proj", "nn", [(merged, 0, p["w_out"], 0)], t, 1024, 1024, 1024, 1024, 1024, (F32,),
                 epi=_add_epi, extras=[(x, 0)])
    h2 = _rmsnorm_fwd(n + "norm_ffn", x1, p["norm_ffn_w"])
    up = _matmul(n + "ffn_up", "nn", [(h2, 0, p["w_ffn_in"], 2)], t, FFN, 1024, 512, FFN // 2, 1024, (BF16,))
    gate, act = _matmul(n + "ffn_gate", "nn", [(h2, 0, p["w_ffn_in"], 0)], t, FFN, 1024, 512, FFN // 2, 1024, (BF16, BF16),
                        epi=_silu_mul_epi, extras=[(up, 0)])
    x2 = _matmul(n + "ffn_down", "nn", [(act, 0, p["w_ffn_down"], 0)], t, 1024, FFN, 512, 1024, FFN, (F32,),
                 epi=_add_epi, extras=[(x1, 0)])
    saved = dict(x=x, h=h, proj=proj, act_g=act_g, act_s=act_s, y_ssm=y_ssm, ysc=ysc, st_s=st_s, y_gdn=y_gdn, oraw=oraw,
                 st_g=st_g, p1=p1, p2=p2, merged=merged, x1=x1, h2=h2, up=up, gate=gate, act=act, k=k, p=p)
    return x2, saved


def _layer_bwd(l, dx2, dx2b, s, p, rmat, hooks):
    t = dx2.shape[0]
    n = f"l{l}_"
    k = s["k"]
    tk_tok = 1024
    hf = FFN // 2
    g = {}
    dgate, dup = _matmul(n + "d_ffn_act", "nt", [(dx2b, 0, p["w_ffn_down"], 0)], t, FFN, 1024, 512, hf, 1024, (BF16, BF16),
                         epi=_ffn_bwd_epi, extras=[(s["gate"], 0), (s["up"], 0)])
    g["w_ffn_down"] = _matmul(n + "dw_ffn_down", "tn", [(s["act"], 0, dx2b, 0)], FFN, 1024, t, hf, 1024, tk_tok, (F32,))
    dh2 = _matmul(n + "d_ffn_in", "nt", [(dgate, 0, p["w_ffn_in"], 0), (dup, 0, p["w_ffn_in"], 2)], t, 1024, FFN,
                  512, 1024, hf, (F32,))
    dwg = _matmul(n + "dw_ffn_gate", "tn", [(s["h2"], 0, dgate, 0)], 1024, FFN, t, 1024, hf, tk_tok, (F32,))
    dwu = _matmul(n + "dw_ffn_up", "tn", [(s["h2"], 0, dup, 0)], 1024, FFN, t, 1024, hf, tk_tok, (F32,))
    g["w_ffn_in"] = jnp.concatenate([dwg, dwu], axis=1)
    dx1, dx1b, g["norm_ffn_w"] = _rmsnorm_bwd(n + "d_norm_ffn", s["x1"], p["norm_ffn_w"], dh2, dx2)
    dx1b = hooks.ffn_done(dx1b)
    dp1, dp2, dg1, dg2 = _matmul(
        n + "d_out_proj", "nt", [(dx1b, 0, p["w_out"], 0)], t, 1024, 1024, 512, 1024, 1024, (BF16,) * 4, epi=_merge_bwd_epi,
        extras=[(s["proj"], G1_OFF // 1024), (s["proj"], G2_OFF // 1024), (s["p1"], 0), (s["p2"], 0)])
    g["w_out"] = _matmul(n + "dw_out", "tn", [(s["merged"], 0, dx1b, 0)], 1024, 1024, t, 1024, 1024, tk_tok, (F32,))
    g["w_proj_ssm"] = _matmul(n + "dw_proj_ssm", "tn", [(s["y_ssm"], 0, dp1, 0)], 1024, 1024, t, 1024, 1024, tk_tok, (F32,))
    g["w_proj_gdn"] = _matmul(n + "dw_proj_gdn", "tn", [(s["y_gdn"], 0, dp2, 0)], 1024, 1024, t, 1024, 1024, tk_tok, (F32,))
    dp1, dp2 = hooks.early_ready(l, g, dp1, dp2)
    dy_ssm = _matmul(n + "d_proj_ssm", "nt", [(dp1, 0, p["w_proj_ssm"], 0)], t, 1024, 1024, 1024, 1024, 1024, (F32,))
    dy_gdn = _matmul(n + "d_proj_gdn", "nt", [(dp2, 0, p["w_proj_gdn"], 0)], t, 1024, 1024, 1024, 1024, 1024, (F32,))
    dact_s, dz, dsm_s, dsnw, dd, dal, ddtb = _ssd_bwd(n + "ssd_bwd", s["act_s"], s["proj"], k["dtb"], k["arow"], k["dxrow"],
                                                        k["snw"], rmat, s["ysc"], s["st_s"], dy_ssm)
    dact_g, dgz, dsm_g, dgnw, dgal, dgb = _gdn_bwd(n + "gdn_bwd", s["act_g"], s["proj"], k["gb"], k["garow"], k["gnw"],
                                                     s["oraw"], s["st_g"], dy_gdn)
    dsm_s = hooks.mixers_done(dsm_s)
    du_s, g["ssm_conv_w"], dcb = _conv_bwd(n + "d_conv_ssm", s["proj"], XBC_OFF, p["ssm_conv_w"], k["scb"], dact_s)
    du_g, g["gdn_conv_w"], _ = _conv_bwd(n + "d_conv_gdn", s["proj"], QKV_OFF, p["gdn_conv_w"], k["zb"], dact_g)
    g["ssm_conv_b"] = dcb.reshape(-1)
    g["ssm_norm_w"] = dsnw.reshape(-1)
    g["ssm_d"] = dd[0, :SSM_HEADS]
    g["ssm_a_log"] = dal[0, :SSM_HEADS]
    g["ssm_dt_bias"] = ddtb[0, :SSM_HEADS]
    g["gdn_norm_w"] = dgnw.reshape(-1)
    g["gdn_a_log"] = dgal[0, LANE_A: LANE_A + GDN_HEADS]
    g["gdn_dt_bias"] = dgb[0, LANE_A: LANE_A + GDN_HEADS]
    dproj = jnp.concatenate([dz, dgz, dg1, dg2, du_g, du_s, dsm_s + dsm_g, jnp.zeros((t, PROJ_W - SM_OFF - 128), BF16)],
                            axis=1)
    dh = _matmul(n + "d_in_proj", "nt", [(dproj, 0, p["w_in"], 0)], t, 1024, PROJ_W, 1024, 1024, 1280, (F32,))
    g["w_in"] = _matmul(n + "dw_in", "tn", [(s["h"], 0, dproj, 0)], 1024, PROJ_W, t, 1024, 1280, tk_tok, (F32,))
    dx, dxb, g["norm_mix_w"] = _rmsnorm_bwd(n + "d_norm_mix", s["x"], p["norm_mix_w"], dh, dx1)
    g["norm_mix_w"] = g["norm_mix_w"].reshape(-1)
    g["norm_ffn_w"] = g["norm_ffn_w"].reshape(-1)
    return dx, dxb, g


def _local_step(x, tgt, layers, final_norm_w, reduce=False):
    rmat = _expand_matrix()
    saved, params = [], []
    for l in range(DEPTH):
        x, p = layers[l](x)
        x, s = _layer_fwd(l, x, p, rmat)
        saved.append(s)
        params.append(s["p"])
    loss, dx, dxb, dfw = _loss_head("loss_head", x, final_norm_w, tgt)
    grads = [None] * DEPTH
    hooks = _ReduceBesideBackward() if reduce else _NoReduce()
    for l in reversed(range(DEPTH)):
        dx, dxb, grads[l] = _layer_bwd(l, dx, dxb, saved[l], params[l], rmat, hooks)
        if reduce:
            dxb = hooks.layer_done(l, grads[l], dxb)
    if reduce:
        hooks.finish(dxb)
    return loss[0, 0], dx, grads, dfw.reshape(-1), hooks.shards if reduce else None


SMALL = ("norm_mix_w", "ssm_conv_b", "ssm_dt_bias", "ssm_a_log", "ssm_d", "ssm_norm_w", "gdn_a_log", "gdn_dt_bias",
         "gdn_norm_w", "norm_ffn_w")
WEIGHTS = ("norm_mix_w", "w_in", "ssm_conv_w", "ssm_conv_b", "ssm_dt_bias", "ssm_a_log", "ssm_d", "ssm_norm_w", "gdn_conv_w",
           "gdn_a_log", "gdn_dt_bias", "gdn_norm_w", "w_proj_ssm", "w_proj_gdn", "w_out", "norm_ffn_w", "w_ffn_in",
           "w_ffn_down", "final_norm_w")


FIRST_USED = ("w_in", "ssm_conv_w", "gdn_conv_w")


def _gather_layer(l, w):
    conv_names = [nm for nm, _ in CONVS]
    groups = ([s for s in BIG + CONVS if s[0] in FIRST_USED], [s for s in BIG + CONVS if s[0] not in FIRST_USED])
    gathered = []
    for i, (specs, tag) in enumerate(zip(groups, ("first", "rest"))):
        shards = [w[nm][l] if nm in conv_names else w[nm][l].astype(BF16) for nm, _ in specs]
        gathered.append(_allgather_seq(f"l{l}_gather_{tag}", shards, collective_id=2 * l + i))
    small = {nm: w[nm][l] for nm in SMALL}

    def use(i, act):
        act, blocks = lax.optimization_barrier((act, gathered[i]))
        return act, {nm: _from_gathered(g, axis) for (nm, axis), g in zip(groups[i], blocks)}

    def full_weights(x):
        x, out = use(0, x)
        out["w_in"] = _permute_in_cols(out["w_in"])
        out.update(small)
        out["late"] = lambda y: use(1, y)
        return x, out

    return full_weights


EARLY_GRADS = ("w_ffn_down", "w_ffn_in", "w_out", "w_proj_ssm", "w_proj_gdn")


class _GradReduceScatter:
    def __init__(self, tag, specs, grads):
        self.tag = tag
        self.specs = specs
        self.blocks = [_to_dest_major(grads[nm], axis) for nm, axis in specs]

    def _tied(self, started, acts):
        *acts, self.token = lax.optimization_barrier((*acts, started[4]))
        return acts

    def start(self, *acts):
        cc = lax.axis_index("c")
        self.keep = [lax.dynamic_index_in_dim(b.reshape((4, 2) + b.shape[1:]), cc, axis=1, keepdims=False)
                     for b in self.blocks]
        self.to_sibling = _copies_start(f"{self.tag}_to_sibling_start", _sibling_plan, 4, self.blocks, 4)
        return self._tied(self.to_sibling, acts)

    def mid(self, *acts):
        got = _copies_wait(f"{self.tag}_to_sibling_wait", _sibling_plan, self.to_sibling, (acts[0], self.token))
        chip_sums = [_sum_terms(f"{self.tag}_chip_sum_{nm}", [(k[None], 0), (g[None], 0)], BF16)
                     for (nm, _), k, g in zip(self.specs, self.keep, got)]
        self.to_chips = _copies_start(f"{self.tag}_between_chips_start", _chips_plan, 3, chip_sums, 3)
        return self._tied(self.to_chips, acts)

    def end(self, after):
        landed = _copies_wait(f"{self.tag}_between_chips_wait", _chips_plan, self.to_chips, (after, self.token))
        my_chip = 2 * lax.axis_index("x") + lax.axis_index("y")
        own = [lax.dynamic_index_in_dim(s, my_chip, axis=0, keepdims=True) for s in self.to_chips[2]]
        return {nm: _sum_terms(f"{self.tag}_total_{nm}", [(o, 0), (e, 0), (e, 1), (e, 2)], F32)
                for (nm, _), o, e in zip(self.specs, own, landed)}


class _NoReduce:
    def ffn_done(self, dx1b):
        return dx1b

    def early_ready(self, l, g, dp1, dp2):
        return dp1, dp2

    def mixers_done(self, dsm):
        return dsm


class _ReduceBesideBackward(_NoReduce):
    def __init__(self):
        self.late = None
        self.early = None
        self.shards = [dict() for _ in range(DEPTH)]

    def ffn_done(self, dx1b):
        if self.late is not None:
            (dx1b,) = self.late.mid(dx1b)
        return dx1b

    def early_ready(self, l, g, dp1, dp2):
        self.early = _GradReduceScatter(f"l{l}_early_grads", [s for s in BIG if s[0] in EARLY_GRADS], g)
        return self.early.start(dp1, dp2)

    def mixers_done(self, dsm):
        (dsm,) = self.early.mid(dsm)
        return dsm

    def layer_done(self, l, g, dxb):
        if self.late is not None:
            self.shards[l + 1].update(self.late.end(dxb))
        self.shards[l].update(self.early.end(dxb))
        g = dict(g)
        g["w_in"] = _unpermute_in_cols(g["w_in"])
        self.late = _GradReduceScatter(f"l{l}_late_grads", [s for s in BIG + CONVS if s[0] not in EARLY_GRADS], g)
        (dxb,) = self.late.start(dxb)
        return dxb

    def finish(self, dxb):
        (dxb,) = self.late.mid(dxb)
        self.shards[0].update(self.late.end(dxb))
        return dxb


def _allreduce_small(vecs):
    flat = jnp.concatenate(vecs)
    n = flat.shape[0]
    rows = -(-n // 128)
    rows = -(-rows // 8) * 8
    buf = jnp.pad(flat, (0, rows * 128 - n)).reshape(rows, 128)
    (allv,) = _allgather("gather_small_grads", [buf])
    tot = _sum_terms("small_grads_total", [(allv, d) for d in range(N_DEV)], F32).reshape(-1)
    out, o = [], 0
    for v in vecs:
        out.append(tot[o: o + v.shape[0]])
        o += v.shape[0]
    return out


def kernel(x, norm_mix_w, w_in, ssm_conv_w, ssm_conv_b, ssm_dt_bias, ssm_a_log, ssm_d, ssm_norm_w, gdn_conv_w, gdn_a_log, gdn_dt_bias, gdn_norm_w, w_proj_ssm, w_proj_gdn, w_out, norm_ffn_w, w_ffn_in, w_ffn_down, final_norm_w, loss_target, m_norm_mix_w, m_w_in, m_ssm_conv_w, m_ssm_conv_b, m_ssm_dt_bias, m_ssm_a_log, m_ssm_d, m_ssm_norm_w, m_gdn_conv_w, m_gdn_a_log, m_gdn_dt_bias, m_gdn_norm_w, m_w_proj_ssm, m_w_proj_gdn, m_w_out, m_norm_ffn_w, m_w_ffn_in, m_w_ffn_down, m_final_norm_w, v_norm_mix_w, v_w_in, v_ssm_conv_w, v_ssm_conv_b, v_ssm_dt_bias, v_ssm_a_log, v_ssm_d, v_ssm_norm_w, v_gdn_conv_w, v_gdn_a_log, v_gdn_dt_bias, v_gdn_norm_w, v_w_proj_ssm, v_w_proj_gdn, v_w_out, v_norm_ffn_w, v_w_ffn_in, v_w_ffn_down, v_final_norm_w):
    w = dict(norm_mix_w=norm_mix_w, w_in=w_in, ssm_conv_w=ssm_conv_w, ssm_conv_b=ssm_conv_b, ssm_dt_bias=ssm_dt_bias,
             ssm_a_log=ssm_a_log, ssm_d=ssm_d, ssm_norm_w=ssm_norm_w, gdn_conv_w=gdn_conv_w, gdn_a_log=gdn_a_log,
             gdn_dt_bias=gdn_dt_bias, gdn_norm_w=gdn_norm_w, w_proj_ssm=w_proj_ssm, w_proj_gdn=w_proj_gdn, w_out=w_out,
             norm_ffn_w=norm_ffn_w, w_ffn_in=w_ffn_in, w_ffn_down=w_ffn_down, final_norm_w=final_norm_w)
    m = dict(norm_mix_w=m_norm_mix_w, w_in=m_w_in, ssm_conv_w=m_ssm_conv_w, ssm_conv_b=m_ssm_conv_b, ssm_dt_bias=m_ssm_dt_bias,
             ssm_a_log=m_ssm_a_log, ssm_d=m_ssm_d, ssm_norm_w=m_ssm_norm_w, gdn_conv_w=m_gdn_conv_w, gdn_a_log=m_gdn_a_log,
             gdn_dt_bias=m_gdn_dt_bias, gdn_norm_w=m_gdn_norm_w, w_proj_ssm=m_w_proj_ssm, w_proj_gdn=m_w_proj_gdn,
             w_out=m_w_out, norm_ffn_w=m_norm_ffn_w, w_ffn_in=m_w_ffn_in, w_ffn_down=m_w_ffn_down,
             final_norm_w=m_final_norm_w)
    v = dict(norm_mix_w=v_norm_mix_w, w_in=v_w_in, ssm_conv_w=v_ssm_conv_w, ssm_conv_b=v_ssm_conv_b, ssm_dt_bias=v_ssm_dt_bias,
             ssm_a_log=v_ssm_a_log, ssm_d=v_ssm_d, ssm_norm_w=v_ssm_norm_w, gdn_conv_w=v_gdn_conv_w, gdn_a_log=v_gdn_a_log,
             gdn_dt_bias=v_gdn_dt_bias, gdn_norm_w=v_gdn_norm_w, w_proj_ssm=v_w_proj_ssm, w_proj_gdn=v_w_proj_gdn,
             w_out=v_w_out, norm_ffn_w=v_norm_ffn_w, w_ffn_in=v_w_ffn_in, w_ffn_down=v_w_ffn_down,
             final_norm_w=v_final_norm_w)

    layers = [_gather_layer(l, w) for l in range(DEPTH)]
    loss_part, dx, lgrads, dfw, shard_grads = _local_step(x[0], loss_target[0], layers, final_norm_w, reduce=True)
    loss = lax.psum(loss_part, ("x", "y", "c"))
    grad = {nm: [shard_grads[l][nm] for l in range(DEPTH)] for nm, _ in BIG + CONVS}
    small_vecs = [lgrads[l][nm].reshape(-1) for l in range(DEPTH) for nm in SMALL] + [dfw]
    small_sum = _allreduce_small(small_vecs)
    for i, nm in enumerate(SMALL):
        grad[nm] = jnp.stack([small_sum[l * len(SMALL) + i].reshape(w[nm].shape[1:]) for l in range(DEPTH)])
    grad["final_norm_w"] = small_sum[-1]

    deltas, new_m, new_v = {}, {}, {}
    for nm in WEIGHTS:
        grad[nm], deltas[nm], new_m[nm], new_v[nm] = _adamw("adamw_" + nm, w[nm], grad[nm], m[nm], v[nm])
    return (loss, dx[None], *[grad[nm] for nm in WEIGHTS], *[deltas[nm] for nm in WEIGHTS],
            *[new_m[nm] for nm in WEIGHTS], *[new_v[nm] for nm in WEIGHTS])
```

```python
import functools

import jax
import jax.numpy as jnp
from jax import lax
from jax.experimental import pallas as pl
from jax.experimental.pallas import tpu as pltpu
from jax.experimental.pallas import tpu_sc as plsc

F32 = jnp.float32
BF16 = jnp.bfloat16
HI = lax.Precision.HIGHEST
SDS = jax.ShapeDtypeStruct

D_MODEL = 1024
DEPTH = 2
SSM_HEADS = 16
SSM_P = 64
SSM_N = 128
SSM_GROUPS = 2
SSM_CONV = 1536
GDN_HEADS = 8
GDN_DK = 128
GDN_QKV = 3072
CONV_K = 4
CHUNK = 64
FFN = 2816
IN_DIM = 8736
EPS = 1e-6
N_DEV = 8

Z_OFF = 0
GZ_OFF = 1024
G1_OFF = 2048
G2_OFF = 3072
QKV_OFF = 4096
XBC_OFF = 7168
SM_OFF = 8704
PROJ_W = 8960
LANE_A = 16
LANE_B = 24
O_Z, O_XBC, O_DT, O_QKV, O_GZ, O_A, O_B, O_G1, O_G2 = 0, 1024, 2560, 2576, 5648, 6672, 6680, 6688, 7712

ADAM_LR = 0.001
ADAM_B1 = 0.9
ADAM_B2 = 0.999
ADAM_EPS = 1e-08
ADAM_WD = 0.01
ADAM_STEP = 10

V7X_VMEM_LIMIT = 48 * 1024 * 1024

NN = ((1,), (0,))
NT = ((1,), (1,))
TN = ((0,), (0,))


def _bdot(a, b, dims):
    return lax.dot_general(a.astype(BF16), b.astype(BF16), (dims, ((), ())), preferred_element_type=F32)


def _hdot(a, b, dims=NN):
    return lax.dot_general(a, b, (dims, ((), ())), precision=HI, preferred_element_type=F32)


def _sigmoid(x):
    return 1.0 / (1.0 + jnp.exp(-x))


def _softplus(x):
    return jnp.maximum(x, 0.0) + jnp.log(1.0 + jnp.exp(-jnp.abs(x)))


def _params(dims):
    return pltpu.CompilerParams(dimension_semantics=dims, vmem_limit_bytes=V7X_VMEM_LIMIT)


def _rowsum(x):
    return jnp.sum(x, axis=-1, keepdims=True)


def _colsum(x):
    return jnp.sum(x, axis=0, keepdims=True)


def _matmul(name, mode, pairs, m, n, kdim, tm, tn, tk, out_dtypes, epi=None, extras=()):
    tm, tn, tk = min(tm, m), min(tn, n), min(tk, kdim)
    nk = kdim // tk
    assert m % tm == 0 and n % tn == 0 and kdim % tk == 0, (name, m, n, kdim, tm, tn, tk)
    in_specs, args = [], []
    for a, a_off, b, b_off in pairs:
        if mode == "nn":
            in_specs.append(pl.BlockSpec((tm, tk), lambda i, j, k, o=a_off: (i, k + o)))
            in_specs.append(pl.BlockSpec((tk, tn), lambda i, j, k, o=b_off: (k, j + o)))
            dims = NN
        elif mode == "nt":
            in_specs.append(pl.BlockSpec((tm, tk), lambda i, j, k, o=a_off: (i, k + o)))
            in_specs.append(pl.BlockSpec((tn, tk), lambda i, j, k, o=b_off: (j, k + o)))
            dims = NT
        else:
            in_specs.append(pl.BlockSpec((tk, tm), lambda i, j, k, o=a_off: (k, i + o)))
            in_specs.append(pl.BlockSpec((tk, tn), lambda i, j, k, o=b_off: (k, j + o)))
            dims = TN
        args += [a, b]
    for e, e_off in extras:
        in_specs.append(pl.BlockSpec((tm, tn), lambda i, j, k, o=e_off: (i, j + o)))
        args.append(e)
    npair, nex, nout = len(pairs), len(extras), len(out_dtypes)

    def body(*refs):
        prefs = refs[: 2 * npair]
        erefs = refs[2 * npair: 2 * npair + nex]
        orefs = refs[2 * npair + nex: 2 * npair + nex + nout]

        def finish(res):
            outs = (res,) if epi is None else epi(res, *[e[...] for e in erefs])
            for o, r in zip(orefs, outs):
                o[...] = r.astype(o.dtype)

        s = _bdot(prefs[0][...], prefs[1][...], dims)
        for p in range(1, npair):
            s = s + _bdot(prefs[2 * p][...], prefs[2 * p + 1][...], dims)
        if nk == 1:
            finish(s)
            return
        acc = refs[-1]
        k = pl.program_id(2)

        @pl.when(k == 0)
        def _():
            acc[...] = s

        @pl.when(k > 0)
        def _():
            acc[...] += s

        @pl.when(k == nk - 1)
        def _():
            finish(acc[...])

    out_shape = tuple(SDS((m, n), dt) for dt in out_dtypes)
    out_specs = tuple(pl.BlockSpec((tm, tn), lambda i, j, k: (i, j)) for _ in out_dtypes)
    res = pl.pallas_call(
        body, grid=(m // tm, n // tn, nk), in_specs=in_specs, out_specs=out_specs, out_shape=out_shape,
        scratch_shapes=[pltpu.VMEM((tm, tn), F32)] if nk > 1 else [], name=name,
        compiler_params=_params(("parallel", "parallel", "arbitrary")),
    )(*args)
    return res if nout > 1 else res[0]


def _rmsnorm_fwd(name, x, w):
    t, d = x.shape
    tm = min(512, t)

    def body(x_ref, w_ref, h_ref):
        xv = x_ref[...]
        r = lax.rsqrt(jnp.mean(xv * xv, axis=-1, keepdims=True) + EPS)
        h_ref[...] = (xv * r * w_ref[...]).astype(BF16)

    return pl.pallas_call(
        body, grid=(t // tm,),
        in_specs=[pl.BlockSpec((tm, d), lambda i: (i, 0)), pl.BlockSpec((1, d), lambda i: (0, 0))],
        out_specs=pl.BlockSpec((tm, d), lambda i: (i, 0)), out_shape=SDS((t, d), BF16), name=name,
        compiler_params=_params(("parallel",)),
    )(x, w.reshape(1, d))


def _rmsnorm_bwd(name, x, w, dh, dres):
    t, d = x.shape
    tm = min(512, t)

    def body(x_ref, w_ref, dh_ref, dres_ref, dx_ref, dxb_ref, dw_ref):
        xv = x_ref[...]
        r = lax.rsqrt(jnp.mean(xv * xv, axis=-1, keepdims=True) + EPS)
        xh = xv * r
        dhv = dh_ref[...].astype(F32)
        dxh = dhv * w_ref[...]
        dx = r * (dxh - xh * jnp.mean(dxh * xh, axis=-1, keepdims=True)) + dres_ref[...]
        dx_ref[...] = dx
        dxb_ref[...] = dx.astype(BF16)

        @pl.when(pl.program_id(0) == 0)
        def _():
            dw_ref[...] = jnp.zeros_like(dw_ref)

        dw_ref[...] += _colsum(dhv * xh)

    row = pl.BlockSpec((tm, d), lambda i: (i, 0))
    vec = pl.BlockSpec((1, d), lambda i: (0, 0))
    return pl.pallas_call(
        body, grid=(t // tm,), in_specs=[row, vec, row, row], out_specs=(row, row, vec),
        out_shape=(SDS((t, d), F32), SDS((t, d), BF16), SDS((1, d), F32)), name=name,
        compiler_params=_params(("arbitrary",)),
    )(x, w.reshape(1, d), dh, dres)


def _loss_head(name, x, w, tgt):
    t, d = x.shape
    tm = min(512, t)

    def body(x_ref, w_ref, t_ref, loss_ref, dx_ref, dxb_ref, dw_ref):
        xv = x_ref[...]
        wv = w_ref[...]
        r = lax.rsqrt(jnp.mean(xv * xv, axis=-1, keepdims=True) + EPS)
        xh = xv * r
        e = xh * wv - t_ref[...]
        dy = e * (1.0 / d)
        dxh = dy * wv
        dx = r * (dxh - xh * jnp.mean(dxh * xh, axis=-1, keepdims=True))
        dx_ref[...] = dx
        dxb_ref[...] = dx.astype(BF16)

        @pl.when(pl.program_id(0) == 0)
        def _():
            dw_ref[...] = jnp.zeros_like(dw_ref)
            loss_ref[...] = jnp.zeros_like(loss_ref)

        dw_ref[...] += _colsum(dy * xh)
        loss_ref[...] += 0.5 * jnp.sum(jnp.mean(e * e, axis=-1, keepdims=True), axis=0, keepdims=True)

    row = pl.BlockSpec((tm, d), lambda i: (i, 0))
    vec = pl.BlockSpec((1, d), lambda i: (0, 0))
    return pl.pallas_call(
        body, grid=(t // tm,), in_specs=[row, vec, row],
        out_specs=(pl.BlockSpec((1, 1), lambda i: (0, 0)), row, row, vec),
        out_shape=(SDS((1, 1), F32), SDS((t, d), F32), SDS((t, d), BF16), SDS((1, d), F32)), name=name,
        compiler_params=_params(("arbitrary",)),
    )(x, w.reshape(1, d), tgt)


def _shift_down(u, s, row):
    return jnp.where(row >= s, pltpu.roll(u, shift=s, axis=0), 0.0)


def _conv_fwd(name, src, col0, w, b):
    t = src.shape[0]
    c = w.shape[1]
    tc = 256
    assert c % tc == 0 and col0 % tc == 0

    def body(u_ref, w_ref, b_ref, o_ref):
        u = u_ref[...]
        wv = w_ref[...]
        row = lax.broadcasted_iota(jnp.int32, u.shape, 0)
        pre = b_ref[...] + wv[3:4, :] * u
        for s in range(1, CONV_K):
            pre = pre + wv[3 - s: 4 - s, :] * _shift_down(u, s, row)
        o_ref[...] = pre * _sigmoid(pre)

    return pl.pallas_call(
        body, grid=(c // tc,),
        in_specs=[pl.BlockSpec((t, tc), lambda j: (0, j + col0 // tc)), pl.BlockSpec((CONV_K, tc), lambda j: (0, j)),
                  pl.BlockSpec((1, tc), lambda j: (0, j))],
        out_specs=pl.BlockSpec((t, tc), lambda j: (0, j)), out_shape=SDS((t, c), F32), name=name,
        compiler_params=_params(("parallel",)),
    )(src, w, b)


def _conv_bwd(name, src, col0, w, b, dact):
    t = src.shape[0]
    c = w.shape[1]
    tc = 128

    def body(u_ref, w_ref, b_ref, da_ref, du_ref, dw_ref, db_ref):
        u = u_ref[...]
        wv = w_ref[...]
        row = lax.broadcasted_iota(jnp.int32, u.shape, 0)
        shifted = [u] + [_shift_down(u, s, row) for s in range(1, CONV_K)]
        pre = b_ref[...] + wv[3:4, :] * u
        for s in range(1, CONV_K):
            pre = pre + wv[3 - s: 4 - s, :] * shifted[s]
        sg = _sigmoid(pre)
        dpre = da_ref[...] * (sg * (1.0 + pre * (1.0 - sg)))
        du = wv[3:4, :] * dpre
        for s in range(1, CONV_K):
            du = du + wv[3 - s: 4 - s, :] * jnp.where(row < t - s, pltpu.roll(dpre, shift=t - s, axis=0), 0.0)
        du_ref[...] = du.astype(BF16)
        for s in range(CONV_K):
            dw_ref[3 - s: 4 - s, :] = _colsum(dpre * shifted[s])
        db_ref[...] = _colsum(dpre)

    return pl.pallas_call(
        body, grid=(c // tc,),
        in_specs=[pl.BlockSpec((t, tc), lambda j: (0, j + col0 // tc)), pl.BlockSpec((CONV_K, tc), lambda j: (0, j)),
                  pl.BlockSpec((1, tc), lambda j: (0, j)), pl.BlockSpec((t, tc), lambda j: (0, j))],
        out_specs=(pl.BlockSpec((t, tc), lambda j: (0, j)), pl.BlockSpec((CONV_K, tc), lambda j: (0, j)),
                   pl.BlockSpec((1, tc), lambda j: (0, j))),
        out_shape=(SDS((t, c), BF16), SDS((CONV_K, c), F32), SDS((1, c), F32)), name=name,
        compiler_params=_params(("parallel",)),
    )(src, w, b, dact)


def _tri(q):
    ii = lax.broadcasted_iota(jnp.int32, (q, q), 0)
    jj = lax.broadcasted_iota(jnp.int32, (q, q), 1)
    return ii, jj


def _dot01(x, r01, dims, terms=3):
    out, rem = None, x
    for i in range(terms):
        hi = rem.astype(BF16)
        d = lax.dot_general(hi, r01, (dims, ((), ())), preferred_element_type=F32)
        out = d if out is None else out + d
        if i + 1 < terms:
            rem = rem - hi.astype(F32)
    return out


def _ssd_common(act, sm, dtb, arow, rmat):
    q = CHUNK
    ii, jj = _tri(q)
    lane = lax.broadcasted_iota(jnp.int32, (q, 128), 1)
    m16 = lane < SSM_HEADS
    dt = jnp.where(m16, _softplus(sm + dtb), 0.0)
    a = dt * arow
    tril = (ii >= jj).astype(F32)
    triu = (ii <= jj).astype(F32)
    acum = _hdot(tril, a)
    acum_r = _hdot(a.T, triu)
    dtx = _dot01(dt, rmat, NN)
    acx = _dot01(acum, rmat, NN)
    ex = jnp.exp(acx)
    alx = acx[q - 1: q, :]
    dex = jnp.exp(alx - acx)
    xs = act[:, :1024]
    return dict(ii=ii, jj=jj, m16=m16, dt=dt, a=a, triu=triu, acum=acum, acum_r=acum_r, dtx=dtx, ex=ex, dex=dex,
                elx=jnp.exp(alx), xs=xs, x=xs * dtx)


def _ssd_lmat(cm, h):
    return jnp.where(cm["ii"] >= cm["jj"], jnp.exp(cm["acum"][:, h: h + 1] - cm["acum_r"][h: h + 1, :]), 0.0)


def _ssd_fwd(name, act, proj, dtb, arow, dxrow, nw, rmat):
    t = act.shape[0]
    q = CHUNK
    nc = t // q
    hg = SSM_HEADS // SSM_GROUPS
    gw = hg * SSM_P

    def body(act_ref, z_ref, sm_ref, dtb_ref, arow_ref, dx_ref, nw_ref, r_ref, y_ref, ys_ref, st_ref, s_scr, yd_scr):
        @pl.when(pl.program_id(0) == 0)
        def _():
            s_scr[...] = jnp.zeros_like(s_scr)

        s_all = s_scr[...]
        st_ref[0] = s_all
        actv = act_ref[...]
        cm = _ssd_common(actv, sm_ref[...], dtb_ref[...], arow_ref[...], r_ref[...])
        x = cm["x"]
        xd = x * cm["dex"]
        yoffs, snew = [], []
        for g in range(SSM_GROUPS):
            bg = actv[:, 1024 + g * SSM_N: 1024 + (g + 1) * SSM_N]
            cg = actv[:, 1280 + g * SSM_N: 1280 + (g + 1) * SSM_N]
            sg = s_all[:, g * gw: (g + 1) * gw]
            cb = _bdot(cg, bg, NT)
            yoffs.append(_bdot(cg, sg, NN))
            snew.append(_bdot(bg, xd[:, g * gw: (g + 1) * gw], TN))
            for r in range(hg):
                h = g * hg + r
                mm = cb * _ssd_lmat(cm, h)
                yd_scr[:, h * SSM_P: (h + 1) * SSM_P] = _bdot(mm, x[:, h * SSM_P: (h + 1) * SSM_P], NN)
        s_scr[...] = s_all * cm["elx"] + jnp.concatenate(snew, axis=1)
        ysc = yd_scr[...] + jnp.concatenate(yoffs, axis=1) * cm["ex"]
        ys_ref[...] = ysc
        zv = z_ref[...]
        yg = (ysc + dx_ref[...] * cm["xs"]) * (zv * _sigmoid(zv))
        nwv = nw_ref[...]
        for g in range(SSM_GROUPS):
            sl = yg[:, g * gw: (g + 1) * gw]
            rr = lax.rsqrt(jnp.mean(sl * sl, axis=-1, keepdims=True) + EPS)
            y_ref[:, g * gw: (g + 1) * gw] = (sl * rr * nwv[:, g * gw: (g + 1) * gw]).astype(BF16)

    vec128 = pl.BlockSpec((1, 128), lambda c: (0, 0))
    vec1k = pl.BlockSpec((1, 1024), lambda c: (0, 0))
    return pl.pallas_call(
        body, grid=(nc,),
        in_specs=[pl.BlockSpec((q, SSM_CONV), lambda c: (c, 0)), pl.BlockSpec((q, 1024), lambda c: (c, Z_OFF // 1024)),
                  pl.BlockSpec((q, 128), lambda c: (c, SM_OFF // 128)), vec128, vec128, vec1k, vec1k,
                  pl.BlockSpec((128, 1024), lambda c: (0, 0))],
        out_specs=(pl.BlockSpec((q, 1024), lambda c: (c, 0)), pl.BlockSpec((q, 1024), lambda c: (c, 0)),
                   pl.BlockSpec((1, 128, 1024), lambda c: (c, 0, 0))),
        out_shape=(SDS((t, 1024), BF16), SDS((t, 1024), F32), SDS((nc, 128, 1024), F32)),
        scratch_shapes=[pltpu.VMEM((128, 1024), F32), pltpu.VMEM((q, 1024), F32)], name=name,
        compiler_params=_params(("arbitrary",)),
    )(act, proj, proj, dtb, arow, dxrow, nw, rmat)


def _ssd_bwd(name, act, proj, dtb, arow, dxrow, nw, rmat, ysc, states, dy):
    t = act.shape[0]
    q = CHUNK
    nc = t // q
    hg = SSM_HEADS // SSM_GROUPS
    gw = hg * SSM_P

    def body(act_ref, z_ref, sm_ref, dtb_ref, arow_ref, dx_ref, nw_ref, r_ref, ys_ref, st_ref, dy_ref,
             dact_ref, dz_ref, dsm_ref, dnw_ref, dd_ref, dal_ref, ddtb_ref, ds_scr, dxd_scr):
        @pl.when(pl.program_id(0) == 0)
        def _():
            ds_scr[...] = jnp.zeros_like(ds_scr)
            dnw_ref[...] = jnp.zeros_like(dnw_ref)
            dd_ref[...] = jnp.zeros_like(dd_ref)
            dal_ref[...] = jnp.zeros_like(dal_ref)
            ddtb_ref[...] = jnp.zeros_like(ddtb_ref)

        actv = act_ref[...]
        smv = sm_ref[...]
        rmat_v = r_ref[...]
        cm = _ssd_common(actv, smv, dtb_ref[...], arow_ref[...], rmat_v)
        ii, jj = cm["ii"], cm["jj"]
        x, xs = cm["x"], cm["xs"]
        s_all = st_ref[0]
        dsn = ds_scr[...]
        ysv = ys_ref[...]
        dxr = dx_ref[...]
        y = ysv + dxr * xs
        zv = z_ref[...]
        sz = _sigmoid(zv)
        silz = zv * sz
        yg = y * silz
        dout = dy_ref[...]
        nwv = nw_ref[...]
        dyn = dout * nwv
        yn_parts, dyg_parts = [], []
        for g in range(SSM_GROUPS):
            sl = yg[:, g * gw: (g + 1) * gw]
            rr = lax.rsqrt(jnp.mean(sl * sl, axis=-1, keepdims=True) + EPS)
            yn = sl * rr
            dn = dyn[:, g * gw: (g + 1) * gw]
            yn_parts.append(yn)
            dyg_parts.append(rr * (dn - yn * jnp.mean(dn * yn, axis=-1, keepdims=True)))
        dnw_ref[...] += _colsum(dout * jnp.concatenate(yn_parts, axis=1))
        dyg = jnp.concatenate(dyg_parts, axis=1)
        dyv = dyg * silz
        dz_ref[...] = (dyg * y * (sz * (1.0 + zv * (1.0 - sz)))).astype(BF16)
        dd_ref[...] += _dot01(_colsum(dyv * xs), rmat_v, NT)
        dxs = dyv * dxr
        dcs = dyv * cm["ex"]
        xd = x * cm["dex"]
        dxst_parts, ds_parts, db_parts, dc_parts, yoff_parts, wcol_rows = [], [], [], [], [], []
        lane128 = lax.broadcasted_iota(jnp.int32, (q, 128), 1)
        wrow = jnp.zeros((q, 128), F32)
        for g in range(SSM_GROUPS):
            bg = actv[:, 1024 + g * SSM_N: 1024 + (g + 1) * SSM_N]
            cg = actv[:, 1280 + g * SSM_N: 1280 + (g + 1) * SSM_N]
            sg = s_all[:, g * gw: (g + 1) * gw]
            dsng = dsn[:, g * gw: (g + 1) * gw]
            dcsg = dcs[:, g * gw: (g + 1) * gw]
            dcg = _bdot(dcsg, sg, NT)
            yoff_parts.append(_bdot(cg, sg, NN))
            ds_parts.append(_bdot(cg, dcsg, TN))
            dxst_parts.append(_bdot(bg, dsng, NN))
            dbg = _bdot(xd[:, g * gw: (g + 1) * gw], dsng, NT)
            cb = _bdot(cg, bg, NT)
            dcb = jnp.zeros((q, q), F32)
            for r in range(hg):
                h = g * hg + r
                lm = _ssd_lmat(cm, h)
                mm = cb * lm
                dyh = dyv[:, h * SSM_P: (h + 1) * SSM_P]
                dm = jnp.where(ii >= jj, _bdot(dyh, x[:, h * SSM_P: (h + 1) * SSM_P], NT), 0.0)
                dxd_scr[:, h * SSM_P: (h + 1) * SSM_P] = _bdot(mm, dyh, TN)
                dcb = dcb + dm * lm
                wm = dm * mm
                wrow = wrow + jnp.where(lane128 == h, _rowsum(wm), 0.0)
                wcol_rows.append(_colsum(wm))
            dc_parts.append(dcg + _bdot(dcb, bg, NN))
            db_parts.append(dbg + _bdot(dcb, cg, TN))
        dxst = jnp.concatenate(dxst_parts, axis=1) * cm["dex"]
        dx = dxd_scr[...] + dxst
        ds_scr[...] = jnp.concatenate(ds_parts, axis=1) + dsn * cm["elx"]
        wcol = jnp.concatenate(wcol_rows + [jnp.zeros((128 - SSM_HEADS, q), F32)], axis=0).T
        yoff = jnp.concatenate(yoff_parts, axis=1) * cm["ex"]
        xdxst = x * dxst
        dac = wrow - wcol + _dot01(dyv * yoff - xdxst, rmat_v, NT)
        last = _dot01(_colsum(dsn * s_all) * cm["elx"] + _colsum(xdxst), rmat_v, NT)
        rowq = lax.broadcasted_iota(jnp.int32, (q, 128), 0)
        dac = dac + jnp.where(rowq == q - 1, last, 0.0)
        da = _hdot(cm["triu"], dac)
        arow_v = arow_ref[...]
        ddt = da * arow_v + _dot01(dx * xs, rmat_v, NT)
        dxs = dxs + dx * cm["dtx"]
        dal_ref[...] += _colsum(da * cm["a"])
        ddtraw = jnp.where(cm["m16"], ddt * _sigmoid(smv + dtb_ref[...]), 0.0)
        ddtb_ref[...] += _colsum(ddtraw)
        dsm_ref[...] = ddtraw.astype(BF16)
        dact_ref[:, :1024] = dxs
        for g in range(SSM_GROUPS):
            dact_ref[:, 1024 + g * SSM_N: 1024 + (g + 1) * SSM_N] = db_parts[g]
            dact_ref[:, 1280 + g * SSM_N: 1280 + (g + 1) * SSM_N] = dc_parts[g]

    rev = lambda c: nc - 1 - c
    vec128 = pl.BlockSpec((1, 128), lambda c: (0, 0))
    vec1k = pl.BlockSpec((1, 1024), lambda c: (0, 0))
    return pl.pallas_call(
        body, grid=(nc,),
        in_specs=[pl.BlockSpec((q, SSM_CONV), lambda c: (rev(c), 0)),
                  pl.BlockSpec((q, 1024), lambda c: (rev(c), Z_OFF // 1024)),
                  pl.BlockSpec((q, 128), lambda c: (rev(c), SM_OFF // 128)), vec128, vec128, vec1k, vec1k,
                  pl.BlockSpec((128, 1024), lambda c: (0, 0)),
                  pl.BlockSpec((q, 1024), lambda c: (rev(c), 0)), pl.BlockSpec((1, 128, 1024), lambda c: (rev(c), 0, 0)),
                  pl.BlockSpec((q, 1024), lambda c: (rev(c), 0))],
        out_specs=(pl.BlockSpec((q, SSM_CONV), lambda c: (rev(c), 0)), pl.BlockSpec((q, 1024), lambda c: (rev(c), 0)),
                   pl.BlockSpec((q, 128), lambda c: (rev(c), 0)), vec1k, vec128, vec128, vec128),
        out_shape=(SDS((t, SSM_CONV), F32), SDS((t, 1024), BF16), SDS((t, 128), BF16), SDS((1, 1024), F32),
                   SDS((1, 128), F32), SDS((1, 128), F32), SDS((1, 128), F32)),
        scratch_shapes=[pltpu.VMEM((128, 1024), F32), pltpu.VMEM((q, 1024), F32)], name=name,
        compiler_params=_params(("arbitrary",)),
    )(act, proj, proj, dtb, arow, dxrow, nw, rmat, ysc, states, dy)


def _split(a):
    hi = a.astype(BF16)
    return hi, (a - hi.astype(F32)).astype(BF16)


def _dot3(a, b, dims=NN):
    (ah, al), (bh, bl) = a, b

    def d(x, y):
        return lax.dot_general(x, y, (dims, ((), ())), preferred_element_type=F32)

    return d(ah, bh) + (d(ah, bl) + d(al, bh))


def _tri_inverses(amats, ii, jj):
    eye = jnp.where(ii == jj, 1.0, 0.0)
    tms = [eye - a for a in amats]
    sp = [_split(a) for a in amats]
    for _ in range(5):
        sp = [_split(_dot3(s, s)) for s in sp]
        tms = [t + _dot3(_split(t), s) for t, s in zip(tms, sp)]
    return tms


def _gdn_common(sm, gb, garow):
    q = CHUNK
    ii, jj = _tri(q)
    lane = lax.broadcasted_iota(jnp.int32, (q, 128), 1)
    ma = (lane >= LANE_A) & (lane < LANE_A + GDN_HEADS)
    spre = sm + gb
    g = jnp.where(ma, garow * _softplus(spre), 0.0)
    beta = _sigmoid(sm)
    tril = (ii >= jj).astype(F32)
    triu = (ii <= jj).astype(F32)
    gc = _hdot(tril, g)
    gc_r = _hdot(g.T, triu)
    return dict(ii=ii, jj=jj, lane=lane, ma=ma, spre=spre, g=g, beta=beta, triu=triu, gc=gc, gc_r=gc_r)


def _each(f, *lists):
    return [f(*xs) for xs in zip(*lists)]


GDN_SCALE = GDN_DK ** -0.5


def _gdn_heads(cm, actv, states):
    q = CHUNK
    ii, jj = cm["ii"], cm["jj"]
    heads = range(GDN_HEADS)
    qr = [actv[:, h * 128: (h + 1) * 128] for h in heads]
    kr = [actv[:, 1024 + h * 128: 1024 + (h + 1) * 128] for h in heads]
    v = [actv[:, 2048 + h * 128: 2048 + (h + 1) * 128] for h in heads]
    rq = _each(lambda x: lax.rsqrt(_rowsum(x * x) + EPS), qr)
    rk = _each(lambda x: lax.rsqrt(_rowsum(x * x) + EPS), kr)
    qn = _each(lambda x, r: x * r * GDN_SCALE, qr, rq)
    kn = _each(lambda x, r: x * r, kr, rk)
    gcc = [cm["gc"][:, LANE_A + h: LANE_A + h + 1] for h in heads]
    gcr = [cm["gc_r"][LANE_A + h: LANE_A + h + 1, :] for h in heads]
    bcol = [cm["beta"][:, LANE_B + h: LANE_B + h + 1] for h in heads]
    dm = _each(lambda c, r: jnp.where(ii >= jj, jnp.exp(c - r), 0.0), gcc, gcr)
    kq = _each(lambda k, a: _bdot(jnp.concatenate([k, a], axis=0), k, NT), kn, qn)
    ak = _each(lambda x, d: jnp.where(ii > jj, x[:q] * d, 0.0), kq, dm)
    qkm = _each(lambda x, d: jnp.where(ii >= jj, x[q:] * d, 0.0), kq, dm)
    tm = _tri_inverses(_each(lambda a, b: a * b, ak, bcol), ii, jj)
    eg = _each(jnp.exp, gcc)
    gl = [c[q - 1: q, :] for c in gcc]
    rm = _each(lambda vv, k, b, e: jnp.concatenate([vv * b, k * (b * e)], axis=1), v, kn, bcol, eg)
    tt = _each(lambda t, r: _dot3(_split(t), _split(r)), tm, rm)
    w = [t[:, 128:] for t in tt]
    qg = _each(lambda a, e: a * e, qn, eg)
    ws = _each(lambda ww, a, s: _bdot(jnp.concatenate([ww, a], axis=0), s, NN), w, qg, states)
    vnew = _each(lambda t, x: t[:, :128] - x[:q], tt, ws)
    return dict(qr=qr, v=v, rq=rq, rk=rk, qn=qn, kn=kn, gcc=gcc, bcol=bcol, dm=dm, ak=ak, tm=tm, eg=eg, gl=gl,
                egl=_each(jnp.exp, gl), ed=_each(lambda g, c: jnp.exp(g - c), gl, gcc), tt=tt, w=w, vnew=vnew, qkm=qkm,
                qg=qg, qgs=[x[q:] for x in ws])


def _gdn_fwd(name, act, proj, gb, garow, gnw):
    t = act.shape[0]
    q = CHUNK
    nc = t // q

    def body(act_ref, gz_ref, sm_ref, gb_ref, ga_ref, nw_ref, y_ref, o_ref, st_ref, s_scr):
        @pl.when(pl.program_id(0) == 0)
        def _():
            s_scr[...] = jnp.zeros_like(s_scr)

        st_ref[0] = s_scr[...]
        actv = act_ref[...]
        cm = _gdn_common(sm_ref[...], gb_ref[...], ga_ref[...])
        nwv = nw_ref[...]
        gzv = gz_ref[...]
        states = [s_scr[h * 128: (h + 1) * 128, :] for h in range(GDN_HEADS)]
        hd = _gdn_heads(cm, actv, states)
        outs = _each(lambda qs, m, vn: qs + _bdot(m, vn, NN), hd["qgs"], hd["qkm"], hd["vnew"])
        snew = _each(lambda s, e, k, d, vn: s * e + _bdot(k * d, vn, TN), states, hd["egl"], hd["kn"], hd["ed"], hd["vnew"])
        for h in range(GDN_HEADS):
            o = outs[h]
            s_scr[h * 128: (h + 1) * 128, :] = snew[h]
            o_ref[:, h * 128: (h + 1) * 128] = o
            rr = lax.rsqrt(jnp.mean(o * o, axis=-1, keepdims=True) + EPS)
            gz = gzv[:, h * 128: (h + 1) * 128]
            y_ref[:, h * 128: (h + 1) * 128] = (o * rr * nwv * (gz * _sigmoid(gz))).astype(BF16)

    vec128 = pl.BlockSpec((1, 128), lambda c: (0, 0))
    return pl.pallas_call(
        body, grid=(nc,),
        in_specs=[pl.BlockSpec((q, GDN_QKV), lambda c: (c, 0)), pl.BlockSpec((q, 1024), lambda c: (c, GZ_OFF // 1024)),
                  pl.BlockSpec((q, 128), lambda c: (c, SM_OFF // 128)), vec128, vec128, vec128],
        out_specs=(pl.BlockSpec((q, 1024), lambda c: (c, 0)), pl.BlockSpec((q, 1024), lambda c: (c, 0)),
                   pl.BlockSpec((1, 1024, 128), lambda c: (c, 0, 0))),
        out_shape=(SDS((t, 1024), BF16), SDS((t, 1024), F32), SDS((nc, 1024, 128), F32)),
        scratch_shapes=[pltpu.VMEM((1024, 128), F32)], name=name, compiler_params=_params(("arbitrary",)),
    )(act, proj, proj, gb, garow, gnw)


def _gdn_bwd(name, act, proj, gb, garow, gnw, oraw, states, dy):
    t = act.shape[0]
    q = CHUNK
    nc = t // q

    def body(act_ref, gz_ref, sm_ref, gb_ref, ga_ref, nw_ref, o_ref, st_ref, dy_ref,
             dact_ref, dgz_ref, dsm_ref, dnw_ref, dal_ref, dgb_ref, ds_scr):
        @pl.when(pl.program_id(0) == 0)
        def _():
            ds_scr[...] = jnp.zeros_like(ds_scr)
            dnw_ref[...] = jnp.zeros_like(dnw_ref)
            dal_ref[...] = jnp.zeros_like(dal_ref)
            dgb_ref[...] = jnp.zeros_like(dgb_ref)

        actv = act_ref[...]
        smv = sm_ref[...]
        garow_v = ga_ref[...]
        cm = _gdn_common(smv, gb_ref[...], garow_v)
        ii, jj, lane = cm["ii"], cm["jj"], cm["lane"]
        nwv = nw_ref[...]
        rowq = lax.broadcasted_iota(jnp.int32, (q, 1), 0)
        dgc_all = jnp.zeros((q, 128), F32)
        dbeta_all = jnp.zeros((q, 128), F32)
        dnw_acc = jnp.zeros((1, 128), F32)
        heads = range(GDN_HEADS)
        sts = [st_ref[0, h * 128: (h + 1) * 128, :] for h in heads]
        dsn = [ds_scr[h * 128: (h + 1) * 128, :] for h in heads]
        ov, gzv, dyv = o_ref[...], gz_ref[...], dy_ref[...]
        hd = _gdn_heads(cm, actv, sts)
        qn, kn, v, eg, ed, egl, bcol = hd["qn"], hd["kn"], hd["v"], hd["eg"], hd["ed"], hd["egl"], hd["bcol"]
        vnew, qkm, qg, w, tt, dm, ak = hd["vnew"], hd["qkm"], hd["qg"], hd["w"], hd["tt"], hd["dm"], hd["ak"]
        do = []
        for h in heads:
            hs = slice(h * 128, (h + 1) * 128)
            o = ov[:, hs]
            rr = lax.rsqrt(jnp.mean(o * o, axis=-1, keepdims=True) + EPS)
            on = o * rr
            gz = gzv[:, hs]
            sz = _sigmoid(gz)
            silz = gz * sz
            dyh = dyv[:, hs]
            dnw_acc = dnw_acc + _colsum(dyh * on * silz)
            dgz_ref[:, hs] = (dyh * on * nwv * (sz * (1.0 + gz * (1.0 - sz)))).astype(BF16)
            don = dyh * nwv * silz
            do.append(rr * (don - on * jnp.mean(don * on, axis=-1, keepdims=True)))
        kd = _each(lambda k, e: k * e, kn, ed)
        dkd = _each(lambda vn, d: _bdot(vn, d, NT), vnew, dsn)
        dvnew_a = _each(lambda k, d: _bdot(k, d, NN), kd, dsn)
        ded = _each(lambda a, b: _rowsum(a * b), dkd, kd)
        dgl = _each(lambda d, s, e, de: jnp.sum(_rowsum(d * s), axis=0, keepdims=True) * e + _colsum(de), dsn, sts, egl, ded)
        dqk = _each(lambda d, vn: jnp.where(ii >= jj, _bdot(d, vn, NT), 0.0), do, vnew)
        dvnew = _each(lambda a, m, d: a + _bdot(m, d, TN), dvnew_a, qkm, do)
        pq = _each(lambda a, b: a * b, dqk, dm)
        w1 = _each(lambda a, b: a * b, dqk, qkm)
        dod = _each(lambda a, b: jnp.concatenate([a, b], axis=0), do, dvnew)
        dos = _each(lambda x, s: _bdot(x, s, NT), dod, sts)
        dqg = [x[:q] for x in dos]
        dw = [-x[q:] for x in dos]
        ds12 = _each(lambda a, ww, x: _bdot(jnp.concatenate([a, -ww], axis=0), x, TN), qg, w, dod)
        dr = _each(lambda t, a, b: _dot3(_split(t), _split(jnp.concatenate([a, b], axis=1)), TN), hd["tm"], dvnew, dw)
        da = _each(lambda r, t: jnp.where(ii > jj, -_dot3(_split(r), _split(t), NT), 0.0), dr, tt)
        sk = _each(lambda r, k: _rowsum(r[:, 128:] * k), dr, kn)
        pk = _each(lambda a, d, b: a * d * b, da, dm, bcol)
        pkn = _each(lambda p, pp, k: _bdot(jnp.concatenate([p, pp + pp.T], axis=0), k, NN), pq, pk, kn)
        dq = _each(lambda a, e, x: a * e + x[:q], dqg, eg, pkn)
        dk = _each(lambda a, e, p, x, r, b, eg_, y: a * e + _bdot(p, x, TN) + r[:, 128:] * (b * eg_) + y[q:],
                   dkd, ed, pq, qn, dr, bcol, eg, pkn)
        w2 = _each(lambda a, k, b: a * (k * b), da, ak, bcol)
        for h in heads:
            hs = slice(h * 128, (h + 1) * 128)
            dgc = (-ded[h] + _rowsum(dqg[h] * qg[h]) + _rowsum(w1[h]) - _rowsum(w1[h].T) + sk[h] * bcol[h] * eg[h]
                   + _rowsum(w2[h]) - _rowsum(w2[h].T) + jnp.where(rowq == q - 1, dgl[h], 0.0))
            dbeta = _rowsum(dr[h][:, :128] * v[h]) + sk[h] * eg[h] + _rowsum(da[h] * ak[h])
            qhat = hd["qr"][h] * hd["rq"][h]
            dqhat = dq[h] * GDN_SCALE
            dact_ref[:, hs] = hd["rq"][h] * (dqhat - qhat * _rowsum(dqhat * qhat))
            dact_ref[:, 1024 + h * 128: 1024 + (h + 1) * 128] = hd["rk"][h] * (dk[h] - kn[h] * _rowsum(dk[h] * kn[h]))
            dact_ref[:, 2048 + h * 128: 2048 + (h + 1) * 128] = dr[h][:, :128] * bcol[h]
            dgc_all = dgc_all + jnp.where(lane == LANE_A + h, dgc, 0.0)
            dbeta_all = dbeta_all + jnp.where(lane == LANE_B + h, dbeta, 0.0)
            ds_scr[hs, :] = dsn[h] * egl[h] + ds12[h]
        dnw_ref[...] += dnw_acc
        dg = _hdot(cm["triu"], dgc_all)
        da_raw = jnp.where(cm["ma"], dg * garow_v * _sigmoid(cm["spre"]), 0.0)
        dal_ref[...] += _colsum(dg * cm["g"])
        dgb_ref[...] += _colsum(da_raw)
        beta = cm["beta"]
        dsm_ref[...] = (da_raw + dbeta_all * beta * (1.0 - beta)).astype(BF16)

    rev = lambda c: nc - 1 - c
    vec128 = pl.BlockSpec((1, 128), lambda c: (0, 0))
    return pl.pallas_call(
        body, grid=(nc,),
        in_specs=[pl.BlockSpec((q, GDN_QKV), lambda c: (rev(c), 0)),
                  pl.BlockSpec((q, 1024), lambda c: (rev(c), GZ_OFF // 1024)),
                  pl.BlockSpec((q, 128), lambda c: (rev(c), SM_OFF // 128)), vec128, vec128, vec128,
                  pl.BlockSpec((q, 1024), lambda c: (rev(c), 0)), pl.BlockSpec((1, 1024, 128), lambda c: (rev(c), 0, 0)),
                  pl.BlockSpec((q, 1024), lambda c: (rev(c), 0))],
        out_specs=(pl.BlockSpec((q, GDN_QKV), lambda c: (rev(c), 0)), pl.BlockSpec((q, 1024), lambda c: (rev(c), 0)),
                   pl.BlockSpec((q, 128), lambda c: (rev(c), 0)), vec128, vec128, vec128),
        out_shape=(SDS((t, GDN_QKV), F32), SDS((t, 1024), BF16), SDS((t, 128), BF16), SDS((1, 128), F32),
                   SDS((1, 128), F32), SDS((1, 128), F32)),
        scratch_shapes=[pltpu.VMEM((1024, 128), F32)], name=name, compiler_params=_params(("arbitrary",)),
    )(act, proj, proj, gb, garow, gnw, oraw, states, dy)


def _row_tile(r):
    for cand in (512, 256, 128, 64, 32, 16, 8):
        if r % cand == 0:
            return cand
    return r


def _sum_terms(name, terms, out_dtype):
    shape = terms[0][0].shape[1:]
    c = shape[-1]
    r = 1
    for s in shape[:-1]:
        r *= s
    tr = min(_row_tile(r), 256)
    n = len(terms)

    def body(*refs):
        acc = refs[0][...].astype(F32)
        for k in range(1, n):
            acc = acc + refs[k][...].astype(F32)
        refs[n][...] = acc.astype(out_dtype)

    in_specs = [pl.BlockSpec((None, tr, c), lambda i, q=lead: (q, i, 0)) for _, lead in terms]
    args = [a.reshape(a.shape[0], r, c) for a, _ in terms]
    out = pl.pallas_call(body, grid=(r // tr,), in_specs=in_specs, out_specs=pl.BlockSpec((tr, c), lambda i: (i, 0)),
                         out_shape=SDS((r, c), out_dtype), name=name, compiler_params=_params(("parallel",)))(*args)
    return out.reshape(shape)


def _adamw(name, w, g, m, v):
    shape = w.shape
    c = shape[-1]
    per_layer = isinstance(g, (list, tuple))
    nl = len(g) if per_layer else 1
    gs = [a.reshape(-1, c) for a in g] if per_layer else [g.reshape(-1, c)]
    r = gs[0].shape[0]
    w3, m3, v3 = (a.reshape(nl, r, c) for a in (w, m, v))
    tr = min(_row_tile(r), 256)

    def body(*refs):
        w_ref, m_ref, v_ref = refs[:3]
        g_refs = refs[3: 3 + nl]
        go_ref, d_ref, nm_ref, nv_ref = refs[3 + nl:]
        layer = pl.program_id(0)
        gv = g_refs[0][...]
        for k in range(1, nl):
            gv = jnp.where(layer == k, g_refs[k][...], gv)
        mn = ADAM_B1 * m_ref[...] + (1.0 - ADAM_B1) * gv
        vn = ADAM_B2 * v_ref[...] + (1.0 - ADAM_B2) * (gv * gv)
        m_hat = mn / (1.0 - ADAM_B1 ** ADAM_STEP)
        v_hat = vn / (1.0 - ADAM_B2 ** ADAM_STEP)
        go_ref[...] = gv
        d_ref[...] = -ADAM_LR * (m_hat / (jnp.sqrt(v_hat) + ADAM_EPS) + ADAM_WD * w_ref[...])
        nm_ref[...] = mn
        nv_ref[...] = vn

    spec3 = pl.BlockSpec((None, tr, c), lambda l, i: (l, i, 0))
    gspec = pl.BlockSpec((tr, c), lambda l, i: (i, 0))
    outs = pl.pallas_call(body, grid=(nl, r // tr), in_specs=[spec3] * 3 + [gspec] * nl, out_specs=(spec3,) * 4,
                          out_shape=(SDS((nl, r, c), F32),) * 4, name=name,
                          compiler_params=_params(("parallel", "parallel")))(w3, m3, v3, *gs)
    return tuple(o.reshape(shape) for o in outs)


ANY = pl.BlockSpec(memory_space=pl.ANY)
MESH = pl.DeviceIdType.MESH


def _allgather(name, xs):
    n = len(xs)

    def body(*refs):
        x_refs, out_refs = refs[:n], refs[n: 2 * n]
        send_sems, recv_sems, local_sems = refs[2 * n:]
        x, y, cc = lax.axis_index("x"), lax.axis_index("y"), lax.axis_index("c")
        me, sibling = (x, y, cc), (x, y, 1 - cc)
        chips = [(1 - x, y), (x, 1 - y), (1 - x, 1 - y)]

        def rows(a, px, py, pc):
            return out_refs[a].at[4 * px + 2 * py + pc]

        def copy(a, k, block, to, src=None):
            return pltpu.make_async_remote_copy(
                src_ref=rows(a, *block) if src is None else src, dst_ref=rows(a, *block),
                send_sem=send_sems.at[7 * a + k], recv_sem=recv_sems.at[7 * a + k], device_id=to, device_id_type=MESH)

        mine = [pltpu.make_async_copy(x_refs[a], rows(a, *me), local_sems.at[a]) for a in range(n)]
        for cp in mine:
            cp.start()
        first = []
        for a in range(n):
            first.append(copy(a, 0, me, sibling, src=x_refs[a]))
            first += [copy(a, 1 + j, me, (*chip, cc), src=x_refs[a]) for j, chip in enumerate(chips)]
        for cp in first:
            cp.start()
        passed = []
        for j, chip in enumerate(chips):
            for a in range(n):
                copy(a, 1 + j, (*chip, cc), me).wait_recv()
                fwd = copy(a, 4 + j, (*chip, cc), sibling)
                fwd.start()
                passed.append(fwd)
        for a in range(n):
            copy(a, 0, sibling, me).wait_recv()
        for j, chip in enumerate(chips):
            for a in range(n):
                copy(a, 4 + j, (*chip, 1 - cc), me).wait_recv()
        for cp in first + passed:
            cp.wait_send()
        for cp in mine:
            cp.wait()

    return pl.pallas_call(
        body, out_shape=tuple(SDS((N_DEV,) + a.shape, a.dtype) for a in xs), in_specs=[ANY] * n, out_specs=(ANY,) * n,
        scratch_shapes=[pltpu.SemaphoreType.DMA((7 * n,)), pltpu.SemaphoreType.DMA((7 * n,)),
                        pltpu.SemaphoreType.DMA((n,))],
        name=name,
    )(*xs)


def _allgather_seq(name, xs, collective_id):
    n = len(xs)
    x_refs = [jax.new_ref(a, memory_space=pltpu.MemorySpace.HBM) for a in xs]
    out_refs = [jax.empty_ref(SDS((N_DEV,) + a.shape, a.dtype), memory_space=pltpu.MemorySpace.HBM) for a in xs]

    @pl.kernel(mesh=plsc.ScalarSubcoreMesh(axis_name="seq", num_cores=1), name=name,
               scratch_types=(pltpu.SemaphoreType.DMA((7 * n,)), pltpu.SemaphoreType.DMA((7 * n,)),
                              pltpu.SemaphoreType.DMA((n,))),
               compiler_params=pltpu.CompilerParams(collective_id=collective_id))
    def launch(send_sems, recv_sems, local_sems):
        x, y, cc = lax.axis_index("x"), lax.axis_index("y"), lax.axis_index("c")
        me, sibling = (x, y, cc), (x, y, 1 - cc)
        chips = [(1 - x, y), (x, 1 - y), (1 - x, 1 - y)]
        barrier = pltpu.get_barrier_semaphore()
        for peer in [sibling] + [(*chip, cc) for chip in chips]:
            pl.semaphore_signal(barrier, inc=1, device_id=peer, device_id_type=MESH)
        pl.semaphore_wait(barrier, 4)

        def rows(a, px, py, pc):
            return out_refs[a].at[4 * px + 2 * py + pc]

        def copy(a, k, block, to, src=None):
            return pltpu.make_async_remote_copy(
                src_ref=rows(a, *block) if src is None else src, dst_ref=rows(a, *block),
                send_sem=send_sems.at[7 * a + k], recv_sem=recv_sems.at[7 * a + k], device_id=to, device_id_type=MESH)

        mine = [pltpu.make_async_copy(x_refs[a], rows(a, *me), local_sems.at[a]) for a in range(n)]
        for cp in mine:
            cp.start()
        first = []
        for a in range(n):
            first.append(copy(a, 0, me, sibling, src=x_refs[a]))
            first += [copy(a, 1 + j, me, (*chip, cc), src=x_refs[a]) for j, chip in enumerate(chips)]
        for cp in first:
            cp.start()
        passed = []
        for j, chip in enumerate(chips):
            for a in range(n):
                copy(a, 1 + j, (*chip, cc), me).wait_recv()
                fwd = copy(a, 4 + j, (*chip, cc), sibling)
                fwd.start()
                passed.append(fwd)
        for a in range(n):
            copy(a, 0, sibling, me).wait_recv()
        for j, chip in enumerate(chips):
            for a in range(n):
                copy(a, 4 + j, (*chip, 1 - cc), me).wait_recv()
        for cp in first + passed:
            cp.wait_send()
        for cp in mine:
            cp.wait()

    launch()
    return [r[...] for r in out_refs]


HBM = pl.BlockSpec(memory_space=pltpu.HBM)
SEM = pl.BlockSpec(memory_space=pltpu.SEMAPHORE)
EFFECT = pltpu.SideEffectType.DATAFLOW_SIDE_EFFECTING


def _sibling_plan(srcs, lands, send_sems, recv_sems):
    x, y, cc = lax.axis_index("x"), lax.axis_index("y"), lax.axis_index("c")
    return [pltpu.make_async_remote_copy(
        src_ref=srcs[a].at[2 * q + 1 - cc], dst_ref=lands[a].at[q], send_sem=send_sems.at[4 * a + q],
        recv_sem=recv_sems.at[4 * a + q], device_id=(x, y, 1 - cc), device_id_type=MESH)
        for a in range(len(srcs)) for q in range(4)]


def _chips_plan(srcs, lands, send_sems, recv_sems):
    x, y, cc = lax.axis_index("x"), lax.axis_index("y"), lax.axis_index("c")
    chips = [(1 - x, y), (x, 1 - y), (1 - x, 1 - y)]
    return [pltpu.make_async_remote_copy(
        src_ref=srcs[a].at[2 * px + py], dst_ref=lands[a].at[j], send_sem=send_sems.at[3 * a + j],
        recv_sem=recv_sems.at[3 * a + j], device_id=(px, py, cc), device_id_type=MESH)
        for a in range(len(srcs)) for j, (px, py) in enumerate(chips)]


def _copies_start(name, plan, per_array, srcs, land_lead):
    n = len(srcs)
    k = per_array * n

    def body(*refs):
        src_refs, land_refs = refs[:n], refs[n: 2 * n]
        send_sems, recv_sems = refs[2 * n], refs[2 * n + 1]
        token = refs[-1]
        for cp in plan(src_refs, land_refs, send_sems, recv_sems):
            cp.start()
        token[...] = jnp.zeros_like(token)

    lands = [lax.empty((land_lead,) + a.shape[1:], a.dtype) for a in srcs]
    outs = pl.pallas_call(
        body, name=name,
        out_shape=(pltpu.SemaphoreType.DMA((k,)), pltpu.SemaphoreType.DMA((k,)),
                   *[pltpu.HBM(a.shape, a.dtype) for a in srcs], *[pltpu.HBM(a.shape, a.dtype) for a in lands],
                   SDS((8, 128), F32)),
        in_specs=[HBM] * (2 * n), out_specs=(SEM, SEM, *[HBM] * (2 * n), pl.BlockSpec(memory_space=pltpu.VMEM)),
        input_output_aliases={i: 2 + i for i in range(2 * n)},
        compiler_params=pltpu.CompilerParams(has_side_effects=EFFECT),
    )(*[pltpu.with_memory_space_constraint(a, pltpu.HBM) for a in srcs],
      *[pltpu.with_memory_space_constraint(a, pltpu.HBM) for a in lands])
    return outs[0], outs[1], list(outs[2: 2 + n]), list(outs[2 + n: 2 + 2 * n]), outs[-1]


def _copies_wait(name, plan, started, after):
    send_sems, recv_sems, srcs, lands, _ = started
    n = len(srcs)
    after = tuple(after)

    def body(*refs):
        src_refs, land_refs = refs[:n], refs[n: 2 * n]
        for cp in plan(src_refs, land_refs, refs[2 * n], refs[2 * n + 1]):
            cp.wait_send()
            cp.wait_recv()

    outs = pl.pallas_call(
        body, name=name,
        out_shape=tuple(pltpu.HBM(a.shape, a.dtype) for a in srcs + lands),
        in_specs=[HBM] * (2 * n) + [SEM, SEM] + [ANY] * len(after), out_specs=(HBM,) * (2 * n),
        input_output_aliases={i: i for i in range(2 * n)},
        compiler_params=pltpu.CompilerParams(has_side_effects=EFFECT),
    )(*srcs, *lands, send_sems, recv_sems, *after)
    return list(outs[n:])


BIG = (("w_in", 1), ("w_ffn_in", 1), ("w_proj_ssm", 0), ("w_proj_gdn", 0), ("w_out", 0), ("w_ffn_down", 0))
CONVS = (("ssm_conv_w", 1), ("gdn_conv_w", 1))


def _to_dest_major(full, axis):
    a, b = full.shape
    if axis == 0:
        return full.reshape(N_DEV, a // N_DEV, b)
    s = b // N_DEV
    return jnp.stack([full[:, d * s: (d + 1) * s] for d in range(N_DEV)])


def _from_gathered(g, axis):
    if axis == 0:
        return g.reshape(-1, g.shape[2])
    return jnp.concatenate([g[d] for d in range(N_DEV)], axis=1)


IN_RUNS = ((Z_OFF, O_Z, 1024), (GZ_OFF, O_GZ, 1024), (G1_OFF, O_G1, 1024), (G2_OFF, O_G2, 1024), (QKV_OFF, O_QKV, 3072),
           (XBC_OFF, O_XBC, 1536), (SM_OFF, O_DT, 16), (SM_OFF + LANE_A, O_A, 8), (SM_OFF + LANE_B, O_B, 8))
IN_SHARD = IN_DIM // N_DEV


def _w_in_from_blocks(g):
    rows = g.shape[1]
    parts, pos = [], 0
    for off, o0, width in IN_RUNS:
        if off > pos:
            parts.append(jnp.zeros((rows, off - pos), g.dtype))
        c = o0
        while c < o0 + width:
            d = c // IN_SHARD
            hi = min(o0 + width, (d + 1) * IN_SHARD)
            parts.append(g[d][:, c - d * IN_SHARD: hi - d * IN_SHARD])
            c = hi
        pos = off + width
    parts.append(jnp.zeros((rows, PROJ_W - pos), g.dtype))
    return jnp.concatenate(parts, axis=1)


def _w_in_to_blocks(wp):
    by_orig = sorted(IN_RUNS, key=lambda r: r[1])
    blocks = []
    for d in range(N_DEV):
        lo, hi = d * IN_SHARD, (d + 1) * IN_SHARD
        parts = []
        for off, o0, width in by_orig:
            a, b = max(lo, o0), min(hi, o0 + width)
            if a < b:
                parts.append(wp[:, off + a - o0: off + b - o0])
        blocks.append(jnp.concatenate(parts, axis=1))
    return jnp.stack(blocks)


def _pad128(v, lane0):
    return jnp.zeros((1, 128), F32).at[0, lane0: lane0 + v.shape[0]].set(v)


def _layer_consts(p):
    return dict(
        dtb=_pad128(p["ssm_dt_bias"], 0), arow=_pad128(-jnp.exp(p["ssm_a_log"]), 0),
        dxrow=jnp.repeat(p["ssm_d"], SSM_P).reshape(1, 1024), snw=p["ssm_norm_w"].reshape(1, 1024),
        gb=_pad128(p["gdn_dt_bias"], LANE_A), garow=_pad128(-jnp.exp(p["gdn_a_log"]), LANE_A),
        gnw=p["gdn_norm_w"].reshape(1, 128), zb=jnp.zeros((1, GDN_QKV), F32), scb=p["ssm_conv_b"].reshape(1, SSM_CONV))


def _expand_matrix():
    row = lax.broadcasted_iota(jnp.int32, (128, 1024), 0)
    col = lax.broadcasted_iota(jnp.int32, (128, 1024), 1)
    return (col // SSM_P == row).astype(BF16)


def _silu_mul_epi(acc, up):
    g = acc
    return g, g * _sigmoid(g) * up.astype(F32)


def _merge_epi(acc, p1, g1, g2):
    return acc, _sigmoid(g1) * p1.astype(F32) + _sigmoid(g2) * acc


def _add_epi(acc, res):
    return (acc + res,)


def _ffn_bwd_epi(acc, gate, up):
    g = gate.astype(F32)
    sg = _sigmoid(g)
    return acc * up.astype(F32) * (sg * (1.0 + g * (1.0 - sg))), acc * (g * sg)


def _merge_bwd_epi(acc, g1, g2, p1, p2):
    s1, s2 = _sigmoid(g1), _sigmoid(g2)
    return acc * s1, acc * s2, acc * p1.astype(F32) * (s1 * (1.0 - s1)), acc * p2.astype(F32) * (s2 * (1.0 - s2))


def _layer_fwd(l, x, p, rmat):
    t = x.shape[0]
    n = f"l{l}_"
    k = _layer_consts(p)
    h = _rmsnorm_fwd(n + "norm_mix", x, p["norm_mix_w"])
    proj = _matmul(n + "in_proj", "nn", [(h, 0, p["w_in"], 0)], t, PROJ_W, 1024, 1024, 1280, 1024, (F32,))
    act_g = _conv_fwd(n + "conv_gdn", proj, QKV_OFF, p["gdn_conv_w"], k["zb"])
    act_s = _conv_fwd(n + "conv_ssm", proj, XBC_OFF, p["ssm_conv_w"], k["scb"])
    y_ssm, ysc, st_s = _ssd_fwd(n + "ssd_fwd", act_s, proj, k["dtb"], k["arow"], k["dxrow"], k["snw"], rmat)
    y_gdn, oraw, st_g = _gdn_fwd(n + "gdn_fwd", act_g, proj, k["gb"], k["garow"], k["gnw"])
    if "late" in p:
        y_gdn, late = p["late"](y_gdn)
        p = {**p, **late}
    p1 = _matmul(n + "proj_ssm", "nn", [(y_ssm, 0, p["w_proj_ssm"], 0)], t, 1024, 1024, 1024, 1024, 1024, (BF16,))
    p2, merged = _matmul(n + "proj_gdn_merge", "nn", [(y_gdn, 0, p["w_proj_gdn"], 0)], t, 1024, 1024, 512, 1024, 1024,
                         (BF16, BF16), epi=_merge_epi, extras=[(p1, 0), (proj, G1_OFF // 1024), (proj, G2_OFF // 1024)])
    x1 = _matmul(n + "out_proj", "nn", [(merged, 0, p["w_out"], 0)], t, 1024, 1024, 1024, 1024, 1024, (F32,),
                 epi=_add_epi, extras=[(x, 0)])
    h2 = _rmsnorm_fwd(n + "norm_ffn", x1, p["norm_ffn_w"])
    up = _matmul(n + "ffn_up", "nn", [(h2, 0, p["w_ffn_in"], 2)], t, FFN, 1024, 1024, FFN // 2, 1024, (BF16,))
    gate, act = _matmul(n + "ffn_gate", "nn", [(h2, 0, p["w_ffn_in"], 0)], t, FFN, 1024, 1024, FFN // 2, 1024, (BF16, BF16),
                        epi=_silu_mul_epi, extras=[(up, 0)])
    x2 = _matmul(n + "ffn_down", "nn", [(act, 0, p["w_ffn_down"], 0)], t, 1024, FFN, 1024, 1024, FFN, (F32,),
                 epi=_add_epi, extras=[(x1, 0)])
    saved = dict(x=x, h=h, proj=proj, act_g=act_g, act_s=act_s, y_ssm=y_ssm, ysc=ysc, st_s=st_s, y_gdn=y_gdn, oraw=oraw,
                 st_g=st_g, p1=p1, p2=p2, merged=merged, x1=x1, h2=h2, up=up, gate=gate, act=act, k=k, p=p)
    return x2, saved


def _layer_bwd(l, dx2, dx2b, s, p, rmat, hooks):
    t = dx2.shape[0]
    n = f"l{l}_"
    k = s["k"]
    tk_tok = 1024
    hf = FFN // 2
    g = {}
    dgate, dup = _matmul(n + "d_ffn_act", "nt", [(dx2b, 0, p["w_ffn_down"], 0)], t, FFN, 1024, 1024, hf, 1024, (BF16, BF16),
                         epi=_ffn_bwd_epi, extras=[(s["gate"], 0), (s["up"], 0)])
    g["w_ffn_down"] = _matmul(n + "dw_ffn_down", "tn", [(s["act"], 0, dx2b, 0)], FFN, 1024, t, hf, 1024, tk_tok, (BF16,))
    dh2 = _matmul(n + "d_ffn_in", "nt", [(dgate, 0, p["w_ffn_in"], 0), (dup, 0, p["w_ffn_in"], 2)], t, 1024, FFN,
                  1024, 1024, hf, (F32,))
    dwg = _matmul(n + "dw_ffn_gate", "tn", [(s["h2"], 0, dgate, 0)], 1024, FFN, t, 1024, hf, tk_tok, (BF16,))
    dwu = _matmul(n + "dw_ffn_up", "tn", [(s["h2"], 0, dup, 0)], 1024, FFN, t, 1024, hf, tk_tok, (BF16,))
    g["w_ffn_in"] = jnp.concatenate([dwg, dwu], axis=1)
    dx1, dx1b, g["norm_ffn_w"] = _rmsnorm_bwd(n + "d_norm_ffn", s["x1"], p["norm_ffn_w"], dh2, dx2)
    dx1b = hooks.ffn_done(dx1b)
    dp1, dp2, dg1, dg2 = _matmul(
        n + "d_out_proj", "nt", [(dx1b, 0, p["w_out"], 0)], t, 1024, 1024, 512, 1024, 1024, (BF16,) * 4, epi=_merge_bwd_epi,
        extras=[(s["proj"], G1_OFF // 1024), (s["proj"], G2_OFF // 1024), (s["p1"], 0), (s["p2"], 0)])
    g["w_out"] = _matmul(n + "dw_out", "tn", [(s["merged"], 0, dx1b, 0)], 1024, 1024, t, 1024, 1024, tk_tok, (BF16,))
    g["w_proj_ssm"] = _matmul(n + "dw_proj_ssm", "tn", [(s["y_ssm"], 0, dp1, 0)], 1024, 1024, t, 1024, 1024, tk_tok, (BF16,))
    g["w_proj_gdn"] = _matmul(n + "dw_proj_gdn", "tn", [(s["y_gdn"], 0, dp2, 0)], 1024, 1024, t, 1024, 1024, tk_tok, (BF16,))
    dp1, dp2 = hooks.early_ready(l, g, dp1, dp2)
    dy_ssm = _matmul(n + "d_proj_ssm", "nt", [(dp1, 0, p["w_proj_ssm"], 0)], t, 1024, 1024, 1024, 1024, 1024, (F32,))
    dy_gdn = _matmul(n + "d_proj_gdn", "nt", [(dp2, 0, p["w_proj_gdn"], 0)], t, 1024, 1024, 1024, 1024, 1024, (F32,))
    dact_s, dz, dsm_s, dsnw, dd, dal, ddtb = _ssd_bwd(n + "ssd_bwd", s["act_s"], s["proj"], k["dtb"], k["arow"], k["dxrow"],
                                                        k["snw"], rmat, s["ysc"], s["st_s"], dy_ssm)
    dact_g, dgz, dsm_g, dgnw, dgal, dgb = _gdn_bwd(n + "gdn_bwd", s["act_g"], s["proj"], k["gb"], k["garow"], k["gnw"],
                                                     s["oraw"], s["st_g"], dy_gdn)
    dsm_s = hooks.mixers_done(dsm_s)
    du_s, g["ssm_conv_w"], dcb = _conv_bwd(n + "d_conv_ssm", s["proj"], XBC_OFF, p["ssm_conv_w"], k["scb"], dact_s)
    du_g, g["gdn_conv_w"], _ = _conv_bwd(n + "d_conv_gdn", s["proj"], QKV_OFF, p["gdn_conv_w"], k["zb"], dact_g)
    g["ssm_conv_b"] = dcb.reshape(-1)
    g["ssm_norm_w"] = dsnw.reshape(-1)
    g["ssm_d"] = dd[0, :SSM_HEADS]
    g["ssm_a_log"] = dal[0, :SSM_HEADS]
    g["ssm_dt_bias"] = ddtb[0, :SSM_HEADS]
    g["gdn_norm_w"] = dgnw.reshape(-1)
    g["gdn_a_log"] = dgal[0, LANE_A: LANE_A + GDN_HEADS]
    g["gdn_dt_bias"] = dgb[0, LANE_A: LANE_A + GDN_HEADS]
    dproj = jnp.concatenate([dz, dgz, dg1, dg2, du_g, du_s, dsm_s + dsm_g, jnp.zeros((t, PROJ_W - SM_OFF - 128), BF16)],
                            axis=1)
    dh = _matmul(n + "d_in_proj", "nt", [(dproj, 0, p["w_in"], 0)], t, 1024, PROJ_W, 1024, 1024, 1280, (F32,))
    g["w_in"] = _matmul(n + "dw_in", "tn", [(s["h"], 0, dproj, 0)], 1024, PROJ_W, t, 1024, 1280, tk_tok, (BF16,))
    dx, dxb, g["norm_mix_w"] = _rmsnorm_bwd(n + "d_norm_mix", s["x"], p["norm_mix_w"], dh, dx1)
    g["norm_mix_w"] = g["norm_mix_w"].reshape(-1)
    g["norm_ffn_w"] = g["norm_ffn_w"].reshape(-1)
    return dx, dxb, g


def _local_step(x, tgt, layers, final_norm_w, reduce=False):
    rmat = _expand_matrix()
    saved, params = [], []
    for l in range(DEPTH):
        x, p = layers[l](x)
        x, s = _layer_fwd(l, x, p, rmat)
        saved.append(s)
        params.append(s["p"])
    loss, dx, dxb, dfw = _loss_head("loss_head", x, final_norm_w, tgt)
    grads = [None] * DEPTH
    hooks = _ReduceBesideBackward() if reduce else _NoReduce()
    for l in reversed(range(DEPTH)):
        dx, dxb, grads[l] = _layer_bwd(l, dx, dxb, saved[l], params[l], rmat, hooks)
        if reduce:
            dxb = hooks.layer_done(l, grads[l], dxb)
    if reduce:
        hooks.finish(dxb)
    return loss[0, 0], dx, grads, dfw.reshape(-1), hooks.shards if reduce else None


SMALL = ("norm_mix_w", "ssm_conv_b", "ssm_dt_bias", "ssm_a_log", "ssm_d", "ssm_norm_w", "gdn_a_log", "gdn_dt_bias",
         "gdn_norm_w", "norm_ffn_w")
WEIGHTS = ("norm_mix_w", "w_in", "ssm_conv_w", "ssm_conv_b", "ssm_dt_bias", "ssm_a_log", "ssm_d", "ssm_norm_w", "gdn_conv_w",
           "gdn_a_log", "gdn_dt_bias", "gdn_norm_w", "w_proj_ssm", "w_proj_gdn", "w_out", "norm_ffn_w", "w_ffn_in",
           "w_ffn_down", "final_norm_w")


FIRST_USED = ("w_in", "ssm_conv_w", "gdn_conv_w")


def _gather_layer(l, w):
    conv_names = [nm for nm, _ in CONVS]
    groups = ([s for s in BIG + CONVS if s[0] in FIRST_USED], [s for s in BIG + CONVS if s[0] not in FIRST_USED])
    gathered = []
    for i, (specs, tag) in enumerate(zip(groups, ("first", "rest"))):
        shards = [w[nm][l] if nm in conv_names else w[nm][l].astype(BF16) for nm, _ in specs]
        gathered.append(_allgather_seq(f"l{l}_gather_{tag}", shards, collective_id=2 * l + i))
    small = {nm: w[nm][l] for nm in SMALL}

    def use(i, act):
        act, blocks = lax.optimization_barrier((act, gathered[i]))
        return act, {nm: _w_in_from_blocks(g) if nm == "w_in" else _from_gathered(g, axis)
                     for (nm, axis), g in zip(groups[i], blocks)}

    def full_weights(x):
        x, out = use(0, x)
        out.update(small)
        out["late"] = lambda y: use(1, y)
        return x, out

    return full_weights


EARLY_GRADS = ("w_ffn_down", "w_ffn_in", "w_out", "w_proj_ssm", "w_proj_gdn")


class _GradReduceScatter:
    def __init__(self, tag, specs, grads):
        self.tag = tag
        self.specs = specs
        self.blocks = [_w_in_to_blocks(grads[nm]) if nm == "w_in" else _to_dest_major(grads[nm], axis)
                       for nm, axis in specs]

    def _tied(self, started, acts):
        *acts, self.token = lax.optimization_barrier((*acts, started[4]))
        return acts

    def start(self, *acts):
        cc = lax.axis_index("c")
        self.keep = [lax.dynamic_index_in_dim(b.reshape((4, 2) + b.shape[1:]), cc, axis=1, keepdims=False)
                     for b in self.blocks]
        self.to_sibling = _copies_start(f"{self.tag}_to_sibling_start", _sibling_plan, 4, self.blocks, 4)
        return self._tied(self.to_sibling, acts)

    def mid(self, *acts):
        got = _copies_wait(f"{self.tag}_to_sibling_wait", _sibling_plan, self.to_sibling, (acts[0], self.token))
        chip_sums = [_sum_terms(f"{self.tag}_chip_sum_{nm}", [(k[None], 0), (g[None], 0)], BF16)
                     for (nm, _), k, g in zip(self.specs, self.keep, got)]
        self.to_chips = _copies_start(f"{self.tag}_between_chips_start", _chips_plan, 3, chip_sums, 3)
        return self._tied(self.to_chips, acts)

    def end(self, after):
        landed = _copies_wait(f"{self.tag}_between_chips_wait", _chips_plan, self.to_chips, (after, self.token))
        my_chip = 2 * lax.axis_index("x") + lax.axis_index("y")
        own = [lax.dynamic_index_in_dim(s, my_chip, axis=0, keepdims=True) for s in self.to_chips[2]]
        return {nm: _sum_terms(f"{self.tag}_total_{nm}", [(o, 0), (e, 0), (e, 1), (e, 2)], F32)
                for (nm, _), o, e in zip(self.specs, own, landed)}


class _NoReduce:
    def ffn_done(self, dx1b):
        return dx1b

    def early_ready(self, l, g, dp1, dp2):
        return dp1, dp2

    def mixers_done(self, dsm):
        return dsm


class _ReduceBesideBackward(_NoReduce):
    def __init__(self):
        self.late = None
        self.early = None
        self.shards = [dict() for _ in range(DEPTH)]

    def ffn_done(self, dx1b):
        if self.late is not None:
            (dx1b,) = self.late.mid(dx1b)
        return dx1b

    def early_ready(self, l, g, dp1, dp2):
        self.early = _GradReduceScatter(f"l{l}_early_grads", [s for s in BIG if s[0] in EARLY_GRADS], g)
        return self.early.start(dp1, dp2)

    def mixers_done(self, dsm):
        (dsm,) = self.early.mid(dsm)
        return dsm

    def layer_done(self, l, g, dxb):
        if self.late is not None:
            self.shards[l + 1].update(self.late.end(dxb))
        self.shards[l].update(self.early.end(dxb))
        self.late = _GradReduceScatter(f"l{l}_late_grads", [s for s in BIG + CONVS if s[0] not in EARLY_GRADS], g)
        (dxb,) = self.late.start(dxb)
        return dxb

    def finish(self, dxb):
        (dxb,) = self.late.mid(dxb)
        self.shards[0].update(self.late.end(dxb))
        return dxb


def _allreduce_small(vecs):
    flat = jnp.concatenate(vecs)
    n = flat.shape[0]
    rows = -(-n // 128)
    rows = -(-rows // 8) * 8
    buf = jnp.pad(flat, (0, rows * 128 - n)).reshape(rows, 128)
    (allv,) = _allgather("gather_small_grads", [buf])
    tot = _sum_terms("small_grads_total", [(allv, d) for d in range(N_DEV)], F32).reshape(-1)
    out, o = [], 0
    for v in vecs:
        out.append(tot[o: o + v.shape[0]])
        o += v.shape[0]
    return out


def kernel(x, norm_mix_w, w_in, ssm_conv_w, ssm_conv_b, ssm_dt_bias, ssm_a_log, ssm_d, ssm_norm_w, gdn_conv_w, gdn_a_log, gdn_dt_bias, gdn_norm_w, w_proj_ssm, w_proj_gdn, w_out, norm_ffn_w, w_ffn_in, w_ffn_down, final_norm_w, loss_target, m_norm_mix_w, m_w_in, m_ssm_conv_w, m_ssm_conv_b, m_ssm_dt_bias, m_ssm_a_log, m_ssm_d, m_ssm_norm_w, m_gdn_conv_w, m_gdn_a_log, m_gdn_dt_bias, m_gdn_norm_w, m_w_proj_ssm, m_w_proj_gdn, m_w_out, m_norm_ffn_w, m_w_ffn_in, m_w_ffn_down, m_final_norm_w, v_norm_mix_w, v_w_in, v_ssm_conv_w, v_ssm_conv_b, v_ssm_dt_bias, v_ssm_a_log, v_ssm_d, v_ssm_norm_w, v_gdn_conv_w, v_gdn_a_log, v_gdn_dt_bias, v_gdn_norm_w, v_w_proj_ssm, v_w_proj_gdn, v_w_out, v_norm_ffn_w, v_w_ffn_in, v_w_ffn_down, v_final_norm_w):
    w = dict(norm_mix_w=norm_mix_w, w_in=w_in, ssm_conv_w=ssm_conv_w, ssm_conv_b=ssm_conv_b, ssm_dt_bias=ssm_dt_bias,
             ssm_a_log=ssm_a_log, ssm_d=ssm_d, ssm_norm_w=ssm_norm_w, gdn_conv_w=gdn_conv_w, gdn_a_log=gdn_a_log,
             gdn_dt_bias=gdn_dt_bias, gdn_norm_w=gdn_norm_w, w_proj_ssm=w_proj_ssm, w_proj_gdn=w_proj_gdn, w_out=w_out,
             norm_ffn_w=norm_ffn_w, w_ffn_in=w_ffn_in, w_ffn_down=w_ffn_down, final_norm_w=final_norm_w)
    m = dict(norm_mix_w=m_norm_mix_w, w_in=m_w_in, ssm_conv_w=m_ssm_conv_w, ssm_conv_b=m_ssm_conv_b, ssm_dt_bias=m_ssm_dt_bias,
             ssm_a_log=m_ssm_a_log, ssm_d=m_ssm_d, ssm_norm_w=m_ssm_norm_w, gdn_conv_w=m_gdn_conv_w, gdn_a_log=m_gdn_a_log,
             gdn_dt_bias=m_gdn_dt_bias, gdn_norm_w=m_gdn_norm_w, w_proj_ssm=m_w_proj_ssm, w_proj_gdn=m_w_proj_gdn,
             w_out=m_w_out, norm_ffn_w=m_norm_ffn_w, w_ffn_in=m_w_ffn_in, w_ffn_down=m_w_ffn_down,
             final_norm_w=m_final_norm_w)
    v = dict(norm_mix_w=v_norm_mix_w, w_in=v_w_in, ssm_conv_w=v_ssm_conv_w, ssm_conv_b=v_ssm_conv_b, ssm_dt_bias=v_ssm_dt_bias,
             ssm_a_log=v_ssm_a_log, ssm_d=v_ssm_d, ssm_norm_w=v_ssm_norm_w, gdn_conv_w=v_gdn_conv_w, gdn_a_log=v_gdn_a_log,
             gdn_dt_bias=v_gdn_dt_bias, gdn_norm_w=v_gdn_norm_w, w_proj_ssm=v_w_proj_ssm, w_proj_gdn=v_w_proj_gdn,
             w_out=v_w_out, norm_ffn_w=v_norm_ffn_w, w_ffn_in=v_w_ffn_in, w_ffn_down=v_w_ffn_down,
             final_norm_w=v_final_norm_w)

    layers = [_gather_layer(l, w) for l in range(DEPTH)]
    loss_part, dx, lgrads, dfw, shard_grads = _local_step(x[0], loss_target[0], layers, final_norm_w, reduce=True)
    loss = lax.psum(loss_part, ("x", "y", "c"))
    grad = {nm: [shard_grads[l][nm] for l in range(DEPTH)] for nm, _ in BIG + CONVS}
    small_vecs = [lgrads[l][nm].reshape(-1) for l in range(DEPTH) for nm in SMALL] + [dfw]
    small_sum = _allreduce_small(small_vecs)
    for i, nm in enumerate(SMALL):
        grad[nm] = jnp.stack([small_sum[l * len(SMALL) + i].reshape(w[nm].shape[1:]) for l in range(DEPTH)])
    grad["final_norm_w"] = small_sum[-1]

    deltas, new_m, new_v = {}, {}, {}
    for nm in WEIGHTS:
        grad[nm], deltas[nm], new_m[nm], new_v[nm] = _adamw("adamw_" + nm, w[nm], grad[nm], m[nm], v[nm])
    return (loss, dx[None], *[grad[nm] for nm in WEIGHTS], *[deltas[nm] for nm in WEIGHTS],
            *[new_m[nm] for nm in WEIGHTS], *[new_v[nm] for nm in WEIGHTS])
```

```python
import functools

import jax
import jax.numpy as jnp
from jax import lax
from jax.experimental import pallas as pl
from jax.experimental.pallas import tpu as pltpu
from jax.experimental.pallas import tpu_sc as plsc

F32 = jnp.float32
BF16 = jnp.bfloat16
HI = lax.Precision.HIGHEST
SDS = jax.ShapeDtypeStruct

D_MODEL = 1024
DEPTH = 2
SSM_HEADS = 16
SSM_P = 64
SSM_N = 128
SSM_GROUPS = 2
SSM_CONV = 1536
GDN_HEADS = 8
GDN_DK = 128
GDN_QKV = 3072
CONV_K = 4
CHUNK = 64
FFN = 2816
IN_DIM = 8736
EPS = 1e-6
N_DEV = 8

Z_OFF = 0
GZ_OFF = 1024
G1_OFF = 2048
G2_OFF = 3072
QKV_OFF = 4096
XBC_OFF = 7168
SM_OFF = 8704
PROJ_W = 8960
LANE_A = 16
LANE_B = 24
O_Z, O_XBC, O_DT, O_QKV, O_GZ, O_A, O_B, O_G1, O_G2 = 0, 1024, 2560, 2576, 5648, 6672, 6680, 6688, 7712

ADAM_LR = 0.001
ADAM_B1 = 0.9
ADAM_B2 = 0.999
ADAM_EPS = 1e-08
ADAM_WD = 0.01
ADAM_STEP = 10

V7X_VMEM_LIMIT = 48 * 1024 * 1024

NN = ((1,), (0,))
NT = ((1,), (1,))
TN = ((0,), (0,))


def _bdot(a, b, dims):
    return lax.dot_general(a.astype(BF16), b.astype(BF16), (dims, ((), ())), preferred_element_type=F32)


def _hdot(a, b, dims=NN):
    return lax.dot_general(a, b, (dims, ((), ())), precision=HI, preferred_element_type=F32)


def _sigmoid(x):
    return 1.0 / (1.0 + jnp.exp(-x))


def _softplus(x):
    return jnp.maximum(x, 0.0) + jnp.log(1.0 + jnp.exp(-jnp.abs(x)))


def _params(dims):
    return pltpu.CompilerParams(dimension_semantics=dims, vmem_limit_bytes=V7X_VMEM_LIMIT)


def _rowsum(x):
    return jnp.sum(x, axis=-1, keepdims=True)


def _colsum(x):
    return jnp.sum(x, axis=0, keepdims=True)


def _matmul(name, mode, pairs, m, n, kdim, tm, tn, tk, out_dtypes, epi=None, extras=()):
    tm, tn, tk = min(tm, m), min(tn, n), min(tk, kdim)
    nk = kdim // tk
    assert m % tm == 0 and n % tn == 0 and kdim % tk == 0, (name, m, n, kdim, tm, tn, tk)
    in_specs, args = [], []
    for a, a_off, b, b_off in pairs:
        if mode == "nn":
            in_specs.append(pl.BlockSpec((tm, tk), lambda i, j, k, o=a_off: (i, k + o)))
            in_specs.append(pl.BlockSpec((tk, tn), lambda i, j, k, o=b_off: (k, j + o)))
            dims = NN
        elif mode == "nt":
            in_specs.append(pl.BlockSpec((tm, tk), lambda i, j, k, o=a_off: (i, k + o)))
            in_specs.append(pl.BlockSpec((tn, tk), lambda i, j, k, o=b_off: (j, k + o)))
            dims = NT
        else:
            in_specs.append(pl.BlockSpec((tk, tm), lambda i, j, k, o=a_off: (k, i + o)))
            in_specs.append(pl.BlockSpec((tk, tn), lambda i, j, k, o=b_off: (k, j + o)))
            dims = TN
        args += [a, b]
    for e, e_off in extras:
        in_specs.append(pl.BlockSpec((tm, tn), lambda i, j, k, o=e_off: (i, j + o)))
        args.append(e)
    npair, nex, nout = len(pairs), len(extras), len(out_dtypes)

    def body(*refs):
        prefs = refs[: 2 * npair]
        erefs = refs[2 * npair: 2 * npair + nex]
        orefs = refs[2 * npair + nex: 2 * npair + nex + nout]

        def finish(res):
            outs = (res,) if epi is None else epi(res, *[e[...] for e in erefs])
            for o, r in zip(orefs, outs):
                o[...] = r.astype(o.dtype)

        s = _bdot(prefs[0][...], prefs[1][...], dims)
        for p in range(1, npair):
            s = s + _bdot(prefs[2 * p][...], prefs[2 * p + 1][...], dims)
        if nk == 1:
            finish(s)
            return
        acc = refs[-1]
        k = pl.program_id(2)

        @pl.when(k == 0)
        def _():
            acc[...] = s

        @pl.when(k > 0)
        def _():
            acc[...] += s

        @pl.when(k == nk - 1)
        def _():
            finish(acc[...])

    out_shape, out_specs = [], []
    for od in out_dtypes:
        if isinstance(od, tuple):
            dt, full_w, blk_w, cblk = od
            assert n == tn
            out_shape.append(SDS((m, full_w), dt))
            out_specs.append(pl.BlockSpec((tm, blk_w), lambda i, j, k, c=cblk: (i, c)))
        else:
            out_shape.append(SDS((m, n), od))
            out_specs.append(pl.BlockSpec((tm, tn), lambda i, j, k: (i, j)))
    out_shape, out_specs = tuple(out_shape), tuple(out_specs)
    res = pl.pallas_call(
        body, grid=(m // tm, n // tn, nk), in_specs=in_specs, out_specs=out_specs, out_shape=out_shape,
        scratch_shapes=[pltpu.VMEM((tm, tn), F32)] if nk > 1 else [], name=name,
        compiler_params=_params(("parallel", "parallel", "arbitrary")),
    )(*args)
    return res if nout > 1 else res[0]


def _rmsnorm_fwd(name, x, w):
    t, d = x.shape
    tm = min(512, t)

    def body(x_ref, w_ref, h_ref):
        xv = x_ref[...]
        r = lax.rsqrt(jnp.mean(xv * xv, axis=-1, keepdims=True) + EPS)
        h_ref[...] = (xv * r * w_ref[...]).astype(BF16)

    return pl.pallas_call(
        body, grid=(t // tm,),
        in_specs=[pl.BlockSpec((tm, d), lambda i: (i, 0)), pl.BlockSpec((1, d), lambda i: (0, 0))],
        out_specs=pl.BlockSpec((tm, d), lambda i: (i, 0)), out_shape=SDS((t, d), BF16), name=name,
        compiler_params=_params(("parallel",)),
    )(x, w.reshape(1, d))


def _rmsnorm_bwd(name, x, w, dh, dres):
    t, d = x.shape
    tm = min(512, t)

    def body(x_ref, w_ref, dh_ref, dres_ref, dx_ref, dxb_ref, dw_ref):
        xv = x_ref[...]
        r = lax.rsqrt(jnp.mean(xv * xv, axis=-1, keepdims=True) + EPS)
        xh = xv * r
        dhv = dh_ref[...].astype(F32)
        dxh = dhv * w_ref[...]
        dx = r * (dxh - xh * jnp.mean(dxh * xh, axis=-1, keepdims=True)) + dres_ref[...]
        dx_ref[...] = dx
        dxb_ref[...] = dx.astype(BF16)

        @pl.when(pl.program_id(0) == 0)
        def _():
            dw_ref[...] = jnp.zeros_like(dw_ref)

        dw_ref[...] += _colsum(dhv * xh)

    row = pl.BlockSpec((tm, d), lambda i: (i, 0))
    vec = pl.BlockSpec((1, d), lambda i: (0, 0))
    return pl.pallas_call(
        body, grid=(t // tm,), in_specs=[row, vec, row, row], out_specs=(row, row, vec),
        out_shape=(SDS((t, d), F32), SDS((t, d), BF16), SDS((1, d), F32)), name=name,
        compiler_params=_params(("arbitrary",)),
    )(x, w.reshape(1, d), dh, dres)


def _loss_head(name, x, w, tgt):
    t, d = x.shape
    tm = min(512, t)

    def body(x_ref, w_ref, t_ref, loss_ref, dx_ref, dxb_ref, dw_ref):
        xv = x_ref[...]
        wv = w_ref[...]
        r = lax.rsqrt(jnp.mean(xv * xv, axis=-1, keepdims=True) + EPS)
        xh = xv * r
        e = xh * wv - t_ref[...]
        dy = e * (1.0 / d)
        dxh = dy * wv
        dx = r * (dxh - xh * jnp.mean(dxh * xh, axis=-1, keepdims=True))
        dx_ref[...] = dx
        dxb_ref[...] = dx.astype(BF16)

        @pl.when(pl.program_id(0) == 0)
        def _():
            dw_ref[...] = jnp.zeros_like(dw_ref)
            loss_ref[...] = jnp.zeros_like(loss_ref)

        dw_ref[...] += _colsum(dy * xh)
        loss_ref[...] += 0.5 * jnp.sum(jnp.mean(e * e, axis=-1, keepdims=True), axis=0, keepdims=True)

    row = pl.BlockSpec((tm, d), lambda i: (i, 0))
    vec = pl.BlockSpec((1, d), lambda i: (0, 0))
    return pl.pallas_call(
        body, grid=(t // tm,), in_specs=[row, vec, row],
        out_specs=(pl.BlockSpec((1, 1), lambda i: (0, 0)), row, row, vec),
        out_shape=(SDS((1, 1), F32), SDS((t, d), F32), SDS((t, d), BF16), SDS((1, d), F32)), name=name,
        compiler_params=_params(("arbitrary",)),
    )(x, w.reshape(1, d), tgt)


def _shift_down(u, s, row):
    return jnp.where(row >= s, pltpu.roll(u, shift=s, axis=0), 0.0)


def _conv_fwd(name, src, col0, w, b):
    t = src.shape[0]
    c = w.shape[1]
    tc = 256
    assert c % tc == 0 and col0 % tc == 0

    def body(u_ref, w_ref, b_ref, o_ref):
        u = u_ref[...]
        wv = w_ref[...]
        row = lax.broadcasted_iota(jnp.int32, u.shape, 0)
        pre = b_ref[...] + wv[3:4, :] * u
        for s in range(1, CONV_K):
            pre = pre + wv[3 - s: 4 - s, :] * _shift_down(u, s, row)
        o_ref[...] = pre * _sigmoid(pre)

    return pl.pallas_call(
        body, grid=(c // tc,),
        in_specs=[pl.BlockSpec((t, tc), lambda j: (0, j + col0 // tc)), pl.BlockSpec((CONV_K, tc), lambda j: (0, j)),
                  pl.BlockSpec((1, tc), lambda j: (0, j))],
        out_specs=pl.BlockSpec((t, tc), lambda j: (0, j)), out_shape=SDS((t, c), F32), name=name,
        compiler_params=_params(("parallel",)),
    )(src, w, b)


def _place_small(name, dsm_a, dsm_b, dproj):
    t = dsm_a.shape[0]
    width = PROJ_W - SM_OFF
    tr = min(512, t)

    def body(a_ref, b_ref, dproj_ref, o_ref):
        o_ref[:, :128] = a_ref[...] + b_ref[...]
        o_ref[:, 128:] = jnp.zeros((tr, width - 128), BF16)

    row = pl.BlockSpec((tr, 128), lambda i: (i, 0))
    return pl.pallas_call(
        body, grid=(t // tr,), in_specs=[row, row, ANY],
        out_specs=pl.BlockSpec((tr, width), lambda i: (i, SM_OFF // width)), out_shape=SDS(dproj.shape, BF16),
        input_output_aliases={2: 0}, name=name, compiler_params=_params(("parallel",)),
    )(dsm_a, dsm_b, dproj)


def _conv_bwd(name, src, col0, w, b, dact, dproj):
    t = src.shape[0]
    c = w.shape[1]
    tc = 128

    def body(u_ref, w_ref, b_ref, da_ref, dproj_ref, du_ref, dw_ref, db_ref):
        u = u_ref[...]
        wv = w_ref[...]
        row = lax.broadcasted_iota(jnp.int32, u.shape, 0)
        shifted = [u] + [_shift_down(u, s, row) for s in range(1, CONV_K)]
        pre = b_ref[...] + wv[3:4, :] * u
        for s in range(1, CONV_K):
            pre = pre + wv[3 - s: 4 - s, :] * shifted[s]
        sg = _sigmoid(pre)
        dpre = da_ref[...] * (sg * (1.0 + pre * (1.0 - sg)))
        du = wv[3:4, :] * dpre
        for s in range(1, CONV_K):
            du = du + wv[3 - s: 4 - s, :] * jnp.where(row < t - s, pltpu.roll(dpre, shift=t - s, axis=0), 0.0)
        du_ref[...] = du.astype(BF16)
        for s in range(CONV_K):
            dw_ref[3 - s: 4 - s, :] = _colsum(dpre * shifted[s])
        db_ref[...] = _colsum(dpre)

    return pl.pallas_call(
        body, grid=(c // tc,),
        in_specs=[pl.BlockSpec((t, tc), lambda j: (0, j + col0 // tc)), pl.BlockSpec((CONV_K, tc), lambda j: (0, j)),
                  pl.BlockSpec((1, tc), lambda j: (0, j)), pl.BlockSpec((t, tc), lambda j: (0, j)), ANY],
        out_specs=(pl.BlockSpec((t, tc), lambda j: (0, j + col0 // tc)), pl.BlockSpec((CONV_K, tc), lambda j: (0, j)),
                   pl.BlockSpec((1, tc), lambda j: (0, j))),
        out_shape=(SDS(dproj.shape, BF16), SDS((CONV_K, c), F32), SDS((1, c), F32)), name=name,
        input_output_aliases={4: 0},
        compiler_params=_params(("parallel",)),
    )(src, w, b, dact, dproj)


def _tri(q):
    ii = lax.broadcasted_iota(jnp.int32, (q, q), 0)
    jj = lax.broadcasted_iota(jnp.int32, (q, q), 1)
    return ii, jj


def _dot01(x, r01, dims, terms=3):
    out, rem = None, x
    for i in range(terms):
        hi = rem.astype(BF16)
        d = lax.dot_general(hi, r01, (dims, ((), ())), preferred_element_type=F32)
        out = d if out is None else out + d
        if i + 1 < terms:
            rem = rem - hi.astype(F32)
    return out


def _ssd_common(act, sm, dtb, arow, rmat):
    q = CHUNK
    ii, jj = _tri(q)
    lane = lax.broadcasted_iota(jnp.int32, (q, 128), 1)
    m16 = lane < SSM_HEADS
    dt = jnp.where(m16, _softplus(sm + dtb), 0.0)
    a = dt * arow
    tril = (ii >= jj).astype(F32)
    triu = (ii <= jj).astype(F32)
    acum = _hdot(tril, a)
    acum_r = _hdot(a.T, triu)
    dtx = _dot01(dt, rmat, NN)
    acx = _dot01(acum, rmat, NN)
    ex = jnp.exp(acx)
    alx = acx[q - 1: q, :]
    dex = jnp.exp(alx - acx)
    xs = act[:, :1024]
    return dict(ii=ii, jj=jj, m16=m16, dt=dt, a=a, triu=triu, acum=acum, acum_r=acum_r, dtx=dtx, ex=ex, dex=dex,
                elx=jnp.exp(alx), xs=xs, x=xs * dtx)


def _ssd_lmat(cm, h):
    return jnp.where(cm["ii"] >= cm["jj"], jnp.exp(cm["acum"][:, h: h + 1] - cm["acum_r"][h: h + 1, :]), 0.0)


def _ssd_fwd(name, act, proj, dtb, arow, dxrow, nw, rmat):
    t = act.shape[0]
    q = CHUNK
    nc = t // q
    hg = SSM_HEADS // SSM_GROUPS
    gw = hg * SSM_P

    def body(act_ref, z_ref, sm_ref, dtb_ref, arow_ref, dx_ref, nw_ref, r_ref, y_ref, ys_ref, st_ref, s_scr, yd_scr):
        @pl.when(pl.program_id(0) == 0)
        def _():
            s_scr[...] = jnp.zeros_like(s_scr)

        s_all = s_scr[...]
        st_ref[0] = s_all
        actv = act_ref[...]
        cm = _ssd_common(actv, sm_ref[...], dtb_ref[...], arow_ref[...], r_ref[...])
        x = cm["x"]
        xd = x * cm["dex"]
        yoffs, snew = [], []
        for g in range(SSM_GROUPS):
            bg = actv[:, 1024 + g * SSM_N: 1024 + (g + 1) * SSM_N]
            cg = actv[:, 1280 + g * SSM_N: 1280 + (g + 1) * SSM_N]
            sg = s_all[:, g * gw: (g + 1) * gw]
            cb = _bdot(cg, bg, NT)
            yoffs.append(_bdot(cg, sg, NN))
            snew.append(_bdot(bg, xd[:, g * gw: (g + 1) * gw], TN))
            for r in range(hg):
                h = g * hg + r
                mm = cb * _ssd_lmat(cm, h)
                yd_scr[:, h * SSM_P: (h + 1) * SSM_P] = _bdot(mm, x[:, h * SSM_P: (h + 1) * SSM_P], NN)
        s_scr[...] = s_all * cm["elx"] + jnp.concatenate(snew, axis=1)
        ysc = yd_scr[...] + jnp.concatenate(yoffs, axis=1) * cm["ex"]
        ys_ref[...] = ysc
        zv = z_ref[...]
        yg = (ysc + dx_ref[...] * cm["xs"]) * (zv * _sigmoid(zv))
        nwv = nw_ref[...]
        for g in range(SSM_GROUPS):
            sl = yg[:, g * gw: (g + 1) * gw]
            rr = lax.rsqrt(jnp.mean(sl * sl, axis=-1, keepdims=True) + EPS)
            y_ref[:, g * gw: (g + 1) * gw] = (sl * rr * nwv[:, g * gw: (g + 1) * gw]).astype(BF16)

    vec128 = pl.BlockSpec((1, 128), lambda c: (0, 0))
    vec1k = pl.BlockSpec((1, 1024), lambda c: (0, 0))
    return pl.pallas_call(
        body, grid=(nc,),
        in_specs=[pl.BlockSpec((q, SSM_CONV), lambda c: (c, 0)), pl.BlockSpec((q, 1024), lambda c: (c, Z_OFF // 1024)),
                  pl.BlockSpec((q, 128), lambda c: (c, SM_OFF // 128)), vec128, vec128, vec1k, vec1k,
                  pl.BlockSpec((128, 1024), lambda c: (0, 0))],
        out_specs=(pl.BlockSpec((q, 1024), lambda c: (c, 0)), pl.BlockSpec((q, 1024), lambda c: (c, 0)),
                   pl.BlockSpec((1, 128, 1024), lambda c: (c, 0, 0))),
        out_shape=(SDS((t, 1024), BF16), SDS((t, 1024), F32), SDS((nc, 128, 1024), F32)),
        scratch_shapes=[pltpu.VMEM((128, 1024), F32), pltpu.VMEM((q, 1024), F32)], name=name,
        compiler_params=_params(("arbitrary",)),
    )(act, proj, proj, dtb, arow, dxrow, nw, rmat)


def _ssd_bwd(name, act, proj, dtb, arow, dxrow, nw, rmat, ysc, states, dy, dproj):
    t = act.shape[0]
    q = CHUNK
    nc = t // q
    hg = SSM_HEADS // SSM_GROUPS
    gw = hg * SSM_P

    def body(act_ref, z_ref, sm_ref, dtb_ref, arow_ref, dx_ref, nw_ref, r_ref, ys_ref, st_ref, dy_ref, dproj_ref,
             dact_ref, dz_ref, dsm_ref, dnw_ref, dd_ref, dal_ref, ddtb_ref, ds_scr, dxd_scr):
        @pl.when(pl.program_id(0) == 0)
        def _():
            ds_scr[...] = jnp.zeros_like(ds_scr)
            dnw_ref[...] = jnp.zeros_like(dnw_ref)
            dd_ref[...] = jnp.zeros_like(dd_ref)
            dal_ref[...] = jnp.zeros_like(dal_ref)
            ddtb_ref[...] = jnp.zeros_like(ddtb_ref)

        actv = act_ref[...]
        smv = sm_ref[...]
        rmat_v = r_ref[...]
        cm = _ssd_common(actv, smv, dtb_ref[...], arow_ref[...], rmat_v)
        ii, jj = cm["ii"], cm["jj"]
        x, xs = cm["x"], cm["xs"]
        s_all = st_ref[0]
        dsn = ds_scr[...]
        ysv = ys_ref[...]
        dxr = dx_ref[...]
        y = ysv + dxr * xs
        zv = z_ref[...]
        sz = _sigmoid(zv)
        silz = zv * sz
        yg = y * silz
        dout = dy_ref[...]
        nwv = nw_ref[...]
        dyn = dout * nwv
        yn_parts, dyg_parts = [], []
        for g in range(SSM_GROUPS):
            sl = yg[:, g * gw: (g + 1) * gw]
            rr = lax.rsqrt(jnp.mean(sl * sl, axis=-1, keepdims=True) + EPS)
            yn = sl * rr
            dn = dyn[:, g * gw: (g + 1) * gw]
            yn_parts.append(yn)
            dyg_parts.append(rr * (dn - yn * jnp.mean(dn * yn, axis=-1, keepdims=True)))
        dnw_ref[...] += _colsum(dout * jnp.concatenate(yn_parts, axis=1))
        dyg = jnp.concatenate(dyg_parts, axis=1)
        dyv = dyg * silz
        dz_ref[...] = (dyg * y * (sz * (1.0 + zv * (1.0 - sz)))).astype(BF16)
        dd_ref[...] += _dot01(_colsum(dyv * xs), rmat_v, NT)
        dxs = dyv * dxr
        dcs = dyv * cm["ex"]
        xd = x * cm["dex"]
        dxst_parts, ds_parts, db_parts, dc_parts, yoff_parts, wcol_rows = [], [], [], [], [], []
        lane128 = lax.broadcasted_iota(jnp.int32, (q, 128), 1)
        wrow = jnp.zeros((q, 128), F32)
        for g in range(SSM_GROUPS):
            bg = actv[:, 1024 + g * SSM_N: 1024 + (g + 1) * SSM_N]
            cg = actv[:, 1280 + g * SSM_N: 1280 + (g + 1) * SSM_N]
            sg = s_all[:, g * gw: (g + 1) * gw]
            dsng = dsn[:, g * gw: (g + 1) * gw]
            dcsg = dcs[:, g * gw: (g + 1) * gw]
            dcg = _bdot(dcsg, sg, NT)
            yoff_parts.append(_bdot(cg, sg, NN))
            ds_parts.append(_bdot(cg, dcsg, TN))
            dxst_parts.append(_bdot(bg, dsng, NN))
            dbg = _bdot(xd[:, g * gw: (g + 1) * gw], dsng, NT)
            cb = _bdot(cg, bg, NT)
            dcb = jnp.zeros((q, q), F32)
            for r in range(hg):
                h = g * hg + r
                lm = _ssd_lmat(cm, h)
                mm = cb * lm
                dyh = dyv[:, h * SSM_P: (h + 1) * SSM_P]
                dm = jnp.where(ii >= jj, _bdot(dyh, x[:, h * SSM_P: (h + 1) * SSM_P], NT), 0.0)
                dxd_scr[:, h * SSM_P: (h + 1) * SSM_P] = _bdot(mm, dyh, TN)
                dcb = dcb + dm * lm
                wm = dm * mm
                wrow = wrow + jnp.where(lane128 == h, _rowsum(wm), 0.0)
                wcol_rows.append(_colsum(wm))
            dc_parts.append(dcg + _bdot(dcb, bg, NN))
            db_parts.append(dbg + _bdot(dcb, cg, TN))
        dxst = jnp.concatenate(dxst_parts, axis=1) * cm["dex"]
        dx = dxd_scr[...] + dxst
        ds_scr[...] = jnp.concatenate(ds_parts, axis=1) + dsn * cm["elx"]
        wcol = jnp.concatenate(wcol_rows + [jnp.zeros((128 - SSM_HEADS, q), F32)], axis=0).T
        yoff = jnp.concatenate(yoff_parts, axis=1) * cm["ex"]
        xdxst = x * dxst
        dac = wrow - wcol + _dot01(dyv * yoff - xdxst, rmat_v, NT)
        last = _dot01(_colsum(dsn * s_all) * cm["elx"] + _colsum(xdxst), rmat_v, NT)
        rowq = lax.broadcasted_iota(jnp.int32, (q, 128), 0)
        dac = dac + jnp.where(rowq == q - 1, last, 0.0)
        da = _hdot(cm["triu"], dac)
        arow_v = arow_ref[...]
        ddt = da * arow_v + _dot01(dx * xs, rmat_v, NT)
        dxs = dxs + dx * cm["dtx"]
        dal_ref[...] += _colsum(da * cm["a"])
        ddtraw = jnp.where(cm["m16"], ddt * _sigmoid(smv + dtb_ref[...]), 0.0)
        ddtb_ref[...] += _colsum(ddtraw)
        dsm_ref[...] = ddtraw.astype(BF16)
        dact_ref[:, :1024] = dxs
        for g in range(SSM_GROUPS):
            dact_ref[:, 1024 + g * SSM_N: 1024 + (g + 1) * SSM_N] = db_parts[g]
            dact_ref[:, 1280 + g * SSM_N: 1280 + (g + 1) * SSM_N] = dc_parts[g]

    rev = lambda c: nc - 1 - c
    vec128 = pl.BlockSpec((1, 128), lambda c: (0, 0))
    vec1k = pl.BlockSpec((1, 1024), lambda c: (0, 0))
    return pl.pallas_call(
        body, grid=(nc,),
        in_specs=[pl.BlockSpec((q, SSM_CONV), lambda c: (rev(c), 0)),
                  pl.BlockSpec((q, 1024), lambda c: (rev(c), Z_OFF // 1024)),
                  pl.BlockSpec((q, 128), lambda c: (rev(c), SM_OFF // 128)), vec128, vec128, vec1k, vec1k,
                  pl.BlockSpec((128, 1024), lambda c: (0, 0)),
                  pl.BlockSpec((q, 1024), lambda c: (rev(c), 0)), pl.BlockSpec((1, 128, 1024), lambda c: (rev(c), 0, 0)),
                  pl.BlockSpec((q, 1024), lambda c: (rev(c), 0)), ANY],
        out_specs=(pl.BlockSpec((q, SSM_CONV), lambda c: (rev(c), 0)),
                   pl.BlockSpec((q, 1024), lambda c: (rev(c), Z_OFF // 1024)),
                   pl.BlockSpec((q, 128), lambda c: (rev(c), 0)), vec1k, vec128, vec128, vec128),
        out_shape=(SDS((t, SSM_CONV), F32), SDS(dproj.shape, BF16), SDS((t, 128), BF16), SDS((1, 1024), F32),
                   SDS((1, 128), F32), SDS((1, 128), F32), SDS((1, 128), F32)),
        input_output_aliases={11: 1},
        scratch_shapes=[pltpu.VMEM((128, 1024), F32), pltpu.VMEM((q, 1024), F32)], name=name,
        compiler_params=_params(("arbitrary",)),
    )(act, proj, proj, dtb, arow, dxrow, nw, rmat, ysc, states, dy, dproj)


def _split(a):
    hi = a.astype(BF16)
    return hi, (a - hi.astype(F32)).astype(BF16)


def _dot3(a, b, dims=NN):
    (ah, al), (bh, bl) = a, b

    def d(x, y):
        return lax.dot_general(x, y, (dims, ((), ())), preferred_element_type=F32)

    return d(ah, bh) + (d(ah, bl) + d(al, bh))


def _tri_inverses(amats, ii, jj):
    eye = jnp.where(ii == jj, 1.0, 0.0)
    tms = [eye - a for a in amats]
    sp = [_split(a) for a in amats]
    for _ in range(5):
        sp = [_split(_dot3(s, s)) for s in sp]
        tms = [t + _dot3(_split(t), s) for t, s in zip(tms, sp)]
    return tms


def _gdn_common(sm, gb, garow):
    q = CHUNK
    ii, jj = _tri(q)
    lane = lax.broadcasted_iota(jnp.int32, (q, 128), 1)
    ma = (lane >= LANE_A) & (lane < LANE_A + GDN_HEADS)
    spre = sm + gb
    g = jnp.where(ma, garow * _softplus(spre), 0.0)
    beta = _sigmoid(sm)
    tril = (ii >= jj).astype(F32)
    triu = (ii <= jj).astype(F32)
    gc = _hdot(tril, g)
    gc_r = _hdot(g.T, triu)
    return dict(ii=ii, jj=jj, lane=lane, ma=ma, spre=spre, g=g, beta=beta, triu=triu, gc=gc, gc_r=gc_r)


def _each(f, *lists):
    return [f(*xs) for xs in zip(*lists)]


GDN_SCALE = GDN_DK ** -0.5


def _gdn_heads(cm, actv, states, tm=None):
    q = CHUNK
    ii, jj = cm["ii"], cm["jj"]
    heads = range(GDN_HEADS)
    qr = [actv[:, h * 128: (h + 1) * 128] for h in heads]
    kr = [actv[:, 1024 + h * 128: 1024 + (h + 1) * 128] for h in heads]
    v = [actv[:, 2048 + h * 128: 2048 + (h + 1) * 128] for h in heads]
    rq = _each(lambda x: lax.rsqrt(_rowsum(x * x) + EPS), qr)
    rk = _each(lambda x: lax.rsqrt(_rowsum(x * x) + EPS), kr)
    qn = _each(lambda x, r: x * r * GDN_SCALE, qr, rq)
    kn = _each(lambda x, r: x * r, kr, rk)
    gcc = [cm["gc"][:, LANE_A + h: LANE_A + h + 1] for h in heads]
    gcr = [cm["gc_r"][LANE_A + h: LANE_A + h + 1, :] for h in heads]
    bcol = [cm["beta"][:, LANE_B + h: LANE_B + h + 1] for h in heads]
    dm = _each(lambda c, r: jnp.where(ii >= jj, jnp.exp(c - r), 0.0), gcc, gcr)
    kq = _each(lambda k, a: _bdot(jnp.concatenate([k, a], axis=0), k, NT), kn, qn)
    ak = _each(lambda x, d: jnp.where(ii > jj, x[:q] * d, 0.0), kq, dm)
    qkm = _each(lambda x, d: jnp.where(ii >= jj, x[q:] * d, 0.0), kq, dm)
    if tm is None:
        tm = _tri_inverses(_each(lambda a, b: a * b, ak, bcol), ii, jj)
    eg = _each(jnp.exp, gcc)
    gl = [c[q - 1: q, :] for c in gcc]
    rm = _each(lambda vv, k, b, e: jnp.concatenate([vv * b, k * (b * e)], axis=1), v, kn, bcol, eg)
    tt = _each(lambda t, r: _dot3(_split(t), _split(r)), tm, rm)
    w = [t[:, 128:] for t in tt]
    qg = _each(lambda a, e: a * e, qn, eg)
    ws = _each(lambda ww, a, s: _bdot(jnp.concatenate([ww, a], axis=0), s, NN), w, qg, states)
    vnew = _each(lambda t, x: t[:, :128] - x[:q], tt, ws)
    return dict(qr=qr, v=v, rq=rq, rk=rk, qn=qn, kn=kn, gcc=gcc, bcol=bcol, dm=dm, ak=ak, tm=tm, eg=eg, gl=gl,
                egl=_each(jnp.exp, gl), ed=_each(lambda g, c: jnp.exp(g - c), gl, gcc), tt=tt, w=w, vnew=vnew, qkm=qkm,
                qg=qg, qgs=[x[q:] for x in ws])


def _gdn_fwd(name, act, proj, gb, garow, gnw):
    t = act.shape[0]
    q = CHUNK
    nc = t // q

    def body(act_ref, gz_ref, sm_ref, gb_ref, ga_ref, nw_ref, y_ref, o_ref, st_ref, tm_ref, s_scr):
        @pl.when(pl.program_id(0) == 0)
        def _():
            s_scr[...] = jnp.zeros_like(s_scr)

        st_ref[0] = s_scr[...]
        actv = act_ref[...]
        cm = _gdn_common(sm_ref[...], gb_ref[...], ga_ref[...])
        nwv = nw_ref[...]
        gzv = gz_ref[...]
        states = [s_scr[h * 128: (h + 1) * 128, :] for h in range(GDN_HEADS)]
        hd = _gdn_heads(cm, actv, states)
        outs = _each(lambda qs, m, vn: qs + _bdot(m, vn, NN), hd["qgs"], hd["qkm"], hd["vnew"])
        snew = _each(lambda s, e, k, d, vn: s * e + _bdot(k * d, vn, TN), states, hd["egl"], hd["kn"], hd["ed"], hd["vnew"])
        for h in range(GDN_HEADS):
            o = outs[h]
            s_scr[h * 128: (h + 1) * 128, :] = snew[h]
            tm_ref[:, h * q: (h + 1) * q] = hd["tm"][h]
            o_ref[:, h * 128: (h + 1) * 128] = o
            rr = lax.rsqrt(jnp.mean(o * o, axis=-1, keepdims=True) + EPS)
            gz = gzv[:, h * 128: (h + 1) * 128]
            y_ref[:, h * 128: (h + 1) * 128] = (o * rr * nwv * (gz * _sigmoid(gz))).astype(BF16)

    vec128 = pl.BlockSpec((1, 128), lambda c: (0, 0))
    return pl.pallas_call(
        body, grid=(nc,),
        in_specs=[pl.BlockSpec((q, GDN_QKV), lambda c: (c, 0)), pl.BlockSpec((q, 1024), lambda c: (c, GZ_OFF // 1024)),
                  pl.BlockSpec((q, 128), lambda c: (c, SM_OFF // 128)), vec128, vec128, vec128],
        out_specs=(pl.BlockSpec((q, 1024), lambda c: (c, 0)), pl.BlockSpec((q, 1024), lambda c: (c, 0)),
                   pl.BlockSpec((1, 1024, 128), lambda c: (c, 0, 0)), pl.BlockSpec((q, GDN_HEADS * q), lambda c: (c, 0))),
        out_shape=(SDS((t, 1024), BF16), SDS((t, 1024), F32), SDS((nc, 1024, 128), F32), SDS((t, GDN_HEADS * q), F32)),
        scratch_shapes=[pltpu.VMEM((1024, 128), F32)], name=name, compiler_params=_params(("arbitrary",)),
    )(act, proj, proj, gb, garow, gnw)


def _gdn_bwd(name, act, proj, gb, garow, gnw, oraw, states, dy, tms, dproj):
    t = act.shape[0]
    q = CHUNK
    nc = t // q

    def body(act_ref, gz_ref, sm_ref, gb_ref, ga_ref, nw_ref, o_ref, st_ref, dy_ref, tm_ref, dproj_ref,
             dact_ref, dgz_ref, dsm_ref, dnw_ref, dal_ref, dgb_ref, ds_scr):
        @pl.when(pl.program_id(0) == 0)
        def _():
            ds_scr[...] = jnp.zeros_like(ds_scr)
            dnw_ref[...] = jnp.zeros_like(dnw_ref)
            dal_ref[...] = jnp.zeros_like(dal_ref)
            dgb_ref[...] = jnp.zeros_like(dgb_ref)

        actv = act_ref[...]
        smv = sm_ref[...]
        garow_v = ga_ref[...]
        cm = _gdn_common(smv, gb_ref[...], garow_v)
        ii, jj, lane = cm["ii"], cm["jj"], cm["lane"]
        nwv = nw_ref[...]
        rowq = lax.broadcasted_iota(jnp.int32, (q, 1), 0)
        dgc_all = jnp.zeros((q, 128), F32)
        dbeta_all = jnp.zeros((q, 128), F32)
        dnw_acc = jnp.zeros((1, 128), F32)
        heads = range(GDN_HEADS)
        sts = [st_ref[0, h * 128: (h + 1) * 128, :] for h in heads]
        dsn = [ds_scr[h * 128: (h + 1) * 128, :] for h in heads]
        ov, gzv, dyv = o_ref[...], gz_ref[...], dy_ref[...]
        hd = _gdn_heads(cm, actv, sts, tm=[tm_ref[:, h * q: (h + 1) * q] for h in heads])
        qn, kn, v, eg, ed, egl, bcol = hd["qn"], hd["kn"], hd["v"], hd["eg"], hd["ed"], hd["egl"], hd["bcol"]
        vnew, qkm, qg, w, tt, dm, ak = hd["vnew"], hd["qkm"], hd["qg"], hd["w"], hd["tt"], hd["dm"], hd["ak"]
        do = []
        for h in heads:
            hs = slice(h * 128, (h + 1) * 128)
            o = ov[:, hs]
            rr = lax.rsqrt(jnp.mean(o * o, axis=-1, keepdims=True) + EPS)
            on = o * rr
            gz = gzv[:, hs]
            sz = _sigmoid(gz)
            silz = gz * sz
            dyh = dyv[:, hs]
            dnw_acc = dnw_acc + _colsum(dyh * on * silz)
            dgz_ref[:, hs] = (dyh * on * nwv * (sz * (1.0 + gz * (1.0 - sz)))).astype(BF16)
            don = dyh * nwv * silz
            do.append(rr * (don - on * jnp.mean(don * on, axis=-1, keepdims=True)))
        kd = _each(lambda k, e: k * e, kn, ed)
        dkd = _each(lambda vn, d: _bdot(vn, d, NT), vnew, dsn)
        dvnew_a = _each(lambda k, d: _bdot(k, d, NN), kd, dsn)
        ded = _each(lambda a, b: _rowsum(a * b), dkd, kd)
        dgl = _each(lambda d, s, e, de: jnp.sum(_rowsum(d * s), axis=0, keepdims=True) * e + _colsum(de), dsn, sts, egl, ded)
        dqk = _each(lambda d, vn: jnp.where(ii >= jj, _bdot(d, vn, NT), 0.0), do, vnew)
        dvnew = _each(lambda a, m, d: a + _bdot(m, d, TN), dvnew_a, qkm, do)
        pq = _each(lambda a, b: a * b, dqk, dm)
        w1 = _each(lambda a, b: a * b, dqk, qkm)
        dod = _each(lambda a, b: jnp.concatenate([a, b], axis=0), do, dvnew)
        dos = _each(lambda x, s: _bdot(x, s, NT), dod, sts)
        dqg = [x[:q] for x in dos]
        dw = [-x[q:] for x in dos]
        ds12 = _each(lambda a, ww, x: _bdot(jnp.concatenate([a, -ww], axis=0), x, TN), qg, w, dod)
        dr = _each(lambda t, a, b: _dot3(_split(t), _split(jnp.concatenate([a, b], axis=1)), TN), hd["tm"], dvnew, dw)
        da = _each(lambda r, t: jnp.where(ii > jj, -_dot3(_split(r), _split(t), NT), 0.0), dr, tt)
        sk = _each(lambda r, k: _rowsum(r[:, 128:] * k), dr, kn)
        pk = _each(lambda a, d, b: a * d * b, da, dm, bcol)
        pkn = _each(lambda p, pp, k: _bdot(jnp.concatenate([p, pp + pp.T], axis=0), k, NN), pq, pk, kn)
        dq = _each(lambda a, e, x: a * e + x[:q], dqg, eg, pkn)
        dk = _each(lambda a, e, p, x, r, b, eg_, y: a * e + _bdot(p, x, TN) + r[:, 128:] * (b * eg_) + y[q:],
                   dkd, ed, pq, qn, dr, bcol, eg, pkn)
        w2 = _each(lambda a, k, b: a * (k * b), da, ak, bcol)
        for h in heads:
            hs = slice(h * 128, (h + 1) * 128)
            dgc = (-ded[h] + _rowsum(dqg[h] * qg[h]) + _rowsum(w1[h]) - _rowsum(w1[h].T) + sk[h] * bcol[h] * eg[h]
                   + _rowsum(w2[h]) - _rowsum(w2[h].T) + jnp.where(rowq == q - 1, dgl[h], 0.0))
            dbeta = _rowsum(dr[h][:, :128] * v[h]) + sk[h] * eg[h] + _rowsum(da[h] * ak[h])
            qhat = hd["qr"][h] * hd["rq"][h]
            dqhat = dq[h] * GDN_SCALE
            dact_ref[:, hs] = hd["rq"][h] * (dqhat - qhat * _rowsum(dqhat * qhat))
            dact_ref[:, 1024 + h * 128: 1024 + (h + 1) * 128] = hd["rk"][h] * (dk[h] - kn[h] * _rowsum(dk[h] * kn[h]))
            dact_ref[:, 2048 + h * 128: 2048 + (h + 1) * 128] = dr[h][:, :128] * bcol[h]
            dgc_all = dgc_all + jnp.where(lane == LANE_A + h, dgc, 0.0)
            dbeta_all = dbeta_all + jnp.where(lane == LANE_B + h, dbeta, 0.0)
            ds_scr[hs, :] = dsn[h] * egl[h] + ds12[h]
        dnw_ref[...] += dnw_acc
        dg = _hdot(cm["triu"], dgc_all)
        da_raw = jnp.where(cm["ma"], dg * garow_v * _sigmoid(cm["spre"]), 0.0)
        dal_ref[...] += _colsum(dg * cm["g"])
        dgb_ref[...] += _colsum(da_raw)
        beta = cm["beta"]
        dsm_ref[...] = (da_raw + dbeta_all * beta * (1.0 - beta)).astype(BF16)

    rev = lambda c: nc - 1 - c
    vec128 = pl.BlockSpec((1, 128), lambda c: (0, 0))
    return pl.pallas_call(
        body, grid=(nc,),
        in_specs=[pl.BlockSpec((q, GDN_QKV), lambda c: (rev(c), 0)),
                  pl.BlockSpec((q, 1024), lambda c: (rev(c), GZ_OFF // 1024)),
                  pl.BlockSpec((q, 128), lambda c: (rev(c), SM_OFF // 128)), vec128, vec128, vec128,
                  pl.BlockSpec((q, 1024), lambda c: (rev(c), 0)), pl.BlockSpec((1, 1024, 128), lambda c: (rev(c), 0, 0)),
                  pl.BlockSpec((q, 1024), lambda c: (rev(c), 0)), pl.BlockSpec((q, GDN_HEADS * q), lambda c: (rev(c), 0)),
                  ANY],
        out_specs=(pl.BlockSpec((q, GDN_QKV), lambda c: (rev(c), 0)),
                   pl.BlockSpec((q, 1024), lambda c: (rev(c), GZ_OFF // 1024)),
                   pl.BlockSpec((q, 128), lambda c: (rev(c), 0)), vec128, vec128, vec128),
        out_shape=(SDS((t, GDN_QKV), F32), SDS(dproj.shape, BF16), SDS((t, 128), BF16), SDS((1, 128), F32),
                   SDS((1, 128), F32), SDS((1, 128), F32)),
        input_output_aliases={10: 1},
        scratch_shapes=[pltpu.VMEM((1024, 128), F32)], name=name, compiler_params=_params(("arbitrary",)),
    )(act, proj, proj, gb, garow, gnw, oraw, states, dy, tms, dproj)


def _row_tile(r):
    for cand in (512, 256, 128, 64, 32, 16, 8):
        if r % cand == 0:
            return cand
    return r


def _sum_terms(name, terms, out_dtype):
    shape = terms[0][0].shape[1:]
    c = shape[-1]
    r = 1
    for s in shape[:-1]:
        r *= s
    tr = min(_row_tile(r), 256)
    n = len(terms)

    def body(*refs):
        acc = refs[0][...].astype(F32)
        for k in range(1, n):
            acc = acc + refs[k][...].astype(F32)
        refs[n][...] = acc.astype(out_dtype)

    in_specs = [pl.BlockSpec((None, tr, c), lambda i, q=lead: (q, i, 0)) for _, lead in terms]
    args = [a.reshape(a.shape[0], r, c) for a, _ in terms]
    out = pl.pallas_call(body, grid=(r // tr,), in_specs=in_specs, out_specs=pl.BlockSpec((tr, c), lambda i: (i, 0)),
                         out_shape=SDS((r, c), out_dtype), name=name, compiler_params=_params(("parallel",)))(*args)
    return out.reshape(shape)


def _adamw(name, w, g, m, v):
    shape = w.shape
    c = shape[-1]
    per_layer = isinstance(g, (list, tuple))
    nl = len(g) if per_layer else 1
    gs = [a.reshape(-1, c) for a in g] if per_layer else [g.reshape(-1, c)]
    r = gs[0].shape[0]
    w3, m3, v3 = (a.reshape(nl, r, c) for a in (w, m, v))
    tr = min(_row_tile(r), 256)

    def body(*refs):
        w_ref, m_ref, v_ref = refs[:3]
        g_refs = refs[3: 3 + nl]
        go_ref, d_ref, nm_ref, nv_ref = refs[3 + nl:]
        layer = pl.program_id(0)
        gv = g_refs[0][...]
        for k in range(1, nl):
            gv = jnp.where(layer == k, g_refs[k][...], gv)
        mn = ADAM_B1 * m_ref[...] + (1.0 - ADAM_B1) * gv
        vn = ADAM_B2 * v_ref[...] + (1.0 - ADAM_B2) * (gv * gv)
        m_hat = mn / (1.0 - ADAM_B1 ** ADAM_STEP)
        v_hat = vn / (1.0 - ADAM_B2 ** ADAM_STEP)
        go_ref[...] = gv
        d_ref[...] = -ADAM_LR * (m_hat / (jnp.sqrt(v_hat) + ADAM_EPS) + ADAM_WD * w_ref[...])
        nm_ref[...] = mn
        nv_ref[...] = vn

    spec3 = pl.BlockSpec((None, tr, c), lambda l, i: (l, i, 0))
    gspec = pl.BlockSpec((tr, c), lambda l, i: (i, 0))
    outs = pl.pallas_call(body, grid=(nl, r // tr), in_specs=[spec3] * 3 + [gspec] * nl, out_specs=(spec3,) * 4,
                          out_shape=(SDS((nl, r, c), F32),) * 4, name=name,
                          compiler_params=_params(("parallel", "parallel")))(w3, m3, v3, *gs)
    return tuple(o.reshape(shape) for o in outs)


ANY = pl.BlockSpec(memory_space=pl.ANY)
MESH = pl.DeviceIdType.MESH


def _allgather(name, xs):
    n = len(xs)

    def body(*refs):
        x_refs, out_refs = refs[:n], refs[n: 2 * n]
        send_sems, recv_sems, local_sems = refs[2 * n:]
        x, y, cc = lax.axis_index("x"), lax.axis_index("y"), lax.axis_index("c")
        me, sibling = (x, y, cc), (x, y, 1 - cc)
        chips = [(1 - x, y), (x, 1 - y), (1 - x, 1 - y)]

        def rows(a, px, py, pc):
            return out_refs[a].at[4 * px + 2 * py + pc]

        def copy(a, k, block, to, src=None):
            return pltpu.make_async_remote_copy(
                src_ref=rows(a, *block) if src is None else src, dst_ref=rows(a, *block),
                send_sem=send_sems.at[7 * a + k], recv_sem=recv_sems.at[7 * a + k], device_id=to, device_id_type=MESH)

        mine = [pltpu.make_async_copy(x_refs[a], rows(a, *me), local_sems.at[a]) for a in range(n)]
        for cp in mine:
            cp.start()
        first = []
        for a in range(n):
            first.append(copy(a, 0, me, sibling, src=x_refs[a]))
            first += [copy(a, 1 + j, me, (*chip, cc), src=x_refs[a]) for j, chip in enumerate(chips)]
        for cp in first:
            cp.start()
        passed = []
        for j, chip in enumerate(chips):
            for a in range(n):
                copy(a, 1 + j, (*chip, cc), me).wait_recv()
                fwd = copy(a, 4 + j, (*chip, cc), sibling)
                fwd.start()
                passed.append(fwd)
        for a in range(n):
            copy(a, 0, sibling, me).wait_recv()
        for j, chip in enumerate(chips):
            for a in range(n):
                copy(a, 4 + j, (*chip, 1 - cc), me).wait_recv()
        for cp in first + passed:
            cp.wait_send()
        for cp in mine:
            cp.wait()

    return pl.pallas_call(
        body, out_shape=tuple(SDS((N_DEV,) + a.shape, a.dtype) for a in xs), in_specs=[ANY] * n, out_specs=(ANY,) * n,
        scratch_shapes=[pltpu.SemaphoreType.DMA((7 * n,)), pltpu.SemaphoreType.DMA((7 * n,)),
                        pltpu.SemaphoreType.DMA((n,))],
        name=name,
    )(*xs)


def _allgather_seq(name, xs, collective_id):
    n = len(xs)
    x_refs = [jax.new_ref(a, memory_space=pltpu.MemorySpace.HBM) for a in xs]
    out_refs = [jax.empty_ref(SDS((N_DEV,) + a.shape, a.dtype), memory_space=pltpu.MemorySpace.HBM) for a in xs]

    @pl.kernel(mesh=plsc.ScalarSubcoreMesh(axis_name="seq", num_cores=1), name=name,
               scratch_types=(pltpu.SemaphoreType.DMA((7 * n,)), pltpu.SemaphoreType.DMA((7 * n,)),
                              pltpu.SemaphoreType.DMA((n,))),
               compiler_params=pltpu.CompilerParams(collective_id=collective_id))
    def launch(send_sems, recv_sems, local_sems):
        x, y, cc = lax.axis_index("x"), lax.axis_index("y"), lax.axis_index("c")
        me, sibling = (x, y, cc), (x, y, 1 - cc)
        chips = [(1 - x, y), (x, 1 - y), (1 - x, 1 - y)]
        barrier = pltpu.get_barrier_semaphore()
        for peer in [sibling] + [(*chip, cc) for chip in chips]:
            pl.semaphore_signal(barrier, inc=1, device_id=peer, device_id_type=MESH)
        pl.semaphore_wait(barrier, 4)

        def rows(a, px, py, pc):
            return out_refs[a].at[4 * px + 2 * py + pc]

        def copy(a, k, block, to, src=None):
            return pltpu.make_async_remote_copy(
                src_ref=rows(a, *block) if src is None else src, dst_ref=rows(a, *block),
                send_sem=send_sems.at[7 * a + k], recv_sem=recv_sems.at[7 * a + k], device_id=to, device_id_type=MESH)

        mine = [pltpu.make_async_copy(x_refs[a], rows(a, *me), local_sems.at[a]) for a in range(n)]
        for cp in mine:
            cp.start()
        first = []
        for a in range(n):
            first.append(copy(a, 0, me, sibling, src=x_refs[a]))
            first += [copy(a, 1 + j, me, (*chip, cc), src=x_refs[a]) for j, chip in enumerate(chips)]
        for cp in first:
            cp.start()
        passed = []
        for j, chip in enumerate(chips):
            for a in range(n):
                copy(a, 1 + j, (*chip, cc), me).wait_recv()
                fwd = copy(a, 4 + j, (*chip, cc), sibling)
                fwd.start()
                passed.append(fwd)
        for a in range(n):
            copy(a, 0, sibling, me).wait_recv()
        for j, chip in enumerate(chips):
            for a in range(n):
                copy(a, 4 + j, (*chip, 1 - cc), me).wait_recv()
        for cp in first + passed:
            cp.wait_send()
        for cp in mine:
            cp.wait()

    launch()
    return [r[...] for r in out_refs]


HBM = pl.BlockSpec(memory_space=pltpu.HBM)
SEM = pl.BlockSpec(memory_space=pltpu.SEMAPHORE)
EFFECT = pltpu.SideEffectType.DATAFLOW_SIDE_EFFECTING


def _sibling_plan(srcs, lands, send_sems, recv_sems):
    x, y, cc = lax.axis_index("x"), lax.axis_index("y"), lax.axis_index("c")
    return [pltpu.make_async_remote_copy(
        src_ref=srcs[a].at[2 * q + 1 - cc], dst_ref=lands[a].at[q], send_sem=send_sems.at[4 * a + q],
        recv_sem=recv_sems.at[4 * a + q], device_id=(x, y, 1 - cc), device_id_type=MESH)
        for a in range(len(srcs)) for q in range(4)]


def _chips_plan(srcs, lands, send_sems, recv_sems):
    x, y, cc = lax.axis_index("x"), lax.axis_index("y"), lax.axis_index("c")
    chips = [(1 - x, y), (x, 1 - y), (1 - x, 1 - y)]
    return [pltpu.make_async_remote_copy(
        src_ref=srcs[a].at[2 * px + py], dst_ref=lands[a].at[j], send_sem=send_sems.at[3 * a + j],
        recv_sem=recv_sems.at[3 * a + j], device_id=(px, py, cc), device_id_type=MESH)
        for a in range(len(srcs)) for j, (px, py) in enumerate(chips)]


def _copies_start(name, plan, per_array, srcs, land_lead):
    n = len(srcs)
    k = per_array * n

    def body(*refs):
        src_refs, land_refs = refs[:n], refs[n: 2 * n]
        send_sems, recv_sems = refs[2 * n], refs[2 * n + 1]
        token = refs[-1]
        for cp in plan(src_refs, land_refs, send_sems, recv_sems):
            cp.start()
        token[...] = jnp.zeros_like(token)

    lands = [lax.empty((land_lead,) + a.shape[1:], a.dtype) for a in srcs]
    outs = pl.pallas_call(
        body, name=name,
        out_shape=(pltpu.SemaphoreType.DMA((k,)), pltpu.SemaphoreType.DMA((k,)),
                   *[pltpu.HBM(a.shape, a.dtype) for a in srcs], *[pltpu.HBM(a.shape, a.dtype) for a in lands],
                   SDS((8, 128), F32)),
        in_specs=[HBM] * (2 * n), out_specs=(SEM, SEM, *[HBM] * (2 * n), pl.BlockSpec(memory_space=pltpu.VMEM)),
        input_output_aliases={i: 2 + i for i in range(2 * n)},
        compiler_params=pltpu.CompilerParams(has_side_effects=EFFECT),
    )(*[pltpu.with_memory_space_constraint(a, pltpu.HBM) for a in srcs],
      *[pltpu.with_memory_space_constraint(a, pltpu.HBM) for a in lands])
    return outs[0], outs[1], list(outs[2: 2 + n]), list(outs[2 + n: 2 + 2 * n]), outs[-1]


def _copies_wait(name, plan, started, after):
    send_sems, recv_sems, srcs, lands, _ = started
    n = len(srcs)
    after = tuple(after)

    def body(*refs):
        src_refs, land_refs = refs[:n], refs[n: 2 * n]
        for cp in plan(src_refs, land_refs, refs[2 * n], refs[2 * n + 1]):
            cp.wait_send()
            cp.wait_recv()

    outs = pl.pallas_call(
        body, name=name,
        out_shape=tuple(pltpu.HBM(a.shape, a.dtype) for a in srcs + lands),
        in_specs=[HBM] * (2 * n) + [SEM, SEM] + [ANY] * len(after), out_specs=(HBM,) * (2 * n),
        input_output_aliases={i: i for i in range(2 * n)},
        compiler_params=pltpu.CompilerParams(has_side_effects=EFFECT),
    )(*srcs, *lands, send_sems, recv_sems, *after)
    return list(outs[n:])


BIG = (("w_in", 1), ("w_ffn_in", 1), ("w_proj_ssm", 0), ("w_proj_gdn", 0), ("w_out", 0), ("w_ffn_down", 0))
CONVS = (("ssm_conv_w", 1), ("gdn_conv_w", 1))


def _to_dest_major(full, axis):
    a, b = full.shape
    if axis == 0:
        return full.reshape(N_DEV, a // N_DEV, b)
    s = b // N_DEV
    return jnp.stack([full[:, d * s: (d + 1) * s] for d in range(N_DEV)])


def _from_gathered(g, axis):
    if axis == 0:
        return g.reshape(-1, g.shape[2])
    return jnp.concatenate([g[d] for d in range(N_DEV)], axis=1)


IN_RUNS = ((Z_OFF, O_Z, 1024), (GZ_OFF, O_GZ, 1024), (G1_OFF, O_G1, 1024), (G2_OFF, O_G2, 1024), (QKV_OFF, O_QKV, 3072),
           (XBC_OFF, O_XBC, 1536), (SM_OFF, O_DT, 16), (SM_OFF + LANE_A, O_A, 8), (SM_OFF + LANE_B, O_B, 8))
IN_SHARD = IN_DIM // N_DEV


def _w_in_from_blocks(g):
    rows = g.shape[1]
    parts, pos = [], 0
    for off, o0, width in IN_RUNS:
        if off > pos:
            parts.append(jnp.zeros((rows, off - pos), g.dtype))
        c = o0
        while c < o0 + width:
            d = c // IN_SHARD
            hi = min(o0 + width, (d + 1) * IN_SHARD)
            parts.append(g[d][:, c - d * IN_SHARD: hi - d * IN_SHARD])
            c = hi
        pos = off + width
    parts.append(jnp.zeros((rows, PROJ_W - pos), g.dtype))
    return jnp.concatenate(parts, axis=1)


def _w_in_to_blocks(wp):
    by_orig = sorted(IN_RUNS, key=lambda r: r[1])
    blocks = []
    for d in range(N_DEV):
        lo, hi = d * IN_SHARD, (d + 1) * IN_SHARD
        parts = []
        for off, o0, width in by_orig:
            a, b = max(lo, o0), min(hi, o0 + width)
            if a < b:
                parts.append(wp[:, off + a - o0: off + b - o0])
        blocks.append(jnp.concatenate(parts, axis=1))
    return jnp.stack(blocks)


def _pad128(v, lane0):
    return jnp.zeros((1, 128), F32).at[0, lane0: lane0 + v.shape[0]].set(v)


def _layer_consts(p):
    return dict(
        dtb=_pad128(p["ssm_dt_bias"], 0), arow=_pad128(-jnp.exp(p["ssm_a_log"]), 0),
        dxrow=jnp.repeat(p["ssm_d"], SSM_P).reshape(1, 1024), snw=p["ssm_norm_w"].reshape(1, 1024),
        gb=_pad128(p["gdn_dt_bias"], LANE_A), garow=_pad128(-jnp.exp(p["gdn_a_log"]), LANE_A),
        gnw=p["gdn_norm_w"].reshape(1, 128), zb=jnp.zeros((1, GDN_QKV), F32), scb=p["ssm_conv_b"].reshape(1, SSM_CONV))


def _expand_matrix():
    row = lax.broadcasted_iota(jnp.int32, (128, 1024), 0)
    col = lax.broadcasted_iota(jnp.int32, (128, 1024), 1)
    return (col // SSM_P == row).astype(BF16)


def _silu_mul_epi(acc, up):
    g = acc
    return g, g * _sigmoid(g) * up.astype(F32)


def _merge_epi(acc, p1, g1, g2):
    return acc, _sigmoid(g1) * p1.astype(F32) + _sigmoid(g2) * acc


def _add_epi(acc, res):
    return (acc + res,)


def _ffn_bwd_epi(acc, gate, up):
    g = gate.astype(F32)
    sg = _sigmoid(g)
    return acc * up.astype(F32) * (sg * (1.0 + g * (1.0 - sg))), acc * (g * sg)


def _merge_bwd_epi(acc, g1, g2, p1, p2):
    s1, s2 = _sigmoid(g1), _sigmoid(g2)
    dg1, dg2 = acc * p1.astype(F32) * (s1 * (1.0 - s1)), acc * p2.astype(F32) * (s2 * (1.0 - s2))
    return acc * s1, acc * s2, jnp.concatenate([dg1, dg2], axis=1)


def _layer_fwd(l, x, p, rmat):
    t = x.shape[0]
    n = f"l{l}_"
    k = _layer_consts(p)
    h = _rmsnorm_fwd(n + "norm_mix", x, p["norm_mix_w"])
    proj = _matmul(n + "in_proj", "nn", [(h, 0, p["w_in"], 0)], t, PROJ_W, 1024, 1024, 1280, 1024, (F32,))
    act_g = _conv_fwd(n + "conv_gdn", proj, QKV_OFF, p["gdn_conv_w"], k["zb"])
    act_s = _conv_fwd(n + "conv_ssm", proj, XBC_OFF, p["ssm_conv_w"], k["scb"])
    y_ssm, ysc, st_s = _ssd_fwd(n + "ssd_fwd", act_s, proj, k["dtb"], k["arow"], k["dxrow"], k["snw"], rmat)
    y_gdn, oraw, st_g, tms = _gdn_fwd(n + "gdn_fwd", act_g, proj, k["gb"], k["garow"], k["gnw"])
    if "late" in p:
        y_gdn, late = p["late"](y_gdn)
        p = {**p, **late}
    p1 = _matmul(n + "proj_ssm", "nn", [(y_ssm, 0, p["w_proj_ssm"], 0)], t, 1024, 1024, 1024, 1024, 1024, (BF16,))
    p2, merged = _matmul(n + "proj_gdn_merge", "nn", [(y_gdn, 0, p["w_proj_gdn"], 0)], t, 1024, 1024, 512, 1024, 1024,
                         (BF16, BF16), epi=_merge_epi, extras=[(p1, 0), (proj, G1_OFF // 1024), (proj, G2_OFF // 1024)])
    x1 = _matmul(n + "out_proj", "nn", [(merged, 0, p["w_out"], 0)], t, 1024, 1024, 1024, 1024, 1024, (F32,),
                 epi=_add_epi, extras=[(x, 0)])
    h2 = _rmsnorm_fwd(n + "norm_ffn", x1, p["norm_ffn_w"])
    up = _matmul(n + "ffn_up", "nn", [(h2, 0, p["w_ffn_in"], 2)], t, FFN, 1024, 1024, FFN // 2, 1024, (BF16,))
    gate, act = _matmul(n + "ffn_gate", "nn", [(h2, 0, p["w_ffn_in"], 0)], t, FFN, 1024, 1024, FFN // 2, 1024, (BF16, BF16),
                        epi=_silu_mul_epi, extras=[(up, 0)])
    x2 = _matmul(n + "ffn_down", "nn", [(act, 0, p["w_ffn_down"], 0)], t, 1024, FFN, 1024, 1024, FFN, (F32,),
                 epi=_add_epi, extras=[(x1, 0)])
    saved = dict(x=x, h=h, proj=proj, act_g=act_g, act_s=act_s, y_ssm=y_ssm, ysc=ysc, st_s=st_s, y_gdn=y_gdn, oraw=oraw,
                 st_g=st_g, tms=tms, p1=p1, p2=p2, merged=merged, x1=x1, h2=h2, up=up, gate=gate, act=act, k=k, p=p)
    return x2, saved


def _layer_bwd(l, dx2, dx2b, s, p, rmat, hooks):
    t = dx2.shape[0]
    n = f"l{l}_"
    k = s["k"]
    tk_tok = 1024
    hf = FFN // 2
    g = {}
    dgate, dup = _matmul(n + "d_ffn_act", "nt", [(dx2b, 0, p["w_ffn_down"], 0)], t, FFN, 1024, 1024, hf, 1024, (BF16, BF16),
                         epi=_ffn_bwd_epi, extras=[(s["gate"], 0), (s["up"], 0)])
    g["w_ffn_down"] = _matmul(n + "dw_ffn_down", "tn", [(s["act"], 0, dx2b, 0)], FFN, 1024, t, hf, 1024, tk_tok, (BF16,))
    dh2 = _matmul(n + "d_ffn_in", "nt", [(dgate, 0, p["w_ffn_in"], 0), (dup, 0, p["w_ffn_in"], 2)], t, 1024, FFN,
                  1024, 1024, hf, (F32,))
    dwg = _matmul(n + "dw_ffn_gate", "tn", [(s["h2"], 0, dgate, 0)], 1024, FFN, t, 1024, hf, tk_tok, (BF16,))
    dwu = _matmul(n + "dw_ffn_up", "tn", [(s["h2"], 0, dup, 0)], 1024, FFN, t, 1024, hf, tk_tok, (BF16,))
    g["w_ffn_in"] = jnp.concatenate([dwg, dwu], axis=1)
    dx1, dx1b, g["norm_ffn_w"] = _rmsnorm_bwd(n + "d_norm_ffn", s["x1"], p["norm_ffn_w"], dh2, dx2)
    dx1b = hooks.ffn_done(dx1b)
    dp1, dp2, dproj = _matmul(
        n + "d_out_proj", "nt", [(dx1b, 0, p["w_out"], 0)], t, 1024, 1024, 512, 1024, 1024,
        (BF16, BF16, (BF16, PROJ_W, 2048, G1_OFF // 2048)), epi=_merge_bwd_epi,
        extras=[(s["proj"], G1_OFF // 1024), (s["proj"], G2_OFF // 1024), (s["p1"], 0), (s["p2"], 0)])
    g["w_out"] = _matmul(n + "dw_out", "tn", [(s["merged"], 0, dx1b, 0)], 1024, 1024, t, 1024, 1024, tk_tok, (BF16,))
    g["w_proj_ssm"] = _matmul(n + "dw_proj_ssm", "tn", [(s["y_ssm"], 0, dp1, 0)], 1024, 1024, t, 1024, 1024, tk_tok, (BF16,))
    g["w_proj_gdn"] = _matmul(n + "dw_proj_gdn", "tn", [(s["y_gdn"], 0, dp2, 0)], 1024, 1024, t, 1024, 1024, tk_tok, (BF16,))
    dp1, dp2 = hooks.early_ready(l, g, dp1, dp2)
    dy_ssm = _matmul(n + "d_proj_ssm", "nt", [(dp1, 0, p["w_proj_ssm"], 0)], t, 1024, 1024, 1024, 1024, 1024, (F32,))
    dy_gdn = _matmul(n + "d_proj_gdn", "nt", [(dp2, 0, p["w_proj_gdn"], 0)], t, 1024, 1024, 1024, 1024, 1024, (F32,))
    dact_s, dproj, dsm_s, dsnw, dd, dal, ddtb = _ssd_bwd(n + "ssd_bwd", s["act_s"], s["proj"], k["dtb"], k["arow"],
                                                           k["dxrow"], k["snw"], rmat, s["ysc"], s["st_s"], dy_ssm, dproj)
    dact_g, dproj, dsm_g, dgnw, dgal, dgb = _gdn_bwd(n + "gdn_bwd", s["act_g"], s["proj"], k["gb"], k["garow"], k["gnw"],
                                                       s["oraw"], s["st_g"], dy_gdn, s["tms"], dproj)
    dsm_s = hooks.mixers_done(dsm_s)
    dproj = _place_small(n + "d_small", dsm_s, dsm_g, dproj)
    dproj, g["ssm_conv_w"], dcb = _conv_bwd(n + "d_conv_ssm", s["proj"], XBC_OFF, p["ssm_conv_w"], k["scb"], dact_s, dproj)
    dproj, g["gdn_conv_w"], _ = _conv_bwd(n + "d_conv_gdn", s["proj"], QKV_OFF, p["gdn_conv_w"], k["zb"], dact_g, dproj)
    g["ssm_conv_b"] = dcb.reshape(-1)
    g["ssm_norm_w"] = dsnw.reshape(-1)
    g["ssm_d"] = dd[0, :SSM_HEADS]
    g["ssm_a_log"] = dal[0, :SSM_HEADS]
    g["ssm_dt_bias"] = ddtb[0, :SSM_HEADS]
    g["gdn_norm_w"] = dgnw.reshape(-1)
    g["gdn_a_log"] = dgal[0, LANE_A: LANE_A + GDN_HEADS]
    g["gdn_dt_bias"] = dgb[0, LANE_A: LANE_A + GDN_HEADS]
    dh = _matmul(n + "d_in_proj", "nt", [(dproj, 0, p["w_in"], 0)], t, 1024, PROJ_W, 1024, 1024, 1280, (F32,))
    g["w_in"] = _matmul(n + "dw_in", "tn", [(s["h"], 0, dproj, 0)], 1024, PROJ_W, t, 1024, 1280, tk_tok, (BF16,))
    dx, dxb, g["norm_mix_w"] = _rmsnorm_bwd(n + "d_norm_mix", s["x"], p["norm_mix_w"], dh, dx1)
    g["norm_mix_w"] = g["norm_mix_w"].reshape(-1)
    g["norm_ffn_w"] = g["norm_ffn_w"].reshape(-1)
    return dx, dxb, g


def _local_step(x, tgt, layers, final_norm_w, reduce=False):
    rmat = _expand_matrix()
    saved, params = [], []
    for l in range(DEPTH):
        x, p = layers[l](x)
        x, s = _layer_fwd(l, x, p, rmat)
        saved.append(s)
        params.append(s["p"])
    loss, dx, dxb, dfw = _loss_head("loss_head", x, final_norm_w, tgt)
    grads = [None] * DEPTH
    hooks = _ReduceBesideBackward() if reduce else _NoReduce()
    for l in reversed(range(DEPTH)):
        dx, dxb, grads[l] = _layer_bwd(l, dx, dxb, saved[l], params[l], rmat, hooks)
        if reduce:
            dxb = hooks.layer_done(l, grads[l], dxb)
    if reduce:
        hooks.finish(dxb)
    return loss[0, 0], dx, grads, dfw.reshape(-1), hooks.shards if reduce else None


SMALL = ("norm_mix_w", "ssm_conv_b", "ssm_dt_bias", "ssm_a_log", "ssm_d", "ssm_norm_w", "gdn_a_log", "gdn_dt_bias",
         "gdn_norm_w", "norm_ffn_w")
WEIGHTS = ("norm_mix_w", "w_in", "ssm_conv_w", "ssm_conv_b", "ssm_dt_bias", "ssm_a_log", "ssm_d", "ssm_norm_w", "gdn_conv_w",
           "gdn_a_log", "gdn_dt_bias", "gdn_norm_w", "w_proj_ssm", "w_proj_gdn", "w_out", "norm_ffn_w", "w_ffn_in",
           "w_ffn_down", "final_norm_w")


FIRST_USED = ("w_in", "ssm_conv_w", "gdn_conv_w")


def _gather_layer(l, w):
    conv_names = [nm for nm, _ in CONVS]
    groups = ([s for s in BIG + CONVS if s[0] in FIRST_USED], [s for s in BIG + CONVS if s[0] not in FIRST_USED])
    gathered = []
    for i, (specs, tag) in enumerate(zip(groups, ("first", "rest"))):
        shards = [w[nm][l] if nm in conv_names else w[nm][l].astype(BF16) for nm, _ in specs]
        gathered.append(_allgather_seq(f"l{l}_gather_{tag}", shards, collective_id=2 * l + i))
    small = {nm: w[nm][l] for nm in SMALL}

    def use(i, act):
        act, blocks = lax.optimization_barrier((act, gathered[i]))
        return act, {nm: _w_in_from_blocks(g) if nm == "w_in" else _from_gathered(g, axis)
                     for (nm, axis), g in zip(groups[i], blocks)}

    def full_weights(x):
        x, out = use(0, x)
        out.update(small)
        out["late"] = lambda y: use(1, y)
        return x, out

    return full_weights


EARLY_GRADS = ("w_ffn_down", "w_ffn_in", "w_out", "w_proj_ssm", "w_proj_gdn")


class _GradReduceScatter:
    def __init__(self, tag, specs, grads):
        self.tag = tag
        self.specs = specs
        self.blocks = [_w_in_to_blocks(grads[nm]) if nm == "w_in" else _to_dest_major(grads[nm], axis)
                       for nm, axis in specs]

    def _tied(self, started, acts):
        *acts, self.token = lax.optimization_barrier((*acts, started[4]))
        return acts

    def start(self, *acts):
        cc = lax.axis_index("c")
        self.keep = [lax.dynamic_index_in_dim(b.reshape((4, 2) + b.shape[1:]), cc, axis=1, keepdims=False)
                     for b in self.blocks]
        self.to_sibling = _copies_start(f"{self.tag}_to_sibling_start", _sibling_plan, 4, self.blocks, 4)
        return self._tied(self.to_sibling, acts)

    def mid(self, *acts):
        got = _copies_wait(f"{self.tag}_to_sibling_wait", _sibling_plan, self.to_sibling, (acts[0], self.token))
        chip_sums = [_sum_terms(f"{self.tag}_chip_sum_{nm}", [(k[None], 0), (g[None], 0)], BF16)
                     for (nm, _), k, g in zip(self.specs, self.keep, got)]
        self.to_chips = _copies_start(f"{self.tag}_between_chips_start", _chips_plan, 3, chip_sums, 3)
        return self._tied(self.to_chips, acts)

    def end(self, after):
        landed = _copies_wait(f"{self.tag}_between_chips_wait", _chips_plan, self.to_chips, (after, self.token))
        my_chip = 2 * lax.axis_index("x") + lax.axis_index("y")
        own = [lax.dynamic_index_in_dim(s, my_chip, axis=0, keepdims=True) for s in self.to_chips[2]]
        return {nm: _sum_terms(f"{self.tag}_total_{nm}", [(o, 0), (e, 0), (e, 1), (e, 2)], F32)
                for (nm, _), o, e in zip(self.specs, own, landed)}


class _NoReduce:
    def ffn_done(self, dx1b):
        return dx1b

    def early_ready(self, l, g, dp1, dp2):
        return dp1, dp2

    def mixers_done(self, dsm):
        return dsm


class _ReduceBesideBackward(_NoReduce):
    def __init__(self):
        self.late = None
        self.early = None
        self.shards = [dict() for _ in range(DEPTH)]

    def ffn_done(self, dx1b):
        if self.late is not None:
            (dx1b,) = self.late.mid(dx1b)
        return dx1b

    def early_ready(self, l, g, dp1, dp2):
        self.early = _GradReduceScatter(f"l{l}_early_grads", [s for s in BIG if s[0] in EARLY_GRADS], g)
        return self.early.start(dp1, dp2)

    def mixers_done(self, dsm):
        (dsm,) = self.early.mid(dsm)
        return dsm

    def layer_done(self, l, g, dxb):
        if self.late is not None:
            self.shards[l + 1].update(self.late.end(dxb))
        self.shards[l].update(self.early.end(dxb))
        self.late = _GradReduceScatter(f"l{l}_late_grads", [s for s in BIG + CONVS if s[0] not in EARLY_GRADS], g)
        (dxb,) = self.late.start(dxb)
        return dxb

    def finish(self, dxb):
        (dxb,) = self.late.mid(dxb)
        self.shards[0].update(self.late.end(dxb))
        return dxb


def _allreduce_small(vecs):
    flat = jnp.concatenate(vecs)
    n = flat.shape[0]
    rows = -(-n // 128)
    rows = -(-rows // 8) * 8
    buf = jnp.pad(flat, (0, rows * 128 - n)).reshape(rows, 128)
    (allv,) = _allgather("gather_small_grads", [buf])
    tot = _sum_terms("small_grads_total", [(allv, d) for d in range(N_DEV)], F32).reshape(-1)
    out, o = [], 0
    for v in vecs:
        out.append(tot[o: o + v.shape[0]])
        o += v.shape[0]
    return out


def kernel(x, norm_mix_w, w_in, ssm_conv_w, ssm_conv_b, ssm_dt_bias, ssm_a_log, ssm_d, ssm_norm_w, gdn_conv_w, gdn_a_log, gdn_dt_bias, gdn_norm_w, w_proj_ssm, w_proj_gdn, w_out, norm_ffn_w, w_ffn_in, w_ffn_down, final_norm_w, loss_target, m_norm_mix_w, m_w_in, m_ssm_conv_w, m_ssm_conv_b, m_ssm_dt_bias, m_ssm_a_log, m_ssm_d, m_ssm_norm_w, m_gdn_conv_w, m_gdn_a_log, m_gdn_dt_bias, m_gdn_norm_w, m_w_proj_ssm, m_w_proj_gdn, m_w_out, m_norm_ffn_w, m_w_ffn_in, m_w_ffn_down, m_final_norm_w, v_norm_mix_w, v_w_in, v_ssm_conv_w, v_ssm_conv_b, v_ssm_dt_bias, v_ssm_a_log, v_ssm_d, v_ssm_norm_w, v_gdn_conv_w, v_gdn_a_log, v_gdn_dt_bias, v_gdn_norm_w, v_w_proj_ssm, v_w_proj_gdn, v_w_out, v_norm_ffn_w, v_w_ffn_in, v_w_ffn_down, v_final_norm_w):
    w = dict(norm_mix_w=norm_mix_w, w_in=w_in, ssm_conv_w=ssm_conv_w, ssm_conv_b=ssm_conv_b, ssm_dt_bias=ssm_dt_bias,
             ssm_a_log=ssm_a_log, ssm_d=ssm_d, ssm_norm_w=ssm_norm_w, gdn_conv_w=gdn_conv_w, gdn_a_log=gdn_a_log,
             gdn_dt_bias=gdn_dt_bias, gdn_norm_w=gdn_norm_w, w_proj_ssm=w_proj_ssm, w_proj_gdn=w_proj_gdn, w_out=w_out,
             norm_ffn_w=norm_ffn_w, w_ffn_in=w_ffn_in, w_ffn_down=w_ffn_down, final_norm_w=final_norm_w)
    m = dict(norm_mix_w=m_norm_mix_w, w_in=m_w_in, ssm_conv_w=m_ssm_conv_w, ssm_conv_b=m_ssm_conv_b, ssm_dt_bias=m_ssm_dt_bias,
             ssm_a_log=m_ssm_a_log, ssm_d=m_ssm_d, ssm_norm_w=m_ssm_norm_w, gdn_conv_w=m_gdn_conv_w, gdn_a_log=m_gdn_a_log,
             gdn_dt_bias=m_gdn_dt_bias, gdn_norm_w=m_gdn_norm_w, w_proj_ssm=m_w_proj_ssm, w_proj_gdn=m_w_proj_gdn,
             w_out=m_w_out, norm_ffn_w=m_norm_ffn_w, w_ffn_in=m_w_ffn_in, w_ffn_down=m_w_ffn_down,
             final_norm_w=m_final_norm_w)
    v = dict(norm_mix_w=v_norm_mix_w, w_in=v_w_in, ssm_conv_w=v_ssm_conv_w, ssm_conv_b=v_ssm_conv_b, ssm_dt_bias=v_ssm_dt_bias,
             ssm_a_log=v_ssm_a_log, ssm_d=v_ssm_d, ssm_norm_w=v_ssm_norm_w, gdn_conv_w=v_gdn_conv_w, gdn_a_log=v_gdn_a_log,
             gdn_dt_bias=v_gdn_dt_bias, gdn_norm_w=v_gdn_norm_w, w_proj_ssm=v_w_proj_ssm, w_proj_gdn=v_w_proj_gdn,
             w_out=v_w_out, norm_ffn_w=v_norm_ffn_w, w_ffn_in=v_w_ffn_in, w_ffn_down=v_w_ffn_down,
             final_norm_w=v_final_norm_w)

    layers = [_gather_layer(l, w) for l in range(DEPTH)]
    loss_part, dx, lgrads, dfw, shard_grads = _local_step(x[0], loss_target[0], layers, final_norm_w, reduce=True)
    loss = lax.psum(loss_part, ("x", "y", "c"))
    grad = {nm: [shard_grads[l][nm] for l in range(DEPTH)] for nm, _ in BIG + CONVS}
    small_vecs = [lgrads[l][nm].reshape(-1) for l in range(DEPTH) for nm in SMALL] + [dfw]
    small_sum = _allreduce_small(small_vecs)
    for i, nm in enumerate(SMALL):
        grad[nm] = jnp.stack([small_sum[l * len(SMALL) + i].reshape(w[nm].shape[1:]) for l in range(DEPTH)])
    grad["final_norm_w"] = small_sum[-1]

    deltas, new_m, new_v = {}, {}, {}
    for nm in WEIGHTS:
        grad[nm], deltas[nm], new_m[nm], new_v[nm] = _adamw("adamw_" + nm, w[nm], grad[nm], m[nm], v[nm])
    return (loss, dx[None], *[grad[nm] for nm in WEIGHTS], *[deltas[nm] for nm in WEIGHTS],
            *[new_m[nm] for nm in WEIGHTS], *[new_v[nm] for nm in WEIGHTS])
```

```python
import functools

import jax
import jax.numpy as jnp
from jax import lax
from jax.experimental import pallas as pl
from jax.experimental.pallas import tpu as pltpu
from jax.experimental.pallas import tpu_sc as plsc

F32 = jnp.float32
BF16 = jnp.bfloat16
HI = lax.Precision.HIGHEST
SDS = jax.ShapeDtypeStruct

D_MODEL = 1024
DEPTH = 2
SSM_HEADS = 16
SSM_P = 64
SSM_N = 128
SSM_GROUPS = 2
SSM_CONV = 1536
GDN_HEADS = 8
GDN_DK = 128
GDN_QKV = 3072
CONV_K = 4
CHUNK = 64
FFN = 2816
IN_DIM = 8736
EPS = 1e-6
N_DEV = 8

Z_OFF = 0
GZ_OFF = 1024
G1_OFF = 2048
G2_OFF = 3072
QKV_OFF = 4096
XBC_OFF = 7168
SM_OFF = 8704
PROJ_W = 8960
LANE_A = 16
LANE_B = 24
O_Z, O_XBC, O_DT, O_QKV, O_GZ, O_A, O_B, O_G1, O_G2 = 0, 1024, 2560, 2576, 5648, 6672, 6680, 6688, 7712

ADAM_LR = 0.001
ADAM_B1 = 0.9
ADAM_B2 = 0.999
ADAM_EPS = 1e-08
ADAM_WD = 0.01
ADAM_STEP = 10

V7X_VMEM_LIMIT = 48 * 1024 * 1024

NN = ((1,), (0,))
NT = ((1,), (1,))
TN = ((0,), (0,))


def _bdot(a, b, dims):
    return lax.dot_general(a.astype(BF16), b.astype(BF16), (dims, ((), ())), preferred_element_type=F32)


def _hdot(a, b, dims=NN):
    return lax.dot_general(a, b, (dims, ((), ())), precision=HI, preferred_element_type=F32)


def _sigmoid(x):
    return 1.0 / (1.0 + jnp.exp(-x))


def _softplus(x):
    return jnp.maximum(x, 0.0) + jnp.log(1.0 + jnp.exp(-jnp.abs(x)))


def _params(dims):
    return pltpu.CompilerParams(dimension_semantics=dims, vmem_limit_bytes=V7X_VMEM_LIMIT)


def _rowsum(x):
    return jnp.sum(x, axis=-1, keepdims=True)


def _colsum(x):
    return jnp.sum(x, axis=0, keepdims=True)


def _matmul(name, mode, pairs, m, n, kdim, tm, tn, tk, out_dtypes, epi=None, extras=()):
    tm, tn, tk = min(tm, m), min(tn, n), min(tk, kdim)
    nk = kdim // tk
    assert m % tm == 0 and n % tn == 0 and kdim % tk == 0, (name, m, n, kdim, tm, tn, tk)
    in_specs, args = [], []
    for a, a_off, b, b_off in pairs:
        if mode == "nn":
            in_specs.append(pl.BlockSpec((tm, tk), lambda i, j, k, o=a_off: (i, k + o)))
            in_specs.append(pl.BlockSpec((tk, tn), lambda i, j, k, o=b_off: (k, j + o)))
            dims = NN
        elif mode == "nt":
            in_specs.append(pl.BlockSpec((tm, tk), lambda i, j, k, o=a_off: (i, k + o)))
            in_specs.append(pl.BlockSpec((tn, tk), lambda i, j, k, o=b_off: (j, k + o)))
            dims = NT
        else:
            in_specs.append(pl.BlockSpec((tk, tm), lambda i, j, k, o=a_off: (k, i + o)))
            in_specs.append(pl.BlockSpec((tk, tn), lambda i, j, k, o=b_off: (k, j + o)))
            dims = TN
        args += [a, b]
    for e, e_off in extras:
        in_specs.append(pl.BlockSpec((tm, tn), lambda i, j, k, o=e_off: (i, j + o)))
        args.append(e)
    npair, nex, nout = len(pairs), len(extras), len(out_dtypes)

    def body(*refs):
        prefs = refs[: 2 * npair]
        erefs = refs[2 * npair: 2 * npair + nex]
        orefs = refs[2 * npair + nex: 2 * npair + nex + nout]

        def finish(res):
            outs = (res,) if epi is None else epi(res, *[e[...] for e in erefs])
            for o, r in zip(orefs, outs):
                o[...] = r.astype(o.dtype)

        s = _bdot(prefs[0][...], prefs[1][...], dims)
        for p in range(1, npair):
            s = s + _bdot(prefs[2 * p][...], prefs[2 * p + 1][...], dims)
        if nk == 1:
            finish(s)
            return
        acc = refs[-1]
        k = pl.program_id(2)

        @pl.when(k == 0)
        def _():
            acc[...] = s

        @pl.when(k > 0)
        def _():
            acc[...] += s

        @pl.when(k == nk - 1)
        def _():
            finish(acc[...])

    out_shape, out_specs = [], []
    for od in out_dtypes:
        if isinstance(od, tuple):
            dt, full_w, blk_w, cblk = od
            assert n == tn
            out_shape.append(SDS((m, full_w), dt))
            out_specs.append(pl.BlockSpec((tm, blk_w), lambda i, j, k, c=cblk: (i, c)))
        else:
            out_shape.append(SDS((m, n), od))
            out_specs.append(pl.BlockSpec((tm, tn), lambda i, j, k: (i, j)))
    out_shape, out_specs = tuple(out_shape), tuple(out_specs)
    res = pl.pallas_call(
        body, grid=(m // tm, n // tn, nk), in_specs=in_specs, out_specs=out_specs, out_shape=out_shape,
        scratch_shapes=[pltpu.VMEM((tm, tn), F32)] if nk > 1 else [], name=name,
        compiler_params=_params(("parallel", "parallel", "arbitrary")),
    )(*args)
    return res if nout > 1 else res[0]


def _rmsnorm_fwd(name, x, w):
    t, d = x.shape
    tm = min(512, t)

    def body(x_ref, w_ref, h_ref):
        xv = x_ref[...]
        r = lax.rsqrt(jnp.mean(xv * xv, axis=-1, keepdims=True) + EPS)
        h_ref[...] = (xv * r * w_ref[...]).astype(BF16)

    return pl.pallas_call(
        body, grid=(t // tm,),
        in_specs=[pl.BlockSpec((tm, d), lambda i: (i, 0)), pl.BlockSpec((1, d), lambda i: (0, 0))],
        out_specs=pl.BlockSpec((tm, d), lambda i: (i, 0)), out_shape=SDS((t, d), BF16), name=name,
        compiler_params=_params(("parallel",)),
    )(x, w.reshape(1, d))


def _rmsnorm_bwd(name, x, w, dh, dres):
    t, d = x.shape
    tm = min(512, t)

    def body(x_ref, w_ref, dh_ref, dres_ref, dx_ref, dxb_ref, dw_ref):
        xv = x_ref[...]
        r = lax.rsqrt(jnp.mean(xv * xv, axis=-1, keepdims=True) + EPS)
        xh = xv * r
        dhv = dh_ref[...].astype(F32)
        dxh = dhv * w_ref[...]
        dx = r * (dxh - xh * jnp.mean(dxh * xh, axis=-1, keepdims=True)) + dres_ref[...]
        dx_ref[...] = dx
        dxb_ref[...] = dx.astype(BF16)

        @pl.when(pl.program_id(0) == 0)
        def _():
            dw_ref[...] = jnp.zeros_like(dw_ref)

        dw_ref[...] += _colsum(dhv * xh)

    row = pl.BlockSpec((tm, d), lambda i: (i, 0))
    vec = pl.BlockSpec((1, d), lambda i: (0, 0))
    return pl.pallas_call(
        body, grid=(t // tm,), in_specs=[row, vec, row, row], out_specs=(row, row, vec),
        out_shape=(SDS((t, d), F32), SDS((t, d), BF16), SDS((1, d), F32)), name=name,
        compiler_params=_params(("arbitrary",)),
    )(x, w.reshape(1, d), dh, dres)


def _loss_head(name, x, w, tgt):
    t, d = x.shape
    tm = min(512, t)

    def body(x_ref, w_ref, t_ref, loss_ref, dx_ref, dxb_ref, dw_ref):
        xv = x_ref[...]
        wv = w_ref[...]
        r = lax.rsqrt(jnp.mean(xv * xv, axis=-1, keepdims=True) + EPS)
        xh = xv * r
        e = xh * wv - t_ref[...]
        dy = e * (1.0 / d)
        dxh = dy * wv
        dx = r * (dxh - xh * jnp.mean(dxh * xh, axis=-1, keepdims=True))
        dx_ref[...] = dx
        dxb_ref[...] = dx.astype(BF16)

        @pl.when(pl.program_id(0) == 0)
        def _():
            dw_ref[...] = jnp.zeros_like(dw_ref)
            loss_ref[...] = jnp.zeros_like(loss_ref)

        dw_ref[...] += _colsum(dy * xh)
        loss_ref[...] += 0.5 * jnp.sum(jnp.mean(e * e, axis=-1, keepdims=True), axis=0, keepdims=True)

    row = pl.BlockSpec((tm, d), lambda i: (i, 0))
    vec = pl.BlockSpec((1, d), lambda i: (0, 0))
    return pl.pallas_call(
        body, grid=(t // tm,), in_specs=[row, vec, row],
        out_specs=(pl.BlockSpec((1, 1), lambda i: (0, 0)), row, row, vec),
        out_shape=(SDS((1, 1), F32), SDS((t, d), F32), SDS((t, d), BF16), SDS((1, d), F32)), name=name,
        compiler_params=_params(("arbitrary",)),
    )(x, w.reshape(1, d), tgt)


def _shift_down(u, s, row):
    return jnp.where(row >= s, pltpu.roll(u, shift=s, axis=0), 0.0)


def _conv_fwd(name, src, col0, w, b):
    t = src.shape[0]
    c = w.shape[1]
    tc = 256
    assert c % tc == 0 and col0 % tc == 0

    def body(u_ref, w_ref, b_ref, o_ref):
        u = u_ref[...]
        wv = w_ref[...]
        row = lax.broadcasted_iota(jnp.int32, u.shape, 0)
        pre = b_ref[...] + wv[3:4, :] * u
        for s in range(1, CONV_K):
            pre = pre + wv[3 - s: 4 - s, :] * _shift_down(u, s, row)
        o_ref[...] = pre * _sigmoid(pre)

    return pl.pallas_call(
        body, grid=(c // tc,),
        in_specs=[pl.BlockSpec((t, tc), lambda j: (0, j + col0 // tc)), pl.BlockSpec((CONV_K, tc), lambda j: (0, j)),
                  pl.BlockSpec((1, tc), lambda j: (0, j))],
        out_specs=pl.BlockSpec((t, tc), lambda j: (0, j)), out_shape=SDS((t, c), F32), name=name,
        compiler_params=_params(("parallel",)),
    )(src, w, b)


def _place_small(name, dsm_a, dsm_b, dproj):
    t = dsm_a.shape[0]
    width = PROJ_W - SM_OFF
    tr = min(512, t)

    def body(a_ref, b_ref, dproj_ref, o_ref):
        o_ref[:, :128] = a_ref[...] + b_ref[...]
        o_ref[:, 128:] = jnp.zeros((tr, width - 128), BF16)

    row = pl.BlockSpec((tr, 128), lambda i: (i, 0))
    return pl.pallas_call(
        body, grid=(t // tr,), in_specs=[row, row, ANY],
        out_specs=pl.BlockSpec((tr, width), lambda i: (i, SM_OFF // width)), out_shape=SDS(dproj.shape, BF16),
        input_output_aliases={2: 0}, name=name, compiler_params=_params(("parallel",)),
    )(dsm_a, dsm_b, dproj)


def _conv_bwd(name, src, col0, w, b, dact, dproj):
    t = src.shape[0]
    c = w.shape[1]
    tc = 128

    def body(u_ref, w_ref, b_ref, da_ref, dproj_ref, du_ref, dw_ref, db_ref):
        u = u_ref[...]
        wv = w_ref[...]
        row = lax.broadcasted_iota(jnp.int32, u.shape, 0)
        shifted = [u] + [_shift_down(u, s, row) for s in range(1, CONV_K)]
        pre = b_ref[...] + wv[3:4, :] * u
        for s in range(1, CONV_K):
            pre = pre + wv[3 - s: 4 - s, :] * shifted[s]
        sg = _sigmoid(pre)
        dpre = da_ref[...] * (sg * (1.0 + pre * (1.0 - sg)))
        du = wv[3:4, :] * dpre
        for s in range(1, CONV_K):
            du = du + wv[3 - s: 4 - s, :] * jnp.where(row < t - s, pltpu.roll(dpre, shift=t - s, axis=0), 0.0)
        du_ref[...] = du.astype(BF16)
        for s in range(CONV_K):
            dw_ref[3 - s: 4 - s, :] = _colsum(dpre * shifted[s])
        db_ref[...] = _colsum(dpre)

    return pl.pallas_call(
        body, grid=(c // tc,),
        in_specs=[pl.BlockSpec((t, tc), lambda j: (0, j + col0 // tc)), pl.BlockSpec((CONV_K, tc), lambda j: (0, j)),
                  pl.BlockSpec((1, tc), lambda j: (0, j)), pl.BlockSpec((t, tc), lambda j: (0, j)), ANY],
        out_specs=(pl.BlockSpec((t, tc), lambda j: (0, j + col0 // tc)), pl.BlockSpec((CONV_K, tc), lambda j: (0, j)),
                   pl.BlockSpec((1, tc), lambda j: (0, j))),
        out_shape=(SDS(dproj.shape, BF16), SDS((CONV_K, c), F32), SDS((1, c), F32)), name=name,
        input_output_aliases={4: 0},
        compiler_params=_params(("parallel",)),
    )(src, w, b, dact, dproj)


def _tri(q):
    ii = lax.broadcasted_iota(jnp.int32, (q, q), 0)
    jj = lax.broadcasted_iota(jnp.int32, (q, q), 1)
    return ii, jj


def _dot01(x, r01, dims, terms=3):
    out, rem = None, x
    for i in range(terms):
        hi = rem.astype(BF16)
        d = lax.dot_general(hi, r01, (dims, ((), ())), preferred_element_type=F32)
        out = d if out is None else out + d
        if i + 1 < terms:
            rem = rem - hi.astype(F32)
    return out


def _ssd_common(act, sm, dtb, arow, rmat):
    q = CHUNK
    ii, jj = _tri(q)
    lane = lax.broadcasted_iota(jnp.int32, (q, 128), 1)
    m16 = lane < SSM_HEADS
    dt = jnp.where(m16, _softplus(sm + dtb), 0.0)
    a = dt * arow
    tril = (ii >= jj).astype(F32)
    triu = (ii <= jj).astype(F32)
    acum = _hdot(tril, a)
    acum_r = _hdot(a.T, triu)
    dtx = _dot01(dt, rmat, NN)
    acx = _dot01(acum, rmat, NN)
    ex = jnp.exp(acx)
    alx = acx[q - 1: q, :]
    dex = jnp.exp(alx - acx)
    xs = act[:, :1024]
    return dict(ii=ii, jj=jj, m16=m16, dt=dt, a=a, triu=triu, acum=acum, acum_r=acum_r, dtx=dtx, ex=ex, dex=dex,
                elx=jnp.exp(alx), xs=xs, x=xs * dtx)


def _ssd_lmat(cm, h):
    return jnp.where(cm["ii"] >= cm["jj"], jnp.exp(cm["acum"][:, h: h + 1] - cm["acum_r"][h: h + 1, :]), 0.0)


def _ssd_fwd(name, act, proj, dtb, arow, dxrow, nw, rmat):
    t = act.shape[0]
    q = CHUNK
    nc = t // q
    hg = SSM_HEADS // SSM_GROUPS
    gw = hg * SSM_P

    def body(act_ref, z_ref, sm_ref, dtb_ref, arow_ref, dx_ref, nw_ref, r_ref, y_ref, ys_ref, st_ref, s_scr, yd_scr):
        @pl.when(pl.program_id(0) == 0)
        def _():
            s_scr[...] = jnp.zeros_like(s_scr)

        s_all = s_scr[...]
        for sub in range(cps):
            rows = pl.ds(sub * q, q)
            s_all = chunk(act_ref.at[rows, :], z_ref.at[rows, :], sm_ref.at[rows, :], dtb_ref, arow_ref, dx_ref, nw_ref, r_ref,
                          y_ref.at[rows, :], ys_ref.at[rows, :], st_ref.at[sub], yd_scr.at[rows, :], s_all)
        s_scr[...] = s_all

    def chunk(act_ref, z_ref, sm_ref, dtb_ref, arow_ref, dx_ref, nw_ref, r_ref, y_ref, ys_ref, st_ref, yd_scr, s_all):
        st_ref[...] = s_all
        actv = act_ref[...]
        cm = _ssd_common(actv, sm_ref[...], dtb_ref[...], arow_ref[...], r_ref[...])
        x = cm["x"]
        xd = x * cm["dex"]
        yoffs, snew = [], []
        for g in range(SSM_GROUPS):
            bg = actv[:, 1024 + g * SSM_N: 1024 + (g + 1) * SSM_N]
            cg = actv[:, 1280 + g * SSM_N: 1280 + (g + 1) * SSM_N]
            sg = s_all[:, g * gw: (g + 1) * gw]
            cb = _bdot(cg, bg, NT)
            yoffs.append(_bdot(cg, sg, NN))
            snew.append(_bdot(bg, xd[:, g * gw: (g + 1) * gw], TN))
            for r in range(hg):
                h = g * hg + r
                mm = cb * _ssd_lmat(cm, h)
                yd_scr[:, h * SSM_P: (h + 1) * SSM_P] = _bdot(mm, x[:, h * SSM_P: (h + 1) * SSM_P], NN)
        s_next = s_all * cm["elx"] + jnp.concatenate(snew, axis=1)
        ysc = yd_scr[...] + jnp.concatenate(yoffs, axis=1) * cm["ex"]
        ys_ref[...] = ysc
        zv = z_ref[...]
        yg = (ysc + dx_ref[...] * cm["xs"]) * (zv * _sigmoid(zv))
        nwv = nw_ref[...]
        for g in range(SSM_GROUPS):
            sl = yg[:, g * gw: (g + 1) * gw]
            rr = lax.rsqrt(jnp.mean(sl * sl, axis=-1, keepdims=True) + EPS)
            y_ref[:, g * gw: (g + 1) * gw] = (sl * rr * nwv[:, g * gw: (g + 1) * gw]).astype(BF16)
        return s_next

    cps = 2 if nc % 2 == 0 else 1
    qq = cps * q
    vec128 = pl.BlockSpec((1, 128), lambda c: (0, 0))
    vec1k = pl.BlockSpec((1, 1024), lambda c: (0, 0))
    return pl.pallas_call(
        body, grid=(nc // cps,),
        in_specs=[pl.BlockSpec((qq, SSM_CONV), lambda c: (c, 0)), pl.BlockSpec((qq, 1024), lambda c: (c, Z_OFF // 1024)),
                  pl.BlockSpec((qq, 128), lambda c: (c, SM_OFF // 128)), vec128, vec128, vec1k, vec1k,
                  pl.BlockSpec((128, 1024), lambda c: (0, 0))],
        out_specs=(pl.BlockSpec((qq, 1024), lambda c: (c, 0)), pl.BlockSpec((qq, 1024), lambda c: (c, 0)),
                   pl.BlockSpec((cps, 128, 1024), lambda c: (c, 0, 0))),
        out_shape=(SDS((t, 1024), BF16), SDS((t, 1024), F32), SDS((nc, 128, 1024), F32)),
        scratch_shapes=[pltpu.VMEM((128, 1024), F32), pltpu.VMEM((qq, 1024), F32)], name=name,
        compiler_params=_params(("arbitrary",)),
    )(act, proj, proj, dtb, arow, dxrow, nw, rmat)


def _ssd_bwd(name, act, proj, dtb, arow, dxrow, nw, rmat, ysc, states, dy, dproj):
    t = act.shape[0]
    q = CHUNK
    nc = t // q
    hg = SSM_HEADS // SSM_GROUPS
    gw = hg * SSM_P

    def body(act_ref, z_ref, sm_ref, dtb_ref, arow_ref, dx_ref, nw_ref, r_ref, ys_ref, st_ref, dy_ref, dproj_ref,
             dact_ref, dz_ref, dsm_ref, dnw_ref, dd_ref, dal_ref, ddtb_ref, ds_scr, dxd_scr):
        @pl.when(pl.program_id(0) == 0)
        def _():
            ds_scr[...] = jnp.zeros_like(ds_scr)
            dnw_ref[...] = jnp.zeros_like(dnw_ref)
            dd_ref[...] = jnp.zeros_like(dd_ref)
            dal_ref[...] = jnp.zeros_like(dal_ref)
            ddtb_ref[...] = jnp.zeros_like(ddtb_ref)

        dsn = ds_scr[...]
        for sub in reversed(range(cps)):
            rows = pl.ds(sub * q, q)
            dsn = chunk(act_ref.at[rows, :], z_ref.at[rows, :], sm_ref.at[rows, :], dtb_ref, arow_ref, dx_ref, nw_ref, r_ref,
                        ys_ref.at[rows, :], st_ref.at[sub], dy_ref.at[rows, :], dact_ref.at[rows, :], dz_ref.at[rows, :],
                        dsm_ref.at[rows, :], dnw_ref, dd_ref, dal_ref, ddtb_ref, dxd_scr.at[rows, :], dsn)
        ds_scr[...] = dsn

    def chunk(act_ref, z_ref, sm_ref, dtb_ref, arow_ref, dx_ref, nw_ref, r_ref, ys_ref, st_ref, dy_ref,
              dact_ref, dz_ref, dsm_ref, dnw_ref, dd_ref, dal_ref, ddtb_ref, dxd_scr, dsn):
        actv = act_ref[...]
        smv = sm_ref[...]
        rmat_v = r_ref[...]
        cm = _ssd_common(actv, smv, dtb_ref[...], arow_ref[...], rmat_v)
        ii, jj = cm["ii"], cm["jj"]
        x, xs = cm["x"], cm["xs"]
        s_all = st_ref[...]
        ysv = ys_ref[...]
        dxr = dx_ref[...]
        y = ysv + dxr * xs
        zv = z_ref[...]
        sz = _sigmoid(zv)
        silz = zv * sz
        yg = y * silz
        dout = dy_ref[...]
        nwv = nw_ref[...]
        dyn = dout * nwv
        yn_parts, dyg_parts = [], []
        for g in range(SSM_GROUPS):
            sl = yg[:, g * gw: (g + 1) * gw]
            rr = lax.rsqrt(jnp.mean(sl * sl, axis=-1, keepdims=True) + EPS)
            yn = sl * rr
            dn = dyn[:, g * gw: (g + 1) * gw]
            yn_parts.append(yn)
            dyg_parts.append(rr * (dn - yn * jnp.mean(dn * yn, axis=-1, keepdims=True)))
        dnw_ref[...] += _colsum(dout * jnp.concatenate(yn_parts, axis=1))
        dyg = jnp.concatenate(dyg_parts, axis=1)
        dyv = dyg * silz
        dz_ref[...] = (dyg * y * (sz * (1.0 + zv * (1.0 - sz)))).astype(BF16)
        dd_ref[...] += _dot01(_colsum(dyv * xs), rmat_v, NT)
        dxs = dyv * dxr
        dcs = dyv * cm["ex"]
        xd = x * cm["dex"]
        dxst_parts, ds_parts, db_parts, dc_parts, yoff_parts, wcol_rows = [], [], [], [], [], []
        lane128 = lax.broadcasted_iota(jnp.int32, (q, 128), 1)
        wrow = jnp.zeros((q, 128), F32)
        for g in range(SSM_GROUPS):
            bg = actv[:, 1024 + g * SSM_N: 1024 + (g + 1) * SSM_N]
            cg = actv[:, 1280 + g * SSM_N: 1280 + (g + 1) * SSM_N]
            sg = s_all[:, g * gw: (g + 1) * gw]
            dsng = dsn[:, g * gw: (g + 1) * gw]
            dcsg = dcs[:, g * gw: (g + 1) * gw]
            dcg = _bdot(dcsg, sg, NT)
            yoff_parts.append(_bdot(cg, sg, NN))
            ds_parts.append(_bdot(cg, dcsg, TN))
            dxst_parts.append(_bdot(bg, dsng, NN))
            dbg = _bdot(xd[:, g * gw: (g + 1) * gw], dsng, NT)
            cb = _bdot(cg, bg, NT)
            dcb = jnp.zeros((q, q), F32)
            for r in range(hg):
                h = g * hg + r
                lm = _ssd_lmat(cm, h)
                mm = cb * lm
                dyh = dyv[:, h * SSM_P: (h + 1) * SSM_P]
                dm = jnp.where(ii >= jj, _bdot(dyh, x[:, h * SSM_P: (h + 1) * SSM_P], NT), 0.0)
                dxd_scr[:, h * SSM_P: (h + 1) * SSM_P] = _bdot(mm, dyh, TN)
                dcb = dcb + dm * lm
                wm = dm * mm
                wrow = wrow + jnp.where(lane128 == h, _rowsum(wm), 0.0)
                wcol_rows.append(_colsum(wm))
            dc_parts.append(dcg + _bdot(dcb, bg, NN))
            db_parts.append(dbg + _bdot(dcb, cg, TN))
        dxst = jnp.concatenate(dxst_parts, axis=1) * cm["dex"]
        dx = dxd_scr[...] + dxst
        ds_prev = jnp.concatenate(ds_parts, axis=1) + dsn * cm["elx"]
        wcol = jnp.concatenate(wcol_rows + [jnp.zeros((128 - SSM_HEADS, q), F32)], axis=0).T
        yoff = jnp.concatenate(yoff_parts, axis=1) * cm["ex"]
        xdxst = x * dxst
        dac = wrow - wcol + _dot01(dyv * yoff - xdxst, rmat_v, NT)
        last = _dot01(_colsum(dsn * s_all) * cm["elx"] + _colsum(xdxst), rmat_v, NT)
        rowq = lax.broadcasted_iota(jnp.int32, (q, 128), 0)
        dac = dac + jnp.where(rowq == q - 1, last, 0.0)
        da = _hdot(cm["triu"], dac)
        arow_v = arow_ref[...]
        ddt = da * arow_v + _dot01(dx * xs, rmat_v, NT)
        dxs = dxs + dx * cm["dtx"]
        dal_ref[...] += _colsum(da * cm["a"])
        ddtraw = jnp.where(cm["m16"], ddt * _sigmoid(smv + dtb_ref[...]), 0.0)
        ddtb_ref[...] += _colsum(ddtraw)
        dsm_ref[...] = ddtraw.astype(BF16)
        dact_ref[:, :1024] = dxs
        for g in range(SSM_GROUPS):
            dact_ref[:, 1024 + g * SSM_N: 1024 + (g + 1) * SSM_N] = db_parts[g]
            dact_ref[:, 1280 + g * SSM_N: 1280 + (g + 1) * SSM_N] = dc_parts[g]
        return ds_prev

    cps = 2 if nc % 2 == 0 else 1
    qq = cps * q
    rev = lambda c: nc // cps - 1 - c
    vec128 = pl.BlockSpec((1, 128), lambda c: (0, 0))
    vec1k = pl.BlockSpec((1, 1024), lambda c: (0, 0))
    return pl.pallas_call(
        body, grid=(nc // cps,),
        in_specs=[pl.BlockSpec((qq, SSM_CONV), lambda c: (rev(c), 0)),
                  pl.BlockSpec((qq, 1024), lambda c: (rev(c), Z_OFF // 1024)),
                  pl.BlockSpec((qq, 128), lambda c: (rev(c), SM_OFF // 128)), vec128, vec128, vec1k, vec1k,
                  pl.BlockSpec((128, 1024), lambda c: (0, 0)),
                  pl.BlockSpec((qq, 1024), lambda c: (rev(c), 0)), pl.BlockSpec((cps, 128, 1024), lambda c: (rev(c), 0, 0)),
                  pl.BlockSpec((qq, 1024), lambda c: (rev(c), 0)), ANY],
        out_specs=(pl.BlockSpec((qq, SSM_CONV), lambda c: (rev(c), 0)),
                   pl.BlockSpec((qq, 1024), lambda c: (rev(c), Z_OFF // 1024)),
                   pl.BlockSpec((qq, 128), lambda c: (rev(c), 0)), vec1k, vec128, vec128, vec128),
        out_shape=(SDS((t, SSM_CONV), F32), SDS(dproj.shape, BF16), SDS((t, 128), BF16), SDS((1, 1024), F32),
                   SDS((1, 128), F32), SDS((1, 128), F32), SDS((1, 128), F32)),
        input_output_aliases={11: 1},
        scratch_shapes=[pltpu.VMEM((128, 1024), F32), pltpu.VMEM((qq, 1024), F32)], name=name,
        compiler_params=_params(("arbitrary",)),
    )(act, proj, proj, dtb, arow, dxrow, nw, rmat, ysc, states, dy, dproj)


def _split(a):
    hi = a.astype(BF16)
    return hi, (a - hi.astype(F32)).astype(BF16)


def _dot3(a, b, dims=NN):
    (ah, al), (bh, bl) = a, b

    def d(x, y):
        return lax.dot_general(x, y, (dims, ((), ())), preferred_element_type=F32)

    return d(ah, bh) + (d(ah, bl) + d(al, bh))


def _tri_inverses(amats, ii, jj):
    eye = jnp.where(ii == jj, 1.0, 0.0)
    tms = [eye - a for a in amats]
    sp = [_split(a) for a in amats]
    for _ in range(5):
        sp = [_split(_dot3(s, s)) for s in sp]
        tms = [t + _dot3(_split(t), s) for t, s in zip(tms, sp)]
    return tms


def _gdn_common(sm, gb, garow):
    q = CHUNK
    ii, jj = _tri(q)
    lane = lax.broadcasted_iota(jnp.int32, (q, 128), 1)
    ma = (lane >= LANE_A) & (lane < LANE_A + GDN_HEADS)
    spre = sm + gb
    g = jnp.where(ma, garow * _softplus(spre), 0.0)
    beta = _sigmoid(sm)
    tril = (ii >= jj).astype(F32)
    triu = (ii <= jj).astype(F32)
    gc = _hdot(tril, g)
    gc_r = _hdot(g.T, triu)
    return dict(ii=ii, jj=jj, lane=lane, ma=ma, spre=spre, g=g, beta=beta, triu=triu, gc=gc, gc_r=gc_r)


def _each(f, *lists):
    return [f(*xs) for xs in zip(*lists)]


GDN_SCALE = GDN_DK ** -0.5


def _gdn_heads(cm, actv, states):
    q = CHUNK
    ii, jj = cm["ii"], cm["jj"]
    heads = range(GDN_HEADS)
    qr = [actv[:, h * 128: (h + 1) * 128] for h in heads]
    kr = [actv[:, 1024 + h * 128: 1024 + (h + 1) * 128] for h in heads]
    v = [actv[:, 2048 + h * 128: 2048 + (h + 1) * 128] for h in heads]
    rq = _each(lambda x: lax.rsqrt(_rowsum(x * x) + EPS), qr)
    rk = _each(lambda x: lax.rsqrt(_rowsum(x * x) + EPS), kr)
    qn = _each(lambda x, r: x * r * GDN_SCALE, qr, rq)
    kn = _each(lambda x, r: x * r, kr, rk)
    gcc = [cm["gc"][:, LANE_A + h: LANE_A + h + 1] for h in heads]
    gcr = [cm["gc_r"][LANE_A + h: LANE_A + h + 1, :] for h in heads]
    bcol = [cm["beta"][:, LANE_B + h: LANE_B + h + 1] for h in heads]
    dm = _each(lambda c, r: jnp.where(ii >= jj, jnp.exp(c - r), 0.0), gcc, gcr)
    kq = _each(lambda k, a: _bdot(jnp.concatenate([k, a], axis=0), k, NT), kn, qn)
    ak = _each(lambda x, d: jnp.where(ii > jj, x[:q] * d, 0.0), kq, dm)
    qkm = _each(lambda x, d: jnp.where(ii >= jj, x[q:] * d, 0.0), kq, dm)
    tm = _tri_inverses(_each(lambda a, b: a * b, ak, bcol), ii, jj)
    eg = _each(jnp.exp, gcc)
    gl = [c[q - 1: q, :] for c in gcc]
    rm = _each(lambda vv, k, b, e: jnp.concatenate([vv * b, k * (b * e)], axis=1), v, kn, bcol, eg)
    tt = _each(lambda t, r: _dot3(_split(t), _split(r)), tm, rm)
    w = [t[:, 128:] for t in tt]
    qg = _each(lambda a, e: a * e, qn, eg)
    ws = _each(lambda ww, a, s: _bdot(jnp.concatenate([ww, a], axis=0), s, NN), w, qg, states)
    vnew = _each(lambda t, x: t[:, :128] - x[:q], tt, ws)
    return dict(qr=qr, v=v, rq=rq, rk=rk, qn=qn, kn=kn, gcc=gcc, bcol=bcol, dm=dm, ak=ak, tm=tm, eg=eg, gl=gl,
                egl=_each(jnp.exp, gl), ed=_each(lambda g, c: jnp.exp(g - c), gl, gcc), tt=tt, w=w, vnew=vnew, qkm=qkm,
                qg=qg, qgs=[x[q:] for x in ws])


def _gdn_fwd(name, act, proj, gb, garow, gnw):
    t = act.shape[0]
    q = CHUNK
    nc = t // q

    def body(act_ref, gz_ref, sm_ref, gb_ref, ga_ref, nw_ref, y_ref, o_ref, st_ref, s_scr):
        @pl.when(pl.program_id(0) == 0)
        def _():
            s_scr[...] = jnp.zeros_like(s_scr)

        states = [s_scr[h * 128: (h + 1) * 128, :] for h in range(GDN_HEADS)]
        for sub in range(cps):
            rows = pl.ds(sub * q, q)
            states = chunk(act_ref.at[rows, :], gz_ref.at[rows, :], sm_ref.at[rows, :], gb_ref, ga_ref, nw_ref,
                           y_ref.at[rows, :], o_ref.at[rows, :], st_ref.at[sub], states)
        for h in range(GDN_HEADS):
            s_scr[h * 128: (h + 1) * 128, :] = states[h]

    def chunk(act_ref, gz_ref, sm_ref, gb_ref, ga_ref, nw_ref, y_ref, o_ref, st_ref, states):
        for h in range(GDN_HEADS):
            st_ref[h * 128: (h + 1) * 128, :] = states[h]
        actv = act_ref[...]
        cm = _gdn_common(sm_ref[...], gb_ref[...], ga_ref[...])
        nwv = nw_ref[...]
        gzv = gz_ref[...]
        hd = _gdn_heads(cm, actv, states)
        outs = _each(lambda qs, m, vn: qs + _bdot(m, vn, NN), hd["qgs"], hd["qkm"], hd["vnew"])
        snew = _each(lambda s, e, k, d, vn: s * e + _bdot(k * d, vn, TN), states, hd["egl"], hd["kn"], hd["ed"], hd["vnew"])
        for h in range(GDN_HEADS):
            o = outs[h]
            o_ref[:, h * 128: (h + 1) * 128] = o
            rr = lax.rsqrt(jnp.mean(o * o, axis=-1, keepdims=True) + EPS)
            gz = gzv[:, h * 128: (h + 1) * 128]
            y_ref[:, h * 128: (h + 1) * 128] = (o * rr * nwv * (gz * _sigmoid(gz))).astype(BF16)
        return snew

    cps = 2 if nc % 2 == 0 else 1
    qq = cps * q
    vec128 = pl.BlockSpec((1, 128), lambda c: (0, 0))
    return pl.pallas_call(
        body, grid=(nc // cps,),
        in_specs=[pl.BlockSpec((qq, GDN_QKV), lambda c: (c, 0)), pl.BlockSpec((qq, 1024), lambda c: (c, GZ_OFF // 1024)),
                  pl.BlockSpec((qq, 128), lambda c: (c, SM_OFF // 128)), vec128, vec128, vec128],
        out_specs=(pl.BlockSpec((qq, 1024), lambda c: (c, 0)), pl.BlockSpec((qq, 1024), lambda c: (c, 0)),
                   pl.BlockSpec((cps, 1024, 128), lambda c: (c, 0, 0))),
        out_shape=(SDS((t, 1024), BF16), SDS((t, 1024), F32), SDS((nc, 1024, 128), F32)),
        scratch_shapes=[pltpu.VMEM((1024, 128), F32)], name=name, compiler_params=_params(("arbitrary",)),
    )(act, proj, proj, gb, garow, gnw)


def _gdn_bwd(name, act, proj, gb, garow, gnw, oraw, states, dy, dproj):
    t = act.shape[0]
    q = CHUNK
    nc = t // q

    def body(act_ref, gz_ref, sm_ref, gb_ref, ga_ref, nw_ref, o_ref, st_ref, dy_ref, dproj_ref,
             dact_ref, dgz_ref, dsm_ref, dnw_ref, dal_ref, dgb_ref, ds_scr):
        @pl.when(pl.program_id(0) == 0)
        def _():
            ds_scr[...] = jnp.zeros_like(ds_scr)
            dnw_ref[...] = jnp.zeros_like(dnw_ref)
            dal_ref[...] = jnp.zeros_like(dal_ref)
            dgb_ref[...] = jnp.zeros_like(dgb_ref)

        dsn = [ds_scr[h * 128: (h + 1) * 128, :] for h in range(GDN_HEADS)]
        for sub in reversed(range(cps)):
            rows = pl.ds(sub * q, q)
            dsn = chunk(act_ref.at[rows, :], gz_ref.at[rows, :], sm_ref.at[rows, :], gb_ref, ga_ref, nw_ref,
                        o_ref.at[rows, :], st_ref.at[sub], dy_ref.at[rows, :],
                        dact_ref.at[rows, :], dgz_ref.at[rows, :], dsm_ref.at[rows, :], dnw_ref, dal_ref, dgb_ref, dsn)
        for h in range(GDN_HEADS):
            ds_scr[h * 128: (h + 1) * 128, :] = dsn[h]

    def chunk(act_ref, gz_ref, sm_ref, gb_ref, ga_ref, nw_ref, o_ref, st_ref, dy_ref,
              dact_ref, dgz_ref, dsm_ref, dnw_ref, dal_ref, dgb_ref, dsn):
        actv = act_ref[...]
        smv = sm_ref[...]
        garow_v = ga_ref[...]
        cm = _gdn_common(smv, gb_ref[...], garow_v)
        ii, jj, lane = cm["ii"], cm["jj"], cm["lane"]
        nwv = nw_ref[...]
        rowq = lax.broadcasted_iota(jnp.int32, (q, 1), 0)
        dgc_all = jnp.zeros((q, 128), F32)
        dbeta_all = jnp.zeros((q, 128), F32)
        dnw_acc = jnp.zeros((1, 128), F32)
        heads = range(GDN_HEADS)
        sts = [st_ref[h * 128: (h + 1) * 128, :] for h in heads]
        ds_out = []
        ov, gzv, dyv = o_ref[...], gz_ref[...], dy_ref[...]
        hd = _gdn_heads(cm, actv, sts)
        qn, kn, v, eg, ed, egl, bcol = hd["qn"], hd["kn"], hd["v"], hd["eg"], hd["ed"], hd["egl"], hd["bcol"]
        vnew, qkm, qg, w, tt, dm, ak = hd["vnew"], hd["qkm"], hd["qg"], hd["w"], hd["tt"], hd["dm"], hd["ak"]
        do = []
        for h in heads:
            hs = slice(h * 128, (h + 1) * 128)
            o = ov[:, hs]
            rr = lax.rsqrt(jnp.mean(o * o, axis=-1, keepdims=True) + EPS)
            on = o * rr
            gz = gzv[:, hs]
            sz = _sigmoid(gz)
            silz = gz * sz
            dyh = dyv[:, hs]
            dnw_acc = dnw_acc + _colsum(dyh * on * silz)
            dgz_ref[:, hs] = (dyh * on * nwv * (sz * (1.0 + gz * (1.0 - sz)))).astype(BF16)
            don = dyh * nwv * silz
            do.append(rr * (don - on * jnp.mean(don * on, axis=-1, keepdims=True)))
        kd = _each(lambda k, e: k * e, kn, ed)
        dkd = _each(lambda vn, d: _bdot(vn, d, NT), vnew, dsn)
        dvnew_a = _each(lambda k, d: _bdot(k, d, NN), kd, dsn)
        ded = _each(lambda a, b: _rowsum(a * b), dkd, kd)
        dgl = _each(lambda d, s, e, de: jnp.sum(_rowsum(d * s), axis=0, keepdims=True) * e + _colsum(de), dsn, sts, egl, ded)
        dqk = _each(lambda d, vn: jnp.where(ii >= jj, _bdot(d, vn, NT), 0.0), do, vnew)
        dvnew = _each(lambda a, m, d: a + _bdot(m, d, TN), dvnew_a, qkm, do)
        pq = _each(lambda a, b: a * b, dqk, dm)
        w1 = _each(lambda a, b: a * b, dqk, qkm)
        dod = _each(lambda a, b: jnp.concatenate([a, b], axis=0), do, dvnew)
        dos = _each(lambda x, s: _bdot(x, s, NT), dod, sts)
        dqg = [x[:q] for x in dos]
        dw = [-x[q:] for x in dos]
        ds12 = _each(lambda a, ww, x: _bdot(jnp.concatenate([a, -ww], axis=0), x, TN), qg, w, dod)
        dr = _each(lambda t, a, b: _dot3(_split(t), _split(jnp.concatenate([a, b], axis=1)), TN), hd["tm"], dvnew, dw)
        da = _each(lambda r, t: jnp.where(ii > jj, -_dot3(_split(r), _split(t), NT), 0.0), dr, tt)
        sk = _each(lambda r, k: _rowsum(r[:, 128:] * k), dr, kn)
        pk = _each(lambda a, d, b: a * d * b, da, dm, bcol)
        pkn = _each(lambda p, pp, k: _bdot(jnp.concatenate([p, pp + pp.T], axis=0), k, NN), pq, pk, kn)
        dq = _each(lambda a, e, x: a * e + x[:q], dqg, eg, pkn)
        dk = _each(lambda a, e, p, x, r, b, eg_, y: a * e + _bdot(p, x, TN) + r[:, 128:] * (b * eg_) + y[q:],
                   dkd, ed, pq, qn, dr, bcol, eg, pkn)
        w2 = _each(lambda a, k, b: a * (k * b), da, ak, bcol)
        for h in heads:
            hs = slice(h * 128, (h + 1) * 128)
            dgc = (-ded[h] + _rowsum(dqg[h] * qg[h]) + _rowsum(w1[h]) - _rowsum(w1[h].T) + sk[h] * bcol[h] * eg[h]
                   + _rowsum(w2[h]) - _rowsum(w2[h].T) + jnp.where(rowq == q - 1, dgl[h], 0.0))
            dbeta = _rowsum(dr[h][:, :128] * v[h]) + sk[h] * eg[h] + _rowsum(da[h] * ak[h])
            qhat = hd["qr"][h] * hd["rq"][h]
            dqhat = dq[h] * GDN_SCALE
            dact_ref[:, hs] = hd["rq"][h] * (dqhat - qhat * _rowsum(dqhat * qhat))
            dact_ref[:, 1024 + h * 128: 1024 + (h + 1) * 128] = hd["rk"][h] * (dk[h] - kn[h] * _rowsum(dk[h] * kn[h]))
            dact_ref[:, 2048 + h * 128: 2048 + (h + 1) * 128] = dr[h][:, :128] * bcol[h]
            dgc_all = dgc_all + jnp.where(lane == LANE_A + h, dgc, 0.0)
            dbeta_all = dbeta_all + jnp.where(lane == LANE_B + h, dbeta, 0.0)
            ds_out.append(dsn[h] * egl[h] + ds12[h])
        dnw_ref[...] += dnw_acc
        dg = _hdot(cm["triu"], dgc_all)
        da_raw = jnp.where(cm["ma"], dg * garow_v * _sigmoid(cm["spre"]), 0.0)
        dal_ref[...] += _colsum(dg * cm["g"])
        dgb_ref[...] += _colsum(da_raw)
        beta = cm["beta"]
        dsm_ref[...] = (da_raw + dbeta_all * beta * (1.0 - beta)).astype(BF16)
        return ds_out

    cps = 2 if nc % 2 == 0 else 1
    qq = cps * q
    rev = lambda c: nc // cps - 1 - c
    vec128 = pl.BlockSpec((1, 128), lambda c: (0, 0))
    return pl.pallas_call(
        body, grid=(nc // cps,),
        in_specs=[pl.BlockSpec((qq, GDN_QKV), lambda c: (rev(c), 0)),
                  pl.BlockSpec((qq, 1024), lambda c: (rev(c), GZ_OFF // 1024)),
                  pl.BlockSpec((qq, 128), lambda c: (rev(c), SM_OFF // 128)), vec128, vec128, vec128,
                  pl.BlockSpec((qq, 1024), lambda c: (rev(c), 0)), pl.BlockSpec((cps, 1024, 128), lambda c: (rev(c), 0, 0)),
                  pl.BlockSpec((qq, 1024), lambda c: (rev(c), 0)), ANY],
        out_specs=(pl.BlockSpec((qq, GDN_QKV), lambda c: (rev(c), 0)),
                   pl.BlockSpec((qq, 1024), lambda c: (rev(c), GZ_OFF // 1024)),
                   pl.BlockSpec((qq, 128), lambda c: (rev(c), 0)), vec128, vec128, vec128),
        out_shape=(SDS((t, GDN_QKV), F32), SDS(dproj.shape, BF16), SDS((t, 128), BF16), SDS((1, 128), F32),
                   SDS((1, 128), F32), SDS((1, 128), F32)),
        input_output_aliases={9: 1},
        scratch_shapes=[pltpu.VMEM((1024, 128), F32)], name=name, compiler_params=_params(("arbitrary",)),
    )(act, proj, proj, gb, garow, gnw, oraw, states, dy, dproj)


def _row_tile(r):
    for cand in (512, 256, 128, 64, 32, 16, 8):
        if r % cand == 0:
            return cand
    return r


def _sum_terms(name, terms, out_dtype):
    shape = terms[0][0].shape[1:]
    c = shape[-1]
    r = 1
    for s in shape[:-1]:
        r *= s
    tr = min(_row_tile(r), 256)
    n = len(terms)

    def body(*refs):
        acc = refs[0][...].astype(F32)
        for k in range(1, n):
            acc = acc + refs[k][...].astype(F32)
        refs[n][...] = acc.astype(out_dtype)

    in_specs = [pl.BlockSpec((None, tr, c), lambda i, q=lead: (q, i, 0)) for _, lead in terms]
    args = [a.reshape(a.shape[0], r, c) for a, _ in terms]
    out = pl.pallas_call(body, grid=(r // tr,), in_specs=in_specs, out_specs=pl.BlockSpec((tr, c), lambda i: (i, 0)),
                         out_shape=SDS((r, c), out_dtype), name=name, compiler_params=_params(("parallel",)))(*args)
    return out.reshape(shape)


def _adamw(name, w, g, m, v):
    shape = w.shape
    c = shape[-1]
    per_layer = isinstance(g, (list, tuple))
    nl = len(g) if per_layer else 1
    gs = [a.reshape(-1, c) for a in g] if per_layer else [g.reshape(-1, c)]
    r = gs[0].shape[0]
    w3, m3, v3 = (a.reshape(nl, r, c) for a in (w, m, v))
    tr = min(_row_tile(r), 256)

    def body(*refs):
        w_ref, m_ref, v_ref = refs[:3]
        g_refs = refs[3: 3 + nl]
        go_ref, d_ref, nm_ref, nv_ref = refs[3 + nl:]
        layer = pl.program_id(0)
        gv = g_refs[0][...]
        for k in range(1, nl):
            gv = jnp.where(layer == k, g_refs[k][...], gv)
        mn = ADAM_B1 * m_ref[...] + (1.0 - ADAM_B1) * gv
        vn = ADAM_B2 * v_ref[...] + (1.0 - ADAM_B2) * (gv * gv)
        m_hat = mn / (1.0 - ADAM_B1 ** ADAM_STEP)
        v_hat = vn / (1.0 - ADAM_B2 ** ADAM_STEP)
        go_ref[...] = gv
        d_ref[...] = -ADAM_LR * (m_hat / (jnp.sqrt(v_hat) + ADAM_EPS) + ADAM_WD * w_ref[...])
        nm_ref[...] = mn
        nv_ref[...] = vn

    spec3 = pl.BlockSpec((None, tr, c), lambda l, i: (l, i, 0))
    gspec = pl.BlockSpec((tr, c), lambda l, i: (i, 0))
    outs = pl.pallas_call(body, grid=(nl, r // tr), in_specs=[spec3] * 3 + [gspec] * nl, out_specs=(spec3,) * 4,
                          out_shape=(SDS((nl, r, c), F32),) * 4, name=name,
                          compiler_params=_params(("parallel", "parallel")))(w3, m3, v3, *gs)
    return tuple(o.reshape(shape) for o in outs)


ANY = pl.BlockSpec(memory_space=pl.ANY)
MESH = pl.DeviceIdType.MESH


def _allgather(name, xs):
    n = len(xs)

    def body(*refs):
        x_refs, out_refs = refs[:n], refs[n: 2 * n]
        send_sems, recv_sems, local_sems = refs[2 * n:]
        x, y, cc = lax.axis_index("x"), lax.axis_index("y"), lax.axis_index("c")
        me, sibling = (x, y, cc), (x, y, 1 - cc)
        chips = [(1 - x, y), (x, 1 - y), (1 - x, 1 - y)]

        def rows(a, px, py, pc):
            return out_refs[a].at[4 * px + 2 * py + pc]

        def copy(a, k, block, to, src=None):
            return pltpu.make_async_remote_copy(
                src_ref=rows(a, *block) if src is None else src, dst_ref=rows(a, *block),
                send_sem=send_sems.at[7 * a + k], recv_sem=recv_sems.at[7 * a + k], device_id=to, device_id_type=MESH)

        mine = [pltpu.make_async_copy(x_refs[a], rows(a, *me), local_sems.at[a]) for a in range(n)]
        for cp in mine:
            cp.start()
        first = []
        for a in range(n):
            first.append(copy(a, 0, me, sibling, src=x_refs[a]))
            first += [copy(a, 1 + j, me, (*chip, cc), src=x_refs[a]) for j, chip in enumerate(chips)]
        for cp in first:
            cp.start()
        passed = []
        for j, chip in enumerate(chips):
            for a in range(n):
                copy(a, 1 + j, (*chip, cc), me).wait_recv()
                fwd = copy(a, 4 + j, (*chip, cc), sibling)
                fwd.start()
                passed.append(fwd)
        for a in range(n):
            copy(a, 0, sibling, me).wait_recv()
        for j, chip in enumerate(chips):
            for a in range(n):
                copy(a, 4 + j, (*chip, 1 - cc), me).wait_recv()
        for cp in first + passed:
            cp.wait_send()
        for cp in mine:
            cp.wait()

    return pl.pallas_call(
        body, out_shape=tuple(SDS((N_DEV,) + a.shape, a.dtype) for a in xs), in_specs=[ANY] * n, out_specs=(ANY,) * n,
        scratch_shapes=[pltpu.SemaphoreType.DMA((7 * n,)), pltpu.SemaphoreType.DMA((7 * n,)),
                        pltpu.SemaphoreType.DMA((n,))],
        name=name,
    )(*xs)


def _allgather_seq(name, xs, collective_id):
    n = len(xs)
    x_refs = [jax.new_ref(a, memory_space=pltpu.MemorySpace.HBM) for a in xs]
    out_refs = [jax.empty_ref(SDS((N_DEV,) + a.shape, a.dtype), memory_space=pltpu.MemorySpace.HBM) for a in xs]

    @pl.kernel(mesh=plsc.ScalarSubcoreMesh(axis_name="seq", num_cores=1), name=name,
               scratch_types=(pltpu.SemaphoreType.DMA((7 * n,)), pltpu.SemaphoreType.DMA((7 * n,)),
                              pltpu.SemaphoreType.DMA((n,))),
               compiler_params=pltpu.CompilerParams(collective_id=collective_id))
    def launch(send_sems, recv_sems, local_sems):
        x, y, cc = lax.axis_index("x"), lax.axis_index("y"), lax.axis_index("c")
        me, sibling = (x, y, cc), (x, y, 1 - cc)
        chips = [(1 - x, y), (x, 1 - y), (1 - x, 1 - y)]
        barrier = pltpu.get_barrier_semaphore()
        for peer in [sibling] + [(*chip, cc) for chip in chips]:
            pl.semaphore_signal(barrier, inc=1, device_id=peer, device_id_type=MESH)
        pl.semaphore_wait(barrier, 4)

        def rows(a, px, py, pc):
            return out_refs[a].at[4 * px + 2 * py + pc]

        def copy(a, k, block, to, src=None):
            return pltpu.make_async_remote_copy(
                src_ref=rows(a, *block) if src is None else src, dst_ref=rows(a, *block),
                send_sem=send_sems.at[7 * a + k], recv_sem=recv_sems.at[7 * a + k], device_id=to, device_id_type=MESH)

        mine = [pltpu.make_async_copy(x_refs[a], rows(a, *me), local_sems.at[a]) for a in range(n)]
        for cp in mine:
            cp.start()
        first = []
        for a in range(n):
            first.append(copy(a, 0, me, sibling, src=x_refs[a]))
            first += [copy(a, 1 + j, me, (*chip, cc), src=x_refs[a]) for j, chip in enumerate(chips)]
        for cp in first:
            cp.start()
        passed = []
        for j, chip in enumerate(chips):
            for a in range(n):
                copy(a, 1 + j, (*chip, cc), me).wait_recv()
                fwd = copy(a, 4 + j, (*chip, cc), sibling)
                fwd.start()
                passed.append(fwd)
        for a in range(n):
            copy(a, 0, sibling, me).wait_recv()
        for j, chip in enumerate(chips):
            for a in range(n):
                copy(a, 4 + j, (*chip, 1 - cc), me).wait_recv()
        for cp in first + passed:
            cp.wait_send()
        for cp in mine:
            cp.wait()

    launch()
    return [r[...] for r in out_refs]


HBM = pl.BlockSpec(memory_space=pltpu.HBM)
SEM = pl.BlockSpec(memory_space=pltpu.SEMAPHORE)
EFFECT = pltpu.SideEffectType.DATAFLOW_SIDE_EFFECTING


def _sibling_plan(srcs, lands, send_sems, recv_sems):
    x, y, cc = lax.axis_index("x"), lax.axis_index("y"), lax.axis_index("c")
    return [pltpu.make_async_remote_copy(
        src_ref=srcs[a].at[2 * q + 1 - cc], dst_ref=lands[a].at[q], send_sem=send_sems.at[4 * a + q],
        recv_sem=recv_sems.at[4 * a + q], device_id=(x, y, 1 - cc), device_id_type=MESH)
        for a in range(len(srcs)) for q in range(4)]


def _chips_plan(srcs, lands, send_sems, recv_sems):
    x, y, cc = lax.axis_index("x"), lax.axis_index("y"), lax.axis_index("c")
    chips = [(1 - x, y), (x, 1 - y), (1 - x, 1 - y)]
    return [pltpu.make_async_remote_copy(
        src_ref=srcs[a].at[2 * px + py], dst_ref=lands[a].at[j], send_sem=send_sems.at[3 * a + j],
        recv_sem=recv_sems.at[3 * a + j], device_id=(px, py, cc), device_id_type=MESH)
        for a in range(len(srcs)) for j, (px, py) in enumerate(chips)]


def _copies_start(name, plan, per_array, srcs, land_lead):
    n = len(srcs)
    k = per_array * n

    def body(*refs):
        src_refs, land_refs = refs[:n], refs[n: 2 * n]
        send_sems, recv_sems = refs[2 * n], refs[2 * n + 1]
        token = refs[-1]
        for cp in plan(src_refs, land_refs, send_sems, recv_sems):
            cp.start()
        token[...] = jnp.zeros_like(token)

    lands = [lax.empty((land_lead,) + a.shape[1:], a.dtype) for a in srcs]
    outs = pl.pallas_call(
        body, name=name,
        out_shape=(pltpu.SemaphoreType.DMA((k,)), pltpu.SemaphoreType.DMA((k,)),
                   *[pltpu.HBM(a.shape, a.dtype) for a in srcs], *[pltpu.HBM(a.shape, a.dtype) for a in lands],
                   SDS((8, 128), F32)),
        in_specs=[HBM] * (2 * n), out_specs=(SEM, SEM, *[HBM] * (2 * n), pl.BlockSpec(memory_space=pltpu.VMEM)),
        input_output_aliases={i: 2 + i for i in range(2 * n)},
        compiler_params=pltpu.CompilerParams(has_side_effects=EFFECT),
    )(*[pltpu.with_memory_space_constraint(a, pltpu.HBM) for a in srcs],
      *[pltpu.with_memory_space_constraint(a, pltpu.HBM) for a in lands])
    return outs[0], outs[1], list(outs[2: 2 + n]), list(outs[2 + n: 2 + 2 * n]), outs[-1]


def _copies_wait(name, plan, started, after):
    send_sems, recv_sems, srcs, lands, _ = started
    n = len(srcs)
    after = tuple(after)

    def body(*refs):
        src_refs, land_refs = refs[:n], refs[n: 2 * n]
        for cp in plan(src_refs, land_refs, refs[2 * n], refs[2 * n + 1]):
            cp.wait_send()
            cp.wait_recv()

    outs = pl.pallas_call(
        body, name=name,
        out_shape=tuple(pltpu.HBM(a.shape, a.dtype) for a in srcs + lands),
        in_specs=[HBM] * (2 * n) + [SEM, SEM] + [ANY] * len(after), out_specs=(HBM,) * (2 * n),
        input_output_aliases={i: i for i in range(2 * n)},
        compiler_params=pltpu.CompilerParams(has_side_effects=EFFECT),
    )(*srcs, *lands, send_sems, recv_sems, *after)
    return list(outs[n:])


BIG = (("w_in", 1), ("w_ffn_in", 1), ("w_proj_ssm", 0), ("w_proj_gdn", 0), ("w_out", 0), ("w_ffn_down", 0))
CONVS = (("ssm_conv_w", 1), ("gdn_conv_w", 1))


def _to_dest_major(full, axis):
    a, b = full.shape
    if axis == 0:
        return full.reshape(N_DEV, a // N_DEV, b)
    s = b // N_DEV
    return jnp.stack([full[:, d * s: (d + 1) * s] for d in range(N_DEV)])


def _from_gathered(g, axis):
    if axis == 0:
        return g.reshape(-1, g.shape[2])
    return jnp.concatenate([g[d] for d in range(N_DEV)], axis=1)


IN_RUNS = ((Z_OFF, O_Z, 1024), (GZ_OFF, O_GZ, 1024), (G1_OFF, O_G1, 1024), (G2_OFF, O_G2, 1024), (QKV_OFF, O_QKV, 3072),
           (XBC_OFF, O_XBC, 1536), (SM_OFF, O_DT, 16), (SM_OFF + LANE_A, O_A, 8), (SM_OFF + LANE_B, O_B, 8))
IN_SHARD = IN_DIM // N_DEV


def _w_in_from_blocks(g):
    rows = g.shape[1]
    parts, pos = [], 0
    for off, o0, width in IN_RUNS:
        if off > pos:
            parts.append(jnp.zeros((rows, off - pos), g.dtype))
        c = o0
        while c < o0 + width:
            d = c // IN_SHARD
            hi = min(o0 + width, (d + 1) * IN_SHARD)
            parts.append(g[d][:, c - d * IN_SHARD: hi - d * IN_SHARD])
            c = hi
        pos = off + width
    parts.append(jnp.zeros((rows, PROJ_W - pos), g.dtype))
    return jnp.concatenate(parts, axis=1)


def _w_in_to_blocks(wp):
    by_orig = sorted(IN_RUNS, key=lambda r: r[1])
    blocks = []
    for d in range(N_DEV):
        lo, hi = d * IN_SHARD, (d + 1) * IN_SHARD
        parts = []
        for off, o0, width in by_orig:
            a, b = max(lo, o0), min(hi, o0 + width)
            if a < b:
                parts.append(wp[:, off + a - o0: off + b - o0])
        blocks.append(jnp.concatenate(parts, axis=1))
    return jnp.stack(blocks)


def _pad128(v, lane0):
    return jnp.zeros((1, 128), F32).at[0, lane0: lane0 + v.shape[0]].set(v)


def _layer_consts(p):
    return dict(
        dtb=_pad128(p["ssm_dt_bias"], 0), arow=_pad128(-jnp.exp(p["ssm_a_log"]), 0),
        dxrow=jnp.repeat(p["ssm_d"], SSM_P).reshape(1, 1024), snw=p["ssm_norm_w"].reshape(1, 1024),
        gb=_pad128(p["gdn_dt_bias"], LANE_A), garow=_pad128(-jnp.exp(p["gdn_a_log"]), LANE_A),
        gnw=p["gdn_norm_w"].reshape(1, 128), zb=jnp.zeros((1, GDN_QKV), F32), scb=p["ssm_conv_b"].reshape(1, SSM_CONV))


def _expand_matrix():
    row = lax.broadcasted_iota(jnp.int32, (128, 1024), 0)
    col = lax.broadcasted_iota(jnp.int32, (128, 1024), 1)
    return (col // SSM_P == row).astype(BF16)


def _silu_mul_epi(acc, up):
    g = acc
    return g, g * _sigmoid(g) * up.astype(F32)


def _merge_epi(acc, p1, g1, g2):
    return acc, _sigmoid(g1) * p1.astype(F32) + _sigmoid(g2) * acc


def _add_epi(acc, res):
    return (acc + res,)


def _ffn_bwd_epi(acc, gate, up):
    g = gate.astype(F32)
    sg = _sigmoid(g)
    return acc * up.astype(F32) * (sg * (1.0 + g * (1.0 - sg))), acc * (g * sg)


def _merge_bwd_epi(acc, g1, g2, p1, p2):
    s1, s2 = _sigmoid(g1), _sigmoid(g2)
    dg1, dg2 = acc * p1.astype(F32) * (s1 * (1.0 - s1)), acc * p2.astype(F32) * (s2 * (1.0 - s2))
    return acc * s1, acc * s2, jnp.concatenate([dg1, dg2], axis=1)


def _layer_fwd(l, x, p, rmat):
    t = x.shape[0]
    n = f"l{l}_"
    k = _layer_consts(p)
    h = _rmsnorm_fwd(n + "norm_mix", x, p["norm_mix_w"])
    proj = _matmul(n + "in_proj", "nn", [(h, 0, p["w_in"], 0)], t, PROJ_W, 1024, 1024, 1280, 1024, (F32,))
    act_g = _conv_fwd(n + "conv_gdn", proj, QKV_OFF, p["gdn_conv_w"], k["zb"])
    act_s = _conv_fwd(n + "conv_ssm", proj, XBC_OFF, p["ssm_conv_w"], k["scb"])
    y_ssm, ysc, st_s = _ssd_fwd(n + "ssd_fwd", act_s, proj, k["dtb"], k["arow"], k["dxrow"], k["snw"], rmat)
    y_gdn, oraw, st_g = _gdn_fwd(n + "gdn_fwd", act_g, proj, k["gb"], k["garow"], k["gnw"])
    if "late" in p:
        y_gdn, late = p["late"](y_gdn)
        p = {**p, **late}
    p1 = _matmul(n + "proj_ssm", "nn", [(y_ssm, 0, p["w_proj_ssm"], 0)], t, 1024, 1024, 1024, 1024, 1024, (BF16,))
    p2, merged = _matmul(n + "proj_gdn_merge", "nn", [(y_gdn, 0, p["w_proj_gdn"], 0)], t, 1024, 1024, 512, 1024, 1024,
                         (BF16, BF16), epi=_merge_epi, extras=[(p1, 0), (proj, G1_OFF // 1024), (proj, G2_OFF // 1024)])
    x1 = _matmul(n + "out_proj", "nn", [(merged, 0, p["w_out"], 0)], t, 1024, 1024, 1024, 1024, 1024, (F32,),
                 epi=_add_epi, extras=[(x, 0)])
    h2 = _rmsnorm_fwd(n + "norm_ffn", x1, p["norm_ffn_w"])
    up = _matmul(n + "ffn_up", "nn", [(h2, 0, p["w_ffn_in"], 2)], t, FFN, 1024, 1024, FFN // 2, 1024, (BF16,))
    gate, act = _matmul(n + "ffn_gate", "nn", [(h2, 0, p["w_ffn_in"], 0)], t, FFN, 1024, 1024, FFN // 2, 1024, (BF16, BF16),
                        epi=_silu_mul_epi, extras=[(up, 0)])
    x2 = _matmul(n + "ffn_down", "nn", [(act, 0, p["w_ffn_down"], 0)], t, 1024, FFN, 1024, 1024, FFN, (F32,),
                 epi=_add_epi, extras=[(x1, 0)])
    saved = dict(x=x, h=h, proj=proj, act_g=act_g, act_s=act_s, y_ssm=y_ssm, ysc=ysc, st_s=st_s, y_gdn=y_gdn, oraw=oraw,
                 st_g=st_g, p1=p1, p2=p2, merged=merged, x1=x1, h2=h2, up=up, gate=gate, act=act, k=k, p=p)
    return x2, saved


def _layer_bwd(l, dx2, dx2b, s, p, rmat, hooks):
    t = dx2.shape[0]
    n = f"l{l}_"
    k = s["k"]
    tk_tok = 1024
    hf = FFN // 2
    g = {}
    dgate, dup = _matmul(n + "d_ffn_act", "nt", [(dx2b, 0, p["w_ffn_down"], 0)], t, FFN, 1024, 1024, hf, 1024, (BF16, BF16),
                         epi=_ffn_bwd_epi, extras=[(s["gate"], 0), (s["up"], 0)])
    g["w_ffn_down"] = _matmul(n + "dw_ffn_down", "tn", [(s["act"], 0, dx2b, 0)], FFN, 1024, t, hf, 1024, tk_tok, (BF16,))
    dh2 = _matmul(n + "d_ffn_in", "nt", [(dgate, 0, p["w_ffn_in"], 0), (dup, 0, p["w_ffn_in"], 2)], t, 1024, FFN,
                  1024, 1024, hf, (F32,))
    dwg = _matmul(n + "dw_ffn_gate", "tn", [(s["h2"], 0, dgate, 0)], 1024, FFN, t, 1024, hf, tk_tok, (BF16,))
    dwu = _matmul(n + "dw_ffn_up", "tn", [(s["h2"], 0, dup, 0)], 1024, FFN, t, 1024, hf, tk_tok, (BF16,))
    g["w_ffn_in"] = jnp.concatenate([dwg, dwu], axis=1)
    dx1, dx1b, g["norm_ffn_w"] = _rmsnorm_bwd(n + "d_norm_ffn", s["x1"], p["norm_ffn_w"], dh2, dx2)
    dx1b = hooks.ffn_done(dx1b)
    dp1, dp2, dproj = _matmul(
        n + "d_out_proj", "nt", [(dx1b, 0, p["w_out"], 0)], t, 1024, 1024, 512, 1024, 1024,
        (BF16, BF16, (BF16, PROJ_W, 2048, G1_OFF // 2048)), epi=_merge_bwd_epi,
        extras=[(s["proj"], G1_OFF // 1024), (s["proj"], G2_OFF // 1024), (s["p1"], 0), (s["p2"], 0)])
    g["w_out"] = _matmul(n + "dw_out", "tn", [(s["merged"], 0, dx1b, 0)], 1024, 1024, t, 1024, 1024, tk_tok, (BF16,))
    g["w_proj_ssm"] = _matmul(n + "dw_proj_ssm", "tn", [(s["y_ssm"], 0, dp1, 0)], 1024, 1024, t, 1024, 1024, tk_tok, (BF16,))
    g["w_proj_gdn"] = _matmul(n + "dw_proj_gdn", "tn", [(s["y_gdn"], 0, dp2, 0)], 1024, 1024, t, 1024, 1024, tk_tok, (BF16,))
    dp1, dp2 = hooks.early_ready(l, g, dp1, dp2)
    dy_ssm = _matmul(n + "d_proj_ssm", "nt", [(dp1, 0, p["w_proj_ssm"], 0)], t, 1024, 1024, 1024, 1024, 1024, (F32,))
    dy_gdn = _matmul(n + "d_proj_gdn", "nt", [(dp2, 0, p["w_proj_gdn"], 0)], t, 1024, 1024, 1024, 1024, 1024, (F32,))
    dact_s, dproj, dsm_s, dsnw, dd, dal, ddtb = _ssd_bwd(n + "ssd_bwd", s["act_s"], s["proj"], k["dtb"], k["arow"],
                                                           k["dxrow"], k["snw"], rmat, s["ysc"], s["st_s"], dy_ssm, dproj)
    dact_g, dproj, dsm_g, dgnw, dgal, dgb = _gdn_bwd(n + "gdn_bwd", s["act_g"], s["proj"], k["gb"], k["garow"], k["gnw"],
                                                       s["oraw"], s["st_g"], dy_gdn, dproj)
    dsm_s = hooks.mixers_done(dsm_s)
    dproj = _place_small(n + "d_small", dsm_s, dsm_g, dproj)
    dproj, g["ssm_conv_w"], dcb = _conv_bwd(n + "d_conv_ssm", s["proj"], XBC_OFF, p["ssm_conv_w"], k["scb"], dact_s, dproj)
    dproj, g["gdn_conv_w"], _ = _conv_bwd(n + "d_conv_gdn", s["proj"], QKV_OFF, p["gdn_conv_w"], k["zb"], dact_g, dproj)
    g["ssm_conv_b"] = dcb.reshape(-1)
    g["ssm_norm_w"] = dsnw.reshape(-1)
    g["ssm_d"] = dd[0, :SSM_HEADS]
    g["ssm_a_log"] = dal[0, :SSM_HEADS]
    g["ssm_dt_bias"] = ddtb[0, :SSM_HEADS]
    g["gdn_norm_w"] = dgnw.reshape(-1)
    g["gdn_a_log"] = dgal[0, LANE_A: LANE_A + GDN_HEADS]
    g["gdn_dt_bias"] = dgb[0, LANE_A: LANE_A + GDN_HEADS]
    dh = _matmul(n + "d_in_proj", "nt", [(dproj, 0, p["w_in"], 0)], t, 1024, PROJ_W, 1024, 1024, 1280, (F32,))
    g["w_in"] = _matmul(n + "dw_in", "tn", [(s["h"], 0, dproj, 0)], 1024, PROJ_W, t, 1024, 1280, tk_tok, (BF16,))
    dx, dxb, g["norm_mix_w"] = _rmsnorm_bwd(n + "d_norm_mix", s["x"], p["norm_mix_w"], dh, dx1)
    g["norm_mix_w"] = g["norm_mix_w"].reshape(-1)
    g["norm_ffn_w"] = g["norm_ffn_w"].reshape(-1)
    return dx, dxb, g


def _local_step(x, tgt, layers, final_norm_w, reduce=False):
    rmat = _expand_matrix()
    saved, params = [], []
    for l in range(DEPTH):
        x, p = layers[l](x)
        x, s = _layer_fwd(l, x, p, rmat)
        saved.append(s)
        params.append(s["p"])
    loss, dx, dxb, dfw = _loss_head("loss_head", x, final_norm_w, tgt)
    grads = [None] * DEPTH
    hooks = _ReduceBesideBackward() if reduce else _NoReduce()
    for l in reversed(range(DEPTH)):
        dx, dxb, grads[l] = _layer_bwd(l, dx, dxb, saved[l], params[l], rmat, hooks)
        if reduce:
            dxb = hooks.layer_done(l, grads[l], dxb)
    if reduce:
        hooks.finish(dxb)
    return loss[0, 0], dx, grads, dfw.reshape(-1), hooks.shards if reduce else None


SMALL = ("norm_mix_w", "ssm_conv_b", "ssm_dt_bias", "ssm_a_log", "ssm_d", "ssm_norm_w", "gdn_a_log", "gdn_dt_bias",
         "gdn_norm_w", "norm_ffn_w")
WEIGHTS = ("norm_mix_w", "w_in", "ssm_conv_w", "ssm_conv_b", "ssm_dt_bias", "ssm_a_log", "ssm_d", "ssm_norm_w", "gdn_conv_w",
           "gdn_a_log", "gdn_dt_bias", "gdn_norm_w", "w_proj_ssm", "w_proj_gdn", "w_out", "norm_ffn_w", "w_ffn_in",
           "w_ffn_down", "final_norm_w")


FIRST_USED = ("w_in", "ssm_conv_w", "gdn_conv_w")


def _gather_layer(l, w):
    conv_names = [nm for nm, _ in CONVS]
    groups = ([s for s in BIG + CONVS if s[0] in FIRST_USED], [s for s in BIG + CONVS if s[0] not in FIRST_USED])
    gathered = []
    for i, (specs, tag) in enumerate(zip(groups, ("first", "rest"))):
        shards = [w[nm][l] if nm in conv_names else w[nm][l].astype(BF16) for nm, _ in specs]
        gathered.append(_allgather_seq(f"l{l}_gather_{tag}", shards, collective_id=2 * l + i))
    small = {nm: w[nm][l] for nm in SMALL}

    def use(i, act):
        act, blocks = lax.optimization_barrier((act, gathered[i]))
        return act, {nm: _w_in_from_blocks(g) if nm == "w_in" else _from_gathered(g, axis)
                     for (nm, axis), g in zip(groups[i], blocks)}

    def full_weights(x):
        x, out = use(0, x)
        out.update(small)
        out["late"] = lambda y: use(1, y)
        return x, out

    return full_weights


EARLY_GRADS = ("w_ffn_down", "w_ffn_in", "w_out", "w_proj_ssm", "w_proj_gdn")


class _GradReduceScatter:
    def __init__(self, tag, specs, grads):
        self.tag = tag
        self.specs = specs
        self.blocks = [_w_in_to_blocks(grads[nm]) if nm == "w_in" else _to_dest_major(grads[nm], axis)
                       for nm, axis in specs]

    def _tied(self, started, acts):
        *acts, self.token = lax.optimization_barrier((*acts, started[4]))
        return acts

    def start(self, *acts):
        cc = lax.axis_index("c")
        self.keep = [lax.dynamic_index_in_dim(b.reshape((4, 2) + b.shape[1:]), cc, axis=1, keepdims=False)
                     for b in self.blocks]
        self.to_sibling = _copies_start(f"{self.tag}_to_sibling_start", _sibling_plan, 4, self.blocks, 4)
        return self._tied(self.to_sibling, acts)

    def mid(self, *acts):
        got = _copies_wait(f"{self.tag}_to_sibling_wait", _sibling_plan, self.to_sibling, (acts[0], self.token))
        chip_sums = [_sum_terms(f"{self.tag}_chip_sum_{nm}", [(k[None], 0), (g[None], 0)], BF16)
                     for (nm, _), k, g in zip(self.specs, self.keep, got)]
        self.to_chips = _copies_start(f"{self.tag}_between_chips_start", _chips_plan, 3, chip_sums, 3)
        return self._tied(self.to_chips, acts)

    def end(self, after):
        landed = _copies_wait(f"{self.tag}_between_chips_wait", _chips_plan, self.to_chips, (after, self.token))
        my_chip = 2 * lax.axis_index("x") + lax.axis_index("y")
        own = [lax.dynamic_index_in_dim(s, my_chip, axis=0, keepdims=True) for s in self.to_chips[2]]
        return {nm: _sum_terms(f"{self.tag}_total_{nm}", [(o, 0), (e, 0), (e, 1), (e, 2)], F32)
                for (nm, _), o, e in zip(self.specs, own, landed)}


class _NoReduce:
    def ffn_done(self, dx1b):
        return dx1b

    def early_ready(self, l, g, dp1, dp2):
        return dp1, dp2

    def mixers_done(self, dsm):
        return dsm


class _ReduceBesideBackward(_NoReduce):
    def __init__(self):
        self.late = None
        self.early = None
        self.shards = [dict() for _ in range(DEPTH)]

    def ffn_done(self, dx1b):
        if self.late is not None:
            (dx1b,) = self.late.mid(dx1b)
        return dx1b

    def early_ready(self, l, g, dp1, dp2):
        self.early = _GradReduceScatter(f"l{l}_early_grads", [s for s in BIG if s[0] in EARLY_GRADS], g)
        return self.early.start(dp1, dp2)

    def mixers_done(self, dsm):
        (dsm,) = self.early.mid(dsm)
        return dsm

    def layer_done(self, l, g, dxb):
        if self.late is not None:
            self.shards[l + 1].update(self.late.end(dxb))
        self.shards[l].update(self.early.end(dxb))
        self.late = _GradReduceScatter(f"l{l}_late_grads", [s for s in BIG + CONVS if s[0] not in EARLY_GRADS], g)
        (dxb,) = self.late.start(dxb)
        return dxb

    def finish(self, dxb):
        (dxb,) = self.late.mid(dxb)
        self.shards[0].update(self.late.end(dxb))
        return dxb


def _allreduce_small(vecs):
    flat = jnp.concatenate(vecs)
    n = flat.shape[0]
    rows = -(-n // 128)
    rows = -(-rows // 8) * 8
    buf = jnp.pad(flat, (0, rows * 128 - n)).reshape(rows, 128)
    (allv,) = _allgather("gather_small_grads", [buf])
    tot = _sum_terms("small_grads_total", [(allv, d) for d in range(N_DEV)], F32).reshape(-1)
    out, o = [], 0
    for v in vecs:
        out.append(tot[o: o + v.shape[0]])
        o += v.shape[0]
    return out


def kernel(x, norm_mix_w, w_in, ssm_conv_w, ssm_conv_b, ssm_dt_bias, ssm_a_log, ssm_d, ssm_norm_w, gdn_conv_w, gdn_a_log, gdn_dt_bias, gdn_norm_w, w_proj_ssm, w_proj_gdn, w_out, norm_ffn_w, w_ffn_in, w_ffn_down, final_norm_w, loss_target, m_norm_mix_w, m_w_in, m_ssm_conv_w, m_ssm_conv_b, m_ssm_dt_bias, m_ssm_a_log, m_ssm_d, m_ssm_norm_w, m_gdn_conv_w, m_gdn_a_log, m_gdn_dt_bias, m_gdn_norm_w, m_w_proj_ssm, m_w_proj_gdn, m_w_out, m_norm_ffn_w, m_w_ffn_in, m_w_ffn_down, m_final_norm_w, v_norm_mix_w, v_w_in, v_ssm_conv_w, v_ssm_conv_b, v_ssm_dt_bias, v_ssm_a_log, v_ssm_d, v_ssm_norm_w, v_gdn_conv_w, v_gdn_a_log, v_gdn_dt_bias, v_gdn_norm_w, v_w_proj_ssm, v_w_proj_gdn, v_w_out, v_norm_ffn_w, v_w_ffn_in, v_w_ffn_down, v_final_norm_w):
    w = dict(norm_mix_w=norm_mix_w, w_in=w_in, ssm_conv_w=ssm_conv_w, ssm_conv_b=ssm_conv_b, ssm_dt_bias=ssm_dt_bias,
             ssm_a_log=ssm_a_log, ssm_d=ssm_d, ssm_norm_w=ssm_norm_w, gdn_conv_w=gdn_conv_w, gdn_a_log=gdn_a_log,
             gdn_dt_bias=gdn_dt_bias, gdn_norm_w=gdn_norm_w, w_proj_ssm=w_proj_ssm, w_proj_gdn=w_proj_gdn, w_out=w_out,
             norm_ffn_w=norm_ffn_w, w_ffn_in=w_ffn_in, w_ffn_down=w_ffn_down, final_norm_w=final_norm_w)
    m = dict(norm_mix_w=m_norm_mix_w, w_in=m_w_in, ssm_conv_w=m_ssm_conv_w, ssm_conv_b=m_ssm_conv_b, ssm_dt_bias=m_ssm_dt_bias,
             ssm_a_log=m_ssm_a_log, ssm_d=m_ssm_d, ssm_norm_w=m_ssm_norm_w, gdn_conv_w=m_gdn_conv_w, gdn_a_log=m_gdn_a_log,
             gdn_dt_bias=m_gdn_dt_bias, gdn_norm_w=m_gdn_norm_w, w_proj_ssm=m_w_proj_ssm, w_proj_gdn=m_w_proj_gdn,
             w_out=m_w_out, norm_ffn_w=m_norm_ffn_w, w_ffn_in=m_w_ffn_in, w_ffn_down=m_w_ffn_down,
             final_norm_w=m_final_norm_w)
    v = dict(norm_mix_w=v_norm_mix_w, w_in=v_w_in, ssm_conv_w=v_ssm_conv_w, ssm_conv_b=v_ssm_conv_b, ssm_dt_bias=v_ssm_dt_bias,
             ssm_a_log=v_ssm_a_log, ssm_d=v_ssm_d, ssm_norm_w=v_ssm_norm_w, gdn_conv_w=v_gdn_conv_w, gdn_a_log=v_gdn_a_log,
             gdn_dt_bias=v_gdn_dt_bias, gdn_norm_w=v_gdn_norm_w, w_proj_ssm=v_w_proj_ssm, w_proj_gdn=v_w_proj_gdn,
             w_out=v_w_out, norm_ffn_w=v_norm_ffn_w, w_ffn_in=v_w_ffn_in, w_ffn_down=v_w_ffn_down,
             final_norm_w=v_final_norm_w)

    layers = [_gather_layer(l, w) for l in range(DEPTH)]
    loss_part, dx, lgrads, dfw, shard_grads = _local_step(x[0], loss_target[0], layers, final_norm_w, reduce=True)
    loss = lax.psum(loss_part, ("x", "y", "c"))
    grad = {nm: [shard_grads[l][nm] for l in range(DEPTH)] for nm, _ in BIG + CONVS}
    small_vecs = [lgrads[l][nm].reshape(-1) for l in range(DEPTH) for nm in SMALL] + [dfw]
    small_sum = _allreduce_small(small_vecs)
    for i, nm in enumerate(SMALL):
        grad[nm] = jnp.stack([small_sum[l * len(SMALL) + i].reshape(w[nm].shape[1:]) for l in range(DEPTH)])
    grad["final_norm_w"] = small_sum[-1]

    deltas, new_m, new_v = {}, {}, {}
    for nm in WEIGHTS:
        grad[nm], deltas[nm], new_m[nm], new_v[nm] = _adamw("adamw_" + nm, w[nm], grad[nm], m[nm], v[nm])
    return (loss, dx[None], *[grad[nm] for nm in WEIGHTS], *[deltas[nm] for nm in WEIGHTS],
            *[new_m[nm] for nm in WEIGHTS], *[new_v[nm] for nm in WEIGHTS])
```

```python
import functools

import jax
import jax.numpy as jnp
from jax import lax
from jax.experimental import pallas as pl
from jax.experimental.pallas import tpu as pltpu
from jax.experimental.pallas import tpu_sc as plsc

F32 = jnp.float32
BF16 = jnp.bfloat16
HI = lax.Precision.HIGHEST
SDS = jax.ShapeDtypeStruct

D_MODEL = 1024
DEPTH = 2
SSM_HEADS = 16
SSM_P = 64
SSM_N = 128
SSM_GROUPS = 2
SSM_CONV = 1536
GDN_HEADS = 8
GDN_DK = 128
GDN_QKV = 3072
CONV_K = 4
CHUNK = 64
SCAN_CHUNKS_PER_STEP = 4
FFN = 2816
IN_DIM = 8736
EPS = 1e-6
N_DEV = 8

Z_OFF = 0
GZ_OFF = 1024
G1_OFF = 2048
G2_OFF = 3072
QKV_OFF = 4096
XBC_OFF = 7168
SM_OFF = 8704
PROJ_W = 8960
LANE_A = 16
LANE_B = 24
O_Z, O_XBC, O_DT, O_QKV, O_GZ, O_A, O_B, O_G1, O_G2 = 0, 1024, 2560, 2576, 5648, 6672, 6680, 6688, 7712

ADAM_LR = 0.001
ADAM_B1 = 0.9
ADAM_B2 = 0.999
ADAM_EPS = 1e-08
ADAM_WD = 0.01
ADAM_STEP = 10

V7X_VMEM_LIMIT = 48 * 1024 * 1024

NN = ((1,), (0,))
NT = ((1,), (1,))
TN = ((0,), (0,))


def _bdot(a, b, dims):
    return lax.dot_general(a.astype(BF16), b.astype(BF16), (dims, ((), ())), preferred_element_type=F32)


def _hdot(a, b, dims=NN):
    return lax.dot_general(a, b, (dims, ((), ())), precision=HI, preferred_element_type=F32)


def _sigmoid(x):
    return 1.0 / (1.0 + jnp.exp(-x))


def _softplus(x):
    return jnp.maximum(x, 0.0) + jnp.log(1.0 + jnp.exp(-jnp.abs(x)))


def _params(dims):
    return pltpu.CompilerParams(dimension_semantics=dims, vmem_limit_bytes=V7X_VMEM_LIMIT)


def _rowsum(x):
    return jnp.sum(x, axis=-1, keepdims=True)


def _colsum(x):
    return jnp.sum(x, axis=0, keepdims=True)


def _matmul(name, mode, pairs, m, n, kdim, tm, tn, tk, out_dtypes, epi=None, extras=()):
    tm, tn, tk = min(tm, m), min(tn, n), min(tk, kdim)
    nk = kdim // tk
    assert m % tm == 0 and n % tn == 0 and kdim % tk == 0, (name, m, n, kdim, tm, tn, tk)
    in_specs, args = [], []
    for a, a_off, b, b_off in pairs:
        if mode == "nn":
            in_specs.append(pl.BlockSpec((tm, tk), lambda i, j, k, o=a_off: (i, k + o)))
            in_specs.append(pl.BlockSpec((tk, tn), lambda i, j, k, o=b_off: (k, j + o)))
            dims = NN
        elif mode == "nt":
            in_specs.append(pl.BlockSpec((tm, tk), lambda i, j, k, o=a_off: (i, k + o)))
            in_specs.append(pl.BlockSpec((tn, tk), lambda i, j, k, o=b_off: (j, k + o)))
            dims = NT
        else:
            in_specs.append(pl.BlockSpec((tk, tm), lambda i, j, k, o=a_off: (k, i + o)))
            in_specs.append(pl.BlockSpec((tk, tn), lambda i, j, k, o=b_off: (k, j + o)))
            dims = TN
        args += [a, b]
    for e, e_off in extras:
        in_specs.append(pl.BlockSpec((tm, tn), lambda i, j, k, o=e_off: (i, j + o)))
        args.append(e)
    npair, nex, nout = len(pairs), len(extras), len(out_dtypes)

    def body(*refs):
        prefs = refs[: 2 * npair]
        erefs = refs[2 * npair: 2 * npair + nex]
        orefs = refs[2 * npair + nex: 2 * npair + nex + nout]

        def finish(res):
            outs = (res,) if epi is None else epi(res, *[e[...] for e in erefs])
            for o, r in zip(orefs, outs):
                o[...] = r.astype(o.dtype)

        s = _bdot(prefs[0][...], prefs[1][...], dims)
        for p in range(1, npair):
            s = s + _bdot(prefs[2 * p][...], prefs[2 * p + 1][...], dims)
        if nk == 1:
            finish(s)
            return
        acc = refs[-1]
        k = pl.program_id(2)

        @pl.when(k == 0)
        def _():
            acc[...] = s

        @pl.when(k > 0)
        def _():
            acc[...] += s

        @pl.when(k == nk - 1)
        def _():
            finish(acc[...])

    out_shape, out_specs = [], []
    for od in out_dtypes:
        if isinstance(od, tuple):
            dt, full_w, blk_w, cblk = od
            assert n == tn
            out_shape.append(SDS((m, full_w), dt))
            out_specs.append(pl.BlockSpec((tm, blk_w), lambda i, j, k, c=cblk: (i, c)))
        else:
            out_shape.append(SDS((m, n), od))
            out_specs.append(pl.BlockSpec((tm, tn), lambda i, j, k: (i, j)))
    out_shape, out_specs = tuple(out_shape), tuple(out_specs)
    res = pl.pallas_call(
        body, grid=(m // tm, n // tn, nk), in_specs=in_specs, out_specs=out_specs, out_shape=out_shape,
        scratch_shapes=[pltpu.VMEM((tm, tn), F32)] if nk > 1 else [], name=name,
        compiler_params=_params(("parallel", "parallel", "arbitrary")),
    )(*args)
    return res if nout > 1 else res[0]


def _rmsnorm_fwd(name, x, w):
    t, d = x.shape
    tm = min(512, t)

    def body(x_ref, w_ref, h_ref):
        xv = x_ref[...]
        r = lax.rsqrt(jnp.mean(xv * xv, axis=-1, keepdims=True) + EPS)
        h_ref[...] = (xv * r * w_ref[...]).astype(BF16)

    return pl.pallas_call(
        body, grid=(t // tm,),
        in_specs=[pl.BlockSpec((tm, d), lambda i: (i, 0)), pl.BlockSpec((1, d), lambda i: (0, 0))],
        out_specs=pl.BlockSpec((tm, d), lambda i: (i, 0)), out_shape=SDS((t, d), BF16), name=name,
        compiler_params=_params(("parallel",)),
    )(x, w.reshape(1, d))


def _rmsnorm_bwd(name, x, w, dh, dres):
    t, d = x.shape
    tm = min(512, t)

    def body(x_ref, w_ref, dh_ref, dres_ref, dx_ref, dxb_ref, dw_ref):
        xv = x_ref[...]
        r = lax.rsqrt(jnp.mean(xv * xv, axis=-1, keepdims=True) + EPS)
        xh = xv * r
        dhv = dh_ref[...].astype(F32)
        dxh = dhv * w_ref[...]
        dx = r * (dxh - xh * jnp.mean(dxh * xh, axis=-1, keepdims=True)) + dres_ref[...]
        dx_ref[...] = dx
        dxb_ref[...] = dx.astype(BF16)

        @pl.when(pl.program_id(0) == 0)
        def _():
            dw_ref[...] = jnp.zeros_like(dw_ref)

        dw_ref[...] += _colsum(dhv * xh)

    row = pl.BlockSpec((tm, d), lambda i: (i, 0))
    vec = pl.BlockSpec((1, d), lambda i: (0, 0))
    return pl.pallas_call(
        body, grid=(t // tm,), in_specs=[row, vec, row, row], out_specs=(row, row, vec),
        out_shape=(SDS((t, d), F32), SDS((t, d), BF16), SDS((1, d), F32)), name=name,
        compiler_params=_params(("arbitrary",)),
    )(x, w.reshape(1, d), dh, dres)


def _loss_head(name, x, w, tgt):
    t, d = x.shape
    tm = min(512, t)

    def body(x_ref, w_ref, t_ref, loss_ref, dx_ref, dxb_ref, dw_ref):
        xv = x_ref[...]
        wv = w_ref[...]
        r = lax.rsqrt(jnp.mean(xv * xv, axis=-1, keepdims=True) + EPS)
        xh = xv * r
        e = xh * wv - t_ref[...]
        dy = e * (1.0 / d)
        dxh = dy * wv
        dx = r * (dxh - xh * jnp.mean(dxh * xh, axis=-1, keepdims=True))
        dx_ref[...] = dx
        dxb_ref[...] = dx.astype(BF16)

        @pl.when(pl.program_id(0) == 0)
        def _():
            dw_ref[...] = jnp.zeros_like(dw_ref)
            loss_ref[...] = jnp.zeros_like(loss_ref)

        dw_ref[...] += _colsum(dy * xh)
        loss_ref[...] += 0.5 * jnp.sum(jnp.mean(e * e, axis=-1, keepdims=True), axis=0, keepdims=True)

    row = pl.BlockSpec((tm, d), lambda i: (i, 0))
    vec = pl.BlockSpec((1, d), lambda i: (0, 0))
    return pl.pallas_call(
        body, grid=(t // tm,), in_specs=[row, vec, row],
        out_specs=(pl.BlockSpec((1, 1), lambda i: (0, 0)), row, row, vec),
        out_shape=(SDS((1, 1), F32), SDS((t, d), F32), SDS((t, d), BF16), SDS((1, d), F32)), name=name,
        compiler_params=_params(("arbitrary",)),
    )(x, w.reshape(1, d), tgt)


def _shift_down(u, s, row):
    return jnp.where(row >= s, pltpu.roll(u, shift=s, axis=0), 0.0)


def _conv_fwd(name, src, col0, w, b):
    t = src.shape[0]
    c = w.shape[1]
    tc = 256
    assert c % tc == 0 and col0 % tc == 0

    def body(u_ref, w_ref, b_ref, o_ref):
        u = u_ref[...]
        wv = w_ref[...]
        row = lax.broadcasted_iota(jnp.int32, u.shape, 0)
        pre = b_ref[...] + wv[3:4, :] * u
        for s in range(1, CONV_K):
            pre = pre + wv[3 - s: 4 - s, :] * _shift_down(u, s, row)
        o_ref[...] = pre * _sigmoid(pre)

    return pl.pallas_call(
        body, grid=(c // tc,),
        in_specs=[pl.BlockSpec((t, tc), lambda j: (0, j + col0 // tc)), pl.BlockSpec((CONV_K, tc), lambda j: (0, j)),
                  pl.BlockSpec((1, tc), lambda j: (0, j))],
        out_specs=pl.BlockSpec((t, tc), lambda j: (0, j)), out_shape=SDS((t, c), F32), name=name,
        compiler_params=_params(("parallel",)),
    )(src, w, b)


def _place_small(name, dsm_a, dsm_b, dproj):
    t = dsm_a.shape[0]
    width = PROJ_W - SM_OFF
    tr = min(512, t)

    def body(a_ref, b_ref, dproj_ref, o_ref):
        o_ref[:, :128] = a_ref[...] + b_ref[...]
        o_ref[:, 128:] = jnp.zeros((tr, width - 128), BF16)

    row = pl.BlockSpec((tr, 128), lambda i: (i, 0))
    return pl.pallas_call(
        body, grid=(t // tr,), in_specs=[row, row, ANY],
        out_specs=pl.BlockSpec((tr, width), lambda i: (i, SM_OFF // width)), out_shape=SDS(dproj.shape, BF16),
        input_output_aliases={2: 0}, name=name, compiler_params=_params(("parallel",)),
    )(dsm_a, dsm_b, dproj)


def _conv_bwd(name, src, col0, w, b, dact, dproj):
    t = src.shape[0]
    c = w.shape[1]
    tc = 128

    def body(u_ref, w_ref, b_ref, da_ref, dproj_ref, du_ref, dw_ref, db_ref):
        u = u_ref[...]
        wv = w_ref[...]
        row = lax.broadcasted_iota(jnp.int32, u.shape, 0)
        shifted = [u] + [_shift_down(u, s, row) for s in range(1, CONV_K)]
        pre = b_ref[...] + wv[3:4, :] * u
        for s in range(1, CONV_K):
            pre = pre + wv[3 - s: 4 - s, :] * shifted[s]
        sg = _sigmoid(pre)
        dpre = da_ref[...] * (sg * (1.0 + pre * (1.0 - sg)))
        du = wv[3:4, :] * dpre
        for s in range(1, CONV_K):
            du = du + wv[3 - s: 4 - s, :] * jnp.where(row < t - s, pltpu.roll(dpre, shift=t - s, axis=0), 0.0)
        du_ref[...] = du.astype(BF16)
        for s in range(CONV_K):
            dw_ref[3 - s: 4 - s, :] = _colsum(dpre * shifted[s])
        db_ref[...] = _colsum(dpre)

    return pl.pallas_call(
        body, grid=(c // tc,),
        in_specs=[pl.BlockSpec((t, tc), lambda j: (0, j + col0 // tc)), pl.BlockSpec((CONV_K, tc), lambda j: (0, j)),
                  pl.BlockSpec((1, tc), lambda j: (0, j)), pl.BlockSpec((t, tc), lambda j: (0, j)), ANY],
        out_specs=(pl.BlockSpec((t, tc), lambda j: (0, j + col0 // tc)), pl.BlockSpec((CONV_K, tc), lambda j: (0, j)),
                   pl.BlockSpec((1, tc), lambda j: (0, j))),
        out_shape=(SDS(dproj.shape, BF16), SDS((CONV_K, c), F32), SDS((1, c), F32)), name=name,
        input_output_aliases={4: 0},
        compiler_params=_params(("parallel",)),
    )(src, w, b, dact, dproj)


def _tri(q):
    ii = lax.broadcasted_iota(jnp.int32, (q, q), 0)
    jj = lax.broadcasted_iota(jnp.int32, (q, q), 1)
    return ii, jj


def _dot01(x, r01, dims, terms=3):
    out, rem = None, x
    for i in range(terms):
        hi = rem.astype(BF16)
        d = lax.dot_general(hi, r01, (dims, ((), ())), preferred_element_type=F32)
        out = d if out is None else out + d
        if i + 1 < terms:
            rem = rem - hi.astype(F32)
    return out


def _ssd_common(act, sm, dtb, arow, rmat):
    q = CHUNK
    ii, jj = _tri(q)
    lane = lax.broadcasted_iota(jnp.int32, (q, 128), 1)
    m16 = lane < SSM_HEADS
    dt = jnp.where(m16, _softplus(sm + dtb), 0.0)
    a = dt * arow
    tril = (ii >= jj).astype(F32)
    triu = (ii <= jj).astype(F32)
    acum = _hdot(tril, a)
    acum_r = _hdot(a.T, triu)
    dtx = _dot01(dt, rmat, NN)
    acx = _dot01(acum, rmat, NN)
    ex = jnp.exp(acx)
    alx = acx[q - 1: q, :]
    dex = jnp.exp(alx - acx)
    xs = act[:, :1024]
    return dict(ii=ii, jj=jj, m16=m16, dt=dt, a=a, triu=triu, acum=acum, acum_r=acum_r, dtx=dtx, ex=ex, dex=dex,
                elx=jnp.exp(alx), xs=xs, x=xs * dtx)


def _ssd_lmat(cm, h):
    return jnp.where(cm["ii"] >= cm["jj"], jnp.exp(cm["acum"][:, h: h + 1] - cm["acum_r"][h: h + 1, :]), 0.0)


def _ssd_fwd(name, act, proj, dtb, arow, dxrow, nw, rmat):
    t = act.shape[0]
    q = CHUNK
    nc = t // q
    hg = SSM_HEADS // SSM_GROUPS
    gw = hg * SSM_P

    def body(act_ref, z_ref, sm_ref, dtb_ref, arow_ref, dx_ref, nw_ref, r_ref, y_ref, ys_ref, st_ref, s_scr, yd_scr):
        @pl.when(pl.program_id(0) == 0)
        def _():
            s_scr[...] = jnp.zeros_like(s_scr)

        s_all = s_scr[...]
        for sub in range(cps):
            rows = pl.ds(sub * q, q)
            s_all = chunk(act_ref.at[rows, :], z_ref.at[rows, :], sm_ref.at[rows, :], dtb_ref, arow_ref, dx_ref, nw_ref, r_ref,
                          y_ref.at[rows, :], ys_ref.at[rows, :], st_ref.at[sub], yd_scr.at[rows, :], s_all)
        s_scr[...] = s_all

    def chunk(act_ref, z_ref, sm_ref, dtb_ref, arow_ref, dx_ref, nw_ref, r_ref, y_ref, ys_ref, st_ref, yd_scr, s_all):
        st_ref[...] = s_all
        actv = act_ref[...]
        cm = _ssd_common(actv, sm_ref[...], dtb_ref[...], arow_ref[...], r_ref[...])
        x = cm["x"]
        xd = x * cm["dex"]
        yoffs, snew = [], []
        for g in range(SSM_GROUPS):
            bg = actv[:, 1024 + g * SSM_N: 1024 + (g + 1) * SSM_N]
            cg = actv[:, 1280 + g * SSM_N: 1280 + (g + 1) * SSM_N]
            sg = s_all[:, g * gw: (g + 1) * gw]
            cb = _bdot(cg, bg, NT)
            yoffs.append(_bdot(cg, sg, NN))
            snew.append(_bdot(bg, xd[:, g * gw: (g + 1) * gw], TN))
            for r in range(hg):
                h = g * hg + r
                mm = cb * _ssd_lmat(cm, h)
                yd_scr[:, h * SSM_P: (h + 1) * SSM_P] = _bdot(mm, x[:, h * SSM_P: (h + 1) * SSM_P], NN)
        s_next = s_all * cm["elx"] + jnp.concatenate(snew, axis=1)
        ysc = yd_scr[...] + jnp.concatenate(yoffs, axis=1) * cm["ex"]
        ys_ref[...] = ysc
        zv = z_ref[...]
        yg = (ysc + dx_ref[...] * cm["xs"]) * (zv * _sigmoid(zv))
        nwv = nw_ref[...]
        for g in range(SSM_GROUPS):
            sl = yg[:, g * gw: (g + 1) * gw]
            rr = lax.rsqrt(jnp.mean(sl * sl, axis=-1, keepdims=True) + EPS)
            y_ref[:, g * gw: (g + 1) * gw] = (sl * rr * nwv[:, g * gw: (g + 1) * gw]).astype(BF16)
        return s_next

    cps = SCAN_CHUNKS_PER_STEP if nc % SCAN_CHUNKS_PER_STEP == 0 else 1
    qq = cps * q
    vec128 = pl.BlockSpec((1, 128), lambda c: (0, 0))
    vec1k = pl.BlockSpec((1, 1024), lambda c: (0, 0))
    return pl.pallas_call(
        body, grid=(nc // cps,),
        in_specs=[pl.BlockSpec((qq, SSM_CONV), lambda c: (c, 0)), pl.BlockSpec((qq, 1024), lambda c: (c, Z_OFF // 1024)),
                  pl.BlockSpec((qq, 128), lambda c: (c, SM_OFF // 128)), vec128, vec128, vec1k, vec1k,
                  pl.BlockSpec((128, 1024), lambda c: (0, 0))],
        out_specs=(pl.BlockSpec((qq, 1024), lambda c: (c, 0)), pl.BlockSpec((qq, 1024), lambda c: (c, 0)),
                   pl.BlockSpec((cps, 128, 1024), lambda c: (c, 0, 0))),
        out_shape=(SDS((t, 1024), BF16), SDS((t, 1024), F32), SDS((nc, 128, 1024), F32)),
        scratch_shapes=[pltpu.VMEM((128, 1024), F32), pltpu.VMEM((qq, 1024), F32)], name=name,
        compiler_params=_params(("arbitrary",)),
    )(act, proj, proj, dtb, arow, dxrow, nw, rmat)


def _ssd_bwd(name, act, proj, dtb, arow, dxrow, nw, rmat, ysc, states, dy, dproj):
    t = act.shape[0]
    q = CHUNK
    nc = t // q
    hg = SSM_HEADS // SSM_GROUPS
    gw = hg * SSM_P

    def body(act_ref, z_ref, sm_ref, dtb_ref, arow_ref, dx_ref, nw_ref, r_ref, ys_ref, st_ref, dy_ref, dproj_ref,
             dact_ref, dz_ref, dsm_ref, dnw_ref, dd_ref, dal_ref, ddtb_ref, ds_scr, dxd_scr):
        @pl.when(pl.program_id(0) == 0)
        def _():
            ds_scr[...] = jnp.zeros_like(ds_scr)
            dnw_ref[...] = jnp.zeros_like(dnw_ref)
            dd_ref[...] = jnp.zeros_like(dd_ref)
            dal_ref[...] = jnp.zeros_like(dal_ref)
            ddtb_ref[...] = jnp.zeros_like(ddtb_ref)

        dsn = ds_scr[...]
        for sub in reversed(range(cps)):
            rows = pl.ds(sub * q, q)
            dsn = chunk(act_ref.at[rows, :], z_ref.at[rows, :], sm_ref.at[rows, :], dtb_ref, arow_ref, dx_ref, nw_ref, r_ref,
                        ys_ref.at[rows, :], st_ref.at[sub], dy_ref.at[rows, :], dact_ref.at[rows, :], dz_ref.at[rows, :],
                        dsm_ref.at[rows, :], dnw_ref, dd_ref, dal_ref, ddtb_ref, dxd_scr.at[rows, :], dsn)
        ds_scr[...] = dsn

    def chunk(act_ref, z_ref, sm_ref, dtb_ref, arow_ref, dx_ref, nw_ref, r_ref, ys_ref, st_ref, dy_ref,
              dact_ref, dz_ref, dsm_ref, dnw_ref, dd_ref, dal_ref, ddtb_ref, dxd_scr, dsn):
        actv = act_ref[...]
        smv = sm_ref[...]
        rmat_v = r_ref[...]
        cm = _ssd_common(actv, smv, dtb_ref[...], arow_ref[...], rmat_v)
        ii, jj = cm["ii"], cm["jj"]
        x, xs = cm["x"], cm["xs"]
        s_all = st_ref[...]
        ysv = ys_ref[...]
        dxr = dx_ref[...]
        y = ysv + dxr * xs
        zv = z_ref[...]
        sz = _sigmoid(zv)
        silz = zv * sz
        yg = y * silz
        dout = dy_ref[...]
        nwv = nw_ref[...]
        dyn = dout * nwv
        yn_parts, dyg_parts = [], []
        for g in range(SSM_GROUPS):
            sl = yg[:, g * gw: (g + 1) * gw]
            rr = lax.rsqrt(jnp.mean(sl * sl, axis=-1, keepdims=True) + EPS)
            yn = sl * rr
            dn = dyn[:, g * gw: (g + 1) * gw]
            yn_parts.append(yn)
            dyg_parts.append(rr * (dn - yn * jnp.mean(dn * yn, axis=-1, keepdims=True)))
        dnw_ref[...] += _colsum(dout * jnp.concatenate(yn_parts, axis=1))
        dyg = jnp.concatenate(dyg_parts, axis=1)
        dyv = dyg * silz
        dz_ref[...] = (dyg * y * (sz * (1.0 + zv * (1.0 - sz)))).astype(BF16)
        dd_ref[...] += _dot01(_colsum(dyv * xs), rmat_v, NT)
        dxs = dyv * dxr
        dcs = dyv * cm["ex"]
        xd = x * cm["dex"]
        dxst_parts, ds_parts, db_parts, dc_parts, yoff_parts, wcol_rows = [], [], [], [], [], []
        lane128 = lax.broadcasted_iota(jnp.int32, (q, 128), 1)
        wrow = jnp.zeros((q, 128), F32)
        for g in range(SSM_GROUPS):
            bg = actv[:, 1024 + g * SSM_N: 1024 + (g + 1) * SSM_N]
            cg = actv[:, 1280 + g * SSM_N: 1280 + (g + 1) * SSM_N]
            sg = s_all[:, g * gw: (g + 1) * gw]
            dsng = dsn[:, g * gw: (g + 1) * gw]
            dcsg = dcs[:, g * gw: (g + 1) * gw]
            dcg = _bdot(dcsg, sg, NT)
            yoff_parts.append(_bdot(cg, sg, NN))
            ds_parts.append(_bdot(cg, dcsg, TN))
            dxst_parts.append(_bdot(bg, dsng, NN))
            dbg = _bdot(xd[:, g * gw: (g + 1) * gw], dsng, NT)
            cb = _bdot(cg, bg, NT)
            dcb = jnp.zeros((q, q), F32)
            for r in range(hg):
                h = g * hg + r
                lm = _ssd_lmat(cm, h)
                mm = cb * lm
                dyh = dyv[:, h * SSM_P: (h + 1) * SSM_P]
                dm = jnp.where(ii >= jj, _bdot(dyh, x[:, h * SSM_P: (h + 1) * SSM_P], NT), 0.0)
                dxd_scr[:, h * SSM_P: (h + 1) * SSM_P] = _bdot(mm, dyh, TN)
                dcb = dcb + dm * lm
                wm = dm * mm
                wrow = wrow + jnp.where(lane128 == h, _rowsum(wm), 0.0)
                wcol_rows.append(_colsum(wm))
            dc_parts.append(dcg + _bdot(dcb, bg, NN))
            db_parts.append(dbg + _bdot(dcb, cg, TN))
        dxst = jnp.concatenate(dxst_parts, axis=1) * cm["dex"]
        dx = dxd_scr[...] + dxst
        ds_prev = jnp.concatenate(ds_parts, axis=1) + dsn * cm["elx"]
        wcol = jnp.concatenate(wcol_rows + [jnp.zeros((128 - SSM_HEADS, q), F32)], axis=0).T
        yoff = jnp.concatenate(yoff_parts, axis=1) * cm["ex"]
        xdxst = x * dxst
        dac = wrow - wcol + _dot01(dyv * yoff - xdxst, rmat_v, NT)
        last = _dot01(_colsum(dsn * s_all) * cm["elx"] + _colsum(xdxst), rmat_v, NT)
        rowq = lax.broadcasted_iota(jnp.int32, (q, 128), 0)
        dac = dac + jnp.where(rowq == q - 1, last, 0.0)
        da = _hdot(cm["triu"], dac)
        arow_v = arow_ref[...]
        ddt = da * arow_v + _dot01(dx * xs, rmat_v, NT)
        dxs = dxs + dx * cm["dtx"]
        dal_ref[...] += _colsum(da * cm["a"])
        ddtraw = jnp.where(cm["m16"], ddt * _sigmoid(smv + dtb_ref[...]), 0.0)
        ddtb_ref[...] += _colsum(ddtraw)
        dsm_ref[...] = ddtraw.astype(BF16)
        dact_ref[:, :1024] = dxs
        for g in range(SSM_GROUPS):
            dact_ref[:, 1024 + g * SSM_N: 1024 + (g + 1) * SSM_N] = db_parts[g]
            dact_ref[:, 1280 + g * SSM_N: 1280 + (g + 1) * SSM_N] = dc_parts[g]
        return ds_prev

    cps = SCAN_CHUNKS_PER_STEP if nc % SCAN_CHUNKS_PER_STEP == 0 else 1
    qq = cps * q
    rev = lambda c: nc // cps - 1 - c
    vec128 = pl.BlockSpec((1, 128), lambda c: (0, 0))
    vec1k = pl.BlockSpec((1, 1024), lambda c: (0, 0))
    return pl.pallas_call(
        body, grid=(nc // cps,),
        in_specs=[pl.BlockSpec((qq, SSM_CONV), lambda c: (rev(c), 0)),
                  pl.BlockSpec((qq, 1024), lambda c: (rev(c), Z_OFF // 1024)),
                  pl.BlockSpec((qq, 128), lambda c: (rev(c), SM_OFF // 128)), vec128, vec128, vec1k, vec1k,
                  pl.BlockSpec((128, 1024), lambda c: (0, 0)),
                  pl.BlockSpec((qq, 1024), lambda c: (rev(c), 0)), pl.BlockSpec((cps, 128, 1024), lambda c: (rev(c), 0, 0)),
                  pl.BlockSpec((qq, 1024), lambda c: (rev(c), 0)), ANY],
        out_specs=(pl.BlockSpec((qq, SSM_CONV), lambda c: (rev(c), 0)),
                   pl.BlockSpec((qq, 1024), lambda c: (rev(c), Z_OFF // 1024)),
                   pl.BlockSpec((qq, 128), lambda c: (rev(c), 0)), vec1k, vec128, vec128, vec128),
        out_shape=(SDS((t, SSM_CONV), F32), SDS(dproj.shape, BF16), SDS((t, 128), BF16), SDS((1, 1024), F32),
                   SDS((1, 128), F32), SDS((1, 128), F32), SDS((1, 128), F32)),
        input_output_aliases={11: 1},
        scratch_shapes=[pltpu.VMEM((128, 1024), F32), pltpu.VMEM((qq, 1024), F32)], name=name,
        compiler_params=_params(("arbitrary",)),
    )(act, proj, proj, dtb, arow, dxrow, nw, rmat, ysc, states, dy, dproj)


def _split(a):
    hi = a.astype(BF16)
    return hi, (a - hi.astype(F32)).astype(BF16)


def _dot3(a, b, dims=NN):
    (ah, al), (bh, bl) = a, b

    def d(x, y):
        return lax.dot_general(x, y, (dims, ((), ())), preferred_element_type=F32)

    return d(ah, bh) + (d(ah, bl) + d(al, bh))


def _tri_inverses(amats, ii, jj):
    eye = jnp.where(ii == jj, 1.0, 0.0)
    tms = [eye - a for a in amats]
    sp = [_split(a) for a in amats]
    for _ in range(5):
        sp = [_split(_dot3(s, s)) for s in sp]
        tms = [t + _dot3(_split(t), s) for t, s in zip(tms, sp)]
    return tms


def _gdn_common(sm, gb, garow):
    q = CHUNK
    ii, jj = _tri(q)
    lane = lax.broadcasted_iota(jnp.int32, (q, 128), 1)
    ma = (lane >= LANE_A) & (lane < LANE_A + GDN_HEADS)
    spre = sm + gb
    g = jnp.where(ma, garow * _softplus(spre), 0.0)
    beta = _sigmoid(sm)
    tril = (ii >= jj).astype(F32)
    triu = (ii <= jj).astype(F32)
    gc = _hdot(tril, g)
    gc_r = _hdot(g.T, triu)
    return dict(ii=ii, jj=jj, lane=lane, ma=ma, spre=spre, g=g, beta=beta, triu=triu, gc=gc, gc_r=gc_r)


def _each(f, *lists):
    return [f(*xs) for xs in zip(*lists)]


GDN_SCALE = GDN_DK ** -0.5


def _gdn_heads(cm, actv, states):
    q = CHUNK
    ii, jj = cm["ii"], cm["jj"]
    heads = range(GDN_HEADS)
    qr = [actv[:, h * 128: (h + 1) * 128] for h in heads]
    kr = [actv[:, 1024 + h * 128: 1024 + (h + 1) * 128] for h in heads]
    v = [actv[:, 2048 + h * 128: 2048 + (h + 1) * 128] for h in heads]
    rq = _each(lambda x: lax.rsqrt(_rowsum(x * x) + EPS), qr)
    rk = _each(lambda x: lax.rsqrt(_rowsum(x * x) + EPS), kr)
    qn = _each(lambda x, r: x * r * GDN_SCALE, qr, rq)
    kn = _each(lambda x, r: x * r, kr, rk)
    gcc = [cm["gc"][:, LANE_A + h: LANE_A + h + 1] for h in heads]
    gcr = [cm["gc_r"][LANE_A + h: LANE_A + h + 1, :] for h in heads]
    bcol = [cm["beta"][:, LANE_B + h: LANE_B + h + 1] for h in heads]
    dm = _each(lambda c, r: jnp.where(ii >= jj, jnp.exp(c - r), 0.0), gcc, gcr)
    kq = _each(lambda k, a: _bdot(jnp.concatenate([k, a], axis=0), k, NT), kn, qn)
    ak = _each(lambda x, d: jnp.where(ii > jj, x[:q] * d, 0.0), kq, dm)
    qkm = _each(lambda x, d: jnp.where(ii >= jj, x[q:] * d, 0.0), kq, dm)
    tm = _tri_inverses(_each(lambda a, b: a * b, ak, bcol), ii, jj)
    eg = _each(jnp.exp, gcc)
    gl = [c[q - 1: q, :] for c in gcc]
    rm = _each(lambda vv, k, b, e: jnp.concatenate([vv * b, k * (b * e)], axis=1), v, kn, bcol, eg)
    tt = _each(lambda t, r: _dot3(_split(t), _split(r)), tm, rm)
    w = [t[:, 128:] for t in tt]
    qg = _each(lambda a, e: a * e, qn, eg)
    ws = _each(lambda ww, a, s: _bdot(jnp.concatenate([ww, a], axis=0), s, NN), w, qg, states)
    vnew = _each(lambda t, x: t[:, :128] - x[:q], tt, ws)
    return dict(qr=qr, v=v, rq=rq, rk=rk, qn=qn, kn=kn, gcc=gcc, bcol=bcol, dm=dm, ak=ak, tm=tm, eg=eg, gl=gl,
                egl=_each(jnp.exp, gl), ed=_each(lambda g, c: jnp.exp(g - c), gl, gcc), tt=tt, w=w, vnew=vnew, qkm=qkm,
                qg=qg, qgs=[x[q:] for x in ws])


def _gdn_fwd(name, act, proj, gb, garow, gnw):
    t = act.shape[0]
    q = CHUNK
    nc = t // q

    def body(act_ref, gz_ref, sm_ref, gb_ref, ga_ref, nw_ref, y_ref, o_ref, st_ref, s_scr):
        @pl.when(pl.program_id(0) == 0)
        def _():
            s_scr[...] = jnp.zeros_like(s_scr)

        states = [s_scr[h * 128: (h + 1) * 128, :] for h in range(GDN_HEADS)]
        for sub in range(cps):
            rows = pl.ds(sub * q, q)
            states = chunk(act_ref.at[rows, :], gz_ref.at[rows, :], sm_ref.at[rows, :], gb_ref, ga_ref, nw_ref,
                           y_ref.at[rows, :], o_ref.at[rows, :], st_ref.at[sub], states)
        for h in range(GDN_HEADS):
            s_scr[h * 128: (h + 1) * 128, :] = states[h]

    def chunk(act_ref, gz_ref, sm_ref, gb_ref, ga_ref, nw_ref, y_ref, o_ref, st_ref, states):
        for h in range(GDN_HEADS):
            st_ref[h * 128: (h + 1) * 128, :] = states[h]
        actv = act_ref[...]
        cm = _gdn_common(sm_ref[...], gb_ref[...], ga_ref[...])
        nwv = nw_ref[...]
        gzv = gz_ref[...]
        hd = _gdn_heads(cm, actv, states)
        outs = _each(lambda qs, m, vn: qs + _bdot(m, vn, NN), hd["qgs"], hd["qkm"], hd["vnew"])
        snew = _each(lambda s, e, k, d, vn: s * e + _bdot(k * d, vn, TN), states, hd["egl"], hd["kn"], hd["ed"], hd["vnew"])
        for h in range(GDN_HEADS):
            o = outs[h]
            o_ref[:, h * 128: (h + 1) * 128] = o
            rr = lax.rsqrt(jnp.mean(o * o, axis=-1, keepdims=True) + EPS)
            gz = gzv[:, h * 128: (h + 1) * 128]
            y_ref[:, h * 128: (h + 1) * 128] = (o * rr * nwv * (gz * _sigmoid(gz))).astype(BF16)
        return snew

    cps = SCAN_CHUNKS_PER_STEP if nc % SCAN_CHUNKS_PER_STEP == 0 else 1
    qq = cps * q
    vec128 = pl.BlockSpec((1, 128), lambda c: (0, 0))
    return pl.pallas_call(
        body, grid=(nc // cps,),
        in_specs=[pl.BlockSpec((qq, GDN_QKV), lambda c: (c, 0)), pl.BlockSpec((qq, 1024), lambda c: (c, GZ_OFF // 1024)),
                  pl.BlockSpec((qq, 128), lambda c: (c, SM_OFF // 128)), vec128, vec128, vec128],
        out_specs=(pl.BlockSpec((qq, 1024), lambda c: (c, 0)), pl.BlockSpec((qq, 1024), lambda c: (c, 0)),
                   pl.BlockSpec((cps, 1024, 128), lambda c: (c, 0, 0))),
        out_shape=(SDS((t, 1024), BF16), SDS((t, 1024), F32), SDS((nc, 1024, 128), F32)),
        scratch_shapes=[pltpu.VMEM((1024, 128), F32)], name=name, compiler_params=_params(("arbitrary",)),
    )(act, proj, proj, gb, garow, gnw)


def _gdn_bwd(name, act, proj, gb, garow, gnw, oraw, states, dy, dproj):
    t = act.shape[0]
    q = CHUNK
    nc = t // q

    def body(act_ref, gz_ref, sm_ref, gb_ref, ga_ref, nw_ref, o_ref, st_ref, dy_ref, dproj_ref,
             dact_ref, dgz_ref, dsm_ref, dnw_ref, dal_ref, dgb_ref, ds_scr):
        @pl.when(pl.program_id(0) == 0)
        def _():
            ds_scr[...] = jnp.zeros_like(ds_scr)
            dnw_ref[...] = jnp.zeros_like(dnw_ref)
            dal_ref[...] = jnp.zeros_like(dal_ref)
            dgb_ref[...] = jnp.zeros_like(dgb_ref)

        dsn = [ds_scr[h * 128: (h + 1) * 128, :] for h in range(GDN_HEADS)]
        for sub in reversed(range(cps)):
            rows = pl.ds(sub * q, q)
            dsn = chunk(act_ref.at[rows, :], gz_ref.at[rows, :], sm_ref.at[rows, :], gb_ref, ga_ref, nw_ref,
                        o_ref.at[rows, :], st_ref.at[sub], dy_ref.at[rows, :],
                        dact_ref.at[rows, :], dgz_ref.at[rows, :], dsm_ref.at[rows, :], dnw_ref, dal_ref, dgb_ref, dsn)
        for h in range(GDN_HEADS):
            ds_scr[h * 128: (h + 1) * 128, :] = dsn[h]

    def chunk(act_ref, gz_ref, sm_ref, gb_ref, ga_ref, nw_ref, o_ref, st_ref, dy_ref,
              dact_ref, dgz_ref, dsm_ref, dnw_ref, dal_ref, dgb_ref, dsn):
        actv = act_ref[...]
        smv = sm_ref[...]
        garow_v = ga_ref[...]
        cm = _gdn_common(smv, gb_ref[...], garow_v)
        ii, jj, lane = cm["ii"], cm["jj"], cm["lane"]
        nwv = nw_ref[...]
        rowq = lax.broadcasted_iota(jnp.int32, (q, 1), 0)
        dgc_all = jnp.zeros((q, 128), F32)
        dbeta_all = jnp.zeros((q, 128), F32)
        dnw_acc = jnp.zeros((1, 128), F32)
        heads = range(GDN_HEADS)
        sts = [st_ref[h * 128: (h + 1) * 128, :] for h in heads]
        ds_out = []
        ov, gzv, dyv = o_ref[...], gz_ref[...], dy_ref[...]
        hd = _gdn_heads(cm, actv, sts)
        qn, kn, v, eg, ed, egl, bcol = hd["qn"], hd["kn"], hd["v"], hd["eg"], hd["ed"], hd["egl"], hd["bcol"]
        vnew, qkm, qg, w, tt, dm, ak = hd["vnew"], hd["qkm"], hd["qg"], hd["w"], hd["tt"], hd["dm"], hd["ak"]
        do = []
        for h in heads:
            hs = slice(h * 128, (h + 1) * 128)
            o = ov[:, hs]
            rr = lax.rsqrt(jnp.mean(o * o, axis=-1, keepdims=True) + EPS)
            on = o * rr
            gz = gzv[:, hs]
            sz = _sigmoid(gz)
            silz = gz * sz
            dyh = dyv[:, hs]
            dnw_acc = dnw_acc + _colsum(dyh * on * silz)
            dgz_ref[:, hs] = (dyh * on * nwv * (sz * (1.0 + gz * (1.0 - sz)))).astype(BF16)
            don = dyh * nwv * silz
            do.append(rr * (don - on * jnp.mean(don * on, axis=-1, keepdims=True)))
        kd = _each(lambda k, e: k * e, kn, ed)
        dkd = _each(lambda vn, d: _bdot(vn, d, NT), vnew, dsn)
        dvnew_a = _each(lambda k, d: _bdot(k, d, NN), kd, dsn)
        ded = _each(lambda a, b: _rowsum(a * b), dkd, kd)
        dgl = _each(lambda d, s, e, de: jnp.sum(_rowsum(d * s), axis=0, keepdims=True) * e + _colsum(de), dsn, sts, egl, ded)
        dqk = _each(lambda d, vn: jnp.where(ii >= jj, _bdot(d, vn, NT), 0.0), do, vnew)
        dvnew = _each(lambda a, m, d: a + _bdot(m, d, TN), dvnew_a, qkm, do)
        pq = _each(lambda a, b: a * b, dqk, dm)
        w1 = _each(lambda a, b: a * b, dqk, qkm)
        dod = _each(lambda a, b: jnp.concatenate([a, b], axis=0), do, dvnew)
        dos = _each(lambda x, s: _bdot(x, s, NT), dod, sts)
        dqg = [x[:q] for x in dos]
        dw = [-x[q:] for x in dos]
        ds12 = _each(lambda a, ww, x: _bdot(jnp.concatenate([a, -ww], axis=0), x, TN), qg, w, dod)
        dr = _each(lambda t, a, b: _dot3(_split(t), _split(jnp.concatenate([a, b], axis=1)), TN), hd["tm"], dvnew, dw)
        da = _each(lambda r, t: jnp.where(ii > jj, -_dot3(_split(r), _split(t), NT), 0.0), dr, tt)
        sk = _each(lambda r, k: _rowsum(r[:, 128:] * k), dr, kn)
        pk = _each(lambda a, d, b: a * d * b, da, dm, bcol)
        pkn = _each(lambda p, pp, k: _bdot(jnp.concatenate([p, pp + pp.T], axis=0), k, NN), pq, pk, kn)
        dq = _each(lambda a, e, x: a * e + x[:q], dqg, eg, pkn)
        dk = _each(lambda a, e, p, x, r, b, eg_, y: a * e + _bdot(p, x, TN) + r[:, 128:] * (b * eg_) + y[q:],
                   dkd, ed, pq, qn, dr, bcol, eg, pkn)
        w2 = _each(lambda a, k, b: a * (k * b), da, ak, bcol)
        for h in heads:
            hs = slice(h * 128, (h + 1) * 128)
            dgc = (-ded[h] + _rowsum(dqg[h] * qg[h]) + _rowsum(w1[h]) - _rowsum(w1[h].T) + sk[h] * bcol[h] * eg[h]
                   + _rowsum(w2[h]) - _rowsum(w2[h].T) + jnp.where(rowq == q - 1, dgl[h], 0.0))
            dbeta = _rowsum(dr[h][:, :128] * v[h]) + sk[h] * eg[h] + _rowsum(da[h] * ak[h])
            qhat = hd["qr"][h] * hd["rq"][h]
            dqhat = dq[h] * GDN_SCALE
            dact_ref[:, hs] = hd["rq"][h] * (dqhat - qhat * _rowsum(dqhat * qhat))
            dact_ref[:, 1024 + h * 128: 1024 + (h + 1) * 128] = hd["rk"][h] * (dk[h] - kn[h] * _rowsum(dk[h] * kn[h]))
            dact_ref[:, 2048 + h * 128: 2048 + (h + 1) * 128] = dr[h][:, :128] * bcol[h]
            dgc_all = dgc_all + jnp.where(lane == LANE_A + h, dgc, 0.0)
            dbeta_all = dbeta_all + jnp.where(lane == LANE_B + h, dbeta, 0.0)
            ds_out.append(dsn[h] * egl[h] + ds12[h])
        dnw_ref[...] += dnw_acc
        dg = _hdot(cm["triu"], dgc_all)
        da_raw = jnp.where(cm["ma"], dg * garow_v * _sigmoid(cm["spre"]), 0.0)
        dal_ref[...] += _colsum(dg * cm["g"])
        dgb_ref[...] += _colsum(da_raw)
        beta = cm["beta"]
        dsm_ref[...] = (da_raw + dbeta_all * beta * (1.0 - beta)).astype(BF16)
        return ds_out

    cps = SCAN_CHUNKS_PER_STEP if nc % SCAN_CHUNKS_PER_STEP == 0 else 1
    qq = cps * q
    rev = lambda c: nc // cps - 1 - c
    vec128 = pl.BlockSpec((1, 128), lambda c: (0, 0))
    return pl.pallas_call(
        body, grid=(nc // cps,),
        in_specs=[pl.BlockSpec((qq, GDN_QKV), lambda c: (rev(c), 0)),
                  pl.BlockSpec((qq, 1024), lambda c: (rev(c), GZ_OFF // 1024)),
                  pl.BlockSpec((qq, 128), lambda c: (rev(c), SM_OFF // 128)), vec128, vec128, vec128,
                  pl.BlockSpec((qq, 1024), lambda c: (rev(c), 0)), pl.BlockSpec((cps, 1024, 128), lambda c: (rev(c), 0, 0)),
                  pl.BlockSpec((qq, 1024), lambda c: (rev(c), 0)), ANY],
        out_specs=(pl.BlockSpec((qq, GDN_QKV), lambda c: (rev(c), 0)),
                   pl.BlockSpec((qq, 1024), lambda c: (rev(c), GZ_OFF // 1024)),
                   pl.BlockSpec((qq, 128), lambda c: (rev(c), 0)), vec128, vec128, vec128),
        out_shape=(SDS((t, GDN_QKV), F32), SDS(dproj.shape, BF16), SDS((t, 128), BF16), SDS((1, 128), F32),
                   SDS((1, 128), F32), SDS((1, 128), F32)),
        input_output_aliases={9: 1},
        scratch_shapes=[pltpu.VMEM((1024, 128), F32)], name=name, compiler_params=_params(("arbitrary",)),
    )(act, proj, proj, gb, garow, gnw, oraw, states, dy, dproj)


def _row_tile(r):
    for cand in (512, 256, 128, 64, 32, 16, 8):
        if r % cand == 0:
            return cand
    return r


def _sum_terms(name, terms, out_dtype):
    shape = terms[0][0].shape[1:]
    c = shape[-1]
    r = 1
    for s in shape[:-1]:
        r *= s
    tr = min(_row_tile(r), 256)
    n = len(terms)

    def body(*refs):
        acc = refs[0][...].astype(F32)
        for k in range(1, n):
            acc = acc + refs[k][...].astype(F32)
        refs[n][...] = acc.astype(out_dtype)

    in_specs = [pl.BlockSpec((None, tr, c), lambda i, q=lead: (q, i, 0)) for _, lead in terms]
    args = [a.reshape(a.shape[0], r, c) for a, _ in terms]
    out = pl.pallas_call(body, grid=(r // tr,), in_specs=in_specs, out_specs=pl.BlockSpec((tr, c), lambda i: (i, 0)),
                         out_shape=SDS((r, c), out_dtype), name=name, compiler_params=_params(("parallel",)))(*args)
    return out.reshape(shape)


def _adamw(name, w, g, m, v):
    shape = w.shape
    c = shape[-1]
    per_layer = isinstance(g, (list, tuple))
    nl = len(g) if per_layer else 1
    gs = [a.reshape(-1, c) for a in g] if per_layer else [g.reshape(-1, c)]
    r = gs[0].shape[0]
    w3, m3, v3 = (a.reshape(nl, r, c) for a in (w, m, v))
    tr = min(_row_tile(r), 256)

    def body(*refs):
        w_ref, m_ref, v_ref = refs[:3]
        g_refs = refs[3: 3 + nl]
        go_ref, d_ref, nm_ref, nv_ref = refs[3 + nl:]
        layer = pl.program_id(0)
        gv = g_refs[0][...]
        for k in range(1, nl):
            gv = jnp.where(layer == k, g_refs[k][...], gv)
        mn = ADAM_B1 * m_ref[...] + (1.0 - ADAM_B1) * gv
        vn = ADAM_B2 * v_ref[...] + (1.0 - ADAM_B2) * (gv * gv)
        m_hat = mn / (1.0 - ADAM_B1 ** ADAM_STEP)
        v_hat = vn / (1.0 - ADAM_B2 ** ADAM_STEP)
        go_ref[...] = gv
        d_ref[...] = -ADAM_LR * (m_hat / (jnp.sqrt(v_hat) + ADAM_EPS) + ADAM_WD * w_ref[...])
        nm_ref[...] = mn
        nv_ref[...] = vn

    spec3 = pl.BlockSpec((None, tr, c), lambda l, i: (l, i, 0))
    gspec = pl.BlockSpec((tr, c), lambda l, i: (i, 0))
    outs = pl.pallas_call(body, grid=(nl, r // tr), in_specs=[spec3] * 3 + [gspec] * nl, out_specs=(spec3,) * 4,
                          out_shape=(SDS((nl, r, c), F32),) * 4, name=name,
                          compiler_params=_params(("parallel", "parallel")))(w3, m3, v3, *gs)
    return tuple(o.reshape(shape) for o in outs)


ANY = pl.BlockSpec(memory_space=pl.ANY)
MESH = pl.DeviceIdType.MESH


def _allgather(name, xs):
    n = len(xs)

    def body(*refs):
        x_refs, out_refs = refs[:n], refs[n: 2 * n]
        send_sems, recv_sems, local_sems = refs[2 * n:]
        x, y, cc = lax.axis_index("x"), lax.axis_index("y"), lax.axis_index("c")
        me, sibling = (x, y, cc), (x, y, 1 - cc)
        chips = [(1 - x, y), (x, 1 - y), (1 - x, 1 - y)]

        def rows(a, px, py, pc):
            return out_refs[a].at[4 * px + 2 * py + pc]

        def copy(a, k, block, to, src=None):
            return pltpu.make_async_remote_copy(
                src_ref=rows(a, *block) if src is None else src, dst_ref=rows(a, *block),
                send_sem=send_sems.at[7 * a + k], recv_sem=recv_sems.at[7 * a + k], device_id=to, device_id_type=MESH)

        mine = [pltpu.make_async_copy(x_refs[a], rows(a, *me), local_sems.at[a]) for a in range(n)]
        for cp in mine:
            cp.start()
        first = []
        for a in range(n):
            first.append(copy(a, 0, me, sibling, src=x_refs[a]))
            first += [copy(a, 1 + j, me, (*chip, cc), src=x_refs[a]) for j, chip in enumerate(chips)]
        for cp in first:
            cp.start()
        passed = []
        for j, chip in enumerate(chips):
            for a in range(n):
                copy(a, 1 + j, (*chip, cc), me).wait_recv()
                fwd = copy(a, 4 + j, (*chip, cc), sibling)
                fwd.start()
                passed.append(fwd)
        for a in range(n):
            copy(a, 0, sibling, me).wait_recv()
        for j, chip in enumerate(chips):
            for a in range(n):
                copy(a, 4 + j, (*chip, 1 - cc), me).wait_recv()
        for cp in first + passed:
            cp.wait_send()
        for cp in mine:
            cp.wait()

    return pl.pallas_call(
        body, out_shape=tuple(SDS((N_DEV,) + a.shape, a.dtype) for a in xs), in_specs=[ANY] * n, out_specs=(ANY,) * n,
        scratch_shapes=[pltpu.SemaphoreType.DMA((7 * n,)), pltpu.SemaphoreType.DMA((7 * n,)),
                        pltpu.SemaphoreType.DMA((n,))],
        name=name,
    )(*xs)


def _allgather_seq(name, xs, collective_id):
    n = len(xs)
    x_refs = [jax.new_ref(a, memory_space=pltpu.MemorySpace.HBM) for a in xs]
    out_refs = [jax.empty_ref(SDS((N_DEV,) + a.shape, a.dtype), memory_space=pltpu.MemorySpace.HBM) for a in xs]

    @pl.kernel(mesh=plsc.ScalarSubcoreMesh(axis_name="seq", num_cores=1), name=name,
               scratch_types=(pltpu.SemaphoreType.DMA((7 * n,)), pltpu.SemaphoreType.DMA((7 * n,)),
                              pltpu.SemaphoreType.DMA((n,))),
               compiler_params=pltpu.CompilerParams(collective_id=collective_id))
    def launch(send_sems, recv_sems, local_sems):
        x, y, cc = lax.axis_index("x"), lax.axis_index("y"), lax.axis_index("c")
        me, sibling = (x, y, cc), (x, y, 1 - cc)
        chips = [(1 - x, y), (x, 1 - y), (1 - x, 1 - y)]
        barrier = pltpu.get_barrier_semaphore()
        for peer in [sibling] + [(*chip, cc) for chip in chips]:
            pl.semaphore_signal(barrier, inc=1, device_id=peer, device_id_type=MESH)
        pl.semaphore_wait(barrier, 4)

        def rows(a, px, py, pc):
            return out_refs[a].at[4 * px + 2 * py + pc]

        def copy(a, k, block, to, src=None):
            return pltpu.make_async_remote_copy(
                src_ref=rows(a, *block) if src is None else src, dst_ref=rows(a, *block),
                send_sem=send_sems.at[7 * a + k], recv_sem=recv_sems.at[7 * a + k], device_id=to, device_id_type=MESH)

        mine = [pltpu.make_async_copy(x_refs[a], rows(a, *me), local_sems.at[a]) for a in range(n)]
        for cp in mine:
            cp.start()
        first = []
        for a in range(n):
            first.append(copy(a, 0, me, sibling, src=x_refs[a]))
            first += [copy(a, 1 + j, me, (*chip, cc), src=x_refs[a]) for j, chip in enumerate(chips)]
        for cp in first:
            cp.start()
        passed = []
        for j, chip in enumerate(chips):
            for a in range(n):
                copy(a, 1 + j, (*chip, cc), me).wait_recv()
                fwd = copy(a, 4 + j, (*chip, cc), sibling)
                fwd.start()
                passed.append(fwd)
        for a in range(n):
            copy(a, 0, sibling, me).wait_recv()
        for j, chip in enumerate(chips):
            for a in range(n):
                copy(a, 4 + j, (*chip, 1 - cc), me).wait_recv()
        for cp in first + passed:
            cp.wait_send()
        for cp in mine:
            cp.wait()

    launch()
    return [r[...] for r in out_refs]


HBM = pl.BlockSpec(memory_space=pltpu.HBM)
SEM = pl.BlockSpec(memory_space=pltpu.SEMAPHORE)
EFFECT = pltpu.SideEffectType.DATAFLOW_SIDE_EFFECTING


def _sibling_plan(srcs, lands, send_sems, recv_sems):
    x, y, cc = lax.axis_index("x"), lax.axis_index("y"), lax.axis_index("c")
    return [pltpu.make_async_remote_copy(
        src_ref=srcs[a].at[2 * q + 1 - cc], dst_ref=lands[a].at[q], send_sem=send_sems.at[4 * a + q],
        recv_sem=recv_sems.at[4 * a + q], device_id=(x, y, 1 - cc), device_id_type=MESH)
        for a in range(len(srcs)) for q in range(4)]


def _chips_plan(srcs, lands, send_sems, recv_sems):
    x, y, cc = lax.axis_index("x"), lax.axis_index("y"), lax.axis_index("c")
    chips = [(1 - x, y), (x, 1 - y), (1 - x, 1 - y)]
    return [pltpu.make_async_remote_copy(
        src_ref=srcs[a].at[2 * px + py], dst_ref=lands[a].at[j], send_sem=send_sems.at[3 * a + j],
        recv_sem=recv_sems.at[3 * a + j], device_id=(px, py, cc), device_id_type=MESH)
        for a in range(len(srcs)) for j, (px, py) in enumerate(chips)]


def _copies_start(name, plan, per_array, srcs, land_lead):
    n = len(srcs)
    k = per_array * n

    def body(*refs):
        src_refs, land_refs = refs[:n], refs[n: 2 * n]
        send_sems, recv_sems = refs[2 * n], refs[2 * n + 1]
        token = refs[-1]
        for cp in plan(src_refs, land_refs, send_sems, recv_sems):
            cp.start()
        token[...] = jnp.zeros_like(token)

    lands = [lax.empty((land_lead,) + a.shape[1:], a.dtype) for a in srcs]
    outs = pl.pallas_call(
        body, name=name,
        out_shape=(pltpu.SemaphoreType.DMA((k,)), pltpu.SemaphoreType.DMA((k,)),
                   *[pltpu.HBM(a.shape, a.dtype) for a in srcs], *[pltpu.HBM(a.shape, a.dtype) for a in lands],
                   SDS((8, 128), F32)),
        in_specs=[HBM] * (2 * n), out_specs=(SEM, SEM, *[HBM] * (2 * n), pl.BlockSpec(memory_space=pltpu.VMEM)),
        input_output_aliases={i: 2 + i for i in range(2 * n)},
        compiler_params=pltpu.CompilerParams(has_side_effects=EFFECT),
    )(*[pltpu.with_memory_space_constraint(a, pltpu.HBM) for a in srcs],
      *[pltpu.with_memory_space_constraint(a, pltpu.HBM) for a in lands])
    return outs[0], outs[1], list(outs[2: 2 + n]), list(outs[2 + n: 2 + 2 * n]), outs[-1]


def _copies_wait(name, plan, started, after):
    send_sems, recv_sems, srcs, lands, _ = started
    n = len(srcs)
    after = tuple(after)

    def body(*refs):
        src_refs, land_refs = refs[:n], refs[n: 2 * n]
        for cp in plan(src_refs, land_refs, refs[2 * n], refs[2 * n + 1]):
            cp.wait_send()
            cp.wait_recv()

    outs = pl.pallas_call(
        body, name=name,
        out_shape=tuple(pltpu.HBM(a.shape, a.dtype) for a in srcs + lands),
        in_specs=[HBM] * (2 * n) + [SEM, SEM] + [ANY] * len(after), out_specs=(HBM,) * (2 * n),
        input_output_aliases={i: i for i in range(2 * n)},
        compiler_params=pltpu.CompilerParams(has_side_effects=EFFECT),
    )(*srcs, *lands, send_sems, recv_sems, *after)
    return list(outs[n:])


BIG = (("w_in", 1), ("w_ffn_in", 1), ("w_proj_ssm", 0), ("w_proj_gdn", 0), ("w_out", 0), ("w_ffn_down", 0))
CONVS = (("ssm_conv_w", 1), ("gdn_conv_w", 1))


def _to_dest_major(full, axis):
    a, b = full.shape
    if axis == 0:
        return full.reshape(N_DEV, a // N_DEV, b)
    s = b // N_DEV
    return jnp.stack([full[:, d * s: (d + 1) * s] for d in range(N_DEV)])


def _from_gathered(g, axis):
    if axis == 0:
        return g.reshape(-1, g.shape[2])
    return jnp.concatenate([g[d] for d in range(N_DEV)], axis=1)


IN_RUNS = ((Z_OFF, O_Z, 1024), (GZ_OFF, O_GZ, 1024), (G1_OFF, O_G1, 1024), (G2_OFF, O_G2, 1024), (QKV_OFF, O_QKV, 3072),
           (XBC_OFF, O_XBC, 1536), (SM_OFF, O_DT, 16), (SM_OFF + LANE_A, O_A, 8), (SM_OFF + LANE_B, O_B, 8))
IN_SHARD = IN_DIM // N_DEV


def _w_in_from_blocks(g):
    rows = g.shape[1]
    parts, pos = [], 0
    for off, o0, width in IN_RUNS:
        if off > pos:
            parts.append(jnp.zeros((rows, off - pos), g.dtype))
        c = o0
        while c < o0 + width:
            d = c // IN_SHARD
            hi = min(o0 + width, (d + 1) * IN_SHARD)
            parts.append(g[d][:, c - d * IN_SHARD: hi - d * IN_SHARD])
            c = hi
        pos = off + width
    parts.append(jnp.zeros((rows, PROJ_W - pos), g.dtype))
    return jnp.concatenate(parts, axis=1)


def _w_in_to_blocks(wp):
    by_orig = sorted(IN_RUNS, key=lambda r: r[1])
    blocks = []
    for d in range(N_DEV):
        lo, hi = d * IN_SHARD, (d + 1) * IN_SHARD
        parts = []
        for off, o0, width in by_orig:
            a, b = max(lo, o0), min(hi, o0 + width)
            if a < b:
                parts.append(wp[:, off + a - o0: off + b - o0])
        blocks.append(jnp.concatenate(parts, axis=1))
    return jnp.stack(blocks)


def _pad128(v, lane0):
    return jnp.zeros((1, 128), F32).at[0, lane0: lane0 + v.shape[0]].set(v)


def _layer_consts(p):
    return dict(
        dtb=_pad128(p["ssm_dt_bias"], 0), arow=_pad128(-jnp.exp(p["ssm_a_log"]), 0),
        dxrow=jnp.repeat(p["ssm_d"], SSM_P).reshape(1, 1024), snw=p["ssm_norm_w"].reshape(1, 1024),
        gb=_pad128(p["gdn_dt_bias"], LANE_A), garow=_pad128(-jnp.exp(p["gdn_a_log"]), LANE_A),
        gnw=p["gdn_norm_w"].reshape(1, 128), zb=jnp.zeros((1, GDN_QKV), F32), scb=p["ssm_conv_b"].reshape(1, SSM_CONV))


def _expand_matrix():
    row = lax.broadcasted_iota(jnp.int32, (128, 1024), 0)
    col = lax.broadcasted_iota(jnp.int32, (128, 1024), 1)
    return (col // SSM_P == row).astype(BF16)


def _silu_mul_epi(acc, up):
    g = acc
    return g, g * _sigmoid(g) * up.astype(F32)


def _merge_epi(acc, p1, g1, g2):
    return acc, _sigmoid(g1) * p1.astype(F32) + _sigmoid(g2) * acc


def _add_epi(acc, res):
    return (acc + res,)


def _ffn_bwd_epi(acc, gate, up):
    g = gate.astype(F32)
    sg = _sigmoid(g)
    return acc * up.astype(F32) * (sg * (1.0 + g * (1.0 - sg))), acc * (g * sg)


def _merge_bwd_epi(acc, g1, g2, p1, p2):
    s1, s2 = _sigmoid(g1), _sigmoid(g2)
    dg1, dg2 = acc * p1.astype(F32) * (s1 * (1.0 - s1)), acc * p2.astype(F32) * (s2 * (1.0 - s2))
    return acc * s1, acc * s2, jnp.concatenate([dg1, dg2], axis=1)


def _layer_fwd(l, x, p, rmat):
    t = x.shape[0]
    n = f"l{l}_"
    k = _layer_consts(p)
    h = _rmsnorm_fwd(n + "norm_mix", x, p["norm_mix_w"])
    proj = _matmul(n + "in_proj", "nn", [(h, 0, p["w_in"], 0)], t, PROJ_W, 1024, 1024, 1280, 1024, (F32,))
    act_g = _conv_fwd(n + "conv_gdn", proj, QKV_OFF, p["gdn_conv_w"], k["zb"])
    act_s = _conv_fwd(n + "conv_ssm", proj, XBC_OFF, p["ssm_conv_w"], k["scb"])
    y_ssm, ysc, st_s = _ssd_fwd(n + "ssd_fwd", act_s, proj, k["dtb"], k["arow"], k["dxrow"], k["snw"], rmat)
    y_gdn, oraw, st_g = _gdn_fwd(n + "gdn_fwd", act_g, proj, k["gb"], k["garow"], k["gnw"])
    if "late" in p:
        y_gdn, late = p["late"](y_gdn)
        p = {**p, **late}
    p1 = _matmul(n + "proj_ssm", "nn", [(y_ssm, 0, p["w_proj_ssm"], 0)], t, 1024, 1024, 1024, 1024, 1024, (BF16,))
    p2, merged = _matmul(n + "proj_gdn_merge", "nn", [(y_gdn, 0, p["w_proj_gdn"], 0)], t, 1024, 1024, 512, 1024, 1024,
                         (BF16, BF16), epi=_merge_epi, extras=[(p1, 0), (proj, G1_OFF // 1024), (proj, G2_OFF // 1024)])
    x1 = _matmul(n + "out_proj", "nn", [(merged, 0, p["w_out"], 0)], t, 1024, 1024, 1024, 1024, 1024, (F32,),
                 epi=_add_epi, extras=[(x, 0)])
    h2 = _rmsnorm_fwd(n + "norm_ffn", x1, p["norm_ffn_w"])
    up = _matmul(n + "ffn_up", "nn", [(h2, 0, p["w_ffn_in"], 2)], t, FFN, 1024, 1024, FFN // 2, 1024, (BF16,))
    gate, act = _matmul(n + "ffn_gate", "nn", [(h2, 0, p["w_ffn_in"], 0)], t, FFN, 1024, 1024, FFN // 2, 1024, (BF16, BF16),
                        epi=_silu_mul_epi, extras=[(up, 0)])
    x2 = _matmul(n + "ffn_down", "nn", [(act, 0, p["w_ffn_down"], 0)], t, 1024, FFN, 1024, 1024, FFN, (F32,),
                 epi=_add_epi, extras=[(x1, 0)])
    saved = dict(x=x, h=h, proj=proj, act_g=act_g, act_s=act_s, y_ssm=y_ssm, ysc=ysc, st_s=st_s, y_gdn=y_gdn, oraw=oraw,
                 st_g=st_g, p1=p1, p2=p2, merged=merged, x1=x1, h2=h2, up=up, gate=gate, act=act, k=k, p=p)
    return x2, saved


def _layer_bwd(l, dx2, dx2b, s, p, rmat, hooks):
    t = dx2.shape[0]
    n = f"l{l}_"
    k = s["k"]
    tk_tok = 1024
    hf = FFN // 2
    g = {}
    dgate, dup = _matmul(n + "d_ffn_act", "nt", [(dx2b, 0, p["w_ffn_down"], 0)], t, FFN, 1024, 1024, hf, 1024, (BF16, BF16),
                         epi=_ffn_bwd_epi, extras=[(s["gate"], 0), (s["up"], 0)])
    g["w_ffn_down"] = _matmul(n + "dw_ffn_down", "tn", [(s["act"], 0, dx2b, 0)], FFN, 1024, t, hf, 1024, tk_tok, (BF16,))
    dh2 = _matmul(n + "d_ffn_in", "nt", [(dgate, 0, p["w_ffn_in"], 0), (dup, 0, p["w_ffn_in"], 2)], t, 1024, FFN,
                  1024, 1024, hf, (F32,))
    dwg = _matmul(n + "dw_ffn_gate", "tn", [(s["h2"], 0, dgate, 0)], 1024, FFN, t, 1024, hf, tk_tok, (BF16,))
    dwu = _matmul(n + "dw_ffn_up", "tn", [(s["h2"], 0, dup, 0)], 1024, FFN, t, 1024, hf, tk_tok, (BF16,))
    g["w_ffn_in"] = jnp.concatenate([dwg, dwu], axis=1)
    dx1, dx1b, g["norm_ffn_w"] = _rmsnorm_bwd(n + "d_norm_ffn", s["x1"], p["norm_ffn_w"], dh2, dx2)
    dx1b = hooks.ffn_done(dx1b)
    dp1, dp2, dproj = _matmul(
        n + "d_out_proj", "nt", [(dx1b, 0, p["w_out"], 0)], t, 1024, 1024, 512, 1024, 1024,
        (BF16, BF16, (BF16, PROJ_W, 2048, G1_OFF // 2048)), epi=_merge_bwd_epi,
        extras=[(s["proj"], G1_OFF // 1024), (s["proj"], G2_OFF // 1024), (s["p1"], 0), (s["p2"], 0)])
    g["w_out"] = _matmul(n + "dw_out", "tn", [(s["merged"], 0, dx1b, 0)], 1024, 1024, t, 1024, 1024, tk_tok, (BF16,))
    g["w_proj_ssm"] = _matmul(n + "dw_proj_ssm", "tn", [(s["y_ssm"], 0, dp1, 0)], 1024, 1024, t, 1024, 1024, tk_tok, (BF16,))
    g["w_proj_gdn"] = _matmul(n + "dw_proj_gdn", "tn", [(s["y_gdn"], 0, dp2, 0)], 1024, 1024, t, 1024, 1024, tk_tok, (BF16,))
    dp1, dp2 = hooks.early_ready(l, g, dp1, dp2)
    dy_ssm = _matmul(n + "d_proj_ssm", "nt", [(dp1, 0, p["w_proj_ssm"], 0)], t, 1024, 1024, 1024, 1024, 1024, (F32,))
    dy_gdn = _matmul(n + "d_proj_gdn", "nt", [(dp2, 0, p["w_proj_gdn"], 0)], t, 1024, 1024, 1024, 1024, 1024, (F32,))
    dact_s, dproj, dsm_s, dsnw, dd, dal, ddtb = _ssd_bwd(n + "ssd_bwd", s["act_s"], s["proj"], k["dtb"], k["arow"],
                                                           k["dxrow"], k["snw"], rmat, s["ysc"], s["st_s"], dy_ssm, dproj)
    dact_g, dproj, dsm_g, dgnw, dgal, dgb = _gdn_bwd(n + "gdn_bwd", s["act_g"], s["proj"], k["gb"], k["garow"], k["gnw"],
                                                       s["oraw"], s["st_g"], dy_gdn, dproj)
    dsm_s = hooks.mixers_done(dsm_s)
    dproj = _place_small(n + "d_small", dsm_s, dsm_g, dproj)
    dproj, g["ssm_conv_w"], dcb = _conv_bwd(n + "d_conv_ssm", s["proj"], XBC_OFF, p["ssm_conv_w"], k["scb"], dact_s, dproj)
    dproj, g["gdn_conv_w"], _ = _conv_bwd(n + "d_conv_gdn", s["proj"], QKV_OFF, p["gdn_conv_w"], k["zb"], dact_g, dproj)
    g["ssm_conv_b"] = dcb.reshape(-1)
    g["ssm_norm_w"] = dsnw.reshape(-1)
    g["ssm_d"] = dd[0, :SSM_HEADS]
    g["ssm_a_log"] = dal[0, :SSM_HEADS]
    g["ssm_dt_bias"] = ddtb[0, :SSM_HEADS]
    g["gdn_norm_w"] = dgnw.reshape(-1)
    g["gdn_a_log"] = dgal[0, LANE_A: LANE_A + GDN_HEADS]
    g["gdn_dt_bias"] = dgb[0, LANE_A: LANE_A + GDN_HEADS]
    dh = _matmul(n + "d_in_proj", "nt", [(dproj, 0, p["w_in"], 0)], t, 1024, PROJ_W, 1024, 1024, 1280, (F32,))
    g["w_in"] = _matmul(n + "dw_in", "tn", [(s["h"], 0, dproj, 0)], 1024, PROJ_W, t, 1024, 1280, tk_tok, (BF16,))
    dx, dxb, g["norm_mix_w"] = _rmsnorm_bwd(n + "d_norm_mix", s["x"], p["norm_mix_w"], dh, dx1)
    g["norm_mix_w"] = g["norm_mix_w"].reshape(-1)
    g["norm_ffn_w"] = g["norm_ffn_w"].reshape(-1)
    return dx, dxb, g


def _local_step(x, tgt, layers, final_norm_w, reduce=False):
    rmat = _expand_matrix()
    saved, params = [], []
    for l in range(DEPTH):
        x, p = layers[l](x)
        x, s = _layer_fwd(l, x, p, rmat)
        saved.append(s)
        params.append(s["p"])
    loss, dx, dxb, dfw = _loss_head("loss_head", x, final_norm_w, tgt)
    grads = [None] * DEPTH
    hooks = _ReduceBesideBackward() if reduce else _NoReduce()
    for l in reversed(range(DEPTH)):
        dx, dxb, grads[l] = _layer_bwd(l, dx, dxb, saved[l], params[l], rmat, hooks)
        if reduce:
            dxb = hooks.layer_done(l, grads[l], dxb)
    if reduce:
        hooks.finish_start(dxb)
    return loss[0, 0], dx, grads, dfw.reshape(-1), hooks if reduce else None


SMALL = ("norm_mix_w", "ssm_conv_b", "ssm_dt_bias", "ssm_a_log", "ssm_d", "ssm_norm_w", "gdn_a_log", "gdn_dt_bias",
         "gdn_norm_w", "norm_ffn_w")
WEIGHTS = ("norm_mix_w", "w_in", "ssm_conv_w", "ssm_conv_b", "ssm_dt_bias", "ssm_a_log", "ssm_d", "ssm_norm_w", "gdn_conv_w",
           "gdn_a_log", "gdn_dt_bias", "gdn_norm_w", "w_proj_ssm", "w_proj_gdn", "w_out", "norm_ffn_w", "w_ffn_in",
           "w_ffn_down", "final_norm_w")


FIRST_USED = ("w_in", "ssm_conv_w", "gdn_conv_w")


def _gather_layer(l, w):
    conv_names = [nm for nm, _ in CONVS]
    groups = ([s for s in BIG + CONVS if s[0] in FIRST_USED], [s for s in BIG + CONVS if s[0] not in FIRST_USED])
    gathered = []
    for i, (specs, tag) in enumerate(zip(groups, ("first", "rest"))):
        shards = [w[nm][l] if nm in conv_names else w[nm][l].astype(BF16) for nm, _ in specs]
        gathered.append(_allgather_seq(f"l{l}_gather_{tag}", shards, collective_id=2 * l + i))
    small = {nm: w[nm][l] for nm in SMALL}

    def use(i, act):
        act, blocks = lax.optimization_barrier((act, gathered[i]))
        return act, {nm: _w_in_from_blocks(g) if nm == "w_in" else _from_gathered(g, axis)
                     for (nm, axis), g in zip(groups[i], blocks)}

    def full_weights(x):
        x, out = use(0, x)
        out.update(small)
        out["late"] = lambda y: use(1, y)
        return x, out

    return full_weights


EARLY_GRADS = ("w_ffn_down", "w_ffn_in", "w_out", "w_proj_ssm", "w_proj_gdn")


class _GradReduceScatter:
    def __init__(self, tag, specs, grads):
        self.tag = tag
        self.specs = specs
        self.blocks = [_w_in_to_blocks(grads[nm]) if nm == "w_in" else _to_dest_major(grads[nm], axis)
                       for nm, axis in specs]

    def _tied(self, started, acts):
        *acts, self.token = lax.optimization_barrier((*acts, started[4]))
        return acts

    def start(self, *acts):
        cc = lax.axis_index("c")
        self.keep = [lax.dynamic_index_in_dim(b.reshape((4, 2) + b.shape[1:]), cc, axis=1, keepdims=False)
                     for b in self.blocks]
        self.to_sibling = _copies_start(f"{self.tag}_to_sibling_start", _sibling_plan, 4, self.blocks, 4)
        return self._tied(self.to_sibling, acts)

    def mid(self, *acts):
        got = _copies_wait(f"{self.tag}_to_sibling_wait", _sibling_plan, self.to_sibling, (acts[0], self.token))
        chip_sums = [_sum_terms(f"{self.tag}_chip_sum_{nm}", [(k[None], 0), (g[None], 0)], BF16)
                     for (nm, _), k, g in zip(self.specs, self.keep, got)]
        self.to_chips = _copies_start(f"{self.tag}_between_chips_start", _chips_plan, 3, chip_sums, 3)
        return self._tied(self.to_chips, acts)

    def end(self, after):
        after = tuple(after) if isinstance(after, (tuple, list)) else (after,)
        landed = _copies_wait(f"{self.tag}_between_chips_wait", _chips_plan, self.to_chips, (*after, self.token))
        my_chip = 2 * lax.axis_index("x") + lax.axis_index("y")
        own = [lax.dynamic_index_in_dim(s, my_chip, axis=0, keepdims=True) for s in self.to_chips[2]]
        return {nm: _sum_terms(f"{self.tag}_total_{nm}", [(o, 0), (e, 0), (e, 1), (e, 2)], F32)
                for (nm, _), o, e in zip(self.specs, own, landed)}


class _NoReduce:
    def ffn_done(self, dx1b):
        return dx1b

    def early_ready(self, l, g, dp1, dp2):
        return dp1, dp2

    def mixers_done(self, dsm):
        return dsm


class _ReduceBesideBackward(_NoReduce):
    def __init__(self):
        self.late = None
        self.early = None
        self.shards = [dict() for _ in range(DEPTH)]

    def ffn_done(self, dx1b):
        if self.late is not None:
            (dx1b,) = self.late.mid(dx1b)
        return dx1b

    def early_ready(self, l, g, dp1, dp2):
        self.early = _GradReduceScatter(f"l{l}_early_grads", [s for s in BIG if s[0] in EARLY_GRADS], g)
        return self.early.start(dp1, dp2)

    def mixers_done(self, dsm):
        (dsm,) = self.early.mid(dsm)
        return dsm

    def layer_done(self, l, g, dxb):
        if self.late is not None:
            self.shards[l + 1].update(self.late.end(dxb))
        self.shards[l].update(self.early.end(dxb))
        self.late = _GradReduceScatter(f"l{l}_late_grads", [s for s in BIG + CONVS if s[0] not in EARLY_GRADS], g)
        (dxb,) = self.late.start(dxb)
        return dxb

    def finish_start(self, dxb):
        self.late.mid(dxb)

    def finish_end(self, after):
        return self.late.end(after)


def _allreduce_small(vecs):
    flat = jnp.concatenate(vecs)
    n = flat.shape[0]
    rows = -(-n // 128)
    rows = -(-rows // 8) * 8
    buf = jnp.pad(flat, (0, rows * 128 - n)).reshape(rows, 128)
    (allv,) = _allgather("gather_small_grads", [buf])
    tot = _sum_terms("small_grads_total", [(allv, d) for d in range(N_DEV)], F32).reshape(-1)
    out, o = [], 0
    for v in vecs:
        out.append(tot[o: o + v.shape[0]])
        o += v.shape[0]
    return out


def kernel(x, norm_mix_w, w_in, ssm_conv_w, ssm_conv_b, ssm_dt_bias, ssm_a_log, ssm_d, ssm_norm_w, gdn_conv_w, gdn_a_log, gdn_dt_bias, gdn_norm_w, w_proj_ssm, w_proj_gdn, w_out, norm_ffn_w, w_ffn_in, w_ffn_down, final_norm_w, loss_target, m_norm_mix_w, m_w_in, m_ssm_conv_w, m_ssm_conv_b, m_ssm_dt_bias, m_ssm_a_log, m_ssm_d, m_ssm_norm_w, m_gdn_conv_w, m_gdn_a_log, m_gdn_dt_bias, m_gdn_norm_w, m_w_proj_ssm, m_w_proj_gdn, m_w_out, m_norm_ffn_w, m_w_ffn_in, m_w_ffn_down, m_final_norm_w, v_norm_mix_w, v_w_in, v_ssm_conv_w, v_ssm_conv_b, v_ssm_dt_bias, v_ssm_a_log, v_ssm_d, v_ssm_norm_w, v_gdn_conv_w, v_gdn_a_log, v_gdn_dt_bias, v_gdn_norm_w, v_w_proj_ssm, v_w_proj_gdn, v_w_out, v_norm_ffn_w, v_w_ffn_in, v_w_ffn_down, v_final_norm_w):
    w = dict(norm_mix_w=norm_mix_w, w_in=w_in, ssm_conv_w=ssm_conv_w, ssm_conv_b=ssm_conv_b, ssm_dt_bias=ssm_dt_bias,
             ssm_a_log=ssm_a_log, ssm_d=ssm_d, ssm_norm_w=ssm_norm_w, gdn_conv_w=gdn_conv_w, gdn_a_log=gdn_a_log,
             gdn_dt_bias=gdn_dt_bias, gdn_norm_w=gdn_norm_w, w_proj_ssm=w_proj_ssm, w_proj_gdn=w_proj_gdn, w_out=w_out,
             norm_ffn_w=norm_ffn_w, w_ffn_in=w_ffn_in, w_ffn_down=w_ffn_down, final_norm_w=final_norm_w)
    m = dict(norm_mix_w=m_norm_mix_w, w_in=m_w_in, ssm_conv_w=m_ssm_conv_w, ssm_conv_b=m_ssm_conv_b, ssm_dt_bias=m_ssm_dt_bias,
             ssm_a_log=m_ssm_a_log, ssm_d=m_ssm_d, ssm_norm_w=m_ssm_norm_w, gdn_conv_w=m_gdn_conv_w, gdn_a_log=m_gdn_a_log,
             gdn_dt_bias=m_gdn_dt_bias, gdn_norm_w=m_gdn_norm_w, w_proj_ssm=m_w_proj_ssm, w_proj_gdn=m_w_proj_gdn,
             w_out=m_w_out, norm_ffn_w=m_norm_ffn_w, w_ffn_in=m_w_ffn_in, w_ffn_down=m_w_ffn_down,
             final_norm_w=m_final_norm_w)
    v = dict(norm_mix_w=v_norm_mix_w, w_in=v_w_in, ssm_conv_w=v_ssm_conv_w, ssm_conv_b=v_ssm_conv_b, ssm_dt_bias=v_ssm_dt_bias,
             ssm_a_log=v_ssm_a_log, ssm_d=v_ssm_d, ssm_norm_w=v_ssm_norm_w, gdn_conv_w=v_gdn_conv_w, gdn_a_log=v_gdn_a_log,
             gdn_dt_bias=v_gdn_dt_bias, gdn_norm_w=v_gdn_norm_w, w_proj_ssm=v_w_proj_ssm, w_proj_gdn=v_w_proj_gdn,
             w_out=v_w_out, norm_ffn_w=v_norm_ffn_w, w_ffn_in=v_w_ffn_in, w_ffn_down=v_w_ffn_down,
             final_norm_w=v_final_norm_w)

    layers = [_gather_layer(l, w) for l in range(DEPTH)]
    loss_part, dx, lgrads, dfw, reducer = _local_step(x[0], loss_target[0], layers, final_norm_w, reduce=True)
    loss = lax.psum(loss_part, ("x", "y", "c"))
    shard_grads = reducer.shards
    late = [nm for nm, _ in BIG + CONVS if nm not in EARLY_GRADS]
    grad = {nm: [shard_grads[l][nm] for l in range(DEPTH)] for nm in EARLY_GRADS}
    small_vecs = [lgrads[l][nm].reshape(-1) for l in range(DEPTH) for nm in SMALL] + [dfw]
    small_sum = _allreduce_small(small_vecs)
    for i, nm in enumerate(SMALL):
        grad[nm] = jnp.stack([small_sum[l * len(SMALL) + i].reshape(w[nm].shape[1:]) for l in range(DEPTH)])
    grad["final_norm_w"] = small_sum[-1]

    deltas, new_m, new_v = {}, {}, {}
    for nm in [n for n in WEIGHTS if n not in late]:
        grad[nm], deltas[nm], new_m[nm], new_v[nm] = _adamw("adamw_" + nm, w[nm], grad[nm], m[nm], v[nm])
    shard_grads[0].update(reducer.finish_end([deltas[nm] for nm in EARLY_GRADS]))
    for nm in late:
        grad[nm], deltas[nm], new_m[nm], new_v[nm] = _adamw("adamw_" + nm, w[nm], [shard_grads[l][nm] for l in range(DEPTH)],
                                                            m[nm], v[nm])
    return (loss, dx[None], *[grad[nm] for nm in WEIGHTS], *[deltas[nm] for nm in WEIGHTS],
            *[new_m[nm] for nm in WEIGHTS], *[new_v[nm] for nm in WEIGHTS])
```

```python
import functools

import jax
import jax.numpy as jnp
from jax import lax
from jax.experimental import pallas as pl
from jax.experimental.pallas import tpu as pltpu
from jax.experimental.pallas import tpu_sc as plsc

F32 = jnp.float32
BF16 = jnp.bfloat16
HI = lax.Precision.HIGHEST
SDS = jax.ShapeDtypeStruct

D_MODEL = 1024
DEPTH = 2
SSM_HEADS = 16
SSM_P = 64
SSM_N = 128
SSM_GROUPS = 2
SSM_CONV = 1536
GDN_HEADS = 8
GDN_DK = 128
GDN_QKV = 3072
CONV_K = 4
CHUNK = 64
SCAN_CHUNKS_PER_STEP = 4
FFN = 2816
IN_DIM = 8736
EPS = 1e-6
N_DEV = 8

Z_OFF = 0
GZ_OFF = 1024
G1_OFF = 2048
G2_OFF = 3072
QKV_OFF = 4096
XBC_OFF = 7168
SM_OFF = 8704
PROJ_W = 8960
LANE_A = 16
LANE_B = 24
O_Z, O_XBC, O_DT, O_QKV, O_GZ, O_A, O_B, O_G1, O_G2 = 0, 1024, 2560, 2576, 5648, 6672, 6680, 6688, 7712

ADAM_LR = 0.001
ADAM_B1 = 0.9
ADAM_B2 = 0.999
ADAM_EPS = 1e-08
ADAM_WD = 0.01
ADAM_STEP = 10

V7X_VMEM_LIMIT = 48 * 1024 * 1024

NN = ((1,), (0,))
NT = ((1,), (1,))
TN = ((0,), (0,))


def _bdot(a, b, dims):
    return lax.dot_general(a.astype(BF16), b.astype(BF16), (dims, ((), ())), preferred_element_type=F32)


def _hdot(a, b, dims=NN):
    return lax.dot_general(a, b, (dims, ((), ())), precision=HI, preferred_element_type=F32)


def _sigmoid(x):
    return 1.0 / (1.0 + jnp.exp(-x))


def _softplus(x):
    return jnp.maximum(x, 0.0) + jnp.log(1.0 + jnp.exp(-jnp.abs(x)))


def _params(dims):
    return pltpu.CompilerParams(dimension_semantics=dims, vmem_limit_bytes=V7X_VMEM_LIMIT)


def _rowsum(x):
    return jnp.sum(x, axis=-1, keepdims=True)


def _colsum(x):
    return jnp.sum(x, axis=0, keepdims=True)


def _matmul(name, mode, pairs, m, n, kdim, tm, tn, tk, out_dtypes, epi=None, extras=()):
    tm, tn, tk = min(tm, m), min(tn, n), min(tk, kdim)
    nk = kdim // tk
    assert m % tm == 0 and n % tn == 0 and kdim % tk == 0, (name, m, n, kdim, tm, tn, tk)
    in_specs, args = [], []
    for a, a_off, b, b_off in pairs:
        if mode == "nn":
            in_specs.append(pl.BlockSpec((tm, tk), lambda i, j, k, o=a_off: (i, k + o)))
            in_specs.append(pl.BlockSpec((tk, tn), lambda i, j, k, o=b_off: (k, j + o)))
            dims = NN
        elif mode == "nt":
            in_specs.append(pl.BlockSpec((tm, tk), lambda i, j, k, o=a_off: (i, k + o)))
            in_specs.append(pl.BlockSpec((tn, tk), lambda i, j, k, o=b_off: (j, k + o)))
            dims = NT
        else:
            in_specs.append(pl.BlockSpec((tk, tm), lambda i, j, k, o=a_off: (k, i + o)))
            in_specs.append(pl.BlockSpec((tk, tn), lambda i, j, k, o=b_off: (k, j + o)))
            dims = TN
        args += [a, b]
    for e, e_off in extras:
        in_specs.append(pl.BlockSpec((tm, tn), lambda i, j, k, o=e_off: (i, j + o)))
        args.append(e)
    npair, nex, nout = len(pairs), len(extras), len(out_dtypes)

    def body(*refs):
        prefs = refs[: 2 * npair]
        erefs = refs[2 * npair: 2 * npair + nex]
        orefs = refs[2 * npair + nex: 2 * npair + nex + nout]

        def finish(res):
            outs = (res,) if epi is None else epi(res, *[e[...] for e in erefs])
            for o, r in zip(orefs, outs):
                o[...] = r.astype(o.dtype)

        s = _bdot(prefs[0][...], prefs[1][...], dims)
        for p in range(1, npair):
            s = s + _bdot(prefs[2 * p][...], prefs[2 * p + 1][...], dims)
        if nk == 1:
            finish(s)
            return
        acc = refs[-1]
        k = pl.program_id(2)

        @pl.when(k == 0)
        def _():
            acc[...] = s

        @pl.when(k > 0)
        def _():
            acc[...] += s

        @pl.when(k == nk - 1)
        def _():
            finish(acc[...])

    out_shape, out_specs = [], []
    for od in out_dtypes:
        if isinstance(od, tuple):
            dt, full_w, blk_w, cblk = od
            assert n == tn
            out_shape.append(SDS((m, full_w), dt))
            out_specs.append(pl.BlockSpec((tm, blk_w), lambda i, j, k, c=cblk: (i, c)))
        else:
            out_shape.append(SDS((m, n), od))
            out_specs.append(pl.BlockSpec((tm, tn), lambda i, j, k: (i, j)))
    out_shape, out_specs = tuple(out_shape), tuple(out_specs)
    res = pl.pallas_call(
        body, grid=(m // tm, n // tn, nk), in_specs=in_specs, out_specs=out_specs, out_shape=out_shape,
        scratch_shapes=[pltpu.VMEM((tm, tn), F32)] if nk > 1 else [], name=name,
        compiler_params=_params(("parallel", "parallel", "arbitrary")),
    )(*args)
    return res if nout > 1 else res[0]


def _rmsnorm_fwd(name, x, w):
    t, d = x.shape
    tm = min(512, t)

    def body(x_ref, w_ref, h_ref):
        xv = x_ref[...]
        r = lax.rsqrt(jnp.mean(xv * xv, axis=-1, keepdims=True) + EPS)
        h_ref[...] = (xv * r * w_ref[...]).astype(BF16)

    return pl.pallas_call(
        body, grid=(t // tm,),
        in_specs=[pl.BlockSpec((tm, d), lambda i: (i, 0)), pl.BlockSpec((1, d), lambda i: (0, 0))],
        out_specs=pl.BlockSpec((tm, d), lambda i: (i, 0)), out_shape=SDS((t, d), BF16), name=name,
        compiler_params=_params(("parallel",)),
    )(x, w.reshape(1, d))


def _rmsnorm_bwd(name, x, w, dh, dres):
    t, d = x.shape
    tm = min(512, t)

    def body(x_ref, w_ref, dh_ref, dres_ref, dx_ref, dxb_ref, dw_ref):
        xv = x_ref[...]
        r = lax.rsqrt(jnp.mean(xv * xv, axis=-1, keepdims=True) + EPS)
        xh = xv * r
        dhv = dh_ref[...].astype(F32)
        dxh = dhv * w_ref[...]
        dx = r * (dxh - xh * jnp.mean(dxh * xh, axis=-1, keepdims=True)) + dres_ref[...]
        dx_ref[...] = dx
        dxb_ref[...] = dx.astype(BF16)

        @pl.when(pl.program_id(0) == 0)
        def _():
            dw_ref[...] = jnp.zeros_like(dw_ref)

        dw_ref[...] += _colsum(dhv * xh)

    row = pl.BlockSpec((tm, d), lambda i: (i, 0))
    vec = pl.BlockSpec((1, d), lambda i: (0, 0))
    return pl.pallas_call(
        body, grid=(t // tm,), in_specs=[row, vec, row, row], out_specs=(row, row, vec),
        out_shape=(SDS((t, d), F32), SDS((t, d), BF16), SDS((1, d), F32)), name=name,
        compiler_params=_params(("arbitrary",)),
    )(x, w.reshape(1, d), dh, dres)


def _loss_head(name, x, w, tgt):
    t, d = x.shape
    tm = min(512, t)

    def body(x_ref, w_ref, t_ref, loss_ref, dx_ref, dxb_ref, dw_ref):
        xv = x_ref[...]
        wv = w_ref[...]
        r = lax.rsqrt(jnp.mean(xv * xv, axis=-1, keepdims=True) + EPS)
        xh = xv * r
        e = xh * wv - t_ref[...]
        dy = e * (1.0 / d)
        dxh = dy * wv
        dx = r * (dxh - xh * jnp.mean(dxh * xh, axis=-1, keepdims=True))
        dx_ref[...] = dx
        dxb_ref[...] = dx.astype(BF16)

        @pl.when(pl.program_id(0) == 0)
        def _():
            dw_ref[...] = jnp.zeros_like(dw_ref)
            loss_ref[...] = jnp.zeros_like(loss_ref)

        dw_ref[...] += _colsum(dy * xh)
        loss_ref[...] += 0.5 * jnp.sum(jnp.mean(e * e, axis=-1, keepdims=True), axis=0, keepdims=True)

    row = pl.BlockSpec((tm, d), lambda i: (i, 0))
    vec = pl.BlockSpec((1, d), lambda i: (0, 0))
    return pl.pallas_call(
        body, grid=(t // tm,), in_specs=[row, vec, row],
        out_specs=(pl.BlockSpec((1, 1), lambda i: (0, 0)), row, row, vec),
        out_shape=(SDS((1, 1), F32), SDS((t, d), F32), SDS((t, d), BF16), SDS((1, d), F32)), name=name,
        compiler_params=_params(("arbitrary",)),
    )(x, w.reshape(1, d), tgt)


def _shift_down(u, s, row):
    return jnp.where(row >= s, pltpu.roll(u, shift=s, axis=0), 0.0)


def _conv_fwd(name, src, col0, w, b):
    t = src.shape[0]
    c = w.shape[1]
    tc = 256
    assert c % tc == 0 and col0 % tc == 0

    def body(u_ref, w_ref, b_ref, o_ref, pre_ref):
        u = u_ref[...]
        wv = w_ref[...]
        row = lax.broadcasted_iota(jnp.int32, u.shape, 0)
        pre = b_ref[...] + wv[3:4, :] * u
        for s in range(1, CONV_K):
            pre = pre + wv[3 - s: 4 - s, :] * _shift_down(u, s, row)
        pre_ref[...] = pre
        o_ref[...] = pre * _sigmoid(pre)

    col = pl.BlockSpec((t, tc), lambda j: (0, j))
    return pl.pallas_call(
        body, grid=(c // tc,),
        in_specs=[pl.BlockSpec((t, tc), lambda j: (0, j + col0 // tc)), pl.BlockSpec((CONV_K, tc), lambda j: (0, j)),
                  pl.BlockSpec((1, tc), lambda j: (0, j))],
        out_specs=(col, col), out_shape=(SDS((t, c), F32), SDS((t, c), F32)), name=name,
        compiler_params=_params(("parallel",)),
    )(src, w, b)


def _place_small(name, dsm_a, dsm_b, dproj):
    t = dsm_a.shape[0]
    width = PROJ_W - SM_OFF
    tr = min(512, t)

    def body(a_ref, b_ref, dproj_ref, o_ref):
        o_ref[:, :128] = a_ref[...] + b_ref[...]
        o_ref[:, 128:] = jnp.zeros((tr, width - 128), BF16)

    row = pl.BlockSpec((tr, 128), lambda i: (i, 0))
    return pl.pallas_call(
        body, grid=(t // tr,), in_specs=[row, row, ANY],
        out_specs=pl.BlockSpec((tr, width), lambda i: (i, SM_OFF // width)), out_shape=SDS(dproj.shape, BF16),
        input_output_aliases={2: 0}, name=name, compiler_params=_params(("parallel",)),
    )(dsm_a, dsm_b, dproj)


def _conv_bwd(name, src, col0, w, pre, dact, dproj):
    t = src.shape[0]
    c = w.shape[1]
    tc = 128

    def body(u_ref, w_ref, pre_ref, da_ref, dproj_ref, du_ref, dw_ref, db_ref):
        u = u_ref[...]
        wv = w_ref[...]
        prev = pre_ref[...]
        row = lax.broadcasted_iota(jnp.int32, u.shape, 0)
        sg = _sigmoid(prev)
        dpre = da_ref[...] * (sg * (1.0 + prev * (1.0 - sg)))
        du = wv[3:4, :] * dpre
        dw_ref[3:4, :] = _colsum(dpre * u)
        for s in range(1, CONV_K):
            up = jnp.where(row < t - s, pltpu.roll(dpre, shift=t - s, axis=0), 0.0)
            du = du + wv[3 - s: 4 - s, :] * up
            dw_ref[3 - s: 4 - s, :] = _colsum(up * u)
        du_ref[...] = du.astype(BF16)
        db_ref[...] = _colsum(dpre)

    col = pl.BlockSpec((t, tc), lambda j: (0, j))
    return pl.pallas_call(
        body, grid=(c // tc,),
        in_specs=[pl.BlockSpec((t, tc), lambda j: (0, j + col0 // tc)), pl.BlockSpec((CONV_K, tc), lambda j: (0, j)),
                  col, col, ANY],
        out_specs=(pl.BlockSpec((t, tc), lambda j: (0, j + col0 // tc)), pl.BlockSpec((CONV_K, tc), lambda j: (0, j)),
                   pl.BlockSpec((1, tc), lambda j: (0, j))),
        out_shape=(SDS(dproj.shape, BF16), SDS((CONV_K, c), F32), SDS((1, c), F32)), name=name,
        input_output_aliases={4: 0},
        compiler_params=_params(("parallel",)),
    )(src, w, pre, dact, dproj)


def _tri(q):
    ii = lax.broadcasted_iota(jnp.int32, (q, q), 0)
    jj = lax.broadcasted_iota(jnp.int32, (q, q), 1)
    return ii, jj


def _dot01(x, r01, dims, terms=3):
    out, rem = None, x
    for i in range(terms):
        hi = rem.astype(BF16)
        d = lax.dot_general(hi, r01, (dims, ((), ())), preferred_element_type=F32)
        out = d if out is None else out + d
        if i + 1 < terms:
            rem = rem - hi.astype(F32)
    return out


def _ssd_common(act, sm, dtb, arow, rmat):
    q = CHUNK
    ii, jj = _tri(q)
    lane = lax.broadcasted_iota(jnp.int32, (q, 128), 1)
    m16 = lane < SSM_HEADS
    dt = jnp.where(m16, _softplus(sm + dtb), 0.0)
    a = dt * arow
    tril = (ii >= jj).astype(F32)
    triu = (ii <= jj).astype(F32)
    acum = _hdot(tril, a)
    acum_r = _hdot(a.T, triu)
    dtx = _dot01(dt, rmat, NN)
    acx = _dot01(acum, rmat, NN)
    ex = jnp.exp(acx)
    alx = acx[q - 1: q, :]
    dex = jnp.exp(alx - acx)
    xs = act[:, :1024]
    return dict(ii=ii, jj=jj, m16=m16, dt=dt, a=a, triu=triu, acum=acum, acum_r=acum_r, dtx=dtx, ex=ex, dex=dex,
                elx=jnp.exp(alx), xs=xs, x=xs * dtx)


def _ssd_lmat(cm, h):
    return jnp.where(cm["ii"] >= cm["jj"], jnp.exp(cm["acum"][:, h: h + 1] - cm["acum_r"][h: h + 1, :]), 0.0)


def _ssd_fwd(name, act, proj, dtb, arow, dxrow, nw, rmat):
    t = act.shape[0]
    q = CHUNK
    nc = t // q
    hg = SSM_HEADS // SSM_GROUPS
    gw = hg * SSM_P

    def body(act_ref, z_ref, sm_ref, dtb_ref, arow_ref, dx_ref, nw_ref, r_ref, y_ref, ys_ref, st_ref, s_scr, yd_scr):
        @pl.when(pl.program_id(0) == 0)
        def _():
            s_scr[...] = jnp.zeros_like(s_scr)

        s_all = s_scr[...]
        for sub in range(cps):
            rows = pl.ds(sub * q, q)
            s_all = chunk(act_ref.at[rows, :], z_ref.at[rows, :], sm_ref.at[rows, :], dtb_ref, arow_ref, dx_ref, nw_ref, r_ref,
                          y_ref.at[rows, :], ys_ref.at[rows, :], st_ref.at[sub], yd_scr.at[rows, :], s_all)
        s_scr[...] = s_all

    def chunk(act_ref, z_ref, sm_ref, dtb_ref, arow_ref, dx_ref, nw_ref, r_ref, y_ref, ys_ref, st_ref, yd_scr, s_all):
        st_ref[...] = s_all
        actv = act_ref[...]
        cm = _ssd_common(actv, sm_ref[...], dtb_ref[...], arow_ref[...], r_ref[...])
        x = cm["x"]
        xd = x * cm["dex"]
        yoffs, snew = [], []
        for g in range(SSM_GROUPS):
            bg = actv[:, 1024 + g * SSM_N: 1024 + (g + 1) * SSM_N]
            cg = actv[:, 1280 + g * SSM_N: 1280 + (g + 1) * SSM_N]
            sg = s_all[:, g * gw: (g + 1) * gw]
            cb = _bdot(cg, bg, NT)
            yoffs.append(_bdot(cg, sg, NN))
            snew.append(_bdot(bg, xd[:, g * gw: (g + 1) * gw], TN))
            for r in range(hg):
                h = g * hg + r
                mm = cb * _ssd_lmat(cm, h)
                yd_scr[:, h * SSM_P: (h + 1) * SSM_P] = _bdot(mm, x[:, h * SSM_P: (h + 1) * SSM_P], NN)
        s_next = s_all * cm["elx"] + jnp.concatenate(snew, axis=1)
        ysc = yd_scr[...] + jnp.concatenate(yoffs, axis=1) * cm["ex"]
        ys_ref[...] = ysc
        zv = z_ref[...]
        yg = (ysc + dx_ref[...] * cm["xs"]) * (zv * _sigmoid(zv))
        nwv = nw_ref[...]
        for g in range(SSM_GROUPS):
            sl = yg[:, g * gw: (g + 1) * gw]
            rr = lax.rsqrt(jnp.mean(sl * sl, axis=-1, keepdims=True) + EPS)
            y_ref[:, g * gw: (g + 1) * gw] = (sl * rr * nwv[:, g * gw: (g + 1) * gw]).astype(BF16)
        return s_next

    cps = SCAN_CHUNKS_PER_STEP if nc % SCAN_CHUNKS_PER_STEP == 0 else 1
    qq = cps * q
    vec128 = pl.BlockSpec((1, 128), lambda c: (0, 0))
    vec1k = pl.BlockSpec((1, 1024), lambda c: (0, 0))
    return pl.pallas_call(
        body, grid=(nc // cps,),
        in_specs=[pl.BlockSpec((qq, SSM_CONV), lambda c: (c, 0)), pl.BlockSpec((qq, 1024), lambda c: (c, Z_OFF // 1024)),
                  pl.BlockSpec((qq, 128), lambda c: (c, SM_OFF // 128)), vec128, vec128, vec1k, vec1k,
                  pl.BlockSpec((128, 1024), lambda c: (0, 0))],
        out_specs=(pl.BlockSpec((qq, 1024), lambda c: (c, 0)), pl.BlockSpec((qq, 1024), lambda c: (c, 0)),
                   pl.BlockSpec((cps, 128, 1024), lambda c: (c, 0, 0))),
        out_shape=(SDS((t, 1024), BF16), SDS((t, 1024), F32), SDS((nc, 128, 1024), F32)),
        scratch_shapes=[pltpu.VMEM((128, 1024), F32), pltpu.VMEM((qq, 1024), F32)], name=name,
        compiler_params=_params(("arbitrary",)),
    )(act, proj, proj, dtb, arow, dxrow, nw, rmat)


def _ssd_bwd(name, act, proj, dtb, arow, dxrow, nw, rmat, ysc, states, dy, dproj):
    t = act.shape[0]
    q = CHUNK
    nc = t // q
    hg = SSM_HEADS // SSM_GROUPS
    gw = hg * SSM_P

    def body(act_ref, z_ref, sm_ref, dtb_ref, arow_ref, dx_ref, nw_ref, r_ref, ys_ref, st_ref, dy_ref, dproj_ref,
             dact_ref, dz_ref, dsm_ref, dnw_ref, dd_ref, dal_ref, ddtb_ref, ds_scr, dxd_scr):
        @pl.when(pl.program_id(0) == 0)
        def _():
            ds_scr[...] = jnp.zeros_like(ds_scr)
            dnw_ref[...] = jnp.zeros_like(dnw_ref)
            dd_ref[...] = jnp.zeros_like(dd_ref)
            dal_ref[...] = jnp.zeros_like(dal_ref)
            ddtb_ref[...] = jnp.zeros_like(ddtb_ref)

        dsn = ds_scr[...]
        for sub in reversed(range(cps)):
            rows = pl.ds(sub * q, q)
            dsn = chunk(act_ref.at[rows, :], z_ref.at[rows, :], sm_ref.at[rows, :], dtb_ref, arow_ref, dx_ref, nw_ref, r_ref,
                        ys_ref.at[rows, :], st_ref.at[sub], dy_ref.at[rows, :], dact_ref.at[rows, :], dz_ref.at[rows, :],
                        dsm_ref.at[rows, :], dnw_ref, dd_ref, dal_ref, ddtb_ref, dxd_scr.at[rows, :], dsn)
        ds_scr[...] = dsn

    def chunk(act_ref, z_ref, sm_ref, dtb_ref, arow_ref, dx_ref, nw_ref, r_ref, ys_ref, st_ref, dy_ref,
              dact_ref, dz_ref, dsm_ref, dnw_ref, dd_ref, dal_ref, ddtb_ref, dxd_scr, dsn):
        actv = act_ref[...]
        smv = sm_ref[...]
        rmat_v = r_ref[...]
        cm = _ssd_common(actv, smv, dtb_ref[...], arow_ref[...], rmat_v)
        ii, jj = cm["ii"], cm["jj"]
        x, xs = cm["x"], cm["xs"]
        s_all = st_ref[...]
        ysv = ys_ref[...]
        dxr = dx_ref[...]
        y = ysv + dxr * xs
        zv = z_ref[...]
        sz = _sigmoid(zv)
        silz = zv * sz
        yg = y * silz
        dout = dy_ref[...]
        nwv = nw_ref[...]
        dyn = dout * nwv
        yn_parts, dyg_parts = [], []
        for g in range(SSM_GROUPS):
            sl = yg[:, g * gw: (g + 1) * gw]
            rr = lax.rsqrt(jnp.mean(sl * sl, axis=-1, keepdims=True) + EPS)
            yn = sl * rr
            dn = dyn[:, g * gw: (g + 1) * gw]
            yn_parts.append(yn)
            dyg_parts.append(rr * (dn - yn * jnp.mean(dn * yn, axis=-1, keepdims=True)))
        dnw_ref[...] += _colsum(dout * jnp.concatenate(yn_parts, axis=1))
        dyg = jnp.concatenate(dyg_parts, axis=1)
        dyv = dyg * silz
        dz_ref[...] = (dyg * y * (sz * (1.0 + zv * (1.0 - sz)))).astype(BF16)
        dd_ref[...] += _dot01(_colsum(dyv * xs), rmat_v, NT)
        dxs = dyv * dxr
        dcs = dyv * cm["ex"]
        xd = x * cm["dex"]
        dxst_parts, ds_parts, db_parts, dc_parts, yoff_parts, wcol_rows = [], [], [], [], [], []
        lane128 = lax.broadcasted_iota(jnp.int32, (q, 128), 1)
        wrow = jnp.zeros((q, 128), F32)
        for g in range(SSM_GROUPS):
            bg = actv[:, 1024 + g * SSM_N: 1024 + (g + 1) * SSM_N]
            cg = actv[:, 1280 + g * SSM_N: 1280 + (g + 1) * SSM_N]
            sg = s_all[:, g * gw: (g + 1) * gw]
            dsng = dsn[:, g * gw: (g + 1) * gw]
            dcsg = dcs[:, g * gw: (g + 1) * gw]
            dcg = _bdot(dcsg, sg, NT)
            yoff_parts.append(_bdot(cg, sg, NN))
            ds_parts.append(_bdot(cg, dcsg, TN))
            dxst_parts.append(_bdot(bg, dsng, NN))
            dbg = _bdot(xd[:, g * gw: (g + 1) * gw], dsng, NT)
            cb = _bdot(cg, bg, NT)
            dcb = jnp.zeros((q, q), F32)
            for r in range(hg):
                h = g * hg + r
                lm = _ssd_lmat(cm, h)
                mm = cb * lm
                dyh = dyv[:, h * SSM_P: (h + 1) * SSM_P]
                dm = jnp.where(ii >= jj, _bdot(dyh, x[:, h * SSM_P: (h + 1) * SSM_P], NT), 0.0)
                dxd_scr[:, h * SSM_P: (h + 1) * SSM_P] = _bdot(mm, dyh, TN)
                dcb = dcb + dm * lm
                wm = dm * mm
                wrow = wrow + jnp.where(lane128 == h, _rowsum(wm), 0.0)
                wcol_rows.append(_colsum(wm))
            dc_parts.append(dcg + _bdot(dcb, bg, NN))
            db_parts.append(dbg + _bdot(dcb, cg, TN))
        dxst = jnp.concatenate(dxst_parts, axis=1) * cm["dex"]
        dx = dxd_scr[...] + dxst
        ds_prev = jnp.concatenate(ds_parts, axis=1) + dsn * cm["elx"]
        wcol = jnp.concatenate(wcol_rows + [jnp.zeros((128 - SSM_HEADS, q), F32)], axis=0).T
        yoff = jnp.concatenate(yoff_parts, axis=1) * cm["ex"]
        xdxst = x * dxst
        dac = wrow - wcol + _dot01(dyv * yoff - xdxst, rmat_v, NT)
        last = _dot01(_colsum(dsn * s_all) * cm["elx"] + _colsum(xdxst), rmat_v, NT)
        rowq = lax.broadcasted_iota(jnp.int32, (q, 128), 0)
        dac = dac + jnp.where(rowq == q - 1, last, 0.0)
        da = _hdot(cm["triu"], dac)
        arow_v = arow_ref[...]
        ddt = da * arow_v + _dot01(dx * xs, rmat_v, NT)
        dxs = dxs + dx * cm["dtx"]
        dal_ref[...] += _colsum(da * cm["a"])
        ddtraw = jnp.where(cm["m16"], ddt * _sigmoid(smv + dtb_ref[...]), 0.0)
        ddtb_ref[...] += _colsum(ddtraw)
        dsm_ref[...] = ddtraw.astype(BF16)
        dact_ref[:, :1024] = dxs
        for g in range(SSM_GROUPS):
            dact_ref[:, 1024 + g * SSM_N: 1024 + (g + 1) * SSM_N] = db_parts[g]
            dact_ref[:, 1280 + g * SSM_N: 1280 + (g + 1) * SSM_N] = dc_parts[g]
        return ds_prev

    cps = SCAN_CHUNKS_PER_STEP if nc % SCAN_CHUNKS_PER_STEP == 0 else 1
    qq = cps * q
    rev = lambda c: nc // cps - 1 - c
    vec128 = pl.BlockSpec((1, 128), lambda c: (0, 0))
    vec1k = pl.BlockSpec((1, 1024), lambda c: (0, 0))
    return pl.pallas_call(
        body, grid=(nc // cps,),
        in_specs=[pl.BlockSpec((qq, SSM_CONV), lambda c: (rev(c), 0)),
                  pl.BlockSpec((qq, 1024), lambda c: (rev(c), Z_OFF // 1024)),
                  pl.BlockSpec((qq, 128), lambda c: (rev(c), SM_OFF // 128)), vec128, vec128, vec1k, vec1k,
                  pl.BlockSpec((128, 1024), lambda c: (0, 0)),
                  pl.BlockSpec((qq, 1024), lambda c: (rev(c), 0)), pl.BlockSpec((cps, 128, 1024), lambda c: (rev(c), 0, 0)),
                  pl.BlockSpec((qq, 1024), lambda c: (rev(c), 0)), ANY],
        out_specs=(pl.BlockSpec((qq, SSM_CONV), lambda c: (rev(c), 0)),
                   pl.BlockSpec((qq, 1024), lambda c: (rev(c), Z_OFF // 1024)),
                   pl.BlockSpec((qq, 128), lambda c: (rev(c), 0)), vec1k, vec128, vec128, vec128),
        out_shape=(SDS((t, SSM_CONV), F32), SDS(dproj.shape, BF16), SDS((t, 128), BF16), SDS((1, 1024), F32),
                   SDS((1, 128), F32), SDS((1, 128), F32), SDS((1, 128), F32)),
        input_output_aliases={11: 1},
        scratch_shapes=[pltpu.VMEM((128, 1024), F32), pltpu.VMEM((qq, 1024), F32)], name=name,
        compiler_params=_params(("arbitrary",)),
    )(act, proj, proj, dtb, arow, dxrow, nw, rmat, ysc, states, dy, dproj)


def _split(a):
    hi = a.astype(BF16)
    return hi, (a - hi.astype(F32)).astype(BF16)


def _dot3(a, b, dims=NN):
    (ah, al), (bh, bl) = a, b

    def d(x, y):
        return lax.dot_general(x, y, (dims, ((), ())), preferred_element_type=F32)

    return d(ah, bh) + (d(ah, bl) + d(al, bh))


def _tri_inverses(amats, ii, jj):
    eye = jnp.where(ii == jj, 1.0, 0.0)
    tms = [eye - a for a in amats]
    sp = [_split(a) for a in amats]
    for _ in range(5):
        sp = [_split(_dot3(s, s)) for s in sp]
        tms = [t + _dot3(_split(t), s) for t, s in zip(tms, sp)]
    return tms


def _gdn_common(sm, gb, garow):
    q = CHUNK
    ii, jj = _tri(q)
    lane = lax.broadcasted_iota(jnp.int32, (q, 128), 1)
    ma = (lane >= LANE_A) & (lane < LANE_A + GDN_HEADS)
    spre = sm + gb
    g = jnp.where(ma, garow * _softplus(spre), 0.0)
    beta = _sigmoid(sm)
    tril = (ii >= jj).astype(F32)
    triu = (ii <= jj).astype(F32)
    gc = _hdot(tril, g)
    gc_r = _hdot(g.T, triu)
    return dict(ii=ii, jj=jj, lane=lane, ma=ma, spre=spre, g=g, beta=beta, triu=triu, gc=gc, gc_r=gc_r)


def _each(f, *lists):
    return [f(*xs) for xs in zip(*lists)]


GDN_SCALE = GDN_DK ** -0.5


def _gdn_heads(cm, actv, states):
    q = CHUNK
    ii, jj = cm["ii"], cm["jj"]
    heads = range(GDN_HEADS)
    qr = [actv[:, h * 128: (h + 1) * 128] for h in heads]
    kr = [actv[:, 1024 + h * 128: 1024 + (h + 1) * 128] for h in heads]
    v = [actv[:, 2048 + h * 128: 2048 + (h + 1) * 128] for h in heads]
    rq = _each(lambda x: lax.rsqrt(_rowsum(x * x) + EPS), qr)
    rk = _each(lambda x: lax.rsqrt(_rowsum(x * x) + EPS), kr)
    qn = _each(lambda x, r: x * r * GDN_SCALE, qr, rq)
    kn = _each(lambda x, r: x * r, kr, rk)
    gcc = [cm["gc"][:, LANE_A + h: LANE_A + h + 1] for h in heads]
    gcr = [cm["gc_r"][LANE_A + h: LANE_A + h + 1, :] for h in heads]
    bcol = [cm["beta"][:, LANE_B + h: LANE_B + h + 1] for h in heads]
    dm = _each(lambda c, r: jnp.where(ii >= jj, jnp.exp(c - r), 0.0), gcc, gcr)
    kq = _each(lambda k, a: _bdot(jnp.concatenate([k, a], axis=0), k, NT), kn, qn)
    ak = _each(lambda x, d: jnp.where(ii > jj, x[:q] * d, 0.0), kq, dm)
    qkm = _each(lambda x, d: jnp.where(ii >= jj, x[q:] * d, 0.0), kq, dm)
    tm = _tri_inverses(_each(lambda a, b: a * b, ak, bcol), ii, jj)
    eg = _each(jnp.exp, gcc)
    gl = [c[q - 1: q, :] for c in gcc]
    rm = _each(lambda vv, k, b, e: jnp.concatenate([vv * b, k * (b * e)], axis=1), v, kn, bcol, eg)
    tt = _each(lambda t, r: _dot3(_split(t), _split(r)), tm, rm)
    w = [t[:, 128:] for t in tt]
    qg = _each(lambda a, e: a * e, qn, eg)
    ws = _each(lambda ww, a, s: _bdot(jnp.concatenate([ww, a], axis=0), s, NN), w, qg, states)
    vnew = _each(lambda t, x: t[:, :128] - x[:q], tt, ws)
    return dict(qr=qr, v=v, rq=rq, rk=rk, qn=qn, kn=kn, gcc=gcc, bcol=bcol, dm=dm, ak=ak, tm=tm, eg=eg, gl=gl,
                egl=_each(jnp.exp, gl), ed=_each(lambda g, c: jnp.exp(g - c), gl, gcc), tt=tt, w=w, vnew=vnew, qkm=qkm,
                qg=qg, qgs=[x[q:] for x in ws])


def _gdn_fwd(name, act, proj, gb, garow, gnw):
    t = act.shape[0]
    q = CHUNK
    nc = t // q

    def body(act_ref, gz_ref, sm_ref, gb_ref, ga_ref, nw_ref, y_ref, o_ref, st_ref, s_scr):
        @pl.when(pl.program_id(0) == 0)
        def _():
            s_scr[...] = jnp.zeros_like(s_scr)

        states = [s_scr[h * 128: (h + 1) * 128, :] for h in range(GDN_HEADS)]
        for sub in range(cps):
            rows = pl.ds(sub * q, q)
            states = chunk(act_ref.at[rows, :], gz_ref.at[rows, :], sm_ref.at[rows, :], gb_ref, ga_ref, nw_ref,
                           y_ref.at[rows, :], o_ref.at[rows, :], st_ref.at[sub], states)
        for h in range(GDN_HEADS):
            s_scr[h * 128: (h + 1) * 128, :] = states[h]

    def chunk(act_ref, gz_ref, sm_ref, gb_ref, ga_ref, nw_ref, y_ref, o_ref, st_ref, states):
        for h in range(GDN_HEADS):
            st_ref[h * 128: (h + 1) * 128, :] = states[h]
        actv = act_ref[...]
        cm = _gdn_common(sm_ref[...], gb_ref[...], ga_ref[...])
        nwv = nw_ref[...]
        gzv = gz_ref[...]
        hd = _gdn_heads(cm, actv, states)
        outs = _each(lambda qs, m, vn: qs + _bdot(m, vn, NN), hd["qgs"], hd["qkm"], hd["vnew"])
        snew = _each(lambda s, e, k, d, vn: s * e + _bdot(k * d, vn, TN), states, hd["egl"], hd["kn"], hd["ed"], hd["vnew"])
        for h in range(GDN_HEADS):
            o = outs[h]
            o_ref[:, h * 128: (h + 1) * 128] = o
            rr = lax.rsqrt(jnp.mean(o * o, axis=-1, keepdims=True) + EPS)
            gz = gzv[:, h * 128: (h + 1) * 128]
            y_ref[:, h * 128: (h + 1) * 128] = (o * rr * nwv * (gz * _sigmoid(gz))).astype(BF16)
        return snew

    cps = SCAN_CHUNKS_PER_STEP if nc % SCAN_CHUNKS_PER_STEP == 0 else 1
    qq = cps * q
    vec128 = pl.BlockSpec((1, 128), lambda c: (0, 0))
    return pl.pallas_call(
        body, grid=(nc // cps,),
        in_specs=[pl.BlockSpec((qq, GDN_QKV), lambda c: (c, 0)), pl.BlockSpec((qq, 1024), lambda c: (c, GZ_OFF // 1024)),
                  pl.BlockSpec((qq, 128), lambda c: (c, SM_OFF // 128)), vec128, vec128, vec128],
        out_specs=(pl.BlockSpec((qq, 1024), lambda c: (c, 0)), pl.BlockSpec((qq, 1024), lambda c: (c, 0)),
                   pl.BlockSpec((cps, 1024, 128), lambda c: (c, 0, 0))),
        out_shape=(SDS((t, 1024), BF16), SDS((t, 1024), F32), SDS((nc, 1024, 128), F32)),
        scratch_shapes=[pltpu.VMEM((1024, 128), F32)], name=name, compiler_params=_params(("arbitrary",)),
    )(act, proj, proj, gb, garow, gnw)


def _gdn_bwd(name, act, proj, gb, garow, gnw, oraw, states, dy, dproj):
    t = act.shape[0]
    q = CHUNK
    nc = t // q

    def body(act_ref, gz_ref, sm_ref, gb_ref, ga_ref, nw_ref, o_ref, st_ref, dy_ref, dproj_ref,
             dact_ref, dgz_ref, dsm_ref, dnw_ref, dal_ref, dgb_ref, ds_scr):
        @pl.when(pl.program_id(0) == 0)
        def _():
            ds_scr[...] = jnp.zeros_like(ds_scr)
            dnw_ref[...] = jnp.zeros_like(dnw_ref)
            dal_ref[...] = jnp.zeros_like(dal_ref)
            dgb_ref[...] = jnp.zeros_like(dgb_ref)

        dsn = [ds_scr[h * 128: (h + 1) * 128, :] for h in range(GDN_HEADS)]
        for sub in reversed(range(cps)):
            rows = pl.ds(sub * q, q)
            dsn = chunk(act_ref.at[rows, :], gz_ref.at[rows, :], sm_ref.at[rows, :], gb_ref, ga_ref, nw_ref,
                        o_ref.at[rows, :], st_ref.at[sub], dy_ref.at[rows, :],
                        dact_ref.at[rows, :], dgz_ref.at[rows, :], dsm_ref.at[rows, :], dnw_ref, dal_ref, dgb_ref, dsn)
        for h in range(GDN_HEADS):
            ds_scr[h * 128: (h + 1) * 128, :] = dsn[h]

    def chunk(act_ref, gz_ref, sm_ref, gb_ref, ga_ref, nw_ref, o_ref, st_ref, dy_ref,
              dact_ref, dgz_ref, dsm_ref, dnw_ref, dal_ref, dgb_ref, dsn):
        actv = act_ref[...]
        smv = sm_ref[...]
        garow_v = ga_ref[...]
        cm = _gdn_common(smv, gb_ref[...], garow_v)
        ii, jj, lane = cm["ii"], cm["jj"], cm["lane"]
        nwv = nw_ref[...]
        rowq = lax.broadcasted_iota(jnp.int32, (q, 1), 0)
        dgc_all = jnp.zeros((q, 128), F32)
        dbeta_all = jnp.zeros((q, 128), F32)
        dnw_acc = jnp.zeros((1, 128), F32)
        heads = range(GDN_HEADS)
        sts = [st_ref[h * 128: (h + 1) * 128, :] for h in heads]
        ds_out = []
        ov, gzv, dyv = o_ref[...], gz_ref[...], dy_ref[...]
        hd = _gdn_heads(cm, actv, sts)
        qn, kn, v, eg, ed, egl, bcol = hd["qn"], hd["kn"], hd["v"], hd["eg"], hd["ed"], hd["egl"], hd["bcol"]
        vnew, qkm, qg, w, tt, dm, ak = hd["vnew"], hd["qkm"], hd["qg"], hd["w"], hd["tt"], hd["dm"], hd["ak"]
        do = []
        for h in heads:
            hs = slice(h * 128, (h + 1) * 128)
            o = ov[:, hs]
            rr = lax.rsqrt(jnp.mean(o * o, axis=-1, keepdims=True) + EPS)
            on = o * rr
            gz = gzv[:, hs]
            sz = _sigmoid(gz)
            silz = gz * sz
            dyh = dyv[:, hs]
            dnw_acc = dnw_acc + _colsum(dyh * on * silz)
            dgz_ref[:, hs] = (dyh * on * nwv * (sz * (1.0 + gz * (1.0 - sz)))).astype(BF16)
            don = dyh * nwv * silz
            do.append(rr * (don - on * jnp.mean(don * on, axis=-1, keepdims=True)))
        kd = _each(lambda k, e: k * e, kn, ed)
        dkd = _each(lambda vn, d: _bdot(vn, d, NT), vnew, dsn)
        dvnew_a = _each(lambda k, d: _bdot(k, d, NN), kd, dsn)
        ded = _each(lambda a, b: _rowsum(a * b), dkd, kd)
        dgl = _each(lambda d, s, e, de: jnp.sum(_rowsum(d * s), axis=0, keepdims=True) * e + _colsum(de), dsn, sts, egl, ded)
        dqk = _each(lambda d, vn: jnp.where(ii >= jj, _bdot(d, vn, NT), 0.0), do, vnew)
        dvnew = _each(lambda a, m, d: a + _bdot(m, d, TN), dvnew_a, qkm, do)
        pq = _each(lambda a, b: a * b, dqk, dm)
        w1 = _each(lambda a, b: a * b, dqk, qkm)
        dod = _each(lambda a, b: jnp.concatenate([a, b], axis=0), do, dvnew)
        dos = _each(lambda x, s: _bdot(x, s, NT), dod, sts)
        dqg = [x[:q] for x in dos]
        dw = [-x[q:] for x in dos]
        ds12 = _each(lambda a, ww, x: _bdot(jnp.concatenate([a, -ww], axis=0), x, TN), qg, w, dod)
        dr = _each(lambda t, a, b: _dot3(_split(t), _split(jnp.concatenate([a, b], axis=1)), TN), hd["tm"], dvnew, dw)
        da = _each(lambda r, t: jnp.where(ii > jj, -_dot3(_split(r), _split(t), NT), 0.0), dr, tt)
        sk = _each(lambda r, k: _rowsum(r[:, 128:] * k), dr, kn)
        pk = _each(lambda a, d, b: a * d * b, da, dm, bcol)
        pkn = _each(lambda p, pp, k: _bdot(jnp.concatenate([p, pp + pp.T], axis=0), k, NN), pq, pk, kn)
        dq = _each(lambda a, e, x: a * e + x[:q], dqg, eg, pkn)
        dk = _each(lambda a, e, p, x, r, b, eg_, y: a * e + _bdot(p, x, TN) + r[:, 128:] * (b * eg_) + y[q:],
                   dkd, ed, pq, qn, dr, bcol, eg, pkn)
        w2 = _each(lambda a, k, b: a * (k * b), da, ak, bcol)
        for h in heads:
            hs = slice(h * 128, (h + 1) * 128)
            dgc = (-ded[h] + _rowsum(dqg[h] * qg[h]) + _rowsum(w1[h]) - _rowsum(w1[h].T) + sk[h] * bcol[h] * eg[h]
                   + _rowsum(w2[h]) - _rowsum(w2[h].T) + jnp.where(rowq == q - 1, dgl[h], 0.0))
            dbeta = _rowsum(dr[h][:, :128] * v[h]) + sk[h] * eg[h] + _rowsum(da[h] * ak[h])
            qhat = hd["qr"][h] * hd["rq"][h]
            dqhat = dq[h] * GDN_SCALE
            dact_ref[:, hs] = hd["rq"][h] * (dqhat - qhat * _rowsum(dqhat * qhat))
            dact_ref[:, 1024 + h * 128: 1024 + (h + 1) * 128] = hd["rk"][h] * (dk[h] - kn[h] * _rowsum(dk[h] * kn[h]))
            dact_ref[:, 2048 + h * 128: 2048 + (h + 1) * 128] = dr[h][:, :128] * bcol[h]
            dgc_all = dgc_all + jnp.where(lane == LANE_A + h, dgc, 0.0)
            dbeta_all = dbeta_all + jnp.where(lane == LANE_B + h, dbeta, 0.0)
            ds_out.append(dsn[h] * egl[h] + ds12[h])
        dnw_ref[...] += dnw_acc
        dg = _hdot(cm["triu"], dgc_all)
        da_raw = jnp.where(cm["ma"], dg * garow_v * _sigmoid(cm["spre"]), 0.0)
        dal_ref[...] += _colsum(dg * cm["g"])
        dgb_ref[...] += _colsum(da_raw)
        beta = cm["beta"]
        dsm_ref[...] = (da_raw + dbeta_all * beta * (1.0 - beta)).astype(BF16)
        return ds_out

    cps = 2 if nc % 2 == 0 else 1
    qq = cps * q
    rev = lambda c: nc // cps - 1 - c
    vec128 = pl.BlockSpec((1, 128), lambda c: (0, 0))
    return pl.pallas_call(
        body, grid=(nc // cps,),
        in_specs=[pl.BlockSpec((qq, GDN_QKV), lambda c: (rev(c), 0)),
                  pl.BlockSpec((qq, 1024), lambda c: (rev(c), GZ_OFF // 1024)),
                  pl.BlockSpec((qq, 128), lambda c: (rev(c), SM_OFF // 128)), vec128, vec128, vec128,
                  pl.BlockSpec((qq, 1024), lambda c: (rev(c), 0)), pl.BlockSpec((cps, 1024, 128), lambda c: (rev(c), 0, 0)),
                  pl.BlockSpec((qq, 1024), lambda c: (rev(c), 0)), ANY],
        out_specs=(pl.BlockSpec((qq, GDN_QKV), lambda c: (rev(c), 0)),
                   pl.BlockSpec((qq, 1024), lambda c: (rev(c), GZ_OFF // 1024)),
                   pl.BlockSpec((qq, 128), lambda c: (rev(c), 0)), vec128, vec128, vec128),
        out_shape=(SDS((t, GDN_QKV), F32), SDS(dproj.shape, BF16), SDS((t, 128), BF16), SDS((1, 128), F32),
                   SDS((1, 128), F32), SDS((1, 128), F32)),
        input_output_aliases={9: 1},
        scratch_shapes=[pltpu.VMEM((1024, 128), F32)], name=name, compiler_params=_params(("arbitrary",)),
    )(act, proj, proj, gb, garow, gnw, oraw, states, dy, dproj)


def _row_tile(r):
    for cand in (512, 256, 128, 64, 32, 16, 8):
        if r % cand == 0:
            return cand
    return r


def _sum_terms(name, terms, out_dtype):
    shape = terms[0][0].shape[1:]
    c = shape[-1]
    r = 1
    for s in shape[:-1]:
        r *= s
    tr = min(_row_tile(r), 256)
    n = len(terms)

    def body(*refs):
        acc = refs[0][...].astype(F32)
        for k in range(1, n):
            acc = acc + refs[k][...].astype(F32)
        refs[n][...] = acc.astype(out_dtype)

    in_specs = [pl.BlockSpec((None, tr, c), lambda i, q=lead: (q, i, 0)) for _, lead in terms]
    args = [a.reshape(a.shape[0], r, c) for a, _ in terms]
    out = pl.pallas_call(body, grid=(r // tr,), in_specs=in_specs, out_specs=pl.BlockSpec((tr, c), lambda i: (i, 0)),
                         out_shape=SDS((r, c), out_dtype), name=name, compiler_params=_params(("parallel",)))(*args)
    return out.reshape(shape)


def _adamw(name, w, g, m, v):
    shape = w.shape
    c = shape[-1]
    per_layer = isinstance(g, (list, tuple))
    nl = len(g) if per_layer else 1
    gs = [a.reshape(-1, c) for a in g] if per_layer else [g.reshape(-1, c)]
    r = gs[0].shape[0]
    w3, m3, v3 = (a.reshape(nl, r, c) for a in (w, m, v))
    tr = min(_row_tile(r), 256)

    def body(*refs):
        w_ref, m_ref, v_ref = refs[:3]
        g_refs = refs[3: 3 + nl]
        go_ref, d_ref, nm_ref, nv_ref = refs[3 + nl:]
        layer = pl.program_id(0)
        gv = g_refs[0][...]
        for k in range(1, nl):
            gv = jnp.where(layer == k, g_refs[k][...], gv)
        mn = ADAM_B1 * m_ref[...] + (1.0 - ADAM_B1) * gv
        vn = ADAM_B2 * v_ref[...] + (1.0 - ADAM_B2) * (gv * gv)
        m_hat = mn / (1.0 - ADAM_B1 ** ADAM_STEP)
        v_hat = vn / (1.0 - ADAM_B2 ** ADAM_STEP)
        go_ref[...] = gv
        d_ref[...] = -ADAM_LR * (m_hat / (jnp.sqrt(v_hat) + ADAM_EPS) + ADAM_WD * w_ref[...])
        nm_ref[...] = mn
        nv_ref[...] = vn

    spec3 = pl.BlockSpec((None, tr, c), lambda l, i: (l, i, 0))
    gspec = pl.BlockSpec((tr, c), lambda l, i: (i, 0))
    outs = pl.pallas_call(body, grid=(nl, r // tr), in_specs=[spec3] * 3 + [gspec] * nl, out_specs=(spec3,) * 4,
                          out_shape=(SDS((nl, r, c), F32),) * 4, name=name,
                          compiler_params=_params(("parallel", "parallel")))(w3, m3, v3, *gs)
    return tuple(o.reshape(shape) for o in outs)


ANY = pl.BlockSpec(memory_space=pl.ANY)
MESH = pl.DeviceIdType.MESH


def _allgather(name, xs):
    n = len(xs)

    def body(*refs):
        x_refs, out_refs = refs[:n], refs[n: 2 * n]
        send_sems, recv_sems, local_sems = refs[2 * n:]
        x, y, cc = lax.axis_index("x"), lax.axis_index("y"), lax.axis_index("c")
        me, sibling = (x, y, cc), (x, y, 1 - cc)
        chips = [(1 - x, y), (x, 1 - y), (1 - x, 1 - y)]

        def rows(a, px, py, pc):
            return out_refs[a].at[4 * px + 2 * py + pc]

        def copy(a, k, block, to, src=None):
            return pltpu.make_async_remote_copy(
                src_ref=rows(a, *block) if src is None else src, dst_ref=rows(a, *block),
                send_sem=send_sems.at[7 * a + k], recv_sem=recv_sems.at[7 * a + k], device_id=to, device_id_type=MESH)

        mine = [pltpu.make_async_copy(x_refs[a], rows(a, *me), local_sems.at[a]) for a in range(n)]
        for cp in mine:
            cp.start()
        first = []
        for a in range(n):
            first.append(copy(a, 0, me, sibling, src=x_refs[a]))
            first += [copy(a, 1 + j, me, (*chip, cc), src=x_refs[a]) for j, chip in enumerate(chips)]
        for cp in first:
            cp.start()
        passed = []
        for j, chip in enumerate(chips):
            for a in range(n):
                copy(a, 1 + j, (*chip, cc), me).wait_recv()
                fwd = copy(a, 4 + j, (*chip, cc), sibling)
                fwd.start()
                passed.append(fwd)
        for a in range(n):
            copy(a, 0, sibling, me).wait_recv()
        for j, chip in enumerate(chips):
            for a in range(n):
                copy(a, 4 + j, (*chip, 1 - cc), me).wait_recv()
        for cp in first + passed:
            cp.wait_send()
        for cp in mine:
            cp.wait()

    return pl.pallas_call(
        body, out_shape=tuple(SDS((N_DEV,) + a.shape, a.dtype) for a in xs), in_specs=[ANY] * n, out_specs=(ANY,) * n,
        scratch_shapes=[pltpu.SemaphoreType.DMA((7 * n,)), pltpu.SemaphoreType.DMA((7 * n,)),
                        pltpu.SemaphoreType.DMA((n,))],
        name=name,
    )(*xs)


def _allgather_seq(name, xs, collective_id):
    n = len(xs)
    x_refs = [jax.new_ref(a, memory_space=pltpu.MemorySpace.HBM) for a in xs]
    out_refs = [jax.empty_ref(SDS((N_DEV,) + a.shape, a.dtype), memory_space=pltpu.MemorySpace.HBM) for a in xs]

    @pl.kernel(mesh=plsc.ScalarSubcoreMesh(axis_name="seq", num_cores=1), name=name,
               scratch_types=(pltpu.SemaphoreType.DMA((7 * n,)), pltpu.SemaphoreType.DMA((7 * n,)),
                              pltpu.SemaphoreType.DMA((n,))),
               compiler_params=pltpu.CompilerParams(collective_id=collective_id))
    def launch(send_sems, recv_sems, local_sems):
        x, y, cc = lax.axis_index("x"), lax.axis_index("y"), lax.axis_index("c")
        me, sibling = (x, y, cc), (x, y, 1 - cc)
        chips = [(1 - x, y), (x, 1 - y), (1 - x, 1 - y)]
        barrier = pltpu.get_barrier_semaphore()
        for peer in [sibling] + [(*chip, cc) for chip in chips]:
            pl.semaphore_signal(barrier, inc=1, device_id=peer, device_id_type=MESH)
        pl.semaphore_wait(barrier, 4)

        def rows(a, px, py, pc):
            return out_refs[a].at[4 * px + 2 * py + pc]

        def copy(a, k, block, to, src=None):
            return pltpu.make_async_remote_copy(
                src_ref=rows(a, *block) if src is None else src, dst_ref=rows(a, *block),
                send_sem=send_sems.at[7 * a + k], recv_sem=recv_sems.at[7 * a + k], device_id=to, device_id_type=MESH)

        mine = [pltpu.make_async_copy(x_refs[a], rows(a, *me), local_sems.at[a]) for a in range(n)]
        for cp in mine:
            cp.start()
        first = []
        for a in range(n):
            first.append(copy(a, 0, me, sibling, src=x_refs[a]))
            first += [copy(a, 1 + j, me, (*chip, cc), src=x_refs[a]) for j, chip in enumerate(chips)]
        for cp in first:
            cp.start()
        passed = []
        for j, chip in enumerate(chips):
            for a in range(n):
                copy(a, 1 + j, (*chip, cc), me).wait_recv()
                fwd = copy(a, 4 + j, (*chip, cc), sibling)
                fwd.start()
                passed.append(fwd)
        for a in range(n):
            copy(a, 0, sibling, me).wait_recv()
        for j, chip in enumerate(chips):
            for a in range(n):
                copy(a, 4 + j, (*chip, 1 - cc), me).wait_recv()
        for cp in first + passed:
            cp.wait_send()
        for cp in mine:
            cp.wait()

    launch()
    return [r[...] for r in out_refs]


HBM = pl.BlockSpec(memory_space=pltpu.HBM)
SEM = pl.BlockSpec(memory_space=pltpu.SEMAPHORE)
EFFECT = pltpu.SideEffectType.DATAFLOW_SIDE_EFFECTING


def _sibling_plan(srcs, lands, send_sems, recv_sems):
    x, y, cc = lax.axis_index("x"), lax.axis_index("y"), lax.axis_index("c")
    return [pltpu.make_async_remote_copy(
        src_ref=srcs[a].at[2 * q + 1 - cc], dst_ref=lands[a].at[q], send_sem=send_sems.at[4 * a + q],
        recv_sem=recv_sems.at[4 * a + q], device_id=(x, y, 1 - cc), device_id_type=MESH)
        for a in range(len(srcs)) for q in range(4)]


def _chips_plan(srcs, lands, send_sems, recv_sems):
    x, y, cc = lax.axis_index("x"), lax.axis_index("y"), lax.axis_index("c")
    chips = [(1 - x, y), (x, 1 - y), (1 - x, 1 - y)]
    return [pltpu.make_async_remote_copy(
        src_ref=srcs[a].at[2 * px + py], dst_ref=lands[a].at[j], send_sem=send_sems.at[3 * a + j],
        recv_sem=recv_sems.at[3 * a + j], device_id=(px, py, cc), device_id_type=MESH)
        for a in range(len(srcs)) for j, (px, py) in enumerate(chips)]


def _copies_start(name, plan, per_array, srcs, land_lead):
    n = len(srcs)
    k = per_array * n

    def body(*refs):
        src_refs, land_refs = refs[:n], refs[n: 2 * n]
        send_sems, recv_sems = refs[2 * n], refs[2 * n + 1]
        token = refs[-1]
        for cp in plan(src_refs, land_refs, send_sems, recv_sems):
            cp.start()
        token[...] = jnp.zeros_like(token)

    lands = [lax.empty((land_lead,) + a.shape[1:], a.dtype) for a in srcs]
    outs = pl.pallas_call(
        body, name=name,
        out_shape=(pltpu.SemaphoreType.DMA((k,)), pltpu.SemaphoreType.DMA((k,)),
                   *[pltpu.HBM(a.shape, a.dtype) for a in srcs], *[pltpu.HBM(a.shape, a.dtype) for a in lands],
                   SDS((8, 128), F32)),
        in_specs=[HBM] * (2 * n), out_specs=(SEM, SEM, *[HBM] * (2 * n), pl.BlockSpec(memory_space=pltpu.VMEM)),
        input_output_aliases={i: 2 + i for i in range(2 * n)},
        compiler_params=pltpu.CompilerParams(has_side_effects=EFFECT),
    )(*[pltpu.with_memory_space_constraint(a, pltpu.HBM) for a in srcs],
      *[pltpu.with_memory_space_constraint(a, pltpu.HBM) for a in lands])
    return outs[0], outs[1], list(outs[2: 2 + n]), list(outs[2 + n: 2 + 2 * n]), outs[-1]


def _copies_wait(name, plan, started, after):
    send_sems, recv_sems, srcs, lands, _ = started
    n = len(srcs)
    after = tuple(after)

    def body(*refs):
        src_refs, land_refs = refs[:n], refs[n: 2 * n]
        for cp in plan(src_refs, land_refs, refs[2 * n], refs[2 * n + 1]):
            cp.wait_send()
            cp.wait_recv()

    outs = pl.pallas_call(
        body, name=name,
        out_shape=tuple(pltpu.HBM(a.shape, a.dtype) for a in srcs + lands),
        in_specs=[HBM] * (2 * n) + [SEM, SEM] + [ANY] * len(after), out_specs=(HBM,) * (2 * n),
        input_output_aliases={i: i for i in range(2 * n)},
        compiler_params=pltpu.CompilerParams(has_side_effects=EFFECT),
    )(*srcs, *lands, send_sems, recv_sems, *after)
    return list(outs[n:])


BIG = (("w_in", 1), ("w_ffn_in", 1), ("w_proj_ssm", 0), ("w_proj_gdn", 0), ("w_out", 0), ("w_ffn_down", 0))
CONVS = (("ssm_conv_w", 1), ("gdn_conv_w", 1))


def _to_dest_major(full, axis):
    a, b = full.shape
    if axis == 0:
        return full.reshape(N_DEV, a // N_DEV, b)
    s = b // N_DEV
    return jnp.stack([full[:, d * s: (d + 1) * s] for d in range(N_DEV)])


def _from_gathered(g, axis):
    if axis == 0:
        return g.reshape(-1, g.shape[2])
    return jnp.concatenate([g[d] for d in range(N_DEV)], axis=1)


IN_RUNS = ((Z_OFF, O_Z, 1024), (GZ_OFF, O_GZ, 1024), (G1_OFF, O_G1, 1024), (G2_OFF, O_G2, 1024), (QKV_OFF, O_QKV, 3072),
           (XBC_OFF, O_XBC, 1536), (SM_OFF, O_DT, 16), (SM_OFF + LANE_A, O_A, 8), (SM_OFF + LANE_B, O_B, 8))
IN_SHARD = IN_DIM // N_DEV


def _w_in_from_blocks(g):
    rows = g.shape[1]
    parts, pos = [], 0
    for off, o0, width in IN_RUNS:
        if off > pos:
            parts.append(jnp.zeros((rows, off - pos), g.dtype))
        c = o0
        while c < o0 + width:
            d = c // IN_SHARD
            hi = min(o0 + width, (d + 1) * IN_SHARD)
            parts.append(g[d][:, c - d * IN_SHARD: hi - d * IN_SHARD])
            c = hi
        pos = off + width
    parts.append(jnp.zeros((rows, PROJ_W - pos), g.dtype))
    return jnp.concatenate(parts, axis=1)


def _w_in_to_blocks(wp):
    by_orig = sorted(IN_RUNS, key=lambda r: r[1])
    blocks = []
    for d in range(N_DEV):
        lo, hi = d * IN_SHARD, (d + 1) * IN_SHARD
        parts = []
        for off, o0, width in by_orig:
            a, b = max(lo, o0), min(hi, o0 + width)
            if a < b:
                parts.append(wp[:, off + a - o0: off + b - o0])
        blocks.append(jnp.concatenate(parts, axis=1))
    return jnp.stack(blocks)


def _pad128(v, lane0):
    return jnp.zeros((1, 128), F32).at[0, lane0: lane0 + v.shape[0]].set(v)


def _layer_consts(p):
    return dict(
        dtb=_pad128(p["ssm_dt_bias"], 0), arow=_pad128(-jnp.exp(p["ssm_a_log"]), 0),
        dxrow=jnp.repeat(p["ssm_d"], SSM_P).reshape(1, 1024), snw=p["ssm_norm_w"].reshape(1, 1024),
        gb=_pad128(p["gdn_dt_bias"], LANE_A), garow=_pad128(-jnp.exp(p["gdn_a_log"]), LANE_A),
        gnw=p["gdn_norm_w"].reshape(1, 128), zb=jnp.zeros((1, GDN_QKV), F32), scb=p["ssm_conv_b"].reshape(1, SSM_CONV))


def _expand_matrix():
    row = lax.broadcasted_iota(jnp.int32, (128, 1024), 0)
    col = lax.broadcasted_iota(jnp.int32, (128, 1024), 1)
    return (col // SSM_P == row).astype(BF16)


def _silu_mul_epi(acc, up):
    g = acc
    return g, g * _sigmoid(g) * up.astype(F32)


def _merge_epi(acc, p1, g1, g2):
    return acc, _sigmoid(g1) * p1.astype(F32) + _sigmoid(g2) * acc


def _add_epi(acc, res):
    return (acc + res,)


def _ffn_bwd_epi(acc, gate, up):
    g = gate.astype(F32)
    sg = _sigmoid(g)
    return acc * up.astype(F32) * (sg * (1.0 + g * (1.0 - sg))), acc * (g * sg)


def _merge_bwd_epi(acc, g1, g2, p1, p2):
    s1, s2 = _sigmoid(g1), _sigmoid(g2)
    dg1, dg2 = acc * p1.astype(F32) * (s1 * (1.0 - s1)), acc * p2.astype(F32) * (s2 * (1.0 - s2))
    return acc * s1, acc * s2, jnp.concatenate([dg1, dg2], axis=1)


def _layer_fwd(l, x, p, rmat):
    t = x.shape[0]
    n = f"l{l}_"
    k = _layer_consts(p)
    h = _rmsnorm_fwd(n + "norm_mix", x, p["norm_mix_w"])
    proj = _matmul(n + "in_proj", "nn", [(h, 0, p["w_in"], 0)], t, PROJ_W, 1024, 1024, 1280, 1024, (F32,))
    act_g, pre_g = _conv_fwd(n + "conv_gdn", proj, QKV_OFF, p["gdn_conv_w"], k["zb"])
    act_s, pre_s = _conv_fwd(n + "conv_ssm", proj, XBC_OFF, p["ssm_conv_w"], k["scb"])
    y_ssm, ysc, st_s = _ssd_fwd(n + "ssd_fwd", act_s, proj, k["dtb"], k["arow"], k["dxrow"], k["snw"], rmat)
    y_gdn, oraw, st_g = _gdn_fwd(n + "gdn_fwd", act_g, proj, k["gb"], k["garow"], k["gnw"])
    if "late" in p:
        y_gdn, late = p["late"](y_gdn)
        p = {**p, **late}
    p1 = _matmul(n + "proj_ssm", "nn", [(y_ssm, 0, p["w_proj_ssm"], 0)], t, 1024, 1024, 1024, 1024, 1024, (BF16,))
    p2, merged = _matmul(n + "proj_gdn_merge", "nn", [(y_gdn, 0, p["w_proj_gdn"], 0)], t, 1024, 1024, 512, 1024, 1024,
                         (BF16, BF16), epi=_merge_epi, extras=[(p1, 0), (proj, G1_OFF // 1024), (proj, G2_OFF // 1024)])
    x1 = _matmul(n + "out_proj", "nn", [(merged, 0, p["w_out"], 0)], t, 1024, 1024, 1024, 1024, 1024, (F32,),
                 epi=_add_epi, extras=[(x, 0)])
    h2 = _rmsnorm_fwd(n + "norm_ffn", x1, p["norm_ffn_w"])
    up = _matmul(n + "ffn_up", "nn", [(h2, 0, p["w_ffn_in"], 2)], t, FFN, 1024, 1024, FFN // 2, 1024, (BF16,))
    gate, act = _matmul(n + "ffn_gate", "nn", [(h2, 0, p["w_ffn_in"], 0)], t, FFN, 1024, 1024, FFN // 2, 1024, (BF16, BF16),
                        epi=_silu_mul_epi, extras=[(up, 0)])
    x2 = _matmul(n + "ffn_down", "nn", [(act, 0, p["w_ffn_down"], 0)], t, 1024, FFN, 1024, 1024, FFN, (F32,),
                 epi=_add_epi, extras=[(x1, 0)])
    saved = dict(x=x, h=h, proj=proj, act_g=act_g, act_s=act_s, pre_g=pre_g, pre_s=pre_s, y_ssm=y_ssm, ysc=ysc, st_s=st_s, y_gdn=y_gdn, oraw=oraw,
                 st_g=st_g, p1=p1, p2=p2, merged=merged, x1=x1, h2=h2, up=up, gate=gate, act=act, k=k, p=p)
    return x2, saved


def _layer_bwd(l, dx2, dx2b, s, p, rmat, hooks):
    t = dx2.shape[0]
    n = f"l{l}_"
    k = s["k"]
    tk_tok = 1024
    hf = FFN // 2
    g = {}
    dgate, dup = _matmul(n + "d_ffn_act", "nt", [(dx2b, 0, p["w_ffn_down"], 0)], t, FFN, 1024, 1024, hf, 1024, (BF16, BF16),
                         epi=_ffn_bwd_epi, extras=[(s["gate"], 0), (s["up"], 0)])
    g["w_ffn_down"] = _matmul(n + "dw_ffn_down", "tn", [(s["act"], 0, dx2b, 0)], FFN, 1024, t, hf, 1024, tk_tok, (BF16,))
    dh2 = _matmul(n + "d_ffn_in", "nt", [(dgate, 0, p["w_ffn_in"], 0), (dup, 0, p["w_ffn_in"], 2)], t, 1024, FFN,
                  1024, 1024, hf, (F32,))
    dwg = _matmul(n + "dw_ffn_gate", "tn", [(s["h2"], 0, dgate, 0)], 1024, FFN, t, 1024, hf, tk_tok, (BF16,))
    dwu = _matmul(n + "dw_ffn_up", "tn", [(s["h2"], 0, dup, 0)], 1024, FFN, t, 1024, hf, tk_tok, (BF16,))
    g["w_ffn_in"] = jnp.concatenate([dwg, dwu], axis=1)
    dx1, dx1b, g["norm_ffn_w"] = _rmsnorm_bwd(n + "d_norm_ffn", s["x1"], p["norm_ffn_w"], dh2, dx2)
    dx1b = hooks.ffn_done(dx1b)
    dp1, dp2, dproj = _matmul(
        n + "d_out_proj", "nt", [(dx1b, 0, p["w_out"], 0)], t, 1024, 1024, 512, 1024, 1024,
        (BF16, BF16, (BF16, PROJ_W, 2048, G1_OFF // 2048)), epi=_merge_bwd_epi,
        extras=[(s["proj"], G1_OFF // 1024), (s["proj"], G2_OFF // 1024), (s["p1"], 0), (s["p2"], 0)])
    g["w_out"] = _matmul(n + "dw_out", "tn", [(s["merged"], 0, dx1b, 0)], 1024, 1024, t, 1024, 1024, tk_tok, (BF16,))
    g["w_proj_ssm"] = _matmul(n + "dw_proj_ssm", "tn", [(s["y_ssm"], 0, dp1, 0)], 1024, 1024, t, 1024, 1024, tk_tok, (BF16,))
    g["w_proj_gdn"] = _matmul(n + "dw_proj_gdn", "tn", [(s["y_gdn"], 0, dp2, 0)], 1024, 1024, t, 1024, 1024, tk_tok, (BF16,))
    dp1, dp2 = hooks.early_ready(l, g, dp1, dp2)
    dy_ssm = _matmul(n + "d_proj_ssm", "nt", [(dp1, 0, p["w_proj_ssm"], 0)], t, 1024, 1024, 1024, 1024, 1024, (F32,))
    dy_gdn = _matmul(n + "d_proj_gdn", "nt", [(dp2, 0, p["w_proj_gdn"], 0)], t, 1024, 1024, 1024, 1024, 1024, (F32,))
    dact_s, dproj, dsm_s, dsnw, dd, dal, ddtb = _ssd_bwd(n + "ssd_bwd", s["act_s"], s["proj"], k["dtb"], k["arow"],
                                                           k["dxrow"], k["snw"], rmat, s["ysc"], s["st_s"], dy_ssm, dproj)
    dact_g, dproj, dsm_g, dgnw, dgal, dgb = _gdn_bwd(n + "gdn_bwd", s["act_g"], s["proj"], k["gb"], k["garow"], k["gnw"],
                                                       s["oraw"], s["st_g"], dy_gdn, dproj)
    dsm_s = hooks.mixers_done(dsm_s)
    dproj = _place_small(n + "d_small", dsm_s, dsm_g, dproj)
    dproj, g["ssm_conv_w"], dcb = _conv_bwd(n + "d_conv_ssm", s["proj"], XBC_OFF, p["ssm_conv_w"], s["pre_s"], dact_s, dproj)
    dproj, g["gdn_conv_w"], _ = _conv_bwd(n + "d_conv_gdn", s["proj"], QKV_OFF, p["gdn_conv_w"], s["pre_g"], dact_g, dproj)
    g["ssm_conv_b"] = dcb.reshape(-1)
    g["ssm_norm_w"] = dsnw.reshape(-1)
    g["ssm_d"] = dd[0, :SSM_HEADS]
    g["ssm_a_log"] = dal[0, :SSM_HEADS]
    g["ssm_dt_bias"] = ddtb[0, :SSM_HEADS]
    g["gdn_norm_w"] = dgnw.reshape(-1)
    g["gdn_a_log"] = dgal[0, LANE_A: LANE_A + GDN_HEADS]
    g["gdn_dt_bias"] = dgb[0, LANE_A: LANE_A + GDN_HEADS]
    dh = _matmul(n + "d_in_proj", "nt", [(dproj, 0, p["w_in"], 0)], t, 1024, PROJ_W, 1024, 1024, 1280, (F32,))
    g["w_in"] = _matmul(n + "dw_in", "tn", [(s["h"], 0, dproj, 0)], 1024, PROJ_W, t, 1024, 1280, tk_tok, (BF16,))
    dx, dxb, g["norm_mix_w"] = _rmsnorm_bwd(n + "d_norm_mix", s["x"], p["norm_mix_w"], dh, dx1)
    g["norm_mix_w"] = g["norm_mix_w"].reshape(-1)
    g["norm_ffn_w"] = g["norm_ffn_w"].reshape(-1)
    return dx, dxb, g


def _local_step(x, tgt, layers, final_norm_w, reduce=False):
    rmat = _expand_matrix()
    saved, params = [], []
    for l in range(DEPTH):
        x, p = layers[l](x)
        x, s = _layer_fwd(l, x, p, rmat)
        saved.append(s)
        params.append(s["p"])
    loss, dx, dxb, dfw = _loss_head("loss_head", x, final_norm_w, tgt)
    grads = [None] * DEPTH
    hooks = _ReduceBesideBackward() if reduce else _NoReduce()
    for l in reversed(range(DEPTH)):
        dx, dxb, grads[l] = _layer_bwd(l, dx, dxb, saved[l], params[l], rmat, hooks)
        if reduce:
            dxb = hooks.layer_done(l, grads[l], dxb)
    if reduce:
        hooks.finish_start(dxb)
    return loss[0, 0], dx, grads, dfw.reshape(-1), hooks if reduce else None


SMALL = ("norm_mix_w", "ssm_conv_b", "ssm_dt_bias", "ssm_a_log", "ssm_d", "ssm_norm_w", "gdn_a_log", "gdn_dt_bias",
         "gdn_norm_w", "norm_ffn_w")
WEIGHTS = ("norm_mix_w", "w_in", "ssm_conv_w", "ssm_conv_b", "ssm_dt_bias", "ssm_a_log", "ssm_d", "ssm_norm_w", "gdn_conv_w",
           "gdn_a_log", "gdn_dt_bias", "gdn_norm_w", "w_proj_ssm", "w_proj_gdn", "w_out", "norm_ffn_w", "w_ffn_in",
           "w_ffn_down", "final_norm_w")


FIRST_USED = ("w_in", "ssm_conv_w", "gdn_conv_w")


def _gather_layer(l, w):
    conv_names = [nm for nm, _ in CONVS]
    groups = ([s for s in BIG + CONVS if s[0] in FIRST_USED], [s for s in BIG + CONVS if s[0] not in FIRST_USED])
    gathered = []
    for i, (specs, tag) in enumerate(zip(groups, ("first", "rest"))):
        shards = [w[nm][l] if nm in conv_names else w[nm][l].astype(BF16) for nm, _ in specs]
        gathered.append(_allgather_seq(f"l{l}_gather_{tag}", shards, collective_id=2 * l + i))
    small = {nm: w[nm][l] for nm in SMALL}

    def use(i, act):
        act, blocks = lax.optimization_barrier((act, gathered[i]))
        return act, {nm: _w_in_from_blocks(g) if nm == "w_in" else _from_gathered(g, axis)
                     for (nm, axis), g in zip(groups[i], blocks)}

    def full_weights(x):
        x, out = use(0, x)
        out.update(small)
        out["late"] = lambda y: use(1, y)
        return x, out

    return full_weights


EARLY_GRADS = ("w_ffn_down", "w_ffn_in", "w_out", "w_proj_ssm", "w_proj_gdn")


class _GradReduceScatter:
    def __init__(self, tag, specs, grads):
        self.tag = tag
        self.specs = specs
        self.blocks = [_w_in_to_blocks(grads[nm]) if nm == "w_in" else _to_dest_major(grads[nm], axis)
                       for nm, axis in specs]

    def _tied(self, started, acts):
        *acts, self.token = lax.optimization_barrier((*acts, started[4]))
        return acts

    def start(self, *acts):
        cc = lax.axis_index("c")
        self.keep = [lax.dynamic_index_in_dim(b.reshape((4, 2) + b.shape[1:]), cc, axis=1, keepdims=False)
                     for b in self.blocks]
        self.to_sibling = _copies_start(f"{self.tag}_to_sibling_start", _sibling_plan, 4, self.blocks, 4)
        return self._tied(self.to_sibling, acts)

    def mid(self, *acts):
        got = _copies_wait(f"{self.tag}_to_sibling_wait", _sibling_plan, self.to_sibling, (acts[0], self.token))
        chip_sums = [_sum_terms(f"{self.tag}_chip_sum_{nm}", [(k[None], 0), (g[None], 0)], BF16)
                     for (nm, _), k, g in zip(self.specs, self.keep, got)]
        self.to_chips = _copies_start(f"{self.tag}_between_chips_start", _chips_plan, 3, chip_sums, 3)
        return self._tied(self.to_chips, acts)

    def end(self, after):
        after = tuple(after) if isinstance(after, (tuple, list)) else (after,)
        landed = _copies_wait(f"{self.tag}_between_chips_wait", _chips_plan, self.to_chips, (*after, self.token))
        my_chip = 2 * lax.axis_index("x") + lax.axis_index("y")
        own = [lax.dynamic_index_in_dim(s, my_chip, axis=0, keepdims=True) for s in self.to_chips[2]]
        return {nm: _sum_terms(f"{self.tag}_total_{nm}", [(o, 0), (e, 0), (e, 1), (e, 2)], F32)
                for (nm, _), o, e in zip(self.specs, own, landed)}


class _NoReduce:
    def ffn_done(self, dx1b):
        return dx1b

    def early_ready(self, l, g, dp1, dp2):
        return dp1, dp2

    def mixers_done(self, dsm):
        return dsm


class _ReduceBesideBackward(_NoReduce):
    def __init__(self):
        self.late = None
        self.early = None
        self.shards = [dict() for _ in range(DEPTH)]

    def ffn_done(self, dx1b):
        if self.late is not None:
            (dx1b,) = self.late.mid(dx1b)
        return dx1b

    def early_ready(self, l, g, dp1, dp2):
        self.early = _GradReduceScatter(f"l{l}_early_grads", [s for s in BIG if s[0] in EARLY_GRADS], g)
        return self.early.start(dp1, dp2)

    def mixers_done(self, dsm):
        (dsm,) = self.early.mid(dsm)
        return dsm

    def layer_done(self, l, g, dxb):
        if self.late is not None:
            self.shards[l + 1].update(self.late.end(dxb))
        self.shards[l].update(self.early.end(dxb))
        self.late = _GradReduceScatter(f"l{l}_late_grads", [s for s in BIG + CONVS if s[0] not in EARLY_GRADS], g)
        (dxb,) = self.late.start(dxb)
        return dxb

    def finish_start(self, dxb):
        self.late.mid(dxb)

    def finish_end(self, after):
        return self.late.end(after)


def _allreduce_small(vecs):
    flat = jnp.concatenate(vecs)
    n = flat.shape[0]
    rows = -(-n // 128)
    rows = -(-rows // 8) * 8
    buf = jnp.pad(flat, (0, rows * 128 - n)).reshape(rows, 128)
    (allv,) = _allgather("gather_small_grads", [buf])
    tot = _sum_terms("small_grads_total", [(allv, d) for d in range(N_DEV)], F32).reshape(-1)
    out, o = [], 0
    for v in vecs:
        out.append(tot[o: o + v.shape[0]])
        o += v.shape[0]
    return out


def kernel(x, norm_mix_w, w_in, ssm_conv_w, ssm_conv_b, ssm_dt_bias, ssm_a_log, ssm_d, ssm_norm_w, gdn_conv_w, gdn_a_log, gdn_dt_bias, gdn_norm_w, w_proj_ssm, w_proj_gdn, w_out, norm_ffn_w, w_ffn_in, w_ffn_down, final_norm_w, loss_target, m_norm_mix_w, m_w_in, m_ssm_conv_w, m_ssm_conv_b, m_ssm_dt_bias, m_ssm_a_log, m_ssm_d, m_ssm_norm_w, m_gdn_conv_w, m_gdn_a_log, m_gdn_dt_bias, m_gdn_norm_w, m_w_proj_ssm, m_w_proj_gdn, m_w_out, m_norm_ffn_w, m_w_ffn_in, m_w_ffn_down, m_final_norm_w, v_norm_mix_w, v_w_in, v_ssm_conv_w, v_ssm_conv_b, v_ssm_dt_bias, v_ssm_a_log, v_ssm_d, v_ssm_norm_w, v_gdn_conv_w, v_gdn_a_log, v_gdn_dt_bias, v_gdn_norm_w, v_w_proj_ssm, v_w_proj_gdn, v_w_out, v_norm_ffn_w, v_w_ffn_in, v_w_ffn_down, v_final_norm_w):
    w = dict(norm_mix_w=norm_mix_w, w_in=w_in, ssm_conv_w=ssm_conv_w, ssm_conv_b=ssm_conv_b, ssm_dt_bias=ssm_dt_bias,
             ssm_a_log=ssm_a_log, ssm_d=ssm_d, ssm_norm_w=ssm_norm_w, gdn_conv_w=gdn_conv_w, gdn_a_log=gdn_a_log,
             gdn_dt_bias=gdn_dt_bias, gdn_norm_w=gdn_norm_w, w_proj_ssm=w_proj_ssm, w_proj_gdn=w_proj_gdn, w_out=w_out,
             norm_ffn_w=norm_ffn_w, w_ffn_in=w_ffn_in, w_ffn_down=w_ffn_down, final_norm_w=final_norm_w)
    m = dict(norm_mix_w=m_norm_mix_w, w_in=m_w_in, ssm_conv_w=m_ssm_conv_w, ssm_conv_b=m_ssm_conv_b, ssm_dt_bias=m_ssm_dt_bias,
             ssm_a_log=m_ssm_a_log, ssm_d=m_ssm_d, ssm_norm_w=m_ssm_norm_w, gdn_conv_w=m_gdn_conv_w, gdn_a_log=m_gdn_a_log,
             gdn_dt_bias=m_gdn_dt_bias, gdn_norm_w=m_gdn_norm_w, w_proj_ssm=m_w_proj_ssm, w_proj_gdn=m_w_proj_gdn,
             w_out=m_w_out, norm_ffn_w=m_norm_ffn_w, w_ffn_in=m_w_ffn_in, w_ffn_down=m_w_ffn_down,
             final_norm_w=m_final_norm_w)
    v = dict(norm_mix_w=v_norm_mix_w, w_in=v_w_in, ssm_conv_w=v_ssm_conv_w, ssm_conv_b=v_ssm_conv_b, ssm_dt_bias=v_ssm_dt_bias,
             ssm_a_log=v_ssm_a_log, ssm_d=v_ssm_d, ssm_norm_w=v_ssm_norm_w, gdn_conv_w=v_gdn_conv_w, gdn_a_log=v_gdn_a_log,
             gdn_dt_bias=v_gdn_dt_bias, gdn_norm_w=v_gdn_norm_w, w_proj_ssm=v_w_proj_ssm, w_proj_gdn=v_w_proj_gdn,
             w_out=v_w_out, norm_ffn_w=v_norm_ffn_w, w_ffn_in=v_w_ffn_in, w_ffn_down=v_w_ffn_down,
             final_norm_w=v_final_norm_w)

    layers = [_gather_layer(l, w) for l in range(DEPTH)]
    loss_part, dx, lgrads, dfw, reducer = _local_step(x[0], loss_target[0], layers, final_norm_w, reduce=True)
    loss = lax.psum(loss_part, ("x", "y", "c"))
    shard_grads = reducer.shards
    late = [nm for nm, _ in BIG + CONVS if nm not in EARLY_GRADS]
    grad = {nm: [shard_grads[l][nm] for l in range(DEPTH)] for nm in EARLY_GRADS}
    small_vecs = [lgrads[l][nm].reshape(-1) for l in range(DEPTH) for nm in SMALL] + [dfw]
    small_sum = _allreduce_small(small_vecs)
    for i, nm in enumerate(SMALL):
        grad[nm] = jnp.stack([small_sum[l * len(SMALL) + i].reshape(w[nm].shape[1:]) for l in range(DEPTH)])
    grad["final_norm_w"] = small_sum[-1]

    deltas, new_m, new_v = {}, {}, {}
    for nm in [n for n in WEIGHTS if n not in late]:
        grad[nm], deltas[nm], new_m[nm], new_v[nm] = _adamw("adamw_" + nm, w[nm], grad[nm], m[nm], v[nm])
    shard_grads[0].update(reducer.finish_end([deltas[nm] for nm in EARLY_GRADS]))
    for nm in late:
        grad[nm], deltas[nm], new_m[nm], new_v[nm] = _adamw("adamw_" + nm, w[nm], [shard_grads[l][nm] for l in range(DEPTH)],
                                                            m[nm], v[nm])
    return (loss, dx[None], *[grad[nm] for nm in WEIGHTS], *[deltas[nm] for nm in WEIGHTS],
            *[new_m[nm] for nm in WEIGHTS], *[new_v[nm] for nm in WEIGHTS])
```

```python
import functools

import jax
import jax.numpy as jnp
from jax import lax
from jax.experimental import pallas as pl
from jax.experimental.pallas import tpu as pltpu
from jax.experimental.pallas import tpu_sc as plsc

F32 = jnp.float32
BF16 = jnp.bfloat16
HI = lax.Precision.HIGHEST
SDS = jax.ShapeDtypeStruct

D_MODEL = 1024
DEPTH = 2
SSM_HEADS = 16
SSM_P = 64
SSM_N = 128
SSM_GROUPS = 2
SSM_CONV = 1536
GDN_HEADS = 8
GDN_DK = 128
GDN_QKV = 3072
CONV_K = 4
CHUNK = 64
SCAN_CHUNKS_PER_STEP = 4
FFN = 2816
IN_DIM = 8736
EPS = 1e-6
N_DEV = 8

Z_OFF = 0
GZ_OFF = 1024
G1_OFF = 2048
G2_OFF = 3072
QKV_OFF = 4096
XBC_OFF = 7168
SM_OFF = 8704
PROJ_W = 8960
LANE_A = 16
LANE_B = 24
O_Z, O_XBC, O_DT, O_QKV, O_GZ, O_A, O_B, O_G1, O_G2 = 0, 1024, 2560, 2576, 5648, 6672, 6680, 6688, 7712

ADAM_LR = 0.001
ADAM_B1 = 0.9
ADAM_B2 = 0.999
ADAM_EPS = 1e-08
ADAM_WD = 0.01
ADAM_STEP = 10

V7X_VMEM_LIMIT = 48 * 1024 * 1024

NN = ((1,), (0,))
NT = ((1,), (1,))
TN = ((0,), (0,))


def _bdot(a, b, dims):
    return lax.dot_general(a.astype(BF16), b.astype(BF16), (dims, ((), ())), preferred_element_type=F32)


def _hdot(a, b, dims=NN):
    return lax.dot_general(a, b, (dims, ((), ())), precision=HI, preferred_element_type=F32)


def _sigmoid(x):
    return 1.0 / (1.0 + jnp.exp(-x))


def _softplus(x):
    return jnp.maximum(x, 0.0) + jnp.log(1.0 + jnp.exp(-jnp.abs(x)))


def _params(dims):
    return pltpu.CompilerParams(dimension_semantics=dims, vmem_limit_bytes=V7X_VMEM_LIMIT)


def _rowsum(x):
    return jnp.sum(x, axis=-1, keepdims=True)


def _colsum(x):
    return jnp.sum(x, axis=0, keepdims=True)


def _matmul(name, mode, pairs, m, n, kdim, tm, tn, tk, out_dtypes, epi=None, extras=()):
    tm, tn, tk = min(tm, m), min(tn, n), min(tk, kdim)
    nk = kdim // tk
    assert m % tm == 0 and n % tn == 0 and kdim % tk == 0, (name, m, n, kdim, tm, tn, tk)
    in_specs, args = [], []
    for a, a_off, b, b_off in pairs:
        if mode == "nn":
            in_specs.append(pl.BlockSpec((tm, tk), lambda i, j, k, o=a_off: (i, k + o)))
            in_specs.append(pl.BlockSpec((tk, tn), lambda i, j, k, o=b_off: (k, j + o)))
            dims = NN
        elif mode == "nt":
            in_specs.append(pl.BlockSpec((tm, tk), lambda i, j, k, o=a_off: (i, k + o)))
            in_specs.append(pl.BlockSpec((tn, tk), lambda i, j, k, o=b_off: (j, k + o)))
            dims = NT
        else:
            in_specs.append(pl.BlockSpec((tk, tm), lambda i, j, k, o=a_off: (k, i + o)))
            in_specs.append(pl.BlockSpec((tk, tn), lambda i, j, k, o=b_off: (k, j + o)))
            dims = TN
        args += [a, b]
    for e, e_off in extras:
        in_specs.append(pl.BlockSpec((tm, tn), lambda i, j, k, o=e_off: (i, j + o)))
        args.append(e)
    npair, nex, nout = len(pairs), len(extras), len(out_dtypes)

    def body(*refs):
        prefs = refs[: 2 * npair]
        erefs = refs[2 * npair: 2 * npair + nex]
        orefs = refs[2 * npair + nex: 2 * npair + nex + nout]

        def finish(res):
            outs = (res,) if epi is None else epi(res, *[e[...] for e in erefs])
            for o, r in zip(orefs, outs):
                o[...] = r.astype(o.dtype)

        s = _bdot(prefs[0][...], prefs[1][...], dims)
        for p in range(1, npair):
            s = s + _bdot(prefs[2 * p][...], prefs[2 * p + 1][...], dims)
        if nk == 1:
            finish(s)
            return
        acc = refs[-1]
        k = pl.program_id(2)

        @pl.when(k == 0)
        def _():
            acc[...] = s

        @pl.when(k > 0)
        def _():
            acc[...] += s

        @pl.when(k == nk - 1)
        def _():
            finish(acc[...])

    out_shape, out_specs = [], []
    for od in out_dtypes:
        if isinstance(od, tuple):
            dt, full_w, blk_w, cblk = od
            assert n == tn
            out_shape.append(SDS((m, full_w), dt))
            out_specs.append(pl.BlockSpec((tm, blk_w), lambda i, j, k, c=cblk: (i, c)))
        else:
            out_shape.append(SDS((m, n), od))
            out_specs.append(pl.BlockSpec((tm, tn), lambda i, j, k: (i, j)))
    out_shape, out_specs = tuple(out_shape), tuple(out_specs)
    res = pl.pallas_call(
        body, grid=(m // tm, n // tn, nk), in_specs=in_specs, out_specs=out_specs, out_shape=out_shape,
        scratch_shapes=[pltpu.VMEM((tm, tn), F32)] if nk > 1 else [], name=name,
        compiler_params=_params(("parallel", "parallel", "arbitrary")),
    )(*args)
    return res if nout > 1 else res[0]


def _rmsnorm_fwd(name, x, w):
    t, d = x.shape
    tm = min(512, t)

    def body(x_ref, w_ref, h_ref):
        xv = x_ref[...]
        r = lax.rsqrt(jnp.mean(xv * xv, axis=-1, keepdims=True) + EPS)
        h_ref[...] = (xv * r * w_ref[...]).astype(BF16)

    return pl.pallas_call(
        body, grid=(t // tm,),
        in_specs=[pl.BlockSpec((tm, d), lambda i: (i, 0)), pl.BlockSpec((1, d), lambda i: (0, 0))],
        out_specs=pl.BlockSpec((tm, d), lambda i: (i, 0)), out_shape=SDS((t, d), BF16), name=name,
        compiler_params=_params(("parallel",)),
    )(x, w.reshape(1, d))


def _rmsnorm_bwd(name, x, w, dh, dres):
    t, d = x.shape
    tm = min(512, t)

    def body(x_ref, w_ref, dh_ref, dres_ref, dx_ref, dxb_ref, dw_ref):
        xv = x_ref[...]
        r = lax.rsqrt(jnp.mean(xv * xv, axis=-1, keepdims=True) + EPS)
        xh = xv * r
        dhv = dh_ref[...].astype(F32)
        dxh = dhv * w_ref[...]
        dx = r * (dxh - xh * jnp.mean(dxh * xh, axis=-1, keepdims=True)) + dres_ref[...]
        dx_ref[...] = dx
        dxb_ref[...] = dx.astype(BF16)

        @pl.when(pl.program_id(0) == 0)
        def _():
            dw_ref[...] = jnp.zeros_like(dw_ref)

        dw_ref[...] += _colsum(dhv * xh)

    row = pl.BlockSpec((tm, d), lambda i: (i, 0))
    vec = pl.BlockSpec((1, d), lambda i: (0, 0))
    return pl.pallas_call(
        body, grid=(t // tm,), in_specs=[row, vec, row, row], out_specs=(row, row, vec),
        out_shape=(SDS((t, d), F32), SDS((t, d), BF16), SDS((1, d), F32)), name=name,
        compiler_params=_params(("arbitrary",)),
    )(x, w.reshape(1, d), dh, dres)


def _loss_head(name, x, w, tgt):
    t, d = x.shape
    tm = min(512, t)

    def body(x_ref, w_ref, t_ref, loss_ref, dx_ref, dxb_ref, dw_ref):
        xv = x_ref[...]
        wv = w_ref[...]
        r = lax.rsqrt(jnp.mean(xv * xv, axis=-1, keepdims=True) + EPS)
        xh = xv * r
        e = xh * wv - t_ref[...]
        dy = e * (1.0 / d)
        dxh = dy * wv
        dx = r * (dxh - xh * jnp.mean(dxh * xh, axis=-1, keepdims=True))
        dx_ref[...] = dx
        dxb_ref[...] = dx.astype(BF16)

        @pl.when(pl.program_id(0) == 0)
        def _():
            dw_ref[...] = jnp.zeros_like(dw_ref)
            loss_ref[...] = jnp.zeros_like(loss_ref)

        dw_ref[...] += _colsum(dy * xh)
        loss_ref[...] += 0.5 * jnp.sum(jnp.mean(e * e, axis=-1, keepdims=True), axis=0, keepdims=True)

    row = pl.BlockSpec((tm, d), lambda i: (i, 0))
    vec = pl.BlockSpec((1, d), lambda i: (0, 0))
    return pl.pallas_call(
        body, grid=(t // tm,), in_specs=[row, vec, row],
        out_specs=(pl.BlockSpec((1, 1), lambda i: (0, 0)), row, row, vec),
        out_shape=(SDS((1, 1), F32), SDS((t, d), F32), SDS((t, d), BF16), SDS((1, d), F32)), name=name,
        compiler_params=_params(("arbitrary",)),
    )(x, w.reshape(1, d), tgt)


def _shift_down(u, s, row):
    return jnp.where(row >= s, pltpu.roll(u, shift=s, axis=0), 0.0)


def _conv_fwd(name, src, col0, w, b):
    t = src.shape[0]
    c = w.shape[1]
    tc = 256
    assert c % tc == 0 and col0 % tc == 0

    def body(u_ref, w_ref, b_ref, o_ref, pre_ref):
        u = u_ref[...]
        wv = w_ref[...]
        row = lax.broadcasted_iota(jnp.int32, u.shape, 0)
        pre = b_ref[...] + wv[3:4, :] * u
        for s in range(1, CONV_K):
            pre = pre + wv[3 - s: 4 - s, :] * _shift_down(u, s, row)
        pre_ref[...] = pre
        o_ref[...] = pre * _sigmoid(pre)

    col = pl.BlockSpec((t, tc), lambda j: (0, j))
    return pl.pallas_call(
        body, grid=(c // tc,),
        in_specs=[pl.BlockSpec((t, tc), lambda j: (0, j + col0 // tc)), pl.BlockSpec((CONV_K, tc), lambda j: (0, j)),
                  pl.BlockSpec((1, tc), lambda j: (0, j))],
        out_specs=(col, col), out_shape=(SDS((t, c), F32), SDS((t, c), F32)), name=name,
        compiler_params=_params(("parallel",)),
    )(src, w, b)


def _place_small(name, dsm_a, dsm_b, dproj):
    t = dsm_a.shape[0]
    width = PROJ_W - SM_OFF
    tr = min(512, t)

    def body(a_ref, b_ref, dproj_ref, o_ref):
        o_ref[:, :128] = a_ref[...] + b_ref[...]
        o_ref[:, 128:] = jnp.zeros((tr, width - 128), BF16)

    row = pl.BlockSpec((tr, 128), lambda i: (i, 0))
    return pl.pallas_call(
        body, grid=(t // tr,), in_specs=[row, row, ANY],
        out_specs=pl.BlockSpec((tr, width), lambda i: (i, SM_OFF // width)), out_shape=SDS(dproj.shape, BF16),
        input_output_aliases={2: 0}, name=name, compiler_params=_params(("parallel",)),
    )(dsm_a, dsm_b, dproj)


def _conv_bwd(name, src, col0, w, pre, dact, dproj):
    t = src.shape[0]
    c = w.shape[1]
    tc = 128

    def body(u_ref, w_ref, pre_ref, da_ref, dproj_ref, du_ref, dw_ref, db_ref):
        u = u_ref[...]
        wv = w_ref[...]
        prev = pre_ref[...]
        row = lax.broadcasted_iota(jnp.int32, u.shape, 0)
        sg = _sigmoid(prev)
        dpre = da_ref[...] * (sg * (1.0 + prev * (1.0 - sg)))
        du = wv[3:4, :] * dpre
        dw_ref[3:4, :] = _colsum(dpre * u)
        for s in range(1, CONV_K):
            up = jnp.where(row < t - s, pltpu.roll(dpre, shift=t - s, axis=0), 0.0)
            du = du + wv[3 - s: 4 - s, :] * up
            dw_ref[3 - s: 4 - s, :] = _colsum(up * u)
        du_ref[...] = du.astype(BF16)
        db_ref[...] = _colsum(dpre)

    col = pl.BlockSpec((t, tc), lambda j: (0, j))
    return pl.pallas_call(
        body, grid=(c // tc,),
        in_specs=[pl.BlockSpec((t, tc), lambda j: (0, j + col0 // tc)), pl.BlockSpec((CONV_K, tc), lambda j: (0, j)),
                  col, col, ANY],
        out_specs=(pl.BlockSpec((t, tc), lambda j: (0, j + col0 // tc)), pl.BlockSpec((CONV_K, tc), lambda j: (0, j)),
                   pl.BlockSpec((1, tc), lambda j: (0, j))),
        out_shape=(SDS(dproj.shape, BF16), SDS((CONV_K, c), F32), SDS((1, c), F32)), name=name,
        input_output_aliases={4: 0},
        compiler_params=_params(("parallel",)),
    )(src, w, pre, dact, dproj)


def _tri(q):
    ii = lax.broadcasted_iota(jnp.int32, (q, q), 0)
    jj = lax.broadcasted_iota(jnp.int32, (q, q), 1)
    return ii, jj


def _dot01(x, r01, dims, terms=3):
    out, rem = None, x
    for i in range(terms):
        hi = rem.astype(BF16)
        d = lax.dot_general(hi, r01, (dims, ((), ())), preferred_element_type=F32)
        out = d if out is None else out + d
        if i + 1 < terms:
            rem = rem - hi.astype(F32)
    return out


def _ssd_common(act, sm, dtb, arow, rmat):
    q = CHUNK
    ii, jj = _tri(q)
    lane = lax.broadcasted_iota(jnp.int32, (q, 128), 1)
    m16 = lane < SSM_HEADS
    dt = jnp.where(m16, _softplus(sm + dtb), 0.0)
    a = dt * arow
    tril = (ii >= jj).astype(F32)
    triu = (ii <= jj).astype(F32)
    acum = _hdot(tril, a)
    acum_r = _hdot(a.T, triu)
    dtx = _dot01(dt, rmat, NN)
    acx = _dot01(acum, rmat, NN)
    ex = jnp.exp(acx)
    alx = acx[q - 1: q, :]
    dex = jnp.exp(alx - acx)
    xs = act[:, :1024]
    return dict(ii=ii, jj=jj, m16=m16, dt=dt, a=a, triu=triu, acum=acum, acum_r=acum_r, dtx=dtx, ex=ex, dex=dex,
                elx=jnp.exp(alx), xs=xs, x=xs * dtx)


def _ssd_lmat(cm, h):
    return jnp.where(cm["ii"] >= cm["jj"], jnp.exp(cm["acum"][:, h: h + 1] - cm["acum_r"][h: h + 1, :]), 0.0)


def _ssd_fwd(name, act, proj, dtb, arow, dxrow, nw, rmat):
    t = act.shape[0]
    q = CHUNK
    nc = t // q
    hg = SSM_HEADS // SSM_GROUPS
    gw = hg * SSM_P

    def body(act_ref, z_ref, sm_ref, dtb_ref, arow_ref, dx_ref, nw_ref, r_ref, y_ref, ys_ref, st_ref, s_scr, yd_scr):
        @pl.when(pl.program_id(0) == 0)
        def _():
            s_scr[...] = jnp.zeros_like(s_scr)

        s_all = s_scr[...]
        for sub in range(cps):
            rows = pl.ds(sub * q, q)
            s_all = chunk(act_ref.at[rows, :], z_ref.at[rows, :], sm_ref.at[rows, :], dtb_ref, arow_ref, dx_ref, nw_ref, r_ref,
                          y_ref.at[rows, :], ys_ref.at[rows, :], st_ref.at[sub], yd_scr.at[rows, :], s_all)
        s_scr[...] = s_all

    def chunk(act_ref, z_ref, sm_ref, dtb_ref, arow_ref, dx_ref, nw_ref, r_ref, y_ref, ys_ref, st_ref, yd_scr, s_all):
        st_ref[...] = s_all
        actv = act_ref[...]
        cm = _ssd_common(actv, sm_ref[...], dtb_ref[...], arow_ref[...], r_ref[...])
        x = cm["x"]
        xd = x * cm["dex"]
        yoffs, snew = [], []
        for g in range(SSM_GROUPS):
            bg = actv[:, 1024 + g * SSM_N: 1024 + (g + 1) * SSM_N]
            cg = actv[:, 1280 + g * SSM_N: 1280 + (g + 1) * SSM_N]
            sg = s_all[:, g * gw: (g + 1) * gw]
            cb = _bdot(cg, bg, NT)
            yoffs.append(_bdot(cg, sg, NN))
            snew.append(_bdot(bg, xd[:, g * gw: (g + 1) * gw], TN))
            for r in range(hg):
                h = g * hg + r
                mm = cb * _ssd_lmat(cm, h)
                yd_scr[:, h * SSM_P: (h + 1) * SSM_P] = _bdot(mm, x[:, h * SSM_P: (h + 1) * SSM_P], NN)
        s_next = s_all * cm["elx"] + jnp.concatenate(snew, axis=1)
        ysc = yd_scr[...] + jnp.concatenate(yoffs, axis=1) * cm["ex"]
        ys_ref[...] = ysc
        zv = z_ref[...]
        yg = (ysc + dx_ref[...] * cm["xs"]) * (zv * _sigmoid(zv))
        nwv = nw_ref[...]
        for g in range(SSM_GROUPS):
            sl = yg[:, g * gw: (g + 1) * gw]
            rr = lax.rsqrt(jnp.mean(sl * sl, axis=-1, keepdims=True) + EPS)
            y_ref[:, g * gw: (g + 1) * gw] = (sl * rr * nwv[:, g * gw: (g + 1) * gw]).astype(BF16)
        return s_next

    cps = SCAN_CHUNKS_PER_STEP if nc % SCAN_CHUNKS_PER_STEP == 0 else 1
    qq = cps * q
    vec128 = pl.BlockSpec((1, 128), lambda c: (0, 0))
    vec1k = pl.BlockSpec((1, 1024), lambda c: (0, 0))
    return pl.pallas_call(
        body, grid=(nc // cps,),
        in_specs=[pl.BlockSpec((qq, SSM_CONV), lambda c: (c, 0)), pl.BlockSpec((qq, 1024), lambda c: (c, Z_OFF // 1024)),
                  pl.BlockSpec((qq, 128), lambda c: (c, SM_OFF // 128)), vec128, vec128, vec1k, vec1k,
                  pl.BlockSpec((128, 1024), lambda c: (0, 0))],
        out_specs=(pl.BlockSpec((qq, 1024), lambda c: (c, 0)), pl.BlockSpec((qq, 1024), lambda c: (c, 0)),
                   pl.BlockSpec((cps, 128, 1024), lambda c: (c, 0, 0))),
        out_shape=(SDS((t, 1024), BF16), SDS((t, 1024), F32), SDS((nc, 128, 1024), F32)),
        scratch_shapes=[pltpu.VMEM((128, 1024), F32), pltpu.VMEM((qq, 1024), F32)], name=name,
        compiler_params=_params(("arbitrary",)),
    )(act, proj, proj, dtb, arow, dxrow, nw, rmat)


def _ssd_bwd(name, act, proj, dtb, arow, dxrow, nw, rmat, ysc, states, dy, dproj):
    t = act.shape[0]
    q = CHUNK
    nc = t // q
    hg = SSM_HEADS // SSM_GROUPS
    gw = hg * SSM_P

    def body(act_ref, z_ref, sm_ref, dtb_ref, arow_ref, dx_ref, nw_ref, r_ref, ys_ref, st_ref, dy_ref, dproj_ref,
             dact_ref, dz_ref, dsm_ref, dnw_ref, dd_ref, dal_ref, ddtb_ref, ds_scr, dxd_scr):
        @pl.when(pl.program_id(0) == 0)
        def _():
            ds_scr[...] = jnp.zeros_like(ds_scr)
            dnw_ref[...] = jnp.zeros_like(dnw_ref)
            dd_ref[...] = jnp.zeros_like(dd_ref)
            dal_ref[...] = jnp.zeros_like(dal_ref)
            ddtb_ref[...] = jnp.zeros_like(ddtb_ref)

        dsn = ds_scr[...]
        for sub in reversed(range(cps)):
            rows = pl.ds(sub * q, q)
            dsn = chunk(act_ref.at[rows, :], z_ref.at[rows, :], sm_ref.at[rows, :], dtb_ref, arow_ref, dx_ref, nw_ref, r_ref,
                        ys_ref.at[rows, :], st_ref.at[sub], dy_ref.at[rows, :], dact_ref.at[rows, :], dz_ref.at[rows, :],
                        dsm_ref.at[rows, :], dnw_ref, dd_ref, dal_ref, ddtb_ref, dxd_scr.at[rows, :], dsn)
        ds_scr[...] = dsn

    def chunk(act_ref, z_ref, sm_ref, dtb_ref, arow_ref, dx_ref, nw_ref, r_ref, ys_ref, st_ref, dy_ref,
              dact_ref, dz_ref, dsm_ref, dnw_ref, dd_ref, dal_ref, ddtb_ref, dxd_scr, dsn):
        actv = act_ref[...]
        smv = sm_ref[...]
        rmat_v = r_ref[...]
        cm = _ssd_common(actv, smv, dtb_ref[...], arow_ref[...], rmat_v)
        ii, jj = cm["ii"], cm["jj"]
        x, xs = cm["x"], cm["xs"]
        s_all = st_ref[...]
        ysv = ys_ref[...]
        dxr = dx_ref[...]
        y = ysv + dxr * xs
        zv = z_ref[...]
        sz = _sigmoid(zv)
        silz = zv * sz
        yg = y * silz
        dout = dy_ref[...]
        nwv = nw_ref[...]
        dyn = dout * nwv
        yn_parts, dyg_parts = [], []
        for g in range(SSM_GROUPS):
            sl = yg[:, g * gw: (g + 1) * gw]
            rr = lax.rsqrt(jnp.mean(sl * sl, axis=-1, keepdims=True) + EPS)
            yn = sl * rr
            dn = dyn[:, g * gw: (g + 1) * gw]
            yn_parts.append(yn)
            dyg_parts.append(rr * (dn - yn * jnp.mean(dn * yn, axis=-1, keepdims=True)))
        dnw_ref[...] += _colsum(dout * jnp.concatenate(yn_parts, axis=1))
        dyg = jnp.concatenate(dyg_parts, axis=1)
        dyv = dyg * silz
        dz_ref[...] = (dyg * y * (sz * (1.0 + zv * (1.0 - sz)))).astype(BF16)
        dd_ref[...] += _dot01(_colsum(dyv * xs), rmat_v, NT)
        dxs = dyv * dxr
        dcs = dyv * cm["ex"]
        xd = x * cm["dex"]
        dxst_parts, ds_parts, db_parts, dc_parts, yoff_parts, wcol_rows = [], [], [], [], [], []
        lane128 = lax.broadcasted_iota(jnp.int32, (q, 128), 1)
        wrow = jnp.zeros((q, 128), F32)
        for g in range(SSM_GROUPS):
            bg = actv[:, 1024 + g * SSM_N: 1024 + (g + 1) * SSM_N]
            cg = actv[:, 1280 + g * SSM_N: 1280 + (g + 1) * SSM_N]
            sg = s_all[:, g * gw: (g + 1) * gw]
            dsng = dsn[:, g * gw: (g + 1) * gw]
            dcsg = dcs[:, g * gw: (g + 1) * gw]
            dcg = _bdot(dcsg, sg, NT)
            yoff_parts.append(_bdot(cg, sg, NN))
            ds_parts.append(_bdot(cg, dcsg, TN))
            dxst_parts.append(_bdot(bg, dsng, NN))
            dbg = _bdot(xd[:, g * gw: (g + 1) * gw], dsng, NT)
            cb = _bdot(cg, bg, NT)
            dcb = jnp.zeros((q, q), F32)
            for r in range(hg):
                h = g * hg + r
                lm = _ssd_lmat(cm, h)
                mm = cb * lm
                dyh = dyv[:, h * SSM_P: (h + 1) * SSM_P]
                dm = jnp.where(ii >= jj, _bdot(dyh, x[:, h * SSM_P: (h + 1) * SSM_P], NT), 0.0)
                dxd_scr[:, h * SSM_P: (h + 1) * SSM_P] = _bdot(mm, dyh, TN)
                dcb = dcb + dm * lm
                wm = dm * mm
                wrow = wrow + jnp.where(lane128 == h, _rowsum(wm), 0.0)
                wcol_rows.append(_colsum(wm))
            dc_parts.append(dcg + _bdot(dcb, bg, NN))
            db_parts.append(dbg + _bdot(dcb, cg, TN))
        dxst = jnp.concatenate(dxst_parts, axis=1) * cm["dex"]
        dx = dxd_scr[...] + dxst
        ds_prev = jnp.concatenate(ds_parts, axis=1) + dsn * cm["elx"]
        wcol = jnp.concatenate(wcol_rows + [jnp.zeros((128 - SSM_HEADS, q), F32)], axis=0).T
        yoff = jnp.concatenate(yoff_parts, axis=1) * cm["ex"]
        xdxst = x * dxst
        dac = wrow - wcol + _dot01(dyv * yoff - xdxst, rmat_v, NT)
        last = _dot01(_colsum(dsn * s_all) * cm["elx"] + _colsum(xdxst), rmat_v, NT)
        rowq = lax.broadcasted_iota(jnp.int32, (q, 128), 0)
        dac = dac + jnp.where(rowq == q - 1, last, 0.0)
        da = _hdot(cm["triu"], dac)
        arow_v = arow_ref[...]
        ddt = da * arow_v + _dot01(dx * xs, rmat_v, NT)
        dxs = dxs + dx * cm["dtx"]
        dal_ref[...] += _colsum(da * cm["a"])
        ddtraw = jnp.where(cm["m16"], ddt * _sigmoid(smv + dtb_ref[...]), 0.0)
        ddtb_ref[...] += _colsum(ddtraw)
        dsm_ref[...] = ddtraw.astype(BF16)
        dact_ref[:, :1024] = dxs
        for g in range(SSM_GROUPS):
            dact_ref[:, 1024 + g * SSM_N: 1024 + (g + 1) * SSM_N] = db_parts[g]
            dact_ref[:, 1280 + g * SSM_N: 1280 + (g + 1) * SSM_N] = dc_parts[g]
        return ds_prev

    cps = SCAN_CHUNKS_PER_STEP if nc % SCAN_CHUNKS_PER_STEP == 0 else 1
    qq = cps * q
    rev = lambda c: nc // cps - 1 - c
    vec128 = pl.BlockSpec((1, 128), lambda c: (0, 0))
    vec1k = pl.BlockSpec((1, 1024), lambda c: (0, 0))
    return pl.pallas_call(
        body, grid=(nc // cps,),
        in_specs=[pl.BlockSpec((qq, SSM_CONV), lambda c: (rev(c), 0)),
                  pl.BlockSpec((qq, 1024), lambda c: (rev(c), Z_OFF // 1024)),
                  pl.BlockSpec((qq, 128), lambda c: (rev(c), SM_OFF // 128)), vec128, vec128, vec1k, vec1k,
                  pl.BlockSpec((128, 1024), lambda c: (0, 0)),
                  pl.BlockSpec((qq, 1024), lambda c: (rev(c), 0)), pl.BlockSpec((cps, 128, 1024), lambda c: (rev(c), 0, 0)),
                  pl.BlockSpec((qq, 1024), lambda c: (rev(c), 0)), ANY],
        out_specs=(pl.BlockSpec((qq, SSM_CONV), lambda c: (rev(c), 0)),
                   pl.BlockSpec((qq, 1024), lambda c: (rev(c), Z_OFF // 1024)),
                   pl.BlockSpec((qq, 128), lambda c: (rev(c), 0)), vec1k, vec128, vec128, vec128),
        out_shape=(SDS((t, SSM_CONV), F32), SDS(dproj.shape, BF16), SDS((t, 128), BF16), SDS((1, 1024), F32),
                   SDS((1, 128), F32), SDS((1, 128), F32), SDS((1, 128), F32)),
        input_output_aliases={11: 1},
        scratch_shapes=[pltpu.VMEM((128, 1024), F32), pltpu.VMEM((qq, 1024), F32)], name=name,
        compiler_params=_params(("arbitrary",)),
    )(act, proj, proj, dtb, arow, dxrow, nw, rmat, ysc, states, dy, dproj)


def _split(a):
    hi = a.astype(BF16)
    return hi, (a - hi.astype(F32)).astype(BF16)


def _dot3(a, b, dims=NN):
    (ah, al), (bh, bl) = a, b

    def d(x, y):
        return lax.dot_general(x, y, (dims, ((), ())), preferred_element_type=F32)

    return d(ah, bh) + (d(ah, bl) + d(al, bh))


def _tri_inverses(amats, ii, jj):
    eye = jnp.where(ii == jj, 1.0, 0.0)
    tms = [eye - a for a in amats]
    sp = [_split(a) for a in amats]
    for _ in range(5):
        sp = [_split(_dot3(s, s)) for s in sp]
        tms = [t + _dot3(_split(t), s) for t, s in zip(tms, sp)]
    return tms


def _gdn_common(sm, gb, garow):
    q = CHUNK
    ii, jj = _tri(q)
    lane = lax.broadcasted_iota(jnp.int32, (q, 128), 1)
    ma = (lane >= LANE_A) & (lane < LANE_A + GDN_HEADS)
    spre = sm + gb
    g = jnp.where(ma, garow * _softplus(spre), 0.0)
    beta = _sigmoid(sm)
    tril = (ii >= jj).astype(F32)
    triu = (ii <= jj).astype(F32)
    gc = _hdot(tril, g)
    gc_r = _hdot(g.T, triu)
    return dict(ii=ii, jj=jj, lane=lane, ma=ma, spre=spre, g=g, beta=beta, triu=triu, gc=gc, gc_r=gc_r)


def _each(f, *lists):
    return [f(*xs) for xs in zip(*lists)]


GDN_SCALE = GDN_DK ** -0.5


def _gdn_heads(cm, actv, states):
    q = CHUNK
    ii, jj = cm["ii"], cm["jj"]
    heads = range(GDN_HEADS)
    qr = [actv[:, h * 128: (h + 1) * 128] for h in heads]
    kr = [actv[:, 1024 + h * 128: 1024 + (h + 1) * 128] for h in heads]
    v = [actv[:, 2048 + h * 128: 2048 + (h + 1) * 128] for h in heads]
    rq = _each(lambda x: lax.rsqrt(_rowsum(x * x) + EPS), qr)
    rk = _each(lambda x: lax.rsqrt(_rowsum(x * x) + EPS), kr)
    qn = _each(lambda x, r: x * r * GDN_SCALE, qr, rq)
    kn = _each(lambda x, r: x * r, kr, rk)
    gcc = [cm["gc"][:, LANE_A + h: LANE_A + h + 1] for h in heads]
    gcr = [cm["gc_r"][LANE_A + h: LANE_A + h + 1, :] for h in heads]
    bcol = [cm["beta"][:, LANE_B + h: LANE_B + h + 1] for h in heads]
    dm = _each(lambda c, r: jnp.where(ii >= jj, jnp.exp(c - r), 0.0), gcc, gcr)
    kq = _each(lambda k, a: _bdot(jnp.concatenate([k, a], axis=0), k, NT), kn, qn)
    ak = _each(lambda x, d: jnp.where(ii > jj, x[:q] * d, 0.0), kq, dm)
    qkm = _each(lambda x, d: jnp.where(ii >= jj, x[q:] * d, 0.0), kq, dm)
    tm = _tri_inverses(_each(lambda a, b: a * b, ak, bcol), ii, jj)
    eg = _each(jnp.exp, gcc)
    gl = [c[q - 1: q, :] for c in gcc]
    rm = _each(lambda vv, k, b, e: jnp.concatenate([vv * b, k * (b * e)], axis=1), v, kn, bcol, eg)
    tt = _each(lambda t, r: _dot3(_split(t), _split(r)), tm, rm)
    w = [t[:, 128:] for t in tt]
    qg = _each(lambda a, e: a * e, qn, eg)
    ws = _each(lambda ww, a, s: _bdot(jnp.concatenate([ww, a], axis=0), s, NN), w, qg, states)
    vnew = _each(lambda t, x: t[:, :128] - x[:q], tt, ws)
    return dict(qr=qr, v=v, rq=rq, rk=rk, qn=qn, kn=kn, gcc=gcc, bcol=bcol, dm=dm, ak=ak, tm=tm, eg=eg, gl=gl,
                egl=_each(jnp.exp, gl), ed=_each(lambda g, c: jnp.exp(g - c), gl, gcc), tt=tt, w=w, vnew=vnew, qkm=qkm,
                qg=qg, qgs=[x[q:] for x in ws])


def _gdn_fwd(name, act, proj, gb, garow, gnw):
    t = act.shape[0]
    q = CHUNK
    nc = t // q

    def body(act_ref, gz_ref, sm_ref, gb_ref, ga_ref, nw_ref, y_ref, o_ref, st_ref, s_scr):
        @pl.when(pl.program_id(0) == 0)
        def _():
            s_scr[...] = jnp.zeros_like(s_scr)

        states = [s_scr[h * 128: (h + 1) * 128, :] for h in range(GDN_HEADS)]
        for sub in range(cps):
            rows = pl.ds(sub * q, q)
            states = chunk(act_ref.at[rows, :], gz_ref.at[rows, :], sm_ref.at[rows, :], gb_ref, ga_ref, nw_ref,
                           y_ref.at[rows, :], o_ref.at[rows, :], st_ref.at[sub], states)
        for h in range(GDN_HEADS):
            s_scr[h * 128: (h + 1) * 128, :] = states[h]

    def chunk(act_ref, gz_ref, sm_ref, gb_ref, ga_ref, nw_ref, y_ref, o_ref, st_ref, states):
        for h in range(GDN_HEADS):
            st_ref[h * 128: (h + 1) * 128, :] = states[h]
        actv = act_ref[...]
        cm = _gdn_common(sm_ref[...], gb_ref[...], ga_ref[...])
        nwv = nw_ref[...]
        gzv = gz_ref[...]
        hd = _gdn_heads(cm, actv, states)
        outs = _each(lambda qs, m, vn: qs + _bdot(m, vn, NN), hd["qgs"], hd["qkm"], hd["vnew"])
        snew = _each(lambda s, e, k, d, vn: s * e + _bdot(k * d, vn, TN), states, hd["egl"], hd["kn"], hd["ed"], hd["vnew"])
        for h in range(GDN_HEADS):
            o = outs[h]
            o_ref[:, h * 128: (h + 1) * 128] = o
            rr = lax.rsqrt(jnp.mean(o * o, axis=-1, keepdims=True) + EPS)
            gz = gzv[:, h * 128: (h + 1) * 128]
            y_ref[:, h * 128: (h + 1) * 128] = (o * rr * nwv * (gz * _sigmoid(gz))).astype(BF16)
        return snew

    cps = SCAN_CHUNKS_PER_STEP if nc % SCAN_CHUNKS_PER_STEP == 0 else 1
    qq = cps * q
    vec128 = pl.BlockSpec((1, 128), lambda c: (0, 0))
    return pl.pallas_call(
        body, grid=(nc // cps,),
        in_specs=[pl.BlockSpec((qq, GDN_QKV), lambda c: (c, 0)), pl.BlockSpec((qq, 1024), lambda c: (c, GZ_OFF // 1024)),
                  pl.BlockSpec((qq, 128), lambda c: (c, SM_OFF // 128)), vec128, vec128, vec128],
        out_specs=(pl.BlockSpec((qq, 1024), lambda c: (c, 0)), pl.BlockSpec((qq, 1024), lambda c: (c, 0)),
                   pl.BlockSpec((cps, 1024, 128), lambda c: (c, 0, 0))),
        out_shape=(SDS((t, 1024), BF16), SDS((t, 1024), F32), SDS((nc, 1024, 128), F32)),
        scratch_shapes=[pltpu.VMEM((1024, 128), F32)], name=name, compiler_params=_params(("arbitrary",)),
    )(act, proj, proj, gb, garow, gnw)


def _gdn_bwd(name, act, proj, gb, garow, gnw, oraw, states, dy, dproj):
    t = act.shape[0]
    q = CHUNK
    nc = t // q

    def body(act_ref, gz_ref, sm_ref, gb_ref, ga_ref, nw_ref, o_ref, st_ref, dy_ref, dproj_ref,
             dact_ref, dgz_ref, dsm_ref, dnw_ref, dal_ref, dgb_ref, ds_scr):
        @pl.when(pl.program_id(0) == 0)
        def _():
            ds_scr[...] = jnp.zeros_like(ds_scr)
            dnw_ref[...] = jnp.zeros_like(dnw_ref)
            dal_ref[...] = jnp.zeros_like(dal_ref)
            dgb_ref[...] = jnp.zeros_like(dgb_ref)

        dsn = [ds_scr[h * 128: (h + 1) * 128, :] for h in range(GDN_HEADS)]
        for sub in reversed(range(cps)):
            rows = pl.ds(sub * q, q)
            dsn = chunk(act_ref.at[rows, :], gz_ref.at[rows, :], sm_ref.at[rows, :], gb_ref, ga_ref, nw_ref,
                        o_ref.at[rows, :], st_ref.at[sub], dy_ref.at[rows, :],
                        dact_ref.at[rows, :], dgz_ref.at[rows, :], dsm_ref.at[rows, :], dnw_ref, dal_ref, dgb_ref, dsn)
        for h in range(GDN_HEADS):
            ds_scr[h * 128: (h + 1) * 128, :] = dsn[h]

    def chunk(act_ref, gz_ref, sm_ref, gb_ref, ga_ref, nw_ref, o_ref, st_ref, dy_ref,
              dact_ref, dgz_ref, dsm_ref, dnw_ref, dal_ref, dgb_ref, dsn):
        actv = act_ref[...]
        smv = sm_ref[...]
        garow_v = ga_ref[...]
        cm = _gdn_common(smv, gb_ref[...], garow_v)
        ii, jj, lane = cm["ii"], cm["jj"], cm["lane"]
        nwv = nw_ref[...]
        rowq = lax.broadcasted_iota(jnp.int32, (q, 1), 0)
        dgc_all = jnp.zeros((q, 128), F32)
        dbeta_all = jnp.zeros((q, 128), F32)
        dnw_acc = jnp.zeros((1, 128), F32)
        heads = range(GDN_HEADS)
        sts = [st_ref[h * 128: (h + 1) * 128, :] for h in heads]
        ds_out = []
        ov, gzv, dyv = o_ref[...], gz_ref[...], dy_ref[...]
        hd = _gdn_heads(cm, actv, sts)
        qn, kn, v, eg, ed, egl, bcol = hd["qn"], hd["kn"], hd["v"], hd["eg"], hd["ed"], hd["egl"], hd["bcol"]
        vnew, qkm, qg, w, tt, dm, ak = hd["vnew"], hd["qkm"], hd["qg"], hd["w"], hd["tt"], hd["dm"], hd["ak"]
        do = []
        for h in heads:
            hs = slice(h * 128, (h + 1) * 128)
            o = ov[:, hs]
            rr = lax.rsqrt(jnp.mean(o * o, axis=-1, keepdims=True) + EPS)
            on = o * rr
            gz = gzv[:, hs]
            sz = _sigmoid(gz)
            silz = gz * sz
            dyh = dyv[:, hs]
            dnw_acc = dnw_acc + _colsum(dyh * on * silz)
            dgz_ref[:, hs] = (dyh * on * nwv * (sz * (1.0 + gz * (1.0 - sz)))).astype(BF16)
            don = dyh * nwv * silz
            do.append(rr * (don - on * jnp.mean(don * on, axis=-1, keepdims=True)))
        kd = _each(lambda k, e: k * e, kn, ed)
        dkd = _each(lambda vn, d: _bdot(vn, d, NT), vnew, dsn)
        dvnew_a = _each(lambda k, d: _bdot(k, d, NN), kd, dsn)
        ded = _each(lambda a, b: _rowsum(a * b), dkd, kd)
        dgl = _each(lambda d, s, e, de: jnp.sum(_rowsum(d * s), axis=0, keepdims=True) * e + _colsum(de), dsn, sts, egl, ded)
        dqk = _each(lambda d, vn: jnp.where(ii >= jj, _bdot(d, vn, NT), 0.0), do, vnew)
        dvnew = _each(lambda a, m, d: a + _bdot(m, d, TN), dvnew_a, qkm, do)
        pq = _each(lambda a, b: a * b, dqk, dm)
        w1 = _each(lambda a, b: a * b, dqk, qkm)
        dod = _each(lambda a, b: jnp.concatenate([a, b], axis=0), do, dvnew)
        dos = _each(lambda x, s: _bdot(x, s, NT), dod, sts)
        dqg = [x[:q] for x in dos]
        dw = [-x[q:] for x in dos]
        ds12 = _each(lambda a, ww, x: _bdot(jnp.concatenate([a, -ww], axis=0), x, TN), qg, w, dod)
        dr = _each(lambda t, a, b: _dot3(_split(t), _split(jnp.concatenate([a, b], axis=1)), TN), hd["tm"], dvnew, dw)
        da = _each(lambda r, t: jnp.where(ii > jj, -_dot3(_split(r), _split(t), NT), 0.0), dr, tt)
        sk = _each(lambda r, k: _rowsum(r[:, 128:] * k), dr, kn)
        pk = _each(lambda a, d, b: a * d * b, da, dm, bcol)
        pkn = _each(lambda p, pp, k: _bdot(jnp.concatenate([p, pp + pp.T], axis=0), k, NN), pq, pk, kn)
        dq = _each(lambda a, e, x: a * e + x[:q], dqg, eg, pkn)
        dk = _each(lambda a, e, p, x, r, b, eg_, y: a * e + _bdot(p, x, TN) + r[:, 128:] * (b * eg_) + y[q:],
                   dkd, ed, pq, qn, dr, bcol, eg, pkn)
        w2 = _each(lambda a, k, b: a * (k * b), da, ak, bcol)
        for h in heads:
            hs = slice(h * 128, (h + 1) * 128)
            dgc = (-ded[h] + _rowsum(dqg[h] * qg[h]) + _rowsum(w1[h]) - _rowsum(w1[h].T) + sk[h] * bcol[h] * eg[h]
                   + _rowsum(w2[h]) - _rowsum(w2[h].T) + jnp.where(rowq == q - 1, dgl[h], 0.0))
            dbeta = _rowsum(dr[h][:, :128] * v[h]) + sk[h] * eg[h] + _rowsum(da[h] * ak[h])
            qhat = hd["qr"][h] * hd["rq"][h]
            dqhat = dq[h] * GDN_SCALE
            dact_ref[:, hs] = hd["rq"][h] * (dqhat - qhat * _rowsum(dqhat * qhat))
            dact_ref[:, 1024 + h * 128: 1024 + (h + 1) * 128] = hd["rk"][h] * (dk[h] - kn[h] * _rowsum(dk[h] * kn[h]))
            dact_ref[:, 2048 + h * 128: 2048 + (h + 1) * 128] = dr[h][:, :128] * bcol[h]
            dgc_all = dgc_all + jnp.where(lane == LANE_A + h, dgc, 0.0)
            dbeta_all = dbeta_all + jnp.where(lane == LANE_B + h, dbeta, 0.0)
            ds_out.append(dsn[h] * egl[h] + ds12[h])
        dnw_ref[...] += dnw_acc
        dg = _hdot(cm["triu"], dgc_all)
        da_raw = jnp.where(cm["ma"], dg * garow_v * _sigmoid(cm["spre"]), 0.0)
        dal_ref[...] += _colsum(dg * cm["g"])
        dgb_ref[...] += _colsum(da_raw)
        beta = cm["beta"]
        dsm_ref[...] = (da_raw + dbeta_all * beta * (1.0 - beta)).astype(BF16)
        return ds_out

    cps = 2 if nc % 2 == 0 else 1
    qq = cps * q
    rev = lambda c: nc // cps - 1 - c
    vec128 = pl.BlockSpec((1, 128), lambda c: (0, 0))
    return pl.pallas_call(
        body, grid=(nc // cps,),
        in_specs=[pl.BlockSpec((qq, GDN_QKV), lambda c: (rev(c), 0)),
                  pl.BlockSpec((qq, 1024), lambda c: (rev(c), GZ_OFF // 1024)),
                  pl.BlockSpec((qq, 128), lambda c: (rev(c), SM_OFF // 128)), vec128, vec128, vec128,
                  pl.BlockSpec((qq, 1024), lambda c: (rev(c), 0)), pl.BlockSpec((cps, 1024, 128), lambda c: (rev(c), 0, 0)),
                  pl.BlockSpec((qq, 1024), lambda c: (rev(c), 0)), ANY],
        out_specs=(pl.BlockSpec((qq, GDN_QKV), lambda c: (rev(c), 0)),
                   pl.BlockSpec((qq, 1024), lambda c: (rev(c), GZ_OFF // 1024)),
                   pl.BlockSpec((qq, 128), lambda c: (rev(c), 0)), vec128, vec128, vec128),
        out_shape=(SDS((t, GDN_QKV), F32), SDS(dproj.shape, BF16), SDS((t, 128), BF16), SDS((1, 128), F32),
                   SDS((1, 128), F32), SDS((1, 128), F32)),
        input_output_aliases={9: 1},
        scratch_shapes=[pltpu.VMEM((1024, 128), F32)], name=name, compiler_params=_params(("arbitrary",)),
    )(act, proj, proj, gb, garow, gnw, oraw, states, dy, dproj)


def _row_tile(r):
    for cand in (512, 256, 128, 64, 32, 16, 8):
        if r % cand == 0:
            return cand
    return r


def _sum_terms(name, terms, out_dtype):
    shape = terms[0][0].shape[1:]
    c = shape[-1]
    r = 1
    for s in shape[:-1]:
        r *= s
    tr = min(_row_tile(r), 256)
    n = len(terms)

    def body(*refs):
        acc = refs[0][...].astype(F32)
        for k in range(1, n):
            acc = acc + refs[k][...].astype(F32)
        refs[n][...] = acc.astype(out_dtype)

    in_specs = [pl.BlockSpec((None, tr, c), lambda i, q=lead: (q, i, 0)) for _, lead in terms]
    args = [a.reshape(a.shape[0], r, c) for a, _ in terms]
    out = pl.pallas_call(body, grid=(r // tr,), in_specs=in_specs, out_specs=pl.BlockSpec((tr, c), lambda i: (i, 0)),
                         out_shape=SDS((r, c), out_dtype), name=name, compiler_params=_params(("parallel",)))(*args)
    return out.reshape(shape)


def _adamw(name, w, g, m, v):
    shape = w.shape
    c = shape[-1]
    per_layer = isinstance(g, (list, tuple))
    nl = len(g) if per_layer else 1
    gs = [a.reshape(-1, c) for a in g] if per_layer else [g.reshape(-1, c)]
    r = gs[0].shape[0]
    w3, m3, v3 = (a.reshape(nl, r, c) for a in (w, m, v))
    tr = min(_row_tile(r), 256)

    def body(*refs):
        w_ref, m_ref, v_ref = refs[:3]
        g_refs = refs[3: 3 + nl]
        go_ref, d_ref, nm_ref, nv_ref = refs[3 + nl:]
        layer = pl.program_id(0)
        gv = g_refs[0][...]
        for k in range(1, nl):
            gv = jnp.where(layer == k, g_refs[k][...], gv)
        mn = ADAM_B1 * m_ref[...] + (1.0 - ADAM_B1) * gv
        vn = ADAM_B2 * v_ref[...] + (1.0 - ADAM_B2) * (gv * gv)
        m_hat = mn / (1.0 - ADAM_B1 ** ADAM_STEP)
        v_hat = vn / (1.0 - ADAM_B2 ** ADAM_STEP)
        go_ref[...] = gv
        d_ref[...] = -ADAM_LR * (m_hat / (jnp.sqrt(v_hat) + ADAM_EPS) + ADAM_WD * w_ref[...])
        nm_ref[...] = mn
        nv_ref[...] = vn

    spec3 = pl.BlockSpec((None, tr, c), lambda l, i: (l, i, 0))
    gspec = pl.BlockSpec((tr, c), lambda l, i: (i, 0))
    outs = pl.pallas_call(body, grid=(nl, r // tr), in_specs=[spec3] * 3 + [gspec] * nl, out_specs=(spec3,) * 4,
                          out_shape=(SDS((nl, r, c), F32),) * 4, name=name,
                          compiler_params=_params(("parallel", "parallel")))(w3, m3, v3, *gs)
    return tuple(o.reshape(shape) for o in outs)


ANY = pl.BlockSpec(memory_space=pl.ANY)
MESH = pl.DeviceIdType.MESH


def _allgather(name, xs):
    n = len(xs)

    def body(*refs):
        x_refs, out_refs = refs[:n], refs[n: 2 * n]
        send_sems, recv_sems, local_sems = refs[2 * n:]
        x, y, cc = lax.axis_index("x"), lax.axis_index("y"), lax.axis_index("c")
        me, sibling = (x, y, cc), (x, y, 1 - cc)
        chips = [(1 - x, y), (x, 1 - y), (1 - x, 1 - y)]

        def rows(a, px, py, pc):
            return out_refs[a].at[4 * px + 2 * py + pc]

        def copy(a, k, block, to, src=None):
            return pltpu.make_async_remote_copy(
                src_ref=rows(a, *block) if src is None else src, dst_ref=rows(a, *block),
                send_sem=send_sems.at[7 * a + k], recv_sem=recv_sems.at[7 * a + k], device_id=to, device_id_type=MESH)

        mine = [pltpu.make_async_copy(x_refs[a], rows(a, *me), local_sems.at[a]) for a in range(n)]
        for cp in mine:
            cp.start()
        first = []
        for a in range(n):
            first.append(copy(a, 0, me, sibling, src=x_refs[a]))
            first += [copy(a, 1 + j, me, (*chip, cc), src=x_refs[a]) for j, chip in enumerate(chips)]
        for cp in first:
            cp.start()
        passed = []
        for j, chip in enumerate(chips):
            for a in range(n):
                copy(a, 1 + j, (*chip, cc), me).wait_recv()
                fwd = copy(a, 4 + j, (*chip, cc), sibling)
                fwd.start()
                passed.append(fwd)
        for a in range(n):
            copy(a, 0, sibling, me).wait_recv()
        for j, chip in enumerate(chips):
            for a in range(n):
                copy(a, 4 + j, (*chip, 1 - cc), me).wait_recv()
        for cp in first + passed:
            cp.wait_send()
        for cp in mine:
            cp.wait()

    return pl.pallas_call(
        body, out_shape=tuple(SDS((N_DEV,) + a.shape, a.dtype) for a in xs), in_specs=[ANY] * n, out_specs=(ANY,) * n,
        scratch_shapes=[pltpu.SemaphoreType.DMA((7 * n,)), pltpu.SemaphoreType.DMA((7 * n,)),
                        pltpu.SemaphoreType.DMA((n,))],
        name=name,
    )(*xs)


def _allgather_seq(name, xs, collective_id):
    n = len(xs)
    x_refs = [jax.new_ref(a, memory_space=pltpu.MemorySpace.HBM) for a in xs]
    out_refs = [jax.empty_ref(SDS((N_DEV,) + a.shape, a.dtype), memory_space=pltpu.MemorySpace.HBM) for a in xs]

    @pl.kernel(mesh=plsc.ScalarSubcoreMesh(axis_name="seq", num_cores=1), name=name,
               scratch_types=(pltpu.SemaphoreType.DMA((7 * n,)), pltpu.SemaphoreType.DMA((7 * n,)),
                              pltpu.SemaphoreType.DMA((n,))),
               compiler_params=pltpu.CompilerParams(collective_id=collective_id))
    def launch(send_sems, recv_sems, local_sems):
        x, y, cc = lax.axis_index("x"), lax.axis_index("y"), lax.axis_index("c")
        me, sibling = (x, y, cc), (x, y, 1 - cc)
        chips = [(1 - x, y), (x, 1 - y), (1 - x, 1 - y)]
        barrier = pltpu.get_barrier_semaphore()
        for peer in [sibling] + [(*chip, cc) for chip in chips]:
            pl.semaphore_signal(barrier, inc=1, device_id=peer, device_id_type=MESH)
        pl.semaphore_wait(barrier, 4)

        def rows(a, px, py, pc):
            return out_refs[a].at[4 * px + 2 * py + pc]

        def copy(a, k, block, to, src=None):
            return pltpu.make_async_remote_copy(
                src_ref=rows(a, *block) if src is None else src, dst_ref=rows(a, *block),
                send_sem=send_sems.at[7 * a + k], recv_sem=recv_sems.at[7 * a + k], device_id=to, device_id_type=MESH)

        mine = [pltpu.make_async_copy(x_refs[a], rows(a, *me), local_sems.at[a]) for a in range(n)]
        for cp in mine:
            cp.start()
        first = []
        for a in range(n):
            first.append(copy(a, 0, me, sibling, src=x_refs[a]))
            first += [copy(a, 1 + j, me, (*chip, cc), src=x_refs[a]) for j, chip in enumerate(chips)]
        for cp in first:
            cp.start()
        passed = []
        for j, chip in enumerate(chips):
            for a in range(n):
                copy(a, 1 + j, (*chip, cc), me).wait_recv()
                fwd = copy(a, 4 + j, (*chip, cc), sibling)
                fwd.start()
                passed.append(fwd)
        for a in range(n):
            copy(a, 0, sibling, me).wait_recv()
        for j, chip in enumerate(chips):
            for a in range(n):
                copy(a, 4 + j, (*chip, 1 - cc), me).wait_recv()
        for cp in first + passed:
            cp.wait_send()
        for cp in mine:
            cp.wait()

    launch()
    return [r[...] for r in out_refs]


HBM = pl.BlockSpec(memory_space=pltpu.HBM)
SEM = pl.BlockSpec(memory_space=pltpu.SEMAPHORE)
EFFECT = pltpu.SideEffectType.DATAFLOW_SIDE_EFFECTING


def _sibling_plan(srcs, lands, send_sems, recv_sems):
    x, y, cc = lax.axis_index("x"), lax.axis_index("y"), lax.axis_index("c")
    return [pltpu.make_async_remote_copy(
        src_ref=srcs[a].at[2 * q + 1 - cc], dst_ref=lands[a].at[q], send_sem=send_sems.at[4 * a + q],
        recv_sem=recv_sems.at[4 * a + q], device_id=(x, y, 1 - cc), device_id_type=MESH)
        for a in range(len(srcs)) for q in range(4)]


def _chips_plan(srcs, lands, send_sems, recv_sems):
    x, y, cc = lax.axis_index("x"), lax.axis_index("y"), lax.axis_index("c")
    chips = [(1 - x, y), (x, 1 - y), (1 - x, 1 - y)]
    return [pltpu.make_async_remote_copy(
        src_ref=srcs[a].at[2 * px + py], dst_ref=lands[a].at[j], send_sem=send_sems.at[3 * a + j],
        recv_sem=recv_sems.at[3 * a + j], device_id=(px, py, cc), device_id_type=MESH)
        for a in range(len(srcs)) for j, (px, py) in enumerate(chips)]


def _copies_start(name, plan, per_array, srcs, land_lead):
    n = len(srcs)
    k = per_array * n

    def body(*refs):
        src_refs, land_refs = refs[:n], refs[n: 2 * n]
        send_sems, recv_sems = refs[2 * n], refs[2 * n + 1]
        token = refs[-1]
        for cp in plan(src_refs, land_refs, send_sems, recv_sems):
            cp.start()
        token[...] = jnp.zeros_like(token)

    lands = [lax.empty((land_lead,) + a.shape[1:], a.dtype) for a in srcs]
    outs = pl.pallas_call(
        body, name=name,
        out_shape=(pltpu.SemaphoreType.DMA((k,)), pltpu.SemaphoreType.DMA((k,)),
                   *[pltpu.HBM(a.shape, a.dtype) for a in srcs], *[pltpu.HBM(a.shape, a.dtype) for a in lands],
                   SDS((8, 128), F32)),
        in_specs=[HBM] * (2 * n), out_specs=(SEM, SEM, *[HBM] * (2 * n), pl.BlockSpec(memory_space=pltpu.VMEM)),
        input_output_aliases={i: 2 + i for i in range(2 * n)},
        compiler_params=pltpu.CompilerParams(has_side_effects=EFFECT),
    )(*[pltpu.with_memory_space_constraint(a, pltpu.HBM) for a in srcs],
      *[pltpu.with_memory_space_constraint(a, pltpu.HBM) for a in lands])
    return outs[0], outs[1], list(outs[2: 2 + n]), list(outs[2 + n: 2 + 2 * n]), outs[-1]


def _copies_wait(name, plan, started, after):
    send_sems, recv_sems, srcs, lands, _ = started
    n = len(srcs)
    after = tuple(after)

    def body(*refs):
        src_refs, land_refs = refs[:n], refs[n: 2 * n]
        for cp in plan(src_refs, land_refs, refs[2 * n], refs[2 * n + 1]):
            cp.wait_send()
            cp.wait_recv()

    outs = pl.pallas_call(
        body, name=name,
        out_shape=tuple(pltpu.HBM(a.shape, a.dtype) for a in srcs + lands),
        in_specs=[HBM] * (2 * n) + [SEM, SEM] + [ANY] * len(after), out_specs=(HBM,) * (2 * n),
        input_output_aliases={i: i for i in range(2 * n)},
        compiler_params=pltpu.CompilerParams(has_side_effects=EFFECT),
    )(*srcs, *lands, send_sems, recv_sems, *after)
    return list(outs[n:])


BIG = (("w_in", 1), ("w_ffn_in", 1), ("w_proj_ssm", 0), ("w_proj_gdn", 0), ("w_out", 0), ("w_ffn_down", 0))
CONVS = (("ssm_conv_w", 1), ("gdn_conv_w", 1))


def _to_dest_major(full, axis):
    if isinstance(full, tuple):
        per = N_DEV // len(full)
        s = full[0].shape[1] // per
        return jnp.stack([full[d // per][:, (d % per) * s: (d % per + 1) * s] for d in range(N_DEV)])
    a, b = full.shape
    if axis == 0:
        return full.reshape(N_DEV, a // N_DEV, b)
    s = b // N_DEV
    return jnp.stack([full[:, d * s: (d + 1) * s] for d in range(N_DEV)])


def _from_gathered(g, axis):
    if axis == 0:
        return g.reshape(-1, g.shape[2])
    return jnp.concatenate([g[d] for d in range(N_DEV)], axis=1)


IN_RUNS = ((Z_OFF, O_Z, 1024), (GZ_OFF, O_GZ, 1024), (G1_OFF, O_G1, 1024), (G2_OFF, O_G2, 1024), (QKV_OFF, O_QKV, 3072),
           (XBC_OFF, O_XBC, 1536), (SM_OFF, O_DT, 16), (SM_OFF + LANE_A, O_A, 8), (SM_OFF + LANE_B, O_B, 8))
IN_SHARD = IN_DIM // N_DEV


def _w_in_from_blocks(g):
    rows = g.shape[1]
    parts, pos = [], 0
    for off, o0, width in IN_RUNS:
        if off > pos:
            parts.append(jnp.zeros((rows, off - pos), g.dtype))
        c = o0
        while c < o0 + width:
            d = c // IN_SHARD
            hi = min(o0 + width, (d + 1) * IN_SHARD)
            parts.append(g[d][:, c - d * IN_SHARD: hi - d * IN_SHARD])
            c = hi
        pos = off + width
    parts.append(jnp.zeros((rows, PROJ_W - pos), g.dtype))
    return jnp.concatenate(parts, axis=1)


def _w_in_to_blocks(wp):
    by_orig = sorted(IN_RUNS, key=lambda r: r[1])
    blocks = []
    for d in range(N_DEV):
        lo, hi = d * IN_SHARD, (d + 1) * IN_SHARD
        parts = []
        for off, o0, width in by_orig:
            a, b = max(lo, o0), min(hi, o0 + width)
            if a < b:
                parts.append(wp[:, off + a - o0: off + b - o0])
        blocks.append(jnp.concatenate(parts, axis=1))
    return jnp.stack(blocks)


def _pad128(v, lane0):
    return jnp.zeros((1, 128), F32).at[0, lane0: lane0 + v.shape[0]].set(v)


def _layer_consts(p):
    return dict(
        dtb=_pad128(p["ssm_dt_bias"], 0), arow=_pad128(-jnp.exp(p["ssm_a_log"]), 0),
        dxrow=jnp.repeat(p["ssm_d"], SSM_P).reshape(1, 1024), snw=p["ssm_norm_w"].reshape(1, 1024),
        gb=_pad128(p["gdn_dt_bias"], LANE_A), garow=_pad128(-jnp.exp(p["gdn_a_log"]), LANE_A),
        gnw=p["gdn_norm_w"].reshape(1, 128), zb=jnp.zeros((1, GDN_QKV), F32), scb=p["ssm_conv_b"].reshape(1, SSM_CONV))


def _expand_matrix():
    row = lax.broadcasted_iota(jnp.int32, (128, 1024), 0)
    col = lax.broadcasted_iota(jnp.int32, (128, 1024), 1)
    return (col // SSM_P == row).astype(BF16)


def _silu_mul_epi(acc, up):
    g = acc
    return g, g * _sigmoid(g) * up.astype(F32)


def _merge_epi(acc, p1, g1, g2):
    return acc, _sigmoid(g1) * p1.astype(F32) + _sigmoid(g2) * acc


def _add_epi(acc, res):
    return (acc + res,)


def _ffn_bwd_epi(acc, gate, up):
    g = gate.astype(F32)
    sg = _sigmoid(g)
    return acc * up.astype(F32) * (sg * (1.0 + g * (1.0 - sg))), acc * (g * sg)


def _merge_bwd_epi(acc, g1, g2, p1, p2):
    s1, s2 = _sigmoid(g1), _sigmoid(g2)
    dg1, dg2 = acc * p1.astype(F32) * (s1 * (1.0 - s1)), acc * p2.astype(F32) * (s2 * (1.0 - s2))
    return acc * s1, acc * s2, jnp.concatenate([dg1, dg2], axis=1)


def _layer_fwd(l, x, p, rmat):
    t = x.shape[0]
    n = f"l{l}_"
    k = _layer_consts(p)
    h = _rmsnorm_fwd(n + "norm_mix", x, p["norm_mix_w"])
    proj = _matmul(n + "in_proj", "nn", [(h, 0, p["w_in"], 0)], t, PROJ_W, 1024, 1024, 1280, 1024, (F32,))
    act_g, pre_g = _conv_fwd(n + "conv_gdn", proj, QKV_OFF, p["gdn_conv_w"], k["zb"])
    act_s, pre_s = _conv_fwd(n + "conv_ssm", proj, XBC_OFF, p["ssm_conv_w"], k["scb"])
    y_ssm, ysc, st_s = _ssd_fwd(n + "ssd_fwd", act_s, proj, k["dtb"], k["arow"], k["dxrow"], k["snw"], rmat)
    y_gdn, oraw, st_g = _gdn_fwd(n + "gdn_fwd", act_g, proj, k["gb"], k["garow"], k["gnw"])
    if "late" in p:
        y_gdn, late = p["late"](y_gdn)
        p = {**p, **late}
    p1 = _matmul(n + "proj_ssm", "nn", [(y_ssm, 0, p["w_proj_ssm"], 0)], t, 1024, 1024, 1024, 1024, 1024, (BF16,))
    p2, merged = _matmul(n + "proj_gdn_merge", "nn", [(y_gdn, 0, p["w_proj_gdn"], 0)], t, 1024, 1024, 512, 1024, 1024,
                         (BF16, BF16), epi=_merge_epi, extras=[(p1, 0), (proj, G1_OFF // 1024), (proj, G2_OFF // 1024)])
    x1 = _matmul(n + "out_proj", "nn", [(merged, 0, p["w_out"], 0)], t, 1024, 1024, 1024, 1024, 1024, (F32,),
                 epi=_add_epi, extras=[(x, 0)])
    h2 = _rmsnorm_fwd(n + "norm_ffn", x1, p["norm_ffn_w"])
    up = _matmul(n + "ffn_up", "nn", [(h2, 0, p["w_ffn_in"], 2)], t, FFN, 1024, 1024, FFN // 2, 1024, (BF16,))
    gate, act = _matmul(n + "ffn_gate", "nn", [(h2, 0, p["w_ffn_in"], 0)], t, FFN, 1024, 1024, FFN // 2, 1024, (BF16, BF16),
                        epi=_silu_mul_epi, extras=[(up, 0)])
    x2 = _matmul(n + "ffn_down", "nn", [(act, 0, p["w_ffn_down"], 0)], t, 1024, FFN, 1024, 1024, FFN, (F32,),
                 epi=_add_epi, extras=[(x1, 0)])
    saved = dict(x=x, h=h, proj=proj, act_g=act_g, act_s=act_s, pre_g=pre_g, pre_s=pre_s, y_ssm=y_ssm, ysc=ysc, st_s=st_s, y_gdn=y_gdn, oraw=oraw,
                 st_g=st_g, p1=p1, p2=p2, merged=merged, x1=x1, h2=h2, up=up, gate=gate, act=act, k=k, p=p)
    return x2, saved


def _layer_bwd(l, dx2, dx2b, s, p, rmat, hooks):
    t = dx2.shape[0]
    n = f"l{l}_"
    k = s["k"]
    tk_tok = 1024
    hf = FFN // 2
    g = {}
    dgate, dup = _matmul(n + "d_ffn_act", "nt", [(dx2b, 0, p["w_ffn_down"], 0)], t, FFN, 1024, 1024, hf, 1024, (BF16, BF16),
                         epi=_ffn_bwd_epi, extras=[(s["gate"], 0), (s["up"], 0)])
    g["w_ffn_down"] = _matmul(n + "dw_ffn_down", "tn", [(s["act"], 0, dx2b, 0)], FFN, 1024, t, hf, 1024, tk_tok, (BF16,))
    dh2 = _matmul(n + "d_ffn_in", "nt", [(dgate, 0, p["w_ffn_in"], 0), (dup, 0, p["w_ffn_in"], 2)], t, 1024, FFN,
                  1024, 1024, hf, (F32,))
    dwg = _matmul(n + "dw_ffn_gate", "tn", [(s["h2"], 0, dgate, 0)], 1024, FFN, t, 1024, hf, tk_tok, (BF16,))
    dwu = _matmul(n + "dw_ffn_up", "tn", [(s["h2"], 0, dup, 0)], 1024, FFN, t, 1024, hf, tk_tok, (BF16,))
    g["w_ffn_in"] = (dwg, dwu)
    dx1, dx1b, g["norm_ffn_w"] = _rmsnorm_bwd(n + "d_norm_ffn", s["x1"], p["norm_ffn_w"], dh2, dx2)
    dx1b = hooks.ffn_done(dx1b)
    dp1, dp2, dproj = _matmul(
        n + "d_out_proj", "nt", [(dx1b, 0, p["w_out"], 0)], t, 1024, 1024, 512, 1024, 1024,
        (BF16, BF16, (BF16, PROJ_W, 2048, G1_OFF // 2048)), epi=_merge_bwd_epi,
        extras=[(s["proj"], G1_OFF // 1024), (s["proj"], G2_OFF // 1024), (s["p1"], 0), (s["p2"], 0)])
    g["w_out"] = _matmul(n + "dw_out", "tn", [(s["merged"], 0, dx1b, 0)], 1024, 1024, t, 1024, 1024, tk_tok, (BF16,))
    g["w_proj_ssm"] = _matmul(n + "dw_proj_ssm", "tn", [(s["y_ssm"], 0, dp1, 0)], 1024, 1024, t, 1024, 1024, tk_tok, (BF16,))
    g["w_proj_gdn"] = _matmul(n + "dw_proj_gdn", "tn", [(s["y_gdn"], 0, dp2, 0)], 1024, 1024, t, 1024, 1024, tk_tok, (BF16,))
    dp1, dp2 = hooks.early_ready(l, g, dp1, dp2)
    dy_ssm = _matmul(n + "d_proj_ssm", "nt", [(dp1, 0, p["w_proj_ssm"], 0)], t, 1024, 1024, 1024, 1024, 1024, (F32,))
    dy_gdn = _matmul(n + "d_proj_gdn", "nt", [(dp2, 0, p["w_proj_gdn"], 0)], t, 1024, 1024, 1024, 1024, 1024, (F32,))
    dact_s, dproj, dsm_s, dsnw, dd, dal, ddtb = _ssd_bwd(n + "ssd_bwd", s["act_s"], s["proj"], k["dtb"], k["arow"],
                                                           k["dxrow"], k["snw"], rmat, s["ysc"], s["st_s"], dy_ssm, dproj)
    dact_g, dproj, dsm_g, dgnw, dgal, dgb = _gdn_bwd(n + "gdn_bwd", s["act_g"], s["proj"], k["gb"], k["garow"], k["gnw"],
                                                       s["oraw"], s["st_g"], dy_gdn, dproj)
    dsm_s = hooks.mixers_done(dsm_s)
    dproj = _place_small(n + "d_small", dsm_s, dsm_g, dproj)
    dproj, g["ssm_conv_w"], dcb = _conv_bwd(n + "d_conv_ssm", s["proj"], XBC_OFF, p["ssm_conv_w"], s["pre_s"], dact_s, dproj)
    dproj, g["gdn_conv_w"], _ = _conv_bwd(n + "d_conv_gdn", s["proj"], QKV_OFF, p["gdn_conv_w"], s["pre_g"], dact_g, dproj)
    g["ssm_conv_b"] = dcb.reshape(-1)
    g["ssm_norm_w"] = dsnw.reshape(-1)
    g["ssm_d"] = dd[0, :SSM_HEADS]
    g["ssm_a_log"] = dal[0, :SSM_HEADS]
    g["ssm_dt_bias"] = ddtb[0, :SSM_HEADS]
    g["gdn_norm_w"] = dgnw.reshape(-1)
    g["gdn_a_log"] = dgal[0, LANE_A: LANE_A + GDN_HEADS]
    g["gdn_dt_bias"] = dgb[0, LANE_A: LANE_A + GDN_HEADS]
    dh = _matmul(n + "d_in_proj", "nt", [(dproj, 0, p["w_in"], 0)], t, 1024, PROJ_W, 1024, 1024, 1280, (F32,))
    g["w_in"] = _matmul(n + "dw_in", "tn", [(s["h"], 0, dproj, 0)], 1024, PROJ_W, t, 1024, 1280, tk_tok, (BF16,))
    dx, dxb, g["norm_mix_w"] = _rmsnorm_bwd(n + "d_norm_mix", s["x"], p["norm_mix_w"], dh, dx1)
    g["norm_mix_w"] = g["norm_mix_w"].reshape(-1)
    g["norm_ffn_w"] = g["norm_ffn_w"].reshape(-1)
    return dx, dxb, g


def _local_step(x, tgt, layers, final_norm_w, reduce=False):
    rmat = _expand_matrix()
    saved, params = [], []
    for l in range(DEPTH):
        x, p = layers[l](x)
        x, s = _layer_fwd(l, x, p, rmat)
        saved.append(s)
        params.append(s["p"])
    loss, dx, dxb, dfw = _loss_head("loss_head", x, final_norm_w, tgt)
    grads = [None] * DEPTH
    hooks = _ReduceBesideBackward() if reduce else _NoReduce()
    for l in reversed(range(DEPTH)):
        dx, dxb, grads[l] = _layer_bwd(l, dx, dxb, saved[l], params[l], rmat, hooks)
        if reduce:
            dxb = hooks.layer_done(l, grads[l], dxb)
    if reduce:
        hooks.finish_start(dxb)
    return loss[0, 0], dx, grads, dfw.reshape(-1), hooks if reduce else None


SMALL = ("norm_mix_w", "ssm_conv_b", "ssm_dt_bias", "ssm_a_log", "ssm_d", "ssm_norm_w", "gdn_a_log", "gdn_dt_bias",
         "gdn_norm_w", "norm_ffn_w")
WEIGHTS = ("norm_mix_w", "w_in", "ssm_conv_w", "ssm_conv_b", "ssm_dt_bias", "ssm_a_log", "ssm_d", "ssm_norm_w", "gdn_conv_w",
           "gdn_a_log", "gdn_dt_bias", "gdn_norm_w", "w_proj_ssm", "w_proj_gdn", "w_out", "norm_ffn_w", "w_ffn_in",
           "w_ffn_down", "final_norm_w")


FIRST_USED = ("w_in", "ssm_conv_w", "gdn_conv_w")


def _gather_layer(l, w):
    conv_names = [nm for nm, _ in CONVS]
    groups = ([s for s in BIG + CONVS if s[0] in FIRST_USED], [s for s in BIG + CONVS if s[0] not in FIRST_USED])
    gathered = []
    for i, (specs, tag) in enumerate(zip(groups, ("first", "rest"))):
        shards = [w[nm][l] if nm in conv_names else w[nm][l].astype(BF16) for nm, _ in specs]
        gathered.append(_allgather_seq(f"l{l}_gather_{tag}", shards, collective_id=2 * l + i))
    small = {nm: w[nm][l] for nm in SMALL}

    def use(i, act):
        act, blocks = lax.optimization_barrier((act, gathered[i]))
        return act, {nm: _w_in_from_blocks(g) if nm == "w_in" else _from_gathered(g, axis)
                     for (nm, axis), g in zip(groups[i], blocks)}

    def full_weights(x):
        x, out = use(0, x)
        out.update(small)
        out["late"] = lambda y: use(1, y)
        return x, out

    return full_weights


EARLY_GRADS = ("w_ffn_down", "w_ffn_in", "w_out", "w_proj_ssm", "w_proj_gdn")


class _GradReduceScatter:
    def __init__(self, tag, specs, grads):
        self.tag = tag
        self.specs = specs
        self.blocks = [_w_in_to_blocks(grads[nm]) if nm == "w_in" else _to_dest_major(grads[nm], axis)
                       for nm, axis in specs]

    def _tied(self, started, acts):
        *acts, self.token = lax.optimization_barrier((*acts, started[4]))
        return acts

    def start(self, *acts):
        cc = lax.axis_index("c")
        self.keep = [lax.dynamic_index_in_dim(b.reshape((4, 2) + b.shape[1:]), cc, axis=1, keepdims=False)
                     for b in self.blocks]
        self.to_sibling = _copies_start(f"{self.tag}_to_sibling_start", _sibling_plan, 4, self.blocks, 4)
        return self._tied(self.to_sibling, acts)

    def mid(self, *acts):
        got = _copies_wait(f"{self.tag}_to_sibling_wait", _sibling_plan, self.to_sibling, (acts[0], self.token))
        chip_sums = [_sum_terms(f"{self.tag}_chip_sum_{nm}", [(k[None], 0), (g[None], 0)], BF16)
                     for (nm, _), k, g in zip(self.specs, self.keep, got)]
        self.to_chips = _copies_start(f"{self.tag}_between_chips_start", _chips_plan, 3, chip_sums, 3)
        return self._tied(self.to_chips, acts)

    def end(self, after):
        after = tuple(after) if isinstance(after, (tuple, list)) else (after,)
        landed = _copies_wait(f"{self.tag}_between_chips_wait", _chips_plan, self.to_chips, (*after, self.token))
        my_chip = 2 * lax.axis_index("x") + lax.axis_index("y")
        own = [lax.dynamic_index_in_dim(s, my_chip, axis=0, keepdims=True) for s in self.to_chips[2]]
        return {nm: _sum_terms(f"{self.tag}_total_{nm}", [(o, 0), (e, 0), (e, 1), (e, 2)], F32)
                for (nm, _), o, e in zip(self.specs, own, landed)}


class _NoReduce:
    def ffn_done(self, dx1b):
        return dx1b

    def early_ready(self, l, g, dp1, dp2):
        return dp1, dp2

    def mixers_done(self, dsm):
        return dsm


class _ReduceBesideBackward(_NoReduce):
    def __init__(self):
        self.late = None
        self.early = None
        self.shards = [dict() for _ in range(DEPTH)]

    def ffn_done(self, dx1b):
        if self.late is not None:
            (dx1b,) = self.late.mid(dx1b)
        return dx1b

    def early_ready(self, l, g, dp1, dp2):
        self.early = _GradReduceScatter(f"l{l}_early_grads", [s for s in BIG if s[0] in EARLY_GRADS], g)
        return self.early.start(dp1, dp2)

    def mixers_done(self, dsm):
        (dsm,) = self.early.mid(dsm)
        return dsm

    def layer_done(self, l, g, dxb):
        if self.late is not None:
            self.shards[l + 1].update(self.late.end(dxb))
        self.shards[l].update(self.early.end(dxb))
        self.late = _GradReduceScatter(f"l{l}_late_grads", [s for s in BIG + CONVS if s[0] not in EARLY_GRADS], g)
        (dxb,) = self.late.start(dxb)
        return dxb

    def finish_start(self, dxb):
        self.late.mid(dxb)

    def finish_end(self, after):
        return self.late.end(after)


def _allreduce_small(vecs):
    flat = jnp.concatenate(vecs)
    n = flat.shape[0]
    rows = -(-n // 128)
    rows = -(-rows // 8) * 8
    buf = jnp.pad(flat, (0, rows * 128 - n)).reshape(rows, 128)
    (allv,) = _allgather("gather_small_grads", [buf])
    tot = _sum_terms("small_grads_total", [(allv, d) for d in range(N_DEV)], F32).reshape(-1)
    out, o = [], 0
    for v in vecs:
        out.append(tot[o: o + v.shape[0]])
        o += v.shape[0]
    return out


def kernel(x, norm_mix_w, w_in, ssm_conv_w, ssm_conv_b, ssm_dt_bias, ssm_a_log, ssm_d, ssm_norm_w, gdn_conv_w, gdn_a_log, gdn_dt_bias, gdn_norm_w, w_proj_ssm, w_proj_gdn, w_out, norm_ffn_w, w_ffn_in, w_ffn_down, final_norm_w, loss_target, m_norm_mix_w, m_w_in, m_ssm_conv_w, m_ssm_conv_b, m_ssm_dt_bias, m_ssm_a_log, m_ssm_d, m_ssm_norm_w, m_gdn_conv_w, m_gdn_a_log, m_gdn_dt_bias, m_gdn_norm_w, m_w_proj_ssm, m_w_proj_gdn, m_w_out, m_norm_ffn_w, m_w_ffn_in, m_w_ffn_down, m_final_norm_w, v_norm_mix_w, v_w_in, v_ssm_conv_w, v_ssm_conv_b, v_ssm_dt_bias, v_ssm_a_log, v_ssm_d, v_ssm_norm_w, v_gdn_conv_w, v_gdn_a_log, v_gdn_dt_bias, v_gdn_norm_w, v_w_proj_ssm, v_w_proj_gdn, v_w_out, v_norm_ffn_w, v_w_ffn_in, v_w_ffn_down, v_final_norm_w):
    w = dict(norm_mix_w=norm_mix_w, w_in=w_in, ssm_conv_w=ssm_conv_w, ssm_conv_b=ssm_conv_b, ssm_dt_bias=ssm_dt_bias,
             ssm_a_log=ssm_a_log, ssm_d=ssm_d, ssm_norm_w=ssm_norm_w, gdn_conv_w=gdn_conv_w, gdn_a_log=gdn_a_log,
             gdn_dt_bias=gdn_dt_bias, gdn_norm_w=gdn_norm_w, w_proj_ssm=w_proj_ssm, w_proj_gdn=w_proj_gdn, w_out=w_out,
             norm_ffn_w=norm_ffn_w, w_ffn_in=w_ffn_in, w_ffn_down=w_ffn_down, final_norm_w=final_norm_w)
    m = dict(norm_mix_w=m_norm_mix_w, w_in=m_w_in, ssm_conv_w=m_ssm_conv_w, ssm_conv_b=m_ssm_conv_b, ssm_dt_bias=m_ssm_dt_bias,
             ssm_a_log=m_ssm_a_log, ssm_d=m_ssm_d, ssm_norm_w=m_ssm_norm_w, gdn_conv_w=m_gdn_conv_w, gdn_a_log=m_gdn_a_log,
             gdn_dt_bias=m_gdn_dt_bias, gdn_norm_w=m_gdn_norm_w, w_proj_ssm=m_w_proj_ssm, w_proj_gdn=m_w_proj_gdn,
             w_out=m_w_out, norm_ffn_w=m_norm_ffn_w, w_ffn_in=m_w_ffn_in, w_ffn_down=m_w_ffn_down,
             final_norm_w=m_final_norm_w)
    v = dict(norm_mix_w=v_norm_mix_w, w_in=v_w_in, ssm_conv_w=v_ssm_conv_w, ssm_conv_b=v_ssm_conv_b, ssm_dt_bias=v_ssm_dt_bias,
             ssm_a_log=v_ssm_a_log, ssm_d=v_ssm_d, ssm_norm_w=v_ssm_norm_w, gdn_conv_w=v_gdn_conv_w, gdn_a_log=v_gdn_a_log,
             gdn_dt_bias=v_gdn_dt_bias, gdn_norm_w=v_gdn_norm_w, w_proj_ssm=v_w_proj_ssm, w_proj_gdn=v_w_proj_gdn,
             w_out=v_w_out, norm_ffn_w=v_norm_ffn_w, w_ffn_in=v_w_ffn_in, w_ffn_down=v_w_ffn_down,
             final_norm_w=v_final_norm_w)

    layers = [_gather_layer(l, w) for l in range(DEPTH)]
    loss_part, dx, lgrads, dfw, reducer = _local_step(x[0], loss_target[0], layers, final_norm_w, reduce=True)
    loss = lax.psum(loss_part, ("x", "y", "c"))
    shard_grads = reducer.shards
    late = [nm for nm, _ in BIG + CONVS if nm not in EARLY_GRADS]
    grad = {nm: [shard_grads[l][nm] for l in range(DEPTH)] for nm in EARLY_GRADS}
    small_vecs = [lgrads[l][nm].reshape(-1) for l in range(DEPTH) for nm in SMALL] + [dfw]
    small_sum = _allreduce_small(small_vecs)
    for i, nm in enumerate(SMALL):
        grad[nm] = jnp.stack([small_sum[l * len(SMALL) + i].reshape(w[nm].shape[1:]) for l in range(DEPTH)])
    grad["final_norm_w"] = small_sum[-1]

    deltas, new_m, new_v = {}, {}, {}
    for nm in [n for n in WEIGHTS if n not in late]:
        grad[nm], deltas[nm], new_m[nm], new_v[nm] = _adamw("adamw_" + nm, w[nm], grad[nm], m[nm], v[nm])
    shard_grads[0].update(reducer.finish_end([deltas[nm] for nm in EARLY_GRADS]))
    for nm in late:
        grad[nm], deltas[nm], new_m[nm], new_v[nm] = _adamw("adamw_" + nm, w[nm], [shard_grads[l][nm] for l in range(DEPTH)],
                                                            m[nm], v[nm])
    return (loss, dx[None], *[grad[nm] for nm in WEIGHTS], *[deltas[nm] for nm in WEIGHTS],
            *[new_m[nm] for nm in WEIGHTS], *[new_v[nm] for nm in WEIGHTS])
```

```python
import functools

import jax
import jax.numpy as jnp
from jax import lax
from jax.experimental import pallas as pl
from jax.experimental.pallas import tpu as pltpu
from jax.experimental.pallas import tpu_sc as plsc

F32 = jnp.float32
BF16 = jnp.bfloat16
HI = lax.Precision.HIGHEST
SDS = jax.ShapeDtypeStruct

D_MODEL = 1024
DEPTH = 2
SSM_HEADS = 16
SSM_P = 64
SSM_N = 128
SSM_GROUPS = 2
SSM_CONV = 1536
GDN_HEADS = 8
GDN_DK = 128
GDN_QKV = 3072
CONV_K = 4
CHUNK = 64
SCAN_CHUNKS_PER_STEP = 4
FFN = 2816
IN_DIM = 8736
EPS = 1e-6
N_DEV = 8

Z_OFF = 0
GZ_OFF = 1024
G1_OFF = 2048
G2_OFF = 3072
QKV_OFF = 4096
XBC_OFF = 7168
SM_OFF = 8704
PROJ_W = 8960
LANE_A = 16
LANE_B = 24
O_Z, O_XBC, O_DT, O_QKV, O_GZ, O_A, O_B, O_G1, O_G2 = 0, 1024, 2560, 2576, 5648, 6672, 6680, 6688, 7712

ADAM_LR = 0.001
ADAM_B1 = 0.9
ADAM_B2 = 0.999
ADAM_EPS = 1e-08
ADAM_WD = 0.01
ADAM_STEP = 10

V7X_VMEM_LIMIT = 48 * 1024 * 1024

NN = ((1,), (0,))
NT = ((1,), (1,))
TN = ((0,), (0,))


def _bdot(a, b, dims):
    return lax.dot_general(a.astype(BF16), b.astype(BF16), (dims, ((), ())), preferred_element_type=F32)


def _hdot(a, b, dims=NN):
    return lax.dot_general(a, b, (dims, ((), ())), precision=HI, preferred_element_type=F32)


def _sigmoid(x):
    return 1.0 / (1.0 + jnp.exp(-x))


def _softplus(x):
    return jnp.maximum(x, 0.0) + jnp.log(1.0 + jnp.exp(-jnp.abs(x)))


def _params(dims):
    return pltpu.CompilerParams(dimension_semantics=dims, vmem_limit_bytes=V7X_VMEM_LIMIT)


def _rowsum(x):
    return jnp.sum(x, axis=-1, keepdims=True)


def _colsum(x):
    return jnp.sum(x, axis=0, keepdims=True)


def _matmul(name, mode, pairs, m, n, kdim, tm, tn, tk, out_dtypes, epi=None, extras=()):
    tm, tn, tk = min(tm, m), min(tn, n), min(tk, kdim)
    nk = kdim // tk
    assert m % tm == 0 and n % tn == 0 and kdim % tk == 0, (name, m, n, kdim, tm, tn, tk)
    in_specs, args = [], []
    for a, a_off, b, b_off in pairs:
        if mode == "nn":
            in_specs.append(pl.BlockSpec((tm, tk), lambda i, j, k, o=a_off: (i, k + o)))
            in_specs.append(pl.BlockSpec((tk, tn), lambda i, j, k, o=b_off: (k, j + o)))
            dims = NN
        elif mode == "nt":
            in_specs.append(pl.BlockSpec((tm, tk), lambda i, j, k, o=a_off: (i, k + o)))
            in_specs.append(pl.BlockSpec((tn, tk), lambda i, j, k, o=b_off: (j, k + o)))
            dims = NT
        else:
            in_specs.append(pl.BlockSpec((tk, tm), lambda i, j, k, o=a_off: (k, i + o)))
            in_specs.append(pl.BlockSpec((tk, tn), lambda i, j, k, o=b_off: (k, j + o)))
            dims = TN
        args += [a, b]
    for e, e_off in extras:
        in_specs.append(pl.BlockSpec((tm, tn), lambda i, j, k, o=e_off: (i, j + o)))
        args.append(e)
    npair, nex, nout = len(pairs), len(extras), len(out_dtypes)

    def body(*refs):
        prefs = refs[: 2 * npair]
        erefs = refs[2 * npair: 2 * npair + nex]
        orefs = refs[2 * npair + nex: 2 * npair + nex + nout]

        def finish(res):
            outs = (res,) if epi is None else epi(res, *[e[...] for e in erefs])
            for o, r in zip(orefs, outs):
                o[...] = r.astype(o.dtype)

        s = _bdot(prefs[0][...], prefs[1][...], dims)
        for p in range(1, npair):
            s = s + _bdot(prefs[2 * p][...], prefs[2 * p + 1][...], dims)
        if nk == 1:
            finish(s)
            return
        acc = refs[-1]
        k = pl.program_id(2)

        @pl.when(k == 0)
        def _():
            acc[...] = s

        @pl.when(k > 0)
        def _():
            acc[...] += s

        @pl.when(k == nk - 1)
        def _():
            finish(acc[...])

    out_shape, out_specs = [], []
    for od in out_dtypes:
        if isinstance(od, tuple):
            dt, full_w, blk_w, cblk = od
            assert n == tn
            out_shape.append(SDS((m, full_w), dt))
            out_specs.append(pl.BlockSpec((tm, blk_w), lambda i, j, k, c=cblk: (i, c)))
        else:
            out_shape.append(SDS((m, n), od))
            out_specs.append(pl.BlockSpec((tm, tn), lambda i, j, k: (i, j)))
    out_shape, out_specs = tuple(out_shape), tuple(out_specs)
    res = pl.pallas_call(
        body, grid=(m // tm, n // tn, nk), in_specs=in_specs, out_specs=out_specs, out_shape=out_shape,
        scratch_shapes=[pltpu.VMEM((tm, tn), F32)] if nk > 1 else [], name=name,
        compiler_params=_params(("parallel", "parallel", "arbitrary")),
    )(*args)
    return res if nout > 1 else res[0]


def _rmsnorm_fwd(name, x, w):
    t, d = x.shape
    tm = min(512, t)

    def body(x_ref, w_ref, h_ref):
        xv = x_ref[...]
        r = lax.rsqrt(jnp.mean(xv * xv, axis=-1, keepdims=True) + EPS)
        h_ref[...] = (xv * r * w_ref[...]).astype(BF16)

    return pl.pallas_call(
        body, grid=(t // tm,),
        in_specs=[pl.BlockSpec((tm, d), lambda i: (i, 0)), pl.BlockSpec((1, d), lambda i: (0, 0))],
        out_specs=pl.BlockSpec((tm, d), lambda i: (i, 0)), out_shape=SDS((t, d), BF16), name=name,
        compiler_params=_params(("parallel",)),
    )(x, w.reshape(1, d))


def _rmsnorm_bwd(name, x, w, dh, dres):
    t, d = x.shape
    tm = min(512, t)

    def body(x_ref, w_ref, dh_ref, dres_ref, dx_ref, dxb_ref, dw_ref):
        xv = x_ref[...]
        r = lax.rsqrt(jnp.mean(xv * xv, axis=-1, keepdims=True) + EPS)
        xh = xv * r
        dhv = dh_ref[...].astype(F32)
        dxh = dhv * w_ref[...]
        dx = r * (dxh - xh * jnp.mean(dxh * xh, axis=-1, keepdims=True)) + dres_ref[...]
        dx_ref[...] = dx
        dxb_ref[...] = dx.astype(BF16)

        @pl.when(pl.program_id(0) == 0)
        def _():
            dw_ref[...] = jnp.zeros_like(dw_ref)

        dw_ref[...] += _colsum(dhv * xh)

    row = pl.BlockSpec((tm, d), lambda i: (i, 0))
    vec = pl.BlockSpec((1, d), lambda i: (0, 0))
    return pl.pallas_call(
        body, grid=(t // tm,), in_specs=[row, vec, row, row], out_specs=(row, row, vec),
        out_shape=(SDS((t, d), F32), SDS((t, d), BF16), SDS((1, d), F32)), name=name,
        compiler_params=_params(("arbitrary",)),
    )(x, w.reshape(1, d), dh, dres)


def _loss_head(name, x, w, tgt):
    t, d = x.shape
    tm = min(512, t)

    def body(x_ref, w_ref, t_ref, loss_ref, dx_ref, dxb_ref, dw_ref):
        xv = x_ref[...]
        wv = w_ref[...]
        r = lax.rsqrt(jnp.mean(xv * xv, axis=-1, keepdims=True) + EPS)
        xh = xv * r
        e = xh * wv - t_ref[...]
        dy = e * (1.0 / d)
        dxh = dy * wv
        dx = r * (dxh - xh * jnp.mean(dxh * xh, axis=-1, keepdims=True))
        dx_ref[...] = dx
        dxb_ref[...] = dx.astype(BF16)

        @pl.when(pl.program_id(0) == 0)
        def _():
            dw_ref[...] = jnp.zeros_like(dw_ref)
            loss_ref[...] = jnp.zeros_like(loss_ref)

        dw_ref[...] += _colsum(dy * xh)
        loss_ref[...] += 0.5 * jnp.sum(jnp.mean(e * e, axis=-1, keepdims=True), axis=0, keepdims=True)

    row = pl.BlockSpec((tm, d), lambda i: (i, 0))
    vec = pl.BlockSpec((1, d), lambda i: (0, 0))
    return pl.pallas_call(
        body, grid=(t // tm,), in_specs=[row, vec, row],
        out_specs=(pl.BlockSpec((1, 1), lambda i: (0, 0)), row, row, vec),
        out_shape=(SDS((1, 1), F32), SDS((t, d), F32), SDS((t, d), BF16), SDS((1, d), F32)), name=name,
        compiler_params=_params(("arbitrary",)),
    )(x, w.reshape(1, d), tgt)


def _shift_down(u, s, row):
    return jnp.where(row >= s, pltpu.roll(u, shift=s, axis=0), 0.0)


def _conv_fwd(name, src, col0, w, b):
    t = src.shape[0]
    c = w.shape[1]
    tc = 256
    assert c % tc == 0 and col0 % tc == 0

    def body(u_ref, w_ref, b_ref, o_ref, pre_ref):
        u = u_ref[...]
        wv = w_ref[...]
        row = lax.broadcasted_iota(jnp.int32, u.shape, 0)
        pre = b_ref[...] + wv[3:4, :] * u
        for s in range(1, CONV_K):
            pre = pre + wv[3 - s: 4 - s, :] * _shift_down(u, s, row)
        pre_ref[...] = pre
        o_ref[...] = pre * _sigmoid(pre)

    col = pl.BlockSpec((t, tc), lambda j: (0, j))
    return pl.pallas_call(
        body, grid=(c // tc,),
        in_specs=[pl.BlockSpec((t, tc), lambda j: (0, j + col0 // tc)), pl.BlockSpec((CONV_K, tc), lambda j: (0, j)),
                  pl.BlockSpec((1, tc), lambda j: (0, j))],
        out_specs=(col, col), out_shape=(SDS((t, c), F32), SDS((t, c), F32)), name=name,
        compiler_params=_params(("parallel",)),
    )(src, w, b)


def _place_small(name, dsm_a, dsm_b, dproj):
    t = dsm_a.shape[0]
    width = PROJ_W - SM_OFF
    tr = min(512, t)

    def body(a_ref, b_ref, dproj_ref, o_ref):
        o_ref[:, :128] = a_ref[...] + b_ref[...]
        o_ref[:, 128:] = jnp.zeros((tr, width - 128), BF16)

    row = pl.BlockSpec((tr, 128), lambda i: (i, 0))
    return pl.pallas_call(
        body, grid=(t // tr,), in_specs=[row, row, ANY],
        out_specs=pl.BlockSpec((tr, width), lambda i: (i, SM_OFF // width)), out_shape=SDS(dproj.shape, BF16),
        input_output_aliases={2: 0}, name=name, compiler_params=_params(("parallel",)),
    )(dsm_a, dsm_b, dproj)


def _conv_bwd(name, src, col0, w, pre, dact, dproj):
    t = src.shape[0]
    c = w.shape[1]
    tc = 128

    def body(u_ref, w_ref, pre_ref, da_ref, dproj_ref, du_ref, dw_ref, db_ref):
        u = u_ref[...]
        wv = w_ref[...]
        prev = pre_ref[...]
        row = lax.broadcasted_iota(jnp.int32, u.shape, 0)
        sg = _sigmoid(prev)
        dpre = da_ref[...] * (sg * (1.0 + prev * (1.0 - sg)))
        du = wv[3:4, :] * dpre
        dw_ref[3:4, :] = _colsum(dpre * u)
        for s in range(1, CONV_K):
            up = jnp.where(row < t - s, pltpu.roll(dpre, shift=t - s, axis=0), 0.0)
            du = du + wv[3 - s: 4 - s, :] * up
            dw_ref[3 - s: 4 - s, :] = _colsum(up * u)
        du_ref[...] = du.astype(BF16)
        db_ref[...] = _colsum(dpre)

    col = pl.BlockSpec((t, tc), lambda j: (0, j))
    return pl.pallas_call(
        body, grid=(c // tc,),
        in_specs=[pl.BlockSpec((t, tc), lambda j: (0, j + col0 // tc)), pl.BlockSpec((CONV_K, tc), lambda j: (0, j)),
                  col, col, ANY],
        out_specs=(pl.BlockSpec((t, tc), lambda j: (0, j + col0 // tc)), pl.BlockSpec((CONV_K, tc), lambda j: (0, j)),
                   pl.BlockSpec((1, tc), lambda j: (0, j))),
        out_shape=(SDS(dproj.shape, BF16), SDS((CONV_K, c), F32), SDS((1, c), F32)), name=name,
        input_output_aliases={4: 0},
        compiler_params=_params(("parallel",)),
    )(src, w, pre, dact, dproj)


def _tri(q):
    ii = lax.broadcasted_iota(jnp.int32, (q, q), 0)
    jj = lax.broadcasted_iota(jnp.int32, (q, q), 1)
    return ii, jj


def _dot01(x, r01, dims, terms=3):
    out, rem = None, x
    for i in range(terms):
        hi = rem.astype(BF16)
        d = lax.dot_general(hi, r01, (dims, ((), ())), preferred_element_type=F32)
        out = d if out is None else out + d
        if i + 1 < terms:
            rem = rem - hi.astype(F32)
    return out


def _ssd_common(act, sm, dtb, arow, rmat):
    q = CHUNK
    ii, jj = _tri(q)
    lane = lax.broadcasted_iota(jnp.int32, (q, 128), 1)
    m16 = lane < SSM_HEADS
    dt = jnp.where(m16, _softplus(sm + dtb), 0.0)
    a = dt * arow
    tril = (ii >= jj).astype(F32)
    triu = (ii <= jj).astype(F32)
    acum = _hdot(tril, a)
    acum_r = _hdot(a.T, triu)
    dtx = _dot01(dt, rmat, NN)
    acx = _dot01(acum, rmat, NN)
    ex = jnp.exp(acx)
    alx = acx[q - 1: q, :]
    dex = jnp.exp(alx - acx)
    xs = act[:, :1024]
    return dict(ii=ii, jj=jj, m16=m16, dt=dt, a=a, triu=triu, acum=acum, acum_r=acum_r, dtx=dtx, ex=ex, dex=dex,
                elx=jnp.exp(alx), xs=xs, x=xs * dtx)


def _ssd_lmat(cm, h):
    return jnp.where(cm["ii"] >= cm["jj"], jnp.exp(cm["acum"][:, h: h + 1] - cm["acum_r"][h: h + 1, :]), 0.0)


def _ssd_fwd(name, act, proj, dtb, arow, dxrow, nw, rmat):
    t = act.shape[0]
    q = CHUNK
    nc = t // q
    hg = SSM_HEADS // SSM_GROUPS
    gw = hg * SSM_P

    def body(act_ref, z_ref, sm_ref, dtb_ref, arow_ref, dx_ref, nw_ref, r_ref, y_ref, ys_ref, st_ref, s_scr, yd_scr):
        @pl.when(pl.program_id(0) == 0)
        def _():
            s_scr[...] = jnp.zeros_like(s_scr)

        s_all = s_scr[...]
        for sub in range(cps):
            rows = pl.ds(sub * q, q)
            s_all = chunk(act_ref.at[rows, :], z_ref.at[rows, :], sm_ref.at[rows, :], dtb_ref, arow_ref, dx_ref, nw_ref, r_ref,
                          y_ref.at[rows, :], ys_ref.at[rows, :], st_ref.at[sub], yd_scr.at[rows, :], s_all)
        s_scr[...] = s_all

    def chunk(act_ref, z_ref, sm_ref, dtb_ref, arow_ref, dx_ref, nw_ref, r_ref, y_ref, ys_ref, st_ref, yd_scr, s_all):
        st_ref[...] = s_all
        actv = act_ref[...]
        cm = _ssd_common(actv, sm_ref[...], dtb_ref[...], arow_ref[...], r_ref[...])
        x = cm["x"]
        xd = x * cm["dex"]
        yoffs, snew = [], []
        for g in range(SSM_GROUPS):
            bg = actv[:, 1024 + g * SSM_N: 1024 + (g + 1) * SSM_N]
            cg = actv[:, 1280 + g * SSM_N: 1280 + (g + 1) * SSM_N]
            sg = s_all[:, g * gw: (g + 1) * gw]
            cb = _bdot(cg, bg, NT)
            yoffs.append(_bdot(cg, sg, NN))
            snew.append(_bdot(bg, xd[:, g * gw: (g + 1) * gw], TN))
            for r in range(hg):
                h = g * hg + r
                mm = cb * _ssd_lmat(cm, h)
                yd_scr[:, h * SSM_P: (h + 1) * SSM_P] = _bdot(mm, x[:, h * SSM_P: (h + 1) * SSM_P], NN)
        s_next = s_all * cm["elx"] + jnp.concatenate(snew, axis=1)
        ysc = yd_scr[...] + jnp.concatenate(yoffs, axis=1) * cm["ex"]
        ys_ref[...] = ysc
        zv = z_ref[...]
        yg = (ysc + dx_ref[...] * cm["xs"]) * (zv * _sigmoid(zv))
        nwv = nw_ref[...]
        for g in range(SSM_GROUPS):
            sl = yg[:, g * gw: (g + 1) * gw]
            rr = lax.rsqrt(jnp.mean(sl * sl, axis=-1, keepdims=True) + EPS)
            y_ref[:, g * gw: (g + 1) * gw] = (sl * rr * nwv[:, g * gw: (g + 1) * gw]).astype(BF16)
        return s_next

    cps = SCAN_CHUNKS_PER_STEP if nc % SCAN_CHUNKS_PER_STEP == 0 else 1
    qq = cps * q
    vec128 = pl.BlockSpec((1, 128), lambda c: (0, 0))
    vec1k = pl.BlockSpec((1, 1024), lambda c: (0, 0))
    return pl.pallas_call(
        body, grid=(nc // cps,),
        in_specs=[pl.BlockSpec((qq, SSM_CONV), lambda c: (c, 0)), pl.BlockSpec((qq, 1024), lambda c: (c, Z_OFF // 1024)),
                  pl.BlockSpec((qq, 128), lambda c: (c, SM_OFF // 128)), vec128, vec128, vec1k, vec1k,
                  pl.BlockSpec((128, 1024), lambda c: (0, 0))],
        out_specs=(pl.BlockSpec((qq, 1024), lambda c: (c, 0)), pl.BlockSpec((qq, 1024), lambda c: (c, 0)),
                   pl.BlockSpec((cps, 128, 1024), lambda c: (c, 0, 0))),
        out_shape=(SDS((t, 1024), BF16), SDS((t, 1024), F32), SDS((nc, 128, 1024), F32)),
        scratch_shapes=[pltpu.VMEM((128, 1024), F32), pltpu.VMEM((qq, 1024), F32)], name=name,
        compiler_params=_params(("arbitrary",)),
    )(act, proj, proj, dtb, arow, dxrow, nw, rmat)


def _ssd_bwd(name, act, proj, dtb, arow, dxrow, nw, rmat, ysc, states, dy, dproj):
    t = act.shape[0]
    q = CHUNK
    nc = t // q
    hg = SSM_HEADS // SSM_GROUPS
    gw = hg * SSM_P

    def body(act_ref, z_ref, sm_ref, dtb_ref, arow_ref, dx_ref, nw_ref, r_ref, ys_ref, st_ref, dy_ref, dproj_ref,
             dact_ref, dz_ref, dsm_ref, dnw_ref, dd_ref, dal_ref, ddtb_ref, ds_scr, dxd_scr):
        @pl.when(pl.program_id(0) == 0)
        def _():
            ds_scr[...] = jnp.zeros_like(ds_scr)
            dnw_ref[...] = jnp.zeros_like(dnw_ref)
            dd_ref[...] = jnp.zeros_like(dd_ref)
            dal_ref[...] = jnp.zeros_like(dal_ref)
            ddtb_ref[...] = jnp.zeros_like(ddtb_ref)

        dsn = ds_scr[...]
        for sub in reversed(range(cps)):
            rows = pl.ds(sub * q, q)
            dsn = chunk(act_ref.at[rows, :], z_ref.at[rows, :], sm_ref.at[rows, :], dtb_ref, arow_ref, dx_ref, nw_ref, r_ref,
                        ys_ref.at[rows, :], st_ref.at[sub], dy_ref.at[rows, :], dact_ref.at[rows, :], dz_ref.at[rows, :],
                        dsm_ref.at[rows, :], dnw_ref, dd_ref, dal_ref, ddtb_ref, dxd_scr.at[rows, :], dsn)
        ds_scr[...] = dsn

    def chunk(act_ref, z_ref, sm_ref, dtb_ref, arow_ref, dx_ref, nw_ref, r_ref, ys_ref, st_ref, dy_ref,
              dact_ref, dz_ref, dsm_ref, dnw_ref, dd_ref, dal_ref, ddtb_ref, dxd_scr, dsn):
        actv = act_ref[...]
        smv = sm_ref[...]
        rmat_v = r_ref[...]
        cm = _ssd_common(actv, smv, dtb_ref[...], arow_ref[...], rmat_v)
        ii, jj = cm["ii"], cm["jj"]
        x, xs = cm["x"], cm["xs"]
        s_all = st_ref[...]
        ysv = ys_ref[...]
        dxr = dx_ref[...]
        y = ysv + dxr * xs
        zv = z_ref[...]
        sz = _sigmoid(zv)
        silz = zv * sz
        yg = y * silz
        dout = dy_ref[...]
        nwv = nw_ref[...]
        dyn = dout * nwv
        yn_parts, dyg_parts = [], []
        for g in range(SSM_GROUPS):
            sl = yg[:, g * gw: (g + 1) * gw]
            rr = lax.rsqrt(jnp.mean(sl * sl, axis=-1, keepdims=True) + EPS)
            yn = sl * rr
            dn = dyn[:, g * gw: (g + 1) * gw]
            yn_parts.append(yn)
            dyg_parts.append(rr * (dn - yn * jnp.mean(dn * yn, axis=-1, keepdims=True)))
        dnw_ref[...] += _colsum(dout * jnp.concatenate(yn_parts, axis=1))
        dyg = jnp.concatenate(dyg_parts, axis=1)
        dyv = dyg * silz
        dz_ref[...] = (dyg * y * (sz * (1.0 + zv * (1.0 - sz)))).astype(BF16)
        dd_ref[...] += _dot01(_colsum(dyv * xs), rmat_v, NT)
        dxs = dyv * dxr
        dcs = dyv * cm["ex"]
        xd = x * cm["dex"]
        dxst_parts, ds_parts, db_parts, dc_parts, yoff_parts, wcol_rows = [], [], [], [], [], []
        lane128 = lax.broadcasted_iota(jnp.int32, (q, 128), 1)
        wrow = jnp.zeros((q, 128), F32)
        for g in range(SSM_GROUPS):
            bg = actv[:, 1024 + g * SSM_N: 1024 + (g + 1) * SSM_N]
            cg = actv[:, 1280 + g * SSM_N: 1280 + (g + 1) * SSM_N]
            sg = s_all[:, g * gw: (g + 1) * gw]
            dsng = dsn[:, g * gw: (g + 1) * gw]
            dcsg = dcs[:, g * gw: (g + 1) * gw]
            dcg = _bdot(dcsg, sg, NT)
            yoff_parts.append(_bdot(cg, sg, NN))
            ds_parts.append(_bdot(cg, dcsg, TN))
            dxst_parts.append(_bdot(bg, dsng, NN))
            dbg = _bdot(xd[:, g * gw: (g + 1) * gw], dsng, NT)
            cb = _bdot(cg, bg, NT)
            dcb = jnp.zeros((q, q), F32)
            for r in range(hg):
                h = g * hg + r
                lm = _ssd_lmat(cm, h)
                mm = cb * lm
                dyh = dyv[:, h * SSM_P: (h + 1) * SSM_P]
                dm = jnp.where(ii >= jj, _bdot(dyh, x[:, h * SSM_P: (h + 1) * SSM_P], NT), 0.0)
                dxd_scr[:, h * SSM_P: (h + 1) * SSM_P] = _bdot(mm, dyh, TN)
                dcb = dcb + dm * lm
                wm = dm * mm
                wrow = wrow + jnp.where(lane128 == h, _rowsum(wm), 0.0)
                wcol_rows.append(_colsum(wm))
            dc_parts.append(dcg + _bdot(dcb, bg, NN))
            db_parts.append(dbg + _bdot(dcb, cg, TN))
        dxst = jnp.concatenate(dxst_parts, axis=1) * cm["dex"]
        dx = dxd_scr[...] + dxst
        ds_prev = jnp.concatenate(ds_parts, axis=1) + dsn * cm["elx"]
        wcol = jnp.concatenate(wcol_rows + [jnp.zeros((128 - SSM_HEADS, q), F32)], axis=0).T
        yoff = jnp.concatenate(yoff_parts, axis=1) * cm["ex"]
        xdxst = x * dxst
        dac = wrow - wcol + _dot01(dyv * yoff - xdxst, rmat_v, NT)
        last = _dot01(_colsum(dsn * s_all) * cm["elx"] + _colsum(xdxst), rmat_v, NT)
        rowq = lax.broadcasted_iota(jnp.int32, (q, 128), 0)
        dac = dac + jnp.where(rowq == q - 1, last, 0.0)
        da = _hdot(cm["triu"], dac)
        arow_v = arow_ref[...]
        ddt = da * arow_v + _dot01(dx * xs, rmat_v, NT)
        dxs = dxs + dx * cm["dtx"]
        dal_ref[...] += _colsum(da * cm["a"])
        ddtraw = jnp.where(cm["m16"], ddt * _sigmoid(smv + dtb_ref[...]), 0.0)
        ddtb_ref[...] += _colsum(ddtraw)
        dsm_ref[...] = ddtraw.astype(BF16)
        dact_ref[:, :1024] = dxs
        for g in range(SSM_GROUPS):
            dact_ref[:, 1024 + g * SSM_N: 1024 + (g + 1) * SSM_N] = db_parts[g]
            dact_ref[:, 1280 + g * SSM_N: 1280 + (g + 1) * SSM_N] = dc_parts[g]
        return ds_prev

    cps = SCAN_CHUNKS_PER_STEP if nc % SCAN_CHUNKS_PER_STEP == 0 else 1
    qq = cps * q
    rev = lambda c: nc // cps - 1 - c
    vec128 = pl.BlockSpec((1, 128), lambda c: (0, 0))
    vec1k = pl.BlockSpec((1, 1024), lambda c: (0, 0))
    return pl.pallas_call(
        body, grid=(nc // cps,),
        in_specs=[pl.BlockSpec((qq, SSM_CONV), lambda c: (rev(c), 0)),
                  pl.BlockSpec((qq, 1024), lambda c: (rev(c), Z_OFF // 1024)),
                  pl.BlockSpec((qq, 128), lambda c: (rev(c), SM_OFF // 128)), vec128, vec128, vec1k, vec1k,
                  pl.BlockSpec((128, 1024), lambda c: (0, 0)),
                  pl.BlockSpec((qq, 1024), lambda c: (rev(c), 0)), pl.BlockSpec((cps, 128, 1024), lambda c: (rev(c), 0, 0)),
                  pl.BlockSpec((qq, 1024), lambda c: (rev(c), 0)), ANY],
        out_specs=(pl.BlockSpec((qq, SSM_CONV), lambda c: (rev(c), 0)),
                   pl.BlockSpec((qq, 1024), lambda c: (rev(c), Z_OFF // 1024)),
                   pl.BlockSpec((qq, 128), lambda c: (rev(c), 0)), vec1k, vec128, vec128, vec128),
        out_shape=(SDS((t, SSM_CONV), F32), SDS(dproj.shape, BF16), SDS((t, 128), BF16), SDS((1, 1024), F32),
                   SDS((1, 128), F32), SDS((1, 128), F32), SDS((1, 128), F32)),
        input_output_aliases={11: 1},
        scratch_shapes=[pltpu.VMEM((128, 1024), F32), pltpu.VMEM((qq, 1024), F32)], name=name,
        compiler_params=_params(("arbitrary",)),
    )(act, proj, proj, dtb, arow, dxrow, nw, rmat, ysc, states, dy, dproj)


def _split(a):
    hi = a.astype(BF16)
    return hi, (a - hi.astype(F32)).astype(BF16)


def _dot3(a, b, dims=NN):
    (ah, al), (bh, bl) = a, b

    def d(x, y):
        return lax.dot_general(x, y, (dims, ((), ())), preferred_element_type=F32)

    return d(ah, bh) + (d(ah, bl) + d(al, bh))


def _tri_inverses(amats, ii, jj):
    eye = jnp.where(ii == jj, 1.0, 0.0)
    tms = [eye - a for a in amats]
    sp = [_split(a) for a in amats]
    for _ in range(5):
        sp = [_split(_dot3(s, s)) for s in sp]
        tms = [t + _dot3(_split(t), s) for t, s in zip(tms, sp)]
    return tms


def _gdn_common(sm, gb, garow):
    q = CHUNK
    ii, jj = _tri(q)
    lane = lax.broadcasted_iota(jnp.int32, (q, 128), 1)
    ma = (lane >= LANE_A) & (lane < LANE_A + GDN_HEADS)
    spre = sm + gb
    g = jnp.where(ma, garow * _softplus(spre), 0.0)
    beta = _sigmoid(sm)
    tril = (ii >= jj).astype(F32)
    triu = (ii <= jj).astype(F32)
    gc = _hdot(tril, g)
    gc_r = _hdot(g.T, triu)
    return dict(ii=ii, jj=jj, lane=lane, ma=ma, spre=spre, g=g, beta=beta, triu=triu, gc=gc, gc_r=gc_r)


def _each(f, *lists):
    return [f(*xs) for xs in zip(*lists)]


GDN_SCALE = GDN_DK ** -0.5
GDN_BWD_HEAD_GROUPS = (range(GDN_HEADS),)


def _gdn_heads(cm, actv, states, heads=range(GDN_HEADS)):
    q = CHUNK
    ii, jj = cm["ii"], cm["jj"]
    qr = [actv[:, h * 128: (h + 1) * 128] for h in heads]
    kr = [actv[:, 1024 + h * 128: 1024 + (h + 1) * 128] for h in heads]
    v = [actv[:, 2048 + h * 128: 2048 + (h + 1) * 128] for h in heads]
    rq = _each(lambda x: lax.rsqrt(_rowsum(x * x) + EPS), qr)
    rk = _each(lambda x: lax.rsqrt(_rowsum(x * x) + EPS), kr)
    qn = _each(lambda x, r: x * r * GDN_SCALE, qr, rq)
    kn = _each(lambda x, r: x * r, kr, rk)
    gcc = [cm["gc"][:, LANE_A + h: LANE_A + h + 1] for h in heads]
    gcr = [cm["gc_r"][LANE_A + h: LANE_A + h + 1, :] for h in heads]
    bcol = [cm["beta"][:, LANE_B + h: LANE_B + h + 1] for h in heads]
    dm = _each(lambda c, r: jnp.where(ii >= jj, jnp.exp(c - r), 0.0), gcc, gcr)
    kq = _each(lambda k, a: _bdot(jnp.concatenate([k, a], axis=0), k, NT), kn, qn)
    ak = _each(lambda x, d: jnp.where(ii > jj, x[:q] * d, 0.0), kq, dm)
    qkm = _each(lambda x, d: jnp.where(ii >= jj, x[q:] * d, 0.0), kq, dm)
    tm = _tri_inverses(_each(lambda a, b: a * b, ak, bcol), ii, jj)
    eg = _each(jnp.exp, gcc)
    gl = [c[q - 1: q, :] for c in gcc]
    rm = _each(lambda vv, k, b, e: jnp.concatenate([vv * b, k * (b * e)], axis=1), v, kn, bcol, eg)
    tt = _each(lambda t, r: _dot3(_split(t), _split(r)), tm, rm)
    w = [t[:, 128:] for t in tt]
    qg = _each(lambda a, e: a * e, qn, eg)
    ws = _each(lambda ww, a, s: _bdot(jnp.concatenate([ww, a], axis=0), s, NN), w, qg, states)
    vnew = _each(lambda t, x: t[:, :128] - x[:q], tt, ws)
    return dict(qr=qr, v=v, rq=rq, rk=rk, qn=qn, kn=kn, gcc=gcc, bcol=bcol, dm=dm, ak=ak, tm=tm, eg=eg, gl=gl,
                egl=_each(jnp.exp, gl), ed=_each(lambda g, c: jnp.exp(g - c), gl, gcc), tt=tt, w=w, vnew=vnew, qkm=qkm,
                qg=qg, qgs=[x[q:] for x in ws])


def _gdn_fwd(name, act, proj, gb, garow, gnw):
    t = act.shape[0]
    q = CHUNK
    nc = t // q

    def body(act_ref, gz_ref, sm_ref, gb_ref, ga_ref, nw_ref, y_ref, o_ref, st_ref, s_scr):
        @pl.when(pl.program_id(0) == 0)
        def _():
            s_scr[...] = jnp.zeros_like(s_scr)

        states = [s_scr[h * 128: (h + 1) * 128, :] for h in range(GDN_HEADS)]
        for sub in range(cps):
            rows = pl.ds(sub * q, q)
            states = chunk(act_ref.at[rows, :], gz_ref.at[rows, :], sm_ref.at[rows, :], gb_ref, ga_ref, nw_ref,
                           y_ref.at[rows, :], o_ref.at[rows, :], st_ref.at[sub], states)
        for h in range(GDN_HEADS):
            s_scr[h * 128: (h + 1) * 128, :] = states[h]

    def chunk(act_ref, gz_ref, sm_ref, gb_ref, ga_ref, nw_ref, y_ref, o_ref, st_ref, states):
        for h in range(GDN_HEADS):
            st_ref[h * 128: (h + 1) * 128, :] = states[h]
        actv = act_ref[...]
        cm = _gdn_common(sm_ref[...], gb_ref[...], ga_ref[...])
        nwv = nw_ref[...]
        gzv = gz_ref[...]
        hd = _gdn_heads(cm, actv, states)
        outs = _each(lambda qs, m, vn: qs + _bdot(m, vn, NN), hd["qgs"], hd["qkm"], hd["vnew"])
        snew = _each(lambda s, e, k, d, vn: s * e + _bdot(k * d, vn, TN), states, hd["egl"], hd["kn"], hd["ed"], hd["vnew"])
        for h in range(GDN_HEADS):
            o = outs[h]
            o_ref[:, h * 128: (h + 1) * 128] = o
            rr = lax.rsqrt(jnp.mean(o * o, axis=-1, keepdims=True) + EPS)
            gz = gzv[:, h * 128: (h + 1) * 128]
            y_ref[:, h * 128: (h + 1) * 128] = (o * rr * nwv * (gz * _sigmoid(gz))).astype(BF16)
        return snew

    cps = SCAN_CHUNKS_PER_STEP if nc % SCAN_CHUNKS_PER_STEP == 0 else 1
    qq = cps * q
    vec128 = pl.BlockSpec((1, 128), lambda c: (0, 0))
    return pl.pallas_call(
        body, grid=(nc // cps,),
        in_specs=[pl.BlockSpec((qq, GDN_QKV), lambda c: (c, 0)), pl.BlockSpec((qq, 1024), lambda c: (c, GZ_OFF // 1024)),
                  pl.BlockSpec((qq, 128), lambda c: (c, SM_OFF // 128)), vec128, vec128, vec128],
        out_specs=(pl.BlockSpec((qq, 1024), lambda c: (c, 0)), pl.BlockSpec((qq, 1024), lambda c: (c, 0)),
                   pl.BlockSpec((cps, 1024, 128), lambda c: (c, 0, 0))),
        out_shape=(SDS((t, 1024), BF16), SDS((t, 1024), F32), SDS((nc, 1024, 128), F32)),
        scratch_shapes=[pltpu.VMEM((1024, 128), F32)], name=name, compiler_params=_params(("arbitrary",)),
    )(act, proj, proj, gb, garow, gnw)


def _gdn_bwd(name, act, proj, gb, garow, gnw, oraw, states, dy, dproj):
    t = act.shape[0]
    q = CHUNK
    nc = t // q

    def body(act_ref, gz_ref, sm_ref, gb_ref, ga_ref, nw_ref, o_ref, st_ref, dy_ref, dproj_ref,
             dact_ref, dgz_ref, dsm_ref, dnw_ref, dal_ref, dgb_ref, ds_scr):
        @pl.when(pl.program_id(0) == 0)
        def _():
            ds_scr[...] = jnp.zeros_like(ds_scr)
            dnw_ref[...] = jnp.zeros_like(dnw_ref)
            dal_ref[...] = jnp.zeros_like(dal_ref)
            dgb_ref[...] = jnp.zeros_like(dgb_ref)

        dsn = [ds_scr[h * 128: (h + 1) * 128, :] for h in range(GDN_HEADS)]
        for sub in reversed(range(cps)):
            rows = pl.ds(sub * q, q)
            dsn = chunk(act_ref.at[rows, :], gz_ref.at[rows, :], sm_ref.at[rows, :], gb_ref, ga_ref, nw_ref,
                        o_ref.at[rows, :], st_ref.at[sub], dy_ref.at[rows, :],
                        dact_ref.at[rows, :], dgz_ref.at[rows, :], dsm_ref.at[rows, :], dnw_ref, dal_ref, dgb_ref, dsn)
        for h in range(GDN_HEADS):
            ds_scr[h * 128: (h + 1) * 128, :] = dsn[h]

    def chunk(act_ref, gz_ref, sm_ref, gb_ref, ga_ref, nw_ref, o_ref, st_ref, dy_ref,
              dact_ref, dgz_ref, dsm_ref, dnw_ref, dal_ref, dgb_ref, dsn):
        actv = act_ref[...]
        smv = sm_ref[...]
        garow_v = ga_ref[...]
        cm = _gdn_common(smv, gb_ref[...], garow_v)
        ii, jj, lane = cm["ii"], cm["jj"], cm["lane"]
        nwv = nw_ref[...]
        rowq = lax.broadcasted_iota(jnp.int32, (q, 1), 0)
        dgc_all = jnp.zeros((q, 128), F32)
        dbeta_all = jnp.zeros((q, 128), F32)
        dnw_acc = jnp.zeros((1, 128), F32)
        ds_out = []
        ov, gzv, dyv = o_ref[...], gz_ref[...], dy_ref[...]
        for heads in GDN_BWD_HEAD_GROUPS:
            part = group(heads, cm, actv, ov, gzv, dyv, nwv, rowq, st_ref, [dsn[h] for h in heads], dact_ref, dgz_ref)
            ds_out += part[0]
            dgc_all, dbeta_all, dnw_acc = dgc_all + part[1], dbeta_all + part[2], dnw_acc + part[3]
        dnw_ref[...] += dnw_acc
        dg = _hdot(cm["triu"], dgc_all)
        da_raw = jnp.where(cm["ma"], dg * garow_v * _sigmoid(cm["spre"]), 0.0)
        dal_ref[...] += _colsum(dg * cm["g"])
        dgb_ref[...] += _colsum(da_raw)
        beta = cm["beta"]
        dsm_ref[...] = (da_raw + dbeta_all * beta * (1.0 - beta)).astype(BF16)
        return ds_out

    def group(heads, cm, actv, ov, gzv, dyv, nwv, rowq, st_ref, dsn, dact_ref, dgz_ref):
        ii, jj, lane = cm["ii"], cm["jj"], cm["lane"]
        dgc_all = jnp.zeros((q, 128), F32)
        dbeta_all = jnp.zeros((q, 128), F32)
        dnw_acc = jnp.zeros((1, 128), F32)
        ds_out = []
        sts = [st_ref[h * 128: (h + 1) * 128, :] for h in heads]
        hd = _gdn_heads(cm, actv, sts, heads)
        qn, kn, v, eg, ed, egl, bcol = hd["qn"], hd["kn"], hd["v"], hd["eg"], hd["ed"], hd["egl"], hd["bcol"]
        vnew, qkm, qg, w, tt, dm, ak = hd["vnew"], hd["qkm"], hd["qg"], hd["w"], hd["tt"], hd["dm"], hd["ak"]
        do = []
        for h in heads:
            hs = slice(h * 128, (h + 1) * 128)
            o = ov[:, hs]
            rr = lax.rsqrt(jnp.mean(o * o, axis=-1, keepdims=True) + EPS)
            on = o * rr
            gz = gzv[:, hs]
            sz = _sigmoid(gz)
            silz = gz * sz
            dyh = dyv[:, hs]
            dnw_acc = dnw_acc + _colsum(dyh * on * silz)
            dgz_ref[:, hs] = (dyh * on * nwv * (sz * (1.0 + gz * (1.0 - sz)))).astype(BF16)
            don = dyh * nwv * silz
            do.append(rr * (don - on * jnp.mean(don * on, axis=-1, keepdims=True)))
        kd = _each(lambda k, e: k * e, kn, ed)
        dkd = _each(lambda vn, d: _bdot(vn, d, NT), vnew, dsn)
        dvnew_a = _each(lambda k, d: _bdot(k, d, NN), kd, dsn)
        ded = _each(lambda a, b: _rowsum(a * b), dkd, kd)
        dgl = _each(lambda d, s, e, de: jnp.sum(_rowsum(d * s), axis=0, keepdims=True) * e + _colsum(de), dsn, sts, egl, ded)
        dqk = _each(lambda d, vn: jnp.where(ii >= jj, _bdot(d, vn, NT), 0.0), do, vnew)
        dvnew = _each(lambda a, m, d: a + _bdot(m, d, TN), dvnew_a, qkm, do)
        pq = _each(lambda a, b: a * b, dqk, dm)
        w1 = _each(lambda a, b: a * b, dqk, qkm)
        dod = _each(lambda a, b: jnp.concatenate([a, b], axis=0), do, dvnew)
        dos = _each(lambda x, s: _bdot(x, s, NT), dod, sts)
        dqg = [x[:q] for x in dos]
        dw = [-x[q:] for x in dos]
        ds12 = _each(lambda a, ww, x: _bdot(jnp.concatenate([a, -ww], axis=0), x, TN), qg, w, dod)
        dr = _each(lambda t, a, b: _dot3(_split(t), _split(jnp.concatenate([a, b], axis=1)), TN), hd["tm"], dvnew, dw)
        da = _each(lambda r, t: jnp.where(ii > jj, -_dot3(_split(r), _split(t), NT), 0.0), dr, tt)
        sk = _each(lambda r, k: _rowsum(r[:, 128:] * k), dr, kn)
        pk = _each(lambda a, d, b: a * d * b, da, dm, bcol)
        pkn = _each(lambda p, pp, k: _bdot(jnp.concatenate([p, pp + pp.T], axis=0), k, NN), pq, pk, kn)
        dq = _each(lambda a, e, x: a * e + x[:q], dqg, eg, pkn)
        dk = _each(lambda a, e, p, x, r, b, eg_, y: a * e + _bdot(p, x, TN) + r[:, 128:] * (b * eg_) + y[q:],
                   dkd, ed, pq, qn, dr, bcol, eg, pkn)
        w2 = _each(lambda a, k, b: a * (k * b), da, ak, bcol)
        for i, h in enumerate(heads):
            hs = slice(h * 128, (h + 1) * 128)
            dgc = (-ded[i] + _rowsum(dqg[i] * qg[i]) + _rowsum(w1[i]) - _rowsum(w1[i].T) + sk[i] * bcol[i] * eg[i]
                   + _rowsum(w2[i]) - _rowsum(w2[i].T) + jnp.where(rowq == q - 1, dgl[i], 0.0))
            dbeta = _rowsum(dr[i][:, :128] * v[i]) + sk[i] * eg[i] + _rowsum(da[i] * ak[i])
            qhat = hd["qr"][i] * hd["rq"][i]
            dqhat = dq[i] * GDN_SCALE
            dact_ref[:, hs] = hd["rq"][i] * (dqhat - qhat * _rowsum(dqhat * qhat))
            dact_ref[:, 1024 + h * 128: 1024 + (h + 1) * 128] = hd["rk"][i] * (dk[i] - kn[i] * _rowsum(dk[i] * kn[i]))
            dact_ref[:, 2048 + h * 128: 2048 + (h + 1) * 128] = dr[i][:, :128] * bcol[i]
            dgc_all = dgc_all + jnp.where(lane == LANE_A + h, dgc, 0.0)
            dbeta_all = dbeta_all + jnp.where(lane == LANE_B + h, dbeta, 0.0)
            ds_out.append(dsn[i] * egl[i] + ds12[i])
        return ds_out, dgc_all, dbeta_all, dnw_acc

    cps = 2 if nc % 2 == 0 else 1
    qq = cps * q
    rev = lambda c: nc // cps - 1 - c
    vec128 = pl.BlockSpec((1, 128), lambda c: (0, 0))
    return pl.pallas_call(
        body, grid=(nc // cps,),
        in_specs=[pl.BlockSpec((qq, GDN_QKV), lambda c: (rev(c), 0)),
                  pl.BlockSpec((qq, 1024), lambda c: (rev(c), GZ_OFF // 1024)),
                  pl.BlockSpec((qq, 128), lambda c: (rev(c), SM_OFF // 128)), vec128, vec128, vec128,
                  pl.BlockSpec((qq, 1024), lambda c: (rev(c), 0)), pl.BlockSpec((cps, 1024, 128), lambda c: (rev(c), 0, 0)),
                  pl.BlockSpec((qq, 1024), lambda c: (rev(c), 0)), ANY],
        out_specs=(pl.BlockSpec((qq, GDN_QKV), lambda c: (rev(c), 0)),
                   pl.BlockSpec((qq, 1024), lambda c: (rev(c), GZ_OFF // 1024)),
                   pl.BlockSpec((qq, 128), lambda c: (rev(c), 0)), vec128, vec128, vec128),
        out_shape=(SDS((t, GDN_QKV), F32), SDS(dproj.shape, BF16), SDS((t, 128), BF16), SDS((1, 128), F32),
                   SDS((1, 128), F32), SDS((1, 128), F32)),
        input_output_aliases={9: 1},
        scratch_shapes=[pltpu.VMEM((1024, 128), F32)], name=name, compiler_params=_params(("arbitrary",)),
    )(act, proj, proj, gb, garow, gnw, oraw, states, dy, dproj)


def _row_tile(r):
    for cand in (512, 256, 128, 64, 32, 16, 8):
        if r % cand == 0:
            return cand
    return r


def _sum_terms(name, terms, out_dtype):
    shape = terms[0][0].shape[1:]
    c = shape[-1]
    r = 1
    for s in shape[:-1]:
        r *= s
    tr = min(_row_tile(r), 256)
    n = len(terms)

    def body(*refs):
        acc = refs[0][...].astype(F32)
        for k in range(1, n):
            acc = acc + refs[k][...].astype(F32)
        refs[n][...] = acc.astype(out_dtype)

    in_specs = [pl.BlockSpec((None, tr, c), lambda i, q=lead: (q, i, 0)) for _, lead in terms]
    args = [a.reshape(a.shape[0], r, c) for a, _ in terms]
    out = pl.pallas_call(body, grid=(r // tr,), in_specs=in_specs, out_specs=pl.BlockSpec((tr, c), lambda i: (i, 0)),
                         out_shape=SDS((r, c), out_dtype), name=name, compiler_params=_params(("parallel",)))(*args)
    return out.reshape(shape)


def _adamw_math(w, g, m, v):
    mn = ADAM_B1 * m + (1.0 - ADAM_B1) * g
    vn = ADAM_B2 * v + (1.0 - ADAM_B2) * (g * g)
    m_hat = mn / (1.0 - ADAM_B1 ** ADAM_STEP)
    v_hat = vn / (1.0 - ADAM_B2 ** ADAM_STEP)
    return -ADAM_LR * (m_hat / (jnp.sqrt(v_hat) + ADAM_EPS) + ADAM_WD * w), mn, vn


def _adamw_rows(name, w, g, m, v):
    r, a, c = w.shape
    tr = next(t for t in range(min(r, 128), 0, -1) if r % t == 0)

    def body(w_ref, g_ref, m_ref, v_ref, d_ref, nm_ref, nv_ref):
        d_ref[...], nm_ref[...], nv_ref[...] = _adamw_math(w_ref[...], g_ref[...], m_ref[...], v_ref[...])

    spec = pl.BlockSpec((tr, a, c), lambda i: (i, 0, 0))
    return pl.pallas_call(body, grid=(r // tr,), in_specs=[spec] * 4, out_specs=(spec,) * 3,
                          out_shape=(SDS(w.shape, F32),) * 3, name=name, compiler_params=_params(("parallel",)))(w, g, m, v)


def _adamw(name, w, g, m, v):
    shape = w.shape
    c = shape[-1]
    per_layer = isinstance(g, (list, tuple))
    nl = len(g) if per_layer else 1
    gs = [a.reshape(-1, c) for a in g] if per_layer else [g.reshape(-1, c)]
    r = gs[0].shape[0]
    w3, m3, v3 = (a.reshape(nl, r, c) for a in (w, m, v))
    tr = min(_row_tile(r), 256)

    def body(*refs):
        w_ref, m_ref, v_ref = refs[:3]
        g_refs = refs[3: 3 + nl]
        go_ref, d_ref, nm_ref, nv_ref = refs[3 + nl:]
        layer = pl.program_id(0)
        gv = g_refs[0][...]
        for k in range(1, nl):
            gv = jnp.where(layer == k, g_refs[k][...], gv)
        go_ref[...] = gv
        d_ref[...], nm_ref[...], nv_ref[...] = _adamw_math(w_ref[...], gv, m_ref[...], v_ref[...])

    spec3 = pl.BlockSpec((None, tr, c), lambda l, i: (l, i, 0))
    gspec = pl.BlockSpec((tr, c), lambda l, i: (i, 0))
    outs = pl.pallas_call(body, grid=(nl, r // tr), in_specs=[spec3] * 3 + [gspec] * nl, out_specs=(spec3,) * 4,
                          out_shape=(SDS((nl, r, c), F32),) * 4, name=name,
                          compiler_params=_params(("parallel", "parallel")))(w3, m3, v3, *gs)
    return tuple(o.reshape(shape) for o in outs)


ANY = pl.BlockSpec(memory_space=pl.ANY)
MESH = pl.DeviceIdType.MESH


def _allgather(name, xs):
    n = len(xs)

    def body(*refs):
        x_refs, out_refs = refs[:n], refs[n: 2 * n]
        send_sems, recv_sems, local_sems = refs[2 * n:]
        x, y, cc = lax.axis_index("x"), lax.axis_index("y"), lax.axis_index("c")
        me, sibling = (x, y, cc), (x, y, 1 - cc)
        chips = [(1 - x, y), (x, 1 - y), (1 - x, 1 - y)]

        def rows(a, px, py, pc):
            return out_refs[a].at[4 * px + 2 * py + pc]

        def copy(a, k, block, to, src=None):
            return pltpu.make_async_remote_copy(
                src_ref=rows(a, *block) if src is None else src, dst_ref=rows(a, *block),
                send_sem=send_sems.at[7 * a + k], recv_sem=recv_sems.at[7 * a + k], device_id=to, device_id_type=MESH)

        mine = [pltpu.make_async_copy(x_refs[a], rows(a, *me), local_sems.at[a]) for a in range(n)]
        for cp in mine:
            cp.start()
        first = []
        for a in range(n):
            first.append(copy(a, 0, me, sibling, src=x_refs[a]))
            first += [copy(a, 1 + j, me, (*chip, cc), src=x_refs[a]) for j, chip in enumerate(chips)]
        for cp in first:
            cp.start()
        passed = []
        for j, chip in enumerate(chips):
            for a in range(n):
                copy(a, 1 + j, (*chip, cc), me).wait_recv()
                fwd = copy(a, 4 + j, (*chip, cc), sibling)
                fwd.start()
                passed.append(fwd)
        for a in range(n):
            copy(a, 0, sibling, me).wait_recv()
        for j, chip in enumerate(chips):
            for a in range(n):
                copy(a, 4 + j, (*chip, 1 - cc), me).wait_recv()
        for cp in first + passed:
            cp.wait_send()
        for cp in mine:
            cp.wait()

    return pl.pallas_call(
        body, out_shape=tuple(SDS((N_DEV,) + a.shape, a.dtype) for a in xs), in_specs=[ANY] * n, out_specs=(ANY,) * n,
        scratch_shapes=[pltpu.SemaphoreType.DMA((7 * n,)), pltpu.SemaphoreType.DMA((7 * n,)),
                        pltpu.SemaphoreType.DMA((n,))],
        name=name,
    )(*xs)


def _allgather_seq(name, xs, collective_id):
    n = len(xs)
    x_refs = [jax.new_ref(a, memory_space=pltpu.MemorySpace.HBM) for a in xs]
    out_refs = [jax.empty_ref(SDS((N_DEV,) + a.shape, a.dtype), memory_space=pltpu.MemorySpace.HBM) for a in xs]

    @pl.kernel(mesh=plsc.ScalarSubcoreMesh(axis_name="seq", num_cores=1), name=name,
               scratch_types=(pltpu.SemaphoreType.DMA((7 * n,)), pltpu.SemaphoreType.DMA((7 * n,)),
                              pltpu.SemaphoreType.DMA((n,))),
               compiler_params=pltpu.CompilerParams(collective_id=collective_id))
    def launch(send_sems, recv_sems, local_sems):
        x, y, cc = lax.axis_index("x"), lax.axis_index("y"), lax.axis_index("c")
        me, sibling = (x, y, cc), (x, y, 1 - cc)
        chips = [(1 - x, y), (x, 1 - y), (1 - x, 1 - y)]
        barrier = pltpu.get_barrier_semaphore()
        for peer in [sibling] + [(*chip, cc) for chip in chips]:
            pl.semaphore_signal(barrier, inc=1, device_id=peer, device_id_type=MESH)
        pl.semaphore_wait(barrier, 4)

        def rows(a, px, py, pc):
            return out_refs[a].at[4 * px + 2 * py + pc]

        def copy(a, k, block, to, src=None):
            return pltpu.make_async_remote_copy(
                src_ref=rows(a, *block) if src is None else src, dst_ref=rows(a, *block),
                send_sem=send_sems.at[7 * a + k], recv_sem=recv_sems.at[7 * a + k], device_id=to, device_id_type=MESH)

        mine = [pltpu.make_async_copy(x_refs[a], rows(a, *me), local_sems.at[a]) for a in range(n)]
        for cp in mine:
            cp.start()
        first = []
        for a in range(n):
            first.append(copy(a, 0, me, sibling, src=x_refs[a]))
            first += [copy(a, 1 + j, me, (*chip, cc), src=x_refs[a]) for j, chip in enumerate(chips)]
        for cp in first:
            cp.start()
        passed = []
        for j, chip in enumerate(chips):
            for a in range(n):
                copy(a, 1 + j, (*chip, cc), me).wait_recv()
                fwd = copy(a, 4 + j, (*chip, cc), sibling)
                fwd.start()
                passed.append(fwd)
        for a in range(n):
            copy(a, 0, sibling, me).wait_recv()
        for j, chip in enumerate(chips):
            for a in range(n):
                copy(a, 4 + j, (*chip, 1 - cc), me).wait_recv()
        for cp in first + passed:
            cp.wait_send()
        for cp in mine:
            cp.wait()

    launch()
    return [r[...] for r in out_refs]


HBM = pl.BlockSpec(memory_space=pltpu.HBM)
SEM = pl.BlockSpec(memory_space=pltpu.SEMAPHORE)
EFFECT = pltpu.SideEffectType.DATAFLOW_SIDE_EFFECTING


def _sibling_plan(srcs, lands, send_sems, recv_sems):
    x, y, cc = lax.axis_index("x"), lax.axis_index("y"), lax.axis_index("c")
    return [pltpu.make_async_remote_copy(
        src_ref=srcs[a].at[2 * q + 1 - cc], dst_ref=lands[a].at[q], send_sem=send_sems.at[4 * a + q],
        recv_sem=recv_sems.at[4 * a + q], device_id=(x, y, 1 - cc), device_id_type=MESH)
        for a in range(len(srcs)) for q in range(4)]


def _chips_plan(srcs, lands, send_sems, recv_sems):
    x, y, cc = lax.axis_index("x"), lax.axis_index("y"), lax.axis_index("c")
    chips = [(1 - x, y), (x, 1 - y), (1 - x, 1 - y)]
    return [pltpu.make_async_remote_copy(
        src_ref=srcs[a].at[2 * px + py], dst_ref=lands[a].at[j], send_sem=send_sems.at[3 * a + j],
        recv_sem=recv_sems.at[3 * a + j], device_id=(px, py, cc), device_id_type=MESH)
        for a in range(len(srcs)) for j, (px, py) in enumerate(chips)]


def _copies_start(name, plan, per_array, srcs, land_lead):
    n = len(srcs)
    k = per_array * n

    def body(*refs):
        src_refs, land_refs = refs[:n], refs[n: 2 * n]
        send_sems, recv_sems = refs[2 * n], refs[2 * n + 1]
        token = refs[-1]
        for cp in plan(src_refs, land_refs, send_sems, recv_sems):
            cp.start()
        token[...] = jnp.zeros_like(token)

    lands = [lax.empty((land_lead,) + a.shape[1:], a.dtype) for a in srcs]
    outs = pl.pallas_call(
        body, name=name,
        out_shape=(pltpu.SemaphoreType.DMA((k,)), pltpu.SemaphoreType.DMA((k,)),
                   *[pltpu.HBM(a.shape, a.dtype) for a in srcs], *[pltpu.HBM(a.shape, a.dtype) for a in lands],
                   SDS((8, 128), F32)),
        in_specs=[HBM] * (2 * n), out_specs=(SEM, SEM, *[HBM] * (2 * n), pl.BlockSpec(memory_space=pltpu.VMEM)),
        input_output_aliases={i: 2 + i for i in range(2 * n)},
        compiler_params=pltpu.CompilerParams(has_side_effects=EFFECT),
    )(*[pltpu.with_memory_space_constraint(a, pltpu.HBM) for a in srcs],
      *[pltpu.with_memory_space_constraint(a, pltpu.HBM) for a in lands])
    return outs[0], outs[1], list(outs[2: 2 + n]), list(outs[2 + n: 2 + 2 * n]), outs[-1]


def _copies_wait(name, plan, started, after):
    send_sems, recv_sems, srcs, lands, _ = started
    n = len(srcs)
    after = tuple(after)

    def body(*refs):
        src_refs, land_refs = refs[:n], refs[n: 2 * n]
        for cp in plan(src_refs, land_refs, refs[2 * n], refs[2 * n + 1]):
            cp.wait_send()
            cp.wait_recv()

    outs = pl.pallas_call(
        body, name=name,
        out_shape=tuple(pltpu.HBM(a.shape, a.dtype) for a in srcs + lands),
        in_specs=[HBM] * (2 * n) + [SEM, SEM] + [ANY] * len(after), out_specs=(HBM,) * (2 * n),
        input_output_aliases={i: i for i in range(2 * n)},
        compiler_params=pltpu.CompilerParams(has_side_effects=EFFECT),
    )(*srcs, *lands, send_sems, recv_sems, *after)
    return list(outs[n:])


BIG = (("w_in", 1), ("w_ffn_in", 1), ("w_proj_ssm", 0), ("w_proj_gdn", 0), ("w_out", 0), ("w_ffn_down", 0))
CONVS = (("ssm_conv_w", 1), ("gdn_conv_w", 1))


def _to_dest_major(full, axis):
    if isinstance(full, tuple):
        per = N_DEV // len(full)
        s = full[0].shape[1] // per
        return jnp.stack([full[d // per][:, (d % per) * s: (d % per + 1) * s] for d in range(N_DEV)])
    a, b = full.shape
    if axis == 0:
        return full.reshape(N_DEV, a // N_DEV, b)
    s = b // N_DEV
    return jnp.stack([full[:, d * s: (d + 1) * s] for d in range(N_DEV)])


def _from_gathered(g, axis):
    if axis == 0:
        return g.reshape(-1, g.shape[2])
    return jnp.concatenate([g[d] for d in range(N_DEV)], axis=1)


IN_RUNS = ((Z_OFF, O_Z, 1024), (GZ_OFF, O_GZ, 1024), (G1_OFF, O_G1, 1024), (G2_OFF, O_G2, 1024), (QKV_OFF, O_QKV, 3072),
           (XBC_OFF, O_XBC, 1536), (SM_OFF, O_DT, 16), (SM_OFF + LANE_A, O_A, 8), (SM_OFF + LANE_B, O_B, 8))
IN_SHARD = IN_DIM // N_DEV


def _w_in_from_blocks(g):
    rows = g.shape[1]
    parts, pos = [], 0
    for off, o0, width in IN_RUNS:
        if off > pos:
            parts.append(jnp.zeros((rows, off - pos), g.dtype))
        c = o0
        while c < o0 + width:
            d = c // IN_SHARD
            hi = min(o0 + width, (d + 1) * IN_SHARD)
            parts.append(g[d][:, c - d * IN_SHARD: hi - d * IN_SHARD])
            c = hi
        pos = off + width
    parts.append(jnp.zeros((rows, PROJ_W - pos), g.dtype))
    return jnp.concatenate(parts, axis=1)


def _w_in_to_blocks(wp):
    by_orig = sorted(IN_RUNS, key=lambda r: r[1])
    blocks = []
    for d in range(N_DEV):
        lo, hi = d * IN_SHARD, (d + 1) * IN_SHARD
        parts = []
        for off, o0, width in by_orig:
            a, b = max(lo, o0), min(hi, o0 + width)
            if a < b:
                parts.append(wp[:, off + a - o0: off + b - o0])
        blocks.append(jnp.concatenate(parts, axis=1))
    return jnp.stack(blocks)


def _pad128(v, lane0):
    return jnp.zeros((1, 128), F32).at[0, lane0: lane0 + v.shape[0]].set(v)


def _layer_consts(p):
    return dict(
        dtb=_pad128(p["ssm_dt_bias"], 0), arow=_pad128(-jnp.exp(p["ssm_a_log"]), 0),
        dxrow=jnp.repeat(p["ssm_d"], SSM_P).reshape(1, 1024), snw=p["ssm_norm_w"].reshape(1, 1024),
        gb=_pad128(p["gdn_dt_bias"], LANE_A), garow=_pad128(-jnp.exp(p["gdn_a_log"]), LANE_A),
        gnw=p["gdn_norm_w"].reshape(1, 128), zb=jnp.zeros((1, GDN_QKV), F32), scb=p["ssm_conv_b"].reshape(1, SSM_CONV))


def _expand_matrix():
    row = lax.broadcasted_iota(jnp.int32, (128, 1024), 0)
    col = lax.broadcasted_iota(jnp.int32, (128, 1024), 1)
    return (col // SSM_P == row).astype(BF16)


def _silu_mul_epi(acc, up):
    g = acc
    return g, g * _sigmoid(g) * up.astype(F32)


def _merge_epi(acc, p1, g1, g2):
    return acc, _sigmoid(g1) * p1.astype(F32) + _sigmoid(g2) * acc


def _add_epi(acc, res):
    return (acc + res,)


def _ffn_bwd_epi(acc, gate, up):
    g = gate.astype(F32)
    sg = _sigmoid(g)
    return acc * up.astype(F32) * (sg * (1.0 + g * (1.0 - sg))), acc * (g * sg)


def _merge_bwd_epi(acc, g1, g2, p1, p2):
    s1, s2 = _sigmoid(g1), _sigmoid(g2)
    dg1, dg2 = acc * p1.astype(F32) * (s1 * (1.0 - s1)), acc * p2.astype(F32) * (s2 * (1.0 - s2))
    return acc * s1, acc * s2, jnp.concatenate([dg1, dg2], axis=1)


def _layer_fwd(l, x, p, rmat):
    t = x.shape[0]
    n = f"l{l}_"
    k = _layer_consts(p)
    h = _rmsnorm_fwd(n + "norm_mix", x, p["norm_mix_w"])
    proj = _matmul(n + "in_proj", "nn", [(h, 0, p["w_in"], 0)], t, PROJ_W, 1024, 1024, 1280, 1024, (F32,))
    act_g, pre_g = _conv_fwd(n + "conv_gdn", proj, QKV_OFF, p["gdn_conv_w"], k["zb"])
    act_s, pre_s = _conv_fwd(n + "conv_ssm", proj, XBC_OFF, p["ssm_conv_w"], k["scb"])
    y_ssm, ysc, st_s = _ssd_fwd(n + "ssd_fwd", act_s, proj, k["dtb"], k["arow"], k["dxrow"], k["snw"], rmat)
    y_gdn, oraw, st_g = _gdn_fwd(n + "gdn_fwd", act_g, proj, k["gb"], k["garow"], k["gnw"])
    if "late" in p:
        y_gdn, late = p["late"](y_gdn)
        p = {**p, **late}
    p1 = _matmul(n + "proj_ssm", "nn", [(y_ssm, 0, p["w_proj_ssm"], 0)], t, 1024, 1024, 1024, 1024, 1024, (BF16,))
    p2, merged = _matmul(n + "proj_gdn_merge", "nn", [(y_gdn, 0, p["w_proj_gdn"], 0)], t, 1024, 1024, 512, 1024, 1024,
                         (BF16, BF16), epi=_merge_epi, extras=[(p1, 0), (proj, G1_OFF // 1024), (proj, G2_OFF // 1024)])
    x1 = _matmul(n + "out_proj", "nn", [(merged, 0, p["w_out"], 0)], t, 1024, 1024, 1024, 1024, 1024, (F32,),
                 epi=_add_epi, extras=[(x, 0)])
    h2 = _rmsnorm_fwd(n + "norm_ffn", x1, p["norm_ffn_w"])
    up = _matmul(n + "ffn_up", "nn", [(h2, 0, p["w_ffn_in"], 2)], t, FFN, 1024, 1024, FFN // 2, 1024, (BF16,))
    gate, act = _matmul(n + "ffn_gate", "nn", [(h2, 0, p["w_ffn_in"], 0)], t, FFN, 1024, 1024, FFN // 2, 1024, (BF16, BF16),
                        epi=_silu_mul_epi, extras=[(up, 0)])
    x2 = _matmul(n + "ffn_down", "nn", [(act, 0, p["w_ffn_down"], 0)], t, 1024, FFN, 1024, 1024, FFN, (F32,),
                 epi=_add_epi, extras=[(x1, 0)])
    saved = dict(x=x, h=h, proj=proj, act_g=act_g, act_s=act_s, pre_g=pre_g, pre_s=pre_s, y_ssm=y_ssm, ysc=ysc, st_s=st_s, y_gdn=y_gdn, oraw=oraw,
                 st_g=st_g, p1=p1, p2=p2, merged=merged, x1=x1, h2=h2, up=up, gate=gate, act=act, k=k, p=p)
    return x2, saved


def _layer_bwd(l, dx2, dx2b, s, p, rmat, hooks):
    t = dx2.shape[0]
    n = f"l{l}_"
    k = s["k"]
    tk_tok = 1024
    hf = FFN // 2
    g = {}
    dgate, dup = _matmul(n + "d_ffn_act", "nt", [(dx2b, 0, p["w_ffn_down"], 0)], t, FFN, 1024, 1024, hf, 1024, (BF16, BF16),
                         epi=_ffn_bwd_epi, extras=[(s["gate"], 0), (s["up"], 0)])
    g["w_ffn_down"] = _matmul(n + "dw_ffn_down", "tn", [(s["act"], 0, dx2b, 0)], FFN, 1024, t, hf, 1024, tk_tok, (BF16,))
    dh2 = _matmul(n + "d_ffn_in", "nt", [(dgate, 0, p["w_ffn_in"], 0), (dup, 0, p["w_ffn_in"], 2)], t, 1024, FFN,
                  1024, 1024, hf, (F32,))
    dwg = _matmul(n + "dw_ffn_gate", "tn", [(s["h2"], 0, dgate, 0)], 1024, FFN, t, 1024, hf, tk_tok, (BF16,))
    dwu = _matmul(n + "dw_ffn_up", "tn", [(s["h2"], 0, dup, 0)], 1024, FFN, t, 1024, hf, tk_tok, (BF16,))
    g["w_ffn_in"] = (dwg, dwu)
    dx1, dx1b, g["norm_ffn_w"] = _rmsnorm_bwd(n + "d_norm_ffn", s["x1"], p["norm_ffn_w"], dh2, dx2)
    dx1b = hooks.ffn_done(dx1b)
    dp1, dp2, dproj = _matmul(
        n + "d_out_proj", "nt", [(dx1b, 0, p["w_out"], 0)], t, 1024, 1024, 512, 1024, 1024,
        (BF16, BF16, (BF16, PROJ_W, 2048, G1_OFF // 2048)), epi=_merge_bwd_epi,
        extras=[(s["proj"], G1_OFF // 1024), (s["proj"], G2_OFF // 1024), (s["p1"], 0), (s["p2"], 0)])
    g["w_out"] = _matmul(n + "dw_out", "tn", [(s["merged"], 0, dx1b, 0)], 1024, 1024, t, 1024, 1024, tk_tok, (BF16,))
    g["w_proj_ssm"] = _matmul(n + "dw_proj_ssm", "tn", [(s["y_ssm"], 0, dp1, 0)], 1024, 1024, t, 1024, 1024, tk_tok, (BF16,))
    g["w_proj_gdn"] = _matmul(n + "dw_proj_gdn", "tn", [(s["y_gdn"], 0, dp2, 0)], 1024, 1024, t, 1024, 1024, tk_tok, (BF16,))
    dp1, dp2 = hooks.early_ready(l, g, dp1, dp2)
    dy_ssm = _matmul(n + "d_proj_ssm", "nt", [(dp1, 0, p["w_proj_ssm"], 0)], t, 1024, 1024, 1024, 1024, 1024, (F32,))
    dy_gdn = _matmul(n + "d_proj_gdn", "nt", [(dp2, 0, p["w_proj_gdn"], 0)], t, 1024, 1024, 1024, 1024, 1024, (F32,))
    dact_s, dproj, dsm_s, dsnw, dd, dal, ddtb = _ssd_bwd(n + "ssd_bwd", s["act_s"], s["proj"], k["dtb"], k["arow"],
                                                           k["dxrow"], k["snw"], rmat, s["ysc"], s["st_s"], dy_ssm, dproj)
    dact_g, dproj, dsm_g, dgnw, dgal, dgb = _gdn_bwd(n + "gdn_bwd", s["act_g"], s["proj"], k["gb"], k["garow"], k["gnw"],
                                                       s["oraw"], s["st_g"], dy_gdn, dproj)
    dsm_s = hooks.mixers_done(dsm_s)
    dproj = _place_small(n + "d_small", dsm_s, dsm_g, dproj)
    dproj, g["ssm_conv_w"], dcb = _conv_bwd(n + "d_conv_ssm", s["proj"], XBC_OFF, p["ssm_conv_w"], s["pre_s"], dact_s, dproj)
    dproj, g["gdn_conv_w"], _ = _conv_bwd(n + "d_conv_gdn", s["proj"], QKV_OFF, p["gdn_conv_w"], s["pre_g"], dact_g, dproj)
    g["ssm_conv_b"] = dcb.reshape(-1)
    g["ssm_norm_w"] = dsnw.reshape(-1)
    g["ssm_d"] = dd[0, :SSM_HEADS]
    g["ssm_a_log"] = dal[0, :SSM_HEADS]
    g["ssm_dt_bias"] = ddtb[0, :SSM_HEADS]
    g["gdn_norm_w"] = dgnw.reshape(-1)
    g["gdn_a_log"] = dgal[0, LANE_A: LANE_A + GDN_HEADS]
    g["gdn_dt_bias"] = dgb[0, LANE_A: LANE_A + GDN_HEADS]
    dh = _matmul(n + "d_in_proj", "nt", [(dproj, 0, p["w_in"], 0)], t, 1024, PROJ_W, 1024, 1024, 1280, (F32,))
    g["w_in"] = _matmul(n + "dw_in", "tn", [(s["h"], 0, dproj, 0)], 1024, PROJ_W, t, 1024, 1280, tk_tok, (BF16,))
    dx, dxb, g["norm_mix_w"] = _rmsnorm_bwd(n + "d_norm_mix", s["x"], p["norm_mix_w"], dh, dx1)
    g["norm_mix_w"] = g["norm_mix_w"].reshape(-1)
    g["norm_ffn_w"] = g["norm_ffn_w"].reshape(-1)
    return dx, dxb, g


def _local_step(x, tgt, layers, final_norm_w, reduce=False):
    rmat = _expand_matrix()
    saved, params = [], []
    for l in range(DEPTH):
        x, p = layers[l](x)
        x, s = _layer_fwd(l, x, p, rmat)
        saved.append(s)
        params.append(s["p"])
    loss, dx, dxb, dfw = _loss_head("loss_head", x, final_norm_w, tgt)
    grads = [None] * DEPTH
    hooks = _ReduceBesideBackward() if reduce else _NoReduce()
    for l in reversed(range(DEPTH)):
        dx, dxb, grads[l] = _layer_bwd(l, dx, dxb, saved[l], params[l], rmat, hooks)
        if reduce:
            dxb = hooks.layer_done(l, grads[l], dxb)
    if reduce:
        hooks.finish_start(dxb)
    return loss[0, 0], dx, grads, dfw.reshape(-1), hooks if reduce else None


SMALL = ("norm_mix_w", "ssm_conv_b", "ssm_dt_bias", "ssm_a_log", "ssm_d", "ssm_norm_w", "gdn_a_log", "gdn_dt_bias",
         "gdn_norm_w", "norm_ffn_w")
WEIGHTS = ("norm_mix_w", "w_in", "ssm_conv_w", "ssm_conv_b", "ssm_dt_bias", "ssm_a_log", "ssm_d", "ssm_norm_w", "gdn_conv_w",
           "gdn_a_log", "gdn_dt_bias", "gdn_norm_w", "w_proj_ssm", "w_proj_gdn", "w_out", "norm_ffn_w", "w_ffn_in",
           "w_ffn_down", "final_norm_w")


FIRST_USED = ("w_in", "ssm_conv_w", "gdn_conv_w")


def _gather_layer(l, w):
    conv_names = [nm for nm, _ in CONVS]
    groups = ([s for s in BIG + CONVS if s[0] in FIRST_USED], [s for s in BIG + CONVS if s[0] not in FIRST_USED])
    gathered = []
    for i, (specs, tag) in enumerate(zip(groups, ("first", "rest"))):
        shards = [w[nm][l] if nm in conv_names else w[nm][l].astype(BF16) for nm, _ in specs]
        gathered.append(_allgather_seq(f"l{l}_gather_{tag}", shards, collective_id=2 * l + i))
    small = {nm: w[nm][l] for nm in SMALL}

    def use(i, act):
        act, blocks = lax.optimization_barrier((act, gathered[i]))
        return act, {nm: _w_in_from_blocks(g) if nm == "w_in" else _from_gathered(g, axis)
                     for (nm, axis), g in zip(groups[i], blocks)}

    def full_weights(x):
        x, out = use(0, x)
        out.update(small)
        out["late"] = lambda y: use(1, y)
        return x, out

    return full_weights


EARLY_GRADS = ("w_ffn_down", "w_ffn_in", "w_out", "w_proj_ssm", "w_proj_gdn")


class _GradReduceScatter:
    def __init__(self, tag, specs, grads):
        self.tag = tag
        self.specs = specs
        self.blocks = [_w_in_to_blocks(grads[nm]) if nm == "w_in" else _to_dest_major(grads[nm], axis)
                       for nm, axis in specs]

    def _tied(self, started, acts):
        *acts, self.token = lax.optimization_barrier((*acts, started[4]))
        return acts

    def start(self, *acts):
        cc = lax.axis_index("c")
        self.keep = [lax.dynamic_index_in_dim(b.reshape((4, 2) + b.shape[1:]), cc, axis=1, keepdims=False)
                     for b in self.blocks]
        self.to_sibling = _copies_start(f"{self.tag}_to_sibling_start", _sibling_plan, 4, self.blocks, 4)
        return self._tied(self.to_sibling, acts)

    def mid(self, *acts):
        got = _copies_wait(f"{self.tag}_to_sibling_wait", _sibling_plan, self.to_sibling, (acts[0], self.token))
        chip_sums = [_sum_terms(f"{self.tag}_chip_sum_{nm}", [(k[None], 0), (g[None], 0)], BF16)
                     for (nm, _), k, g in zip(self.specs, self.keep, got)]
        self.to_chips = _copies_start(f"{self.tag}_between_chips_start", _chips_plan, 3, chip_sums, 3)
        return self._tied(self.to_chips, acts)

    def end(self, after):
        after = tuple(after) if isinstance(after, (tuple, list)) else (after,)
        landed = _copies_wait(f"{self.tag}_between_chips_wait", _chips_plan, self.to_chips, (*after, self.token))
        my_chip = 2 * lax.axis_index("x") + lax.axis_index("y")
        own = [lax.dynamic_index_in_dim(s, my_chip, axis=0, keepdims=True) for s in self.to_chips[2]]
        return {nm: _sum_terms(f"{self.tag}_total_{nm}", [(o, 0), (e, 0), (e, 1), (e, 2)], F32)
                for (nm, _), o, e in zip(self.specs, own, landed)}


class _NoReduce:
    def ffn_done(self, dx1b):
        return dx1b

    def early_ready(self, l, g, dp1, dp2):
        return dp1, dp2

    def mixers_done(self, dsm):
        return dsm


class _ReduceBesideBackward(_NoReduce):
    def __init__(self):
        self.late = None
        self.early = None
        self.shards = [dict() for _ in range(DEPTH)]

    def ffn_done(self, dx1b):
        if self.late is not None:
            (dx1b,) = self.late.mid(dx1b)
        return dx1b

    def early_ready(self, l, g, dp1, dp2):
        self.early = _GradReduceScatter(f"l{l}_early_grads", [s for s in BIG if s[0] in EARLY_GRADS], g)
        return self.early.start(dp1, dp2)

    def mixers_done(self, dsm):
        (dsm,) = self.early.mid(dsm)
        return dsm

    def layer_done(self, l, g, dxb):
        if self.late is not None:
            self.shards[l + 1].update(self.late.end(dxb))
        self.shards[l].update(self.early.end(dxb))
        self.late = _GradReduceScatter(f"l{l}_late_grads", [s for s in BIG + CONVS if s[0] not in EARLY_GRADS], g)
        (dxb,) = self.late.start(dxb)
        return dxb

    def finish_start(self, dxb):
        self.late.mid(dxb)

    def finish_end(self, after):
        return self.late.end(after)


def _allreduce_small(vecs):
    flat = jnp.concatenate(vecs)
    n = flat.shape[0]
    rows = -(-n // 128)
    rows = -(-rows // 8) * 8
    buf = jnp.pad(flat, (0, rows * 128 - n)).reshape(rows, 128)
    (allv,) = _allgather("gather_small_grads", [buf])
    tot = _sum_terms("small_grads_total", [(allv, d) for d in range(N_DEV)], F32).reshape(-1)
    out, o = [], 0
    for v in vecs:
        out.append(tot[o: o + v.shape[0]])
        o += v.shape[0]
    return out


def kernel(x, norm_mix_w, w_in, ssm_conv_w, ssm_conv_b, ssm_dt_bias, ssm_a_log, ssm_d, ssm_norm_w, gdn_conv_w, gdn_a_log, gdn_dt_bias, gdn_norm_w, w_proj_ssm, w_proj_gdn, w_out, norm_ffn_w, w_ffn_in, w_ffn_down, final_norm_w, loss_target, m_norm_mix_w, m_w_in, m_ssm_conv_w, m_ssm_conv_b, m_ssm_dt_bias, m_ssm_a_log, m_ssm_d, m_ssm_norm_w, m_gdn_conv_w, m_gdn_a_log, m_gdn_dt_bias, m_gdn_norm_w, m_w_proj_ssm, m_w_proj_gdn, m_w_out, m_norm_ffn_w, m_w_ffn_in, m_w_ffn_down, m_final_norm_w, v_norm_mix_w, v_w_in, v_ssm_conv_w, v_ssm_conv_b, v_ssm_dt_bias, v_ssm_a_log, v_ssm_d, v_ssm_norm_w, v_gdn_conv_w, v_gdn_a_log, v_gdn_dt_bias, v_gdn_norm_w, v_w_proj_ssm, v_w_proj_gdn, v_w_out, v_norm_ffn_w, v_w_ffn_in, v_w_ffn_down, v_final_norm_w):
    w = dict(norm_mix_w=norm_mix_w, w_in=w_in, ssm_conv_w=ssm_conv_w, ssm_conv_b=ssm_conv_b, ssm_dt_bias=ssm_dt_bias,
             ssm_a_log=ssm_a_log, ssm_d=ssm_d, ssm_norm_w=ssm_norm_w, gdn_conv_w=gdn_conv_w, gdn_a_log=gdn_a_log,
             gdn_dt_bias=gdn_dt_bias, gdn_norm_w=gdn_norm_w, w_proj_ssm=w_proj_ssm, w_proj_gdn=w_proj_gdn, w_out=w_out,
             norm_ffn_w=norm_ffn_w, w_ffn_in=w_ffn_in, w_ffn_down=w_ffn_down, final_norm_w=final_norm_w)
    m = dict(norm_mix_w=m_norm_mix_w, w_in=m_w_in, ssm_conv_w=m_ssm_conv_w, ssm_conv_b=m_ssm_conv_b, ssm_dt_bias=m_ssm_dt_bias,
             ssm_a_log=m_ssm_a_log, ssm_d=m_ssm_d, ssm_norm_w=m_ssm_norm_w, gdn_conv_w=m_gdn_conv_w, gdn_a_log=m_gdn_a_log,
             gdn_dt_bias=m_gdn_dt_bias, gdn_norm_w=m_gdn_norm_w, w_proj_ssm=m_w_proj_ssm, w_proj_gdn=m_w_proj_gdn,
             w_out=m_w_out, norm_ffn_w=m_norm_ffn_w, w_ffn_in=m_w_ffn_in, w_ffn_down=m_w_ffn_down,
             final_norm_w=m_final_norm_w)
    v = dict(norm_mix_w=v_norm_mix_w, w_in=v_w_in, ssm_conv_w=v_ssm_conv_w, ssm_conv_b=v_ssm_conv_b, ssm_dt_bias=v_ssm_dt_bias,
             ssm_a_log=v_ssm_a_log, ssm_d=v_ssm_d, ssm_norm_w=v_ssm_norm_w, gdn_conv_w=v_gdn_conv_w, gdn_a_log=v_gdn_a_log,
             gdn_dt_bias=v_gdn_dt_bias, gdn_norm_w=v_gdn_norm_w, w_proj_ssm=v_w_proj_ssm, w_proj_gdn=v_w_proj_gdn,
             w_out=v_w_out, norm_ffn_w=v_norm_ffn_w, w_ffn_in=v_w_ffn_in, w_ffn_down=v_w_ffn_down,
             final_norm_w=v_final_norm_w)

    layers = [_gather_layer(l, w) for l in range(DEPTH)]
    loss_part, dx, lgrads, dfw, reducer = _local_step(x[0], loss_target[0], layers, final_norm_w, reduce=True)
    loss = lax.psum(loss_part, ("x", "y", "c"))
    shard_grads = reducer.shards
    late = [nm for nm, _ in BIG + CONVS if nm not in EARLY_GRADS]
    grad = {nm: [shard_grads[l][nm] for l in range(DEPTH)] for nm in EARLY_GRADS}
    small_vecs = [lgrads[l][nm].reshape(-1) for l in range(DEPTH) for nm in SMALL] + [dfw]
    small_sum = _allreduce_small(small_vecs)
    for i, nm in enumerate(SMALL):
        grad[nm] = jnp.stack([small_sum[l * len(SMALL) + i].reshape(w[nm].shape[1:]) for l in range(DEPTH)])
    grad["final_norm_w"] = small_sum[-1]

    deltas, new_m, new_v = {}, {}, {}
    for nm in [n for n in WEIGHTS if n not in late]:
        if nm == "w_ffn_in":
            tr = lambda a: jnp.transpose(a, (0, 2, 1))
            outs = _adamw("adamw_" + nm, tr(w[nm]), [g.T for g in grad[nm]], tr(m[nm]), tr(v[nm]))
            grad[nm], deltas[nm], new_m[nm], new_v[nm] = (tr(o) for o in outs)
        else:
            grad[nm], deltas[nm], new_m[nm], new_v[nm] = _adamw("adamw_" + nm, w[nm], grad[nm], m[nm], v[nm])
    shard_grads[0].update(reducer.finish_end([deltas[nm] for nm in EARLY_GRADS]))
    for nm in late:
        gl = [shard_grads[l][nm] for l in range(DEPTH)]
        if nm == "w_in":
            fwd, back = (lambda a: jnp.transpose(a, (2, 0, 1))), (lambda a: jnp.transpose(a, (1, 2, 0)))
            g3 = jnp.stack([g.T for g in gl], axis=1)
            outs = _adamw_rows("adamw_" + nm, fwd(w[nm]), g3, fwd(m[nm]), fwd(v[nm]))
            grad[nm], deltas[nm], new_m[nm], new_v[nm] = (back(o) for o in (g3,) + tuple(outs))
        else:
            grad[nm], deltas[nm], new_m[nm], new_v[nm] = _adamw("adamw_" + nm, w[nm], gl, m[nm], v[nm])
    return (loss, dx[None], *[grad[nm] for nm in WEIGHTS], *[deltas[nm] for nm in WEIGHTS],
            *[new_m[nm] for nm in WEIGHTS], *[new_v[nm] for nm in WEIGHTS])
```

```python
import functools

import jax
import jax.numpy as jnp
from jax import lax
from jax.experimental import pallas as pl
from jax.experimental.pallas import tpu as pltpu
from jax.experimental.pallas import tpu_sc as plsc

F32 = jnp.float32
BF16 = jnp.bfloat16
HI = lax.Precision.HIGHEST
SDS = jax.ShapeDtypeStruct

D_MODEL = 1024
DEPTH = 2
SSM_HEADS = 16
SSM_P = 64
SSM_N = 128
SSM_GROUPS = 2
SSM_CONV = 1536
GDN_HEADS = 8
GDN_DK = 128
GDN_QKV = 3072
CONV_K = 4
CHUNK = 64
SCAN_CHUNKS_PER_STEP = 8
FFN = 2816
IN_DIM = 8736
EPS = 1e-6
N_DEV = 8

Z_OFF = 0
GZ_OFF = 1024
G1_OFF = 2048
G2_OFF = 3072
QKV_OFF = 4096
XBC_OFF = 7168
SM_OFF = 8704
PROJ_W = 8960
LANE_A = 16
LANE_B = 24
O_Z, O_XBC, O_DT, O_QKV, O_GZ, O_A, O_B, O_G1, O_G2 = 0, 1024, 2560, 2576, 5648, 6672, 6680, 6688, 7712

ADAM_LR = 0.001
ADAM_B1 = 0.9
ADAM_B2 = 0.999
ADAM_EPS = 1e-08
ADAM_WD = 0.01
ADAM_STEP = 10

V7X_VMEM_LIMIT = 48 * 1024 * 1024

NN = ((1,), (0,))
NT = ((1,), (1,))
TN = ((0,), (0,))


def _bdot(a, b, dims):
    return lax.dot_general(a.astype(BF16), b.astype(BF16), (dims, ((), ())), preferred_element_type=F32)


def _hdot(a, b, dims=NN):
    return lax.dot_general(a, b, (dims, ((), ())), precision=HI, preferred_element_type=F32)


def _sigmoid(x):
    return 1.0 / (1.0 + jnp.exp(-x))


def _softplus(x):
    return jnp.maximum(x, 0.0) + jnp.log(1.0 + jnp.exp(-jnp.abs(x)))


def _params(dims):
    return pltpu.CompilerParams(dimension_semantics=dims, vmem_limit_bytes=V7X_VMEM_LIMIT)


def _rowsum(x):
    return jnp.sum(x, axis=-1, keepdims=True)


def _colsum(x):
    return jnp.sum(x, axis=0, keepdims=True)


def _matmul(name, mode, pairs, m, n, kdim, tm, tn, tk, out_dtypes, epi=None, extras=(), second=None):
    tm, tn, tk = min(tm, m), min(tn, n), min(tk, kdim)
    nk = kdim // tk
    assert m % tm == 0 and n % tn == 0 and kdim % tk == 0, (name, m, n, kdim, tm, tn, tk)
    assert second is None or nk == 1
    in_specs, args = [], []
    for a, a_off, b, b_off in pairs:
        if mode == "nn":
            in_specs.append(pl.BlockSpec((tm, tk), lambda i, j, k, o=a_off: (i, k + o)))
            in_specs.append(pl.BlockSpec((tk, tn), lambda i, j, k, o=b_off: (k, j + o)))
            dims = NN
        elif mode == "nt":
            in_specs.append(pl.BlockSpec((tm, tk), lambda i, j, k, o=a_off: (i, k + o)))
            in_specs.append(pl.BlockSpec((tn, tk), lambda i, j, k, o=b_off: (j, k + o)))
            dims = NT
        else:
            in_specs.append(pl.BlockSpec((tk, tm), lambda i, j, k, o=a_off: (k, i + o)))
            in_specs.append(pl.BlockSpec((tk, tn), lambda i, j, k, o=b_off: (k, j + o)))
            dims = TN
        args += [a, b]
    for e, e_off in extras:
        in_specs.append(pl.BlockSpec((tm, tn), lambda i, j, k, o=e_off: (i, j + o)))
        args.append(e)
    npair, nex, nout = len(pairs), len(extras), len(out_dtypes)

    def body(*refs):
        prefs = refs[: 2 * npair]
        erefs = refs[2 * npair: 2 * npair + nex]
        orefs = refs[2 * npair + nex: 2 * npair + nex + nout]

        def finish(*res):
            outs = res if epi is None else epi(*res, *[e[...] for e in erefs])
            for o, r in zip(orefs, outs):
                o[...] = r.astype(o.dtype)

        def total(ps):
            s = _bdot(prefs[2 * ps[0]][...], prefs[2 * ps[0] + 1][...], dims)
            for p in ps[1:]:
                s = s + _bdot(prefs[2 * p][...], prefs[2 * p + 1][...], dims)
            return s

        if second is not None:
            finish(total(list(range(second))), total(list(range(second, npair))))
            return
        s = total(list(range(npair)))
        if nk == 1:
            finish(s)
            return
        acc = refs[-1]
        k = pl.program_id(2)

        @pl.when(k == 0)
        def _():
            acc[...] = s

        @pl.when(k > 0)
        def _():
            acc[...] += s

        @pl.when(k == nk - 1)
        def _():
            finish(acc[...])

    out_shape, out_specs = [], []
    for od in out_dtypes:
        if isinstance(od, tuple):
            dt, full_w, blk_w, cblk = od
            assert n == tn
            out_shape.append(SDS((m, full_w), dt))
            out_specs.append(pl.BlockSpec((tm, blk_w), lambda i, j, k, c=cblk: (i, c)))
        else:
            out_shape.append(SDS((m, n), od))
            out_specs.append(pl.BlockSpec((tm, tn), lambda i, j, k: (i, j)))
    out_shape, out_specs = tuple(out_shape), tuple(out_specs)
    res = pl.pallas_call(
        body, grid=(m // tm, n // tn, nk), in_specs=in_specs, out_specs=out_specs, out_shape=out_shape,
        scratch_shapes=[pltpu.VMEM((tm, tn), F32)] if nk > 1 else [], name=name,
        compiler_params=_params(("parallel", "parallel", "arbitrary")),
    )(*args)
    return res if nout > 1 else res[0]


def _rmsnorm_fwd(name, x, w):
    t, d = x.shape
    tm = min(512, t)

    def body(x_ref, w_ref, h_ref):
        xv = x_ref[...]
        r = lax.rsqrt(jnp.mean(xv * xv, axis=-1, keepdims=True) + EPS)
        h_ref[...] = (xv * r * w_ref[...]).astype(BF16)

    return pl.pallas_call(
        body, grid=(t // tm,),
        in_specs=[pl.BlockSpec((tm, d), lambda i: (i, 0)), pl.BlockSpec((1, d), lambda i: (0, 0))],
        out_specs=pl.BlockSpec((tm, d), lambda i: (i, 0)), out_shape=SDS((t, d), BF16), name=name,
        compiler_params=_params(("parallel",)),
    )(x, w.reshape(1, d))


def _rmsnorm_bwd(name, x, w, dh, dres):
    t, d = x.shape
    tm = min(512, t)

    def body(x_ref, w_ref, dh_ref, dres_ref, dx_ref, dxb_ref, dw_ref):
        xv = x_ref[...]
        r = lax.rsqrt(jnp.mean(xv * xv, axis=-1, keepdims=True) + EPS)
        xh = xv * r
        dhv = dh_ref[...].astype(F32)
        dxh = dhv * w_ref[...]
        dx = r * (dxh - xh * jnp.mean(dxh * xh, axis=-1, keepdims=True)) + dres_ref[...]
        dx_ref[...] = dx
        dxb_ref[...] = dx.astype(BF16)

        @pl.when(pl.program_id(0) == 0)
        def _():
            dw_ref[...] = jnp.zeros_like(dw_ref)

        dw_ref[...] += _colsum(dhv * xh)

    row = pl.BlockSpec((tm, d), lambda i: (i, 0))
    vec = pl.BlockSpec((1, d), lambda i: (0, 0))
    return pl.pallas_call(
        body, grid=(t // tm,), in_specs=[row, vec, row, row], out_specs=(row, row, vec),
        out_shape=(SDS((t, d), F32), SDS((t, d), BF16), SDS((1, d), F32)), name=name,
        compiler_params=_params(("arbitrary",)),
    )(x, w.reshape(1, d), dh, dres)


def _loss_head(name, x, w, tgt):
    t, d = x.shape
    tm = min(512, t)

    def body(x_ref, w_ref, t_ref, loss_ref, dx_ref, dxb_ref, dw_ref):
        xv = x_ref[...]
        wv = w_ref[...]
        r = lax.rsqrt(jnp.mean(xv * xv, axis=-1, keepdims=True) + EPS)
        xh = xv * r
        e = xh * wv - t_ref[...]
        dy = e * (1.0 / d)
        dxh = dy * wv
        dx = r * (dxh - xh * jnp.mean(dxh * xh, axis=-1, keepdims=True))
        dx_ref[...] = dx
        dxb_ref[...] = dx.astype(BF16)

        @pl.when(pl.program_id(0) == 0)
        def _():
            dw_ref[...] = jnp.zeros_like(dw_ref)
            loss_ref[...] = jnp.zeros_like(loss_ref)

        dw_ref[...] += _colsum(dy * xh)
        loss_ref[...] += 0.5 * jnp.sum(jnp.mean(e * e, axis=-1, keepdims=True), axis=0, keepdims=True)

    row = pl.BlockSpec((tm, d), lambda i: (i, 0))
    vec = pl.BlockSpec((1, d), lambda i: (0, 0))
    return pl.pallas_call(
        body, grid=(t // tm,), in_specs=[row, vec, row],
        out_specs=(pl.BlockSpec((1, 1), lambda i: (0, 0)), row, row, vec),
        out_shape=(SDS((1, 1), F32), SDS((t, d), F32), SDS((t, d), BF16), SDS((1, d), F32)), name=name,
        compiler_params=_params(("arbitrary",)),
    )(x, w.reshape(1, d), tgt)


def _shift_down(u, s, row):
    return jnp.where(row >= s, pltpu.roll(u, shift=s, axis=0), 0.0)


def _conv_fwd(name, src, col0, w, b):
    t = src.shape[0]
    c = w.shape[1]
    tc = 256
    assert c % tc == 0 and col0 % tc == 0

    def body(u_ref, w_ref, b_ref, o_ref, pre_ref):
        u = u_ref[...]
        wv = w_ref[...]
        row = lax.broadcasted_iota(jnp.int32, u.shape, 0)
        pre = b_ref[...] + wv[3:4, :] * u
        for s in range(1, CONV_K):
            pre = pre + wv[3 - s: 4 - s, :] * _shift_down(u, s, row)
        pre_ref[...] = pre
        o_ref[...] = pre * _sigmoid(pre)

    col = pl.BlockSpec((t, tc), lambda j: (0, j))
    return pl.pallas_call(
        body, grid=(c // tc,),
        in_specs=[pl.BlockSpec((t, tc), lambda j: (0, j + col0 // tc)), pl.BlockSpec((CONV_K, tc), lambda j: (0, j)),
                  pl.BlockSpec((1, tc), lambda j: (0, j))],
        out_specs=(col, col), out_shape=(SDS((t, c), F32), SDS((t, c), F32)), name=name,
        compiler_params=_params(("parallel",)),
    )(src, w, b)


def _place_small(name, dsm_a, dsm_b, dproj):
    t = dsm_a.shape[0]
    width = PROJ_W - SM_OFF
    tr = min(512, t)

    def body(a_ref, b_ref, dproj_ref, o_ref):
        o_ref[:, :128] = a_ref[...] + b_ref[...]
        o_ref[:, 128:] = jnp.zeros((tr, width - 128), BF16)

    row = pl.BlockSpec((tr, 128), lambda i: (i, 0))
    return pl.pallas_call(
        body, grid=(t // tr,), in_specs=[row, row, ANY],
        out_specs=pl.BlockSpec((tr, width), lambda i: (i, SM_OFF // width)), out_shape=SDS(dproj.shape, BF16),
        input_output_aliases={2: 0}, name=name, compiler_params=_params(("parallel",)),
    )(dsm_a, dsm_b, dproj)


def _conv_bwd(name, src, col0, w, pre, dact, dproj):
    t = src.shape[0]
    c = w.shape[1]
    tc = 128

    def body(u_ref, w_ref, pre_ref, da_ref, dproj_ref, du_ref, dw_ref, db_ref):
        u = u_ref[...]
        wv = w_ref[...]
        prev = pre_ref[...]
        row = lax.broadcasted_iota(jnp.int32, u.shape, 0)
        sg = _sigmoid(prev)
        dpre = da_ref[...] * (sg * (1.0 + prev * (1.0 - sg)))
        du = wv[3:4, :] * dpre
        dw_ref[3:4, :] = _colsum(dpre * u)
        for s in range(1, CONV_K):
            up = jnp.where(row < t - s, pltpu.roll(dpre, shift=t - s, axis=0), 0.0)
            du = du + wv[3 - s: 4 - s, :] * up
            dw_ref[3 - s: 4 - s, :] = _colsum(up * u)
        du_ref[...] = du.astype(BF16)
        db_ref[...] = _colsum(dpre)

    col = pl.BlockSpec((t, tc), lambda j: (0, j))
    return pl.pallas_call(
        body, grid=(c // tc,),
        in_specs=[pl.BlockSpec((t, tc), lambda j: (0, j + col0 // tc)), pl.BlockSpec((CONV_K, tc), lambda j: (0, j)),
                  col, col, ANY],
        out_specs=(pl.BlockSpec((t, tc), lambda j: (0, j + col0 // tc)), pl.BlockSpec((CONV_K, tc), lambda j: (0, j)),
                   pl.BlockSpec((1, tc), lambda j: (0, j))),
        out_shape=(SDS(dproj.shape, BF16), SDS((CONV_K, c), F32), SDS((1, c), F32)), name=name,
        input_output_aliases={4: 0},
        compiler_params=_params(("parallel",)),
    )(src, w, pre, dact, dproj)


def _tri(q):
    ii = lax.broadcasted_iota(jnp.int32, (q, q), 0)
    jj = lax.broadcasted_iota(jnp.int32, (q, q), 1)
    return ii, jj


def _dot01(x, r01, dims, terms=3):
    out, rem = None, x
    for i in range(terms):
        hi = rem.astype(BF16)
        d = lax.dot_general(hi, r01, (dims, ((), ())), preferred_element_type=F32)
        out = d if out is None else out + d
        if i + 1 < terms:
            rem = rem - hi.astype(F32)
    return out


def _ssd_common(act, sm, dtb, arow, rmat):
    q = CHUNK
    ii, jj = _tri(q)
    lane = lax.broadcasted_iota(jnp.int32, (q, 128), 1)
    m16 = lane < SSM_HEADS
    dt = jnp.where(m16, _softplus(sm + dtb), 0.0)
    a = dt * arow
    tril = (ii >= jj).astype(F32)
    triu = (ii <= jj).astype(F32)
    acum = _hdot(tril, a)
    acum_r = _hdot(a.T, triu)
    dtx = _dot01(dt, rmat, NN)
    acx = _dot01(acum, rmat, NN)
    ex = jnp.exp(acx)
    alx = acx[q - 1: q, :]
    dex = jnp.exp(alx - acx)
    xs = act[:, :1024]
    return dict(ii=ii, jj=jj, m16=m16, dt=dt, a=a, triu=triu, acum=acum, acum_r=acum_r, dtx=dtx, ex=ex, dex=dex,
                elx=jnp.exp(alx), xs=xs, x=xs * dtx)


def _ssd_lmat(cm, h):
    return jnp.where(cm["ii"] >= cm["jj"], jnp.exp(cm["acum"][:, h: h + 1] - cm["acum_r"][h: h + 1, :]), 0.0)


def _ssd_fwd(name, act, proj, dtb, arow, dxrow, nw, rmat):
    t = act.shape[0]
    q = CHUNK
    nc = t // q
    hg = SSM_HEADS // SSM_GROUPS
    gw = hg * SSM_P

    def body(act_ref, z_ref, sm_ref, dtb_ref, arow_ref, dx_ref, nw_ref, r_ref, y_ref, ys_ref, st_ref, s_scr, yd_scr):
        @pl.when(pl.program_id(0) == 0)
        def _():
            s_scr[...] = jnp.zeros_like(s_scr)

        s_all = s_scr[...]
        for sub in range(cps):
            rows = pl.ds(sub * q, q)
            s_all = chunk(act_ref.at[rows, :], z_ref.at[rows, :], sm_ref.at[rows, :], dtb_ref, arow_ref, dx_ref, nw_ref, r_ref,
                          y_ref.at[rows, :], ys_ref.at[rows, :], st_ref.at[sub], yd_scr.at[rows, :], s_all)
        s_scr[...] = s_all

    def chunk(act_ref, z_ref, sm_ref, dtb_ref, arow_ref, dx_ref, nw_ref, r_ref, y_ref, ys_ref, st_ref, yd_scr, s_all):
        st_ref[...] = s_all
        actv = act_ref[...]
        cm = _ssd_common(actv, sm_ref[...], dtb_ref[...], arow_ref[...], r_ref[...])
        x = cm["x"]
        xd = x * cm["dex"]
        yoffs, snew = [], []
        for g in range(SSM_GROUPS):
            bg = actv[:, 1024 + g * SSM_N: 1024 + (g + 1) * SSM_N]
            cg = actv[:, 1280 + g * SSM_N: 1280 + (g + 1) * SSM_N]
            sg = s_all[:, g * gw: (g + 1) * gw]
            cb = _bdot(cg, bg, NT)
            yoffs.append(_bdot(cg, sg, NN))
            snew.append(_bdot(bg, xd[:, g * gw: (g + 1) * gw], TN))
            for r in range(hg):
                h = g * hg + r
                mm = cb * _ssd_lmat(cm, h)
                yd_scr[:, h * SSM_P: (h + 1) * SSM_P] = _bdot(mm, x[:, h * SSM_P: (h + 1) * SSM_P], NN)
        s_next = s_all * cm["elx"] + jnp.concatenate(snew, axis=1)
        ysc = yd_scr[...] + jnp.concatenate(yoffs, axis=1) * cm["ex"]
        ys_ref[...] = ysc
        zv = z_ref[...]
        yg = (ysc + dx_ref[...] * cm["xs"]) * (zv * _sigmoid(zv))
        nwv = nw_ref[...]
        for g in range(SSM_GROUPS):
            sl = yg[:, g * gw: (g + 1) * gw]
            rr = lax.rsqrt(jnp.mean(sl * sl, axis=-1, keepdims=True) + EPS)
            y_ref[:, g * gw: (g + 1) * gw] = (sl * rr * nwv[:, g * gw: (g + 1) * gw]).astype(BF16)
        return s_next

    cps = SCAN_CHUNKS_PER_STEP if nc % SCAN_CHUNKS_PER_STEP == 0 else 1
    qq = cps * q
    vec128 = pl.BlockSpec((1, 128), lambda c: (0, 0))
    vec1k = pl.BlockSpec((1, 1024), lambda c: (0, 0))
    return pl.pallas_call(
        body, grid=(nc // cps,),
        in_specs=[pl.BlockSpec((qq, SSM_CONV), lambda c: (c, 0)), pl.BlockSpec((qq, 1024), lambda c: (c, Z_OFF // 1024)),
                  pl.BlockSpec((qq, 128), lambda c: (c, SM_OFF // 128)), vec128, vec128, vec1k, vec1k,
                  pl.BlockSpec((128, 1024), lambda c: (0, 0))],
        out_specs=(pl.BlockSpec((qq, 1024), lambda c: (c, 0)), pl.BlockSpec((qq, 1024), lambda c: (c, 0)),
                   pl.BlockSpec((cps, 128, 1024), lambda c: (c, 0, 0))),
        out_shape=(SDS((t, 1024), BF16), SDS((t, 1024), F32), SDS((nc, 128, 1024), F32)),
        scratch_shapes=[pltpu.VMEM((128, 1024), F32), pltpu.VMEM((qq, 1024), F32)], name=name,
        compiler_params=_params(("arbitrary",)),
    )(act, proj, proj, dtb, arow, dxrow, nw, rmat)


def _ssd_bwd(name, act, proj, dtb, arow, dxrow, nw, rmat, ysc, states, dy, dproj):
    t = act.shape[0]
    q = CHUNK
    nc = t // q
    hg = SSM_HEADS // SSM_GROUPS
    gw = hg * SSM_P

    def body(act_ref, z_ref, sm_ref, dtb_ref, arow_ref, dx_ref, nw_ref, r_ref, ys_ref, st_ref, dy_ref, dproj_ref,
             dact_ref, dz_ref, dsm_ref, dnw_ref, dd_ref, dal_ref, ddtb_ref, ds_scr, dxd_scr):
        @pl.when(pl.program_id(0) == 0)
        def _():
            ds_scr[...] = jnp.zeros_like(ds_scr)
            dnw_ref[...] = jnp.zeros_like(dnw_ref)
            dd_ref[...] = jnp.zeros_like(dd_ref)
            dal_ref[...] = jnp.zeros_like(dal_ref)
            ddtb_ref[...] = jnp.zeros_like(ddtb_ref)

        dsn = ds_scr[...]
        for sub in reversed(range(cps)):
            rows = pl.ds(sub * q, q)
            dsn = chunk(act_ref.at[rows, :], z_ref.at[rows, :], sm_ref.at[rows, :], dtb_ref, arow_ref, dx_ref, nw_ref, r_ref,
                        ys_ref.at[rows, :], st_ref.at[sub], dy_ref.at[rows, :], dact_ref.at[rows, :], dz_ref.at[rows, :],
                        dsm_ref.at[rows, :], dnw_ref, dd_ref, dal_ref, ddtb_ref, dxd_scr.at[rows, :], dsn)
        ds_scr[...] = dsn

    def chunk(act_ref, z_ref, sm_ref, dtb_ref, arow_ref, dx_ref, nw_ref, r_ref, ys_ref, st_ref, dy_ref,
              dact_ref, dz_ref, dsm_ref, dnw_ref, dd_ref, dal_ref, ddtb_ref, dxd_scr, dsn):
        actv = act_ref[...]
        smv = sm_ref[...]
        rmat_v = r_ref[...]
        cm = _ssd_common(actv, smv, dtb_ref[...], arow_ref[...], rmat_v)
        ii, jj = cm["ii"], cm["jj"]
        x, xs = cm["x"], cm["xs"]
        s_all = st_ref[...]
        ysv = ys_ref[...]
        dxr = dx_ref[...]
        y = ysv + dxr * xs
        zv = z_ref[...]
        sz = _sigmoid(zv)
        silz = zv * sz
        yg = y * silz
        dout = dy_ref[...]
        nwv = nw_ref[...]
        dyn = dout * nwv
        yn_parts, dyg_parts = [], []
        for g in range(SSM_GROUPS):
            sl = yg[:, g * gw: (g + 1) * gw]
            rr = lax.rsqrt(jnp.mean(sl * sl, axis=-1, keepdims=True) + EPS)
            yn = sl * rr
            dn = dyn[:, g * gw: (g + 1) * gw]
            yn_parts.append(yn)
            dyg_parts.append(rr * (dn - yn * jnp.mean(dn * yn, axis=-1, keepdims=True)))
        dnw_ref[...] += _colsum(dout * jnp.concatenate(yn_parts, axis=1))
        dyg = jnp.concatenate(dyg_parts, axis=1)
        dyv = dyg * silz
        dz_ref[...] = (dyg * y * (sz * (1.0 + zv * (1.0 - sz)))).astype(BF16)
        dd_ref[...] += _dot01(_colsum(dyv * xs), rmat_v, NT)
        dxs = dyv * dxr
        dcs = dyv * cm["ex"]
        xd = x * cm["dex"]
        dxst_parts, ds_parts, db_parts, dc_parts, yoff_parts, wcol_rows = [], [], [], [], [], []
        lane128 = lax.broadcasted_iota(jnp.int32, (q, 128), 1)
        wrow = jnp.zeros((q, 128), F32)
        for g in range(SSM_GROUPS):
            bg = actv[:, 1024 + g * SSM_N: 1024 + (g + 1) * SSM_N]
            cg = actv[:, 1280 + g * SSM_N: 1280 + (g + 1) * SSM_N]
            sg = s_all[:, g * gw: (g + 1) * gw]
            dsng = dsn[:, g * gw: (g + 1) * gw]
            dcsg = dcs[:, g * gw: (g + 1) * gw]
            dcg = _bdot(dcsg, sg, NT)
            yoff_parts.append(_bdot(cg, sg, NN))
            ds_parts.append(_bdot(cg, dcsg, TN))
            dxst_parts.append(_bdot(bg, dsng, NN))
            dbg = _bdot(xd[:, g * gw: (g + 1) * gw], dsng, NT)
            cb = _bdot(cg, bg, NT)
            dcb = jnp.zeros((q, q), F32)
            for r in range(hg):
                h = g * hg + r
                lm = _ssd_lmat(cm, h)
                mm = cb * lm
                dyh = dyv[:, h * SSM_P: (h + 1) * SSM_P]
                dm = jnp.where(ii >= jj, _bdot(dyh, x[:, h * SSM_P: (h + 1) * SSM_P], NT), 0.0)
                dxd_scr[:, h * SSM_P: (h + 1) * SSM_P] = _bdot(mm, dyh, TN)
                dcb = dcb + dm * lm
                wm = dm * mm
                wrow = wrow + jnp.where(lane128 == h, _rowsum(wm), 0.0)
                wcol_rows.append(_colsum(wm))
            dc_parts.append(dcg + _bdot(dcb, bg, NN))
            db_parts.append(dbg + _bdot(dcb, cg, TN))
        dxst = jnp.concatenate(dxst_parts, axis=1) * cm["dex"]
        dx = dxd_scr[...] + dxst
        ds_prev = jnp.concatenate(ds_parts, axis=1) + dsn * cm["elx"]
        wcol = jnp.concatenate(wcol_rows + [jnp.zeros((128 - SSM_HEADS, q), F32)], axis=0).T
        yoff = jnp.concatenate(yoff_parts, axis=1) * cm["ex"]
        xdxst = x * dxst
        dac = wrow - wcol + _dot01(dyv * yoff - xdxst, rmat_v, NT)
        last = _dot01(_colsum(dsn * s_all) * cm["elx"] + _colsum(xdxst), rmat_v, NT)
        rowq = lax.broadcasted_iota(jnp.int32, (q, 128), 0)
        dac = dac + jnp.where(rowq == q - 1, last, 0.0)
        da = _hdot(cm["triu"], dac)
        arow_v = arow_ref[...]
        ddt = da * arow_v + _dot01(dx * xs, rmat_v, NT)
        dxs = dxs + dx * cm["dtx"]
        dal_ref[...] += _colsum(da * cm["a"])
        ddtraw = jnp.where(cm["m16"], ddt * _sigmoid(smv + dtb_ref[...]), 0.0)
        ddtb_ref[...] += _colsum(ddtraw)
        dsm_ref[...] = ddtraw.astype(BF16)
        dact_ref[:, :1024] = dxs
        for g in range(SSM_GROUPS):
            dact_ref[:, 1024 + g * SSM_N: 1024 + (g + 1) * SSM_N] = db_parts[g]
            dact_ref[:, 1280 + g * SSM_N: 1280 + (g + 1) * SSM_N] = dc_parts[g]
        return ds_prev

    cps = SCAN_CHUNKS_PER_STEP if nc % SCAN_CHUNKS_PER_STEP == 0 else 1
    qq = cps * q
    rev = lambda c: nc // cps - 1 - c
    vec128 = pl.BlockSpec((1, 128), lambda c: (0, 0))
    vec1k = pl.BlockSpec((1, 1024), lambda c: (0, 0))
    return pl.pallas_call(
        body, grid=(nc // cps,),
        in_specs=[pl.BlockSpec((qq, SSM_CONV), lambda c: (rev(c), 0)),
                  pl.BlockSpec((qq, 1024), lambda c: (rev(c), Z_OFF // 1024)),
                  pl.BlockSpec((qq, 128), lambda c: (rev(c), SM_OFF // 128)), vec128, vec128, vec1k, vec1k,
                  pl.BlockSpec((128, 1024), lambda c: (0, 0)),
                  pl.BlockSpec((qq, 1024), lambda c: (rev(c), 0)), pl.BlockSpec((cps, 128, 1024), lambda c: (rev(c), 0, 0)),
                  pl.BlockSpec((qq, 1024), lambda c: (rev(c), 0)), ANY],
        out_specs=(pl.BlockSpec((qq, SSM_CONV), lambda c: (rev(c), 0)),
                   pl.BlockSpec((qq, 1024), lambda c: (rev(c), Z_OFF // 1024)),
                   pl.BlockSpec((qq, 128), lambda c: (rev(c), 0)), vec1k, vec128, vec128, vec128),
        out_shape=(SDS((t, SSM_CONV), F32), SDS(dproj.shape, BF16), SDS((t, 128), BF16), SDS((1, 1024), F32),
                   SDS((1, 128), F32), SDS((1, 128), F32), SDS((1, 128), F32)),
        input_output_aliases={11: 1},
        scratch_shapes=[pltpu.VMEM((128, 1024), F32), pltpu.VMEM((qq, 1024), F32)], name=name,
        compiler_params=_params(("arbitrary",)),
    )(act, proj, proj, dtb, arow, dxrow, nw, rmat, ysc, states, dy, dproj)


def _split(a):
    hi = a.astype(BF16)
    return hi, (a - hi.astype(F32)).astype(BF16)


def _dot3(a, b, dims=NN):
    (ah, al), (bh, bl) = a, b

    def d(x, y):
        return lax.dot_general(x, y, (dims, ((), ())), preferred_element_type=F32)

    return d(ah, bh) + (d(ah, bl) + d(al, bh))


def _tri_inverses(amats, ii, jj):
    eye = jnp.where(ii == jj, 1.0, 0.0)
    tms = [eye - a for a in amats]
    sp = [_split(a) for a in amats]
    for _ in range(5):
        sp = [_split(_dot3(s, s)) for s in sp]
        tms = [t + _dot3(_split(t), s) for t, s in zip(tms, sp)]
    return tms


def _gdn_common(sm, gb, garow):
    q = CHUNK
    ii, jj = _tri(q)
    lane = lax.broadcasted_iota(jnp.int32, (q, 128), 1)
    ma = (lane >= LANE_A) & (lane < LANE_A + GDN_HEADS)
    spre = sm + gb
    g = jnp.where(ma, garow * _softplus(spre), 0.0)
    beta = _sigmoid(sm)
    tril = (ii >= jj).astype(F32)
    triu = (ii <= jj).astype(F32)
    gc = _hdot(tril, g)
    gc_r = _hdot(g.T, triu)
    return dict(ii=ii, jj=jj, lane=lane, ma=ma, spre=spre, g=g, beta=beta, triu=triu, gc=gc, gc_r=gc_r)


def _each(f, *lists):
    return [f(*xs) for xs in zip(*lists)]


GDN_SCALE = GDN_DK ** -0.5
GDN_BWD_HEAD_GROUPS = (range(GDN_HEADS),)


def _gdn_heads(cm, actv, states, heads=range(GDN_HEADS)):
    q = CHUNK
    ii, jj = cm["ii"], cm["jj"]
    qr = [actv[:, h * 128: (h + 1) * 128] for h in heads]
    kr = [actv[:, 1024 + h * 128: 1024 + (h + 1) * 128] for h in heads]
    v = [actv[:, 2048 + h * 128: 2048 + (h + 1) * 128] for h in heads]
    rq = _each(lambda x: lax.rsqrt(_rowsum(x * x) + EPS), qr)
    rk = _each(lambda x: lax.rsqrt(_rowsum(x * x) + EPS), kr)
    qn = _each(lambda x, r: x * r * GDN_SCALE, qr, rq)
    kn = _each(lambda x, r: x * r, kr, rk)
    gcc = [cm["gc"][:, LANE_A + h: LANE_A + h + 1] for h in heads]
    gcr = [cm["gc_r"][LANE_A + h: LANE_A + h + 1, :] for h in heads]
    bcol = [cm["beta"][:, LANE_B + h: LANE_B + h + 1] for h in heads]
    dm = _each(lambda c, r: jnp.where(ii >= jj, jnp.exp(c - r), 0.0), gcc, gcr)
    kq = _each(lambda k, a: _bdot(jnp.concatenate([k, a], axis=0), k, NT), kn, qn)
    ak = _each(lambda x, d: jnp.where(ii > jj, x[:q] * d, 0.0), kq, dm)
    qkm = _each(lambda x, d: jnp.where(ii >= jj, x[q:] * d, 0.0), kq, dm)
    tm = _tri_inverses(_each(lambda a, b: a * b, ak, bcol), ii, jj)
    eg = _each(jnp.exp, gcc)
    gl = [c[q - 1: q, :] for c in gcc]
    rm = _each(lambda vv, k, b, e: jnp.concatenate([vv * b, k * (b * e)], axis=1), v, kn, bcol, eg)
    tt = _each(lambda t, r: _dot3(_split(t), _split(r)), tm, rm)
    w = [t[:, 128:] for t in tt]
    qg = _each(lambda a, e: a * e, qn, eg)
    ws = _each(lambda ww, a, s: _bdot(jnp.concatenate([ww, a], axis=0), s, NN), w, qg, states)
    vnew = _each(lambda t, x: t[:, :128] - x[:q], tt, ws)
    return dict(qr=qr, v=v, rq=rq, rk=rk, qn=qn, kn=kn, gcc=gcc, bcol=bcol, dm=dm, ak=ak, tm=tm, eg=eg, gl=gl,
                egl=_each(jnp.exp, gl), ed=_each(lambda g, c: jnp.exp(g - c), gl, gcc), tt=tt, w=w, vnew=vnew, qkm=qkm,
                qg=qg, qgs=[x[q:] for x in ws])


def _gdn_fwd(name, act, proj, gb, garow, gnw):
    t = act.shape[0]
    q = CHUNK
    nc = t // q

    def body(act_ref, gz_ref, sm_ref, gb_ref, ga_ref, nw_ref, y_ref, o_ref, st_ref, s_scr):
        @pl.when(pl.program_id(0) == 0)
        def _():
            s_scr[...] = jnp.zeros_like(s_scr)

        states = [s_scr[h * 128: (h + 1) * 128, :] for h in range(GDN_HEADS)]
        for sub in range(cps):
            rows = pl.ds(sub * q, q)
            states = chunk(act_ref.at[rows, :], gz_ref.at[rows, :], sm_ref.at[rows, :], gb_ref, ga_ref, nw_ref,
                           y_ref.at[rows, :], o_ref.at[rows, :], st_ref.at[sub], states)
        for h in range(GDN_HEADS):
            s_scr[h * 128: (h + 1) * 128, :] = states[h]

    def chunk(act_ref, gz_ref, sm_ref, gb_ref, ga_ref, nw_ref, y_ref, o_ref, st_ref, states):
        for h in range(GDN_HEADS):
            st_ref[h * 128: (h + 1) * 128, :] = states[h]
        actv = act_ref[...]
        cm = _gdn_common(sm_ref[...], gb_ref[...], ga_ref[...])
        nwv = nw_ref[...]
        gzv = gz_ref[...]
        hd = _gdn_heads(cm, actv, states)
        outs = _each(lambda qs, m, vn: qs + _bdot(m, vn, NN), hd["qgs"], hd["qkm"], hd["vnew"])
        snew = _each(lambda s, e, k, d, vn: s * e + _bdot(k * d, vn, TN), states, hd["egl"], hd["kn"], hd["ed"], hd["vnew"])
        for h in range(GDN_HEADS):
            o = outs[h]
            o_ref[:, h * 128: (h + 1) * 128] = o
            rr = lax.rsqrt(jnp.mean(o * o, axis=-1, keepdims=True) + EPS)
            gz = gzv[:, h * 128: (h + 1) * 128]
            y_ref[:, h * 128: (h + 1) * 128] = (o * rr * nwv * (gz * _sigmoid(gz))).astype(BF16)
        return snew

    cps = SCAN_CHUNKS_PER_STEP if nc % SCAN_CHUNKS_PER_STEP == 0 else 1
    qq = cps * q
    vec128 = pl.BlockSpec((1, 128), lambda c: (0, 0))
    return pl.pallas_call(
        body, grid=(nc // cps,),
        in_specs=[pl.BlockSpec((qq, GDN_QKV), lambda c: (c, 0)), pl.BlockSpec((qq, 1024), lambda c: (c, GZ_OFF // 1024)),
                  pl.BlockSpec((qq, 128), lambda c: (c, SM_OFF // 128)), vec128, vec128, vec128],
        out_specs=(pl.BlockSpec((qq, 1024), lambda c: (c, 0)), pl.BlockSpec((qq, 1024), lambda c: (c, 0)),
                   pl.BlockSpec((cps, 1024, 128), lambda c: (c, 0, 0))),
        out_shape=(SDS((t, 1024), BF16), SDS((t, 1024), F32), SDS((nc, 1024, 128), F32)),
        scratch_shapes=[pltpu.VMEM((1024, 128), F32)], name=name, compiler_params=_params(("arbitrary",)),
    )(act, proj, proj, gb, garow, gnw)


def _gdn_bwd(name, act, proj, gb, garow, gnw, oraw, states, dy, dproj):
    t = act.shape[0]
    q = CHUNK
    nc = t // q

    def body(act_ref, gz_ref, sm_ref, gb_ref, ga_ref, nw_ref, o_ref, st_ref, dy_ref, dproj_ref,
             dact_ref, dgz_ref, dsm_ref, dnw_ref, dal_ref, dgb_ref, ds_scr):
        @pl.when(pl.program_id(0) == 0)
        def _():
            ds_scr[...] = jnp.zeros_like(ds_scr)
            dnw_ref[...] = jnp.zeros_like(dnw_ref)
            dal_ref[...] = jnp.zeros_like(dal_ref)
            dgb_ref[...] = jnp.zeros_like(dgb_ref)

        dsn = [ds_scr[h * 128: (h + 1) * 128, :] for h in range(GDN_HEADS)]
        for sub in reversed(range(cps)):
            rows = pl.ds(sub * q, q)
            dsn = chunk(act_ref.at[rows, :], gz_ref.at[rows, :], sm_ref.at[rows, :], gb_ref, ga_ref, nw_ref,
                        o_ref.at[rows, :], st_ref.at[sub], dy_ref.at[rows, :],
                        dact_ref.at[rows, :], dgz_ref.at[rows, :], dsm_ref.at[rows, :], dnw_ref, dal_ref, dgb_ref, dsn)
        for h in range(GDN_HEADS):
            ds_scr[h * 128: (h + 1) * 128, :] = dsn[h]

    def chunk(act_ref, gz_ref, sm_ref, gb_ref, ga_ref, nw_ref, o_ref, st_ref, dy_ref,
              dact_ref, dgz_ref, dsm_ref, dnw_ref, dal_ref, dgb_ref, dsn):
        actv = act_ref[...]
        smv = sm_ref[...]
        garow_v = ga_ref[...]
        cm = _gdn_common(smv, gb_ref[...], garow_v)
        ii, jj, lane = cm["ii"], cm["jj"], cm["lane"]
        nwv = nw_ref[...]
        rowq = lax.broadcasted_iota(jnp.int32, (q, 1), 0)
        dgc_all = jnp.zeros((q, 128), F32)
        dbeta_all = jnp.zeros((q, 128), F32)
        dnw_acc = jnp.zeros((1, 128), F32)
        ds_out = []
        ov, gzv, dyv = o_ref[...], gz_ref[...], dy_ref[...]
        for heads in GDN_BWD_HEAD_GROUPS:
            part = group(heads, cm, actv, ov, gzv, dyv, nwv, rowq, st_ref, [dsn[h] for h in heads], dact_ref, dgz_ref)
            ds_out += part[0]
            dgc_all, dbeta_all, dnw_acc = dgc_all + part[1], dbeta_all + part[2], dnw_acc + part[3]
        dnw_ref[...] += dnw_acc
        dg = _hdot(cm["triu"], dgc_all)
        da_raw = jnp.where(cm["ma"], dg * garow_v * _sigmoid(cm["spre"]), 0.0)
        dal_ref[...] += _colsum(dg * cm["g"])
        dgb_ref[...] += _colsum(da_raw)
        beta = cm["beta"]
        dsm_ref[...] = (da_raw + dbeta_all * beta * (1.0 - beta)).astype(BF16)
        return ds_out

    def group(heads, cm, actv, ov, gzv, dyv, nwv, rowq, st_ref, dsn, dact_ref, dgz_ref):
        ii, jj, lane = cm["ii"], cm["jj"], cm["lane"]
        dgc_all = jnp.zeros((q, 128), F32)
        dbeta_all = jnp.zeros((q, 128), F32)
        dnw_acc = jnp.zeros((1, 128), F32)
        ds_out = []
        sts = [st_ref[h * 128: (h + 1) * 128, :] for h in heads]
        hd = _gdn_heads(cm, actv, sts, heads)
        qn, kn, v, eg, ed, egl, bcol = hd["qn"], hd["kn"], hd["v"], hd["eg"], hd["ed"], hd["egl"], hd["bcol"]
        vnew, qkm, qg, w, tt, dm, ak = hd["vnew"], hd["qkm"], hd["qg"], hd["w"], hd["tt"], hd["dm"], hd["ak"]
        do = []
        for h in heads:
            hs = slice(h * 128, (h + 1) * 128)
            o = ov[:, hs]
            rr = lax.rsqrt(jnp.mean(o * o, axis=-1, keepdims=True) + EPS)
            on = o * rr
            gz = gzv[:, hs]
            sz = _sigmoid(gz)
            silz = gz * sz
            dyh = dyv[:, hs]
            dnw_acc = dnw_acc + _colsum(dyh * on * silz)
            dgz_ref[:, hs] = (dyh * on * nwv * (sz * (1.0 + gz * (1.0 - sz)))).astype(BF16)
            don = dyh * nwv * silz
            do.append(rr * (don - on * jnp.mean(don * on, axis=-1, keepdims=True)))
        kd = _each(lambda k, e: k * e, kn, ed)
        dkd = _each(lambda vn, d: _bdot(vn, d, NT), vnew, dsn)
        dvnew_a = _each(lambda k, d: _bdot(k, d, NN), kd, dsn)
        ded = _each(lambda a, b: _rowsum(a * b), dkd, kd)
        dgl = _each(lambda d, s, e, de: jnp.sum(_rowsum(d * s), axis=0, keepdims=True) * e + _colsum(de), dsn, sts, egl, ded)
        dqk = _each(lambda d, vn: jnp.where(ii >= jj, _bdot(d, vn, NT), 0.0), do, vnew)
        dvnew = _each(lambda a, m, d: a + _bdot(m, d, TN), dvnew_a, qkm, do)
        pq = _each(lambda a, b: a * b, dqk, dm)
        w1 = _each(lambda a, b: a * b, dqk, qkm)
        dod = _each(lambda a, b: jnp.concatenate([a, b], axis=0), do, dvnew)
        dos = _each(lambda x, s: _bdot(x, s, NT), dod, sts)
        dqg = [x[:q] for x in dos]
        dw = [-x[q:] for x in dos]
        ds12 = _each(lambda a, ww, x: _bdot(jnp.concatenate([a, -ww], axis=0), x, TN), qg, w, dod)
        dr = _each(lambda t, a, b: _dot3(_split(t), _split(jnp.concatenate([a, b], axis=1)), TN), hd["tm"], dvnew, dw)
        da = _each(lambda r, t: jnp.where(ii > jj, -_dot3(_split(r), _split(t), NT), 0.0), dr, tt)
        sk = _each(lambda r, k: _rowsum(r[:, 128:] * k), dr, kn)
        pk = _each(lambda a, d, b: a * d * b, da, dm, bcol)
        pkn = _each(lambda p, pp, k: _bdot(jnp.concatenate([p, pp + pp.T], axis=0), k, NN), pq, pk, kn)
        dq = _each(lambda a, e, x: a * e + x[:q], dqg, eg, pkn)
        dk = _each(lambda a, e, p, x, r, b, eg_, y: a * e + _bdot(p, x, TN) + r[:, 128:] * (b * eg_) + y[q:],
                   dkd, ed, pq, qn, dr, bcol, eg, pkn)
        w2 = _each(lambda a, k, b: a * (k * b), da, ak, bcol)
        for i, h in enumerate(heads):
            hs = slice(h * 128, (h + 1) * 128)
            dgc = (-ded[i] + _rowsum(dqg[i] * qg[i]) + _rowsum(w1[i]) - _rowsum(w1[i].T) + sk[i] * bcol[i] * eg[i]
                   + _rowsum(w2[i]) - _rowsum(w2[i].T) + jnp.where(rowq == q - 1, dgl[i], 0.0))
            dbeta = _rowsum(dr[i][:, :128] * v[i]) + sk[i] * eg[i] + _rowsum(da[i] * ak[i])
            qhat = hd["qr"][i] * hd["rq"][i]
            dqhat = dq[i] * GDN_SCALE
            dact_ref[:, hs] = hd["rq"][i] * (dqhat - qhat * _rowsum(dqhat * qhat))
            dact_ref[:, 1024 + h * 128: 1024 + (h + 1) * 128] = hd["rk"][i] * (dk[i] - kn[i] * _rowsum(dk[i] * kn[i]))
            dact_ref[:, 2048 + h * 128: 2048 + (h + 1) * 128] = dr[i][:, :128] * bcol[i]
            dgc_all = dgc_all + jnp.where(lane == LANE_A + h, dgc, 0.0)
            dbeta_all = dbeta_all + jnp.where(lane == LANE_B + h, dbeta, 0.0)
            ds_out.append(dsn[i] * egl[i] + ds12[i])
        return ds_out, dgc_all, dbeta_all, dnw_acc

    cps = 2 if nc % 2 == 0 else 1
    qq = cps * q
    rev = lambda c: nc // cps - 1 - c
    vec128 = pl.BlockSpec((1, 128), lambda c: (0, 0))
    return pl.pallas_call(
        body, grid=(nc // cps,),
        in_specs=[pl.BlockSpec((qq, GDN_QKV), lambda c: (rev(c), 0)),
                  pl.BlockSpec((qq, 1024), lambda c: (rev(c), GZ_OFF // 1024)),
                  pl.BlockSpec((qq, 128), lambda c: (rev(c), SM_OFF // 128)), vec128, vec128, vec128,
                  pl.BlockSpec((qq, 1024), lambda c: (rev(c), 0)), pl.BlockSpec((cps, 1024, 128), lambda c: (rev(c), 0, 0)),
                  pl.BlockSpec((qq, 1024), lambda c: (rev(c), 0)), ANY],
        out_specs=(pl.BlockSpec((qq, GDN_QKV), lambda c: (rev(c), 0)),
                   pl.BlockSpec((qq, 1024), lambda c: (rev(c), GZ_OFF // 1024)),
                   pl.BlockSpec((qq, 128), lambda c: (rev(c), 0)), vec128, vec128, vec128),
        out_shape=(SDS((t, GDN_QKV), F32), SDS(dproj.shape, BF16), SDS((t, 128), BF16), SDS((1, 128), F32),
                   SDS((1, 128), F32), SDS((1, 128), F32)),
        input_output_aliases={9: 1},
        scratch_shapes=[pltpu.VMEM((1024, 128), F32)], name=name, compiler_params=_params(("arbitrary",)),
    )(act, proj, proj, gb, garow, gnw, oraw, states, dy, dproj)


def _row_tile(r):
    for cand in (512, 256, 128, 64, 32, 16, 8):
        if r % cand == 0:
            return cand
    return r


def _sum_terms(name, terms, out_dtype):
    shape = terms[0][0].shape[1:]
    c = shape[-1]
    r = 1
    for s in shape[:-1]:
        r *= s
    tr = min(_row_tile(r), 256)
    n = len(terms)

    def body(*refs):
        acc = refs[0][...].astype(F32)
        for k in range(1, n):
            acc = acc + refs[k][...].astype(F32)
        refs[n][...] = acc.astype(out_dtype)

    in_specs = [pl.BlockSpec((None, tr, c), lambda i, q=lead: (q, i, 0)) for _, lead in terms]
    args = [a.reshape(a.shape[0], r, c) for a, _ in terms]
    out = pl.pallas_call(body, grid=(r // tr,), in_specs=in_specs, out_specs=pl.BlockSpec((tr, c), lambda i: (i, 0)),
                         out_shape=SDS((r, c), out_dtype), name=name, compiler_params=_params(("parallel",)))(*args)
    return out.reshape(shape)


def _adamw_math(w, g, m, v):
    mn = ADAM_B1 * m + (1.0 - ADAM_B1) * g
    vn = ADAM_B2 * v + (1.0 - ADAM_B2) * (g * g)
    m_hat = mn / (1.0 - ADAM_B1 ** ADAM_STEP)
    v_hat = vn / (1.0 - ADAM_B2 ** ADAM_STEP)
    return -ADAM_LR * (m_hat / (jnp.sqrt(v_hat) + ADAM_EPS) + ADAM_WD * w), mn, vn


def _adamw_rows(name, w, g, m, v):
    r, a, c = w.shape
    tr = next(t for t in range(min(r, 128), 0, -1) if r % t == 0)

    def body(w_ref, g_ref, m_ref, v_ref, d_ref, nm_ref, nv_ref):
        d_ref[...], nm_ref[...], nv_ref[...] = _adamw_math(w_ref[...], g_ref[...], m_ref[...], v_ref[...])

    spec = pl.BlockSpec((tr, a, c), lambda i: (i, 0, 0))
    return pl.pallas_call(body, grid=(r // tr,), in_specs=[spec] * 4, out_specs=(spec,) * 3,
                          out_shape=(SDS(w.shape, F32),) * 3, name=name, compiler_params=_params(("parallel",)))(w, g, m, v)


def _adamw(name, w, g, m, v):
    shape = w.shape
    c = shape[-1]
    per_layer = isinstance(g, (list, tuple))
    nl = len(g) if per_layer else 1
    gs = [a.reshape(-1, c) for a in g] if per_layer else [g.reshape(-1, c)]
    r = gs[0].shape[0]
    w3, m3, v3 = (a.reshape(nl, r, c) for a in (w, m, v))
    tr = min(_row_tile(r), 256)

    def body(*refs):
        w_ref, m_ref, v_ref = refs[:3]
        g_refs = refs[3: 3 + nl]
        go_ref, d_ref, nm_ref, nv_ref = refs[3 + nl:]
        layer = pl.program_id(0)
        gv = g_refs[0][...]
        for k in range(1, nl):
            gv = jnp.where(layer == k, g_refs[k][...], gv)
        go_ref[...] = gv
        d_ref[...], nm_ref[...], nv_ref[...] = _adamw_math(w_ref[...], gv, m_ref[...], v_ref[...])

    spec3 = pl.BlockSpec((None, tr, c), lambda l, i: (l, i, 0))
    gspec = pl.BlockSpec((tr, c), lambda l, i: (i, 0))
    outs = pl.pallas_call(body, grid=(nl, r // tr), in_specs=[spec3] * 3 + [gspec] * nl, out_specs=(spec3,) * 4,
                          out_shape=(SDS((nl, r, c), F32),) * 4, name=name,
                          compiler_params=_params(("parallel", "parallel")))(w3, m3, v3, *gs)
    return tuple(o.reshape(shape) for o in outs)


ANY = pl.BlockSpec(memory_space=pl.ANY)
MESH = pl.DeviceIdType.MESH


def _allgather(name, xs):
    n = len(xs)

    def body(*refs):
        x_refs, out_refs = refs[:n], refs[n: 2 * n]
        send_sems, recv_sems, local_sems = refs[2 * n:]
        x, y, cc = lax.axis_index("x"), lax.axis_index("y"), lax.axis_index("c")
        me, sibling = (x, y, cc), (x, y, 1 - cc)
        chips = [(1 - x, y), (x, 1 - y), (1 - x, 1 - y)]

        def rows(a, px, py, pc):
            return out_refs[a].at[4 * px + 2 * py + pc]

        def copy(a, k, block, to, src=None):
            return pltpu.make_async_remote_copy(
                src_ref=rows(a, *block) if src is None else src, dst_ref=rows(a, *block),
                send_sem=send_sems.at[7 * a + k], recv_sem=recv_sems.at[7 * a + k], device_id=to, device_id_type=MESH)

        mine = [pltpu.make_async_copy(x_refs[a], rows(a, *me), local_sems.at[a]) for a in range(n)]
        for cp in mine:
            cp.start()
        first = []
        for a in range(n):
            first.append(copy(a, 0, me, sibling, src=x_refs[a]))
            first += [copy(a, 1 + j, me, (*chip, cc), src=x_refs[a]) for j, chip in enumerate(chips)]
        for cp in first:
            cp.start()
        passed = []
        for j, chip in enumerate(chips):
            for a in range(n):
                copy(a, 1 + j, (*chip, cc), me).wait_recv()
                fwd = copy(a, 4 + j, (*chip, cc), sibling)
                fwd.start()
                passed.append(fwd)
        for a in range(n):
            copy(a, 0, sibling, me).wait_recv()
        for j, chip in enumerate(chips):
            for a in range(n):
                copy(a, 4 + j, (*chip, 1 - cc), me).wait_recv()
        for cp in first + passed:
            cp.wait_send()
        for cp in mine:
            cp.wait()

    return pl.pallas_call(
        body, out_shape=tuple(SDS((N_DEV,) + a.shape, a.dtype) for a in xs), in_specs=[ANY] * n, out_specs=(ANY,) * n,
        scratch_shapes=[pltpu.SemaphoreType.DMA((7 * n,)), pltpu.SemaphoreType.DMA((7 * n,)),
                        pltpu.SemaphoreType.DMA((n,))],
        name=name,
    )(*xs)


def _allgather_seq(name, xs, collective_id):
    n = len(xs)
    x_refs = [jax.new_ref(a, memory_space=pltpu.MemorySpace.HBM) for a in xs]
    out_refs = [jax.empty_ref(SDS((N_DEV,) + a.shape, a.dtype), memory_space=pltpu.MemorySpace.HBM) for a in xs]

    @pl.kernel(mesh=plsc.ScalarSubcoreMesh(axis_name="seq", num_cores=1), name=name,
               scratch_types=(pltpu.SemaphoreType.DMA((7 * n,)), pltpu.SemaphoreType.DMA((7 * n,)),
                              pltpu.SemaphoreType.DMA((n,))),
               compiler_params=pltpu.CompilerParams(collective_id=collective_id))
    def launch(send_sems, recv_sems, local_sems):
        x, y, cc = lax.axis_index("x"), lax.axis_index("y"), lax.axis_index("c")
        me, sibling = (x, y, cc), (x, y, 1 - cc)
        chips = [(1 - x, y), (x, 1 - y), (1 - x, 1 - y)]
        barrier = pltpu.get_barrier_semaphore()
        for peer in [sibling] + [(*chip, cc) for chip in chips]:
            pl.semaphore_signal(barrier, inc=1, device_id=peer, device_id_type=MESH)
        pl.semaphore_wait(barrier, 4)

        def rows(a, px, py, pc):
            return out_refs[a].at[4 * px + 2 * py + pc]

        def copy(a, k, block, to, src=None):
            return pltpu.make_async_remote_copy(
                src_ref=rows(a, *block) if src is None else src, dst_ref=rows(a, *block),
                send_sem=send_sems.at[7 * a + k], recv_sem=recv_sems.at[7 * a + k], device_id=to, device_id_type=MESH)

        mine = [pltpu.make_async_copy(x_refs[a], rows(a, *me), local_sems.at[a]) for a in range(n)]
        for cp in mine:
            cp.start()
        first = []
        for a in range(n):
            first.append(copy(a, 0, me, sibling, src=x_refs[a]))
            first += [copy(a, 1 + j, me, (*chip, cc), src=x_refs[a]) for j, chip in enumerate(chips)]
        for cp in first:
            cp.start()
        passed = []
        for j, chip in enumerate(chips):
            for a in range(n):
                copy(a, 1 + j, (*chip, cc), me).wait_recv()
                fwd = copy(a, 4 + j, (*chip, cc), sibling)
                fwd.start()
                passed.append(fwd)
        for a in range(n):
            copy(a, 0, sibling, me).wait_recv()
        for j, chip in enumerate(chips):
            for a in range(n):
                copy(a, 4 + j, (*chip, 1 - cc), me).wait_recv()
        for cp in first + passed:
            cp.wait_send()
        for cp in mine:
            cp.wait()

    launch()
    return [r[...] for r in out_refs]


HBM = pl.BlockSpec(memory_space=pltpu.HBM)
SEM = pl.BlockSpec(memory_space=pltpu.SEMAPHORE)
EFFECT = pltpu.SideEffectType.DATAFLOW_SIDE_EFFECTING


def _sibling_plan(srcs, lands, send_sems, recv_sems):
    x, y, cc = lax.axis_index("x"), lax.axis_index("y"), lax.axis_index("c")
    return [pltpu.make_async_remote_copy(
        src_ref=srcs[a].at[2 * q + 1 - cc], dst_ref=lands[a].at[q], send_sem=send_sems.at[4 * a + q],
        recv_sem=recv_sems.at[4 * a + q], device_id=(x, y, 1 - cc), device_id_type=MESH)
        for a in range(len(srcs)) for q in range(4)]


def _chips_plan(srcs, lands, send_sems, recv_sems):
    x, y, cc = lax.axis_index("x"), lax.axis_index("y"), lax.axis_index("c")
    chips = [(1 - x, y), (x, 1 - y), (1 - x, 1 - y)]
    return [pltpu.make_async_remote_copy(
        src_ref=srcs[a].at[2 * px + py], dst_ref=lands[a].at[j], send_sem=send_sems.at[3 * a + j],
        recv_sem=recv_sems.at[3 * a + j], device_id=(px, py, cc), device_id_type=MESH)
        for a in range(len(srcs)) for j, (px, py) in enumerate(chips)]


def _copies_start(name, plan, per_array, srcs, land_lead):
    n = len(srcs)
    k = per_array * n

    def body(*refs):
        src_refs, land_refs = refs[:n], refs[n: 2 * n]
        send_sems, recv_sems = refs[2 * n], refs[2 * n + 1]
        token = refs[-1]
        for cp in plan(src_refs, land_refs, send_sems, recv_sems):
            cp.start()
        token[...] = jnp.zeros_like(token)

    lands = [lax.empty((land_lead,) + a.shape[1:], a.dtype) for a in srcs]
    outs = pl.pallas_call(
        body, name=name,
        out_shape=(pltpu.SemaphoreType.DMA((k,)), pltpu.SemaphoreType.DMA((k,)),
                   *[pltpu.HBM(a.shape, a.dtype) for a in srcs], *[pltpu.HBM(a.shape, a.dtype) for a in lands],
                   SDS((8, 128), F32)),
        in_specs=[HBM] * (2 * n), out_specs=(SEM, SEM, *[HBM] * (2 * n), pl.BlockSpec(memory_space=pltpu.VMEM)),
        input_output_aliases={i: 2 + i for i in range(2 * n)},
        compiler_params=pltpu.CompilerParams(has_side_effects=EFFECT),
    )(*[pltpu.with_memory_space_constraint(a, pltpu.HBM) for a in srcs],
      *[pltpu.with_memory_space_constraint(a, pltpu.HBM) for a in lands])
    return outs[0], outs[1], list(outs[2: 2 + n]), list(outs[2 + n: 2 + 2 * n]), outs[-1]


def _copies_wait(name, plan, started, after):
    send_sems, recv_sems, srcs, lands, _ = started
    n = len(srcs)
    after = tuple(after)

    def body(*refs):
        src_refs, land_refs = refs[:n], refs[n: 2 * n]
        for cp in plan(src_refs, land_refs, refs[2 * n], refs[2 * n + 1]):
            cp.wait_send()
            cp.wait_recv()

    outs = pl.pallas_call(
        body, name=name,
        out_shape=tuple(pltpu.HBM(a.shape, a.dtype) for a in srcs + lands),
        in_specs=[HBM] * (2 * n) + [SEM, SEM] + [ANY] * len(after), out_specs=(HBM,) * (2 * n),
        input_output_aliases={i: i for i in range(2 * n)},
        compiler_params=pltpu.CompilerParams(has_side_effects=EFFECT),
    )(*srcs, *lands, send_sems, recv_sems, *after)
    return list(outs[n:])


BIG = (("w_in", 1), ("w_ffn_in", 1), ("w_proj_ssm", 0), ("w_proj_gdn", 0), ("w_out", 0), ("w_ffn_down", 0))
CONVS = (("ssm_conv_w", 1), ("gdn_conv_w", 1))


def _to_dest_major(full, axis):
    if isinstance(full, tuple):
        per = N_DEV // len(full)
        s = full[0].shape[1] // per
        return jnp.stack([full[d // per][:, (d % per) * s: (d % per + 1) * s] for d in range(N_DEV)])
    a, b = full.shape
    if axis == 0:
        return full.reshape(N_DEV, a // N_DEV, b)
    s = b // N_DEV
    return jnp.stack([full[:, d * s: (d + 1) * s] for d in range(N_DEV)])


def _from_gathered(g, axis):
    if axis == 0:
        return g.reshape(-1, g.shape[2])
    return jnp.concatenate([g[d] for d in range(N_DEV)], axis=1)


IN_RUNS = ((Z_OFF, O_Z, 1024), (GZ_OFF, O_GZ, 1024), (G1_OFF, O_G1, 1024), (G2_OFF, O_G2, 1024), (QKV_OFF, O_QKV, 3072),
           (XBC_OFF, O_XBC, 1536), (SM_OFF, O_DT, 16), (SM_OFF + LANE_A, O_A, 8), (SM_OFF + LANE_B, O_B, 8))
IN_SHARD = IN_DIM // N_DEV


def _w_in_from_blocks(g):
    rows = g.shape[1]
    parts, pos = [], 0
    for off, o0, width in IN_RUNS:
        if off > pos:
            parts.append(jnp.zeros((rows, off - pos), g.dtype))
        c = o0
        while c < o0 + width:
            d = c // IN_SHARD
            hi = min(o0 + width, (d + 1) * IN_SHARD)
            parts.append(g[d][:, c - d * IN_SHARD: hi - d * IN_SHARD])
            c = hi
        pos = off + width
    parts.append(jnp.zeros((rows, PROJ_W - pos), g.dtype))
    return jnp.concatenate(parts, axis=1)


def _w_in_to_blocks(wp):
    by_orig = sorted(IN_RUNS, key=lambda r: r[1])
    blocks = []
    for d in range(N_DEV):
        lo, hi = d * IN_SHARD, (d + 1) * IN_SHARD
        parts = []
        for off, o0, width in by_orig:
            a, b = max(lo, o0), min(hi, o0 + width)
            if a < b:
                parts.append(wp[:, off + a - o0: off + b - o0])
        blocks.append(jnp.concatenate(parts, axis=1))
    return jnp.stack(blocks)


def _pad128(v, lane0):
    return jnp.zeros((1, 128), F32).at[0, lane0: lane0 + v.shape[0]].set(v)


def _layer_consts(p):
    return dict(
        dtb=_pad128(p["ssm_dt_bias"], 0), arow=_pad128(-jnp.exp(p["ssm_a_log"]), 0),
        dxrow=jnp.repeat(p["ssm_d"], SSM_P).reshape(1, 1024), snw=p["ssm_norm_w"].reshape(1, 1024),
        gb=_pad128(p["gdn_dt_bias"], LANE_A), garow=_pad128(-jnp.exp(p["gdn_a_log"]), LANE_A),
        gnw=p["gdn_norm_w"].reshape(1, 128), zb=jnp.zeros((1, GDN_QKV), F32), scb=p["ssm_conv_b"].reshape(1, SSM_CONV))


def _expand_matrix():
    row = lax.broadcasted_iota(jnp.int32, (128, 1024), 0)
    col = lax.broadcasted_iota(jnp.int32, (128, 1024), 1)
    return (col // SSM_P == row).astype(BF16)


def _silu_mul_epi(gate, up):
    return gate, up, gate * _sigmoid(gate) * up


def _merge_epi(acc, p1, g1, g2):
    return acc, _sigmoid(g1) * p1.astype(F32) + _sigmoid(g2) * acc


def _add_epi(acc, res):
    return (acc + res,)


def _ffn_bwd_epi(acc, gate, up):
    g = gate.astype(F32)
    sg = _sigmoid(g)
    return acc * up.astype(F32) * (sg * (1.0 + g * (1.0 - sg))), acc * (g * sg)


def _merge_bwd_epi(acc, g1, g2, p1, p2):
    s1, s2 = _sigmoid(g1), _sigmoid(g2)
    dg1, dg2 = acc * p1.astype(F32) * (s1 * (1.0 - s1)), acc * p2.astype(F32) * (s2 * (1.0 - s2))
    return acc * s1, acc * s2, jnp.concatenate([dg1, dg2], axis=1)


def _layer_fwd(l, x, p, rmat):
    t = x.shape[0]
    n = f"l{l}_"
    k = _layer_consts(p)
    h = _rmsnorm_fwd(n + "norm_mix", x, p["norm_mix_w"])
    proj = _matmul(n + "in_proj", "nn", [(h, 0, p["w_in"], 0)], t, PROJ_W, 1024, 1024, 1280, 1024, (F32,))
    act_g, pre_g = _conv_fwd(n + "conv_gdn", proj, QKV_OFF, p["gdn_conv_w"], k["zb"])
    act_s, pre_s = _conv_fwd(n + "conv_ssm", proj, XBC_OFF, p["ssm_conv_w"], k["scb"])
    y_ssm, ysc, st_s = _ssd_fwd(n + "ssd_fwd", act_s, proj, k["dtb"], k["arow"], k["dxrow"], k["snw"], rmat)
    y_gdn, oraw, st_g = _gdn_fwd(n + "gdn_fwd", act_g, proj, k["gb"], k["garow"], k["gnw"])
    if "late" in p:
        y_gdn, late = p["late"](y_gdn)
        p = {**p, **late}
    p1 = _matmul(n + "proj_ssm", "nn", [(y_ssm, 0, p["w_proj_ssm"], 0)], t, 1024, 1024, 1024, 1024, 1024, (BF16,))
    p2, merged = _matmul(n + "proj_gdn_merge", "nn", [(y_gdn, 0, p["w_proj_gdn"], 0)], t, 1024, 1024, 512, 1024, 1024,
                         (BF16, BF16), epi=_merge_epi, extras=[(p1, 0), (proj, G1_OFF // 1024), (proj, G2_OFF // 1024)])
    x1 = _matmul(n + "out_proj", "nn", [(merged, 0, p["w_out"], 0)], t, 1024, 1024, 1024, 1024, 1024, (F32,),
                 epi=_add_epi, extras=[(x, 0)])
    h2 = _rmsnorm_fwd(n + "norm_ffn", x1, p["norm_ffn_w"])
    gate, up, act = _matmul(n + "ffn_in", "nn", [(h2, 0, p["w_ffn_in"], 0), (h2, 0, p["w_ffn_in"], 2)], t, FFN, 1024,
                            1024, FFN // 2, 1024, (BF16, BF16, BF16), epi=_silu_mul_epi, second=1)
    x2 = _matmul(n + "ffn_down", "nn", [(act, 0, p["w_ffn_down"], 0)], t, 1024, FFN, 1024, 1024, FFN, (F32,),
                 epi=_add_epi, extras=[(x1, 0)])
    saved = dict(x=x, h=h, proj=proj, act_g=act_g, act_s=act_s, pre_g=pre_g, pre_s=pre_s, y_ssm=y_ssm, ysc=ysc, st_s=st_s, y_gdn=y_gdn, oraw=oraw,
                 st_g=st_g, p1=p1, p2=p2, merged=merged, x1=x1, h2=h2, up=up, gate=gate, act=act, k=k, p=p)
    return x2, saved


def _layer_bwd(l, dx2, dx2b, s, p, rmat, hooks):
    t = dx2.shape[0]
    n = f"l{l}_"
    k = s["k"]
    tk_tok = 1024
    hf = FFN // 2
    g = {}
    dgate, dup = _matmul(n + "d_ffn_act", "nt", [(dx2b, 0, p["w_ffn_down"], 0)], t, FFN, 1024, 1024, hf, 1024, (BF16, BF16),
                         epi=_ffn_bwd_epi, extras=[(s["gate"], 0), (s["up"], 0)])
    g["w_ffn_down"] = _matmul(n + "dw_ffn_down", "tn", [(s["act"], 0, dx2b, 0)], FFN, 1024, t, hf, 1024, tk_tok, (BF16,))
    dh2 = _matmul(n + "d_ffn_in", "nt", [(dgate, 0, p["w_ffn_in"], 0), (dup, 0, p["w_ffn_in"], 2)], t, 1024, FFN,
                  1024, 1024, hf, (F32,))
    dwg = _matmul(n + "dw_ffn_gate", "tn", [(s["h2"], 0, dgate, 0)], 1024, FFN, t, 1024, hf, tk_tok, (BF16,))
    dwu = _matmul(n + "dw_ffn_up", "tn", [(s["h2"], 0, dup, 0)], 1024, FFN, t, 1024, hf, tk_tok, (BF16,))
    g["w_ffn_in"] = (dwg, dwu)
    dx1, dx1b, g["norm_ffn_w"] = _rmsnorm_bwd(n + "d_norm_ffn", s["x1"], p["norm_ffn_w"], dh2, dx2)
    dx1b = hooks.ffn_done(dx1b)
    dp1, dp2, dproj = _matmul(
        n + "d_out_proj", "nt", [(dx1b, 0, p["w_out"], 0)], t, 1024, 1024, 512, 1024, 1024,
        (BF16, BF16, (BF16, PROJ_W, 2048, G1_OFF // 2048)), epi=_merge_bwd_epi,
        extras=[(s["proj"], G1_OFF // 1024), (s["proj"], G2_OFF // 1024), (s["p1"], 0), (s["p2"], 0)])
    g["w_out"] = _matmul(n + "dw_out", "tn", [(s["merged"], 0, dx1b, 0)], 1024, 1024, t, 1024, 1024, tk_tok, (BF16,))
    g["w_proj_ssm"] = _matmul(n + "dw_proj_ssm", "tn", [(s["y_ssm"], 0, dp1, 0)], 1024, 1024, t, 1024, 1024, tk_tok, (BF16,))
    g["w_proj_gdn"] = _matmul(n + "dw_proj_gdn", "tn", [(s["y_gdn"], 0, dp2, 0)], 1024, 1024, t, 1024, 1024, tk_tok, (BF16,))
    dp1, dp2 = hooks.early_ready(l, g, dp1, dp2)
    dy_ssm = _matmul(n + "d_proj_ssm", "nt", [(dp1, 0, p["w_proj_ssm"], 0)], t, 1024, 1024, 1024, 1024, 1024, (F32,))
    dy_gdn = _matmul(n + "d_proj_gdn", "nt", [(dp2, 0, p["w_proj_gdn"], 0)], t, 1024, 1024, 1024, 1024, 1024, (F32,))
    dact_s, dproj, dsm_s, dsnw, dd, dal, ddtb = _ssd_bwd(n + "ssd_bwd", s["act_s"], s["proj"], k["dtb"], k["arow"],
                                                           k["dxrow"], k["snw"], rmat, s["ysc"], s["st_s"], dy_ssm, dproj)
    dact_g, dproj, dsm_g, dgnw, dgal, dgb = _gdn_bwd(n + "gdn_bwd", s["act_g"], s["proj"], k["gb"], k["garow"], k["gnw"],
                                                       s["oraw"], s["st_g"], dy_gdn, dproj)
    dsm_s = hooks.mixers_done(dsm_s)
    dproj = _place_small(n + "d_small", dsm_s, dsm_g, dproj)
    dproj, g["ssm_conv_w"], dcb = _conv_bwd(n + "d_conv_ssm", s["proj"], XBC_OFF, p["ssm_conv_w"], s["pre_s"], dact_s, dproj)
    dproj, g["gdn_conv_w"], _ = _conv_bwd(n + "d_conv_gdn", s["proj"], QKV_OFF, p["gdn_conv_w"], s["pre_g"], dact_g, dproj)
    g["ssm_conv_b"] = dcb.reshape(-1)
    g["ssm_norm_w"] = dsnw.reshape(-1)
    g["ssm_d"] = dd[0, :SSM_HEADS]
    g["ssm_a_log"] = dal[0, :SSM_HEADS]
    g["ssm_dt_bias"] = ddtb[0, :SSM_HEADS]
    g["gdn_norm_w"] = dgnw.reshape(-1)
    g["gdn_a_log"] = dgal[0, LANE_A: LANE_A + GDN_HEADS]
    g["gdn_dt_bias"] = dgb[0, LANE_A: LANE_A + GDN_HEADS]
    dh = _matmul(n + "d_in_proj", "nt", [(dproj, 0, p["w_in"], 0)], t, 1024, PROJ_W, 1024, 1024, 1280, (F32,))
    g["w_in"] = _matmul(n + "dw_in", "tn", [(s["h"], 0, dproj, 0)], 1024, PROJ_W, t, 1024, 1280, tk_tok, (BF16,))
    dx, dxb, g["norm_mix_w"] = _rmsnorm_bwd(n + "d_norm_mix", s["x"], p["norm_mix_w"], dh, dx1)
    g["norm_mix_w"] = g["norm_mix_w"].reshape(-1)
    g["norm_ffn_w"] = g["norm_ffn_w"].reshape(-1)
    return dx, dxb, g


def _local_step(x, tgt, layers, final_norm_w, reduce=False):
    rmat = _expand_matrix()
    saved, params = [], []
    for l in range(DEPTH):
        x, p = layers[l](x)
        x, s = _layer_fwd(l, x, p, rmat)
        saved.append(s)
        params.append(s["p"])
    loss, dx, dxb, dfw = _loss_head("loss_head", x, final_norm_w, tgt)
    grads = [None] * DEPTH
    hooks = _ReduceBesideBackward() if reduce else _NoReduce()
    for l in reversed(range(DEPTH)):
        dx, dxb, grads[l] = _layer_bwd(l, dx, dxb, saved[l], params[l], rmat, hooks)
        if reduce:
            dxb = hooks.layer_done(l, grads[l], dxb)
    if reduce:
        hooks.finish_start(dxb)
    return loss[0, 0], dx, grads, dfw.reshape(-1), hooks if reduce else None


SMALL = ("norm_mix_w", "ssm_conv_b", "ssm_dt_bias", "ssm_a_log", "ssm_d", "ssm_norm_w", "gdn_a_log", "gdn_dt_bias",
         "gdn_norm_w", "norm_ffn_w")
WEIGHTS = ("norm_mix_w", "w_in", "ssm_conv_w", "ssm_conv_b", "ssm_dt_bias", "ssm_a_log", "ssm_d", "ssm_norm_w", "gdn_conv_w",
           "gdn_a_log", "gdn_dt_bias", "gdn_norm_w", "w_proj_ssm", "w_proj_gdn", "w_out", "norm_ffn_w", "w_ffn_in",
           "w_ffn_down", "final_norm_w")


FIRST_USED = ("w_in", "ssm_conv_w", "gdn_conv_w")


def _gather_layer(l, w):
    conv_names = [nm for nm, _ in CONVS]
    groups = ([s for s in BIG + CONVS if s[0] in FIRST_USED], [s for s in BIG + CONVS if s[0] not in FIRST_USED])
    gathered = []
    for i, (specs, tag) in enumerate(zip(groups, ("first", "rest"))):
        shards = [w[nm][l] if nm in conv_names else w[nm][l].astype(BF16) for nm, _ in specs]
        gathered.append(_allgather_seq(f"l{l}_gather_{tag}", shards, collective_id=2 * l + i))
    small = {nm: w[nm][l] for nm in SMALL}

    def use(i, act):
        act, blocks = lax.optimization_barrier((act, gathered[i]))
        return act, {nm: _w_in_from_blocks(g) if nm == "w_in" else _from_gathered(g, axis)
                     for (nm, axis), g in zip(groups[i], blocks)}

    def full_weights(x):
        x, out = use(0, x)
        out.update(small)
        out["late"] = lambda y: use(1, y)
        return x, out

    return full_weights


EARLY_GRADS = ("w_ffn_down", "w_ffn_in", "w_out", "w_proj_ssm", "w_proj_gdn")


class _GradReduceScatter:
    def __init__(self, tag, specs, grads):
        self.tag = tag
        self.specs = specs
        self.blocks = [_w_in_to_blocks(grads[nm]) if nm == "w_in" else _to_dest_major(grads[nm], axis)
                       for nm, axis in specs]

    def _tied(self, started, acts):
        *acts, self.token = lax.optimization_barrier((*acts, started[4]))
        return acts

    def start(self, *acts):
        cc = lax.axis_index("c")
        self.keep = [lax.dynamic_index_in_dim(b.reshape((4, 2) + b.shape[1:]), cc, axis=1, keepdims=False)
                     for b in self.blocks]
        self.to_sibling = _copies_start(f"{self.tag}_to_sibling_start", _sibling_plan, 4, self.blocks, 4)
        return self._tied(self.to_sibling, acts)

    def mid(self, *acts):
        got = _copies_wait(f"{self.tag}_to_sibling_wait", _sibling_plan, self.to_sibling, (acts[0], self.token))
        chip_sums = [_sum_terms(f"{self.tag}_chip_sum_{nm}", [(k[None], 0), (g[None], 0)], BF16)
                     for (nm, _), k, g in zip(self.specs, self.keep, got)]
        self.to_chips = _copies_start(f"{self.tag}_between_chips_start", _chips_plan, 3, chip_sums, 3)
        return self._tied(self.to_chips, acts)

    def end(self, after):
        after = tuple(after) if isinstance(after, (tuple, list)) else (after,)
        landed = _copies_wait(f"{self.tag}_between_chips_wait", _chips_plan, self.to_chips, (*after, self.token))
        my_chip = 2 * lax.axis_index("x") + lax.axis_index("y")
        own = [lax.dynamic_index_in_dim(s, my_chip, axis=0, keepdims=True) for s in self.to_chips[2]]
        return {nm: _sum_terms(f"{self.tag}_total_{nm}", [(o, 0), (e, 0), (e, 1), (e, 2)], F32)
                for (nm, _), o, e in zip(self.specs, own, landed)}


class _NoReduce:
    def ffn_done(self, dx1b):
        return dx1b

    def early_ready(self, l, g, dp1, dp2):
        return dp1, dp2

    def mixers_done(self, dsm):
        return dsm


class _ReduceBesideBackward(_NoReduce):
    def __init__(self):
        self.late = None
        self.early = None
        self.shards = [dict() for _ in range(DEPTH)]

    def ffn_done(self, dx1b):
        if self.late is not None:
            (dx1b,) = self.late.mid(dx1b)
        return dx1b

    def early_ready(self, l, g, dp1, dp2):
        self.early = _GradReduceScatter(f"l{l}_early_grads", [s for s in BIG if s[0] in EARLY_GRADS], g)
        return self.early.start(dp1, dp2)

    def mixers_done(self, dsm):
        (dsm,) = self.early.mid(dsm)
        return dsm

    def layer_done(self, l, g, dxb):
        if self.late is not None:
            self.shards[l + 1].update(self.late.end(dxb))
        self.shards[l].update(self.early.end(dxb))
        self.late = _GradReduceScatter(f"l{l}_late_grads", [s for s in BIG + CONVS if s[0] not in EARLY_GRADS], g)
        (dxb,) = self.late.start(dxb)
        return dxb

    def finish_start(self, dxb):
        self.late.mid(dxb)

    def finish_end(self, after):
        return self.late.end(after)


def _allreduce_small(vecs):
    flat = jnp.concatenate(vecs)
    n = flat.shape[0]
    rows = -(-n // 128)
    rows = -(-rows // 8) * 8
    buf = jnp.pad(flat, (0, rows * 128 - n)).reshape(rows, 128)
    (allv,) = _allgather("gather_small_grads", [buf])
    tot = _sum_terms("small_grads_total", [(allv, d) for d in range(N_DEV)], F32).reshape(-1)
    out, o = [], 0
    for v in vecs:
        out.append(tot[o: o + v.shape[0]])
        o += v.shape[0]
    return out


def kernel(x, norm_mix_w, w_in, ssm_conv_w, ssm_conv_b, ssm_dt_bias, ssm_a_log, ssm_d, ssm_norm_w, gdn_conv_w, gdn_a_log, gdn_dt_bias, gdn_norm_w, w_proj_ssm, w_proj_gdn, w_out, norm_ffn_w, w_ffn_in, w_ffn_down, final_norm_w, loss_target, m_norm_mix_w, m_w_in, m_ssm_conv_w, m_ssm_conv_b, m_ssm_dt_bias, m_ssm_a_log, m_ssm_d, m_ssm_norm_w, m_gdn_conv_w, m_gdn_a_log, m_gdn_dt_bias, m_gdn_norm_w, m_w_proj_ssm, m_w_proj_gdn, m_w_out, m_norm_ffn_w, m_w_ffn_in, m_w_ffn_down, m_final_norm_w, v_norm_mix_w, v_w_in, v_ssm_conv_w, v_ssm_conv_b, v_ssm_dt_bias, v_ssm_a_log, v_ssm_d, v_ssm_norm_w, v_gdn_conv_w, v_gdn_a_log, v_gdn_dt_bias, v_gdn_norm_w, v_w_proj_ssm, v_w_proj_gdn, v_w_out, v_norm_ffn_w, v_w_ffn_in, v_w_ffn_down, v_final_norm_w):
    w = dict(norm_mix_w=norm_mix_w, w_in=w_in, ssm_conv_w=ssm_conv_w, ssm_conv_b=ssm_conv_b, ssm_dt_bias=ssm_dt_bias,
             ssm_a_log=ssm_a_log, ssm_d=ssm_d, ssm_norm_w=ssm_norm_w, gdn_conv_w=gdn_conv_w, gdn_a_log=gdn_a_log,
             gdn_dt_bias=gdn_dt_bias, gdn_norm_w=gdn_norm_w, w_proj_ssm=w_proj_ssm, w_proj_gdn=w_proj_gdn, w_out=w_out,
             norm_ffn_w=norm_ffn_w, w_ffn_in=w_ffn_in, w_ffn_down=w_ffn_down, final_norm_w=final_norm_w)
    m = dict(norm_mix_w=m_norm_mix_w, w_in=m_w_in, ssm_conv_w=m_ssm_conv_w, ssm_conv_b=m_ssm_conv_b, ssm_dt_bias=m_ssm_dt_bias,
             ssm_a_log=m_ssm_a_log, ssm_d=m_ssm_d, ssm_norm_w=m_ssm_norm_w, gdn_conv_w=m_gdn_conv_w, gdn_a_log=m_gdn_a_log,
             gdn_dt_bias=m_gdn_dt_bias, gdn_norm_w=m_gdn_norm_w, w_proj_ssm=m_w_proj_ssm, w_proj_gdn=m_w_proj_gdn,
             w_out=m_w_out, norm_ffn_w=m_norm_ffn_w, w_ffn_in=m_w_ffn_in, w_ffn_down=m_w_ffn_down,
             final_norm_w=m_final_norm_w)
    v = dict(norm_mix_w=v_norm_mix_w, w_in=v_w_in, ssm_conv_w=v_ssm_conv_w, ssm_conv_b=v_ssm_conv_b, ssm_dt_bias=v_ssm_dt_bias,
             ssm_a_log=v_ssm_a_log, ssm_d=v_ssm_d, ssm_norm_w=v_ssm_norm_w, gdn_conv_w=v_gdn_conv_w, gdn_a_log=v_gdn_a_log,
             gdn_dt_bias=v_gdn_dt_bias, gdn_norm_w=v_gdn_norm_w, w_proj_ssm=v_w_proj_ssm, w_proj_gdn=v_w_proj_gdn,
             w_out=v_w_out, norm_ffn_w=v_norm_ffn_w, w_ffn_in=v_w_ffn_in, w_ffn_down=v_w_ffn_down,
             final_norm_w=v_final_norm_w)

    layers = [_gather_layer(l, w) for l in range(DEPTH)]
    loss_part, dx, lgrads, dfw, reducer = _local_step(x[0], loss_target[0], layers, final_norm_w, reduce=True)
    loss = lax.psum(loss_part, ("x", "y", "c"))
    shard_grads = reducer.shards
    late = [nm for nm, _ in BIG + CONVS if nm not in EARLY_GRADS]
    grad = {nm: [shard_grads[l][nm] for l in range(DEPTH)] for nm in EARLY_GRADS}
    small_vecs = [lgrads[l][nm].reshape(-1) for l in range(DEPTH) for nm in SMALL] + [dfw]
    small_sum = _allreduce_small(small_vecs)
    for i, nm in enumerate(SMALL):
        grad[nm] = jnp.stack([small_sum[l * len(SMALL) + i].reshape(w[nm].shape[1:]) for l in range(DEPTH)])
    grad["final_norm_w"] = small_sum[-1]

    deltas, new_m, new_v = {}, {}, {}
    for nm in [n for n in WEIGHTS if n not in late]:
        if nm == "w_ffn_in":
            tr = lambda a: jnp.transpose(a, (0, 2, 1))
            outs = _adamw("adamw_" + nm, tr(w[nm]), [g.T for g in grad[nm]], tr(m[nm]), tr(v[nm]))
            grad[nm], deltas[nm], new_m[nm], new_v[nm] = (tr(o) for o in outs)
        else:
            grad[nm], deltas[nm], new_m[nm], new_v[nm] = _adamw("adamw_" + nm, w[nm], grad[nm], m[nm], v[nm])
    shard_grads[0].update(reducer.finish_end([deltas[nm] for nm in EARLY_GRADS]))
    for nm in late:
        gl = [shard_grads[l][nm] for l in range(DEPTH)]
        if nm == "w_in":
            fwd, back = (lambda a: jnp.transpose(a, (2, 0, 1))), (lambda a: jnp.transpose(a, (1, 2, 0)))
            g3 = jnp.stack([g.T for g in gl], axis=1)
            outs = _adamw_rows("adamw_" + nm, fwd(w[nm]), g3, fwd(m[nm]), fwd(v[nm]))
            grad[nm], deltas[nm], new_m[nm], new_v[nm] = (back(o) for o in (g3,) + tuple(outs))
        else:
            grad[nm], deltas[nm], new_m[nm], new_v[nm] = _adamw("adamw_" + nm, w[nm], gl, m[nm], v[nm])
    return (loss, dx[None], *[grad[nm] for nm in WEIGHTS], *[deltas[nm] for nm in WEIGHTS],
            *[new_m[nm] for nm in WEIGHTS], *[new_v[nm] for nm in WEIGHTS])
```

```python
import functools

import jax
import jax.numpy as jnp
from jax import lax
from jax.experimental import pallas as pl
from jax.experimental.pallas import tpu as pltpu
from jax.experimental.pallas import tpu_sc as plsc

F32 = jnp.float32
BF16 = jnp.bfloat16
HI = lax.Precision.HIGHEST
SDS = jax.ShapeDtypeStruct

D_MODEL = 1024
DEPTH = 2
SSM_HEADS = 16
SSM_P = 64
SSM_N = 128
SSM_GROUPS = 2
SSM_CONV = 1536
GDN_HEADS = 8
GDN_DK = 128
GDN_QKV = 3072
CONV_K = 4
CHUNK = 64
SCAN_CHUNKS_PER_STEP = 8
FFN = 2816
IN_DIM = 8736
EPS = 1e-6
N_DEV = 8

Z_OFF = 0
GZ_OFF = 1024
G1_OFF = 2048
G2_OFF = 3072
QKV_OFF = 4096
XBC_OFF = 7168
SM_OFF = 8704
PROJ_W = 8960
LANE_A = 16
LANE_B = 24
O_Z, O_XBC, O_DT, O_QKV, O_GZ, O_A, O_B, O_G1, O_G2 = 0, 1024, 2560, 2576, 5648, 6672, 6680, 6688, 7712

ADAM_LR = 0.001
ADAM_B1 = 0.9
ADAM_B2 = 0.999
ADAM_EPS = 1e-08
ADAM_WD = 0.01
ADAM_STEP = 10

V7X_VMEM_LIMIT = 48 * 1024 * 1024

NN = ((1,), (0,))
NT = ((1,), (1,))
TN = ((0,), (0,))


def _bdot(a, b, dims):
    return lax.dot_general(a.astype(BF16), b.astype(BF16), (dims, ((), ())), preferred_element_type=F32)


def _hdot(a, b, dims=NN):
    return lax.dot_general(a, b, (dims, ((), ())), precision=HI, preferred_element_type=F32)


def _sigmoid(x):
    return 1.0 / (1.0 + jnp.exp(-x))


def _softplus(x):
    return jnp.maximum(x, 0.0) + jnp.log(1.0 + jnp.exp(-jnp.abs(x)))


def _params(dims):
    return pltpu.CompilerParams(dimension_semantics=dims, vmem_limit_bytes=V7X_VMEM_LIMIT)


def _rowsum(x):
    return jnp.sum(x, axis=-1, keepdims=True)


def _colsum(x):
    return jnp.sum(x, axis=0, keepdims=True)


def _matmul(name, mode, pairs, m, n, kdim, tm, tn, tk, out_dtypes, epi=None, extras=(), second=None):
    tm, tn, tk = min(tm, m), min(tn, n), min(tk, kdim)
    nk = kdim // tk
    assert m % tm == 0 and n % tn == 0 and kdim % tk == 0, (name, m, n, kdim, tm, tn, tk)
    assert second is None or nk == 1
    in_specs, args = [], []
    for a, a_off, b, b_off in pairs:
        if mode == "nn":
            in_specs.append(pl.BlockSpec((tm, tk), lambda i, j, k, o=a_off: (i, k + o)))
            in_specs.append(pl.BlockSpec((tk, tn), lambda i, j, k, o=b_off: (k, j + o)))
            dims = NN
        elif mode == "nt":
            in_specs.append(pl.BlockSpec((tm, tk), lambda i, j, k, o=a_off: (i, k + o)))
            in_specs.append(pl.BlockSpec((tn, tk), lambda i, j, k, o=b_off: (j, k + o)))
            dims = NT
        else:
            in_specs.append(pl.BlockSpec((tk, tm), lambda i, j, k, o=a_off: (k, i + o)))
            in_specs.append(pl.BlockSpec((tk, tn), lambda i, j, k, o=b_off: (k, j + o)))
            dims = TN
        args += [a, b]
    for e, e_off in extras:
        in_specs.append(pl.BlockSpec((tm, tn), lambda i, j, k, o=e_off: (i, j + o)))
        args.append(e)
    npair, nex, nout = len(pairs), len(extras), len(out_dtypes)

    def body(*refs):
        prefs = refs[: 2 * npair]
        erefs = refs[2 * npair: 2 * npair + nex]
        orefs = refs[2 * npair + nex: 2 * npair + nex + nout]

        def finish(*res):
            outs = res if epi is None else epi(*res, *[e[...] for e in erefs])
            for o, r in zip(orefs, outs):
                o[...] = r.astype(o.dtype)

        def total(ps):
            s = _bdot(prefs[2 * ps[0]][...], prefs[2 * ps[0] + 1][...], dims)
            for p in ps[1:]:
                s = s + _bdot(prefs[2 * p][...], prefs[2 * p + 1][...], dims)
            return s

        if second is not None:
            finish(total(list(range(second))), total(list(range(second, npair))))
            return
        s = total(list(range(npair)))
        if nk == 1:
            finish(s)
            return
        acc = refs[-1]
        k = pl.program_id(2)

        @pl.when(k == 0)
        def _():
            acc[...] = s

        @pl.when(k > 0)
        def _():
            acc[...] += s

        @pl.when(k == nk - 1)
        def _():
            finish(acc[...])

    out_shape, out_specs = [], []
    for od in out_dtypes:
        if isinstance(od, tuple):
            dt, full_w, blk_w, cblk = od
            assert n == tn
            out_shape.append(SDS((m, full_w), dt))
            out_specs.append(pl.BlockSpec((tm, blk_w), lambda i, j, k, c=cblk: (i, c)))
        else:
            out_shape.append(SDS((m, n), od))
            out_specs.append(pl.BlockSpec((tm, tn), lambda i, j, k: (i, j)))
    out_shape, out_specs = tuple(out_shape), tuple(out_specs)
    res = pl.pallas_call(
        body, grid=(m // tm, n // tn, nk), in_specs=in_specs, out_specs=out_specs, out_shape=out_shape,
        scratch_shapes=[pltpu.VMEM((tm, tn), F32)] if nk > 1 else [], name=name,
        compiler_params=_params(("parallel", "parallel", "arbitrary")),
    )(*args)
    return res if nout > 1 else res[0]


def _rmsnorm_fwd(name, x, w):
    t, d = x.shape
    tm = min(512, t)

    def body(x_ref, w_ref, h_ref):
        xv = x_ref[...]
        r = lax.rsqrt(jnp.mean(xv * xv, axis=-1, keepdims=True) + EPS)
        h_ref[...] = (xv * r * w_ref[...]).astype(BF16)

    return pl.pallas_call(
        body, grid=(t // tm,),
        in_specs=[pl.BlockSpec((tm, d), lambda i: (i, 0)), pl.BlockSpec((1, d), lambda i: (0, 0))],
        out_specs=pl.BlockSpec((tm, d), lambda i: (i, 0)), out_shape=SDS((t, d), BF16), name=name,
        compiler_params=_params(("parallel",)),
    )(x, w.reshape(1, d))


def _rmsnorm_bwd(name, x, w, dh, dres):
    t, d = x.shape
    tm = min(512, t)

    def body(x_ref, w_ref, dh_ref, dres_ref, dx_ref, dxb_ref, dw_ref):
        xv = x_ref[...]
        r = lax.rsqrt(jnp.mean(xv * xv, axis=-1, keepdims=True) + EPS)
        xh = xv * r
        dhv = dh_ref[...].astype(F32)
        dxh = dhv * w_ref[...]
        dx = r * (dxh - xh * jnp.mean(dxh * xh, axis=-1, keepdims=True)) + dres_ref[...]
        dx_ref[...] = dx
        dxb_ref[...] = dx.astype(BF16)

        @pl.when(pl.program_id(0) == 0)
        def _():
            dw_ref[...] = jnp.zeros_like(dw_ref)

        dw_ref[...] += _colsum(dhv * xh)

    row = pl.BlockSpec((tm, d), lambda i: (i, 0))
    vec = pl.BlockSpec((1, d), lambda i: (0, 0))
    return pl.pallas_call(
        body, grid=(t // tm,), in_specs=[row, vec, row, row], out_specs=(row, row, vec),
        out_shape=(SDS((t, d), F32), SDS((t, d), BF16), SDS((1, d), F32)), name=name,
        compiler_params=_params(("arbitrary",)),
    )(x, w.reshape(1, d), dh, dres)


def _loss_head(name, x, w, tgt):
    t, d = x.shape
    tm = min(512, t)

    def body(x_ref, w_ref, t_ref, loss_ref, dx_ref, dxb_ref, dw_ref):
        xv = x_ref[...]
        wv = w_ref[...]
        r = lax.rsqrt(jnp.mean(xv * xv, axis=-1, keepdims=True) + EPS)
        xh = xv * r
        e = xh * wv - t_ref[...]
        dy = e * (1.0 / d)
        dxh = dy * wv
        dx = r * (dxh - xh * jnp.mean(dxh * xh, axis=-1, keepdims=True))
        dx_ref[...] = dx
        dxb_ref[...] = dx.astype(BF16)

        @pl.when(pl.program_id(0) == 0)
        def _():
            dw_ref[...] = jnp.zeros_like(dw_ref)
            loss_ref[...] = jnp.zeros_like(loss_ref)

        dw_ref[...] += _colsum(dy * xh)
        loss_ref[...] += 0.5 * jnp.sum(jnp.mean(e * e, axis=-1, keepdims=True), axis=0, keepdims=True)

    row = pl.BlockSpec((tm, d), lambda i: (i, 0))
    vec = pl.BlockSpec((1, d), lambda i: (0, 0))
    return pl.pallas_call(
        body, grid=(t // tm,), in_specs=[row, vec, row],
        out_specs=(pl.BlockSpec((1, 1), lambda i: (0, 0)), row, row, vec),
        out_shape=(SDS((1, 1), F32), SDS((t, d), F32), SDS((t, d), BF16), SDS((1, d), F32)), name=name,
        compiler_params=_params(("arbitrary",)),
    )(x, w.reshape(1, d), tgt)


def _shift_down(u, s, row):
    return jnp.where(row >= s, pltpu.roll(u, shift=s, axis=0), 0.0)


def _conv_fwd(name, src, col0, w, b):
    t = src.shape[0]
    c = w.shape[1]
    tc = 256
    assert c % tc == 0 and col0 % tc == 0

    def body(u_ref, w_ref, b_ref, o_ref, pre_ref):
        u = u_ref[...]
        wv = w_ref[...]
        rolled = [None] + [pltpu.roll(u, shift=s, axis=0) for s in range(1, CONV_K)]
        row8 = lax.broadcasted_iota(jnp.int32, (8, tc), 0)
        pre = b_ref[...] + wv[3:4, :] * u
        head = b_ref[...] + wv[3:4, :] * u[:8]
        for s in range(1, CONV_K):
            pre = pre + wv[3 - s: 4 - s, :] * rolled[s]
            head = head + wv[3 - s: 4 - s, :] * jnp.where(row8 >= s, rolled[s][:8], 0.0)
        pre_ref[...] = pre
        o_ref[...] = pre * _sigmoid(pre)
        pre_ref[:8, :] = head
        o_ref[:8, :] = head * _sigmoid(head)

    col = pl.BlockSpec((t, tc), lambda j: (0, j))
    return pl.pallas_call(
        body, grid=(c // tc,),
        in_specs=[pl.BlockSpec((t, tc), lambda j: (0, j + col0 // tc)), pl.BlockSpec((CONV_K, tc), lambda j: (0, j)),
                  pl.BlockSpec((1, tc), lambda j: (0, j))],
        out_specs=(col, col), out_shape=(SDS((t, c), F32), SDS((t, c), F32)), name=name,
        compiler_params=_params(("parallel",)),
    )(src, w, b)


def _place_small(name, dsm_a, dsm_b, dproj):
    t = dsm_a.shape[0]
    width = PROJ_W - SM_OFF
    tr = min(512, t)

    def body(a_ref, b_ref, dproj_ref, o_ref):
        o_ref[:, :128] = a_ref[...] + b_ref[...]
        o_ref[:, 128:] = jnp.zeros((tr, width - 128), BF16)

    row = pl.BlockSpec((tr, 128), lambda i: (i, 0))
    return pl.pallas_call(
        body, grid=(t // tr,), in_specs=[row, row, ANY],
        out_specs=pl.BlockSpec((tr, width), lambda i: (i, SM_OFF // width)), out_shape=SDS(dproj.shape, BF16),
        input_output_aliases={2: 0}, name=name, compiler_params=_params(("parallel",)),
    )(dsm_a, dsm_b, dproj)


def _conv_bwd(name, src, col0, w, pre, dact, dproj):
    t = src.shape[0]
    c = w.shape[1]
    tc = 128

    def body(u_ref, w_ref, pre_ref, da_ref, dproj_ref, du_ref, dw_ref, db_ref):
        u = u_ref[...]
        wv = w_ref[...]
        prev = pre_ref[...]
        sg = _sigmoid(prev)
        dpre = da_ref[...] * (sg * (1.0 + prev * (1.0 - sg)))
        edge = 16
        rows = lax.broadcasted_iota(jnp.int32, (edge, tc), 0)
        du = wv[3:4, :] * dpre
        tail = wv[3:4, :] * dpre[t - edge:]
        dw_ref[3:4, :] = _colsum(dpre * u)
        for s in range(1, CONV_K):
            up = pltpu.roll(dpre, shift=t - s, axis=0)
            wrapped = rows >= edge - s
            du = du + wv[3 - s: 4 - s, :] * up
            tail = tail + wv[3 - s: 4 - s, :] * jnp.where(wrapped, 0.0, up[t - edge:])
            dw_ref[3 - s: 4 - s, :] = _colsum(up * u) - _colsum(jnp.where(wrapped, up[t - edge:] * u[t - edge:], 0.0))
        du_ref[...] = du.astype(BF16)
        du_ref[t - edge:, :] = tail.astype(BF16)
        db_ref[...] = _colsum(dpre)

    col = pl.BlockSpec((t, tc), lambda j: (0, j))
    return pl.pallas_call(
        body, grid=(c // tc,),
        in_specs=[pl.BlockSpec((t, tc), lambda j: (0, j + col0 // tc)), pl.BlockSpec((CONV_K, tc), lambda j: (0, j)),
                  col, col, ANY],
        out_specs=(pl.BlockSpec((t, tc), lambda j: (0, j + col0 // tc)), pl.BlockSpec((CONV_K, tc), lambda j: (0, j)),
                   pl.BlockSpec((1, tc), lambda j: (0, j))),
        out_shape=(SDS(dproj.shape, BF16), SDS((CONV_K, c), F32), SDS((1, c), F32)), name=name,
        input_output_aliases={4: 0},
        compiler_params=_params(("parallel",)),
    )(src, w, pre, dact, dproj)


def _tri(q):
    ii = lax.broadcasted_iota(jnp.int32, (q, q), 0)
    jj = lax.broadcasted_iota(jnp.int32, (q, q), 1)
    return ii, jj


def _dot01(x, r01, dims, terms=3):
    out, rem = None, x
    for i in range(terms):
        hi = rem.astype(BF16)
        d = lax.dot_general(hi, r01, (dims, ((), ())), preferred_element_type=F32)
        out = d if out is None else out + d
        if i + 1 < terms:
            rem = rem - hi.astype(F32)
    return out


def _ssd_common(act, sm, dtb, arow, rmat):
    q = CHUNK
    ii, jj = _tri(q)
    lane = lax.broadcasted_iota(jnp.int32, (q, 128), 1)
    m16 = lane < SSM_HEADS
    dt = jnp.where(m16, _softplus(sm + dtb), 0.0)
    a = dt * arow
    tril = (ii >= jj).astype(F32)
    triu = (ii <= jj).astype(F32)
    acum = _hdot(tril, a)
    acum_r = _hdot(a.T, triu)
    dtx = _dot01(dt, rmat, NN)
    acx = _dot01(acum, rmat, NN)
    ex = jnp.exp(acx)
    alx = acx[q - 1: q, :]
    dex = jnp.exp(alx - acx)
    xs = act[:, :1024]
    return dict(ii=ii, jj=jj, m16=m16, dt=dt, a=a, triu=triu, acum=acum, acum_r=acum_r, dtx=dtx, ex=ex, dex=dex,
                elx=jnp.exp(alx), xs=xs, x=xs * dtx)


def _ssd_lmat(cm, h):
    return jnp.where(cm["ii"] >= cm["jj"], jnp.exp(cm["acum"][:, h: h + 1] - cm["acum_r"][h: h + 1, :]), 0.0)


def _ssd_fwd(name, act, proj, dtb, arow, dxrow, nw, rmat):
    t = act.shape[0]
    q = CHUNK
    nc = t // q
    hg = SSM_HEADS // SSM_GROUPS
    gw = hg * SSM_P

    def body(act_ref, z_ref, sm_ref, dtb_ref, arow_ref, dx_ref, nw_ref, r_ref, y_ref, ys_ref, st_ref, s_scr, yd_scr):
        @pl.when(pl.program_id(0) == 0)
        def _():
            s_scr[...] = jnp.zeros_like(s_scr)

        s_all = s_scr[...]
        for sub in range(cps):
            rows = pl.ds(sub * q, q)
            s_all = chunk(act_ref.at[rows, :], z_ref.at[rows, :], sm_ref.at[rows, :], dtb_ref, arow_ref, dx_ref, nw_ref, r_ref,
                          y_ref.at[rows, :], ys_ref.at[rows, :], st_ref.at[sub], yd_scr.at[rows, :], s_all)
        s_scr[...] = s_all

    def chunk(act_ref, z_ref, sm_ref, dtb_ref, arow_ref, dx_ref, nw_ref, r_ref, y_ref, ys_ref, st_ref, yd_scr, s_all):
        st_ref[...] = s_all
        actv = act_ref[...]
        cm = _ssd_common(actv, sm_ref[...], dtb_ref[...], arow_ref[...], r_ref[...])
        x = cm["x"]
        xd = x * cm["dex"]
        yoffs, snew = [], []
        for g in range(SSM_GROUPS):
            bg = actv[:, 1024 + g * SSM_N: 1024 + (g + 1) * SSM_N]
            cg = actv[:, 1280 + g * SSM_N: 1280 + (g + 1) * SSM_N]
            sg = s_all[:, g * gw: (g + 1) * gw]
            cb = _bdot(cg, bg, NT)
            yoffs.append(_bdot(cg, sg, NN))
            snew.append(_bdot(bg, xd[:, g * gw: (g + 1) * gw], TN))
            for r in range(hg):
                h = g * hg + r
                mm = cb * _ssd_lmat(cm, h)
                yd_scr[:, h * SSM_P: (h + 1) * SSM_P] = _bdot(mm, x[:, h * SSM_P: (h + 1) * SSM_P], NN)
        s_next = s_all * cm["elx"] + jnp.concatenate(snew, axis=1)
        ysc = yd_scr[...] + jnp.concatenate(yoffs, axis=1) * cm["ex"]
        ys_ref[...] = ysc
        zv = z_ref[...]
        yg = (ysc + dx_ref[...] * cm["xs"]) * (zv * _sigmoid(zv))
        nwv = nw_ref[...]
        for g in range(SSM_GROUPS):
            sl = yg[:, g * gw: (g + 1) * gw]
            rr = lax.rsqrt(jnp.mean(sl * sl, axis=-1, keepdims=True) + EPS)
            y_ref[:, g * gw: (g + 1) * gw] = (sl * rr * nwv[:, g * gw: (g + 1) * gw]).astype(BF16)
        return s_next

    cps = SCAN_CHUNKS_PER_STEP if nc % SCAN_CHUNKS_PER_STEP == 0 else 1
    qq = cps * q
    vec128 = pl.BlockSpec((1, 128), lambda c: (0, 0))
    vec1k = pl.BlockSpec((1, 1024), lambda c: (0, 0))
    return pl.pallas_call(
        body, grid=(nc // cps,),
        in_specs=[pl.BlockSpec((qq, SSM_CONV), lambda c: (c, 0)), pl.BlockSpec((qq, 1024), lambda c: (c, Z_OFF // 1024)),
                  pl.BlockSpec((qq, 128), lambda c: (c, SM_OFF // 128)), vec128, vec128, vec1k, vec1k,
                  pl.BlockSpec((128, 1024), lambda c: (0, 0))],
        out_specs=(pl.BlockSpec((qq, 1024), lambda c: (c, 0)), pl.BlockSpec((qq, 1024), lambda c: (c, 0)),
                   pl.BlockSpec((cps, 128, 1024), lambda c: (c, 0, 0))),
        out_shape=(SDS((t, 1024), BF16), SDS((t, 1024), F32), SDS((nc, 128, 1024), F32)),
        scratch_shapes=[pltpu.VMEM((128, 1024), F32), pltpu.VMEM((qq, 1024), F32)], name=name,
        compiler_params=_params(("arbitrary",)),
    )(act, proj, proj, dtb, arow, dxrow, nw, rmat)


def _ssd_bwd(name, act, proj, dtb, arow, dxrow, nw, rmat, ysc, states, dy, dproj):
    t = act.shape[0]
    q = CHUNK
    nc = t // q
    hg = SSM_HEADS // SSM_GROUPS
    gw = hg * SSM_P

    def body(act_ref, z_ref, sm_ref, dtb_ref, arow_ref, dx_ref, nw_ref, r_ref, ys_ref, st_ref, dy_ref, dproj_ref,
             dact_ref, dz_ref, dsm_ref, dnw_ref, dd_ref, dal_ref, ddtb_ref, ds_scr, dxd_scr):
        @pl.when(pl.program_id(0) == 0)
        def _():
            ds_scr[...] = jnp.zeros_like(ds_scr)
            dnw_ref[...] = jnp.zeros_like(dnw_ref)
            dd_ref[...] = jnp.zeros_like(dd_ref)
            dal_ref[...] = jnp.zeros_like(dal_ref)
            ddtb_ref[...] = jnp.zeros_like(ddtb_ref)

        dsn = ds_scr[...]
        for sub in reversed(range(cps)):
            rows = pl.ds(sub * q, q)
            dsn = chunk(act_ref.at[rows, :], z_ref.at[rows, :], sm_ref.at[rows, :], dtb_ref, arow_ref, dx_ref, nw_ref, r_ref,
                        ys_ref.at[rows, :], st_ref.at[sub], dy_ref.at[rows, :], dact_ref.at[rows, :], dz_ref.at[rows, :],
                        dsm_ref.at[rows, :], dnw_ref, dd_ref, dal_ref, ddtb_ref, dxd_scr.at[rows, :], dsn)
        ds_scr[...] = dsn

    def chunk(act_ref, z_ref, sm_ref, dtb_ref, arow_ref, dx_ref, nw_ref, r_ref, ys_ref, st_ref, dy_ref,
              dact_ref, dz_ref, dsm_ref, dnw_ref, dd_ref, dal_ref, ddtb_ref, dxd_scr, dsn):
        actv = act_ref[...]
        smv = sm_ref[...]
        rmat_v = r_ref[...]
        cm = _ssd_common(actv, smv, dtb_ref[...], arow_ref[...], rmat_v)
        ii, jj = cm["ii"], cm["jj"]
        x, xs = cm["x"], cm["xs"]
        s_all = st_ref[...]
        ysv = ys_ref[...]
        dxr = dx_ref[...]
        y = ysv + dxr * xs
        zv = z_ref[...]
        sz = _sigmoid(zv)
        silz = zv * sz
        yg = y * silz
        dout = dy_ref[...]
        nwv = nw_ref[...]
        dyn = dout * nwv
        yn_parts, dyg_parts = [], []
        for g in range(SSM_GROUPS):
            sl = yg[:, g * gw: (g + 1) * gw]
            rr = lax.rsqrt(jnp.mean(sl * sl, axis=-1, keepdims=True) + EPS)
            yn = sl * rr
            dn = dyn[:, g * gw: (g + 1) * gw]
            yn_parts.append(yn)
            dyg_parts.append(rr * (dn - yn * jnp.mean(dn * yn, axis=-1, keepdims=True)))
        dnw_ref[...] += _colsum(dout * jnp.concatenate(yn_parts, axis=1))
        dyg = jnp.concatenate(dyg_parts, axis=1)
        dyv = dyg * silz
        dz_ref[...] = (dyg * y * (sz * (1.0 + zv * (1.0 - sz)))).astype(BF16)
        dd_ref[...] += _dot01(_colsum(dyv * xs), rmat_v, NT)
        dxs = dyv * dxr
        dcs = dyv * cm["ex"]
        xd = x * cm["dex"]
        dxst_parts, ds_parts, db_parts, dc_parts, yoff_parts, wcol_rows = [], [], [], [], [], []
        lane128 = lax.broadcasted_iota(jnp.int32, (q, 128), 1)
        wrow = jnp.zeros((q, 128), F32)
        for g in range(SSM_GROUPS):
            bg = actv[:, 1024 + g * SSM_N: 1024 + (g + 1) * SSM_N]
            cg = actv[:, 1280 + g * SSM_N: 1280 + (g + 1) * SSM_N]
            sg = s_all[:, g * gw: (g + 1) * gw]
            dsng = dsn[:, g * gw: (g + 1) * gw]
            dcsg = dcs[:, g * gw: (g + 1) * gw]
            dcg = _bdot(dcsg, sg, NT)
            yoff_parts.append(_bdot(cg, sg, NN))
            ds_parts.append(_bdot(cg, dcsg, TN))
            dxst_parts.append(_bdot(bg, dsng, NN))
            dbg = _bdot(xd[:, g * gw: (g + 1) * gw], dsng, NT)
            cb = _bdot(cg, bg, NT)
            dcb = jnp.zeros((q, q), F32)
            for r in range(hg):
                h = g * hg + r
                lm = _ssd_lmat(cm, h)
                mm = cb * lm
                dyh = dyv[:, h * SSM_P: (h + 1) * SSM_P]
                dm = jnp.where(ii >= jj, _bdot(dyh, x[:, h * SSM_P: (h + 1) * SSM_P], NT), 0.0)
                dxd_scr[:, h * SSM_P: (h + 1) * SSM_P] = _bdot(mm, dyh, TN)
                dcb = dcb + dm * lm
                wm = dm * mm
                wrow = wrow + jnp.where(lane128 == h, _rowsum(wm), 0.0)
                wcol_rows.append(_colsum(wm))
            dc_parts.append(dcg + _bdot(dcb, bg, NN))
            db_parts.append(dbg + _bdot(dcb, cg, TN))
        dxst = jnp.concatenate(dxst_parts, axis=1) * cm["dex"]
        dx = dxd_scr[...] + dxst
        ds_prev = jnp.concatenate(ds_parts, axis=1) + dsn * cm["elx"]
        wcol = jnp.concatenate(wcol_rows + [jnp.zeros((128 - SSM_HEADS, q), F32)], axis=0).T
        yoff = jnp.concatenate(yoff_parts, axis=1) * cm["ex"]
        xdxst = x * dxst
        dac = wrow - wcol + _dot01(dyv * yoff - xdxst, rmat_v, NT)
        last = _dot01(_colsum(dsn * s_all) * cm["elx"] + _colsum(xdxst), rmat_v, NT)
        rowq = lax.broadcasted_iota(jnp.int32, (q, 128), 0)
        dac = dac + jnp.where(rowq == q - 1, last, 0.0)
        da = _hdot(cm["triu"], dac)
        arow_v = arow_ref[...]
        ddt = da * arow_v + _dot01(dx * xs, rmat_v, NT)
        dxs = dxs + dx * cm["dtx"]
        dal_ref[...] += _colsum(da * cm["a"])
        ddtraw = jnp.where(cm["m16"], ddt * _sigmoid(smv + dtb_ref[...]), 0.0)
        ddtb_ref[...] += _colsum(ddtraw)
        dsm_ref[...] = ddtraw.astype(BF16)
        dact_ref[:, :1024] = dxs
        for g in range(SSM_GROUPS):
            dact_ref[:, 1024 + g * SSM_N: 1024 + (g + 1) * SSM_N] = db_parts[g]
            dact_ref[:, 1280 + g * SSM_N: 1280 + (g + 1) * SSM_N] = dc_parts[g]
        return ds_prev

    cps = SCAN_CHUNKS_PER_STEP if nc % SCAN_CHUNKS_PER_STEP == 0 else 1
    qq = cps * q
    rev = lambda c: nc // cps - 1 - c
    vec128 = pl.BlockSpec((1, 128), lambda c: (0, 0))
    vec1k = pl.BlockSpec((1, 1024), lambda c: (0, 0))
    return pl.pallas_call(
        body, grid=(nc // cps,),
        in_specs=[pl.BlockSpec((qq, SSM_CONV), lambda c: (rev(c), 0)),
                  pl.BlockSpec((qq, 1024), lambda c: (rev(c), Z_OFF // 1024)),
                  pl.BlockSpec((qq, 128), lambda c: (rev(c), SM_OFF // 128)), vec128, vec128, vec1k, vec1k,
                  pl.BlockSpec((128, 1024), lambda c: (0, 0)),
                  pl.BlockSpec((qq, 1024), lambda c: (rev(c), 0)), pl.BlockSpec((cps, 128, 1024), lambda c: (rev(c), 0, 0)),
                  pl.BlockSpec((qq, 1024), lambda c: (rev(c), 0)), ANY],
        out_specs=(pl.BlockSpec((qq, SSM_CONV), lambda c: (rev(c), 0)),
                   pl.BlockSpec((qq, 1024), lambda c: (rev(c), Z_OFF // 1024)),
                   pl.BlockSpec((qq, 128), lambda c: (rev(c), 0)), vec1k, vec128, vec128, vec128),
        out_shape=(SDS((t, SSM_CONV), F32), SDS(dproj.shape, BF16), SDS((t, 128), BF16), SDS((1, 1024), F32),
                   SDS((1, 128), F32), SDS((1, 128), F32), SDS((1, 128), F32)),
        input_output_aliases={11: 1},
        scratch_shapes=[pltpu.VMEM((128, 1024), F32), pltpu.VMEM((qq, 1024), F32)], name=name,
        compiler_params=_params(("arbitrary",)),
    )(act, proj, proj, dtb, arow, dxrow, nw, rmat, ysc, states, dy, dproj)


def _split(a):
    hi = a.astype(BF16)
    return hi, (a - hi.astype(F32)).astype(BF16)


def _dot3(a, b, dims=NN):
    (ah, al), (bh, bl) = a, b

    def d(x, y):
        return lax.dot_general(x, y, (dims, ((), ())), preferred_element_type=F32)

    return d(ah, bh) + (d(ah, bl) + d(al, bh))


def _tri_inverses(amats, ii, jj):
    eye = jnp.where(ii == jj, 1.0, 0.0)
    tms = [eye - a for a in amats]
    sp = [_split(a) for a in amats]
    for _ in range(5):
        sp = [_split(_dot3(s, s)) for s in sp]
        tms = [t + _dot3(_split(t), s) for t, s in zip(tms, sp)]
    return tms


def _gdn_common(sm, gb, garow):
    q = CHUNK
    ii, jj = _tri(q)
    lane = lax.broadcasted_iota(jnp.int32, (q, 128), 1)
    ma = (lane >= LANE_A) & (lane < LANE_A + GDN_HEADS)
    spre = sm + gb
    g = jnp.where(ma, garow * _softplus(spre), 0.0)
    beta = _sigmoid(sm)
    tril = (ii >= jj).astype(F32)
    triu = (ii <= jj).astype(F32)
    gc = _hdot(tril, g)
    gc_r = _hdot(g.T, triu)
    return dict(ii=ii, jj=jj, lane=lane, ma=ma, spre=spre, g=g, beta=beta, triu=triu, gc=gc, gc_r=gc_r)


def _each(f, *lists):
    return [f(*xs) for xs in zip(*lists)]


GDN_SCALE = GDN_DK ** -0.5
GDN_BWD_HEAD_GROUPS = (range(GDN_HEADS),)


def _gdn_heads(cm, actv, states, heads=range(GDN_HEADS)):
    q = CHUNK
    ii, jj = cm["ii"], cm["jj"]
    qr = [actv[:, h * 128: (h + 1) * 128] for h in heads]
    kr = [actv[:, 1024 + h * 128: 1024 + (h + 1) * 128] for h in heads]
    v = [actv[:, 2048 + h * 128: 2048 + (h + 1) * 128] for h in heads]
    rq = _each(lambda x: lax.rsqrt(_rowsum(x * x) + EPS), qr)
    rk = _each(lambda x: lax.rsqrt(_rowsum(x * x) + EPS), kr)
    qn = _each(lambda x, r: x * r * GDN_SCALE, qr, rq)
    kn = _each(lambda x, r: x * r, kr, rk)
    gcc = [cm["gc"][:, LANE_A + h: LANE_A + h + 1] for h in heads]
    gcr = [cm["gc_r"][LANE_A + h: LANE_A + h + 1, :] for h in heads]
    bcol = [cm["beta"][:, LANE_B + h: LANE_B + h + 1] for h in heads]
    dm = _each(lambda c, r: jnp.where(ii >= jj, jnp.exp(c - r), 0.0), gcc, gcr)
    kq = _each(lambda k, a: _bdot(jnp.concatenate([k, a], axis=0), k, NT), kn, qn)
    ak = _each(lambda x, d: jnp.where(ii > jj, x[:q] * d, 0.0), kq, dm)
    qkm = _each(lambda x, d: jnp.where(ii >= jj, x[q:] * d, 0.0), kq, dm)
    tm = _tri_inverses(_each(lambda a, b: a * b, ak, bcol), ii, jj)
    eg = _each(jnp.exp, gcc)
    gl = [c[q - 1: q, :] for c in gcc]
    rm = _each(lambda vv, k, b, e: jnp.concatenate([vv * b, k * (b * e)], axis=1), v, kn, bcol, eg)
    tt = _each(lambda t, r: _dot3(_split(t), _split(r)), tm, rm)
    w = [t[:, 128:] for t in tt]
    qg = _each(lambda a, e: a * e, qn, eg)
    ws = _each(lambda ww, a, s: _bdot(jnp.concatenate([ww, a], axis=0), s, NN), w, qg, states)
    vnew = _each(lambda t, x: t[:, :128] - x[:q], tt, ws)
    return dict(qr=qr, v=v, rq=rq, rk=rk, qn=qn, kn=kn, gcc=gcc, bcol=bcol, dm=dm, ak=ak, tm=tm, eg=eg, gl=gl,
                egl=_each(jnp.exp, gl), ed=_each(lambda g, c: jnp.exp(g - c), gl, gcc), tt=tt, w=w, vnew=vnew, qkm=qkm,
                qg=qg, qgs=[x[q:] for x in ws])


def _gdn_fwd(name, act, proj, gb, garow, gnw):
    t = act.shape[0]
    q = CHUNK
    nc = t // q

    def body(act_ref, gz_ref, sm_ref, gb_ref, ga_ref, nw_ref, y_ref, o_ref, st_ref, s_scr):
        @pl.when(pl.program_id(0) == 0)
        def _():
            s_scr[...] = jnp.zeros_like(s_scr)

        states = [s_scr[h * 128: (h + 1) * 128, :] for h in range(GDN_HEADS)]
        for sub in range(cps):
            rows = pl.ds(sub * q, q)
            states = chunk(act_ref.at[rows, :], gz_ref.at[rows, :], sm_ref.at[rows, :], gb_ref, ga_ref, nw_ref,
                           y_ref.at[rows, :], o_ref.at[rows, :], st_ref.at[sub], states)
        for h in range(GDN_HEADS):
            s_scr[h * 128: (h + 1) * 128, :] = states[h]

    def chunk(act_ref, gz_ref, sm_ref, gb_ref, ga_ref, nw_ref, y_ref, o_ref, st_ref, states):
        for h in range(GDN_HEADS):
            st_ref[h * 128: (h + 1) * 128, :] = states[h]
        actv = act_ref[...]
        cm = _gdn_common(sm_ref[...], gb_ref[...], ga_ref[...])
        nwv = nw_ref[...]
        gzv = gz_ref[...]
        hd = _gdn_heads(cm, actv, states)
        outs = _each(lambda qs, m, vn: qs + _bdot(m, vn, NN), hd["qgs"], hd["qkm"], hd["vnew"])
        snew = _each(lambda s, e, k, d, vn: s * e + _bdot(k * d, vn, TN), states, hd["egl"], hd["kn"], hd["ed"], hd["vnew"])
        for h in range(GDN_HEADS):
            o = outs[h]
            o_ref[:, h * 128: (h + 1) * 128] = o
            rr = lax.rsqrt(jnp.mean(o * o, axis=-1, keepdims=True) + EPS)
            gz = gzv[:, h * 128: (h + 1) * 128]
            y_ref[:, h * 128: (h + 1) * 128] = (o * rr * nwv * (gz * _sigmoid(gz))).astype(BF16)
        return snew

    cps = SCAN_CHUNKS_PER_STEP if nc % SCAN_CHUNKS_PER_STEP == 0 else 1
    qq = cps * q
    vec128 = pl.BlockSpec((1, 128), lambda c: (0, 0))
    return pl.pallas_call(
        body, grid=(nc // cps,),
        in_specs=[pl.BlockSpec((qq, GDN_QKV), lambda c: (c, 0)), pl.BlockSpec((qq, 1024), lambda c: (c, GZ_OFF // 1024)),
                  pl.BlockSpec((qq, 128), lambda c: (c, SM_OFF // 128)), vec128, vec128, vec128],
        out_specs=(pl.BlockSpec((qq, 1024), lambda c: (c, 0)), pl.BlockSpec((qq, 1024), lambda c: (c, 0)),
                   pl.BlockSpec((cps, 1024, 128), lambda c: (c, 0, 0))),
        out_shape=(SDS((t, 1024), BF16), SDS((t, 1024), F32), SDS((nc, 1024, 128), F32)),
        scratch_shapes=[pltpu.VMEM((1024, 128), F32)], name=name, compiler_params=_params(("arbitrary",)),
    )(act, proj, proj, gb, garow, gnw)


def _gdn_bwd(name, act, proj, gb, garow, gnw, oraw, states, dy, dproj):
    t = act.shape[0]
    q = CHUNK
    nc = t // q

    def body(act_ref, gz_ref, sm_ref, gb_ref, ga_ref, nw_ref, o_ref, st_ref, dy_ref, dproj_ref,
             dact_ref, dgz_ref, dsm_ref, dnw_ref, dal_ref, dgb_ref, ds_scr):
        @pl.when(pl.program_id(0) == 0)
        def _():
            ds_scr[...] = jnp.zeros_like(ds_scr)
            dnw_ref[...] = jnp.zeros_like(dnw_ref)
            dal_ref[...] = jnp.zeros_like(dal_ref)
            dgb_ref[...] = jnp.zeros_like(dgb_ref)

        dsn = [ds_scr[h * 128: (h + 1) * 128, :] for h in range(GDN_HEADS)]
        for sub in reversed(range(cps)):
            rows = pl.ds(sub * q, q)
            dsn = chunk(act_ref.at[rows, :], gz_ref.at[rows, :], sm_ref.at[rows, :], gb_ref, ga_ref, nw_ref,
                        o_ref.at[rows, :], st_ref.at[sub], dy_ref.at[rows, :],
                        dact_ref.at[rows, :], dgz_ref.at[rows, :], dsm_ref.at[rows, :], dnw_ref, dal_ref, dgb_ref, dsn)
        for h in range(GDN_HEADS):
            ds_scr[h * 128: (h + 1) * 128, :] = dsn[h]

    def chunk(act_ref, gz_ref, sm_ref, gb_ref, ga_ref, nw_ref, o_ref, st_ref, dy_ref,
              dact_ref, dgz_ref, dsm_ref, dnw_ref, dal_ref, dgb_ref, dsn):
        actv = act_ref[...]
        smv = sm_ref[...]
        garow_v = ga_ref[...]
        cm = _gdn_common(smv, gb_ref[...], garow_v)
        ii, jj, lane = cm["ii"], cm["jj"], cm["lane"]
        nwv = nw_ref[...]
        rowq = lax.broadcasted_iota(jnp.int32, (q, 1), 0)
        dgc_all = jnp.zeros((q, 128), F32)
        dbeta_all = jnp.zeros((q, 128), F32)
        dnw_acc = jnp.zeros((1, 128), F32)
        ds_out = []
        ov, gzv, dyv = o_ref[...], gz_ref[...], dy_ref[...]
        for heads in GDN_BWD_HEAD_GROUPS:
            part = group(heads, cm, actv, ov, gzv, dyv, nwv, rowq, st_ref, [dsn[h] for h in heads], dact_ref, dgz_ref)
            ds_out += part[0]
            dgc_all, dbeta_all, dnw_acc = dgc_all + part[1], dbeta_all + part[2], dnw_acc + part[3]
        dnw_ref[...] += dnw_acc
        dg = _hdot(cm["triu"], dgc_all)
        da_raw = jnp.where(cm["ma"], dg * garow_v * _sigmoid(cm["spre"]), 0.0)
        dal_ref[...] += _colsum(dg * cm["g"])
        dgb_ref[...] += _colsum(da_raw)
        beta = cm["beta"]
        dsm_ref[...] = (da_raw + dbeta_all * beta * (1.0 - beta)).astype(BF16)
        return ds_out

    def group(heads, cm, actv, ov, gzv, dyv, nwv, rowq, st_ref, dsn, dact_ref, dgz_ref):
        ii, jj, lane = cm["ii"], cm["jj"], cm["lane"]
        dgc_all = jnp.zeros((q, 128), F32)
        dbeta_all = jnp.zeros((q, 128), F32)
        dnw_acc = jnp.zeros((1, 128), F32)
        ds_out = []
        sts = [st_ref[h * 128: (h + 1) * 128, :] for h in heads]
        hd = _gdn_heads(cm, actv, sts, heads)
        qn, kn, v, eg, ed, egl, bcol = hd["qn"], hd["kn"], hd["v"], hd["eg"], hd["ed"], hd["egl"], hd["bcol"]
        vnew, qkm, qg, w, tt, dm, ak = hd["vnew"], hd["qkm"], hd["qg"], hd["w"], hd["tt"], hd["dm"], hd["ak"]
        do = []
        for h in heads:
            hs = slice(h * 128, (h + 1) * 128)
            o = ov[:, hs]
            rr = lax.rsqrt(jnp.mean(o * o, axis=-1, keepdims=True) + EPS)
            on = o * rr
            gz = gzv[:, hs]
            sz = _sigmoid(gz)
            silz = gz * sz
            dyh = dyv[:, hs]
            dnw_acc = dnw_acc + _colsum(dyh * on * silz)
            dgz_ref[:, hs] = (dyh * on * nwv * (sz * (1.0 + gz * (1.0 - sz)))).astype(BF16)
            don = dyh * nwv * silz
            do.append(rr * (don - on * jnp.mean(don * on, axis=-1, keepdims=True)))
        kd = _each(lambda k, e: k * e, kn, ed)
        dkd = _each(lambda vn, d: _bdot(vn, d, NT), vnew, dsn)
        dvnew_a = _each(lambda k, d: _bdot(k, d, NN), kd, dsn)
        ded = _each(lambda a, b: _rowsum(a * b), dkd, kd)
        dgl = _each(lambda d, s, e, de: jnp.sum(_rowsum(d * s), axis=0, keepdims=True) * e + _colsum(de), dsn, sts, egl, ded)
        dqk = _each(lambda d, vn: jnp.where(ii >= jj, _bdot(d, vn, NT), 0.0), do, vnew)
        dvnew = _each(lambda a, m, d: a + _bdot(m, d, TN), dvnew_a, qkm, do)
        pq = _each(lambda a, b: a * b, dqk, dm)
        w1 = _each(lambda a, b: a * b, dqk, qkm)
        dod = _each(lambda a, b: jnp.concatenate([a, b], axis=0), do, dvnew)
        dos = _each(lambda x, s: _bdot(x, s, NT), dod, sts)
        dqg = [x[:q] for x in dos]
        dw = [-x[q:] for x in dos]
        ds12 = _each(lambda a, ww, x: _bdot(jnp.concatenate([a, -ww], axis=0), x, TN), qg, w, dod)
        dr = _each(lambda t, a, b: _dot3(_split(t), _split(jnp.concatenate([a, b], axis=1)), TN), hd["tm"], dvnew, dw)
        da = _each(lambda r, t: jnp.where(ii > jj, -_dot3(_split(r), _split(t), NT), 0.0), dr, tt)
        sk = _each(lambda r, k: _rowsum(r[:, 128:] * k), dr, kn)
        pk = _each(lambda a, d, b: a * d * b, da, dm, bcol)
        pkn = _each(lambda p, pp, k: _bdot(jnp.concatenate([p, pp + pp.T], axis=0), k, NN), pq, pk, kn)
        dq = _each(lambda a, e, x: a * e + x[:q], dqg, eg, pkn)
        dk = _each(lambda a, e, p, x, r, b, eg_, y: a * e + _bdot(p, x, TN) + r[:, 128:] * (b * eg_) + y[q:],
                   dkd, ed, pq, qn, dr, bcol, eg, pkn)
        w2 = _each(lambda a, k, b: a * (k * b), da, ak, bcol)
        for i, h in enumerate(heads):
            hs = slice(h * 128, (h + 1) * 128)
            dgc = (-ded[i] + _rowsum(dqg[i] * qg[i]) + _rowsum(w1[i]) - _rowsum(w1[i].T) + sk[i] * bcol[i] * eg[i]
                   + _rowsum(w2[i]) - _rowsum(w2[i].T) + jnp.where(rowq == q - 1, dgl[i], 0.0))
            dbeta = _rowsum(dr[i][:, :128] * v[i]) + sk[i] * eg[i] + _rowsum(da[i] * ak[i])
            qhat = hd["qr"][i] * hd["rq"][i]
            dqhat = dq[i] * GDN_SCALE
            dact_ref[:, hs] = hd["rq"][i] * (dqhat - qhat * _rowsum(dqhat * qhat))
            dact_ref[:, 1024 + h * 128: 1024 + (h + 1) * 128] = hd["rk"][i] * (dk[i] - kn[i] * _rowsum(dk[i] * kn[i]))
            dact_ref[:, 2048 + h * 128: 2048 + (h + 1) * 128] = dr[i][:, :128] * bcol[i]
            dgc_all = dgc_all + jnp.where(lane == LANE_A + h, dgc, 0.0)
            dbeta_all = dbeta_all + jnp.where(lane == LANE_B + h, dbeta, 0.0)
            ds_out.append(dsn[i] * egl[i] + ds12[i])
        return ds_out, dgc_all, dbeta_all, dnw_acc

    cps = 2 if nc % 2 == 0 else 1
    qq = cps * q
    rev = lambda c: nc // cps - 1 - c
    vec128 = pl.BlockSpec((1, 128), lambda c: (0, 0))
    return pl.pallas_call(
        body, grid=(nc // cps,),
        in_specs=[pl.BlockSpec((qq, GDN_QKV), lambda c: (rev(c), 0)),
                  pl.BlockSpec((qq, 1024), lambda c: (rev(c), GZ_OFF // 1024)),
                  pl.BlockSpec((qq, 128), lambda c: (rev(c), SM_OFF // 128)), vec128, vec128, vec128,
                  pl.BlockSpec((qq, 1024), lambda c: (rev(c), 0)), pl.BlockSpec((cps, 1024, 128), lambda c: (rev(c), 0, 0)),
                  pl.BlockSpec((qq, 1024), lambda c: (rev(c), 0)), ANY],
        out_specs=(pl.BlockSpec((qq, GDN_QKV), lambda c: (rev(c), 0)),
                   pl.BlockSpec((qq, 1024), lambda c: (rev(c), GZ_OFF // 1024)),
                   pl.BlockSpec((qq, 128), lambda c: (rev(c), 0)), vec128, vec128, vec128),
        out_shape=(SDS((t, GDN_QKV), F32), SDS(dproj.shape, BF16), SDS((t, 128), BF16), SDS((1, 128), F32),
                   SDS((1, 128), F32), SDS((1, 128), F32)),
        input_output_aliases={9: 1},
        scratch_shapes=[pltpu.VMEM((1024, 128), F32)], name=name, compiler_params=_params(("arbitrary",)),
    )(act, proj, proj, gb, garow, gnw, oraw, states, dy, dproj)


def _row_tile(r):
    for cand in (512, 256, 128, 64, 32, 16, 8):
        if r % cand == 0:
            return cand
    return r


def _sum_terms(name, terms, out_dtype):
    shape = terms[0][0].shape[1:]
    c = shape[-1]
    r = 1
    for s in shape[:-1]:
        r *= s
    tr = min(_row_tile(r), 256)
    n = len(terms)

    def body(*refs):
        acc = refs[0][...].astype(F32)
        for k in range(1, n):
            acc = acc + refs[k][...].astype(F32)
        refs[n][...] = acc.astype(out_dtype)

    in_specs = [pl.BlockSpec((None, tr, c), lambda i, q=lead: (q, i, 0)) for _, lead in terms]
    args = [a.reshape(a.shape[0], r, c) for a, _ in terms]
    out = pl.pallas_call(body, grid=(r // tr,), in_specs=in_specs, out_specs=pl.BlockSpec((tr, c), lambda i: (i, 0)),
                         out_shape=SDS((r, c), out_dtype), name=name, compiler_params=_params(("parallel",)))(*args)
    return out.reshape(shape)


def _adamw_math(w, g, m, v):
    mn = ADAM_B1 * m + (1.0 - ADAM_B1) * g
    vn = ADAM_B2 * v + (1.0 - ADAM_B2) * (g * g)
    m_hat = mn / (1.0 - ADAM_B1 ** ADAM_STEP)
    v_hat = vn / (1.0 - ADAM_B2 ** ADAM_STEP)
    return -ADAM_LR * (m_hat / (jnp.sqrt(v_hat) + ADAM_EPS) + ADAM_WD * w), mn, vn


def _adamw_rows(name, w, g, m, v):
    r, a, c = w.shape
    tr = next(t for t in range(min(r, 128), 0, -1) if r % t == 0)

    def body(w_ref, g_ref, m_ref, v_ref, d_ref, nm_ref, nv_ref):
        d_ref[...], nm_ref[...], nv_ref[...] = _adamw_math(w_ref[...], g_ref[...], m_ref[...], v_ref[...])

    spec = pl.BlockSpec((tr, a, c), lambda i: (i, 0, 0))
    return pl.pallas_call(body, grid=(r // tr,), in_specs=[spec] * 4, out_specs=(spec,) * 3,
                          out_shape=(SDS(w.shape, F32),) * 3, name=name, compiler_params=_params(("parallel",)))(w, g, m, v)


def _adamw(name, w, g, m, v):
    shape = w.shape
    c = shape[-1]
    per_layer = isinstance(g, (list, tuple))
    nl = len(g) if per_layer else 1
    gs = [a.reshape(-1, c) for a in g] if per_layer else [g.reshape(-1, c)]
    r = gs[0].shape[0]
    w3, m3, v3 = (a.reshape(nl, r, c) for a in (w, m, v))
    tr = min(_row_tile(r), 256)

    def body(*refs):
        w_ref, m_ref, v_ref = refs[:3]
        g_refs = refs[3: 3 + nl]
        go_ref, d_ref, nm_ref, nv_ref = refs[3 + nl:]
        layer = pl.program_id(0)
        gv = g_refs[0][...]
        for k in range(1, nl):
            gv = jnp.where(layer == k, g_refs[k][...], gv)
        go_ref[...] = gv
        d_ref[...], nm_ref[...], nv_ref[...] = _adamw_math(w_ref[...], gv, m_ref[...], v_ref[...])

    spec3 = pl.BlockSpec((None, tr, c), lambda l, i: (l, i, 0))
    gspec = pl.BlockSpec((tr, c), lambda l, i: (i, 0))
    outs = pl.pallas_call(body, grid=(nl, r // tr), in_specs=[spec3] * 3 + [gspec] * nl, out_specs=(spec3,) * 4,
                          out_shape=(SDS((nl, r, c), F32),) * 4, name=name,
                          compiler_params=_params(("parallel", "parallel")))(w3, m3, v3, *gs)
    return tuple(o.reshape(shape) for o in outs)


ANY = pl.BlockSpec(memory_space=pl.ANY)
MESH = pl.DeviceIdType.MESH


def _allgather(name, xs):
    n = len(xs)

    def body(*refs):
        x_refs, out_refs = refs[:n], refs[n: 2 * n]
        send_sems, recv_sems, local_sems = refs[2 * n:]
        x, y, cc = lax.axis_index("x"), lax.axis_index("y"), lax.axis_index("c")
        me, sibling = (x, y, cc), (x, y, 1 - cc)
        chips = [(1 - x, y), (x, 1 - y), (1 - x, 1 - y)]

        def rows(a, px, py, pc):
            return out_refs[a].at[4 * px + 2 * py + pc]

        def copy(a, k, block, to, src=None):
            return pltpu.make_async_remote_copy(
                src_ref=rows(a, *block) if src is None else src, dst_ref=rows(a, *block),
                send_sem=send_sems.at[7 * a + k], recv_sem=recv_sems.at[7 * a + k], device_id=to, device_id_type=MESH)

        mine = [pltpu.make_async_copy(x_refs[a], rows(a, *me), local_sems.at[a]) for a in range(n)]
        for cp in mine:
            cp.start()
        first = []
        for a in range(n):
            first.append(copy(a, 0, me, sibling, src=x_refs[a]))
            first += [copy(a, 1 + j, me, (*chip, cc), src=x_refs[a]) for j, chip in enumerate(chips)]
        for cp in first:
            cp.start()
        passed = []
        for j, chip in enumerate(chips):
            for a in range(n):
                copy(a, 1 + j, (*chip, cc), me).wait_recv()
                fwd = copy(a, 4 + j, (*chip, cc), sibling)
                fwd.start()
                passed.append(fwd)
        for a in range(n):
            copy(a, 0, sibling, me).wait_recv()
        for j, chip in enumerate(chips):
            for a in range(n):
                copy(a, 4 + j, (*chip, 1 - cc), me).wait_recv()
        for cp in first + passed:
            cp.wait_send()
        for cp in mine:
            cp.wait()

    return pl.pallas_call(
        body, out_shape=tuple(SDS((N_DEV,) + a.shape, a.dtype) for a in xs), in_specs=[ANY] * n, out_specs=(ANY,) * n,
        scratch_shapes=[pltpu.SemaphoreType.DMA((7 * n,)), pltpu.SemaphoreType.DMA((7 * n,)),
                        pltpu.SemaphoreType.DMA((n,))],
        name=name,
    )(*xs)


def _allgather_seq(name, xs, collective_id):
    n = len(xs)
    x_refs = [jax.new_ref(a, memory_space=pltpu.MemorySpace.HBM) for a in xs]
    out_refs = [jax.empty_ref(SDS((N_DEV,) + a.shape, a.dtype), memory_space=pltpu.MemorySpace.HBM) for a in xs]

    @pl.kernel(mesh=plsc.ScalarSubcoreMesh(axis_name="seq", num_cores=1), name=name,
               scratch_types=(pltpu.SemaphoreType.DMA((7 * n,)), pltpu.SemaphoreType.DMA((7 * n,)),
                              pltpu.SemaphoreType.DMA((n,))),
               compiler_params=pltpu.CompilerParams(collective_id=collective_id))
    def launch(send_sems, recv_sems, local_sems):
        x, y, cc = lax.axis_index("x"), lax.axis_index("y"), lax.axis_index("c")
        me, sibling = (x, y, cc), (x, y, 1 - cc)
        chips = [(1 - x, y), (x, 1 - y), (1 - x, 1 - y)]
        barrier = pltpu.get_barrier_semaphore()
        for peer in [sibling] + [(*chip, cc) for chip in chips]:
            pl.semaphore_signal(barrier, inc=1, device_id=peer, device_id_type=MESH)
        pl.semaphore_wait(barrier, 4)

        def rows(a, px, py, pc):
            return out_refs[a].at[4 * px + 2 * py + pc]

        def copy(a, k, block, to, src=None):
            return pltpu.make_async_remote_copy(
                src_ref=rows(a, *block) if src is None else src, dst_ref=rows(a, *block),
                send_sem=send_sems.at[7 * a + k], recv_sem=recv_sems.at[7 * a + k], device_id=to, device_id_type=MESH)

        mine = [pltpu.make_async_copy(x_refs[a], rows(a, *me), local_sems.at[a]) for a in range(n)]
        for cp in mine:
            cp.start()
        first = []
        for a in range(n):
            first.append(copy(a, 0, me, sibling, src=x_refs[a]))
            first += [copy(a, 1 + j, me, (*chip, cc), src=x_refs[a]) for j, chip in enumerate(chips)]
        for cp in first:
            cp.start()
        passed = []
        for j, chip in enumerate(chips):
            for a in range(n):
                copy(a, 1 + j, (*chip, cc), me).wait_recv()
                fwd = copy(a, 4 + j, (*chip, cc), sibling)
                fwd.start()
                passed.append(fwd)
        for a in range(n):
            copy(a, 0, sibling, me).wait_recv()
        for j, chip in enumerate(chips):
            for a in range(n):
                copy(a, 4 + j, (*chip, 1 - cc), me).wait_recv()
        for cp in first + passed:
            cp.wait_send()
        for cp in mine:
            cp.wait()

    launch()
    return [r[...] for r in out_refs]


HBM = pl.BlockSpec(memory_space=pltpu.HBM)
SEM = pl.BlockSpec(memory_space=pltpu.SEMAPHORE)
EFFECT = pltpu.SideEffectType.DATAFLOW_SIDE_EFFECTING


def _sibling_plan(srcs, lands, send_sems, recv_sems):
    x, y, cc = lax.axis_index("x"), lax.axis_index("y"), lax.axis_index("c")
    return [pltpu.make_async_remote_copy(
        src_ref=srcs[a].at[2 * q + 1 - cc], dst_ref=lands[a].at[q], send_sem=send_sems.at[4 * a + q],
        recv_sem=recv_sems.at[4 * a + q], device_id=(x, y, 1 - cc), device_id_type=MESH)
        for a in range(len(srcs)) for q in range(4)]


def _chips_plan(srcs, lands, send_sems, recv_sems):
    x, y, cc = lax.axis_index("x"), lax.axis_index("y"), lax.axis_index("c")
    chips = [(1 - x, y), (x, 1 - y), (1 - x, 1 - y)]
    return [pltpu.make_async_remote_copy(
        src_ref=srcs[a].at[2 * px + py], dst_ref=lands[a].at[j], send_sem=send_sems.at[3 * a + j],
        recv_sem=recv_sems.at[3 * a + j], device_id=(px, py, cc), device_id_type=MESH)
        for a in range(len(srcs)) for j, (px, py) in enumerate(chips)]


def _copies_start(name, plan, per_array, srcs, land_lead):
    n = len(srcs)
    k = per_array * n

    def body(*refs):
        src_refs, land_refs = refs[:n], refs[n: 2 * n]
        send_sems, recv_sems = refs[2 * n], refs[2 * n + 1]
        token = refs[-1]
        for cp in plan(src_refs, land_refs, send_sems, recv_sems):
            cp.start()
        token[...] = jnp.zeros_like(token)

    lands = [lax.empty((land_lead,) + a.shape[1:], a.dtype) for a in srcs]
    outs = pl.pallas_call(
        body, name=name,
        out_shape=(pltpu.SemaphoreType.DMA((k,)), pltpu.SemaphoreType.DMA((k,)),
                   *[pltpu.HBM(a.shape, a.dtype) for a in srcs], *[pltpu.HBM(a.shape, a.dtype) for a in lands],
                   SDS((8, 128), F32)),
        in_specs=[HBM] * (2 * n), out_specs=(SEM, SEM, *[HBM] * (2 * n), pl.BlockSpec(memory_space=pltpu.VMEM)),
        input_output_aliases={i: 2 + i for i in range(2 * n)},
        compiler_params=pltpu.CompilerParams(has_side_effects=EFFECT),
    )(*[pltpu.with_memory_space_constraint(a, pltpu.HBM) for a in srcs],
      *[pltpu.with_memory_space_constraint(a, pltpu.HBM) for a in lands])
    return outs[0], outs[1], list(outs[2: 2 + n]), list(outs[2 + n: 2 + 2 * n]), outs[-1]


def _copies_wait(name, plan, started, after):
    send_sems, recv_sems, srcs, lands, _ = started
    n = len(srcs)
    after = tuple(after)

    def body(*refs):
        src_refs, land_refs = refs[:n], refs[n: 2 * n]
        for cp in plan(src_refs, land_refs, refs[2 * n], refs[2 * n + 1]):
            cp.wait_send()
            cp.wait_recv()

    outs = pl.pallas_call(
        body, name=name,
        out_shape=tuple(pltpu.HBM(a.shape, a.dtype) for a in srcs + lands),
        in_specs=[HBM] * (2 * n) + [SEM, SEM] + [ANY] * len(after), out_specs=(HBM,) * (2 * n),
        input_output_aliases={i: i for i in range(2 * n)},
        compiler_params=pltpu.CompilerParams(has_side_effects=EFFECT),
    )(*srcs, *lands, send_sems, recv_sems, *after)
    return list(outs[n:])


BIG = (("w_in", 1), ("w_ffn_in", 1), ("w_proj_ssm", 0), ("w_proj_gdn", 0), ("w_out", 0), ("w_ffn_down", 0))
CONVS = (("ssm_conv_w", 1), ("gdn_conv_w", 1))


def _to_dest_major(full, axis):
    if isinstance(full, tuple):
        per = N_DEV // len(full)
        s = full[0].shape[1] // per
        return jnp.stack([full[d // per][:, (d % per) * s: (d % per + 1) * s] for d in range(N_DEV)])
    a, b = full.shape
    if axis == 0:
        return full.reshape(N_DEV, a // N_DEV, b)
    s = b // N_DEV
    return jnp.stack([full[:, d * s: (d + 1) * s] for d in range(N_DEV)])


def _from_gathered(g, axis):
    if axis == 0:
        return g.reshape(-1, g.shape[2])
    return jnp.concatenate([g[d] for d in range(N_DEV)], axis=1)


IN_RUNS = ((Z_OFF, O_Z, 1024), (GZ_OFF, O_GZ, 1024), (G1_OFF, O_G1, 1024), (G2_OFF, O_G2, 1024), (QKV_OFF, O_QKV, 3072),
           (XBC_OFF, O_XBC, 1536), (SM_OFF, O_DT, 16), (SM_OFF + LANE_A, O_A, 8), (SM_OFF + LANE_B, O_B, 8))
IN_SHARD = IN_DIM // N_DEV


def _w_in_from_blocks(g):
    rows = g.shape[1]
    parts, pos = [], 0
    for off, o0, width in IN_RUNS:
        if off > pos:
            parts.append(jnp.zeros((rows, off - pos), g.dtype))
        c = o0
        while c < o0 + width:
            d = c // IN_SHARD
            hi = min(o0 + width, (d + 1) * IN_SHARD)
            parts.append(g[d][:, c - d * IN_SHARD: hi - d * IN_SHARD])
            c = hi
        pos = off + width
    parts.append(jnp.zeros((rows, PROJ_W - pos), g.dtype))
    return jnp.concatenate(parts, axis=1)


def _w_in_to_blocks(wp):
    by_orig = sorted(IN_RUNS, key=lambda r: r[1])
    blocks = []
    for d in range(N_DEV):
        lo, hi = d * IN_SHARD, (d + 1) * IN_SHARD
        parts = []
        for off, o0, width in by_orig:
            a, b = max(lo, o0), min(hi, o0 + width)
            if a < b:
                parts.append(wp[:, off + a - o0: off + b - o0])
        blocks.append(jnp.concatenate(parts, axis=1))
    return jnp.stack(blocks)


def _pad128(v, lane0):
    return jnp.zeros((1, 128), F32).at[0, lane0: lane0 + v.shape[0]].set(v)


def _layer_consts(p):
    return dict(
        dtb=_pad128(p["ssm_dt_bias"], 0), arow=_pad128(-jnp.exp(p["ssm_a_log"]), 0),
        dxrow=jnp.repeat(p["ssm_d"], SSM_P).reshape(1, 1024), snw=p["ssm_norm_w"].reshape(1, 1024),
        gb=_pad128(p["gdn_dt_bias"], LANE_A), garow=_pad128(-jnp.exp(p["gdn_a_log"]), LANE_A),
        gnw=p["gdn_norm_w"].reshape(1, 128), zb=jnp.zeros((1, GDN_QKV), F32), scb=p["ssm_conv_b"].reshape(1, SSM_CONV))


def _expand_matrix():
    row = lax.broadcasted_iota(jnp.int32, (128, 1024), 0)
    col = lax.broadcasted_iota(jnp.int32, (128, 1024), 1)
    return (col // SSM_P == row).astype(BF16)


def _silu_mul_epi(gate, up):
    return gate, up, gate * _sigmoid(gate) * up


def _merge_epi(acc, p1, g1, g2):
    return acc, _sigmoid(g1) * p1.astype(F32) + _sigmoid(g2) * acc


def _add_epi(acc, res):
    return (acc + res,)


def _ffn_bwd_epi(acc, gate, up):
    g = gate.astype(F32)
    sg = _sigmoid(g)
    return acc * up.astype(F32) * (sg * (1.0 + g * (1.0 - sg))), acc * (g * sg)


def _merge_bwd_epi(acc, g1, g2, p1, p2):
    s1, s2 = _sigmoid(g1), _sigmoid(g2)
    dg1, dg2 = acc * p1.astype(F32) * (s1 * (1.0 - s1)), acc * p2.astype(F32) * (s2 * (1.0 - s2))
    return acc * s1, acc * s2, jnp.concatenate([dg1, dg2], axis=1)


def _layer_fwd(l, x, p, rmat):
    t = x.shape[0]
    n = f"l{l}_"
    k = _layer_consts(p)
    h = _rmsnorm_fwd(n + "norm_mix", x, p["norm_mix_w"])
    proj = _matmul(n + "in_proj", "nn", [(h, 0, p["w_in"], 0)], t, PROJ_W, 1024, 1024, 1280, 1024, (F32,))
    act_g, pre_g = _conv_fwd(n + "conv_gdn", proj, QKV_OFF, p["gdn_conv_w"], k["zb"])
    act_s, pre_s = _conv_fwd(n + "conv_ssm", proj, XBC_OFF, p["ssm_conv_w"], k["scb"])
    y_ssm, ysc, st_s = _ssd_fwd(n + "ssd_fwd", act_s, proj, k["dtb"], k["arow"], k["dxrow"], k["snw"], rmat)
    y_gdn, oraw, st_g = _gdn_fwd(n + "gdn_fwd", act_g, proj, k["gb"], k["garow"], k["gnw"])
    if "late" in p:
        y_gdn, late = p["late"](y_gdn)
        p = {**p, **late}
    p1 = _matmul(n + "proj_ssm", "nn", [(y_ssm, 0, p["w_proj_ssm"], 0)], t, 1024, 1024, 1024, 1024, 1024, (BF16,))
    p2, merged = _matmul(n + "proj_gdn_merge", "nn", [(y_gdn, 0, p["w_proj_gdn"], 0)], t, 1024, 1024, 512, 1024, 1024,
                         (BF16, BF16), epi=_merge_epi, extras=[(p1, 0), (proj, G1_OFF // 1024), (proj, G2_OFF // 1024)])
    x1 = _matmul(n + "out_proj", "nn", [(merged, 0, p["w_out"], 0)], t, 1024, 1024, 1024, 1024, 1024, (F32,),
                 epi=_add_epi, extras=[(x, 0)])
    h2 = _rmsnorm_fwd(n + "norm_ffn", x1, p["norm_ffn_w"])
    gate, up, act = _matmul(n + "ffn_in", "nn", [(h2, 0, p["w_ffn_in"], 0), (h2, 0, p["w_ffn_in"], 2)], t, FFN, 1024,
                            1024, FFN // 2, 1024, (BF16, BF16, BF16), epi=_silu_mul_epi, second=1)
    x2 = _matmul(n + "ffn_down", "nn", [(act, 0, p["w_ffn_down"], 0)], t, 1024, FFN, 1024, 1024, FFN, (F32,),
                 epi=_add_epi, extras=[(x1, 0)])
    saved = dict(x=x, h=h, proj=proj, act_g=act_g, act_s=act_s, pre_g=pre_g, pre_s=pre_s, y_ssm=y_ssm, ysc=ysc, st_s=st_s, y_gdn=y_gdn, oraw=oraw,
                 st_g=st_g, p1=p1, p2=p2, merged=merged, x1=x1, h2=h2, up=up, gate=gate, act=act, k=k, p=p)
    return x2, saved


def _layer_bwd(l, dx2, dx2b, s, p, rmat, hooks):
    t = dx2.shape[0]
    n = f"l{l}_"
    k = s["k"]
    tk_tok = 1024
    hf = FFN // 2
    g = {}
    dgate, dup = _matmul(n + "d_ffn_act", "nt", [(dx2b, 0, p["w_ffn_down"], 0)], t, FFN, 1024, 1024, hf, 1024, (BF16, BF16),
                         epi=_ffn_bwd_epi, extras=[(s["gate"], 0), (s["up"], 0)])
    g["w_ffn_down"] = _matmul(n + "dw_ffn_down", "tn", [(s["act"], 0, dx2b, 0)], FFN, 1024, t, hf, 1024, tk_tok, (BF16,))
    dh2 = _matmul(n + "d_ffn_in", "nt", [(dgate, 0, p["w_ffn_in"], 0), (dup, 0, p["w_ffn_in"], 2)], t, 1024, FFN,
                  1024, 1024, hf, (F32,))
    dwg = _matmul(n + "dw_ffn_gate", "tn", [(s["h2"], 0, dgate, 0)], 1024, FFN, t, 1024, hf, tk_tok, (BF16,))
    dwu = _matmul(n + "dw_ffn_up", "tn", [(s["h2"], 0, dup, 0)], 1024, FFN, t, 1024, hf, tk_tok, (BF16,))
    g["w_ffn_in"] = (dwg, dwu)
    dx1, dx1b, g["norm_ffn_w"] = _rmsnorm_bwd(n + "d_norm_ffn", s["x1"], p["norm_ffn_w"], dh2, dx2)
    dx1b = hooks.ffn_done(dx1b)
    dp1, dp2, dproj = _matmul(
        n + "d_out_proj", "nt", [(dx1b, 0, p["w_out"], 0)], t, 1024, 1024, 512, 1024, 1024,
        (BF16, BF16, (BF16, PROJ_W, 2048, G1_OFF // 2048)), epi=_merge_bwd_epi,
        extras=[(s["proj"], G1_OFF // 1024), (s["proj"], G2_OFF // 1024), (s["p1"], 0), (s["p2"], 0)])
    g["w_out"] = _matmul(n + "dw_out", "tn", [(s["merged"], 0, dx1b, 0)], 1024, 1024, t, 1024, 1024, tk_tok, (BF16,))
    g["w_proj_ssm"] = _matmul(n + "dw_proj_ssm", "tn", [(s["y_ssm"], 0, dp1, 0)], 1024, 1024, t, 1024, 1024, tk_tok, (BF16,))
    g["w_proj_gdn"] = _matmul(n + "dw_proj_gdn", "tn", [(s["y_gdn"], 0, dp2, 0)], 1024, 1024, t, 1024, 1024, tk_tok, (BF16,))
    dp1, dp2 = hooks.early_ready(l, g, dp1, dp2)
    dy_ssm = _matmul(n + "d_proj_ssm", "nt", [(dp1, 0, p["w_proj_ssm"], 0)], t, 1024, 1024, 1024, 1024, 1024, (F32,))
    dy_gdn = _matmul(n + "d_proj_gdn", "nt", [(dp2, 0, p["w_proj_gdn"], 0)], t, 1024, 1024, 1024, 1024, 1024, (F32,))
    dact_s, dproj, dsm_s, dsnw, dd, dal, ddtb = _ssd_bwd(n + "ssd_bwd", s["act_s"], s["proj"], k["dtb"], k["arow"],
                                                           k["dxrow"], k["snw"], rmat, s["ysc"], s["st_s"], dy_ssm, dproj)
    dact_g, dproj, dsm_g, dgnw, dgal, dgb = _gdn_bwd(n + "gdn_bwd", s["act_g"], s["proj"], k["gb"], k["garow"], k["gnw"],
                                                       s["oraw"], s["st_g"], dy_gdn, dproj)
    dsm_s = hooks.mixers_done(dsm_s)
    dproj = _place_small(n + "d_small", dsm_s, dsm_g, dproj)
    dproj, g["ssm_conv_w"], dcb = _conv_bwd(n + "d_conv_ssm", s["proj"], XBC_OFF, p["ssm_conv_w"], s["pre_s"], dact_s, dproj)
    dproj, g["gdn_conv_w"], _ = _conv_bwd(n + "d_conv_gdn", s["proj"], QKV_OFF, p["gdn_conv_w"], s["pre_g"], dact_g, dproj)
    g["ssm_conv_b"] = dcb.reshape(-1)
    g["ssm_norm_w"] = dsnw.reshape(-1)
    g["ssm_d"] = dd[0, :SSM_HEADS]
    g["ssm_a_log"] = dal[0, :SSM_HEADS]
    g["ssm_dt_bias"] = ddtb[0, :SSM_HEADS]
    g["gdn_norm_w"] = dgnw.reshape(-1)
    g["gdn_a_log"] = dgal[0, LANE_A: LANE_A + GDN_HEADS]
    g["gdn_dt_bias"] = dgb[0, LANE_A: LANE_A + GDN_HEADS]
    dh = _matmul(n + "d_in_proj", "nt", [(dproj, 0, p["w_in"], 0)], t, 1024, PROJ_W, 1024, 1024, 1280, (F32,))
    g["w_in"] = _matmul(n + "dw_in", "tn", [(s["h"], 0, dproj, 0)], 1024, PROJ_W, t, 1024, 1280, tk_tok, (BF16,))
    dx, dxb, g["norm_mix_w"] = _rmsnorm_bwd(n + "d_norm_mix", s["x"], p["norm_mix_w"], dh, dx1)
    g["norm_mix_w"] = g["norm_mix_w"].reshape(-1)
    g["norm_ffn_w"] = g["norm_ffn_w"].reshape(-1)
    return dx, dxb, g


def _local_step(x, tgt, layers, final_norm_w, reduce=False):
    rmat = _expand_matrix()
    saved, params = [], []
    for l in range(DEPTH):
        x, p = layers[l](x)
        x, s = _layer_fwd(l, x, p, rmat)
        saved.append(s)
        params.append(s["p"])
    loss, dx, dxb, dfw = _loss_head("loss_head", x, final_norm_w, tgt)
    grads = [None] * DEPTH
    hooks = _ReduceBesideBackward() if reduce else _NoReduce()
    for l in reversed(range(DEPTH)):
        dx, dxb, grads[l] = _layer_bwd(l, dx, dxb, saved[l], params[l], rmat, hooks)
        if reduce:
            dxb = hooks.layer_done(l, grads[l], dxb)
    if reduce:
        hooks.finish_start(dxb)
    return loss[0, 0], dx, grads, dfw.reshape(-1), hooks if reduce else None


SMALL = ("norm_mix_w", "ssm_conv_b", "ssm_dt_bias", "ssm_a_log", "ssm_d", "ssm_norm_w", "gdn_a_log", "gdn_dt_bias",
         "gdn_norm_w", "norm_ffn_w")
WEIGHTS = ("norm_mix_w", "w_in", "ssm_conv_w", "ssm_conv_b", "ssm_dt_bias", "ssm_a_log", "ssm_d", "ssm_norm_w", "gdn_conv_w",
           "gdn_a_log", "gdn_dt_bias", "gdn_norm_w", "w_proj_ssm", "w_proj_gdn", "w_out", "norm_ffn_w", "w_ffn_in",
           "w_ffn_down", "final_norm_w")


FIRST_USED = ("w_in", "ssm_conv_w", "gdn_conv_w")


def _gather_layer(l, w):
    conv_names = [nm for nm, _ in CONVS]
    groups = ([s for s in BIG + CONVS if s[0] in FIRST_USED], [s for s in BIG + CONVS if s[0] not in FIRST_USED])
    gathered = []
    for i, (specs, tag) in enumerate(zip(groups, ("first", "rest"))):
        shards = [w[nm][l] if nm in conv_names else w[nm][l].astype(BF16) for nm, _ in specs]
        gathered.append(_allgather_seq(f"l{l}_gather_{tag}", shards, collective_id=2 * l + i))
    small = {nm: w[nm][l] for nm in SMALL}

    def use(i, act):
        act, blocks = lax.optimization_barrier((act, gathered[i]))
        return act, {nm: _w_in_from_blocks(g) if nm == "w_in" else _from_gathered(g, axis)
                     for (nm, axis), g in zip(groups[i], blocks)}

    def full_weights(x):
        x, out = use(0, x)
        out.update(small)
        out["late"] = lambda y: use(1, y)
        return x, out

    return full_weights


EARLY_GRADS = ("w_ffn_down", "w_ffn_in", "w_out", "w_proj_ssm", "w_proj_gdn")


class _GradReduceScatter:
    def __init__(self, tag, specs, grads):
        self.tag = tag
        self.specs = specs
        self.blocks = [_w_in_to_blocks(grads[nm]) if nm == "w_in" else _to_dest_major(grads[nm], axis)
                       for nm, axis in specs]

    def _tied(self, started, acts):
        *acts, self.token = lax.optimization_barrier((*acts, started[4]))
        return acts

    def start(self, *acts):
        cc = lax.axis_index("c")
        self.keep = [lax.dynamic_index_in_dim(b.reshape((4, 2) + b.shape[1:]), cc, axis=1, keepdims=False)
                     for b in self.blocks]
        self.to_sibling = _copies_start(f"{self.tag}_to_sibling_start", _sibling_plan, 4, self.blocks, 4)
        return self._tied(self.to_sibling, acts)

    def mid(self, *acts):
        got = _copies_wait(f"{self.tag}_to_sibling_wait", _sibling_plan, self.to_sibling, (acts[0], self.token))
        chip_sums = [_sum_terms(f"{self.tag}_chip_sum_{nm}", [(k[None], 0), (g[None], 0)], BF16)
                     for (nm, _), k, g in zip(self.specs, self.keep, got)]
        self.to_chips = _copies_start(f"{self.tag}_between_chips_start", _chips_plan, 3, chip_sums, 3)
        return self._tied(self.to_chips, acts)

    def end(self, after):
        after = tuple(after) if isinstance(after, (tuple, list)) else (after,)
        landed = _copies_wait(f"{self.tag}_between_chips_wait", _chips_plan, self.to_chips, (*after, self.token))
        my_chip = 2 * lax.axis_index("x") + lax.axis_index("y")
        own = [lax.dynamic_index_in_dim(s, my_chip, axis=0, keepdims=True) for s in self.to_chips[2]]
        return {nm: _sum_terms(f"{self.tag}_total_{nm}", [(o, 0), (e, 0), (e, 1), (e, 2)], F32)
                for (nm, _), o, e in zip(self.specs, own, landed)}


class _NoReduce:
    def ffn_done(self, dx1b):
        return dx1b

    def early_ready(self, l, g, dp1, dp2):
        return dp1, dp2

    def mixers_done(self, dsm):
        return dsm


class _ReduceBesideBackward(_NoReduce):
    def __init__(self):
        self.late = None
        self.early = None
        self.shards = [dict() for _ in range(DEPTH)]

    def ffn_done(self, dx1b):
        if self.late is not None:
            (dx1b,) = self.late.mid(dx1b)
        return dx1b

    def early_ready(self, l, g, dp1, dp2):
        self.early = _GradReduceScatter(f"l{l}_early_grads", [s for s in BIG if s[0] in EARLY_GRADS], g)
        return self.early.start(dp1, dp2)

    def mixers_done(self, dsm):
        (dsm,) = self.early.mid(dsm)
        return dsm

    def layer_done(self, l, g, dxb):
        if self.late is not None:
            self.shards[l + 1].update(self.late.end(dxb))
        self.shards[l].update(self.early.end(dxb))
        self.late = _GradReduceScatter(f"l{l}_late_grads", [s for s in BIG + CONVS if s[0] not in EARLY_GRADS], g)
        (dxb,) = self.late.start(dxb)
        return dxb

    def finish_start(self, dxb):
        self.late.mid(dxb)

    def finish_end(self, after):
        return self.late.end(after)


def _allreduce_small(vecs):
    flat = jnp.concatenate(vecs)
    n = flat.shape[0]
    rows = -(-n // 128)
    rows = -(-rows // 8) * 8
    buf = jnp.pad(flat, (0, rows * 128 - n)).reshape(rows, 128)
    (allv,) = _allgather("gather_small_grads", [buf])
    tot = _sum_terms("small_grads_total", [(allv, d) for d in range(N_DEV)], F32).reshape(-1)
    out, o = [], 0
    for v in vecs:
        out.append(tot[o: o + v.shape[0]])
        o += v.shape[0]
    return out


def kernel(x, norm_mix_w, w_in, ssm_conv_w, ssm_conv_b, ssm_dt_bias, ssm_a_log, ssm_d, ssm_norm_w, gdn_conv_w, gdn_a_log, gdn_dt_bias, gdn_norm_w, w_proj_ssm, w_proj_gdn, w_out, norm_ffn_w, w_ffn_in, w_ffn_down, final_norm_w, loss_target, m_norm_mix_w, m_w_in, m_ssm_conv_w, m_ssm_conv_b, m_ssm_dt_bias, m_ssm_a_log, m_ssm_d, m_ssm_norm_w, m_gdn_conv_w, m_gdn_a_log, m_gdn_dt_bias, m_gdn_norm_w, m_w_proj_ssm, m_w_proj_gdn, m_w_out, m_norm_ffn_w, m_w_ffn_in, m_w_ffn_down, m_final_norm_w, v_norm_mix_w, v_w_in, v_ssm_conv_w, v_ssm_conv_b, v_ssm_dt_bias, v_ssm_a_log, v_ssm_d, v_ssm_norm_w, v_gdn_conv_w, v_gdn_a_log, v_gdn_dt_bias, v_gdn_norm_w, v_w_proj_ssm, v_w_proj_gdn, v_w_out, v_norm_ffn_w, v_w_ffn_in, v_w_ffn_down, v_final_norm_w):
    w = dict(norm_mix_w=norm_mix_w, w_in=w_in, ssm_conv_w=ssm_conv_w, ssm_conv_b=ssm_conv_b, ssm_dt_bias=ssm_dt_bias,
             ssm_a_log=ssm_a_log, ssm_d=ssm_d, ssm_norm_w=ssm_norm_w, gdn_conv_w=gdn_conv_w, gdn_a_log=gdn_a_log,
             gdn_dt_bias=gdn_dt_bias, gdn_norm_w=gdn_norm_w, w_proj_ssm=w_proj_ssm, w_proj_gdn=w_proj_gdn, w_out=w_out,
             norm_ffn_w=norm_ffn_w, w_ffn_in=w_ffn_in, w_ffn_down=w_ffn_down, final_norm_w=final_norm_w)
    m = dict(norm_mix_w=m_norm_mix_w, w_in=m_w_in, ssm_conv_w=m_ssm_conv_w, ssm_conv_b=m_ssm_conv_b, ssm_dt_bias=m_ssm_dt_bias,
             ssm_a_log=m_ssm_a_log, ssm_d=m_ssm_d, ssm_norm_w=m_ssm_norm_w, gdn_conv_w=m_gdn_conv_w, gdn_a_log=m_gdn_a_log,
             gdn_dt_bias=m_gdn_dt_bias, gdn_norm_w=m_gdn_norm_w, w_proj_ssm=m_w_proj_ssm, w_proj_gdn=m_w_proj_gdn,
             w_out=m_w_out, norm_ffn_w=m_norm_ffn_w, w_ffn_in=m_w_ffn_in, w_ffn_down=m_w_ffn_down,
             final_norm_w=m_final_norm_w)
    v = dict(norm_mix_w=v_norm_mix_w, w_in=v_w_in, ssm_conv_w=v_ssm_conv_w, ssm_conv_b=v_ssm_conv_b, ssm_dt_bias=v_ssm_dt_bias,
             ssm_a_log=v_ssm_a_log, ssm_d=v_ssm_d, ssm_norm_w=v_ssm_norm_w, gdn_conv_w=v_gdn_conv_w, gdn_a_log=v_gdn_a_log,
             gdn_dt_bias=v_gdn_dt_bias, gdn_norm_w=v_gdn_norm_w, w_proj_ssm=v_w_proj_ssm, w_proj_gdn=v_w_proj_gdn,
             w_out=v_w_out, norm_ffn_w=v_norm_ffn_w, w_ffn_in=v_w_ffn_in, w_ffn_down=v_w_ffn_down,
             final_norm_w=v_final_norm_w)

    layers = [_gather_layer(l, w) for l in range(DEPTH)]
    loss_part, dx, lgrads, dfw, reducer = _local_step(x[0], loss_target[0], layers, final_norm_w, reduce=True)
    loss = lax.psum(loss_part, ("x", "y", "c"))
    shard_grads = reducer.shards
    late = [nm for nm, _ in BIG + CONVS if nm not in EARLY_GRADS]
    grad = {nm: [shard_grads[l][nm] for l in range(DEPTH)] for nm in EARLY_GRADS}
    small_vecs = [lgrads[l][nm].reshape(-1) for l in range(DEPTH) for nm in SMALL] + [dfw]
    small_sum = _allreduce_small(small_vecs)
    for i, nm in enumerate(SMALL):
        grad[nm] = jnp.stack([small_sum[l * len(SMALL) + i].reshape(w[nm].shape[1:]) for l in range(DEPTH)])
    grad["final_norm_w"] = small_sum[-1]

    deltas, new_m, new_v = {}, {}, {}
    for nm in [n for n in WEIGHTS if n not in late]:
        if nm == "w_ffn_in":
            tr = lambda a: jnp.transpose(a, (0, 2, 1))
            outs = _adamw("adamw_" + nm, tr(w[nm]), [g.T for g in grad[nm]], tr(m[nm]), tr(v[nm]))
            grad[nm], deltas[nm], new_m[nm], new_v[nm] = (tr(o) for o in outs)
        else:
            grad[nm], deltas[nm], new_m[nm], new_v[nm] = _adamw("adamw_" + nm, w[nm], grad[nm], m[nm], v[nm])
    shard_grads[0].update(reducer.finish_end([deltas[nm] for nm in EARLY_GRADS]))
    for nm in late:
        gl = [shard_grads[l][nm] for l in range(DEPTH)]
        if nm == "w_in":
            fwd, back = (lambda a: jnp.transpose(a, (2, 0, 1))), (lambda a: jnp.transpose(a, (1, 2, 0)))
            g3 = jnp.stack([g.T for g in gl], axis=1)
            outs = _adamw_rows("adamw_" + nm, fwd(w[nm]), g3, fwd(m[nm]), fwd(v[nm]))
            grad[nm], deltas[nm], new_m[nm], new_v[nm] = (back(o) for o in (g3,) + tuple(outs))
        else:
            grad[nm], deltas[nm], new_m[nm], new_v[nm] = _adamw("adamw_" + nm, w[nm], gl, m[nm], v[nm])
    return (loss, dx[None], *[grad[nm] for nm in WEIGHTS], *[deltas[nm] for nm in WEIGHTS],
            *[new_m[nm] for nm in WEIGHTS], *[new_v[nm] for nm in WEIGHTS])
```

```python
import functools

import jax
import jax.numpy as jnp
from jax import lax
from jax.experimental import pallas as pl
from jax.experimental.pallas import tpu as pltpu
from jax.experimental.pallas import tpu_sc as plsc

F32 = jnp.float32
BF16 = jnp.bfloat16
HI = lax.Precision.HIGHEST
SDS = jax.ShapeDtypeStruct

D_MODEL = 1024
DEPTH = 2
SSM_HEADS = 16
SSM_P = 64
SSM_N = 128
SSM_GROUPS = 2
SSM_CONV = 1536
GDN_HEADS = 8
GDN_DK = 128
GDN_QKV = 3072
CONV_K = 4
CHUNK = 64
SCAN_CHUNKS_PER_STEP = 8
FFN = 2816
IN_DIM = 8736
EPS = 1e-6
N_DEV = 8

Z_OFF = 0
GZ_OFF = 1024
G1_OFF = 2048
G2_OFF = 3072
QKV_OFF = 4096
XBC_OFF = 7168
SM_OFF = 8704
PROJ_W = 8960
LANE_A = 16
LANE_B = 24
O_Z, O_XBC, O_DT, O_QKV, O_GZ, O_A, O_B, O_G1, O_G2 = 0, 1024, 2560, 2576, 5648, 6672, 6680, 6688, 7712

ADAM_LR = 0.001
ADAM_B1 = 0.9
ADAM_B2 = 0.999
ADAM_EPS = 1e-08
ADAM_WD = 0.01
ADAM_STEP = 10

V7X_VMEM_LIMIT = 48 * 1024 * 1024

NN = ((1,), (0,))
NT = ((1,), (1,))
TN = ((0,), (0,))


def _bdot(a, b, dims):
    return lax.dot_general(a.astype(BF16), b.astype(BF16), (dims, ((), ())), preferred_element_type=F32)


def _hdot(a, b, dims=NN):
    return lax.dot_general(a, b, (dims, ((), ())), precision=HI, preferred_element_type=F32)


def _sigmoid(x):
    return 1.0 / (1.0 + jnp.exp(-x))


def _softplus(x):
    return jnp.maximum(x, 0.0) + jnp.log(1.0 + jnp.exp(-jnp.abs(x)))


def _params(dims):
    return pltpu.CompilerParams(dimension_semantics=dims, vmem_limit_bytes=V7X_VMEM_LIMIT)


def _rowsum(x):
    return jnp.sum(x, axis=-1, keepdims=True)


def _colsum(x):
    return jnp.sum(x, axis=0, keepdims=True)


def _matmul(name, mode, pairs, m, n, kdim, tm, tn, tk, out_dtypes, epi=None, extras=(), second=None):
    tm, tn, tk = min(tm, m), min(tn, n), min(tk, kdim)
    nk = kdim // tk
    assert m % tm == 0 and n % tn == 0 and kdim % tk == 0, (name, m, n, kdim, tm, tn, tk)
    assert second is None or nk == 1
    in_specs, args = [], []
    for a, a_off, b, b_off in pairs:
        if mode == "nn":
            in_specs.append(pl.BlockSpec((tm, tk), lambda i, j, k, o=a_off: (i, k + o)))
            in_specs.append(pl.BlockSpec((tk, tn), lambda i, j, k, o=b_off: (k, j + o)))
            dims = NN
        elif mode == "nt":
            in_specs.append(pl.BlockSpec((tm, tk), lambda i, j, k, o=a_off: (i, k + o)))
            in_specs.append(pl.BlockSpec((tn, tk), lambda i, j, k, o=b_off: (j, k + o)))
            dims = NT
        else:
            in_specs.append(pl.BlockSpec((tk, tm), lambda i, j, k, o=a_off: (k, i + o)))
            in_specs.append(pl.BlockSpec((tk, tn), lambda i, j, k, o=b_off: (k, j + o)))
            dims = TN
        args += [a, b]
    for e, e_off in extras:
        in_specs.append(pl.BlockSpec((tm, tn), lambda i, j, k, o=e_off: (i, j + o)))
        args.append(e)
    npair, nex, nout = len(pairs), len(extras), len(out_dtypes)

    def body(*refs):
        prefs = refs[: 2 * npair]
        erefs = refs[2 * npair: 2 * npair + nex]
        orefs = refs[2 * npair + nex: 2 * npair + nex + nout]

        def finish(*res):
            outs = res if epi is None else epi(*res, *[e[...] for e in erefs])
            for o, r in zip(orefs, outs):
                o[...] = r.astype(o.dtype)

        def total(ps):
            s = _bdot(prefs[2 * ps[0]][...], prefs[2 * ps[0] + 1][...], dims)
            for p in ps[1:]:
                s = s + _bdot(prefs[2 * p][...], prefs[2 * p + 1][...], dims)
            return s

        if second is not None:
            finish(total(list(range(second))), total(list(range(second, npair))))
            return
        s = total(list(range(npair)))
        if nk == 1:
            finish(s)
            return
        acc = refs[-1]
        k = pl.program_id(2)

        @pl.when(k == 0)
        def _():
            acc[...] = s

        @pl.when(k > 0)
        def _():
            acc[...] += s

        @pl.when(k == nk - 1)
        def _():
            finish(acc[...])

    out_shape, out_specs = [], []
    for od in out_dtypes:
        if isinstance(od, tuple):
            dt, full_w, blk_w, cblk = od
            assert n == tn
            out_shape.append(SDS((m, full_w), dt))
            out_specs.append(pl.BlockSpec((tm, blk_w), lambda i, j, k, c=cblk: (i, c)))
        else:
            out_shape.append(SDS((m, n), od))
            out_specs.append(pl.BlockSpec((tm, tn), lambda i, j, k: (i, j)))
    out_shape, out_specs = tuple(out_shape), tuple(out_specs)
    res = pl.pallas_call(
        body, grid=(m // tm, n // tn, nk), in_specs=in_specs, out_specs=out_specs, out_shape=out_shape,
        scratch_shapes=[pltpu.VMEM((tm, tn), F32)] if nk > 1 else [], name=name,
        compiler_params=_params(("parallel", "parallel", "arbitrary")),
    )(*args)
    return res if nout > 1 else res[0]


def _rmsnorm_fwd(name, x, w):
    t, d = x.shape
    tm = min(512, t)

    def body(x_ref, w_ref, h_ref):
        xv = x_ref[...]
        r = lax.rsqrt(jnp.mean(xv * xv, axis=-1, keepdims=True) + EPS)
        h_ref[...] = (xv * r * w_ref[...]).astype(BF16)

    return pl.pallas_call(
        body, grid=(t // tm,),
        in_specs=[pl.BlockSpec((tm, d), lambda i: (i, 0)), pl.BlockSpec((1, d), lambda i: (0, 0))],
        out_specs=pl.BlockSpec((tm, d), lambda i: (i, 0)), out_shape=SDS((t, d), BF16), name=name,
        compiler_params=_params(("parallel",)),
    )(x, w.reshape(1, d))


def _rmsnorm_bwd(name, x, w, dh, dres):
    t, d = x.shape
    tm = min(512, t)

    def body(x_ref, w_ref, dh_ref, dres_ref, dx_ref, dxb_ref, dw_ref):
        xv = x_ref[...]
        r = lax.rsqrt(jnp.mean(xv * xv, axis=-1, keepdims=True) + EPS)
        xh = xv * r
        dhv = dh_ref[...].astype(F32)
        dxh = dhv * w_ref[...]
        dx = r * (dxh - xh * jnp.mean(dxh * xh, axis=-1, keepdims=True)) + dres_ref[...]
        dx_ref[...] = dx
        dxb_ref[...] = dx.astype(BF16)

        @pl.when(pl.program_id(0) == 0)
        def _():
            dw_ref[...] = jnp.zeros_like(dw_ref)

        dw_ref[...] += _colsum(dhv * xh)

    row = pl.BlockSpec((tm, d), lambda i: (i, 0))
    vec = pl.BlockSpec((1, d), lambda i: (0, 0))
    return pl.pallas_call(
        body, grid=(t // tm,), in_specs=[row, vec, row, row], out_specs=(row, row, vec),
        out_shape=(SDS((t, d), F32), SDS((t, d), BF16), SDS((1, d), F32)), name=name,
        compiler_params=_params(("arbitrary",)),
    )(x, w.reshape(1, d), dh, dres)


def _loss_head(name, x, w, tgt):
    t, d = x.shape
    tm = min(512, t)

    def body(x_ref, w_ref, t_ref, loss_ref, dx_ref, dxb_ref, dw_ref):
        xv = x_ref[...]
        wv = w_ref[...]
        r = lax.rsqrt(jnp.mean(xv * xv, axis=-1, keepdims=True) + EPS)
        xh = xv * r
        e = xh * wv - t_ref[...]
        dy = e * (1.0 / d)
        dxh = dy * wv
        dx = r * (dxh - xh * jnp.mean(dxh * xh, axis=-1, keepdims=True))
        dx_ref[...] = dx
        dxb_ref[...] = dx.astype(BF16)

        @pl.when(pl.program_id(0) == 0)
        def _():
            dw_ref[...] = jnp.zeros_like(dw_ref)
            loss_ref[...] = jnp.zeros_like(loss_ref)

        dw_ref[...] += _colsum(dy * xh)
        loss_ref[...] += 0.5 * jnp.sum(jnp.mean(e * e, axis=-1, keepdims=True), axis=0, keepdims=True)

    row = pl.BlockSpec((tm, d), lambda i: (i, 0))
    vec = pl.BlockSpec((1, d), lambda i: (0, 0))
    return pl.pallas_call(
        body, grid=(t // tm,), in_specs=[row, vec, row],
        out_specs=(pl.BlockSpec((1, 1), lambda i: (0, 0)), row, row, vec),
        out_shape=(SDS((1, 1), F32), SDS((t, d), F32), SDS((t, d), BF16), SDS((1, d), F32)), name=name,
        compiler_params=_params(("arbitrary",)),
    )(x, w.reshape(1, d), tgt)


def _shift_down(u, s, row):
    return jnp.where(row >= s, pltpu.roll(u, shift=s, axis=0), 0.0)


def _conv_fwd(name, src, col0, w, b):
    t = src.shape[0]
    c = w.shape[1]
    tc = 256
    assert c % tc == 0 and col0 % tc == 0

    def body(u_ref, w_ref, b_ref, o_ref, pre_ref):
        u = u_ref[...]
        wv = w_ref[...]
        rolled = [None] + [pltpu.roll(u, shift=s, axis=0) for s in range(1, CONV_K)]
        row8 = lax.broadcasted_iota(jnp.int32, (8, tc), 0)
        pre = b_ref[...] + wv[3:4, :] * u
        head = b_ref[...] + wv[3:4, :] * u[:8]
        for s in range(1, CONV_K):
            pre = pre + wv[3 - s: 4 - s, :] * rolled[s]
            head = head + wv[3 - s: 4 - s, :] * jnp.where(row8 >= s, rolled[s][:8], 0.0)
        pre_ref[...] = pre
        o_ref[...] = pre * _sigmoid(pre)
        pre_ref[:8, :] = head
        o_ref[:8, :] = head * _sigmoid(head)

    col = pl.BlockSpec((t, tc), lambda j: (0, j))
    return pl.pallas_call(
        body, grid=(c // tc,),
        in_specs=[pl.BlockSpec((t, tc), lambda j: (0, j + col0 // tc)), pl.BlockSpec((CONV_K, tc), lambda j: (0, j)),
                  pl.BlockSpec((1, tc), lambda j: (0, j))],
        out_specs=(col, col), out_shape=(SDS((t, c), F32), SDS((t, c), F32)), name=name,
        compiler_params=_params(("parallel",)),
    )(src, w, b)


def _place_small(name, dsm_a, dsm_b, dproj):
    t = dsm_a.shape[0]
    width = PROJ_W - SM_OFF
    tr = min(512, t)

    def body(a_ref, b_ref, dproj_ref, o_ref):
        o_ref[:, :128] = a_ref[...] + b_ref[...]
        o_ref[:, 128:] = jnp.zeros((tr, width - 128), BF16)

    row = pl.BlockSpec((tr, 128), lambda i: (i, 0))
    return pl.pallas_call(
        body, grid=(t // tr,), in_specs=[row, row, ANY],
        out_specs=pl.BlockSpec((tr, width), lambda i: (i, SM_OFF // width)), out_shape=SDS(dproj.shape, BF16),
        input_output_aliases={2: 0}, name=name, compiler_params=_params(("parallel",)),
    )(dsm_a, dsm_b, dproj)


def _conv_bwd(name, src, col0, w, pre, dact, dproj):
    t = src.shape[0]
    c = w.shape[1]
    tc = 256

    def body(u_ref, w_ref, pre_ref, da_ref, dproj_ref, du_ref, dw_ref, db_ref):
        u = u_ref[...]
        wv = w_ref[...]
        prev = pre_ref[...]
        sg = _sigmoid(prev)
        dpre = da_ref[...] * (sg * (1.0 + prev * (1.0 - sg)))
        edge = 16
        rows = lax.broadcasted_iota(jnp.int32, (edge, tc), 0)
        du = wv[3:4, :] * dpre
        tail = wv[3:4, :] * dpre[t - edge:]
        dw_ref[3:4, :] = _colsum(dpre * u)
        for s in range(1, CONV_K):
            up = pltpu.roll(dpre, shift=t - s, axis=0)
            wrapped = rows >= edge - s
            du = du + wv[3 - s: 4 - s, :] * up
            tail = tail + wv[3 - s: 4 - s, :] * jnp.where(wrapped, 0.0, up[t - edge:])
            dw_ref[3 - s: 4 - s, :] = _colsum(up * u) - _colsum(jnp.where(wrapped, up[t - edge:] * u[t - edge:], 0.0))
        du_ref[...] = du.astype(BF16)
        du_ref[t - edge:, :] = tail.astype(BF16)
        db_ref[...] = _colsum(dpre)

    col = pl.BlockSpec((t, tc), lambda j: (0, j))
    return pl.pallas_call(
        body, grid=(c // tc,),
        in_specs=[pl.BlockSpec((t, tc), lambda j: (0, j + col0 // tc)), pl.BlockSpec((CONV_K, tc), lambda j: (0, j)),
                  col, col, ANY],
        out_specs=(pl.BlockSpec((t, tc), lambda j: (0, j + col0 // tc)), pl.BlockSpec((CONV_K, tc), lambda j: (0, j)),
                   pl.BlockSpec((1, tc), lambda j: (0, j))),
        out_shape=(SDS(dproj.shape, BF16), SDS((CONV_K, c), F32), SDS((1, c), F32)), name=name,
        input_output_aliases={4: 0},
        compiler_params=_params(("parallel",)),
    )(src, w, pre, dact, dproj)


def _tri(q):
    ii = lax.broadcasted_iota(jnp.int32, (q, q), 0)
    jj = lax.broadcasted_iota(jnp.int32, (q, q), 1)
    return ii, jj


def _dot01(x, r01, dims, terms=3):
    out, rem = None, x
    for i in range(terms):
        hi = rem.astype(BF16)
        d = lax.dot_general(hi, r01, (dims, ((), ())), preferred_element_type=F32)
        out = d if out is None else out + d
        if i + 1 < terms:
            rem = rem - hi.astype(F32)
    return out


def _ssd_common(act, sm, dtb, arow, rmat):
    q = CHUNK
    ii, jj = _tri(q)
    lane = lax.broadcasted_iota(jnp.int32, (q, 128), 1)
    m16 = lane < SSM_HEADS
    dt = jnp.where(m16, _softplus(sm + dtb), 0.0)
    a = dt * arow
    tril = (ii >= jj).astype(F32)
    triu = (ii <= jj).astype(F32)
    acum = _hdot(tril, a)
    acum_r = _hdot(a.T, triu)
    dtx = _dot01(dt, rmat, NN)
    acx = _dot01(acum, rmat, NN)
    ex = jnp.exp(acx)
    alx = acx[q - 1: q, :]
    dex = jnp.exp(alx - acx)
    xs = act[:, :1024]
    return dict(ii=ii, jj=jj, m16=m16, dt=dt, a=a, triu=triu, acum=acum, acum_r=acum_r, dtx=dtx, ex=ex, dex=dex,
                elx=jnp.exp(alx), xs=xs, x=xs * dtx)


def _ssd_lmat(cm, h):
    return jnp.where(cm["ii"] >= cm["jj"], jnp.exp(cm["acum"][:, h: h + 1] - cm["acum_r"][h: h + 1, :]), 0.0)


def _ssd_fwd(name, act, proj, dtb, arow, dxrow, nw, rmat):
    t = act.shape[0]
    q = CHUNK
    nc = t // q
    hg = SSM_HEADS // SSM_GROUPS
    gw = hg * SSM_P

    def body(act_ref, z_ref, sm_ref, dtb_ref, arow_ref, dx_ref, nw_ref, r_ref, y_ref, ys_ref, st_ref, s_scr, yd_scr):
        @pl.when(pl.program_id(0) == 0)
        def _():
            s_scr[...] = jnp.zeros_like(s_scr)

        s_all = s_scr[...]
        for sub in range(cps):
            rows = pl.ds(sub * q, q)
            s_all = chunk(act_ref.at[rows, :], z_ref.at[rows, :], sm_ref.at[rows, :], dtb_ref, arow_ref, dx_ref, nw_ref, r_ref,
                          y_ref.at[rows, :], ys_ref.at[rows, :], st_ref.at[sub], yd_scr.at[rows, :], s_all)
        s_scr[...] = s_all

    def chunk(act_ref, z_ref, sm_ref, dtb_ref, arow_ref, dx_ref, nw_ref, r_ref, y_ref, ys_ref, st_ref, yd_scr, s_all):
        st_ref[...] = s_all
        actv = act_ref[...]
        cm = _ssd_common(actv, sm_ref[...], dtb_ref[...], arow_ref[...], r_ref[...])
        x = cm["x"]
        xd = x * cm["dex"]
        yoffs, snew = [], []
        for g in range(SSM_GROUPS):
            bg = actv[:, 1024 + g * SSM_N: 1024 + (g + 1) * SSM_N]
            cg = actv[:, 1280 + g * SSM_N: 1280 + (g + 1) * SSM_N]
            sg = s_all[:, g * gw: (g + 1) * gw]
            cb = _bdot(cg, bg, NT)
            yoffs.append(_bdot(cg, sg, NN))
            snew.append(_bdot(bg, xd[:, g * gw: (g + 1) * gw], TN))
            for r in range(hg):
                h = g * hg + r
                mm = cb * _ssd_lmat(cm, h)
                yd_scr[:, h * SSM_P: (h + 1) * SSM_P] = _bdot(mm, x[:, h * SSM_P: (h + 1) * SSM_P], NN)
        s_next = s_all * cm["elx"] + jnp.concatenate(snew, axis=1)
        ysc = yd_scr[...] + jnp.concatenate(yoffs, axis=1) * cm["ex"]
        ys_ref[...] = ysc
        zv = z_ref[...]
        yg = (ysc + dx_ref[...] * cm["xs"]) * (zv * _sigmoid(zv))
        nwv = nw_ref[...]
        for g in range(SSM_GROUPS):
            sl = yg[:, g * gw: (g + 1) * gw]
            rr = lax.rsqrt(jnp.mean(sl * sl, axis=-1, keepdims=True) + EPS)
            y_ref[:, g * gw: (g + 1) * gw] = (sl * rr * nwv[:, g * gw: (g + 1) * gw]).astype(BF16)
        return s_next

    cps = SCAN_CHUNKS_PER_STEP if nc % SCAN_CHUNKS_PER_STEP == 0 else 1
    qq = cps * q
    vec128 = pl.BlockSpec((1, 128), lambda c: (0, 0))
    vec1k = pl.BlockSpec((1, 1024), lambda c: (0, 0))
    return pl.pallas_call(
        body, grid=(nc // cps,),
        in_specs=[pl.BlockSpec((qq, SSM_CONV), lambda c: (c, 0)), pl.BlockSpec((qq, 1024), lambda c: (c, Z_OFF // 1024)),
                  pl.BlockSpec((qq, 128), lambda c: (c, SM_OFF // 128)), vec128, vec128, vec1k, vec1k,
                  pl.BlockSpec((128, 1024), lambda c: (0, 0))],
        out_specs=(pl.BlockSpec((qq, 1024), lambda c: (c, 0)), pl.BlockSpec((qq, 1024), lambda c: (c, 0)),
                   pl.BlockSpec((cps, 128, 1024), lambda c: (c, 0, 0))),
        out_shape=(SDS((t, 1024), BF16), SDS((t, 1024), F32), SDS((nc, 128, 1024), F32)),
        scratch_shapes=[pltpu.VMEM((128, 1024), F32), pltpu.VMEM((qq, 1024), F32)], name=name,
        compiler_params=_params(("arbitrary",)),
    )(act, proj, proj, dtb, arow, dxrow, nw, rmat)


def _ssd_bwd(name, act, proj, dtb, arow, dxrow, nw, rmat, ysc, states, dy, dproj):
    t = act.shape[0]
    q = CHUNK
    nc = t // q
    hg = SSM_HEADS // SSM_GROUPS
    gw = hg * SSM_P

    def body(act_ref, z_ref, sm_ref, dtb_ref, arow_ref, dx_ref, nw_ref, r_ref, ys_ref, st_ref, dy_ref, dproj_ref,
             dact_ref, dz_ref, dsm_ref, dnw_ref, dd_ref, dal_ref, ddtb_ref, ds_scr, dxd_scr):
        @pl.when(pl.program_id(0) == 0)
        def _():
            ds_scr[...] = jnp.zeros_like(ds_scr)
            dnw_ref[...] = jnp.zeros_like(dnw_ref)
            dd_ref[...] = jnp.zeros_like(dd_ref)
            dal_ref[...] = jnp.zeros_like(dal_ref)
            ddtb_ref[...] = jnp.zeros_like(ddtb_ref)

        dsn = ds_scr[...]
        for sub in reversed(range(cps)):
            rows = pl.ds(sub * q, q)
            dsn = chunk(act_ref.at[rows, :], z_ref.at[rows, :], sm_ref.at[rows, :], dtb_ref, arow_ref, dx_ref, nw_ref, r_ref,
                        ys_ref.at[rows, :], st_ref.at[sub], dy_ref.at[rows, :], dact_ref.at[rows, :], dz_ref.at[rows, :],
                        dsm_ref.at[rows, :], dnw_ref, dd_ref, dal_ref, ddtb_ref, dxd_scr.at[rows, :], dsn)
        ds_scr[...] = dsn

    def chunk(act_ref, z_ref, sm_ref, dtb_ref, arow_ref, dx_ref, nw_ref, r_ref, ys_ref, st_ref, dy_ref,
              dact_ref, dz_ref, dsm_ref, dnw_ref, dd_ref, dal_ref, ddtb_ref, dxd_scr, dsn):
        actv = act_ref[...]
        smv = sm_ref[...]
        rmat_v = r_ref[...]
        cm = _ssd_common(actv, smv, dtb_ref[...], arow_ref[...], rmat_v)
        ii, jj = cm["ii"], cm["jj"]
        x, xs = cm["x"], cm["xs"]
        s_all = st_ref[...]
        ysv = ys_ref[...]
        dxr = dx_ref[...]
        y = ysv + dxr * xs
        zv = z_ref[...]
        sz = _sigmoid(zv)
        silz = zv * sz
        yg = y * silz
        dout = dy_ref[...]
        nwv = nw_ref[...]
        dyn = dout * nwv
        yn_parts, dyg_parts = [], []
        for g in range(SSM_GROUPS):
            sl = yg[:, g * gw: (g + 1) * gw]
            rr = lax.rsqrt(jnp.mean(sl * sl, axis=-1, keepdims=True) + EPS)
            yn = sl * rr
            dn = dyn[:, g * gw: (g + 1) * gw]
            yn_parts.append(yn)
            dyg_parts.append(rr * (dn - yn * jnp.mean(dn * yn, axis=-1, keepdims=True)))
        dnw_ref[...] += _colsum(dout * jnp.concatenate(yn_parts, axis=1))
        dyg = jnp.concatenate(dyg_parts, axis=1)
        dyv = dyg * silz
        dz_ref[...] = (dyg * y * (sz * (1.0 + zv * (1.0 - sz)))).astype(BF16)
        dd_ref[...] += _dot01(_colsum(dyv * xs), rmat_v, NT)
        dxs = dyv * dxr
        dcs = dyv * cm["ex"]
        xd = x * cm["dex"]
        dxst_parts, ds_parts, db_parts, dc_parts, yoff_parts, wcol_rows = [], [], [], [], [], []
        lane128 = lax.broadcasted_iota(jnp.int32, (q, 128), 1)
        wrow = jnp.zeros((q, 128), F32)
        for g in range(SSM_GROUPS):
            bg = actv[:, 1024 + g * SSM_N: 1024 + (g + 1) * SSM_N]
            cg = actv[:, 1280 + g * SSM_N: 1280 + (g + 1) * SSM_N]
            sg = s_all[:, g * gw: (g + 1) * gw]
            dsng = dsn[:, g * gw: (g + 1) * gw]
            dcsg = dcs[:, g * gw: (g + 1) * gw]
            dcg = _bdot(dcsg, sg, NT)
            yoff_parts.append(_bdot(cg, sg, NN))
            ds_parts.append(_bdot(cg, dcsg, TN))
            dxst_parts.append(_bdot(bg, dsng, NN))
            dbg = _bdot(xd[:, g * gw: (g + 1) * gw], dsng, NT)
            cb = _bdot(cg, bg, NT)
            dcb = jnp.zeros((q, q), F32)
            for r in range(hg):
                h = g * hg + r
                lm = _ssd_lmat(cm, h)
                mm = cb * lm
                dyh = dyv[:, h * SSM_P: (h + 1) * SSM_P]
                dm = jnp.where(ii >= jj, _bdot(dyh, x[:, h * SSM_P: (h + 1) * SSM_P], NT), 0.0)
                dxd_scr[:, h * SSM_P: (h + 1) * SSM_P] = _bdot(mm, dyh, TN)
                dcb = dcb + dm * lm
                wm = dm * mm
                wrow = wrow + jnp.where(lane128 == h, _rowsum(wm), 0.0)
                wcol_rows.append(_colsum(wm))
            dc_parts.append(dcg + _bdot(dcb, bg, NN))
            db_parts.append(dbg + _bdot(dcb, cg, TN))
        dxst = jnp.concatenate(dxst_parts, axis=1) * cm["dex"]
        dx = dxd_scr[...] + dxst
        ds_prev = jnp.concatenate(ds_parts, axis=1) + dsn * cm["elx"]
        wcol = jnp.concatenate(wcol_rows + [jnp.zeros((128 - SSM_HEADS, q), F32)], axis=0).T
        yoff = jnp.concatenate(yoff_parts, axis=1) * cm["ex"]
        xdxst = x * dxst
        dac = wrow - wcol + _dot01(dyv * yoff - xdxst, rmat_v, NT)
        last = _dot01(_colsum(dsn * s_all) * cm["elx"] + _colsum(xdxst), rmat_v, NT)
        rowq = lax.broadcasted_iota(jnp.int32, (q, 128), 0)
        dac = dac + jnp.where(rowq == q - 1, last, 0.0)
        da = _hdot(cm["triu"], dac)
        arow_v = arow_ref[...]
        ddt = da * arow_v + _dot01(dx * xs, rmat_v, NT)
        dxs = dxs + dx * cm["dtx"]
        dal_ref[...] += _colsum(da * cm["a"])
        ddtraw = jnp.where(cm["m16"], ddt * _sigmoid(smv + dtb_ref[...]), 0.0)
        ddtb_ref[...] += _colsum(ddtraw)
        dsm_ref[...] = ddtraw.astype(BF16)
        dact_ref[:, :1024] = dxs
        for g in range(SSM_GROUPS):
            dact_ref[:, 1024 + g * SSM_N: 1024 + (g + 1) * SSM_N] = db_parts[g]
            dact_ref[:, 1280 + g * SSM_N: 1280 + (g + 1) * SSM_N] = dc_parts[g]
        return ds_prev

    cps = SCAN_CHUNKS_PER_STEP if nc % SCAN_CHUNKS_PER_STEP == 0 else 1
    qq = cps * q
    rev = lambda c: nc // cps - 1 - c
    vec128 = pl.BlockSpec((1, 128), lambda c: (0, 0))
    vec1k = pl.BlockSpec((1, 1024), lambda c: (0, 0))
    return pl.pallas_call(
        body, grid=(nc // cps,),
        in_specs=[pl.BlockSpec((qq, SSM_CONV), lambda c: (rev(c), 0)),
                  pl.BlockSpec((qq, 1024), lambda c: (rev(c), Z_OFF // 1024)),
                  pl.BlockSpec((qq, 128), lambda c: (rev(c), SM_OFF // 128)), vec128, vec128, vec1k, vec1k,
                  pl.BlockSpec((128, 1024), lambda c: (0, 0)),
                  pl.BlockSpec((qq, 1024), lambda c: (rev(c), 0)), pl.BlockSpec((cps, 128, 1024), lambda c: (rev(c), 0, 0)),
                  pl.BlockSpec((qq, 1024), lambda c: (rev(c), 0)), ANY],
        out_specs=(pl.BlockSpec((qq, SSM_CONV), lambda c: (rev(c), 0)),
                   pl.BlockSpec((qq, 1024), lambda c: (rev(c), Z_OFF // 1024)),
                   pl.BlockSpec((qq, 128), lambda c: (rev(c), 0)), vec1k, vec128, vec128, vec128),
        out_shape=(SDS((t, SSM_CONV), F32), SDS(dproj.shape, BF16), SDS((t, 128), BF16), SDS((1, 1024), F32),
                   SDS((1, 128), F32), SDS((1, 128), F32), SDS((1, 128), F32)),
        input_output_aliases={11: 1},
        scratch_shapes=[pltpu.VMEM((128, 1024), F32), pltpu.VMEM((qq, 1024), F32)], name=name,
        compiler_params=_params(("arbitrary",)),
    )(act, proj, proj, dtb, arow, dxrow, nw, rmat, ysc, states, dy, dproj)


def _split(a):
    hi = a.astype(BF16)
    return hi, (a - hi.astype(F32)).astype(BF16)


def _dot3(a, b, dims=NN):
    (ah, al), (bh, bl) = a, b

    def d(x, y):
        return lax.dot_general(x, y, (dims, ((), ())), preferred_element_type=F32)

    return d(ah, bh) + (d(ah, bl) + d(al, bh))


def _tri_inverses(amats, ii, jj):
    eye = jnp.where(ii == jj, 1.0, 0.0)
    tms = [eye - a for a in amats]
    sp = [_split(a) for a in amats]
    for _ in range(5):
        sp = [_split(_dot3(s, s)) for s in sp]
        tms = [t + _dot3(_split(t), s) for t, s in zip(tms, sp)]
    return tms


def _gdn_common(sm, gb, garow):
    q = CHUNK
    ii, jj = _tri(q)
    lane = lax.broadcasted_iota(jnp.int32, (q, 128), 1)
    ma = (lane >= LANE_A) & (lane < LANE_A + GDN_HEADS)
    spre = sm + gb
    g = jnp.where(ma, garow * _softplus(spre), 0.0)
    beta = _sigmoid(sm)
    tril = (ii >= jj).astype(F32)
    triu = (ii <= jj).astype(F32)
    gc = _hdot(tril, g)
    gc_r = _hdot(g.T, triu)
    return dict(ii=ii, jj=jj, lane=lane, ma=ma, spre=spre, g=g, beta=beta, triu=triu, gc=gc, gc_r=gc_r)


def _each(f, *lists):
    return [f(*xs) for xs in zip(*lists)]


GDN_SCALE = GDN_DK ** -0.5
GDN_BWD_HEAD_GROUPS = (range(GDN_HEADS),)


def _gdn_heads(cm, actv, states, heads=range(GDN_HEADS)):
    q = CHUNK
    ii, jj = cm["ii"], cm["jj"]
    qr = [actv[:, h * 128: (h + 1) * 128] for h in heads]
    kr = [actv[:, 1024 + h * 128: 1024 + (h + 1) * 128] for h in heads]
    v = [actv[:, 2048 + h * 128: 2048 + (h + 1) * 128] for h in heads]
    rq = _each(lambda x: lax.rsqrt(_rowsum(x * x) + EPS), qr)
    rk = _each(lambda x: lax.rsqrt(_rowsum(x * x) + EPS), kr)
    qn = _each(lambda x, r: x * r * GDN_SCALE, qr, rq)
    kn = _each(lambda x, r: x * r, kr, rk)
    gcc = [cm["gc"][:, LANE_A + h: LANE_A + h + 1] for h in heads]
    gcr = [cm["gc_r"][LANE_A + h: LANE_A + h + 1, :] for h in heads]
    bcol = [cm["beta"][:, LANE_B + h: LANE_B + h + 1] for h in heads]
    dm = _each(lambda c, r: jnp.where(ii >= jj, jnp.exp(c - r), 0.0), gcc, gcr)
    kq = _each(lambda k, a: _bdot(jnp.concatenate([k, a], axis=0), k, NT), kn, qn)
    ak = _each(lambda x, d: jnp.where(ii > jj, x[:q] * d, 0.0), kq, dm)
    qkm = _each(lambda x, d: jnp.where(ii >= jj, x[q:] * d, 0.0), kq, dm)
    tm = _tri_inverses(_each(lambda a, b: a * b, ak, bcol), ii, jj)
    eg = _each(jnp.exp, gcc)
    gl = [c[q - 1: q, :] for c in gcc]
    rm = _each(lambda vv, k, b, e: jnp.concatenate([vv * b, k * (b * e)], axis=1), v, kn, bcol, eg)
    tt = _each(lambda t, r: _dot3(_split(t), _split(r)), tm, rm)
    w = [t[:, 128:] for t in tt]
    qg = _each(lambda a, e: a * e, qn, eg)
    ws = _each(lambda ww, a, s: _bdot(jnp.concatenate([ww, a], axis=0), s, NN), w, qg, states)
    vnew = _each(lambda t, x: t[:, :128] - x[:q], tt, ws)
    return dict(qr=qr, v=v, rq=rq, rk=rk, qn=qn, kn=kn, gcc=gcc, bcol=bcol, dm=dm, ak=ak, tm=tm, eg=eg, gl=gl,
                egl=_each(jnp.exp, gl), ed=_each(lambda g, c: jnp.exp(g - c), gl, gcc), tt=tt, w=w, vnew=vnew, qkm=qkm,
                qg=qg, qgs=[x[q:] for x in ws])


def _gdn_fwd(name, act, proj, gb, garow, gnw):
    t = act.shape[0]
    q = CHUNK
    nc = t // q

    def body(act_ref, gz_ref, sm_ref, gb_ref, ga_ref, nw_ref, y_ref, o_ref, st_ref, s_scr):
        @pl.when(pl.program_id(0) == 0)
        def _():
            s_scr[...] = jnp.zeros_like(s_scr)

        states = [s_scr[h * 128: (h + 1) * 128, :] for h in range(GDN_HEADS)]
        for sub in range(cps):
            rows = pl.ds(sub * q, q)
            states = chunk(act_ref.at[rows, :], gz_ref.at[rows, :], sm_ref.at[rows, :], gb_ref, ga_ref, nw_ref,
                           y_ref.at[rows, :], o_ref.at[rows, :], st_ref.at[sub], states)
        for h in range(GDN_HEADS):
            s_scr[h * 128: (h + 1) * 128, :] = states[h]

    def chunk(act_ref, gz_ref, sm_ref, gb_ref, ga_ref, nw_ref, y_ref, o_ref, st_ref, states):
        for h in range(GDN_HEADS):
            st_ref[h * 128: (h + 1) * 128, :] = states[h]
        actv = act_ref[...]
        cm = _gdn_common(sm_ref[...], gb_ref[...], ga_ref[...])
        nwv = nw_ref[...]
        gzv = gz_ref[...]
        hd = _gdn_heads(cm, actv, states)
        outs = _each(lambda qs, m, vn: qs + _bdot(m, vn, NN), hd["qgs"], hd["qkm"], hd["vnew"])
        snew = _each(lambda s, e, k, d, vn: s * e + _bdot(k * d, vn, TN), states, hd["egl"], hd["kn"], hd["ed"], hd["vnew"])
        for h in range(GDN_HEADS):
            o = outs[h]
            o_ref[:, h * 128: (h + 1) * 128] = o
            rr = lax.rsqrt(jnp.mean(o * o, axis=-1, keepdims=True) + EPS)
            gz = gzv[:, h * 128: (h + 1) * 128]
            y_ref[:, h * 128: (h + 1) * 128] = (o * rr * nwv * (gz * _sigmoid(gz))).astype(BF16)
        return snew

    cps = SCAN_CHUNKS_PER_STEP if nc % SCAN_CHUNKS_PER_STEP == 0 else 1
    qq = cps * q
    vec128 = pl.BlockSpec((1, 128), lambda c: (0, 0))
    return pl.pallas_call(
        body, grid=(nc // cps,),
        in_specs=[pl.BlockSpec((qq, GDN_QKV), lambda c: (c, 0)), pl.BlockSpec((qq, 1024), lambda c: (c, GZ_OFF // 1024)),
                  pl.BlockSpec((qq, 128), lambda c: (c, SM_OFF // 128)), vec128, vec128, vec128],
        out_specs=(pl.BlockSpec((qq, 1024), lambda c: (c, 0)), pl.BlockSpec((qq, 1024), lambda c: (c, 0)),
                   pl.BlockSpec((cps, 1024, 128), lambda c: (c, 0, 0))),
        out_shape=(SDS((t, 1024), BF16), SDS((t, 1024), F32), SDS((nc, 1024, 128), F32)),
        scratch_shapes=[pltpu.VMEM((1024, 128), F32)], name=name, compiler_params=_params(("arbitrary",)),
    )(act, proj, proj, gb, garow, gnw)


def _gdn_bwd(name, act, proj, gb, garow, gnw, oraw, states, dy, dproj):
    t = act.shape[0]
    q = CHUNK
    nc = t // q

    def body(act_ref, gz_ref, sm_ref, gb_ref, ga_ref, nw_ref, o_ref, st_ref, dy_ref, dproj_ref,
             dact_ref, dgz_ref, dsm_ref, dnw_ref, dal_ref, dgb_ref, ds_scr):
        @pl.when(pl.program_id(0) == 0)
        def _():
            ds_scr[...] = jnp.zeros_like(ds_scr)
            dnw_ref[...] = jnp.zeros_like(dnw_ref)
            dal_ref[...] = jnp.zeros_like(dal_ref)
            dgb_ref[...] = jnp.zeros_like(dgb_ref)

        dsn = [ds_scr[h * 128: (h + 1) * 128, :] for h in range(GDN_HEADS)]
        for sub in reversed(range(cps)):
            rows = pl.ds(sub * q, q)
            dsn = chunk(act_ref.at[rows, :], gz_ref.at[rows, :], sm_ref.at[rows, :], gb_ref, ga_ref, nw_ref,
                        o_ref.at[rows, :], st_ref.at[sub], dy_ref.at[rows, :],
                        dact_ref.at[rows, :], dgz_ref.at[rows, :], dsm_ref.at[rows, :], dnw_ref, dal_ref, dgb_ref, dsn)
        for h in range(GDN_HEADS):
            ds_scr[h * 128: (h + 1) * 128, :] = dsn[h]

    def chunk(act_ref, gz_ref, sm_ref, gb_ref, ga_ref, nw_ref, o_ref, st_ref, dy_ref,
              dact_ref, dgz_ref, dsm_ref, dnw_ref, dal_ref, dgb_ref, dsn):
        actv = act_ref[...]
        smv = sm_ref[...]
        garow_v = ga_ref[...]
        cm = _gdn_common(smv, gb_ref[...], garow_v)
        ii, jj, lane = cm["ii"], cm["jj"], cm["lane"]
        nwv = nw_ref[...]
        rowq = lax.broadcasted_iota(jnp.int32, (q, 1), 0)
        dgc_all = jnp.zeros((q, 128), F32)
        dbeta_all = jnp.zeros((q, 128), F32)
        dnw_acc = jnp.zeros((1, 128), F32)
        ds_out = []
        ov, gzv, dyv = o_ref[...], gz_ref[...], dy_ref[...]
        for heads in GDN_BWD_HEAD_GROUPS:
            part = group(heads, cm, actv, ov, gzv, dyv, nwv, rowq, st_ref, [dsn[h] for h in heads], dact_ref, dgz_ref)
            ds_out += part[0]
            dgc_all, dbeta_all, dnw_acc = dgc_all + part[1], dbeta_all + part[2], dnw_acc + part[3]
        dnw_ref[...] += dnw_acc
        dg = _hdot(cm["triu"], dgc_all)
        da_raw = jnp.where(cm["ma"], dg * garow_v * _sigmoid(cm["spre"]), 0.0)
        dal_ref[...] += _colsum(dg * cm["g"])
        dgb_ref[...] += _colsum(da_raw)
        beta = cm["beta"]
        dsm_ref[...] = (da_raw + dbeta_all * beta * (1.0 - beta)).astype(BF16)
        return ds_out

    def group(heads, cm, actv, ov, gzv, dyv, nwv, rowq, st_ref, dsn, dact_ref, dgz_ref):
        ii, jj, lane = cm["ii"], cm["jj"], cm["lane"]
        dgc_all = jnp.zeros((q, 128), F32)
        dbeta_all = jnp.zeros((q, 128), F32)
        dnw_acc = jnp.zeros((1, 128), F32)
        ds_out = []
        sts = [st_ref[h * 128: (h + 1) * 128, :] for h in heads]
        hd = _gdn_heads(cm, actv, sts, heads)
        qn, kn, v, eg, ed, egl, bcol = hd["qn"], hd["kn"], hd["v"], hd["eg"], hd["ed"], hd["egl"], hd["bcol"]
        vnew, qkm, qg, w, tt, dm, ak = hd["vnew"], hd["qkm"], hd["qg"], hd["w"], hd["tt"], hd["dm"], hd["ak"]
        do = []
        for h in heads:
            hs = slice(h * 128, (h + 1) * 128)
            o = ov[:, hs]
            rr = lax.rsqrt(jnp.mean(o * o, axis=-1, keepdims=True) + EPS)
            on = o * rr
            gz = gzv[:, hs]
            sz = _sigmoid(gz)
            silz = gz * sz
            dyh = dyv[:, hs]
            dnw_acc = dnw_acc + _colsum(dyh * on * silz)
            dgz_ref[:, hs] = (dyh * on * nwv * (sz * (1.0 + gz * (1.0 - sz)))).astype(BF16)
            don = dyh * nwv * silz
            do.append(rr * (don - on * jnp.mean(don * on, axis=-1, keepdims=True)))
        kd = _each(lambda k, e: k * e, kn, ed)
        dkd = _each(lambda vn, d: _bdot(vn, d, NT), vnew, dsn)
        dvnew_a = _each(lambda k, d: _bdot(k, d, NN), kd, dsn)
        ded = _each(lambda a, b: _rowsum(a * b), dkd, kd)
        dgl = _each(lambda d, s, e, de: jnp.sum(_rowsum(d * s), axis=0, keepdims=True) * e + _colsum(de), dsn, sts, egl, ded)
        dqk = _each(lambda d, vn: jnp.where(ii >= jj, _bdot(d, vn, NT), 0.0), do, vnew)
        dvnew = _each(lambda a, m, d: a + _bdot(m, d, TN), dvnew_a, qkm, do)
        pq = _each(lambda a, b: a * b, dqk, dm)
        w1 = _each(lambda a, b: a * b, dqk, qkm)
        dod = _each(lambda a, b: jnp.concatenate([a, b], axis=0), do, dvnew)
        dos = _each(lambda x, s: _bdot(x, s, NT), dod, sts)
        dqg = [x[:q] for x in dos]
        dw = [-x[q:] for x in dos]
        ds12 = _each(lambda a, ww, x: _bdot(jnp.concatenate([a, -ww], axis=0), x, TN), qg, w, dod)
        dr = _each(lambda t, a, b: _dot3(_split(t), _split(jnp.concatenate([a, b], axis=1)), TN), hd["tm"], dvnew, dw)
        da = _each(lambda r, t: jnp.where(ii > jj, -_dot3(_split(r), _split(t), NT), 0.0), dr, tt)
        sk = _each(lambda r, k: _rowsum(r[:, 128:] * k), dr, kn)
        pk = _each(lambda a, d, b: a * d * b, da, dm, bcol)
        pkn = _each(lambda p, pp, k: _bdot(jnp.concatenate([p, pp + pp.T], axis=0), k, NN), pq, pk, kn)
        dq = _each(lambda a, e, x: a * e + x[:q], dqg, eg, pkn)
        dk = _each(lambda a, e, p, x, r, b, eg_, y: a * e + _bdot(p, x, TN) + r[:, 128:] * (b * eg_) + y[q:],
                   dkd, ed, pq, qn, dr, bcol, eg, pkn)
        w2 = _each(lambda a, k, b: a * (k * b), da, ak, bcol)
        for i, h in enumerate(heads):
            hs = slice(h * 128, (h + 1) * 128)
            dgc = (-ded[i] + _rowsum(dqg[i] * qg[i]) + _rowsum(w1[i]) - _rowsum(w1[i].T) + sk[i] * bcol[i] * eg[i]
                   + _rowsum(w2[i]) - _rowsum(w2[i].T) + jnp.where(rowq == q - 1, dgl[i], 0.0))
            dbeta = _rowsum(dr[i][:, :128] * v[i]) + sk[i] * eg[i] + _rowsum(da[i] * ak[i])
            qhat = hd["qr"][i] * hd["rq"][i]
            dqhat = dq[i] * GDN_SCALE
            dact_ref[:, hs] = hd["rq"][i] * (dqhat - qhat * _rowsum(dqhat * qhat))
            dact_ref[:, 1024 + h * 128: 1024 + (h + 1) * 128] = hd["rk"][i] * (dk[i] - kn[i] * _rowsum(dk[i] * kn[i]))
            dact_ref[:, 2048 + h * 128: 2048 + (h + 1) * 128] = dr[i][:, :128] * bcol[i]
            dgc_all = dgc_all + jnp.where(lane == LANE_A + h, dgc, 0.0)
            dbeta_all = dbeta_all + jnp.where(lane == LANE_B + h, dbeta, 0.0)
            ds_out.append(dsn[i] * egl[i] + ds12[i])
        return ds_out, dgc_all, dbeta_all, dnw_acc

    cps = 2 if nc % 2 == 0 else 1
    qq = cps * q
    rev = lambda c: nc // cps - 1 - c
    vec128 = pl.BlockSpec((1, 128), lambda c: (0, 0))
    return pl.pallas_call(
        body, grid=(nc // cps,),
        in_specs=[pl.BlockSpec((qq, GDN_QKV), lambda c: (rev(c), 0)),
                  pl.BlockSpec((qq, 1024), lambda c: (rev(c), GZ_OFF // 1024)),
                  pl.BlockSpec((qq, 128), lambda c: (rev(c), SM_OFF // 128)), vec128, vec128, vec128,
                  pl.BlockSpec((qq, 1024), lambda c: (rev(c), 0)), pl.BlockSpec((cps, 1024, 128), lambda c: (rev(c), 0, 0)),
                  pl.BlockSpec((qq, 1024), lambda c: (rev(c), 0)), ANY],
        out_specs=(pl.BlockSpec((qq, GDN_QKV), lambda c: (rev(c), 0)),
                   pl.BlockSpec((qq, 1024), lambda c: (rev(c), GZ_OFF // 1024)),
                   pl.BlockSpec((qq, 128), lambda c: (rev(c), 0)), vec128, vec128, vec128),
        out_shape=(SDS((t, GDN_QKV), F32), SDS(dproj.shape, BF16), SDS((t, 128), BF16), SDS((1, 128), F32),
                   SDS((1, 128), F32), SDS((1, 128), F32)),
        input_output_aliases={9: 1},
        scratch_shapes=[pltpu.VMEM((1024, 128), F32)], name=name, compiler_params=_params(("arbitrary",)),
    )(act, proj, proj, gb, garow, gnw, oraw, states, dy, dproj)


def _row_tile(r):
    for cand in (512, 256, 128, 64, 32, 16, 8):
        if r % cand == 0:
            return cand
    return r


def _sum_terms(name, terms, out_dtype):
    shape = terms[0][0].shape[1:]
    c = shape[-1]
    r = 1
    for s in shape[:-1]:
        r *= s
    tr = min(_row_tile(r), 256)
    n = len(terms)

    def body(*refs):
        acc = refs[0][...].astype(F32)
        for k in range(1, n):
            acc = acc + refs[k][...].astype(F32)
        refs[n][...] = acc.astype(out_dtype)

    in_specs = [pl.BlockSpec((None, tr, c), lambda i, q=lead: (q, i, 0)) for _, lead in terms]
    args = [a.reshape(a.shape[0], r, c) for a, _ in terms]
    out = pl.pallas_call(body, grid=(r // tr,), in_specs=in_specs, out_specs=pl.BlockSpec((tr, c), lambda i: (i, 0)),
                         out_shape=SDS((r, c), out_dtype), name=name, compiler_params=_params(("parallel",)))(*args)
    return out.reshape(shape)


def _adamw_math(w, g, m, v):
    mn = ADAM_B1 * m + (1.0 - ADAM_B1) * g
    vn = ADAM_B2 * v + (1.0 - ADAM_B2) * (g * g)
    m_hat = mn / (1.0 - ADAM_B1 ** ADAM_STEP)
    v_hat = vn / (1.0 - ADAM_B2 ** ADAM_STEP)
    return -ADAM_LR * (m_hat / (jnp.sqrt(v_hat) + ADAM_EPS) + ADAM_WD * w), mn, vn


def _adamw_rows(name, w, g, m, v):
    r, a, c = w.shape
    tr = next(t for t in range(min(r, 128), 0, -1) if r % t == 0)

    def body(w_ref, g_ref, m_ref, v_ref, d_ref, nm_ref, nv_ref):
        d_ref[...], nm_ref[...], nv_ref[...] = _adamw_math(w_ref[...], g_ref[...], m_ref[...], v_ref[...])

    spec = pl.BlockSpec((tr, a, c), lambda i: (i, 0, 0))
    return pl.pallas_call(body, grid=(r // tr,), in_specs=[spec] * 4, out_specs=(spec,) * 3,
                          out_shape=(SDS(w.shape, F32),) * 3, name=name, compiler_params=_params(("parallel",)))(w, g, m, v)


def _adamw(name, w, g, m, v):
    shape = w.shape
    c = shape[-1]
    per_layer = isinstance(g, (list, tuple))
    nl = len(g) if per_layer else 1
    gs = [a.reshape(-1, c) for a in g] if per_layer else [g.reshape(-1, c)]
    r = gs[0].shape[0]
    w3, m3, v3 = (a.reshape(nl, r, c) for a in (w, m, v))
    tr = min(_row_tile(r), 256)

    def body(*refs):
        w_ref, m_ref, v_ref = refs[:3]
        g_refs = refs[3: 3 + nl]
        go_ref, d_ref, nm_ref, nv_ref = refs[3 + nl:]
        layer = pl.program_id(0)
        gv = g_refs[0][...]
        for k in range(1, nl):
            gv = jnp.where(layer == k, g_refs[k][...], gv)
        go_ref[...] = gv
        d_ref[...], nm_ref[...], nv_ref[...] = _adamw_math(w_ref[...], gv, m_ref[...], v_ref[...])

    spec3 = pl.BlockSpec((None, tr, c), lambda l, i: (l, i, 0))
    gspec = pl.BlockSpec((tr, c), lambda l, i: (i, 0))
    outs = pl.pallas_call(body, grid=(nl, r // tr), in_specs=[spec3] * 3 + [gspec] * nl, out_specs=(spec3,) * 4,
                          out_shape=(SDS((nl, r, c), F32),) * 4, name=name,
                          compiler_params=_params(("parallel", "parallel")))(w3, m3, v3, *gs)
    return tuple(o.reshape(shape) for o in outs)


ANY = pl.BlockSpec(memory_space=pl.ANY)
MESH = pl.DeviceIdType.MESH


def _allgather(name, xs):
    n = len(xs)

    def body(*refs):
        x_refs, out_refs = refs[:n], refs[n: 2 * n]
        send_sems, recv_sems, local_sems = refs[2 * n:]
        x, y, cc = lax.axis_index("x"), lax.axis_index("y"), lax.axis_index("c")
        me, sibling = (x, y, cc), (x, y, 1 - cc)
        chips = [(1 - x, y), (x, 1 - y), (1 - x, 1 - y)]

        def rows(a, px, py, pc):
            return out_refs[a].at[4 * px + 2 * py + pc]

        def copy(a, k, block, to, src=None):
            return pltpu.make_async_remote_copy(
                src_ref=rows(a, *block) if src is None else src, dst_ref=rows(a, *block),
                send_sem=send_sems.at[7 * a + k], recv_sem=recv_sems.at[7 * a + k], device_id=to, device_id_type=MESH)

        mine = [pltpu.make_async_copy(x_refs[a], rows(a, *me), local_sems.at[a]) for a in range(n)]
        for cp in mine:
            cp.start()
        first = []
        for a in range(n):
            first.append(copy(a, 0, me, sibling, src=x_refs[a]))
            first += [copy(a, 1 + j, me, (*chip, cc), src=x_refs[a]) for j, chip in enumerate(chips)]
        for cp in first:
            cp.start()
        passed = []
        for j, chip in enumerate(chips):
            for a in range(n):
                copy(a, 1 + j, (*chip, cc), me).wait_recv()
                fwd = copy(a, 4 + j, (*chip, cc), sibling)
                fwd.start()
                passed.append(fwd)
        for a in range(n):
            copy(a, 0, sibling, me).wait_recv()
        for j, chip in enumerate(chips):
            for a in range(n):
                copy(a, 4 + j, (*chip, 1 - cc), me).wait_recv()
        for cp in first + passed:
            cp.wait_send()
        for cp in mine:
            cp.wait()

    return pl.pallas_call(
        body, out_shape=tuple(SDS((N_DEV,) + a.shape, a.dtype) for a in xs), in_specs=[ANY] * n, out_specs=(ANY,) * n,
        scratch_shapes=[pltpu.SemaphoreType.DMA((7 * n,)), pltpu.SemaphoreType.DMA((7 * n,)),
                        pltpu.SemaphoreType.DMA((n,))],
        name=name,
    )(*xs)


def _allgather_seq(name, xs, collective_id):
    n = len(xs)
    x_refs = [jax.new_ref(a, memory_space=pltpu.MemorySpace.HBM) for a in xs]
    out_refs = [jax.empty_ref(SDS((N_DEV,) + a.shape, a.dtype), memory_space=pltpu.MemorySpace.HBM) for a in xs]

    @pl.kernel(mesh=plsc.ScalarSubcoreMesh(axis_name="seq", num_cores=1), name=name,
               scratch_types=(pltpu.SemaphoreType.DMA((7 * n,)), pltpu.SemaphoreType.DMA((7 * n,)),
                              pltpu.SemaphoreType.DMA((n,))),
               compiler_params=pltpu.CompilerParams(collective_id=collective_id))
    def launch(send_sems, recv_sems, local_sems):
        x, y, cc = lax.axis_index("x"), lax.axis_index("y"), lax.axis_index("c")
        me, sibling = (x, y, cc), (x, y, 1 - cc)
        chips = [(1 - x, y), (x, 1 - y), (1 - x, 1 - y)]
        barrier = pltpu.get_barrier_semaphore()
        for peer in [sibling] + [(*chip, cc) for chip in chips]:
            pl.semaphore_signal(barrier, inc=1, device_id=peer, device_id_type=MESH)
        pl.semaphore_wait(barrier, 4)

        def rows(a, px, py, pc):
            return out_refs[a].at[4 * px + 2 * py + pc]

        def copy(a, k, block, to, src=None):
            return pltpu.make_async_remote_copy(
                src_ref=rows(a, *block) if src is None else src, dst_ref=rows(a, *block),
                send_sem=send_sems.at[7 * a + k], recv_sem=recv_sems.at[7 * a + k], device_id=to, device_id_type=MESH)

        mine = [pltpu.make_async_copy(x_refs[a], rows(a, *me), local_sems.at[a]) for a in range(n)]
        for cp in mine:
            cp.start()
        first = []
        for a in range(n):
            first.append(copy(a, 0, me, sibling, src=x_refs[a]))
            first += [copy(a, 1 + j, me, (*chip, cc), src=x_refs[a]) for j, chip in enumerate(chips)]
        for cp in first:
            cp.start()
        passed = []
        for j, chip in enumerate(chips):
            for a in range(n):
                copy(a, 1 + j, (*chip, cc), me).wait_recv()
                fwd = copy(a, 4 + j, (*chip, cc), sibling)
                fwd.start()
                passed.append(fwd)
        for a in range(n):
            copy(a, 0, sibling, me).wait_recv()
        for j, chip in enumerate(chips):
            for a in range(n):
                copy(a, 4 + j, (*chip, 1 - cc), me).wait_recv()
        for cp in first + passed:
            cp.wait_send()
        for cp in mine:
            cp.wait()

    launch()
    return [r[...] for r in out_refs]


HBM = pl.BlockSpec(memory_space=pltpu.HBM)
SEM = pl.BlockSpec(memory_space=pltpu.SEMAPHORE)
EFFECT = pltpu.SideEffectType.DATAFLOW_SIDE_EFFECTING


def _sibling_plan(srcs, lands, send_sems, recv_sems):
    x, y, cc = lax.axis_index("x"), lax.axis_index("y"), lax.axis_index("c")
    return [pltpu.make_async_remote_copy(
        src_ref=srcs[a].at[2 * q + 1 - cc], dst_ref=lands[a].at[q], send_sem=send_sems.at[4 * a + q],
        recv_sem=recv_sems.at[4 * a + q], device_id=(x, y, 1 - cc), device_id_type=MESH)
        for a in range(len(srcs)) for q in range(4)]


def _chips_plan(srcs, lands, send_sems, recv_sems):
    x, y, cc = lax.axis_index("x"), lax.axis_index("y"), lax.axis_index("c")
    chips = [(1 - x, y), (x, 1 - y), (1 - x, 1 - y)]
    return [pltpu.make_async_remote_copy(
        src_ref=srcs[a].at[2 * px + py], dst_ref=lands[a].at[j], send_sem=send_sems.at[3 * a + j],
        recv_sem=recv_sems.at[3 * a + j], device_id=(px, py, cc), device_id_type=MESH)
        for a in range(len(srcs)) for j, (px, py) in enumerate(chips)]


def _copies_start(name, plan, per_array, srcs, land_lead):
    n = len(srcs)
    k = per_array * n

    def body(*refs):
        src_refs, land_refs = refs[:n], refs[n: 2 * n]
        send_sems, recv_sems = refs[2 * n], refs[2 * n + 1]
        token = refs[-1]
        for cp in plan(src_refs, land_refs, send_sems, recv_sems):
            cp.start()
        token[...] = jnp.zeros_like(token)

    lands = [lax.empty((land_lead,) + a.shape[1:], a.dtype) for a in srcs]
    outs = pl.pallas_call(
        body, name=name,
        out_shape=(pltpu.SemaphoreType.DMA((k,)), pltpu.SemaphoreType.DMA((k,)),
                   *[pltpu.HBM(a.shape, a.dtype) for a in srcs], *[pltpu.HBM(a.shape, a.dtype) for a in lands],
                   SDS((8, 128), F32)),
        in_specs=[HBM] * (2 * n), out_specs=(SEM, SEM, *[HBM] * (2 * n), pl.BlockSpec(memory_space=pltpu.VMEM)),
        input_output_aliases={i: 2 + i for i in range(2 * n)},
        compiler_params=pltpu.CompilerParams(has_side_effects=EFFECT),
    )(*[pltpu.with_memory_space_constraint(a, pltpu.HBM) for a in srcs],
      *[pltpu.with_memory_space_constraint(a, pltpu.HBM) for a in lands])
    return outs[0], outs[1], list(outs[2: 2 + n]), list(outs[2 + n: 2 + 2 * n]), outs[-1]


def _copies_wait(name, plan, started, after):
    send_sems, recv_sems, srcs, lands, _ = started
    n = len(srcs)
    after = tuple(after)

    def body(*refs):
        src_refs, land_refs = refs[:n], refs[n: 2 * n]
        for cp in plan(src_refs, land_refs, refs[2 * n], refs[2 * n + 1]):
            cp.wait_send()
            cp.wait_recv()

    outs = pl.pallas_call(
        body, name=name,
        out_shape=tuple(pltpu.HBM(a.shape, a.dtype) for a in srcs + lands),
        in_specs=[HBM] * (2 * n) + [SEM, SEM] + [ANY] * len(after), out_specs=(HBM,) * (2 * n),
        input_output_aliases={i: i for i in range(2 * n)},
        compiler_params=pltpu.CompilerParams(has_side_effects=EFFECT),
    )(*srcs, *lands, send_sems, recv_sems, *after)
    return list(outs[n:])


BIG = (("w_in", 1), ("w_ffn_in", 1), ("w_proj_ssm", 0), ("w_proj_gdn", 0), ("w_out", 0), ("w_ffn_down", 0))
CONVS = (("ssm_conv_w", 1), ("gdn_conv_w", 1))


def _to_dest_major(full, axis):
    if isinstance(full, tuple):
        per = N_DEV // len(full)
        s = full[0].shape[1] // per
        return jnp.stack([full[d // per][:, (d % per) * s: (d % per + 1) * s] for d in range(N_DEV)])
    a, b = full.shape
    if axis == 0:
        return full.reshape(N_DEV, a // N_DEV, b)
    s = b // N_DEV
    return jnp.stack([full[:, d * s: (d + 1) * s] for d in range(N_DEV)])


def _from_gathered(g, axis):
    if axis == 0:
        return g.reshape(-1, g.shape[2])
    return jnp.concatenate([g[d] for d in range(N_DEV)], axis=1)


IN_RUNS = ((Z_OFF, O_Z, 1024), (GZ_OFF, O_GZ, 1024), (G1_OFF, O_G1, 1024), (G2_OFF, O_G2, 1024), (QKV_OFF, O_QKV, 3072),
           (XBC_OFF, O_XBC, 1536), (SM_OFF, O_DT, 16), (SM_OFF + LANE_A, O_A, 8), (SM_OFF + LANE_B, O_B, 8))
IN_SHARD = IN_DIM // N_DEV


def _w_in_from_blocks(g):
    rows = g.shape[1]
    parts, pos = [], 0
    for off, o0, width in IN_RUNS:
        if off > pos:
            parts.append(jnp.zeros((rows, off - pos), g.dtype))
        c = o0
        while c < o0 + width:
            d = c // IN_SHARD
            hi = min(o0 + width, (d + 1) * IN_SHARD)
            parts.append(g[d][:, c - d * IN_SHARD: hi - d * IN_SHARD])
            c = hi
        pos = off + width
    parts.append(jnp.zeros((rows, PROJ_W - pos), g.dtype))
    return jnp.concatenate(parts, axis=1)


def _w_in_to_blocks(wp):
    by_orig = sorted(IN_RUNS, key=lambda r: r[1])
    blocks = []
    for d in range(N_DEV):
        lo, hi = d * IN_SHARD, (d + 1) * IN_SHARD
        parts = []
        for off, o0, width in by_orig:
            a, b = max(lo, o0), min(hi, o0 + width)
            if a < b:
                parts.append(wp[:, off + a - o0: off + b - o0])
        blocks.append(jnp.concatenate(parts, axis=1))
    return jnp.stack(blocks)


def _pad128(v, lane0):
    return jnp.zeros((1, 128), F32).at[0, lane0: lane0 + v.shape[0]].set(v)


def _layer_consts(p):
    return dict(
        dtb=_pad128(p["ssm_dt_bias"], 0), arow=_pad128(-jnp.exp(p["ssm_a_log"]), 0),
        dxrow=jnp.repeat(p["ssm_d"], SSM_P).reshape(1, 1024), snw=p["ssm_norm_w"].reshape(1, 1024),
        gb=_pad128(p["gdn_dt_bias"], LANE_A), garow=_pad128(-jnp.exp(p["gdn_a_log"]), LANE_A),
        gnw=p["gdn_norm_w"].reshape(1, 128), zb=jnp.zeros((1, GDN_QKV), F32), scb=p["ssm_conv_b"].reshape(1, SSM_CONV))


def _expand_matrix():
    row = lax.broadcasted_iota(jnp.int32, (128, 1024), 0)
    col = lax.broadcasted_iota(jnp.int32, (128, 1024), 1)
    return (col // SSM_P == row).astype(BF16)


def _silu_mul_epi(gate, up):
    return gate, up, gate * _sigmoid(gate) * up


def _merge_epi(acc, p1, g1, g2):
    return acc, _sigmoid(g1) * p1.astype(F32) + _sigmoid(g2) * acc


def _add_epi(acc, res):
    return (acc + res,)


def _ffn_bwd_epi(acc, gate, up):
    g = gate.astype(F32)
    sg = _sigmoid(g)
    return acc * up.astype(F32) * (sg * (1.0 + g * (1.0 - sg))), acc * (g * sg)


def _merge_bwd_epi(acc, g1, g2, p1, p2):
    s1, s2 = _sigmoid(g1), _sigmoid(g2)
    dg1, dg2 = acc * p1.astype(F32) * (s1 * (1.0 - s1)), acc * p2.astype(F32) * (s2 * (1.0 - s2))
    return acc * s1, acc * s2, jnp.concatenate([dg1, dg2], axis=1)


def _layer_fwd(l, x, p, rmat):
    t = x.shape[0]
    n = f"l{l}_"
    k = _layer_consts(p)
    h = _rmsnorm_fwd(n + "norm_mix", x, p["norm_mix_w"])
    proj = _matmul(n + "in_proj", "nn", [(h, 0, p["w_in"], 0)], t, PROJ_W, 1024, 1024, 1280, 1024, (F32,))
    act_g, pre_g = _conv_fwd(n + "conv_gdn", proj, QKV_OFF, p["gdn_conv_w"], k["zb"])
    act_s, pre_s = _conv_fwd(n + "conv_ssm", proj, XBC_OFF, p["ssm_conv_w"], k["scb"])
    y_ssm, ysc, st_s = _ssd_fwd(n + "ssd_fwd", act_s, proj, k["dtb"], k["arow"], k["dxrow"], k["snw"], rmat)
    y_gdn, oraw, st_g = _gdn_fwd(n + "gdn_fwd", act_g, proj, k["gb"], k["garow"], k["gnw"])
    if "late" in p:
        y_gdn, late = p["late"](y_gdn)
        p = {**p, **late}
    p1 = _matmul(n + "proj_ssm", "nn", [(y_ssm, 0, p["w_proj_ssm"], 0)], t, 1024, 1024, 1024, 1024, 1024, (BF16,))
    p2, merged = _matmul(n + "proj_gdn_merge", "nn", [(y_gdn, 0, p["w_proj_gdn"], 0)], t, 1024, 1024, 512, 1024, 1024,
                         (BF16, BF16), epi=_merge_epi, extras=[(p1, 0), (proj, G1_OFF // 1024), (proj, G2_OFF // 1024)])
    x1 = _matmul(n + "out_proj", "nn", [(merged, 0, p["w_out"], 0)], t, 1024, 1024, 1024, 1024, 1024, (F32,),
                 epi=_add_epi, extras=[(x, 0)])
    h2 = _rmsnorm_fwd(n + "norm_ffn", x1, p["norm_ffn_w"])
    gate, up, act = _matmul(n + "ffn_in", "nn", [(h2, 0, p["w_ffn_in"], 0), (h2, 0, p["w_ffn_in"], 2)], t, FFN, 1024,
                            1024, FFN // 2, 1024, (BF16, BF16, BF16), epi=_silu_mul_epi, second=1)
    x2 = _matmul(n + "ffn_down", "nn", [(act, 0, p["w_ffn_down"], 0)], t, 1024, FFN, 1024, 1024, FFN, (F32,),
                 epi=_add_epi, extras=[(x1, 0)])
    saved = dict(x=x, h=h, proj=proj, act_g=act_g, act_s=act_s, pre_g=pre_g, pre_s=pre_s, y_ssm=y_ssm, ysc=ysc, st_s=st_s, y_gdn=y_gdn, oraw=oraw,
                 st_g=st_g, p1=p1, p2=p2, merged=merged, x1=x1, h2=h2, up=up, gate=gate, act=act, k=k, p=p)
    return x2, saved


def _layer_bwd(l, dx2, dx2b, s, p, rmat, hooks):
    t = dx2.shape[0]
    n = f"l{l}_"
    k = s["k"]
    tk_tok = 1024
    hf = FFN // 2
    g = {}
    dgate, dup = _matmul(n + "d_ffn_act", "nt", [(dx2b, 0, p["w_ffn_down"], 0)], t, FFN, 1024, 1024, hf, 1024, (BF16, BF16),
                         epi=_ffn_bwd_epi, extras=[(s["gate"], 0), (s["up"], 0)])
    g["w_ffn_down"] = _matmul(n + "dw_ffn_down", "tn", [(s["act"], 0, dx2b, 0)], FFN, 1024, t, hf, 1024, tk_tok, (BF16,))
    dh2 = _matmul(n + "d_ffn_in", "nt", [(dgate, 0, p["w_ffn_in"], 0), (dup, 0, p["w_ffn_in"], 2)], t, 1024, FFN,
                  1024, 1024, hf, (F32,))
    dwg = _matmul(n + "dw_ffn_gate", "tn", [(s["h2"], 0, dgate, 0)], 1024, FFN, t, 1024, hf, tk_tok, (BF16,))
    dwu = _matmul(n + "dw_ffn_up", "tn", [(s["h2"], 0, dup, 0)], 1024, FFN, t, 1024, hf, tk_tok, (BF16,))
    g["w_ffn_in"] = (dwg, dwu)
    dx1, dx1b, g["norm_ffn_w"] = _rmsnorm_bwd(n + "d_norm_ffn", s["x1"], p["norm_ffn_w"], dh2, dx2)
    dx1b = hooks.ffn_done(dx1b)
    dp1, dp2, dproj = _matmul(
        n + "d_out_proj", "nt", [(dx1b, 0, p["w_out"], 0)], t, 1024, 1024, 512, 1024, 1024,
        (BF16, BF16, (BF16, PROJ_W, 2048, G1_OFF // 2048)), epi=_merge_bwd_epi,
        extras=[(s["proj"], G1_OFF // 1024), (s["proj"], G2_OFF // 1024), (s["p1"], 0), (s["p2"], 0)])
    g["w_out"] = _matmul(n + "dw_out", "tn", [(s["merged"], 0, dx1b, 0)], 1024, 1024, t, 1024, 1024, tk_tok, (BF16,))
    g["w_proj_ssm"] = _matmul(n + "dw_proj_ssm", "tn", [(s["y_ssm"], 0, dp1, 0)], 1024, 1024, t, 1024, 1024, tk_tok, (BF16,))
    g["w_proj_gdn"] = _matmul(n + "dw_proj_gdn", "tn", [(s["y_gdn"], 0, dp2, 0)], 1024, 1024, t, 1024, 1024, tk_tok, (BF16,))
    dp1, dp2 = hooks.early_ready(l, g, dp1, dp2)
    dy_ssm = _matmul(n + "d_proj_ssm", "nt", [(dp1, 0, p["w_proj_ssm"], 0)], t, 1024, 1024, 1024, 1024, 1024, (F32,))
    dy_gdn = _matmul(n + "d_proj_gdn", "nt", [(dp2, 0, p["w_proj_gdn"], 0)], t, 1024, 1024, 1024, 1024, 1024, (F32,))
    dact_s, dproj, dsm_s, dsnw, dd, dal, ddtb = _ssd_bwd(n + "ssd_bwd", s["act_s"], s["proj"], k["dtb"], k["arow"],
                                                           k["dxrow"], k["snw"], rmat, s["ysc"], s["st_s"], dy_ssm, dproj)
    dact_g, dproj, dsm_g, dgnw, dgal, dgb = _gdn_bwd(n + "gdn_bwd", s["act_g"], s["proj"], k["gb"], k["garow"], k["gnw"],
                                                       s["oraw"], s["st_g"], dy_gdn, dproj)
    dsm_s = hooks.mixers_done(dsm_s)
    dproj = _place_small(n + "d_small", dsm_s, dsm_g, dproj)
    dproj, g["ssm_conv_w"], dcb = _conv_bwd(n + "d_conv_ssm", s["proj"], XBC_OFF, p["ssm_conv_w"], s["pre_s"], dact_s, dproj)
    dproj, g["gdn_conv_w"], _ = _conv_bwd(n + "d_conv_gdn", s["proj"], QKV_OFF, p["gdn_conv_w"], s["pre_g"], dact_g, dproj)
    g["ssm_conv_b"] = dcb.reshape(-1)
    g["ssm_norm_w"] = dsnw.reshape(-1)
    g["ssm_d"] = dd[0, :SSM_HEADS]
    g["ssm_a_log"] = dal[0, :SSM_HEADS]
    g["ssm_dt_bias"] = ddtb[0, :SSM_HEADS]
    g["gdn_norm_w"] = dgnw.reshape(-1)
    g["gdn_a_log"] = dgal[0, LANE_A: LANE_A + GDN_HEADS]
    g["gdn_dt_bias"] = dgb[0, LANE_A: LANE_A + GDN_HEADS]
    dh = _matmul(n + "d_in_proj", "nt", [(dproj, 0, p["w_in"], 0)], t, 1024, PROJ_W, 1024, 1024, 1280, (F32,))
    g["w_in"] = _matmul(n + "dw_in", "tn", [(s["h"], 0, dproj, 0)], 1024, PROJ_W, t, 1024, 1280, tk_tok, (BF16,))
    dx, dxb, g["norm_mix_w"] = _rmsnorm_bwd(n + "d_norm_mix", s["x"], p["norm_mix_w"], dh, dx1)
    g["norm_mix_w"] = g["norm_mix_w"].reshape(-1)
    g["norm_ffn_w"] = g["norm_ffn_w"].reshape(-1)
    return dx, dxb, g


def _local_step(x, tgt, layers, final_norm_w, reduce=False):
    rmat = _expand_matrix()
    saved, params = [], []
    for l in range(DEPTH):
        x, p = layers[l](x)
        x, s = _layer_fwd(l, x, p, rmat)
        saved.append(s)
        params.append(s["p"])
    loss, dx, dxb, dfw = _loss_head("loss_head", x, final_norm_w, tgt)
    grads = [None] * DEPTH
    hooks = _ReduceBesideBackward() if reduce else _NoReduce()
    for l in reversed(range(DEPTH)):
        dx, dxb, grads[l] = _layer_bwd(l, dx, dxb, saved[l], params[l], rmat, hooks)
        if reduce:
            dxb = hooks.layer_done(l, grads[l], dxb)
    if reduce:
        hooks.finish_start(dxb)
    return loss[0, 0], dx, grads, dfw.reshape(-1), hooks if reduce else None


SMALL = ("norm_mix_w", "ssm_conv_b", "ssm_dt_bias", "ssm_a_log", "ssm_d", "ssm_norm_w", "gdn_a_log", "gdn_dt_bias",
         "gdn_norm_w", "norm_ffn_w")
WEIGHTS = ("norm_mix_w", "w_in", "ssm_conv_w", "ssm_conv_b", "ssm_dt_bias", "ssm_a_log", "ssm_d", "ssm_norm_w", "gdn_conv_w",
           "gdn_a_log", "gdn_dt_bias", "gdn_norm_w", "w_proj_ssm", "w_proj_gdn", "w_out", "norm_ffn_w", "w_ffn_in",
           "w_ffn_down", "final_norm_w")


FIRST_USED = ("w_in", "ssm_conv_w", "gdn_conv_w")


def _gather_layer(l, w):
    conv_names = [nm for nm, _ in CONVS]
    groups = ([s for s in BIG + CONVS if s[0] in FIRST_USED], [s for s in BIG + CONVS if s[0] not in FIRST_USED])
    gathered = []
    for i, (specs, tag) in enumerate(zip(groups, ("first", "rest"))):
        shards = [w[nm][l] if nm in conv_names else w[nm][l].astype(BF16) for nm, _ in specs]
        gathered.append(_allgather_seq(f"l{l}_gather_{tag}", shards, collective_id=2 * l + i))
    small = {nm: w[nm][l] for nm in SMALL}

    def use(i, act):
        act, blocks = lax.optimization_barrier((act, gathered[i]))
        return act, {nm: _w_in_from_blocks(g) if nm == "w_in" else _from_gathered(g, axis)
                     for (nm, axis), g in zip(groups[i], blocks)}

    def full_weights(x):
        x, out = use(0, x)
        out.update(small)
        out["late"] = lambda y: use(1, y)
        return x, out

    return full_weights


EARLY_GRADS = ("w_ffn_down", "w_ffn_in", "w_out", "w_proj_ssm", "w_proj_gdn")


class _GradReduceScatter:
    def __init__(self, tag, specs, grads):
        self.tag = tag
        self.specs = specs
        self.blocks = [_w_in_to_blocks(grads[nm]) if nm == "w_in" else _to_dest_major(grads[nm], axis)
                       for nm, axis in specs]

    def _tied(self, started, acts):
        *acts, self.token = lax.optimization_barrier((*acts, started[4]))
        return acts

    def start(self, *acts):
        cc = lax.axis_index("c")
        self.keep = [lax.dynamic_index_in_dim(b.reshape((4, 2) + b.shape[1:]), cc, axis=1, keepdims=False)
                     for b in self.blocks]
        self.to_sibling = _copies_start(f"{self.tag}_to_sibling_start", _sibling_plan, 4, self.blocks, 4)
        return self._tied(self.to_sibling, acts)

    def mid(self, *acts):
        got = _copies_wait(f"{self.tag}_to_sibling_wait", _sibling_plan, self.to_sibling, (acts[0], self.token))
        chip_sums = [_sum_terms(f"{self.tag}_chip_sum_{nm}", [(k[None], 0), (g[None], 0)], BF16)
                     for (nm, _), k, g in zip(self.specs, self.keep, got)]
        self.to_chips = _copies_start(f"{self.tag}_between_chips_start", _chips_plan, 3, chip_sums, 3)
        return self._tied(self.to_chips, acts)

    def end(self, after):
        after = tuple(after) if isinstance(after, (tuple, list)) else (after,)
        landed = _copies_wait(f"{self.tag}_between_chips_wait", _chips_plan, self.to_chips, (*after, self.token))
        my_chip = 2 * lax.axis_index("x") + lax.axis_index("y")
        own = [lax.dynamic_index_in_dim(s, my_chip, axis=0, keepdims=True) for s in self.to_chips[2]]
        return {nm: _sum_terms(f"{self.tag}_total_{nm}", [(o, 0), (e, 0), (e, 1), (e, 2)], F32)
                for (nm, _), o, e in zip(self.specs, own, landed)}


class _NoReduce:
    def ffn_done(self, dx1b):
        return dx1b

    def early_ready(self, l, g, dp1, dp2):
        return dp1, dp2

    def mixers_done(self, dsm):
        return dsm


class _ReduceBesideBackward(_NoReduce):
    def __init__(self):
        self.late = None
        self.early = None
        self.shards = [dict() for _ in range(DEPTH)]

    def ffn_done(self, dx1b):
        if self.late is not None:
            (dx1b,) = self.late.mid(dx1b)
        return dx1b

    def early_ready(self, l, g, dp1, dp2):
        self.early = _GradReduceScatter(f"l{l}_early_grads", [s for s in BIG if s[0] in EARLY_GRADS], g)
        return self.early.start(dp1, dp2)

    def mixers_done(self, dsm):
        (dsm,) = self.early.mid(dsm)
        return dsm

    def layer_done(self, l, g, dxb):
        if self.late is not None:
            self.shards[l + 1].update(self.late.end(dxb))
        self.shards[l].update(self.early.end(dxb))
        self.late = _GradReduceScatter(f"l{l}_late_grads", [s for s in BIG + CONVS if s[0] not in EARLY_GRADS], g)
        (dxb,) = self.late.start(dxb)
        return dxb

    def finish_start(self, dxb):
        self.late.mid(dxb)

    def finish_end(self, after):
        return self.late.end(after)


def _allreduce_small(vecs):
    flat = jnp.concatenate(vecs)
    n = flat.shape[0]
    rows = -(-n // 128)
    rows = -(-rows // 8) * 8
    buf = jnp.pad(flat, (0, rows * 128 - n)).reshape(rows, 128)
    (allv,) = _allgather("gather_small_grads", [buf])
    tot = _sum_terms("small_grads_total", [(allv, d) for d in range(N_DEV)], F32).reshape(-1)
    out, o = [], 0
    for v in vecs:
        out.append(tot[o: o + v.shape[0]])
        o += v.shape[0]
    return out


def kernel(x, norm_mix_w, w_in, ssm_conv_w, ssm_conv_b, ssm_dt_bias, ssm_a_log, ssm_d, ssm_norm_w, gdn_conv_w, gdn_a_log, gdn_dt_bias, gdn_norm_w, w_proj_ssm, w_proj_gdn, w_out, norm_ffn_w, w_ffn_in, w_ffn_down, final_norm_w, loss_target, m_norm_mix_w, m_w_in, m_ssm_conv_w, m_ssm_conv_b, m_ssm_dt_bias, m_ssm_a_log, m_ssm_d, m_ssm_norm_w, m_gdn_conv_w, m_gdn_a_log, m_gdn_dt_bias, m_gdn_norm_w, m_w_proj_ssm, m_w_proj_gdn, m_w_out, m_norm_ffn_w, m_w_ffn_in, m_w_ffn_down, m_final_norm_w, v_norm_mix_w, v_w_in, v_ssm_conv_w, v_ssm_conv_b, v_ssm_dt_bias, v_ssm_a_log, v_ssm_d, v_ssm_norm_w, v_gdn_conv_w, v_gdn_a_log, v_gdn_dt_bias, v_gdn_norm_w, v_w_proj_ssm, v_w_proj_gdn, v_w_out, v_norm_ffn_w, v_w_ffn_in, v_w_ffn_down, v_final_norm_w):
    w = dict(norm_mix_w=norm_mix_w, w_in=w_in, ssm_conv_w=ssm_conv_w, ssm_conv_b=ssm_conv_b, ssm_dt_bias=ssm_dt_bias,
             ssm_a_log=ssm_a_log, ssm_d=ssm_d, ssm_norm_w=ssm_norm_w, gdn_conv_w=gdn_conv_w, gdn_a_log=gdn_a_log,
             gdn_dt_bias=gdn_dt_bias, gdn_norm_w=gdn_norm_w, w_proj_ssm=w_proj_ssm, w_proj_gdn=w_proj_gdn, w_out=w_out,
             norm_ffn_w=norm_ffn_w, w_ffn_in=w_ffn_in, w_ffn_down=w_ffn_down, final_norm_w=final_norm_w)
    m = dict(norm_mix_w=m_norm_mix_w, w_in=m_w_in, ssm_conv_w=m_ssm_conv_w, ssm_conv_b=m_ssm_conv_b, ssm_dt_bias=m_ssm_dt_bias,
             ssm_a_log=m_ssm_a_log, ssm_d=m_ssm_d, ssm_norm_w=m_ssm_norm_w, gdn_conv_w=m_gdn_conv_w, gdn_a_log=m_gdn_a_log,
             gdn_dt_bias=m_gdn_dt_bias, gdn_norm_w=m_gdn_norm_w, w_proj_ssm=m_w_proj_ssm, w_proj_gdn=m_w_proj_gdn,
             w_out=m_w_out, norm_ffn_w=m_norm_ffn_w, w_ffn_in=m_w_ffn_in, w_ffn_down=m_w_ffn_down,
             final_norm_w=m_final_norm_w)
    v = dict(norm_mix_w=v_norm_mix_w, w_in=v_w_in, ssm_conv_w=v_ssm_conv_w, ssm_conv_b=v_ssm_conv_b, ssm_dt_bias=v_ssm_dt_bias,
             ssm_a_log=v_ssm_a_log, ssm_d=v_ssm_d, ssm_norm_w=v_ssm_norm_w, gdn_conv_w=v_gdn_conv_w, gdn_a_log=v_gdn_a_log,
             gdn_dt_bias=v_gdn_dt_bias, gdn_norm_w=v_gdn_norm_w, w_proj_ssm=v_w_proj_ssm, w_proj_gdn=v_w_proj_gdn,
             w_out=v_w_out, norm_ffn_w=v_norm_ffn_w, w_ffn_in=v_w_ffn_in, w_ffn_down=v_w_ffn_down,
             final_norm_w=v_final_norm_w)

    layers = [_gather_layer(l, w) for l in range(DEPTH)]
    loss_part, dx, lgrads, dfw, reducer = _local_step(x[0], loss_target[0], layers, final_norm_w, reduce=True)
    loss = lax.psum(loss_part, ("x", "y", "c"))
    shard_grads = reducer.shards
    late = [nm for nm, _ in BIG + CONVS if nm not in EARLY_GRADS]
    grad = {nm: [shard_grads[l][nm] for l in range(DEPTH)] for nm in EARLY_GRADS}
    small_vecs = [lgrads[l][nm].reshape(-1) for l in range(DEPTH) for nm in SMALL] + [dfw]
    small_sum = _allreduce_small(small_vecs)
    for i, nm in enumerate(SMALL):
        grad[nm] = jnp.stack([small_sum[l * len(SMALL) + i].reshape(w[nm].shape[1:]) for l in range(DEPTH)])
    grad["final_norm_w"] = small_sum[-1]

    deltas, new_m, new_v = {}, {}, {}
    for nm in [n for n in WEIGHTS if n not in late]:
        if nm == "w_ffn_in":
            tr = lambda a: jnp.transpose(a, (0, 2, 1))
            outs = _adamw("adamw_" + nm, tr(w[nm]), [g.T for g in grad[nm]], tr(m[nm]), tr(v[nm]))
            grad[nm], deltas[nm], new_m[nm], new_v[nm] = (tr(o) for o in outs)
        else:
            grad[nm], deltas[nm], new_m[nm], new_v[nm] = _adamw("adamw_" + nm, w[nm], grad[nm], m[nm], v[nm])
    shard_grads[0].update(reducer.finish_end([deltas[nm] for nm in EARLY_GRADS]))
    for nm in late:
        gl = [shard_grads[l][nm] for l in range(DEPTH)]
        if nm == "w_in":
            fwd, back = (lambda a: jnp.transpose(a, (2, 0, 1))), (lambda a: jnp.transpose(a, (1, 2, 0)))
            g3 = jnp.stack([g.T for g in gl], axis=1)
            outs = _adamw_rows("adamw_" + nm, fwd(w[nm]), g3, fwd(m[nm]), fwd(v[nm]))
            grad[nm], deltas[nm], new_m[nm], new_v[nm] = (back(o) for o in (g3,) + tuple(outs))
        else:
            grad[nm], deltas[nm], new_m[nm], new_v[nm] = _adamw("adamw_" + nm, w[nm], gl, m[nm], v[nm])
    return (loss, dx[None], *[grad[nm] for nm in WEIGHTS], *[deltas[nm] for nm in WEIGHTS],
            *[new_m[nm] for nm in WEIGHTS], *[new_v[nm] for nm in WEIGHTS])
```

```python
import functools

import jax
import jax.numpy as jnp
from jax import lax
from jax.experimental import pallas as pl
from jax.experimental.pallas import tpu as pltpu
from jax.experimental.pallas import tpu_sc as plsc

F32 = jnp.float32
BF16 = jnp.bfloat16
HI = lax.Precision.HIGHEST
SDS = jax.ShapeDtypeStruct

D_MODEL = 1024
DEPTH = 2
SSM_HEADS = 16
SSM_P = 64
SSM_N = 128
SSM_GROUPS = 2
SSM_CONV = 1536
GDN_HEADS = 8
GDN_DK = 128
GDN_QKV = 3072
CONV_K = 4
CHUNK = 64
SCAN_CHUNKS_PER_STEP = 8
FFN = 2816
IN_DIM = 8736
EPS = 1e-6
N_DEV = 8

Z_OFF = 0
GZ_OFF = 1024
G1_OFF = 2048
G2_OFF = 3072
QKV_OFF = 4096
XBC_OFF = 7168
SM_OFF = 8704
PROJ_W = 8960
LANE_A = 16
LANE_B = 24
O_Z, O_XBC, O_DT, O_QKV, O_GZ, O_A, O_B, O_G1, O_G2 = 0, 1024, 2560, 2576, 5648, 6672, 6680, 6688, 7712

ADAM_LR = 0.001
ADAM_B1 = 0.9
ADAM_B2 = 0.999
ADAM_EPS = 1e-08
ADAM_WD = 0.01
ADAM_STEP = 10

V7X_VMEM_LIMIT = 48 * 1024 * 1024

NN = ((1,), (0,))
NT = ((1,), (1,))
TN = ((0,), (0,))


def _bdot(a, b, dims):
    return lax.dot_general(a.astype(BF16), b.astype(BF16), (dims, ((), ())), preferred_element_type=F32)


def _hdot(a, b, dims=NN):
    return lax.dot_general(a, b, (dims, ((), ())), precision=HI, preferred_element_type=F32)


def _sigmoid(x):
    return 1.0 / (1.0 + jnp.exp(-x))


def _softplus(x):
    return jnp.maximum(x, 0.0) + jnp.log(1.0 + jnp.exp(-jnp.abs(x)))


def _params(dims):
    return pltpu.CompilerParams(dimension_semantics=dims, vmem_limit_bytes=V7X_VMEM_LIMIT)


def _rowsum(x):
    return jnp.sum(x, axis=-1, keepdims=True)


def _colsum(x):
    return jnp.sum(x, axis=0, keepdims=True)


def _matmul(name, mode, pairs, m, n, kdim, tm, tn, tk, out_dtypes, epi=None, extras=(), second=None):
    tm, tn, tk = min(tm, m), min(tn, n), min(tk, kdim)
    nk = kdim // tk
    assert m % tm == 0 and n % tn == 0 and kdim % tk == 0, (name, m, n, kdim, tm, tn, tk)
    assert second is None or nk == 1
    in_specs, args = [], []
    for a, a_off, b, b_off in pairs:
        if mode == "nn":
            in_specs.append(pl.BlockSpec((tm, tk), lambda i, j, k, o=a_off: (i, k + o)))
            in_specs.append(pl.BlockSpec((tk, tn), lambda i, j, k, o=b_off: (k, j + o)))
            dims = NN
        elif mode == "nt":
            in_specs.append(pl.BlockSpec((tm, tk), lambda i, j, k, o=a_off: (i, k + o)))
            in_specs.append(pl.BlockSpec((tn, tk), lambda i, j, k, o=b_off: (j, k + o)))
            dims = NT
        else:
            in_specs.append(pl.BlockSpec((tk, tm), lambda i, j, k, o=a_off: (k, i + o)))
            in_specs.append(pl.BlockSpec((tk, tn), lambda i, j, k, o=b_off: (k, j + o)))
            dims = TN
        args += [a, b]
    for e, e_off in extras:
        in_specs.append(pl.BlockSpec((tm, tn), lambda i, j, k, o=e_off: (i, j + o)))
        args.append(e)
    npair, nex, nout = len(pairs), len(extras), len(out_dtypes)

    def body(*refs):
        prefs = refs[: 2 * npair]
        erefs = refs[2 * npair: 2 * npair + nex]
        orefs = refs[2 * npair + nex: 2 * npair + nex + nout]

        def finish(*res):
            outs = res if epi is None else epi(*res, *[e[...] for e in erefs])
            for o, r in zip(orefs, outs):
                o[...] = r.astype(o.dtype)

        def total(ps):
            s = _bdot(prefs[2 * ps[0]][...], prefs[2 * ps[0] + 1][...], dims)
            for p in ps[1:]:
                s = s + _bdot(prefs[2 * p][...], prefs[2 * p + 1][...], dims)
            return s

        if second is not None:
            finish(total(list(range(second))), total(list(range(second, npair))))
            return
        s = total(list(range(npair)))
        if nk == 1:
            finish(s)
            return
        acc = refs[-1]
        k = pl.program_id(2)

        @pl.when(k == 0)
        def _():
            acc[...] = s

        @pl.when(k > 0)
        def _():
            acc[...] += s

        @pl.when(k == nk - 1)
        def _():
            finish(acc[...])

    out_shape, out_specs = [], []
    for od in out_dtypes:
        if isinstance(od, tuple):
            dt, full_w, blk_w, cblk = od
            assert n == tn
            out_shape.append(SDS((m, full_w), dt))
            out_specs.append(pl.BlockSpec((tm, blk_w), lambda i, j, k, c=cblk: (i, c)))
        else:
            out_shape.append(SDS((m, n), od))
            out_specs.append(pl.BlockSpec((tm, tn), lambda i, j, k: (i, j)))
    out_shape, out_specs = tuple(out_shape), tuple(out_specs)
    res = pl.pallas_call(
        body, grid=(m // tm, n // tn, nk), in_specs=in_specs, out_specs=out_specs, out_shape=out_shape,
        scratch_shapes=[pltpu.VMEM((tm, tn), F32)] if nk > 1 else [], name=name,
        compiler_params=_params(("parallel", "parallel", "arbitrary")),
    )(*args)
    return res if nout > 1 else res[0]


def _rmsnorm_fwd(name, x, w):
    t, d = x.shape
    tm = min(512, t)

    def body(x_ref, w_ref, h_ref):
        xv = x_ref[...]
        r = lax.rsqrt(jnp.mean(xv * xv, axis=-1, keepdims=True) + EPS)
        h_ref[...] = (xv * r * w_ref[...]).astype(BF16)

    return pl.pallas_call(
        body, grid=(t // tm,),
        in_specs=[pl.BlockSpec((tm, d), lambda i: (i, 0)), pl.BlockSpec((1, d), lambda i: (0, 0))],
        out_specs=pl.BlockSpec((tm, d), lambda i: (i, 0)), out_shape=SDS((t, d), BF16), name=name,
        compiler_params=_params(("parallel",)),
    )(x, w.reshape(1, d))


def _rmsnorm_bwd(name, x, w, dh, dres):
    t, d = x.shape
    tm = min(512, t)

    def body(x_ref, w_ref, dh_ref, dres_ref, dx_ref, dxb_ref, dw_ref):
        xv = x_ref[...]
        r = lax.rsqrt(jnp.mean(xv * xv, axis=-1, keepdims=True) + EPS)
        xh = xv * r
        dhv = dh_ref[...].astype(F32)
        dxh = dhv * w_ref[...]
        dx = r * (dxh - xh * jnp.mean(dxh * xh, axis=-1, keepdims=True)) + dres_ref[...]
        dx_ref[...] = dx
        dxb_ref[...] = dx.astype(BF16)

        @pl.when(pl.program_id(0) == 0)
        def _():
            dw_ref[...] = jnp.zeros_like(dw_ref)

        dw_ref[...] += _colsum(dhv * xh)

    row = pl.BlockSpec((tm, d), lambda i: (i, 0))
    vec = pl.BlockSpec((1, d), lambda i: (0, 0))
    return pl.pallas_call(
        body, grid=(t // tm,), in_specs=[row, vec, row, row], out_specs=(row, row, vec),
        out_shape=(SDS((t, d), F32), SDS((t, d), BF16), SDS((1, d), F32)), name=name,
        compiler_params=_params(("arbitrary",)),
    )(x, w.reshape(1, d), dh, dres)


def _loss_head(name, x, w, tgt):
    t, d = x.shape
    tm = min(512, t)

    def body(x_ref, w_ref, t_ref, loss_ref, dx_ref, dxb_ref, dw_ref):
        xv = x_ref[...]
        wv = w_ref[...]
        r = lax.rsqrt(jnp.mean(xv * xv, axis=-1, keepdims=True) + EPS)
        xh = xv * r
        e = xh * wv - t_ref[...]
        dy = e * (1.0 / d)
        dxh = dy * wv
        dx = r * (dxh - xh * jnp.mean(dxh * xh, axis=-1, keepdims=True))
        dx_ref[...] = dx
        dxb_ref[...] = dx.astype(BF16)

        @pl.when(pl.program_id(0) == 0)
        def _():
            dw_ref[...] = jnp.zeros_like(dw_ref)
            loss_ref[...] = jnp.zeros_like(loss_ref)

        dw_ref[...] += _colsum(dy * xh)
        loss_ref[...] += 0.5 * jnp.sum(jnp.mean(e * e, axis=-1, keepdims=True), axis=0, keepdims=True)

    row = pl.BlockSpec((tm, d), lambda i: (i, 0))
    vec = pl.BlockSpec((1, d), lambda i: (0, 0))
    return pl.pallas_call(
        body, grid=(t // tm,), in_specs=[row, vec, row],
        out_specs=(pl.BlockSpec((1, 1), lambda i: (0, 0)), row, row, vec),
        out_shape=(SDS((1, 1), F32), SDS((t, d), F32), SDS((t, d), BF16), SDS((1, d), F32)), name=name,
        compiler_params=_params(("arbitrary",)),
    )(x, w.reshape(1, d), tgt)


def _shift_down(u, s, row):
    return jnp.where(row >= s, pltpu.roll(u, shift=s, axis=0), 0.0)


def _conv_fwd(name, src, col0, w, b):
    t = src.shape[0]
    c = w.shape[1]
    tc = 256
    assert c % tc == 0 and col0 % tc == 0

    nt = c // tc
    nbuf = 3

    def fetch(src_ref, buf, sems, j):
        cols = pl.ds(pl.multiple_of((j + col0 // tc) * tc, tc), tc)
        return pltpu.make_async_copy(src_ref.at[:, cols], buf.at[j % nbuf], sems.at[j % nbuf])

    def body(src_ref, w_ref, b_ref, o_ref, pre_ref, buf, sems):
        j = pl.program_id(0)

        @pl.when(j == 0)
        def _():
            for k in range(min(nbuf - 1, nt)):
                fetch(src_ref, buf, sems, k).start()

        @pl.when(j + nbuf - 1 < nt)
        def _():
            fetch(src_ref, buf, sems, j + nbuf - 1).start()

        fetch(src_ref, buf, sems, j).wait()
        u = buf[j % nbuf]
        wv = w_ref[...]
        rolled = [None] + [pltpu.roll(u, shift=s, axis=0) for s in range(1, CONV_K)]
        row8 = lax.broadcasted_iota(jnp.int32, (8, tc), 0)
        pre = b_ref[...] + wv[3:4, :] * u
        head = b_ref[...] + wv[3:4, :] * u[:8]
        for s in range(1, CONV_K):
            pre = pre + wv[3 - s: 4 - s, :] * rolled[s]
            head = head + wv[3 - s: 4 - s, :] * jnp.where(row8 >= s, rolled[s][:8], 0.0)
        pre_ref[...] = pre
        o_ref[...] = pre * _sigmoid(pre)
        pre_ref[:8, :] = head
        o_ref[:8, :] = head * _sigmoid(head)

    col = pl.BlockSpec((t, tc), lambda j: (0, j))
    return pl.pallas_call(
        body, grid=(c // tc,),
        in_specs=[ANY, pl.BlockSpec((CONV_K, tc), lambda j: (0, j)), pl.BlockSpec((1, tc), lambda j: (0, j))],
        out_specs=(col, col), out_shape=(SDS((t, c), F32), SDS((t, c), F32)), name=name,
        scratch_shapes=[pltpu.VMEM((nbuf, t, tc), F32), pltpu.SemaphoreType.DMA((nbuf,))],
        compiler_params=_params(("arbitrary",)),
    )(src, w, b)


def _place_small(name, dsm_a, dsm_b, dproj):
    t = dsm_a.shape[0]
    width = PROJ_W - SM_OFF
    tr = min(512, t)

    def body(a_ref, b_ref, dproj_ref, o_ref):
        o_ref[:, :128] = a_ref[...] + b_ref[...]
        o_ref[:, 128:] = jnp.zeros((tr, width - 128), BF16)

    row = pl.BlockSpec((tr, 128), lambda i: (i, 0))
    return pl.pallas_call(
        body, grid=(t // tr,), in_specs=[row, row, ANY],
        out_specs=pl.BlockSpec((tr, width), lambda i: (i, SM_OFF // width)), out_shape=SDS(dproj.shape, BF16),
        input_output_aliases={2: 0}, name=name, compiler_params=_params(("parallel",)),
    )(dsm_a, dsm_b, dproj)


def _conv_bwd(name, src, col0, w, pre, dact, dproj):
    t = src.shape[0]
    c = w.shape[1]
    tc = 256

    def body(u_ref, w_ref, pre_ref, da_ref, dproj_ref, du_ref, dw_ref, db_ref):
        u = u_ref[...]
        wv = w_ref[...]
        prev = pre_ref[...]
        sg = _sigmoid(prev)
        dpre = da_ref[...] * (sg * (1.0 + prev * (1.0 - sg)))
        edge = 16
        rows = lax.broadcasted_iota(jnp.int32, (edge, tc), 0)
        du = wv[3:4, :] * dpre
        tail = wv[3:4, :] * dpre[t - edge:]
        dw_ref[3:4, :] = _colsum(dpre * u)
        for s in range(1, CONV_K):
            up = pltpu.roll(dpre, shift=t - s, axis=0)
            wrapped = rows >= edge - s
            du = du + wv[3 - s: 4 - s, :] * up
            tail = tail + wv[3 - s: 4 - s, :] * jnp.where(wrapped, 0.0, up[t - edge:])
            dw_ref[3 - s: 4 - s, :] = _colsum(up * u) - _colsum(jnp.where(wrapped, up[t - edge:] * u[t - edge:], 0.0))
        du_ref[...] = du.astype(BF16)
        du_ref[t - edge:, :] = tail.astype(BF16)
        db_ref[...] = _colsum(dpre)

    col = pl.BlockSpec((t, tc), lambda j: (0, j))
    return pl.pallas_call(
        body, grid=(c // tc,),
        in_specs=[pl.BlockSpec((t, tc), lambda j: (0, j + col0 // tc)), pl.BlockSpec((CONV_K, tc), lambda j: (0, j)),
                  col, col, ANY],
        out_specs=(pl.BlockSpec((t, tc), lambda j: (0, j + col0 // tc)), pl.BlockSpec((CONV_K, tc), lambda j: (0, j)),
                   pl.BlockSpec((1, tc), lambda j: (0, j))),
        out_shape=(SDS(dproj.shape, BF16), SDS((CONV_K, c), F32), SDS((1, c), F32)), name=name,
        input_output_aliases={4: 0},
        compiler_params=_params(("parallel",)),
    )(src, w, pre, dact, dproj)


def _tri(q):
    ii = lax.broadcasted_iota(jnp.int32, (q, q), 0)
    jj = lax.broadcasted_iota(jnp.int32, (q, q), 1)
    return ii, jj


def _dot01(x, r01, dims, terms=3):
    out, rem = None, x
    for i in range(terms):
        hi = rem.astype(BF16)
        d = lax.dot_general(hi, r01, (dims, ((), ())), preferred_element_type=F32)
        out = d if out is None else out + d
        if i + 1 < terms:
            rem = rem - hi.astype(F32)
    return out


def _ssd_common(act, sm, dtb, arow, rmat):
    q = CHUNK
    ii, jj = _tri(q)
    lane = lax.broadcasted_iota(jnp.int32, (q, 128), 1)
    m16 = lane < SSM_HEADS
    dt = jnp.where(m16, _softplus(sm + dtb), 0.0)
    a = dt * arow
    tril = (ii >= jj).astype(F32)
    triu = (ii <= jj).astype(F32)
    acum = _hdot(tril, a)
    acum_r = _hdot(a.T, triu)
    dtx = _dot01(dt, rmat, NN)
    acx = _dot01(acum, rmat, NN)
    ex = jnp.exp(acx)
    alx = acx[q - 1: q, :]
    dex = jnp.exp(alx - acx)
    xs = act[:, :1024]
    return dict(ii=ii, jj=jj, m16=m16, dt=dt, a=a, triu=triu, acum=acum, acum_r=acum_r, dtx=dtx, ex=ex, dex=dex,
                elx=jnp.exp(alx), xs=xs, x=xs * dtx)


def _ssd_lmat(cm, h):
    return jnp.where(cm["ii"] >= cm["jj"], jnp.exp(cm["acum"][:, h: h + 1] - cm["acum_r"][h: h + 1, :]), 0.0)


def _ssd_fwd(name, act, proj, dtb, arow, dxrow, nw, rmat):
    t = act.shape[0]
    q = CHUNK
    nc = t // q
    hg = SSM_HEADS // SSM_GROUPS
    gw = hg * SSM_P

    def body(act_ref, z_ref, sm_ref, dtb_ref, arow_ref, dx_ref, nw_ref, r_ref, y_ref, ys_ref, st_ref, s_scr, yd_scr):
        @pl.when(pl.program_id(0) == 0)
        def _():
            s_scr[...] = jnp.zeros_like(s_scr)

        s_all = s_scr[...]
        for sub in range(cps):
            rows = pl.ds(sub * q, q)
            s_all = chunk(act_ref.at[rows, :], z_ref.at[rows, :], sm_ref.at[rows, :], dtb_ref, arow_ref, dx_ref, nw_ref, r_ref,
                          y_ref.at[rows, :], ys_ref.at[rows, :], st_ref.at[sub], yd_scr.at[rows, :], s_all)
        s_scr[...] = s_all

    def chunk(act_ref, z_ref, sm_ref, dtb_ref, arow_ref, dx_ref, nw_ref, r_ref, y_ref, ys_ref, st_ref, yd_scr, s_all):
        st_ref[...] = s_all
        actv = act_ref[...]
        cm = _ssd_common(actv, sm_ref[...], dtb_ref[...], arow_ref[...], r_ref[...])
        x = cm["x"]
        xd = x * cm["dex"]
        yoffs, snew = [], []
        for g in range(SSM_GROUPS):
            bg = actv[:, 1024 + g * SSM_N: 1024 + (g + 1) * SSM_N]
            cg = actv[:, 1280 + g * SSM_N: 1280 + (g + 1) * SSM_N]
            sg = s_all[:, g * gw: (g + 1) * gw]
            cb = _bdot(cg, bg, NT)
            yoffs.append(_bdot(cg, sg, NN))
            snew.append(_bdot(bg, xd[:, g * gw: (g + 1) * gw], TN))
            for r in range(hg):
                h = g * hg + r
                mm = cb * _ssd_lmat(cm, h)
                yd_scr[:, h * SSM_P: (h + 1) * SSM_P] = _bdot(mm, x[:, h * SSM_P: (h + 1) * SSM_P], NN)
        s_next = s_all * cm["elx"] + jnp.concatenate(snew, axis=1)
        ysc = yd_scr[...] + jnp.concatenate(yoffs, axis=1) * cm["ex"]
        ys_ref[...] = ysc
        zv = z_ref[...]
        yg = (ysc + dx_ref[...] * cm["xs"]) * (zv * _sigmoid(zv))
        nwv = nw_ref[...]
        for g in range(SSM_GROUPS):
            sl = yg[:, g * gw: (g + 1) * gw]
            rr = lax.rsqrt(jnp.mean(sl * sl, axis=-1, keepdims=True) + EPS)
            y_ref[:, g * gw: (g + 1) * gw] = (sl * rr * nwv[:, g * gw: (g + 1) * gw]).astype(BF16)
        return s_next

    cps = SCAN_CHUNKS_PER_STEP if nc % SCAN_CHUNKS_PER_STEP == 0 else 1
    qq = cps * q
    vec128 = pl.BlockSpec((1, 128), lambda c: (0, 0))
    vec1k = pl.BlockSpec((1, 1024), lambda c: (0, 0))
    return pl.pallas_call(
        body, grid=(nc // cps,),
        in_specs=[pl.BlockSpec((qq, SSM_CONV), lambda c: (c, 0)), pl.BlockSpec((qq, 1024), lambda c: (c, Z_OFF // 1024)),
                  pl.BlockSpec((qq, 128), lambda c: (c, SM_OFF // 128)), vec128, vec128, vec1k, vec1k,
                  pl.BlockSpec((128, 1024), lambda c: (0, 0))],
        out_specs=(pl.BlockSpec((qq, 1024), lambda c: (c, 0)), pl.BlockSpec((qq, 1024), lambda c: (c, 0)),
                   pl.BlockSpec((cps, 128, 1024), lambda c: (c, 0, 0))),
        out_shape=(SDS((t, 1024), BF16), SDS((t, 1024), F32), SDS((nc, 128, 1024), F32)),
        scratch_shapes=[pltpu.VMEM((128, 1024), F32), pltpu.VMEM((qq, 1024), F32)], name=name,
        compiler_params=_params(("arbitrary",)),
    )(act, proj, proj, dtb, arow, dxrow, nw, rmat)


def _ssd_bwd(name, act, proj, dtb, arow, dxrow, nw, rmat, ysc, states, dy, dproj):
    t = act.shape[0]
    q = CHUNK
    nc = t // q
    hg = SSM_HEADS // SSM_GROUPS
    gw = hg * SSM_P

    def body(act_ref, z_ref, sm_ref, dtb_ref, arow_ref, dx_ref, nw_ref, r_ref, ys_ref, st_ref, dy_ref, dproj_ref,
             dact_ref, dz_ref, dsm_ref, dnw_ref, dd_ref, dal_ref, ddtb_ref, ds_scr, dxd_scr):
        @pl.when(pl.program_id(0) == 0)
        def _():
            ds_scr[...] = jnp.zeros_like(ds_scr)
            dnw_ref[...] = jnp.zeros_like(dnw_ref)
            dd_ref[...] = jnp.zeros_like(dd_ref)
            dal_ref[...] = jnp.zeros_like(dal_ref)
            ddtb_ref[...] = jnp.zeros_like(ddtb_ref)

        dsn = ds_scr[...]
        for sub in reversed(range(cps)):
            rows = pl.ds(sub * q, q)
            dsn = chunk(act_ref.at[rows, :], z_ref.at[rows, :], sm_ref.at[rows, :], dtb_ref, arow_ref, dx_ref, nw_ref, r_ref,
                        ys_ref.at[rows, :], st_ref.at[sub], dy_ref.at[rows, :], dact_ref.at[rows, :], dz_ref.at[rows, :],
                        dsm_ref.at[rows, :], dnw_ref, dd_ref, dal_ref, ddtb_ref, dxd_scr.at[rows, :], dsn)
        ds_scr[...] = dsn

    def chunk(act_ref, z_ref, sm_ref, dtb_ref, arow_ref, dx_ref, nw_ref, r_ref, ys_ref, st_ref, dy_ref,
              dact_ref, dz_ref, dsm_ref, dnw_ref, dd_ref, dal_ref, ddtb_ref, dxd_scr, dsn):
        actv = act_ref[...]
        smv = sm_ref[...]
        rmat_v = r_ref[...]
        cm = _ssd_common(actv, smv, dtb_ref[...], arow_ref[...], rmat_v)
        ii, jj = cm["ii"], cm["jj"]
        x, xs = cm["x"], cm["xs"]
        s_all = st_ref[...]
        ysv = ys_ref[...]
        dxr = dx_ref[...]
        y = ysv + dxr * xs
        zv = z_ref[...]
        sz = _sigmoid(zv)
        silz = zv * sz
        yg = y * silz
        dout = dy_ref[...]
        nwv = nw_ref[...]
        dyn = dout * nwv
        yn_parts, dyg_parts = [], []
        for g in range(SSM_GROUPS):
            sl = yg[:, g * gw: (g + 1) * gw]
            rr = lax.rsqrt(jnp.mean(sl * sl, axis=-1, keepdims=True) + EPS)
            yn = sl * rr
            dn = dyn[:, g * gw: (g + 1) * gw]
            yn_parts.append(yn)
            dyg_parts.append(rr * (dn - yn * jnp.mean(dn * yn, axis=-1, keepdims=True)))
        dnw_ref[...] += _colsum(dout * jnp.concatenate(yn_parts, axis=1))
        dyg = jnp.concatenate(dyg_parts, axis=1)
        dyv = dyg * silz
        dz_ref[...] = (dyg * y * (sz * (1.0 + zv * (1.0 - sz)))).astype(BF16)
        dd_ref[...] += _dot01(_colsum(dyv * xs), rmat_v, NT)
        dxs = dyv * dxr
        dcs = dyv * cm["ex"]
        xd = x * cm["dex"]
        dxst_parts, ds_parts, db_parts, dc_parts, yoff_parts, wcol_rows = [], [], [], [], [], []
        lane128 = lax.broadcasted_iota(jnp.int32, (q, 128), 1)
        wrow = jnp.zeros((q, 128), F32)
        for g in range(SSM_GROUPS):
            bg = actv[:, 1024 + g * SSM_N: 1024 + (g + 1) * SSM_N]
            cg = actv[:, 1280 + g * SSM_N: 1280 + (g + 1) * SSM_N]
            sg = s_all[:, g * gw: (g + 1) * gw]
            dsng = dsn[:, g * gw: (g + 1) * gw]
            dcsg = dcs[:, g * gw: (g + 1) * gw]
            dcg = _bdot(dcsg, sg, NT)
            yoff_parts.append(_bdot(cg, sg, NN))
            ds_parts.append(_bdot(cg, dcsg, TN))
            dxst_parts.append(_bdot(bg, dsng, NN))
            dbg = _bdot(xd[:, g * gw: (g + 1) * gw], dsng, NT)
            cb = _bdot(cg, bg, NT)
            dcb = jnp.zeros((q, q), F32)
            for r in range(hg):
                h = g * hg + r
                lm = _ssd_lmat(cm, h)
                mm = cb * lm
                dyh = dyv[:, h * SSM_P: (h + 1) * SSM_P]
                dm = jnp.where(ii >= jj, _bdot(dyh, x[:, h * SSM_P: (h + 1) * SSM_P], NT), 0.0)
                dxd_scr[:, h * SSM_P: (h + 1) * SSM_P] = _bdot(mm, dyh, TN)
                dcb = dcb + dm * lm
                wm = dm * mm
                wrow = wrow + jnp.where(lane128 == h, _rowsum(wm), 0.0)
                wcol_rows.append(_colsum(wm))
            dc_parts.append(dcg + _bdot(dcb, bg, NN))
            db_parts.append(dbg + _bdot(dcb, cg, TN))
        dxst = jnp.concatenate(dxst_parts, axis=1) * cm["dex"]
        dx = dxd_scr[...] + dxst
        ds_prev = jnp.concatenate(ds_parts, axis=1) + dsn * cm["elx"]
        wcol = jnp.concatenate(wcol_rows + [jnp.zeros((128 - SSM_HEADS, q), F32)], axis=0).T
        yoff = jnp.concatenate(yoff_parts, axis=1) * cm["ex"]
        xdxst = x * dxst
        dac = wrow - wcol + _dot01(dyv * yoff - xdxst, rmat_v, NT)
        last = _dot01(_colsum(dsn * s_all) * cm["elx"] + _colsum(xdxst), rmat_v, NT)
        rowq = lax.broadcasted_iota(jnp.int32, (q, 128), 0)
        dac = dac + jnp.where(rowq == q - 1, last, 0.0)
        da = _hdot(cm["triu"], dac)
        arow_v = arow_ref[...]
        ddt = da * arow_v + _dot01(dx * xs, rmat_v, NT)
        dxs = dxs + dx * cm["dtx"]
        dal_ref[...] += _colsum(da * cm["a"])
        ddtraw = jnp.where(cm["m16"], ddt * _sigmoid(smv + dtb_ref[...]), 0.0)
        ddtb_ref[...] += _colsum(ddtraw)
        dsm_ref[...] = ddtraw.astype(BF16)
        dact_ref[:, :1024] = dxs
        for g in range(SSM_GROUPS):
            dact_ref[:, 1024 + g * SSM_N: 1024 + (g + 1) * SSM_N] = db_parts[g]
            dact_ref[:, 1280 + g * SSM_N: 1280 + (g + 1) * SSM_N] = dc_parts[g]
        return ds_prev

    cps = SCAN_CHUNKS_PER_STEP if nc % SCAN_CHUNKS_PER_STEP == 0 else 1
    qq = cps * q
    rev = lambda c: nc // cps - 1 - c
    vec128 = pl.BlockSpec((1, 128), lambda c: (0, 0))
    vec1k = pl.BlockSpec((1, 1024), lambda c: (0, 0))
    return pl.pallas_call(
        body, grid=(nc // cps,),
        in_specs=[pl.BlockSpec((qq, SSM_CONV), lambda c: (rev(c), 0)),
                  pl.BlockSpec((qq, 1024), lambda c: (rev(c), Z_OFF // 1024)),
                  pl.BlockSpec((qq, 128), lambda c: (rev(c), SM_OFF // 128)), vec128, vec128, vec1k, vec1k,
                  pl.BlockSpec((128, 1024), lambda c: (0, 0)),
                  pl.BlockSpec((qq, 1024), lambda c: (rev(c), 0)), pl.BlockSpec((cps, 128, 1024), lambda c: (rev(c), 0, 0)),
                  pl.BlockSpec((qq, 1024), lambda c: (rev(c), 0)), ANY],
        out_specs=(pl.BlockSpec((qq, SSM_CONV), lambda c: (rev(c), 0)),
                   pl.BlockSpec((qq, 1024), lambda c: (rev(c), Z_OFF // 1024)),
                   pl.BlockSpec((qq, 128), lambda c: (rev(c), 0)), vec1k, vec128, vec128, vec128),
        out_shape=(SDS((t, SSM_CONV), F32), SDS(dproj.shape, BF16), SDS((t, 128), BF16), SDS((1, 1024), F32),
                   SDS((1, 128), F32), SDS((1, 128), F32), SDS((1, 128), F32)),
        input_output_aliases={11: 1},
        scratch_shapes=[pltpu.VMEM((128, 1024), F32), pltpu.VMEM((qq, 1024), F32)], name=name,
        compiler_params=_params(("arbitrary",)),
    )(act, proj, proj, dtb, arow, dxrow, nw, rmat, ysc, states, dy, dproj)


def _split(a):
    hi = a.astype(BF16)
    return hi, (a - hi.astype(F32)).astype(BF16)


def _dot3(a, b, dims=NN):
    (ah, al), (bh, bl) = a, b

    def d(x, y):
        return lax.dot_general(x, y, (dims, ((), ())), preferred_element_type=F32)

    return d(ah, bh) + (d(ah, bl) + d(al, bh))


def _tri_inverses(amats, ii, jj):
    eye = jnp.where(ii == jj, 1.0, 0.0)
    tms = [eye - a for a in amats]
    sp = [_split(a) for a in amats]
    for _ in range(5):
        sp = [_split(_dot3(s, s)) for s in sp]
        tms = [t + _dot3(_split(t), s) for t, s in zip(tms, sp)]
    return tms


def _gdn_common(sm, gb, garow):
    q = CHUNK
    ii, jj = _tri(q)
    lane = lax.broadcasted_iota(jnp.int32, (q, 128), 1)
    ma = (lane >= LANE_A) & (lane < LANE_A + GDN_HEADS)
    spre = sm + gb
    g = jnp.where(ma, garow * _softplus(spre), 0.0)
    beta = _sigmoid(sm)
    tril = (ii >= jj).astype(F32)
    triu = (ii <= jj).astype(F32)
    gc = _hdot(tril, g)
    gc_r = _hdot(g.T, triu)
    return dict(ii=ii, jj=jj, lane=lane, ma=ma, spre=spre, g=g, beta=beta, triu=triu, gc=gc, gc_r=gc_r)


def _each(f, *lists):
    return [f(*xs) for xs in zip(*lists)]


GDN_SCALE = GDN_DK ** -0.5
GDN_BWD_HEAD_GROUPS = (range(GDN_HEADS),)


def _gdn_heads(cm, actv, states, heads=range(GDN_HEADS)):
    q = CHUNK
    ii, jj = cm["ii"], cm["jj"]
    qr = [actv[:, h * 128: (h + 1) * 128] for h in heads]
    kr = [actv[:, 1024 + h * 128: 1024 + (h + 1) * 128] for h in heads]
    v = [actv[:, 2048 + h * 128: 2048 + (h + 1) * 128] for h in heads]
    rq = _each(lambda x: lax.rsqrt(_rowsum(x * x) + EPS), qr)
    rk = _each(lambda x: lax.rsqrt(_rowsum(x * x) + EPS), kr)
    qn = _each(lambda x, r: x * r * GDN_SCALE, qr, rq)
    kn = _each(lambda x, r: x * r, kr, rk)
    gcc = [cm["gc"][:, LANE_A + h: LANE_A + h + 1] for h in heads]
    gcr = [cm["gc_r"][LANE_A + h: LANE_A + h + 1, :] for h in heads]
    bcol = [cm["beta"][:, LANE_B + h: LANE_B + h + 1] for h in heads]
    dm = _each(lambda c, r: jnp.where(ii >= jj, jnp.exp(c - r), 0.0), gcc, gcr)
    kq = _each(lambda k, a: _bdot(jnp.concatenate([k, a], axis=0), k, NT), kn, qn)
    ak = _each(lambda x, d: jnp.where(ii > jj, x[:q] * d, 0.0), kq, dm)
    qkm = _each(lambda x, d: jnp.where(ii >= jj, x[q:] * d, 0.0), kq, dm)
    tm = _tri_inverses(_each(lambda a, b: a * b, ak, bcol), ii, jj)
    eg = _each(jnp.exp, gcc)
    gl = [c[q - 1: q, :] for c in gcc]
    rm = _each(lambda vv, k, b, e: jnp.concatenate([vv * b, k * (b * e)], axis=1), v, kn, bcol, eg)
    tt = _each(lambda t, r: _dot3(_split(t), _split(r)), tm, rm)
    w = [t[:, 128:] for t in tt]
    qg = _each(lambda a, e: a * e, qn, eg)
    ws = _each(lambda ww, a, s: _bdot(jnp.concatenate([ww, a], axis=0), s, NN), w, qg, states)
    vnew = _each(lambda t, x: t[:, :128] - x[:q], tt, ws)
    return dict(qr=qr, v=v, rq=rq, rk=rk, qn=qn, kn=kn, gcc=gcc, bcol=bcol, dm=dm, ak=ak, tm=tm, eg=eg, gl=gl,
                egl=_each(jnp.exp, gl), ed=_each(lambda g, c: jnp.exp(g - c), gl, gcc), tt=tt, w=w, vnew=vnew, qkm=qkm,
                qg=qg, qgs=[x[q:] for x in ws])


def _gdn_fwd(name, act, proj, gb, garow, gnw):
    t = act.shape[0]
    q = CHUNK
    nc = t // q

    def body(act_ref, gz_ref, sm_ref, gb_ref, ga_ref, nw_ref, y_ref, o_ref, st_ref, s_scr):
        @pl.when(pl.program_id(0) == 0)
        def _():
            s_scr[...] = jnp.zeros_like(s_scr)

        states = [s_scr[h * 128: (h + 1) * 128, :] for h in range(GDN_HEADS)]
        for sub in range(cps):
            rows = pl.ds(sub * q, q)
            states = chunk(act_ref.at[rows, :], gz_ref.at[rows, :], sm_ref.at[rows, :], gb_ref, ga_ref, nw_ref,
                           y_ref.at[rows, :], o_ref.at[rows, :], st_ref.at[sub], states)
        for h in range(GDN_HEADS):
            s_scr[h * 128: (h + 1) * 128, :] = states[h]

    def chunk(act_ref, gz_ref, sm_ref, gb_ref, ga_ref, nw_ref, y_ref, o_ref, st_ref, states):
        for h in range(GDN_HEADS):
            st_ref[h * 128: (h + 1) * 128, :] = states[h]
        actv = act_ref[...]
        cm = _gdn_common(sm_ref[...], gb_ref[...], ga_ref[...])
        nwv = nw_ref[...]
        gzv = gz_ref[...]
        hd = _gdn_heads(cm, actv, states)
        outs = _each(lambda qs, m, vn: qs + _bdot(m, vn, NN), hd["qgs"], hd["qkm"], hd["vnew"])
        snew = _each(lambda s, e, k, d, vn: s * e + _bdot(k * d, vn, TN), states, hd["egl"], hd["kn"], hd["ed"], hd["vnew"])
        for h in range(GDN_HEADS):
            o = outs[h]
            o_ref[:, h * 128: (h + 1) * 128] = o
            rr = lax.rsqrt(jnp.mean(o * o, axis=-1, keepdims=True) + EPS)
            gz = gzv[:, h * 128: (h + 1) * 128]
            y_ref[:, h * 128: (h + 1) * 128] = (o * rr * nwv * (gz * _sigmoid(gz))).astype(BF16)
        return snew

    cps = SCAN_CHUNKS_PER_STEP if nc % SCAN_CHUNKS_PER_STEP == 0 else 1
    qq = cps * q
    vec128 = pl.BlockSpec((1, 128), lambda c: (0, 0))
    return pl.pallas_call(
        body, grid=(nc // cps,),
        in_specs=[pl.BlockSpec((qq, GDN_QKV), lambda c: (c, 0)), pl.BlockSpec((qq, 1024), lambda c: (c, GZ_OFF // 1024)),
                  pl.BlockSpec((qq, 128), lambda c: (c, SM_OFF // 128)), vec128, vec128, vec128],
        out_specs=(pl.BlockSpec((qq, 1024), lambda c: (c, 0)), pl.BlockSpec((qq, 1024), lambda c: (c, 0)),
                   pl.BlockSpec((cps, 1024, 128), lambda c: (c, 0, 0))),
        out_shape=(SDS((t, 1024), BF16), SDS((t, 1024), F32), SDS((nc, 1024, 128), F32)),
        scratch_shapes=[pltpu.VMEM((1024, 128), F32)], name=name, compiler_params=_params(("arbitrary",)),
    )(act, proj, proj, gb, garow, gnw)


def _gdn_bwd(name, act, proj, gb, garow, gnw, oraw, states, dy, dproj):
    t = act.shape[0]
    q = CHUNK
    nc = t // q

    def body(act_ref, gz_ref, sm_ref, gb_ref, ga_ref, nw_ref, o_ref, st_ref, dy_ref, dproj_ref,
             dact_ref, dgz_ref, dsm_ref, dnw_ref, dal_ref, dgb_ref, ds_scr):
        @pl.when(pl.program_id(0) == 0)
        def _():
            ds_scr[...] = jnp.zeros_like(ds_scr)
            dnw_ref[...] = jnp.zeros_like(dnw_ref)
            dal_ref[...] = jnp.zeros_like(dal_ref)
            dgb_ref[...] = jnp.zeros_like(dgb_ref)

        dsn = [ds_scr[h * 128: (h + 1) * 128, :] for h in range(GDN_HEADS)]
        for sub in reversed(range(cps)):
            rows = pl.ds(sub * q, q)
            dsn = chunk(act_ref.at[rows, :], gz_ref.at[rows, :], sm_ref.at[rows, :], gb_ref, ga_ref, nw_ref,
                        o_ref.at[rows, :], st_ref.at[sub], dy_ref.at[rows, :],
                        dact_ref.at[rows, :], dgz_ref.at[rows, :], dsm_ref.at[rows, :], dnw_ref, dal_ref, dgb_ref, dsn)
        for h in range(GDN_HEADS):
            ds_scr[h * 128: (h + 1) * 128, :] = dsn[h]

    def chunk(act_ref, gz_ref, sm_ref, gb_ref, ga_ref, nw_ref, o_ref, st_ref, dy_ref,
              dact_ref, dgz_ref, dsm_ref, dnw_ref, dal_ref, dgb_ref, dsn):
        actv = act_ref[...]
        smv = sm_ref[...]
        garow_v = ga_ref[...]
        cm = _gdn_common(smv, gb_ref[...], garow_v)
        ii, jj, lane = cm["ii"], cm["jj"], cm["lane"]
        nwv = nw_ref[...]
        rowq = lax.broadcasted_iota(jnp.int32, (q, 1), 0)
        dgc_all = jnp.zeros((q, 128), F32)
        dbeta_all = jnp.zeros((q, 128), F32)
        dnw_acc = jnp.zeros((1, 128), F32)
        ds_out = []
        ov, gzv, dyv = o_ref[...], gz_ref[...], dy_ref[...]
        for heads in GDN_BWD_HEAD_GROUPS:
            part = group(heads, cm, actv, ov, gzv, dyv, nwv, rowq, st_ref, [dsn[h] for h in heads], dact_ref, dgz_ref)
            ds_out += part[0]
            dgc_all, dbeta_all, dnw_acc = dgc_all + part[1], dbeta_all + part[2], dnw_acc + part[3]
        dnw_ref[...] += dnw_acc
        dg = _hdot(cm["triu"], dgc_all)
        da_raw = jnp.where(cm["ma"], dg * garow_v * _sigmoid(cm["spre"]), 0.0)
        dal_ref[...] += _colsum(dg * cm["g"])
        dgb_ref[...] += _colsum(da_raw)
        beta = cm["beta"]
        dsm_ref[...] = (da_raw + dbeta_all * beta * (1.0 - beta)).astype(BF16)
        return ds_out

    def group(heads, cm, actv, ov, gzv, dyv, nwv, rowq, st_ref, dsn, dact_ref, dgz_ref):
        ii, jj, lane = cm["ii"], cm["jj"], cm["lane"]
        dgc_all = jnp.zeros((q, 128), F32)
        dbeta_all = jnp.zeros((q, 128), F32)
        dnw_acc = jnp.zeros((1, 128), F32)
        ds_out = []
        sts = [st_ref[h * 128: (h + 1) * 128, :] for h in heads]
        hd = _gdn_heads(cm, actv, sts, heads)
        qn, kn, v, eg, ed, egl, bcol = hd["qn"], hd["kn"], hd["v"], hd["eg"], hd["ed"], hd["egl"], hd["bcol"]
        vnew, qkm, qg, w, tt, dm, ak = hd["vnew"], hd["qkm"], hd["qg"], hd["w"], hd["tt"], hd["dm"], hd["ak"]
        do = []
        for h in heads:
            hs = slice(h * 128, (h + 1) * 128)
            o = ov[:, hs]
            rr = lax.rsqrt(jnp.mean(o * o, axis=-1, keepdims=True) + EPS)
            on = o * rr
            gz = gzv[:, hs]
            sz = _sigmoid(gz)
            silz = gz * sz
            dyh = dyv[:, hs]
            dnw_acc = dnw_acc + _colsum(dyh * on * silz)
            dgz_ref[:, hs] = (dyh * on * nwv * (sz * (1.0 + gz * (1.0 - sz)))).astype(BF16)
            don = dyh * nwv * silz
            do.append(rr * (don - on * jnp.mean(don * on, axis=-1, keepdims=True)))
        kd = _each(lambda k, e: k * e, kn, ed)
        dkd = _each(lambda vn, d: _bdot(vn, d, NT), vnew, dsn)
        dvnew_a = _each(lambda k, d: _bdot(k, d, NN), kd, dsn)
        ded = _each(lambda a, b: _rowsum(a * b), dkd, kd)
        dgl = _each(lambda d, s, e, de: jnp.sum(_rowsum(d * s), axis=0, keepdims=True) * e + _colsum(de), dsn, sts, egl, ded)
        dqk = _each(lambda d, vn: jnp.where(ii >= jj, _bdot(d, vn, NT), 0.0), do, vnew)
        dvnew = _each(lambda a, m, d: a + _bdot(m, d, TN), dvnew_a, qkm, do)
        pq = _each(lambda a, b: a * b, dqk, dm)
        w1 = _each(lambda a, b: a * b, dqk, qkm)
        dod = _each(lambda a, b: jnp.concatenate([a, b], axis=0), do, dvnew)
        dos = _each(lambda x, s: _bdot(x, s, NT), dod, sts)
        dqg = [x[:q] for x in dos]
        dw = [-x[q:] for x in dos]
        ds12 = _each(lambda a, ww, x: _bdot(jnp.concatenate([a, -ww], axis=0), x, TN), qg, w, dod)
        dr = _each(lambda t, a, b: _dot3(_split(t), _split(jnp.concatenate([a, b], axis=1)), TN), hd["tm"], dvnew, dw)
        da = _each(lambda r, t: jnp.where(ii > jj, -_dot3(_split(r), _split(t), NT), 0.0), dr, tt)
        sk = _each(lambda r, k: _rowsum(r[:, 128:] * k), dr, kn)
        pk = _each(lambda a, d, b: a * d * b, da, dm, bcol)
        pkn = _each(lambda p, pp, k: _bdot(jnp.concatenate([p, pp + pp.T], axis=0), k, NN), pq, pk, kn)
        dq = _each(lambda a, e, x: a * e + x[:q], dqg, eg, pkn)
        dk = _each(lambda a, e, p, x, r, b, eg_, y: a * e + _bdot(p, x, TN) + r[:, 128:] * (b * eg_) + y[q:],
                   dkd, ed, pq, qn, dr, bcol, eg, pkn)
        w2 = _each(lambda a, k, b: a * (k * b), da, ak, bcol)
        for i, h in enumerate(heads):
            hs = slice(h * 128, (h + 1) * 128)
            dgc = (-ded[i] + _rowsum(dqg[i] * qg[i]) + _rowsum(w1[i]) - _rowsum(w1[i].T) + sk[i] * bcol[i] * eg[i]
                   + _rowsum(w2[i]) - _rowsum(w2[i].T) + jnp.where(rowq == q - 1, dgl[i], 0.0))
            dbeta = _rowsum(dr[i][:, :128] * v[i]) + sk[i] * eg[i] + _rowsum(da[i] * ak[i])
            qhat = hd["qr"][i] * hd["rq"][i]
            dqhat = dq[i] * GDN_SCALE
            dact_ref[:, hs] = hd["rq"][i] * (dqhat - qhat * _rowsum(dqhat * qhat))
            dact_ref[:, 1024 + h * 128: 1024 + (h + 1) * 128] = hd["rk"][i] * (dk[i] - kn[i] * _rowsum(dk[i] * kn[i]))
            dact_ref[:, 2048 + h * 128: 2048 + (h + 1) * 128] = dr[i][:, :128] * bcol[i]
            dgc_all = dgc_all + jnp.where(lane == LANE_A + h, dgc, 0.0)
            dbeta_all = dbeta_all + jnp.where(lane == LANE_B + h, dbeta, 0.0)
            ds_out.append(dsn[i] * egl[i] + ds12[i])
        return ds_out, dgc_all, dbeta_all, dnw_acc

    cps = 2 if nc % 2 == 0 else 1
    qq = cps * q
    rev = lambda c: nc // cps - 1 - c
    vec128 = pl.BlockSpec((1, 128), lambda c: (0, 0))
    return pl.pallas_call(
        body, grid=(nc // cps,),
        in_specs=[pl.BlockSpec((qq, GDN_QKV), lambda c: (rev(c), 0)),
                  pl.BlockSpec((qq, 1024), lambda c: (rev(c), GZ_OFF // 1024)),
                  pl.BlockSpec((qq, 128), lambda c: (rev(c), SM_OFF // 128)), vec128, vec128, vec128,
                  pl.BlockSpec((qq, 1024), lambda c: (rev(c), 0)), pl.BlockSpec((cps, 1024, 128), lambda c: (rev(c), 0, 0)),
                  pl.BlockSpec((qq, 1024), lambda c: (rev(c), 0)), ANY],
        out_specs=(pl.BlockSpec((qq, GDN_QKV), lambda c: (rev(c), 0)),
                   pl.BlockSpec((qq, 1024), lambda c: (rev(c), GZ_OFF // 1024)),
                   pl.BlockSpec((qq, 128), lambda c: (rev(c), 0)), vec128, vec128, vec128),
        out_shape=(SDS((t, GDN_QKV), F32), SDS(dproj.shape, BF16), SDS((t, 128), BF16), SDS((1, 128), F32),
                   SDS((1, 128), F32), SDS((1, 128), F32)),
        input_output_aliases={9: 1},
        scratch_shapes=[pltpu.VMEM((1024, 128), F32)], name=name, compiler_params=_params(("arbitrary",)),
    )(act, proj, proj, gb, garow, gnw, oraw, states, dy, dproj)


def _row_tile(r):
    for cand in (512, 256, 128, 64, 32, 16, 8):
        if r % cand == 0:
            return cand
    return r


def _sum_terms(name, terms, out_dtype):
    shape = terms[0][0].shape[1:]
    c = shape[-1]
    r = 1
    for s in shape[:-1]:
        r *= s
    tr = min(_row_tile(r), 256)
    n = len(terms)

    def body(*refs):
        acc = refs[0][...].astype(F32)
        for k in range(1, n):
            acc = acc + refs[k][...].astype(F32)
        refs[n][...] = acc.astype(out_dtype)

    in_specs = [pl.BlockSpec((None, tr, c), lambda i, q=lead: (q, i, 0)) for _, lead in terms]
    args = [a.reshape(a.shape[0], r, c) for a, _ in terms]
    out = pl.pallas_call(body, grid=(r // tr,), in_specs=in_specs, out_specs=pl.BlockSpec((tr, c), lambda i: (i, 0)),
                         out_shape=SDS((r, c), out_dtype), name=name, compiler_params=_params(("parallel",)))(*args)
    return out.reshape(shape)


def _adamw_math(w, g, m, v):
    mn = ADAM_B1 * m + (1.0 - ADAM_B1) * g
    vn = ADAM_B2 * v + (1.0 - ADAM_B2) * (g * g)
    m_hat = mn / (1.0 - ADAM_B1 ** ADAM_STEP)
    v_hat = vn / (1.0 - ADAM_B2 ** ADAM_STEP)
    return -ADAM_LR * (m_hat / (jnp.sqrt(v_hat) + ADAM_EPS) + ADAM_WD * w), mn, vn


def _adamw_rows(name, w, g, m, v):
    r, a, c = w.shape
    tr = next(t for t in range(min(r, 128), 0, -1) if r % t == 0)

    def body(w_ref, g_ref, m_ref, v_ref, d_ref, nm_ref, nv_ref):
        d_ref[...], nm_ref[...], nv_ref[...] = _adamw_math(w_ref[...], g_ref[...], m_ref[...], v_ref[...])

    spec = pl.BlockSpec((tr, a, c), lambda i: (i, 0, 0))
    return pl.pallas_call(body, grid=(r // tr,), in_specs=[spec] * 4, out_specs=(spec,) * 3,
                          out_shape=(SDS(w.shape, F32),) * 3, name=name, compiler_params=_params(("parallel",)))(w, g, m, v)


def _adamw(name, w, g, m, v):
    shape = w.shape
    c = shape[-1]
    per_layer = isinstance(g, (list, tuple))
    nl = len(g) if per_layer else 1
    gs = [a.reshape(-1, c) for a in g] if per_layer else [g.reshape(-1, c)]
    r = gs[0].shape[0]
    w3, m3, v3 = (a.reshape(nl, r, c) for a in (w, m, v))
    tr = min(_row_tile(r), 256)

    def body(*refs):
        w_ref, m_ref, v_ref = refs[:3]
        g_refs = refs[3: 3 + nl]
        go_ref, d_ref, nm_ref, nv_ref = refs[3 + nl:]
        layer = pl.program_id(0)
        gv = g_refs[0][...]
        for k in range(1, nl):
            gv = jnp.where(layer == k, g_refs[k][...], gv)
        go_ref[...] = gv
        d_ref[...], nm_ref[...], nv_ref[...] = _adamw_math(w_ref[...], gv, m_ref[...], v_ref[...])

    spec3 = pl.BlockSpec((None, tr, c), lambda l, i: (l, i, 0))
    gspec = pl.BlockSpec((tr, c), lambda l, i: (i, 0))
    outs = pl.pallas_call(body, grid=(nl, r // tr), in_specs=[spec3] * 3 + [gspec] * nl, out_specs=(spec3,) * 4,
                          out_shape=(SDS((nl, r, c), F32),) * 4, name=name,
                          compiler_params=_params(("parallel", "parallel")))(w3, m3, v3, *gs)
    return tuple(o.reshape(shape) for o in outs)


ANY = pl.BlockSpec(memory_space=pl.ANY)
MESH = pl.DeviceIdType.MESH


def _allgather(name, xs):
    n = len(xs)

    def body(*refs):
        x_refs, out_refs = refs[:n], refs[n: 2 * n]
        send_sems, recv_sems, local_sems = refs[2 * n:]
        x, y, cc = lax.axis_index("x"), lax.axis_index("y"), lax.axis_index("c")
        me, sibling = (x, y, cc), (x, y, 1 - cc)
        chips = [(1 - x, y), (x, 1 - y), (1 - x, 1 - y)]

        def rows(a, px, py, pc):
            return out_refs[a].at[4 * px + 2 * py + pc]

        def copy(a, k, block, to, src=None):
            return pltpu.make_async_remote_copy(
                src_ref=rows(a, *block) if src is None else src, dst_ref=rows(a, *block),
                send_sem=send_sems.at[7 * a + k], recv_sem=recv_sems.at[7 * a + k], device_id=to, device_id_type=MESH)

        mine = [pltpu.make_async_copy(x_refs[a], rows(a, *me), local_sems.at[a]) for a in range(n)]
        for cp in mine:
            cp.start()
        first = []
        for a in range(n):
            first.append(copy(a, 0, me, sibling, src=x_refs[a]))
            first += [copy(a, 1 + j, me, (*chip, cc), src=x_refs[a]) for j, chip in enumerate(chips)]
        for cp in first:
            cp.start()
        passed = []
        for j, chip in enumerate(chips):
            for a in range(n):
                copy(a, 1 + j, (*chip, cc), me).wait_recv()
                fwd = copy(a, 4 + j, (*chip, cc), sibling)
                fwd.start()
                passed.append(fwd)
        for a in range(n):
            copy(a, 0, sibling, me).wait_recv()
        for j, chip in enumerate(chips):
            for a in range(n):
                copy(a, 4 + j, (*chip, 1 - cc), me).wait_recv()
        for cp in first + passed:
            cp.wait_send()
        for cp in mine:
            cp.wait()

    return pl.pallas_call(
        body, out_shape=tuple(SDS((N_DEV,) + a.shape, a.dtype) for a in xs), in_specs=[ANY] * n, out_specs=(ANY,) * n,
        scratch_shapes=[pltpu.SemaphoreType.DMA((7 * n,)), pltpu.SemaphoreType.DMA((7 * n,)),
                        pltpu.SemaphoreType.DMA((n,))],
        name=name,
    )(*xs)


def _allgather_seq(name, xs, collective_id):
    n = len(xs)
    x_refs = [jax.new_ref(a, memory_space=pltpu.MemorySpace.HBM) for a in xs]
    out_refs = [jax.empty_ref(SDS((N_DEV,) + a.shape, a.dtype), memory_space=pltpu.MemorySpace.HBM) for a in xs]

    @pl.kernel(mesh=plsc.ScalarSubcoreMesh(axis_name="seq", num_cores=1), name=name,
               scratch_types=(pltpu.SemaphoreType.DMA((7 * n,)), pltpu.SemaphoreType.DMA((7 * n,)),
                              pltpu.SemaphoreType.DMA((n,))),
               compiler_params=pltpu.CompilerParams(collective_id=collective_id))
    def launch(send_sems, recv_sems, local_sems):
        x, y, cc = lax.axis_index("x"), lax.axis_index("y"), lax.axis_index("c")
        me, sibling = (x, y, cc), (x, y, 1 - cc)
        chips = [(1 - x, y), (x, 1 - y), (1 - x, 1 - y)]
        barrier = pltpu.get_barrier_semaphore()
        for peer in [sibling] + [(*chip, cc) for chip in chips]:
            pl.semaphore_signal(barrier, inc=1, device_id=peer, device_id_type=MESH)
        pl.semaphore_wait(barrier, 4)

        def rows(a, px, py, pc):
            return out_refs[a].at[4 * px + 2 * py + pc]

        def copy(a, k, block, to, src=None):
            return pltpu.make_async_remote_copy(
                src_ref=rows(a, *block) if src is None else src, dst_ref=rows(a, *block),
                send_sem=send_sems.at[7 * a + k], recv_sem=recv_sems.at[7 * a + k], device_id=to, device_id_type=MESH)

        mine = [pltpu.make_async_copy(x_refs[a], rows(a, *me), local_sems.at[a]) for a in range(n)]
        for cp in mine:
            cp.start()
        first = []
        for a in range(n):
            first.append(copy(a, 0, me, sibling, src=x_refs[a]))
            first += [copy(a, 1 + j, me, (*chip, cc), src=x_refs[a]) for j, chip in enumerate(chips)]
        for cp in first:
            cp.start()
        passed = []
        for j, chip in enumerate(chips):
            for a in range(n):
                copy(a, 1 + j, (*chip, cc), me).wait_recv()
                fwd = copy(a, 4 + j, (*chip, cc), sibling)
                fwd.start()
                passed.append(fwd)
        for a in range(n):
            copy(a, 0, sibling, me).wait_recv()
        for j, chip in enumerate(chips):
            for a in range(n):
                copy(a, 4 + j, (*chip, 1 - cc), me).wait_recv()
        for cp in first + passed:
            cp.wait_send()
        for cp in mine:
            cp.wait()

    launch()
    return [r[...] for r in out_refs]


HBM = pl.BlockSpec(memory_space=pltpu.HBM)
SEM = pl.BlockSpec(memory_space=pltpu.SEMAPHORE)
EFFECT = pltpu.SideEffectType.DATAFLOW_SIDE_EFFECTING


def _sibling_plan(srcs, lands, send_sems, recv_sems):
    x, y, cc = lax.axis_index("x"), lax.axis_index("y"), lax.axis_index("c")
    return [pltpu.make_async_remote_copy(
        src_ref=srcs[a].at[2 * q + 1 - cc], dst_ref=lands[a].at[q], send_sem=send_sems.at[4 * a + q],
        recv_sem=recv_sems.at[4 * a + q], device_id=(x, y, 1 - cc), device_id_type=MESH)
        for a in range(len(srcs)) for q in range(4)]


def _chips_plan(srcs, lands, send_sems, recv_sems):
    x, y, cc = lax.axis_index("x"), lax.axis_index("y"), lax.axis_index("c")
    chips = [(1 - x, y), (x, 1 - y), (1 - x, 1 - y)]
    return [pltpu.make_async_remote_copy(
        src_ref=srcs[a].at[2 * px + py], dst_ref=lands[a].at[j], send_sem=send_sems.at[3 * a + j],
        recv_sem=recv_sems.at[3 * a + j], device_id=(px, py, cc), device_id_type=MESH)
        for a in range(len(srcs)) for j, (px, py) in enumerate(chips)]


def _copies_start(name, plan, per_array, srcs, land_lead):
    n = len(srcs)
    k = per_array * n

    def body(*refs):
        src_refs, land_refs = refs[:n], refs[n: 2 * n]
        send_sems, recv_sems = refs[2 * n], refs[2 * n + 1]
        token = refs[-1]
        for cp in plan(src_refs, land_refs, send_sems, recv_sems):
            cp.start()
        token[...] = jnp.zeros_like(token)

    lands = [lax.empty((land_lead,) + a.shape[1:], a.dtype) for a in srcs]
    outs = pl.pallas_call(
        body, name=name,
        out_shape=(pltpu.SemaphoreType.DMA((k,)), pltpu.SemaphoreType.DMA((k,)),
                   *[pltpu.HBM(a.shape, a.dtype) for a in srcs], *[pltpu.HBM(a.shape, a.dtype) for a in lands],
                   SDS((8, 128), F32)),
        in_specs=[HBM] * (2 * n), out_specs=(SEM, SEM, *[HBM] * (2 * n), pl.BlockSpec(memory_space=pltpu.VMEM)),
        input_output_aliases={i: 2 + i for i in range(2 * n)},
        compiler_params=pltpu.CompilerParams(has_side_effects=EFFECT),
    )(*[pltpu.with_memory_space_constraint(a, pltpu.HBM) for a in srcs],
      *[pltpu.with_memory_space_constraint(a, pltpu.HBM) for a in lands])
    return outs[0], outs[1], list(outs[2: 2 + n]), list(outs[2 + n: 2 + 2 * n]), outs[-1]


def _copies_wait(name, plan, started, after):
    send_sems, recv_sems, srcs, lands, _ = started
    n = len(srcs)
    after = tuple(after)

    def body(*refs):
        src_refs, land_refs = refs[:n], refs[n: 2 * n]
        for cp in plan(src_refs, land_refs, refs[2 * n], refs[2 * n + 1]):
            cp.wait_send()
            cp.wait_recv()

    outs = pl.pallas_call(
        body, name=name,
        out_shape=tuple(pltpu.HBM(a.shape, a.dtype) for a in srcs + lands),
        in_specs=[HBM] * (2 * n) + [SEM, SEM] + [ANY] * len(after), out_specs=(HBM,) * (2 * n),
        input_output_aliases={i: i for i in range(2 * n)},
        compiler_params=pltpu.CompilerParams(has_side_effects=EFFECT),
    )(*srcs, *lands, send_sems, recv_sems, *after)
    return list(outs[n:])


BIG = (("w_in", 1), ("w_ffn_in", 1), ("w_proj_ssm", 0), ("w_proj_gdn", 0), ("w_out", 0), ("w_ffn_down", 0))
CONVS = (("ssm_conv_w", 1), ("gdn_conv_w", 1))


def _to_dest_major(full, axis):
    if isinstance(full, tuple):
        per = N_DEV // len(full)
        s = full[0].shape[1] // per
        return jnp.stack([full[d // per][:, (d % per) * s: (d % per + 1) * s] for d in range(N_DEV)])
    a, b = full.shape
    if axis == 0:
        return full.reshape(N_DEV, a // N_DEV, b)
    s = b // N_DEV
    return jnp.stack([full[:, d * s: (d + 1) * s] for d in range(N_DEV)])


def _from_gathered(g, axis):
    if axis == 0:
        return g.reshape(-1, g.shape[2])
    return jnp.concatenate([g[d] for d in range(N_DEV)], axis=1)


IN_RUNS = ((Z_OFF, O_Z, 1024), (GZ_OFF, O_GZ, 1024), (G1_OFF, O_G1, 1024), (G2_OFF, O_G2, 1024), (QKV_OFF, O_QKV, 3072),
           (XBC_OFF, O_XBC, 1536), (SM_OFF, O_DT, 16), (SM_OFF + LANE_A, O_A, 8), (SM_OFF + LANE_B, O_B, 8))
IN_SHARD = IN_DIM // N_DEV


def _w_in_from_blocks(g):
    rows = g.shape[1]
    parts, pos = [], 0
    for off, o0, width in IN_RUNS:
        if off > pos:
            parts.append(jnp.zeros((rows, off - pos), g.dtype))
        c = o0
        while c < o0 + width:
            d = c // IN_SHARD
            hi = min(o0 + width, (d + 1) * IN_SHARD)
            parts.append(g[d][:, c - d * IN_SHARD: hi - d * IN_SHARD])
            c = hi
        pos = off + width
    parts.append(jnp.zeros((rows, PROJ_W - pos), g.dtype))
    return jnp.concatenate(parts, axis=1)


def _w_in_to_blocks(wp):
    by_orig = sorted(IN_RUNS, key=lambda r: r[1])
    blocks = []
    for d in range(N_DEV):
        lo, hi = d * IN_SHARD, (d + 1) * IN_SHARD
        parts = []
        for off, o0, width in by_orig:
            a, b = max(lo, o0), min(hi, o0 + width)
            if a < b:
                parts.append(wp[:, off + a - o0: off + b - o0])
        blocks.append(jnp.concatenate(parts, axis=1))
    return jnp.stack(blocks)


def _pad128(v, lane0):
    return jnp.zeros((1, 128), F32).at[0, lane0: lane0 + v.shape[0]].set(v)


def _layer_consts(p):
    return dict(
        dtb=_pad128(p["ssm_dt_bias"], 0), arow=_pad128(-jnp.exp(p["ssm_a_log"]), 0),
        dxrow=jnp.repeat(p["ssm_d"], SSM_P).reshape(1, 1024), snw=p["ssm_norm_w"].reshape(1, 1024),
        gb=_pad128(p["gdn_dt_bias"], LANE_A), garow=_pad128(-jnp.exp(p["gdn_a_log"]), LANE_A),
        gnw=p["gdn_norm_w"].reshape(1, 128), zb=jnp.zeros((1, GDN_QKV), F32), scb=p["ssm_conv_b"].reshape(1, SSM_CONV))


def _expand_matrix():
    row = lax.broadcasted_iota(jnp.int32, (128, 1024), 0)
    col = lax.broadcasted_iota(jnp.int32, (128, 1024), 1)
    return (col // SSM_P == row).astype(BF16)


def _silu_mul_epi(gate, up):
    return gate, up, gate * _sigmoid(gate) * up


def _merge_epi(acc, p1, g1, g2):
    return acc, _sigmoid(g1) * p1.astype(F32) + _sigmoid(g2) * acc


def _add_epi(acc, res):
    return (acc + res,)


def _ffn_bwd_epi(acc, gate, up):
    g = gate.astype(F32)
    sg = _sigmoid(g)
    return acc * up.astype(F32) * (sg * (1.0 + g * (1.0 - sg))), acc * (g * sg)


def _merge_bwd_epi(acc, g1, g2, p1, p2):
    s1, s2 = _sigmoid(g1), _sigmoid(g2)
    dg1, dg2 = acc * p1.astype(F32) * (s1 * (1.0 - s1)), acc * p2.astype(F32) * (s2 * (1.0 - s2))
    return acc * s1, acc * s2, jnp.concatenate([dg1, dg2], axis=1)


def _layer_fwd(l, x, p, rmat):
    t = x.shape[0]
    n = f"l{l}_"
    k = _layer_consts(p)
    h = _rmsnorm_fwd(n + "norm_mix", x, p["norm_mix_w"])
    proj = _matmul(n + "in_proj", "nn", [(h, 0, p["w_in"], 0)], t, PROJ_W, 1024, 1024, 1280, 1024, (F32,))
    act_g, pre_g = _conv_fwd(n + "conv_gdn", proj, QKV_OFF, p["gdn_conv_w"], k["zb"])
    act_s, pre_s = _conv_fwd(n + "conv_ssm", proj, XBC_OFF, p["ssm_conv_w"], k["scb"])
    y_ssm, ysc, st_s = _ssd_fwd(n + "ssd_fwd", act_s, proj, k["dtb"], k["arow"], k["dxrow"], k["snw"], rmat)
    y_gdn, oraw, st_g = _gdn_fwd(n + "gdn_fwd", act_g, proj, k["gb"], k["garow"], k["gnw"])
    if "late" in p:
        y_gdn, late = p["late"](y_gdn)
        p = {**p, **late}
    p1 = _matmul(n + "proj_ssm", "nn", [(y_ssm, 0, p["w_proj_ssm"], 0)], t, 1024, 1024, 1024, 1024, 1024, (BF16,))
    p2, merged = _matmul(n + "proj_gdn_merge", "nn", [(y_gdn, 0, p["w_proj_gdn"], 0)], t, 1024, 1024, 512, 1024, 1024,
                         (BF16, BF16), epi=_merge_epi, extras=[(p1, 0), (proj, G1_OFF // 1024), (proj, G2_OFF // 1024)])
    x1 = _matmul(n + "out_proj", "nn", [(merged, 0, p["w_out"], 0)], t, 1024, 1024, 1024, 1024, 1024, (F32,),
                 epi=_add_epi, extras=[(x, 0)])
    h2 = _rmsnorm_fwd(n + "norm_ffn", x1, p["norm_ffn_w"])
    gate, up, act = _matmul(n + "ffn_in", "nn", [(h2, 0, p["w_ffn_in"], 0), (h2, 0, p["w_ffn_in"], 2)], t, FFN, 1024,
                            1024, FFN // 2, 1024, (BF16, BF16, BF16), epi=_silu_mul_epi, second=1)
    x2 = _matmul(n + "ffn_down", "nn", [(act, 0, p["w_ffn_down"], 0)], t, 1024, FFN, 1024, 1024, FFN, (F32,),
                 epi=_add_epi, extras=[(x1, 0)])
    saved = dict(x=x, h=h, proj=proj, act_g=act_g, act_s=act_s, pre_g=pre_g, pre_s=pre_s, y_ssm=y_ssm, ysc=ysc, st_s=st_s, y_gdn=y_gdn, oraw=oraw,
                 st_g=st_g, p1=p1, p2=p2, merged=merged, x1=x1, h2=h2, up=up, gate=gate, act=act, k=k, p=p)
    return x2, saved


def _layer_bwd(l, dx2, dx2b, s, p, rmat, hooks):
    t = dx2.shape[0]
    n = f"l{l}_"
    k = s["k"]
    tk_tok = 1024
    hf = FFN // 2
    g = {}
    dgate, dup = _matmul(n + "d_ffn_act", "nt", [(dx2b, 0, p["w_ffn_down"], 0)], t, FFN, 1024, 1024, hf, 1024, (BF16, BF16),
                         epi=_ffn_bwd_epi, extras=[(s["gate"], 0), (s["up"], 0)])
    g["w_ffn_down"] = _matmul(n + "dw_ffn_down", "tn", [(s["act"], 0, dx2b, 0)], FFN, 1024, t, hf, 1024, tk_tok, (BF16,))
    dh2 = _matmul(n + "d_ffn_in", "nt", [(dgate, 0, p["w_ffn_in"], 0), (dup, 0, p["w_ffn_in"], 2)], t, 1024, FFN,
                  1024, 1024, hf, (F32,))
    dwg = _matmul(n + "dw_ffn_gate", "tn", [(s["h2"], 0, dgate, 0)], 1024, FFN, t, 1024, hf, tk_tok, (BF16,))
    dwu = _matmul(n + "dw_ffn_up", "tn", [(s["h2"], 0, dup, 0)], 1024, FFN, t, 1024, hf, tk_tok, (BF16,))
    g["w_ffn_in"] = (dwg, dwu)
    dx1, dx1b, g["norm_ffn_w"] = _rmsnorm_bwd(n + "d_norm_ffn", s["x1"], p["norm_ffn_w"], dh2, dx2)
    dx1b = hooks.ffn_done(dx1b)
    dp1, dp2, dproj = _matmul(
        n + "d_out_proj", "nt", [(dx1b, 0, p["w_out"], 0)], t, 1024, 1024, 512, 1024, 1024,
        (BF16, BF16, (BF16, PROJ_W, 2048, G1_OFF // 2048)), epi=_merge_bwd_epi,
        extras=[(s["proj"], G1_OFF // 1024), (s["proj"], G2_OFF // 1024), (s["p1"], 0), (s["p2"], 0)])
    g["w_out"] = _matmul(n + "dw_out", "tn", [(s["merged"], 0, dx1b, 0)], 1024, 1024, t, 1024, 1024, tk_tok, (BF16,))
    g["w_proj_ssm"] = _matmul(n + "dw_proj_ssm", "tn", [(s["y_ssm"], 0, dp1, 0)], 1024, 1024, t, 1024, 1024, tk_tok, (BF16,))
    g["w_proj_gdn"] = _matmul(n + "dw_proj_gdn", "tn", [(s["y_gdn"], 0, dp2, 0)], 1024, 1024, t, 1024, 1024, tk_tok, (BF16,))
    dp1, dp2 = hooks.early_ready(l, g, dp1, dp2)
    dy_ssm = _matmul(n + "d_proj_ssm", "nt", [(dp1, 0, p["w_proj_ssm"], 0)], t, 1024, 1024, 1024, 1024, 1024, (F32,))
    dy_gdn = _matmul(n + "d_proj_gdn", "nt", [(dp2, 0, p["w_proj_gdn"], 0)], t, 1024, 1024, 1024, 1024, 1024, (F32,))
    dact_s, dproj, dsm_s, dsnw, dd, dal, ddtb = _ssd_bwd(n + "ssd_bwd", s["act_s"], s["proj"], k["dtb"], k["arow"],
                                                           k["dxrow"], k["snw"], rmat, s["ysc"], s["st_s"], dy_ssm, dproj)
    dact_g, dproj, dsm_g, dgnw, dgal, dgb = _gdn_bwd(n + "gdn_bwd", s["act_g"], s["proj"], k["gb"], k["garow"], k["gnw"],
                                                       s["oraw"], s["st_g"], dy_gdn, dproj)
    dsm_s = hooks.mixers_done(dsm_s)
    dproj = _place_small(n + "d_small", dsm_s, dsm_g, dproj)
    dproj, g["ssm_conv_w"], dcb = _conv_bwd(n + "d_conv_ssm", s["proj"], XBC_OFF, p["ssm_conv_w"], s["pre_s"], dact_s, dproj)
    dproj, g["gdn_conv_w"], _ = _conv_bwd(n + "d_conv_gdn", s["proj"], QKV_OFF, p["gdn_conv_w"], s["pre_g"], dact_g, dproj)
    g["ssm_conv_b"] = dcb.reshape(-1)
    g["ssm_norm_w"] = dsnw.reshape(-1)
    g["ssm_d"] = dd[0, :SSM_HEADS]
    g["ssm_a_log"] = dal[0, :SSM_HEADS]
    g["ssm_dt_bias"] = ddtb[0, :SSM_HEADS]
    g["gdn_norm_w"] = dgnw.reshape(-1)
    g["gdn_a_log"] = dgal[0, LANE_A: LANE_A + GDN_HEADS]
    g["gdn_dt_bias"] = dgb[0, LANE_A: LANE_A + GDN_HEADS]
    dh = _matmul(n + "d_in_proj", "nt", [(dproj, 0, p["w_in"], 0)], t, 1024, PROJ_W, 1024, 1024, 1280, (F32,))
    g["w_in"] = _matmul(n + "dw_in", "tn", [(s["h"], 0, dproj, 0)], 1024, PROJ_W, t, 1024, 1280, tk_tok, (BF16,))
    dx, dxb, g["norm_mix_w"] = _rmsnorm_bwd(n + "d_norm_mix", s["x"], p["norm_mix_w"], dh, dx1)
    g["norm_mix_w"] = g["norm_mix_w"].reshape(-1)
    g["norm_ffn_w"] = g["norm_ffn_w"].reshape(-1)
    return dx, dxb, g


def _local_step(x, tgt, layers, final_norm_w, reduce=False):
    rmat = _expand_matrix()
    saved, params = [], []
    for l in range(DEPTH):
        x, p = layers[l](x)
        x, s = _layer_fwd(l, x, p, rmat)
        saved.append(s)
        params.append(s["p"])
    loss, dx, dxb, dfw = _loss_head("loss_head", x, final_norm_w, tgt)
    grads = [None] * DEPTH
    hooks = _ReduceBesideBackward() if reduce else _NoReduce()
    for l in reversed(range(DEPTH)):
        dx, dxb, grads[l] = _layer_bwd(l, dx, dxb, saved[l], params[l], rmat, hooks)
        if reduce:
            dxb = hooks.layer_done(l, grads[l], dxb)
    if reduce:
        hooks.finish_start(dxb)
    return loss[0, 0], dx, grads, dfw.reshape(-1), hooks if reduce else None


SMALL = ("norm_mix_w", "ssm_conv_b", "ssm_dt_bias", "ssm_a_log", "ssm_d", "ssm_norm_w", "gdn_a_log", "gdn_dt_bias",
         "gdn_norm_w", "norm_ffn_w")
WEIGHTS = ("norm_mix_w", "w_in", "ssm_conv_w", "ssm_conv_b", "ssm_dt_bias", "ssm_a_log", "ssm_d", "ssm_norm_w", "gdn_conv_w",
           "gdn_a_log", "gdn_dt_bias", "gdn_norm_w", "w_proj_ssm", "w_proj_gdn", "w_out", "norm_ffn_w", "w_ffn_in",
           "w_ffn_down", "final_norm_w")


FIRST_USED = ("w_in", "ssm_conv_w", "gdn_conv_w")


def _gather_layer(l, w):
    conv_names = [nm for nm, _ in CONVS]
    groups = ([s for s in BIG + CONVS if s[0] in FIRST_USED], [s for s in BIG + CONVS if s[0] not in FIRST_USED])
    gathered = []
    for i, (specs, tag) in enumerate(zip(groups, ("first", "rest"))):
        shards = [w[nm][l] if nm in conv_names else w[nm][l].astype(BF16) for nm, _ in specs]
        gathered.append(_allgather_seq(f"l{l}_gather_{tag}", shards, collective_id=2 * l + i))
    small = {nm: w[nm][l] for nm in SMALL}

    def use(i, act):
        act, blocks = lax.optimization_barrier((act, gathered[i]))
        return act, {nm: _w_in_from_blocks(g) if nm == "w_in" else _from_gathered(g, axis)
                     for (nm, axis), g in zip(groups[i], blocks)}

    def full_weights(x):
        x, out = use(0, x)
        out.update(small)
        out["late"] = lambda y: use(1, y)
        return x, out

    return full_weights


EARLY_GRADS = ("w_ffn_down", "w_ffn_in", "w_out", "w_proj_ssm", "w_proj_gdn")


class _GradReduceScatter:
    def __init__(self, tag, specs, grads):
        self.tag = tag
        self.specs = specs
        self.blocks = [_w_in_to_blocks(grads[nm]) if nm == "w_in" else _to_dest_major(grads[nm], axis)
                       for nm, axis in specs]

    def _tied(self, started, acts):
        *acts, self.token = lax.optimization_barrier((*acts, started[4]))
        return acts

    def start(self, *acts):
        cc = lax.axis_index("c")
        self.keep = [lax.dynamic_index_in_dim(b.reshape((4, 2) + b.shape[1:]), cc, axis=1, keepdims=False)
                     for b in self.blocks]
        self.to_sibling = _copies_start(f"{self.tag}_to_sibling_start", _sibling_plan, 4, self.blocks, 4)
        return self._tied(self.to_sibling, acts)

    def mid(self, *acts):
        got = _copies_wait(f"{self.tag}_to_sibling_wait", _sibling_plan, self.to_sibling, (acts[0], self.token))
        chip_sums = [_sum_terms(f"{self.tag}_chip_sum_{nm}", [(k[None], 0), (g[None], 0)], BF16)
                     for (nm, _), k, g in zip(self.specs, self.keep, got)]
        self.to_chips = _copies_start(f"{self.tag}_between_chips_start", _chips_plan, 3, chip_sums, 3)
        return self._tied(self.to_chips, acts)

    def end(self, after):
        after = tuple(after) if isinstance(after, (tuple, list)) else (after,)
        landed = _copies_wait(f"{self.tag}_between_chips_wait", _chips_plan, self.to_chips, (*after, self.token))
        my_chip = 2 * lax.axis_index("x") + lax.axis_index("y")
        own = [lax.dynamic_index_in_dim(s, my_chip, axis=0, keepdims=True) for s in self.to_chips[2]]
        return {nm: _sum_terms(f"{self.tag}_total_{nm}", [(o, 0), (e, 0), (e, 1), (e, 2)], F32)
                for (nm, _), o, e in zip(self.specs, own, landed)}


class _NoReduce:
    def ffn_done(self, dx1b):
        return dx1b

    def early_ready(self, l, g, dp1, dp2):
        return dp1, dp2

    def mixers_done(self, dsm):
        return dsm


class _ReduceBesideBackward(_NoReduce):
    def __init__(self):
        self.late = None
        self.early = None
        self.shards = [dict() for _ in range(DEPTH)]

    def ffn_done(self, dx1b):
        if self.late is not None:
            (dx1b,) = self.late.mid(dx1b)
        return dx1b

    def early_ready(self, l, g, dp1, dp2):
        self.early = _GradReduceScatter(f"l{l}_early_grads", [s for s in BIG if s[0] in EARLY_GRADS], g)
        return self.early.start(dp1, dp2)

    def mixers_done(self, dsm):
        (dsm,) = self.early.mid(dsm)
        return dsm

    def layer_done(self, l, g, dxb):
        if self.late is not None:
            self.shards[l + 1].update(self.late.end(dxb))
        self.shards[l].update(self.early.end(dxb))
        self.late = _GradReduceScatter(f"l{l}_late_grads", [s for s in BIG + CONVS if s[0] not in EARLY_GRADS], g)
        (dxb,) = self.late.start(dxb)
        return dxb

    def finish_start(self, dxb):
        self.late.mid(dxb)

    def finish_end(self, after):
        return self.late.end(after)


def _allreduce_small(vecs):
    flat = jnp.concatenate(vecs)
    n = flat.shape[0]
    rows = -(-n // 128)
    rows = -(-rows // 8) * 8
    buf = jnp.pad(flat, (0, rows * 128 - n)).reshape(rows, 128)
    (allv,) = _allgather("gather_small_grads", [buf])
    tot = _sum_terms("small_grads_total", [(allv, d) for d in range(N_DEV)], F32).reshape(-1)
    out, o = [], 0
    for v in vecs:
        out.append(tot[o: o + v.shape[0]])
        o += v.shape[0]
    return out


def kernel(x, norm_mix_w, w_in, ssm_conv_w, ssm_conv_b, ssm_dt_bias, ssm_a_log, ssm_d, ssm_norm_w, gdn_conv_w, gdn_a_log, gdn_dt_bias, gdn_norm_w, w_proj_ssm, w_proj_gdn, w_out, norm_ffn_w, w_ffn_in, w_ffn_down, final_norm_w, loss_target, m_norm_mix_w, m_w_in, m_ssm_conv_w, m_ssm_conv_b, m_ssm_dt_bias, m_ssm_a_log, m_ssm_d, m_ssm_norm_w, m_gdn_conv_w, m_gdn_a_log, m_gdn_dt_bias, m_gdn_norm_w, m_w_proj_ssm, m_w_proj_gdn, m_w_out, m_norm_ffn_w, m_w_ffn_in, m_w_ffn_down, m_final_norm_w, v_norm_mix_w, v_w_in, v_ssm_conv_w, v_ssm_conv_b, v_ssm_dt_bias, v_ssm_a_log, v_ssm_d, v_ssm_norm_w, v_gdn_conv_w, v_gdn_a_log, v_gdn_dt_bias, v_gdn_norm_w, v_w_proj_ssm, v_w_proj_gdn, v_w_out, v_norm_ffn_w, v_w_ffn_in, v_w_ffn_down, v_final_norm_w):
    w = dict(norm_mix_w=norm_mix_w, w_in=w_in, ssm_conv_w=ssm_conv_w, ssm_conv_b=ssm_conv_b, ssm_dt_bias=ssm_dt_bias,
             ssm_a_log=ssm_a_log, ssm_d=ssm_d, ssm_norm_w=ssm_norm_w, gdn_conv_w=gdn_conv_w, gdn_a_log=gdn_a_log,
             gdn_dt_bias=gdn_dt_bias, gdn_norm_w=gdn_norm_w, w_proj_ssm=w_proj_ssm, w_proj_gdn=w_proj_gdn, w_out=w_out,
             norm_ffn_w=norm_ffn_w, w_ffn_in=w_ffn_in, w_ffn_down=w_ffn_down, final_norm_w=final_norm_w)
    m = dict(norm_mix_w=m_norm_mix_w, w_in=m_w_in, ssm_conv_w=m_ssm_conv_w, ssm_conv_b=m_ssm_conv_b, ssm_dt_bias=m_ssm_dt_bias,
             ssm_a_log=m_ssm_a_log, ssm_d=m_ssm_d, ssm_norm_w=m_ssm_norm_w, gdn_conv_w=m_gdn_conv_w, gdn_a_log=m_gdn_a_log,
             gdn_dt_bias=m_gdn_dt_bias, gdn_norm_w=m_gdn_norm_w, w_proj_ssm=m_w_proj_ssm, w_proj_gdn=m_w_proj_gdn,
             w_out=m_w_out, norm_ffn_w=m_norm_ffn_w, w_ffn_in=m_w_ffn_in, w_ffn_down=m_w_ffn_down,
             final_norm_w=m_final_norm_w)
    v = dict(norm_mix_w=v_norm_mix_w, w_in=v_w_in, ssm_conv_w=v_ssm_conv_w, ssm_conv_b=v_ssm_conv_b, ssm_dt_bias=v_ssm_dt_bias,
             ssm_a_log=v_ssm_a_log, ssm_d=v_ssm_d, ssm_norm_w=v_ssm_norm_w, gdn_conv_w=v_gdn_conv_w, gdn_a_log=v_gdn_a_log,
             gdn_dt_bias=v_gdn_dt_bias, gdn_norm_w=v_gdn_norm_w, w_proj_ssm=v_w_proj_ssm, w_proj_gdn=v_w_proj_gdn,
             w_out=v_w_out, norm_ffn_w=v_norm_ffn_w, w_ffn_in=v_w_ffn_in, w_ffn_down=v_w_ffn_down,
             final_norm_w=v_final_norm_w)

    layers = [_gather_layer(l, w) for l in range(DEPTH)]
    loss_part, dx, lgrads, dfw, reducer = _local_step(x[0], loss_target[0], layers, final_norm_w, reduce=True)
    loss = lax.psum(loss_part, ("x", "y", "c"))
    shard_grads = reducer.shards
    late = [nm for nm, _ in BIG + CONVS if nm not in EARLY_GRADS]
    grad = {nm: [shard_grads[l][nm] for l in range(DEPTH)] for nm in EARLY_GRADS}
    small_vecs = [lgrads[l][nm].reshape(-1) for l in range(DEPTH) for nm in SMALL] + [dfw]
    small_sum = _allreduce_small(small_vecs)
    for i, nm in enumerate(SMALL):
        grad[nm] = jnp.stack([small_sum[l * len(SMALL) + i].reshape(w[nm].shape[1:]) for l in range(DEPTH)])
    grad["final_norm_w"] = small_sum[-1]

    deltas, new_m, new_v = {}, {}, {}
    for nm in [n for n in WEIGHTS if n not in late]:
        if nm == "w_ffn_in":
            tr = lambda a: jnp.transpose(a, (0, 2, 1))
            outs = _adamw("adamw_" + nm, tr(w[nm]), [g.T for g in grad[nm]], tr(m[nm]), tr(v[nm]))
            grad[nm], deltas[nm], new_m[nm], new_v[nm] = (tr(o) for o in outs)
        else:
            grad[nm], deltas[nm], new_m[nm], new_v[nm] = _adamw("adamw_" + nm, w[nm], grad[nm], m[nm], v[nm])
    shard_grads[0].update(reducer.finish_end([deltas[nm] for nm in EARLY_GRADS]))
    for nm in late:
        gl = [shard_grads[l][nm] for l in range(DEPTH)]
        if nm == "w_in":
            fwd, back = (lambda a: jnp.transpose(a, (2, 0, 1))), (lambda a: jnp.transpose(a, (1, 2, 0)))
            g3 = jnp.stack([g.T for g in gl], axis=1)
            outs = _adamw_rows("adamw_" + nm, fwd(w[nm]), g3, fwd(m[nm]), fwd(v[nm]))
            grad[nm], deltas[nm], new_m[nm], new_v[nm] = (back(o) for o in (g3,) + tuple(outs))
        else:
            grad[nm], deltas[nm], new_m[nm], new_v[nm] = _adamw("adamw_" + nm, w[nm], gl, m[nm], v[nm])
    return (loss, dx[None], *[grad[nm] for nm in WEIGHTS], *[deltas[nm] for nm in WEIGHTS],
            *[new_m[nm] for nm in WEIGHTS], *[new_v[nm] for nm in WEIGHTS])
```
